```python
import math
import jax, jax.numpy as jnp
from jax import lax
import numpy as np

D_MODEL = 1024
BATCH = 8
SEQ = 4096
DEPTH = 1

MIX_WIDTH = 2 * D_MODEL
HEAD_DIM = 64
SSD_WIDTH = MIX_WIDTH // 2
SSD_HEADS = SSD_WIDTH // HEAD_DIM
SSD_GROUPS = 2
SSD_STATE = 128
SSD_CHUNK = 128
CONV_K = 4
CONV_DIM = SSD_WIDTH + 2 * SSD_GROUPS * SSD_STATE
ATTN_WIDTH = MIX_WIDTH - SSD_WIDTH
ATTN_HEADS = ATTN_WIDTH // HEAD_DIM
ATTN_KV_HEADS = ATTN_HEADS // 8
WINDOW = 128
ATTN_BLOCK = WINDOW
D_FF = 4 * D_MODEL
PROJ_WIDTH = SSD_WIDTH + CONV_DIM + SSD_HEADS + ATTN_WIDTH + 2 * ATTN_KV_HEADS * HEAD_DIM
ALPHA = (2.0 * DEPTH) ** 0.25
BETA = (8.0 * DEPTH) ** -0.25
LN_EPS = 1e-5
RMS_EPS = 1e-5

kernel_name = "hymba_ssd_swa_sink_alibi_deepnorm_adaln"


def layer_norm(x, g, b):
    xf = x.astype(jnp.float32)
    mu = jnp.mean(xf, axis=-1, keepdims=True)
    var = jnp.mean(jnp.square(xf - mu), axis=-1, keepdims=True)
    return ((xf - mu) * lax.rsqrt(var + LN_EPS)).astype(x.dtype) * g + b


def gated_group_rmsnorm(y, z, w):
    bsz, L, _ = y.shape
    h = (y * jax.nn.silu(z)).astype(jnp.float32).reshape(bsz, L, SSD_GROUPS, -1)
    h = h * lax.rsqrt(jnp.mean(jnp.square(h), axis=-1, keepdims=True) + RMS_EPS)
    return h.reshape(bsz, L, -1).astype(y.dtype) * w


def causal_dwconv(u, w, b):
    K = w.shape[0]
    y = lax.conv_general_dilated(u, w[:, None, :].astype(u.dtype), window_strides=(1,),
                                 padding=[(K - 1, 0)], dimension_numbers=("NWC", "WIO", "NWC"),
                                 feature_group_count=u.shape[-1])
    return y + b


def segsum(a):
    T = a.shape[-1]
    ae = jnp.broadcast_to(a[..., None], a.shape + (T,))
    ae = jnp.where(jnp.tril(jnp.ones((T, T), dtype=bool), k=-1), ae, 0.0)
    cs = jnp.cumsum(ae, axis=-2)
    return jnp.where(jnp.tril(jnp.ones((T, T), dtype=bool), k=0), cs, -jnp.inf)


def ssd_chunked(xs, dt, A, Bm, Cm):
    b, L, H, P = xs.shape
    G, N = Bm.shape[2], Bm.shape[3]
    R = H // G
    T = SSD_CHUNK
    nc = L // T
    X = (xs * dt[..., None]).reshape(b, nc, T, G, R, P)
    a = jnp.transpose((dt * A).reshape(b, nc, T, G, R), (0, 3, 4, 1, 2))
    a_cum = jnp.cumsum(a, axis=-1)
    Lm = jnp.exp(segsum(a))
    Bc = Bm.reshape(b, nc, T, G, N)
    Cc = Cm.reshape(b, nc, T, G, N)
    cb = jnp.einsum("bclgn,bcsgn->bcgls", Cc, Bc)
    y_diag = jnp.einsum("bcgls,bgrcls,bcsgrp->bclgrp", cb, Lm, X)
    decay_states = jnp.exp(a_cum[..., -1:] - a_cum)
    states = jnp.einsum("bclgn,bgrcl,bclgrp->bcgrpn", Bc, decay_states, X)
    chunk_decay = jnp.exp(a_cum[..., -1])

    def step(S, inp):
        st, dec = inp
        return S * dec[..., None, None] + st, S

    S0 = jnp.zeros((b, G, R, P, N), dtype=X.dtype)
    _, prev = lax.scan(step, S0, (jnp.moveaxis(states, 1, 0), jnp.moveaxis(chunk_decay, -1, 0)))
    prev = jnp.moveaxis(prev, 0, 1)
    y_off = jnp.einsum("bclgn,bcgrpn,bgrcl->bclgrp", Cc, prev, jnp.exp(a_cum))
    return (y_diag + y_off).reshape(b, L, H, P)


def alibi_slopes(n):
    def pow2(m):
        start = 2.0 ** (-8.0 / m)
        return [start ** (i + 1) for i in range(m)]
    if math.log2(n).is_integer():
        s = pow2(n)
    else:
        c = 2 ** math.floor(math.log2(n))
        s = pow2(c) + pow2(2 * c)[0::2][: n - c]
    return jnp.asarray(np.array(s, dtype=np.float32))


def swa_sink_alibi(q, k, v, sinks):
    b, L, H, d = q.shape
    KV = k.shape[2]
    R = H // KV
    W = ATTN_BLOCK
    nb = L // W
    qb = q.reshape(b, nb, W, KV, R, d)
    kb = k.reshape(b, nb, W, KV, d)
    vb = v.reshape(b, nb, W, KV, d)
    prev = lambda t: jnp.concatenate([jnp.zeros_like(t[:, :1]), t[:, :-1]], axis=1)
    kk = jnp.concatenate([prev(kb), kb], axis=2)
    vv = jnp.concatenate([prev(vb), vb], axis=2)
    s = jnp.einsum("bnqkrd,bnskd->bnkrqs", qb, kk).astype(jnp.float32) * (d ** -0.5)
    dist = jnp.arange(W)[:, None] + W - jnp.arange(2 * W)[None, :]
    key_pos = jnp.arange(nb)[:, None] * W + jnp.arange(2 * W)[None, :] - W
    valid = ((dist >= 0) & (dist < WINDOW))[None] & (key_pos >= 0)[:, None, :]
    slopes = alibi_slopes(H).reshape(KV, R)
    s = s - slopes[:, :, None, None] * dist.astype(jnp.float32)
    s = jnp.where(valid[None, :, None, None], s, -jnp.inf)
    sink = sinks.astype(jnp.float32).reshape(KV, R)[:, :, None]
    m = jnp.maximum(jnp.max(s, axis=-1), sink)
    p = jnp.exp(s - m[..., None])
    denom = jnp.sum(p, axis=-1) + jnp.exp(sink - m)
    p = (p / denom[..., None]).astype(v.dtype)
    o = jnp.einsum("bnkrqs,bnskd->bnqkrd", p, vv)
    return o.reshape(b, L, H * d)


def hybrid_mixer(u, w_in, conv_w, conv_b, dt_bias, a_log, d_skip, norm_w, sinks, w_out):
    bsz, L, _ = u.shape
    proj = u @ w_in
    i1 = SSD_WIDTH
    i2 = i1 + CONV_DIM
    i3 = i2 + SSD_HEADS
    i4 = i3 + ATTN_WIDTH
    i5 = i4 + ATTN_KV_HEADS * HEAD_DIM
    z, xbc, dt_raw, q, k, v = jnp.split(proj, [i1, i2, i3, i4, i5], axis=-1)
    xbc = jax.nn.silu(causal_dwconv(xbc, conv_w, conv_b))
    xs, Bm, Cm = jnp.split(xbc, [SSD_WIDTH, SSD_WIDTH + SSD_GROUPS * SSD_STATE], axis=-1)
    f32 = jnp.float32
    dt = jax.nn.softplus(dt_raw.astype(f32) + dt_bias.astype(f32))
    A = -jnp.exp(a_log.astype(f32))
    xs_h = xs.astype(f32).reshape(bsz, L, SSD_HEADS, HEAD_DIM)
    y = ssd_chunked(xs_h, dt, A,
                    Bm.astype(f32).reshape(bsz, L, SSD_GROUPS, SSD_STATE),
                    Cm.astype(f32).reshape(bsz, L, SSD_GROUPS, SSD_STATE))
    y = y + d_skip.astype(f32)[:, None] * xs_h
    y = gated_group_rmsnorm(y.reshape(bsz, L, SSD_WIDTH).astype(u.dtype), z, norm_w)
    o = swa_sink_alibi(q.reshape(bsz, L, ATTN_HEADS, HEAD_DIM),
                       k.reshape(bsz, L, ATTN_KV_HEADS, HEAD_DIM),
                       v.reshape(bsz, L, ATTN_KV_HEADS, HEAD_DIM), sinks)
    return jnp.concatenate([y, o.astype(y.dtype)], axis=-1) @ w_out


def _fwd_setup_inputs(seed: int = 0) -> dict:
    key = jax.random.key(seed)
    ks = jax.random.split(key, 24)
    nrm = lambda k, shape, s: jax.random.normal(k, shape, jnp.float32) * s
    Dp = DEPTH
    dt0 = jnp.exp(jax.random.uniform(ks[10], (Dp, SSD_HEADS), jnp.float32,
                                     math.log(1e-3), math.log(1e-1)))
    dt_bias = dt0 + jnp.log(-jnp.expm1(-dt0))
    a_log = jnp.log(jax.random.uniform(ks[11], (Dp, SSD_HEADS), jnp.float32, 1.0, 16.0))
    return {
        "x": nrm(ks[0], (BATCH, SEQ, D_MODEL), 1.0),
        "c": nrm(ks[1], (BATCH, D_MODEL), 1.0),
        "ln_in_g": 1.0 + nrm(ks[2], (D_MODEL,), 0.02),
        "ln_in_b": nrm(ks[3], (D_MODEL,), 0.02),
        "ada_w": nrm(ks[4], (Dp, D_MODEL, 6 * D_MODEL), 0.1 * D_MODEL ** -0.5),
        "ada_b": nrm(ks[5], (Dp, 6 * D_MODEL), 0.02),
        "w_in": nrm(ks[6], (Dp, D_MODEL, PROJ_WIDTH), D_MODEL ** -0.5),
        "conv_w": nrm(ks[7], (Dp, CONV_K, CONV_DIM), CONV_K ** -0.5),
        "conv_b": nrm(ks[8], (Dp, CONV_DIM), 0.02),
        "dt_bias": dt_bias,
        "a_log": a_log,
        "d_skip": 1.0 + nrm(ks[12], (Dp, SSD_HEADS), 0.1),
        "ssd_norm_w": 1.0 + nrm(ks[13], (Dp, SSD_WIDTH), 0.02),
        "attn_sinks": nrm(ks[14], (Dp, ATTN_HEADS), 0.5),
        "w_out": nrm(ks[15], (Dp, MIX_WIDTH, D_MODEL), BETA * MIX_WIDTH ** -0.5),
        "ln1_g": 1.0 + nrm(ks[16], (Dp, D_MODEL), 0.02),
        "ln1_b": nrm(ks[17], (Dp, D_MODEL), 0.02),
        "w_ff1": nrm(ks[18], (Dp, D_MODEL, D_FF), D_MODEL ** -0.5),
        "b_ff1": nrm(ks[19], (Dp, D_FF), 0.02),
        "w_ff2": nrm(ks[20], (Dp, D_FF, D_MODEL), BETA * D_FF ** -0.5),
        "b_ff2": nrm(ks[21], (Dp, D_MODEL), 0.02),
        "ln2_g": 1.0 + nrm(ks[22], (Dp, D_MODEL), 0.02),
        "ln2_b": nrm(ks[23], (Dp, D_MODEL), 0.02),
    }


def _fwd_reference(x, c, ln_in_g, ln_in_b, ada_w, ada_b, w_in, conv_w, conv_b, dt_bias, a_log, d_skip,
              ssd_norm_w, attn_sinks, w_out, ln1_g, ln1_b, w_ff1, b_ff1, w_ff2, b_ff2, ln2_g, ln2_b):
    h = layer_norm(x, ln_in_g, ln_in_b)
    cs = jax.nn.silu(c)
    for l in range(DEPTH):
        mod = cs @ ada_w[l] + ada_b[l]
        sh1, sc1, g1, sh2, sc2, g2 = jnp.split(mod[:, None, :], 6, axis=-1)
        u = h * (1.0 + sc1) + sh1
        mix = hybrid_mixer(u, w_in[l], conv_w[l], conv_b[l], dt_bias[l], a_log[l], d_skip[l],
                           ssd_norm_w[l], attn_sinks[l], w_out[l])
        h = layer_norm(ALPHA * h + (1.0 + g1) * mix, ln1_g[l], ln1_b[l])
        u = h * (1.0 + sc2) + sh2
        f = jnp.square(jax.nn.relu(u @ w_ff1[l] + b_ff1[l])) @ w_ff2[l] + b_ff2[l]
        h = layer_norm(ALPHA * h + (1.0 + g2) * f, ln2_g[l], ln2_b[l])
    return h


import jax as _jax
import jax.numpy as _jnp

TWIN_FORMAT = 'train_step'
FWD_PARAMS = ['x', 'c', 'ln_in_g', 'ln_in_b', 'ada_w', 'ada_b', 'w_in', 'conv_w', 'conv_b', 'dt_bias', 'a_log', 'd_skip', 'ssd_norm_w', 'attn_sinks', 'w_out', 'ln1_g', 'ln1_b', 'w_ff1', 'b_ff1', 'w_ff2', 'b_ff2', 'ln2_g', 'ln2_b']
TWIN_WEIGHTS = ['ln_in_g', 'ln_in_b', 'ada_w', 'ada_b', 'w_in', 'conv_w', 'conv_b', 'dt_bias', 'a_log', 'd_skip', 'ssd_norm_w', 'attn_sinks', 'w_out', 'ln1_g', 'ln1_b', 'w_ff1', 'b_ff1', 'w_ff2', 'b_ff2', 'ln2_g', 'ln2_b']
TWIN_DIFF_INPUT = 'x'
TWIN_INPUTS = ['x', 'c', 'ln_in_g', 'ln_in_b', 'ada_w', 'ada_b', 'w_in', 'conv_w', 'conv_b', 'dt_bias', 'a_log', 'd_skip', 'ssd_norm_w', 'attn_sinks', 'w_out', 'ln1_g', 'ln1_b', 'w_ff1', 'b_ff1', 'w_ff2', 'b_ff2', 'ln2_g', 'ln2_b', 'loss_target', 'm_ln_in_g', 'm_ln_in_b', 'm_ada_w', 'm_ada_b', 'm_w_in', 'm_conv_w', 'm_conv_b', 'm_dt_bias', 'm_a_log', 'm_d_skip', 'm_ssd_norm_w', 'm_attn_sinks', 'm_w_out', 'm_ln1_g', 'm_ln1_b', 'm_w_ff1', 'm_b_ff1', 'm_w_ff2', 'm_b_ff2', 'm_ln2_g', 'm_ln2_b', 'v_ln_in_g', 'v_ln_in_b', 'v_ada_w', 'v_ada_b', 'v_w_in', 'v_conv_w', 'v_conv_b', 'v_dt_bias', 'v_a_log', 'v_d_skip', 'v_ssd_norm_w', 'v_attn_sinks', 'v_w_out', 'v_ln1_g', 'v_ln1_b', 'v_w_ff1', 'v_b_ff1', 'v_w_ff2', 'v_b_ff2', 'v_ln2_g', 'v_ln2_b']
TWIN_OUTPUTS = ['loss', 'grad_x', 'grad_ln_in_g', 'grad_ln_in_b', 'grad_ada_w', 'grad_ada_b', 'grad_w_in', 'grad_conv_w', 'grad_conv_b', 'grad_dt_bias', 'grad_a_log', 'grad_d_skip', 'grad_ssd_norm_w', 'grad_attn_sinks', 'grad_w_out', 'grad_ln1_g', 'grad_ln1_b', 'grad_w_ff1', 'grad_b_ff1', 'grad_w_ff2', 'grad_b_ff2', 'grad_ln2_g', 'grad_ln2_b', 'delta_ln_in_g', 'delta_ln_in_b', 'delta_ada_w', 'delta_ada_b', 'delta_w_in', 'delta_conv_w', 'delta_conv_b', 'delta_dt_bias', 'delta_a_log', 'delta_d_skip', 'delta_ssd_norm_w', 'delta_attn_sinks', 'delta_w_out', 'delta_ln1_g', 'delta_ln1_b', 'delta_w_ff1', 'delta_b_ff1', 'delta_w_ff2', 'delta_b_ff2', 'delta_ln2_g', 'delta_ln2_b', 'new_m_ln_in_g', 'new_m_ln_in_b', 'new_m_ada_w', 'new_m_ada_b', 'new_m_w_in', 'new_m_conv_w', 'new_m_conv_b', 'new_m_dt_bias', 'new_m_a_log', 'new_m_d_skip', 'new_m_ssd_norm_w', 'new_m_attn_sinks', 'new_m_w_out', 'new_m_ln1_g', 'new_m_ln1_b', 'new_m_w_ff1', 'new_m_b_ff1', 'new_m_w_ff2', 'new_m_b_ff2', 'new_m_ln2_g', 'new_m_ln2_b', 'new_v_ln_in_g', 'new_v_ln_in_b', 'new_v_ada_w', 'new_v_ada_b', 'new_v_w_in', 'new_v_conv_w', 'new_v_conv_b', 'new_v_dt_bias', 'new_v_a_log', 'new_v_d_skip', 'new_v_ssd_norm_w', 'new_v_attn_sinks', 'new_v_w_out', 'new_v_ln1_g', 'new_v_ln1_b', 'new_v_w_ff1', 'new_v_b_ff1', 'new_v_w_ff2', 'new_v_b_ff2', 'new_v_ln2_g', 'new_v_ln2_b']
TWIN_LEAF_KINDS = {'loss': 'loss', 'grad_x': 'grad_x', 'grad_ln_in_g': 'grad_w', 'grad_ln_in_b': 'grad_w', 'grad_ada_w': 'grad_w', 'grad_ada_b': 'grad_w', 'grad_w_in': 'grad_w', 'grad_conv_w': 'grad_w', 'grad_conv_b': 'grad_w', 'grad_dt_bias': 'grad_w', 'grad_a_log': 'grad_w', 'grad_d_skip': 'grad_w', 'grad_ssd_norm_w': 'grad_w', 'grad_attn_sinks': 'grad_w', 'grad_w_out': 'grad_w', 'grad_ln1_g': 'grad_w', 'grad_ln1_b': 'grad_w', 'grad_w_ff1': 'grad_w', 'grad_b_ff1': 'grad_w', 'grad_w_ff2': 'grad_w', 'grad_b_ff2': 'grad_w', 'grad_ln2_g': 'grad_w', 'grad_ln2_b': 'grad_w', 'delta_ln_in_g': 'delta_w', 'delta_ln_in_b': 'delta_w', 'delta_ada_w': 'delta_w', 'delta_ada_b': 'delta_w', 'delta_w_in': 'delta_w', 'delta_conv_w': 'delta_w', 'delta_conv_b': 'delta_w', 'delta_dt_bias': 'delta_w', 'delta_a_log': 'delta_w', 'delta_d_skip': 'delta_w', 'delta_ssd_norm_w': 'delta_w', 'delta_attn_sinks': 'delta_w', 'delta_w_out': 'delta_w', 'delta_ln1_g': 'delta_w', 'delta_ln1_b': 'delta_w', 'delta_w_ff1': 'delta_w', 'delta_b_ff1': 'delta_w', 'delta_w_ff2': 'delta_w', 'delta_b_ff2': 'delta_w', 'delta_ln2_g': 'delta_w', 'delta_ln2_b': 'delta_w', 'new_m_ln_in_g': 'new_m', 'new_m_ln_in_b': 'new_m', 'new_m_ada_w': 'new_m', 'new_m_ada_b': 'new_m', 'new_m_w_in': 'new_m', 'new_m_conv_w': 'new_m', 'new_m_conv_b': 'new_m', 'new_m_dt_bias': 'new_m', 'new_m_a_log': 'new_m', 'new_m_d_skip': 'new_m', 'new_m_ssd_norm_w': 'new_m', 'new_m_attn_sinks': 'new_m', 'new_m_w_out': 'new_m', 'new_m_ln1_g': 'new_m', 'new_m_ln1_b': 'new_m', 'new_m_w_ff1': 'new_m', 'new_m_b_ff1': 'new_m', 'new_m_w_ff2': 'new_m', 'new_m_b_ff2': 'new_m', 'new_m_ln2_g': 'new_m', 'new_m_ln2_b': 'new_m', 'new_v_ln_in_g': 'new_v', 'new_v_ln_in_b': 'new_v', 'new_v_ada_w': 'new_v', 'new_v_ada_b': 'new_v', 'new_v_w_in': 'new_v', 'new_v_conv_w': 'new_v', 'new_v_conv_b': 'new_v', 'new_v_dt_bias': 'new_v', 'new_v_a_log': 'new_v', 'new_v_d_skip': 'new_v', 'new_v_ssd_norm_w': 'new_v', 'new_v_attn_sinks': 'new_v', 'new_v_w_out': 'new_v', 'new_v_ln1_g': 'new_v', 'new_v_ln1_b': 'new_v', 'new_v_w_ff1': 'new_v', 'new_v_b_ff1': 'new_v', 'new_v_w_ff2': 'new_v', 'new_v_b_ff2': 'new_v', 'new_v_ln2_g': 'new_v', 'new_v_ln2_b': 'new_v'}


def _forward(args):
    return _fwd_reference(*[args[k] for k in FWD_PARAMS])


def _output_shape():
    def fwd():
        inp = _fwd_setup_inputs(0)
        return _fwd_reference(*[inp[k] for k in FWD_PARAMS])
    out = _jax.eval_shape(fwd)
    return out.shape, out.dtype

N_MICROBATCH = 1
ADAM_LR = 0.001
ADAM_B1 = 0.9
ADAM_B2 = 0.999
ADAM_EPS = 1e-08
ADAM_WD = 0.01
ADAM_STEP = 10
PER_EXAMPLE_BATCH_AXIS = {'x': 0, 'c': 0, 'loss_target': 0}
SHARED_INPUTS = []
_WEIGHT_DTYPES = {'ln_in_g': _jnp.float32, 'ln_in_b': _jnp.float32, 'ada_w': _jnp.float32, 'ada_b': _jnp.float32, 'w_in': _jnp.float32, 'conv_w': _jnp.float32, 'conv_b': _jnp.float32, 'dt_bias': _jnp.float32, 'a_log': _jnp.float32, 'd_skip': _jnp.float32, 'ssd_norm_w': _jnp.float32, 'attn_sinks': _jnp.float32, 'w_out': _jnp.float32, 'ln1_g': _jnp.float32, 'ln1_b': _jnp.float32, 'w_ff1': _jnp.float32, 'b_ff1': _jnp.float32, 'w_ff2': _jnp.float32, 'b_ff2': _jnp.float32, 'ln2_g': _jnp.float32, 'ln2_b': _jnp.float32}
MOMENT_SCALE = {'ln_in_g': 7.344489e-01, 'ln_in_b': 5.639978e-01, 'ada_w': 7.489306e-02, 'ada_b': 1.977908e-01, 'w_in': 4.963178e-02, 'conv_w': 5.356994e-02, 'conv_b': 9.236942e-02, 'dt_bias': 1.482016e-01, 'a_log': 1.985387e-01, 'd_skip': 4.357929e-01, 'ssd_norm_w': 6.040688e-02, 'attn_sinks': 1.705775e-02, 'w_out': 1.194141e-01, 'ln1_g': 8.491492e-01, 'ln1_b': 5.869549e-01, 'w_ff1': 5.446135e-02, 'b_ff1': 1.131698e-01, 'w_ff2': 2.488285e-01, 'b_ff2': 4.669343e-01, 'ln2_g': 3.212774e+01, 'ln2_b': 7.248928e+00}


def _to_microbatches(a, axis):
    t = _jnp.moveaxis(a, axis, 0)
    t = t.reshape((N_MICROBATCH, t.shape[0] // N_MICROBATCH) + t.shape[1:])
    return _jnp.moveaxis(t, 1, axis + 1)


def setup_inputs(seed: int = 0) -> dict:
    inp = _fwd_setup_inputs(seed)
    key = _jax.random.fold_in(_jax.random.key(seed), 7919)
    shape, _ = _output_shape()
    out = dict(inp)
    out["loss_target"] = _jax.random.normal(_jax.random.fold_in(key, 0), shape, _jnp.float32)
    for i, name in enumerate(TWIN_WEIGHTS):
        w = inp[name].astype(_jnp.float32)
        if MOMENT_SCALE is None:
            s = _jnp.sqrt(_jnp.mean(_jnp.square(w)) + 1e-30)
        else:
            s = MOMENT_SCALE[name]
        km, kv = _jax.random.split(_jax.random.fold_in(key, i + 1))
        out[name] = w
        out["m_" + name] = s * _jax.random.normal(km, w.shape, _jnp.float32)
        out["v_" + name] = (s * s) * _jax.random.uniform(kv, w.shape, _jnp.float32, 0.5, 1.5)
    if N_MICROBATCH > 1:
        for name, axis in PER_EXAMPLE_BATCH_AXIS.items():
            out[name] = _to_microbatches(out[name], axis)
    return {'x': out['x'], 'c': out['c'], 'ln_in_g': out['ln_in_g'], 'ln_in_b': out['ln_in_b'], 'ada_w': out['ada_w'], 'ada_b': out['ada_b'], 'w_in': out['w_in'], 'conv_w': out['conv_w'], 'conv_b': out['conv_b'], 'dt_bias': out['dt_bias'], 'a_log': out['a_log'], 'd_skip': out['d_skip'], 'ssd_norm_w': out['ssd_norm_w'], 'attn_sinks': out['attn_sinks'], 'w_out': out['w_out'], 'ln1_g': out['ln1_g'], 'ln1_b': out['ln1_b'], 'w_ff1': out['w_ff1'], 'b_ff1': out['b_ff1'], 'w_ff2': out['w_ff2'], 'b_ff2': out['b_ff2'], 'ln2_g': out['ln2_g'], 'ln2_b': out['ln2_b'], 'loss_target': out['loss_target'], 'm_ln_in_g': out['m_ln_in_g'], 'm_ln_in_b': out['m_ln_in_b'], 'm_ada_w': out['m_ada_w'], 'm_ada_b': out['m_ada_b'], 'm_w_in': out['m_w_in'], 'm_conv_w': out['m_conv_w'], 'm_conv_b': out['m_conv_b'], 'm_dt_bias': out['m_dt_bias'], 'm_a_log': out['m_a_log'], 'm_d_skip': out['m_d_skip'], 'm_ssd_norm_w': out['m_ssd_norm_w'], 'm_attn_sinks': out['m_attn_sinks'], 'm_w_out': out['m_w_out'], 'm_ln1_g': out['m_ln1_g'], 'm_ln1_b': out['m_ln1_b'], 'm_w_ff1': out['m_w_ff1'], 'm_b_ff1': out['m_b_ff1'], 'm_w_ff2': out['m_w_ff2'], 'm_b_ff2': out['m_b_ff2'], 'm_ln2_g': out['m_ln2_g'], 'm_ln2_b': out['m_ln2_b'], 'v_ln_in_g': out['v_ln_in_g'], 'v_ln_in_b': out['v_ln_in_b'], 'v_ada_w': out['v_ada_w'], 'v_ada_b': out['v_ada_b'], 'v_w_in': out['v_w_in'], 'v_conv_w': out['v_conv_w'], 'v_conv_b': out['v_conv_b'], 'v_dt_bias': out['v_dt_bias'], 'v_a_log': out['v_a_log'], 'v_d_skip': out['v_d_skip'], 'v_ssd_norm_w': out['v_ssd_norm_w'], 'v_attn_sinks': out['v_attn_sinks'], 'v_w_out': out['v_w_out'], 'v_ln1_g': out['v_ln1_g'], 'v_ln1_b': out['v_ln1_b'], 'v_w_ff1': out['v_w_ff1'], 'v_b_ff1': out['v_b_ff1'], 'v_w_ff2': out['v_w_ff2'], 'v_b_ff2': out['v_b_ff2'], 'v_ln2_g': out['v_ln2_g'], 'v_ln2_b': out['v_ln2_b']}


def _loss(weights, diff, rest, loss_target):
    with _jax.named_scope("forward"):
        args = {**rest, TWIN_DIFF_INPUT: diff, **{k: w.astype(_WEIGHT_DTYPES[k]) for k, w in weights.items()}}
        y = _forward(args)
    with _jax.named_scope("loss_head"):
        err = _jnp.square(y.astype(_jnp.float32) - loss_target)
        return 0.5 * _jnp.sum(_jnp.mean(err, axis=-1)) if err.ndim else 0.5 * err


def _adamw(w, g, m, v):
    m = ADAM_B1 * m + (1.0 - ADAM_B1) * g
    v = ADAM_B2 * v + (1.0 - ADAM_B2) * _jnp.square(g)
    m_hat = m / (1.0 - ADAM_B1 ** ADAM_STEP)
    v_hat = v / (1.0 - ADAM_B2 ** ADAM_STEP)
    delta = -ADAM_LR * (m_hat / (_jnp.sqrt(v_hat) + ADAM_EPS) + ADAM_WD * w)
    return delta, m, v


def reference(x, c, ln_in_g, ln_in_b, ada_w, ada_b, w_in, conv_w, conv_b, dt_bias, a_log, d_skip, ssd_norm_w, attn_sinks, w_out, ln1_g, ln1_b, w_ff1, b_ff1, w_ff2, b_ff2, ln2_g, ln2_b, loss_target, m_ln_in_g, m_ln_in_b, m_ada_w, m_ada_b, m_w_in, m_conv_w, m_conv_b, m_dt_bias, m_a_log, m_d_skip, m_ssd_norm_w, m_attn_sinks, m_w_out, m_ln1_g, m_ln1_b, m_w_ff1, m_b_ff1, m_w_ff2, m_b_ff2, m_ln2_g, m_ln2_b, v_ln_in_g, v_ln_in_b, v_ada_w, v_ada_b, v_w_in, v_conv_w, v_conv_b, v_dt_bias, v_a_log, v_d_skip, v_ssd_norm_w, v_attn_sinks, v_w_out, v_ln1_g, v_ln1_b, v_w_ff1, v_b_ff1, v_w_ff2, v_b_ff2, v_ln2_g, v_ln2_b):
    given = dict(x=x, c=c, ln_in_g=ln_in_g, ln_in_b=ln_in_b, ada_w=ada_w, ada_b=ada_b, w_in=w_in, conv_w=conv_w, conv_b=conv_b, dt_bias=dt_bias, a_log=a_log, d_skip=d_skip, ssd_norm_w=ssd_norm_w, attn_sinks=attn_sinks, w_out=w_out, ln1_g=ln1_g, ln1_b=ln1_b, w_ff1=w_ff1, b_ff1=b_ff1, w_ff2=w_ff2, b_ff2=b_ff2, ln2_g=ln2_g, ln2_b=ln2_b, loss_target=loss_target, m_ln_in_g=m_ln_in_g, m_ln_in_b=m_ln_in_b, m_ada_w=m_ada_w, m_ada_b=m_ada_b, m_w_in=m_w_in, m_conv_w=m_conv_w, m_conv_b=m_conv_b, m_dt_bias=m_dt_bias, m_a_log=m_a_log, m_d_skip=m_d_skip, m_ssd_norm_w=m_ssd_norm_w, m_attn_sinks=m_attn_sinks, m_w_out=m_w_out, m_ln1_g=m_ln1_g, m_ln1_b=m_ln1_b, m_w_ff1=m_w_ff1, m_b_ff1=m_b_ff1, m_w_ff2=m_w_ff2, m_b_ff2=m_b_ff2, m_ln2_g=m_ln2_g, m_ln2_b=m_ln2_b, v_ln_in_g=v_ln_in_g, v_ln_in_b=v_ln_in_b, v_ada_w=v_ada_w, v_ada_b=v_ada_b, v_w_in=v_w_in, v_conv_w=v_conv_w, v_conv_b=v_conv_b, v_dt_bias=v_dt_bias, v_a_log=v_a_log, v_d_skip=v_d_skip, v_ssd_norm_w=v_ssd_norm_w, v_attn_sinks=v_attn_sinks, v_w_out=v_w_out, v_ln1_g=v_ln1_g, v_ln1_b=v_ln1_b, v_w_ff1=v_w_ff1, v_b_ff1=v_b_ff1, v_w_ff2=v_w_ff2, v_b_ff2=v_b_ff2, v_ln2_g=v_ln2_g, v_ln2_b=v_ln2_b)
    weights = {n: given[n] for n in TWIN_WEIGHTS}
    shared = {n: given[n] for n in SHARED_INPUTS}
    per_example = {n: given[n] for n in ['x', 'c']}
    grad_fn = _jax.value_and_grad(_loss, argnums=(0, 1))

    def one_microbatch(ex, loss_target):
        ex = dict(ex)
        diff = ex.pop(TWIN_DIFF_INPUT)
        return grad_fn(weights, diff, {**shared, **ex}, loss_target)

    if N_MICROBATCH == 1:
        loss, (grad_w, grad_x) = one_microbatch(per_example, given["loss_target"])
    else:
        def body(carry, xs):
            loss_sum, grad_sum = carry
            l_k, (gw_k, gx_k) = one_microbatch(xs[0], xs[1])
            with _jax.named_scope("update"):
                return (loss_sum + l_k, _jax.tree.map(_jnp.add, grad_sum, gw_k)), gx_k

        init = (_jnp.zeros((), _jnp.float32), _jax.tree.map(_jnp.zeros_like, weights))
        (loss, grad_w), grad_x = _jax.lax.scan(body, init, (per_example, given["loss_target"]))
    with _jax.named_scope("update"):
        delta_w, new_m, new_v = {}, {}, {}
        for n in TWIN_WEIGHTS:
            delta_w[n], new_m[n], new_v[n] = _adamw(weights[n], grad_w[n], given["m_" + n], given["v_" + n])
    return (loss, grad_x, *[grad_w[n] for n in TWIN_WEIGHTS], *[delta_w[n] for n in TWIN_WEIGHTS],
            *[new_m[n] for n in TWIN_WEIGHTS], *[new_v[n] for n in TWIN_WEIGHTS])
```

```python
import functools
import math

import numpy as np
import jax
import jax.numpy as jnp
from jax import lax
from jax.experimental import pallas as pl
from jax.experimental.pallas import tpu as pltpu

f32 = jnp.float32
bf16 = jnp.bfloat16

D_MODEL = 1024
SSD_WIDTH = 1024
SSD_HEADS = 16
HEAD_DIM = 64
SSD_STATE = 128
SSD_GROUPS = 2
CHUNK = 128
CONV_K = 4
CONV_DIM = 1536
ATTN_HEADS = 16
D_FF = 4096
PROJ_WIDTH = 3856
ALPHA = 2.0 ** 0.25
LN_EPS = 1e-5
RMS_EPS = 1e-5
ATTN_SCALE = HEAD_DIM ** -0.5
NEG = -1e30

ADAM_LR = 0.001
ADAM_B1 = 0.9
ADAM_B2 = 0.999
ADAM_EPS = 1e-08
ADAM_WD = 0.01
ADAM_STEP = 10

C_Z, C_XBC, C_Q, C_KV, C_DT, C_END = 0, 1024, 2560, 3584, 3840, 3968
R_IN, R_OUT, R_FF1, R_FF2, R_END = 0, 1024, 1536, 2560, 3584
R_HALF = R_END // 2
N_CHIPS = 4
N_DEV = 8
VMEM_LIMIT = 56 * 1024 * 1024
MESH = pl.DeviceIdType.MESH

ALIBI_SLOPES = tuple(2.0 ** (-8.0 / ATTN_HEADS * (i + 1)) for i in range(ATTN_HEADS))


def _cparams(*sem):
    return pltpu.CompilerParams(dimension_semantics=sem, vmem_limit_bytes=VMEM_LIMIT)


def _sigmoid(x):
    return 1.0 / (1.0 + jnp.exp(-x))


def _softplus(x):
    return jnp.maximum(x, 0.0) + jnp.log1p(jnp.exp(-jnp.abs(x)))


def _ln_stats(x):
    mu = jnp.mean(x, axis=-1, keepdims=True)
    xc = x - mu
    var = jnp.mean(xc * xc, axis=-1, keepdims=True)
    rstd = lax.rsqrt(var + LN_EPS)
    return xc * rstd, rstd


def _ln_bwd(dy, xhat, rstd, g):
    dxh = dy * g
    m1 = jnp.mean(dxh, axis=-1, keepdims=True)
    m2 = jnp.mean(dxh * xhat, axis=-1, keepdims=True)
    return rstd * (dxh - m1 - xhat * m2)


def _dot(a, b):
    return jnp.dot(a, b, preferred_element_type=f32)


def _dot_nt(a, b):
    return lax.dot_general(a, b, (((1,), (1,)), ((), ())), preferred_element_type=f32)


def _dot_tn(a, b):
    return lax.dot_general(a, b, (((0,), (0,)), ((), ())), preferred_element_type=f32)


def _dot_exact(a, b):
    return jnp.dot(a, b, preferred_element_type=f32, precision=lax.Precision.HIGHEST)


def _full(shape):
    nd = len(shape)
    return pl.BlockSpec(shape, lambda *_: (0,) * nd)


def _resident(shape):
    nd = len(shape)
    return pl.BlockSpec(shape, lambda *_: (0,) * nd, pipeline_mode=pl.Buffered(1))


def _rows(tm, n):
    return pl.BlockSpec((tm, n), lambda i: (i, 0))


def _inproj_fwd(x, mod, ln_g, ln_b, w_in, conv_w, conv_b):
    L = x.shape[0]
    tm = 256

    def body(x_ref, mod_ref, g_ref, b_ref, w_ref, cw_ref, cb_ref,
             u1_ref, z_ref, xr_ref, xc_ref, q_ref, kv_ref, dt_ref, halo, buf):
        @pl.when(pl.program_id(0) == 0)
        def _():
            halo[...] = jnp.zeros_like(halo)

        xhat, _ = _ln_stats(x_ref[...])
        h0 = xhat * g_ref[...] + b_ref[...]
        u1 = (h0 * (1.0 + mod_ref[1:2, :]) + mod_ref[0:1, :]).astype(bf16)
        u1_ref[...] = u1
        z_ref[...] = _dot(u1, w_ref[:, C_Z:C_XBC])
        xr = _dot(u1, w_ref[:, C_XBC:C_Q])
        xr_ref[...] = xr
        q_ref[...] = _dot(u1, w_ref[:, C_Q:C_KV]).astype(bf16)
        kv_ref[...] = _dot(u1, w_ref[:, C_KV:C_DT]).astype(bf16)
        dt_ref[...] = _dot(u1, w_ref[:, C_DT:C_END])
        buf[0:8, :] = halo[...]
        buf[8:8 + tm, :] = xr
        pre = cb_ref[...] + cw_ref[0:1, :] * buf[5:5 + tm, :]
        for k in range(1, CONV_K):
            pre = pre + cw_ref[k:k + 1, :] * buf[5 + k:5 + k + tm, :]
        xc_ref[...] = pre * _sigmoid(pre)
        halo[...] = xr[tm - 8:tm, :]

    return pl.pallas_call(
        body, name="inproj_fwd", grid=(L // tm,),
        in_specs=[_rows(tm, D_MODEL), _full((8, D_MODEL)), _full((1, D_MODEL)), _full((1, D_MODEL)),
                  _resident((D_MODEL, C_END)), _full((CONV_K, CONV_DIM)), _full((1, CONV_DIM))],
        out_specs=[_rows(tm, D_MODEL), _rows(tm, D_MODEL), _rows(tm, CONV_DIM), _rows(tm, CONV_DIM),
                   _rows(tm, D_MODEL), _rows(tm, 256), _rows(tm, 128)],
        out_shape=[jax.ShapeDtypeStruct((L, D_MODEL), bf16), jax.ShapeDtypeStruct((L, D_MODEL), f32),
                   jax.ShapeDtypeStruct((L, CONV_DIM), f32), jax.ShapeDtypeStruct((L, CONV_DIM), f32),
                   jax.ShapeDtypeStruct((L, D_MODEL), bf16), jax.ShapeDtypeStruct((L, 256), bf16),
                   jax.ShapeDtypeStruct((L, 128), f32)],
        scratch_shapes=[pltpu.VMEM((8, CONV_DIM), f32), pltpu.VMEM((tm + 8, CONV_DIM), f32)],
        compiler_params=_cparams("arbitrary"),
    )(x, mod, ln_g, ln_b, w_in, conv_w, conv_b)


def _head_expand():
    e = np.zeros((128, SSD_WIDTH), np.float32)
    for h in range(SSD_HEADS):
        e[h, h * HEAD_DIM:(h + 1) * HEAD_DIM] = 1.0
    return jnp.asarray(e)


def _ssd_chunk_common(dt_raw, dtb, a_row, e_mat):
    T = CHUNK
    lane = lax.broadcasted_iota(jnp.int32, (T, 128), 1)
    dt = jnp.where(lane < SSD_HEADS, _softplus(dt_raw + dtb), 0.0)
    a = dt * a_row
    r = lax.broadcasted_iota(jnp.int32, (T, T), 0)
    c = lax.broadcasted_iota(jnp.int32, (T, T), 1)
    tril = (c <= r).astype(f32)
    cum = _dot_exact(tril, a)
    dtx = _dot_exact(dt, e_mat)
    cumx = _dot_exact(cum, e_mat)
    return dt, a, cum, dtx, cumx, r, c


def _ssd_fwd(xc, dt_raw, dt_bias, a_log, d_skip_x, e_mat):
    L = xc.shape[0]
    nc = L // CHUNK
    T = CHUNK

    def body(xc_ref, dt_ref, dtb_ref, al_ref, dsk_ref, e_ref, y_ref, prev_ref, st):
        @pl.when(pl.program_id(0) == 0)
        def _():
            st[...] = jnp.zeros_like(st)

        a_row = -jnp.exp(al_ref[...])
        lane1 = lax.broadcasted_iota(jnp.int32, (1, 128), 1)
        a_row = jnp.where(lane1 < SSD_HEADS, a_row, 0.0)
        dt, a, cum, dtx, cumx, r, c = _ssd_chunk_common(dt_ref[...], dtb_ref[...], a_row, e_ref[...])
        cum_t = cum.T
        ex = jnp.exp(cumx)
        last = cumx[T - 1:T, :]
        wx = jnp.exp(last - cumx)
        cdx = jnp.exp(last)
        xs = xc_ref[:, 0:SSD_WIDTH]
        X = xs * dtx
        Xb = X.astype(bf16)
        Xd = (X * wx).astype(bf16)
        prev = st[...]
        prev_ref[0] = prev
        prevb = prev.astype(bf16)
        tri = c <= r
        lane = lax.broadcasted_iota(jnp.int32, (T, 128), 1)
        y_blocks = []
        new_states = []
        for g in range(SSD_GROUPS):
            Bg = xc_ref[:, 1024 + 128 * g:1152 + 128 * g].astype(bf16)
            Cg = xc_ref[:, 1280 + 128 * g:1408 + 128 * g].astype(bf16)
            G = _dot_nt(Cg, Bg)
            yoff = _dot(Cg, prevb[:, 512 * g:512 * (g + 1)])
            new_states.append(_dot_tn(Bg, Xd[:, 512 * g:512 * (g + 1)]))
            for j in range(4):
                blk = 4 * g + j
                Xblk = Xb[:, 128 * blk:128 * (blk + 1)]
                ys = []
                for half in range(2):
                    h = 2 * blk + half
                    seg = jnp.minimum(cum[:, h:h + 1] - cum_t[h:h + 1, :], 0.0)
                    M = jnp.where(tri, G * jnp.exp(seg), 0.0).astype(bf16)
                    ys.append(_dot(M, Xblk))
                yd = jnp.where(lane < HEAD_DIM, ys[0], ys[1])
                sl = slice(128 * blk, 128 * (blk + 1))
                y_blocks.append(yd + ex[:, sl] * yoff[:, 128 * j:128 * (j + 1)] + dsk_ref[:, sl] * xs[:, sl])
        y_ref[...] = jnp.concatenate(y_blocks, axis=1)
        st[...] = prev * cdx + jnp.concatenate(new_states, axis=1)

    return pl.pallas_call(
        body, name="ssd_fwd", grid=(nc,),
        in_specs=[_rows(T, CONV_DIM), _rows(T, 128), _full((1, 128)), _full((1, 128)), _full((1, SSD_WIDTH)),
                  _full((128, SSD_WIDTH))],
        out_specs=[_rows(T, SSD_WIDTH), pl.BlockSpec((1, SSD_STATE, SSD_WIDTH), lambda i: (i, 0, 0))],
        out_shape=[jax.ShapeDtypeStruct((L, SSD_WIDTH), f32), jax.ShapeDtypeStruct((nc, SSD_STATE, SSD_WIDTH), f32)],
        scratch_shapes=[pltpu.VMEM((SSD_STATE, SSD_WIDTH), f32)],
        compiler_params=_cparams("arbitrary"),
    )(xc, dt_raw, dt_bias, a_log, d_skip_x, e_mat)


def _kv_halves(kv_prev, kv_cur, first):
    kv = jnp.concatenate([jnp.where(first, 0.0, kv_prev.astype(f32)), kv_cur.astype(f32)], axis=0)
    lane = lax.broadcasted_iota(jnp.int32, (2 * CHUNK, 128), 1)
    lo = lane < HEAD_DIM
    out = []
    for g in range(2):
        per_half = []
        for half in range(2):
            both = []
            for t in (kv[:, 0:128], kv[:, 128:256]):
                src = t if g == half else pltpu.roll(t, HEAD_DIM, 1)
                both.append(jnp.where(lo if half == 0 else ~lo, src, 0.0).astype(bf16))
            per_half.append(tuple(both))
        out.append(per_half)
    return out


def _attn_masks(first):
    r = lax.broadcasted_iota(jnp.int32, (CHUNK, 2 * CHUNK), 0)
    c = lax.broadcasted_iota(jnp.int32, (CHUNK, 2 * CHUNK), 1)
    dist = r + CHUNK - c
    valid = (dist >= 0) & (dist < CHUNK) & ((c >= CHUNK) | jnp.logical_not(first))
    return dist.astype(f32), valid


def _attn_fwd(q, kv, sinks):
    L = q.shape[0]
    nb = L // CHUNK
    T = CHUNK

    def body(sink_ref, q_ref, kvp_ref, kvc_ref, o_ref, lse_ref):
        first = pl.program_id(0) == 0
        ext = _kv_halves(kvp_ref[...], kvc_ref[...], first)
        dist, valid = _attn_masks(first)
        lane = lax.broadcasted_iota(jnp.int32, (T, 128), 1)
        lse = jnp.zeros((T, 128), f32)
        o_blocks = []
        for blk in range(8):
            qb = q_ref[:, 128 * blk:128 * (blk + 1)]
            acc = None
            for half in range(2):
                h = 2 * blk + half
                k_ext, v_ext = ext[h // 8][half]
                s = _dot_nt(qb, k_ext) * ATTN_SCALE - ALIBI_SLOPES[h] * dist
                s = jnp.where(valid, s, NEG)
                sink = sink_ref[h]
                m = jnp.maximum(jnp.max(s, axis=-1, keepdims=True), sink)
                p = jnp.exp(s - m)
                den = jnp.sum(p, axis=-1, keepdims=True) + jnp.exp(sink - m)
                pn = (p / den).astype(bf16)
                oh = _dot(pn, v_ext)
                acc = oh if acc is None else acc + oh
                lse = jnp.where(lane == h, m + jnp.log(den), lse)
            o_blocks.append(acc.astype(bf16))
        o_ref[...] = jnp.concatenate(o_blocks, axis=1)
        lse_ref[...] = lse

    return pl.pallas_call(
        body, name="attn_fwd", grid=(nb,),
        in_specs=[pl.BlockSpec(memory_space=pltpu.SMEM), _rows(T, D_MODEL),
                  pl.BlockSpec((T, 256), lambda i: (jnp.maximum(i - 1, 0), 0)), _rows(T, 256)],
        out_specs=[_rows(T, D_MODEL), _rows(T, 128)],
        out_shape=[jax.ShapeDtypeStruct((L, D_MODEL), bf16), jax.ShapeDtypeStruct((L, 128), f32)],
        compiler_params=_cparams("arbitrary"),
    )(sinks, q, kv, kv)


def _gated_norm(y, z, w):
    sz = _sigmoid(z)
    hg = y * (z * sz)
    ns, rss = [], []
    for g in range(SSD_GROUPS):
        hs = hg[:, 512 * g:512 * (g + 1)]
        rs = lax.rsqrt(jnp.mean(hs * hs, axis=-1, keepdims=True) + RMS_EPS)
        ns.append(hs * rs)
        rss.append(rs)
    n = jnp.concatenate(ns, axis=1)
    return n * w, n, rss, sz


def _outproj_fwd(y, z, o, x, mod, ln_g, ln_b, norm_w, w_out):
    L = x.shape[0]
    tm = 256

    def body(y_ref, z_ref, o_ref, x_ref, mod_ref, g_ref, b_ref, nw_ref, w_ref, yn_ref, mix_ref, r1_ref):
        yn, _, _, _ = _gated_norm(y_ref[...], z_ref[...], nw_ref[...])
        ynb = yn.astype(bf16)
        yn_ref[...] = ynb
        mix = _dot(ynb, w_ref[0:SSD_WIDTH, :]) + _dot(o_ref[...], w_ref[SSD_WIDTH:2 * SSD_WIDTH, :])
        mix_ref[...] = mix
        xhat, _ = _ln_stats(x_ref[...])
        h0 = xhat * g_ref[...] + b_ref[...]
        r1_ref[...] = ALPHA * h0 + (1.0 + mod_ref[2:3, :]) * mix

    v = _full((1, D_MODEL))
    return pl.pallas_call(
        body, name="outproj_fwd", grid=(L // tm,),
        in_specs=[_rows(tm, D_MODEL), _rows(tm, D_MODEL), _rows(tm, D_MODEL), _rows(tm, D_MODEL),
                  _full((8, D_MODEL)), v, v, v, _resident((2 * SSD_WIDTH, D_MODEL))],
        out_specs=[_rows(tm, D_MODEL)] * 3,
        out_shape=[jax.ShapeDtypeStruct((L, D_MODEL), bf16), jax.ShapeDtypeStruct((L, D_MODEL), f32),
                   jax.ShapeDtypeStruct((L, D_MODEL), f32)],
        compiler_params=_cparams("parallel"),
    )(y, z, o, x, mod, ln_g, ln_b, norm_w, w_out)


A_LN2G, A_LN2B, A_G2, A_B2, A_SC2, A_SH2, A_LN1G, A_LN1B, A_LOSS = range(9)


def _mlp_fwd_bwd(r1, target, mod, ln1_g, ln1_b, ln2_g, ln2_b, w1, b1, w2, b2):
    L = r1.shape[0]
    tm = 256
    nj = D_FF // 1024

    def body(r1_ref, t_ref, mod_ref, g1_ref, bb1_ref, g2_ref, bb2_ref, w1_ref, b1_ref, w2_ref, b2_ref,
             dr1_ref, u2_ref, s_ref, da_ref, df_ref, acc_ref, db1_ref, hr):
        @pl.when(pl.program_id(0) == 0)
        def _():
            acc_ref[...] = jnp.zeros_like(acc_ref)
            db1_ref[...] = jnp.zeros_like(db1_ref)

        sc2, sh2, gate2 = mod_ref[4:5, :], mod_ref[3:4, :], mod_ref[5:6, :]
        xhat1, rstd1 = _ln_stats(r1_ref[...])
        h1 = xhat1 * g1_ref[...] + bb1_ref[...]
        u2f = h1 * (1.0 + sc2) + sh2
        u2 = u2f.astype(bf16)
        u2_ref[...] = u2
        f = jnp.zeros((tm, D_MODEL), f32) + b2_ref[...]
        for j in range(nj):
            cs = slice(1024 * j, 1024 * (j + 1))
            a = _dot(u2, w1_ref[:, cs]) + b1_ref[:, cs]
            hrj = jnp.maximum(a, 0.0)
            hr[:, cs] = hrj
            sj = (hrj * hrj).astype(bf16)
            s_ref[:, cs] = sj
            f = f + _dot(sj, w2_ref[cs, :])
        r2 = ALPHA * h1 + (1.0 + gate2) * f
        xhat2, rstd2 = _ln_stats(r2)
        h2 = xhat2 * g2_ref[...] + bb2_ref[...]
        diff = h2 - t_ref[...]
        dh2 = diff * (1.0 / D_MODEL)

        def add(row, val):
            acc_ref[row:row + 1, :] += jnp.sum(val, axis=0, keepdims=True)

        add(A_LOSS, diff * diff * (0.5 / D_MODEL))
        add(A_LN2G, dh2 * xhat2)
        add(A_LN2B, dh2)
        dr2 = _ln_bwd(dh2, xhat2, rstd2, g2_ref[...])
        add(A_G2, dr2 * f)
        df = dr2 * (1.0 + gate2)
        add(A_B2, df)
        dfb = df.astype(bf16)
        df_ref[...] = dfb
        du2 = jnp.zeros((tm, D_MODEL), f32)
        for j in range(nj):
            cs = slice(1024 * j, 1024 * (j + 1))
            ds = _dot_nt(dfb, w2_ref[cs, :])
            daj = ds * (2.0 * hr[:, cs])
            db1_ref[:, cs] += jnp.sum(daj, axis=0, keepdims=True)
            dajb = daj.astype(bf16)
            da_ref[:, cs] = dajb
            du2 = du2 + _dot_nt(dajb, w1_ref[:, cs])
        add(A_SC2, du2 * h1)
        add(A_SH2, du2)
        dh1 = ALPHA * dr2 + du2 * (1.0 + sc2)
        add(A_LN1G, dh1 * xhat1)
        add(A_LN1B, dh1)
        dr1_ref[...] = _ln_bwd(dh1, xhat1, rstd1, g1_ref[...])

    v = _full((1, D_MODEL))
    return pl.pallas_call(
        body, name="mlp_fwd_bwd", grid=(L // tm,),
        in_specs=[_rows(tm, D_MODEL), _rows(tm, D_MODEL), _full((8, D_MODEL)), v, v, v, v,
                  _resident((D_MODEL, D_FF)), _full((1, D_FF)), _resident((D_FF, D_MODEL)), v],
        out_specs=[_rows(tm, D_MODEL), _rows(tm, D_MODEL), _rows(tm, D_FF), _rows(tm, D_FF), _rows(tm, D_MODEL),
                   _full((16, D_MODEL)), _full((1, D_FF))],
        out_shape=[jax.ShapeDtypeStruct((L, D_MODEL), f32), jax.ShapeDtypeStruct((L, D_MODEL), bf16),
                   jax.ShapeDtypeStruct((L, D_FF), bf16), jax.ShapeDtypeStruct((L, D_FF), bf16),
                   jax.ShapeDtypeStruct((L, D_MODEL), bf16), jax.ShapeDtypeStruct((16, D_MODEL), f32),
                   jax.ShapeDtypeStruct((1, D_FF), f32)],
        scratch_shapes=[pltpu.VMEM((tm, D_FF), f32)],
        compiler_params=_cparams("arbitrary"),
    )(r1, target, mod, ln1_g, ln1_b, ln2_g, ln2_b, w1, b1, w2, b2)


def _wgrad(a, b, name):
    L, M = a.shape
    N = b.shape[1]
    tm = min(M, 512)
    tn = next(t for t in (1024, 768, 512, 256, 128) if N % t == 0)

    def body(a_ref, b_ref, o_ref):
        o_ref[...] = _dot_tn(a_ref[...], b_ref[...])

    return pl.pallas_call(
        body, name=name, grid=(M // tm, N // tn),
        in_specs=[pl.BlockSpec((L, tm), lambda i, j: (0, i)), pl.BlockSpec((L, tn), lambda i, j: (0, j))],
        out_specs=pl.BlockSpec((tm, tn), lambda i, j: (i, j)),
        out_shape=jax.ShapeDtypeStruct((M, N), f32),
        compiler_params=_cparams("parallel", "parallel"),
    )(a, b)


def _outproj_bwd(dr1, mix, y, z, mod, norm_w, w_out):
    L = dr1.shape[0]
    tm = 256

    def body(dr1_ref, mix_ref, y_ref, z_ref, mod_ref, nw_ref, w_ref, dy_ref, dz_ref, do_ref, dmix_ref, acc_ref):
        @pl.when(pl.program_id(0) == 0)
        def _():
            acc_ref[...] = jnp.zeros_like(acc_ref)

        dr1 = dr1_ref[...]
        acc_ref[0:1, :] += jnp.sum(dr1 * mix_ref[...], axis=0, keepdims=True)
        dmix = (dr1 * (1.0 + mod_ref[2:3, :])).astype(bf16)
        dmix_ref[...] = dmix
        dyn = _dot_nt(dmix, w_ref[0:SSD_WIDTH, :])
        do_ref[...] = _dot_nt(dmix, w_ref[SSD_WIDTH:2 * SSD_WIDTH, :]).astype(bf16)
        yv, zv = y_ref[...], z_ref[...]
        _, n, rss, sz = _gated_norm(yv, zv, nw_ref[...])
        acc_ref[1:2, :] += jnp.sum(dyn * n, axis=0, keepdims=True)
        dn = dyn * nw_ref[...]
        parts = []
        for g in range(SSD_GROUPS):
            sl = slice(512 * g, 512 * (g + 1))
            dng, ng = dn[:, sl], n[:, sl]
            parts.append(rss[g] * (dng - ng * jnp.mean(dng * ng, axis=-1, keepdims=True)))
        dhg = jnp.concatenate(parts, axis=1)
        dy_ref[...] = dhg * (zv * sz)
        dz_ref[...] = (dhg * yv * (sz * (1.0 + zv * (1.0 - sz)))).astype(bf16)

    return pl.pallas_call(
        body, name="outproj_bwd", grid=(L // tm,),
        in_specs=[_rows(tm, D_MODEL)] * 4 + [_full((8, D_MODEL)), _full((1, D_MODEL)),
                                               _resident((2 * SSD_WIDTH, D_MODEL))],
        out_specs=[_rows(tm, D_MODEL)] * 4 + [_full((8, D_MODEL))],
        out_shape=[jax.ShapeDtypeStruct((L, D_MODEL), f32)] + [jax.ShapeDtypeStruct((L, D_MODEL), bf16)] * 3
        + [jax.ShapeDtypeStruct((8, D_MODEL), f32)],
        compiler_params=_cparams("arbitrary"),
    )(dr1, mix, y, z, mod, norm_w, w_out)


def _attn_bwd(q, kv, do, lse, sinks):
    L = q.shape[0]
    nb = L // CHUNK
    T = CHUNK

    def body(sink_ref, q_ref, kvp_ref, kvc_ref, do_ref, lse_ref, dq_ref, dkv_ref, dsink_ref, carry):
        n = pl.program_id(0)

        @pl.when(n == 0)
        def _():
            carry[...] = jnp.zeros_like(carry)
            dsink_ref[...] = jnp.zeros_like(dsink_ref)

        @pl.when(n < nb)
        def _():
            first = n == 0
            ext = _kv_halves(kvp_ref[...], kvc_ref[...], first)
            dist, valid = _attn_masks(first)
            lane1 = lax.broadcasted_iota(jnp.int32, (1, 128), 1)
            lse = lse_ref[...]
            acck = [[None, None], [None, None]]
            accv = [[None, None], [None, None]]
            dsink = jnp.zeros((1, 128), f32)
            dq_blocks = []
            for blk in range(8):
                qb = q_ref[:, 128 * blk:128 * (blk + 1)]
                dob = do_ref[:, 128 * blk:128 * (blk + 1)]
                dq_acc = None
                for half in range(2):
                    h = 2 * blk + half
                    g = h // 8
                    k_ext, v_ext = ext[g][half]
                    s = _dot_nt(qb, k_ext) * ATTN_SCALE - ALIBI_SLOPES[h] * dist
                    lse_h = lse[:, h:h + 1]
                    p = jnp.where(valid, jnp.exp(s - lse_h), 0.0)
                    dp = _dot_nt(dob, v_ext)
                    delta = jnp.sum(p * dp, axis=-1, keepdims=True)
                    ds = (p * (dp - delta) * ATTN_SCALE).astype(bf16)
                    psink = jnp.exp(sink_ref[h] - lse_h)
                    dsink = dsink - jnp.where(lane1 == h, jnp.sum(psink * delta, axis=0, keepdims=True), 0.0)
                    dqh = _dot(ds, k_ext)
                    dq_acc = dqh if dq_acc is None else dq_acc + dqh
                    dkh = _dot_tn(ds, qb)
                    dvh = _dot_tn(p.astype(bf16), dob)
                    acck[g][half] = dkh if acck[g][half] is None else acck[g][half] + dkh
                    accv[g][half] = dvh if accv[g][half] is None else accv[g][half] + dvh
                dq_blocks.append(dq_acc.astype(bf16))
            dq_ref[...] = jnp.concatenate(dq_blocks, axis=1)
            dsink_ref[...] += dsink
            lo = lax.broadcasted_iota(jnp.int32, (2 * T, 128), 1) < HEAD_DIM

            def fold(acc):
                return jnp.where(lo, acc[0][0] + pltpu.roll(acc[0][1], HEAD_DIM, 1),
                                 acc[1][1] + pltpu.roll(acc[1][0], HEAD_DIM, 1))

            dkv = jnp.concatenate([fold(acck), fold(accv)], axis=1)
            dkv_ref[...] = (carry[...] + dkv[0:T, :]).astype(bf16)
            carry[...] = dkv[T:2 * T, :]

        @pl.when(n == nb)
        def _():
            dkv_ref[...] = carry[...].astype(bf16)

    cur = lambda i: (jnp.minimum(i, nb - 1), 0)
    return pl.pallas_call(
        body, name="attn_bwd", grid=(nb + 1,),
        in_specs=[pl.BlockSpec(memory_space=pltpu.SMEM), pl.BlockSpec((T, D_MODEL), cur),
                  pl.BlockSpec((T, 256), lambda i: (jnp.maximum(jnp.minimum(i, nb - 1) - 1, 0), 0)),
                  pl.BlockSpec((T, 256), cur), pl.BlockSpec((T, D_MODEL), cur), pl.BlockSpec((T, 128), cur)],
        out_specs=[pl.BlockSpec((T, D_MODEL), cur), pl.BlockSpec((T, 256), lambda i: (jnp.maximum(i - 1, 0), 0)),
                   _full((1, 128))],
        out_shape=[jax.ShapeDtypeStruct((L, D_MODEL), bf16), jax.ShapeDtypeStruct((L, 256), bf16),
                   jax.ShapeDtypeStruct((1, 128), f32)],
        scratch_shapes=[pltpu.VMEM((T, 256), f32)],
        compiler_params=_cparams("arbitrary"),
    )(sinks, q, kv, kv, do, lse)


def _ssd_bwd(xc, dt_raw, dy, prev_all, dt_bias, a_log, d_skip_x, e_mat):
    L = xc.shape[0]
    nc = L // CHUNK
    T = CHUNK

    def body(xc_ref, dt_ref, dy_ref, prev_ref, dtb_ref, al_ref, dsk_ref, e_ref,
             dxc_ref, ddt_ref, acc_ref, dd_ref, dst, dxs_s):
        @pl.when(pl.program_id(0) == 0)
        def _():
            dst[...] = jnp.zeros_like(dst)
            acc_ref[...] = jnp.zeros_like(acc_ref)
            dd_ref[...] = jnp.zeros_like(dd_ref)

        lane1 = lax.broadcasted_iota(jnp.int32, (1, 128), 1)
        a_row = jnp.where(lane1 < SSD_HEADS, -jnp.exp(al_ref[...]), 0.0)
        e_mat_v = e_ref[...]
        dt, a, cum, dtx, cumx, r, c = _ssd_chunk_common(dt_ref[...], dtb_ref[...], a_row, e_mat_v)
        cum_t = cum.T
        ex = jnp.exp(cumx)
        last = cumx[T - 1:T, :]
        wx = jnp.exp(last - cumx)
        cdx = jnp.exp(last)
        xs = xc_ref[:, 0:SSD_WIDTH]
        X = xs * dtx
        Xb = X.astype(bf16)
        Xdb = (X * wx).astype(bf16)
        dyv = dy_ref[...]
        prev = prev_ref[0]
        prevb = prev.astype(bf16)
        dnew = dst[...]
        dnewb = dnew.astype(bf16)
        tri = c <= r
        lane = lax.broadcasted_iota(jnp.int32, (T, 128), 1)
        sub = lax.broadcasted_iota(jnp.int32, (128, T), 0)
        lo = lane < HEAD_DIM

        def red(vals, g):
            return lax.dot_general(vals, e_mat_v[:, 512 * g:512 * (g + 1)], (((1,), (1,)), ((), ())),
                                   preferred_element_type=f32, precision=lax.Precision.HIGHEST)

        de = jnp.zeros((T, 128), f32)
        dw = jnp.zeros((T, 128), f32)
        ddt_x = jnp.zeros((T, 128), f32)
        dcum_col = jnp.zeros((T, 128), f32)
        dcum_row = jnp.zeros((128, T), f32)
        dprev_parts, dBs, dCs = [], [], []
        for g in range(SSD_GROUPS):
            s5 = slice(512 * g, 512 * (g + 1))
            Bg = xc_ref[:, 1024 + 128 * g:1152 + 128 * g].astype(bf16)
            Cg = xc_ref[:, 1280 + 128 * g:1408 + 128 * g].astype(bf16)
            G = _dot_nt(Cg, Bg)
            Z = _dot(Cg, prevb[:, s5])
            dyg = dyv[:, s5]
            dZb = (dyg * ex[:, s5]).astype(bf16)
            dXd = _dot(Bg, dnewb[:, s5])
            dC = _dot_nt(dZb, prevb[:, s5])
            dB = _dot_nt(Xdb[:, s5], dnewb[:, s5])
            dprev_parts.append(_dot_tn(Cg, dZb) + dnew[:, s5] * cdx[:, s5])
            de = de + red(dyg * Z, g)
            dw = dw + red(dXd * X[:, s5], g)
            dXg = dXd * wx[:, s5]
            dG = jnp.zeros((T, T), f32)
            for j in range(4):
                blk = 4 * g + j
                sl = slice(128 * blk, 128 * (blk + 1))
                Xblk = Xb[:, sl]
                dyblk = dyv[:, sl]
                dyblk_b = dyblk.astype(bf16)
                dxh = []
                for half in range(2):
                    h = 2 * blk + half
                    seg = jnp.minimum(cum[:, h:h + 1] - cum_t[h:h + 1, :], 0.0)
                    Lm = jnp.where(tri, jnp.exp(seg), 0.0)
                    M = G * Lm
                    dyh = jnp.where(lo if half == 0 else ~lo, dyblk, 0.0).astype(bf16)
                    dM = _dot_nt(dyh, Xblk)
                    dG = dG + dM * Lm
                    Q = dM * M
                    dcum_col = dcum_col + jnp.where(lane == h, jnp.sum(Q, axis=1, keepdims=True), 0.0)
                    dcum_row = dcum_row + jnp.where(sub == h, jnp.sum(Q, axis=0, keepdims=True), 0.0)
                    dxh.append(_dot_tn(M.astype(bf16), dyblk_b))
                dXblk = dXg[:, 128 * j:128 * (j + 1)] + jnp.where(lo, dxh[0], dxh[1])
                xsb = xs[:, sl]
                dxs_s[:, sl] = dXblk * dtx[:, sl] + dsk_ref[:, sl] * dyblk
                ddt_x = ddt_x + lax.dot_general(dXblk * xsb, e_mat_v[:, sl], (((1,), (1,)), ((), ())),
                                                preferred_element_type=f32, precision=lax.Precision.HIGHEST)
                dd_ref[:, sl] += jnp.sum(dyblk * xsb, axis=0, keepdims=True)
            dGb = dG.astype(bf16)
            dCs.append(dC + _dot(dGb, Bg))
            dBs.append(dB + _dot_tn(dGb, Cg))
        e16 = jnp.exp(cum)
        cum_last = cum[T - 1:T, :]
        w16 = jnp.exp(cum_last - cum)
        dcd = jnp.sum(dnew * prev, axis=0, keepdims=True)
        dcd16 = red(dcd[:, 0:512], 0) + red(dcd[:, 512:1024], 1)
        dww = dw * w16
        extra = jnp.sum(dww, axis=0, keepdims=True) + dcd16 * jnp.exp(cum_last)
        rowi = lax.broadcasted_iota(jnp.int32, (T, 128), 0)
        dcum = dcum_col - dcum_row.T + de * e16 - dww + jnp.where(rowi == T - 1, extra, 0.0)
        triu = (c >= r).astype(f32)
        da = _dot_exact(triu, dcum)
        ddt = ddt_x + da * a_row
        acc_ref[0:1, :] += jnp.sum(da * dt, axis=0, keepdims=True)
        ddt_raw = jnp.where(lane < SSD_HEADS, ddt * _sigmoid(dt_ref[...] + dtb_ref[...]), 0.0)
        ddt_ref[...] = ddt_raw
        acc_ref[1:2, :] += jnp.sum(ddt_raw, axis=0, keepdims=True)
        dxc_ref[:, 0:SSD_WIDTH] = dxs_s[...]
        dxc_ref[:, 1024:1280] = jnp.concatenate(dBs, axis=1)
        dxc_ref[:, 1280:1536] = jnp.concatenate(dCs, axis=1)
        dst[...] = jnp.concatenate(dprev_parts, axis=1)

    rev = lambda i: (nc - 1 - i, 0)
    return pl.pallas_call(
        body, name="ssd_bwd", grid=(nc,),
        in_specs=[pl.BlockSpec((T, CONV_DIM), rev), pl.BlockSpec((T, 128), rev), pl.BlockSpec((T, SSD_WIDTH), rev),
                  pl.BlockSpec((1, SSD_STATE, SSD_WIDTH), lambda i: (nc - 1 - i, 0, 0)),
                  _full((1, 128)), _full((1, 128)), _full((1, SSD_WIDTH)), _full((128, SSD_WIDTH))],
        out_specs=[pl.BlockSpec((T, CONV_DIM), rev), pl.BlockSpec((T, 128), rev), _full((8, 128)),
                   _full((1, SSD_WIDTH))],
        out_shape=[jax.ShapeDtypeStruct((L, CONV_DIM), f32), jax.ShapeDtypeStruct((L, 128), f32),
                   jax.ShapeDtypeStruct((8, 128), f32), jax.ShapeDtypeStruct((1, SSD_WIDTH), f32)],
        scratch_shapes=[pltpu.VMEM((SSD_STATE, SSD_WIDTH), f32), pltpu.VMEM((T, SSD_WIDTH), f32)],
        compiler_params=_cparams("arbitrary"),
    )(xc, dt_raw, dy, prev_all, dt_bias, a_log, d_skip_x, e_mat)


def _conv_bwd(dxc, xr, conv_w, conv_b):
    L = dxc.shape[0]
    tm = 256
    nt = L // tm

    def body(dxc_ref, xr_ref, xh_ref, cw_ref, cb_ref, dxr_ref, acc_ref, carry, buf, buf2):
        i = pl.program_id(0)

        @pl.when(i == 0)
        def _():
            carry[...] = jnp.zeros_like(carry)
            acc_ref[...] = jnp.zeros_like(acc_ref)

        buf[0:8, :] = jnp.where(i == nt - 1, 0.0, xh_ref[...])
        buf[8:8 + tm, :] = xr_ref[...]
        pre = cb_ref[...] + cw_ref[0:1, :] * buf[5:5 + tm, :]
        for k in range(1, CONV_K):
            pre = pre + cw_ref[k:k + 1, :] * buf[5 + k:5 + k + tm, :]
        sg = _sigmoid(pre)
        dpre = dxc_ref[...] * (sg * (1.0 + pre * (1.0 - sg)))
        acc_ref[4:5, :] += jnp.sum(dpre, axis=0, keepdims=True)
        for k in range(CONV_K):
            acc_ref[k:k + 1, :] += jnp.sum(dpre * buf[5 + k:5 + k + tm, :], axis=0, keepdims=True)
        buf2[0:tm, :] = dpre
        buf2[tm:tm + 8, :] = carry[...]
        du = cw_ref[0:1, :] * buf2[3:3 + tm, :]
        for k in range(1, CONV_K):
            du = du + cw_ref[k:k + 1, :] * buf2[3 - k:3 - k + tm, :]
        dxr_ref[...] = du.astype(bf16)
        carry[...] = dpre[0:8, :]

    rev = lambda i: (nt - 1 - i, 0)
    return pl.pallas_call(
        body, name="conv_bwd", grid=(nt,),
        in_specs=[pl.BlockSpec((tm, CONV_DIM), rev), pl.BlockSpec((tm, CONV_DIM), rev),
                  pl.BlockSpec((8, CONV_DIM), lambda i: (jnp.maximum((nt - 1 - i) * (tm // 8) - 1, 0), 0)),
                  _full((CONV_K, CONV_DIM)), _full((1, CONV_DIM))],
        out_specs=[pl.BlockSpec((tm, CONV_DIM), rev), _full((8, CONV_DIM))],
        out_shape=[jax.ShapeDtypeStruct((L, CONV_DIM), bf16), jax.ShapeDtypeStruct((8, CONV_DIM), f32)],
        scratch_shapes=[pltpu.VMEM((8, CONV_DIM), f32), pltpu.VMEM((tm + 8, CONV_DIM), f32),
                        pltpu.VMEM((tm + 8, CONV_DIM), f32)],
        compiler_params=_cparams("arbitrary"),
    )(dxc, xr, xr, conv_w, conv_b)


def _inproj_bwd(dz, dxr, dq, dkv, ddt, dr1, x, mod, ln_g, ln_b, w_in):
    L = x.shape[0]
    tm = 256

    def body(dz_ref, dxr_ref, dq_ref, dkv_ref, ddt_ref, dr1_ref, x_ref, mod_ref, g_ref, b_ref, w_ref, dx_ref, acc_ref):
        @pl.when(pl.program_id(0) == 0)
        def _():
            acc_ref[...] = jnp.zeros_like(acc_ref)

        du1 = (_dot_nt(dz_ref[...], w_ref[:, C_Z:C_XBC]) + _dot_nt(dxr_ref[...], w_ref[:, C_XBC:C_Q])
               + _dot_nt(dq_ref[...], w_ref[:, C_Q:C_KV]) + _dot_nt(dkv_ref[...], w_ref[:, C_KV:C_DT])
               + _dot_nt(ddt_ref[...].astype(bf16), w_ref[:, C_DT:C_END]))
        xhat, rstd = _ln_stats(x_ref[...])
        h0 = xhat * g_ref[...] + b_ref[...]
        acc_ref[0:1, :] += jnp.sum(du1 * h0, axis=0, keepdims=True)
        acc_ref[1:2, :] += jnp.sum(du1, axis=0, keepdims=True)
        dh0 = du1 * (1.0 + mod_ref[1:2, :]) + ALPHA * dr1_ref[...]
        acc_ref[2:3, :] += jnp.sum(dh0 * xhat, axis=0, keepdims=True)
        acc_ref[3:4, :] += jnp.sum(dh0, axis=0, keepdims=True)
        dx_ref[...] = _ln_bwd(dh0, xhat, rstd, g_ref[...])

    v = _full((1, D_MODEL))
    return pl.pallas_call(
        body, name="inproj_bwd", grid=(L // tm,),
        in_specs=[_rows(tm, D_MODEL), _rows(tm, CONV_DIM), _rows(tm, D_MODEL), _rows(tm, 256), _rows(tm, 128),
                  _rows(tm, D_MODEL), _rows(tm, D_MODEL), _full((8, D_MODEL)), v, v, _resident((D_MODEL, C_END))],
        out_specs=[_rows(tm, D_MODEL), _full((8, D_MODEL))],
        out_shape=[jax.ShapeDtypeStruct((L, D_MODEL), f32), jax.ShapeDtypeStruct((8, D_MODEL), f32)],
        compiler_params=_cparams("arbitrary"),
    )(dz, dxr, dq, dkv, ddt, dr1, x, mod, ln_g, ln_b, w_in)


def _adamw_math(w, g, m, v):
    m = ADAM_B1 * m + (1.0 - ADAM_B1) * g
    v = ADAM_B2 * v + (1.0 - ADAM_B2) * (g * g)
    m_hat = m / (1.0 - ADAM_B1 ** ADAM_STEP)
    v_hat = v / (1.0 - ADAM_B2 ** ADAM_STEP)
    delta = -ADAM_LR * (m_hat / (jnp.sqrt(v_hat) + ADAM_EPS) + ADAM_WD * w)
    return delta, m, v


def _adamw(w, g, m, v, name):
    R, C = w.shape
    tr = R if R <= 256 else 256

    def body(w_ref, g_ref, m_ref, v_ref, d_ref, m2_ref, v2_ref):
        d_ref[...], m2_ref[...], v2_ref[...] = _adamw_math(w_ref[...], g_ref[...], m_ref[...], v_ref[...])

    spec = pl.BlockSpec((tr, C), lambda i: (i, 0))
    return pl.pallas_call(
        body, name=name, grid=(R // tr,), in_specs=[spec] * 4, out_specs=[spec] * 3,
        out_shape=[jax.ShapeDtypeStruct((R, C), f32)] * 3, compiler_params=_cparams("parallel"),
    )(w, g, m, v)


ADA_COLS = 6 * D_MODEL // N_CHIPS
ADA_TN = 512


def _ada_fwd(c_all, ada_w, ada_b):
    def body(c_ref, w_ref, b_ref, o_ref):
        cv = c_ref[...]
        o_ref[...] = _dot_exact(cv * _sigmoid(cv), w_ref[...]) + b_ref[...]

    return pl.pallas_call(
        body, name="ada_fwd", grid=(ADA_COLS // ADA_TN,),
        in_specs=[_full((N_DEV, D_MODEL)), pl.BlockSpec((D_MODEL, ADA_TN), lambda j: (0, j)),
                  pl.BlockSpec((1, ADA_TN), lambda j: (0, j))],
        out_specs=pl.BlockSpec((N_DEV, ADA_TN), lambda j: (0, j)),
        out_shape=jax.ShapeDtypeStruct((N_DEV, ADA_COLS), f32), compiler_params=_cparams("parallel"),
    )(c_all, ada_w, ada_b)


def _ada_bwd(c_all, dmod, w, m, v):
    def body(c_ref, d_ref, w_ref, m_ref, v_ref, g_ref, dl_ref, m2_ref, v2_ref):
        cv = c_ref[...]
        g = lax.dot_general(cv * _sigmoid(cv), d_ref[...], (((0,), (0,)), ((), ())), preferred_element_type=f32,
                            precision=lax.Precision.HIGHEST)
        g_ref[...] = g
        dl_ref[...], m2_ref[...], v2_ref[...] = _adamw_math(w_ref[...], g, m_ref[...], v_ref[...])

    wspec = pl.BlockSpec((D_MODEL, ADA_TN), lambda j: (0, j))
    return pl.pallas_call(
        body, name="ada_bwd", grid=(ADA_COLS // ADA_TN,),
        in_specs=[_full((N_DEV, D_MODEL)), pl.BlockSpec((N_DEV, ADA_TN), lambda j: (0, j)), wspec, wspec, wspec],
        out_specs=[wspec] * 4, out_shape=[jax.ShapeDtypeStruct((D_MODEL, ADA_COLS), f32)] * 4,
        compiler_params=_cparams("parallel"),
    )(c_all, dmod, w, m, v)


def _small_update(gathered, w, m, v):
    n = w.shape[1]

    def body(g_ref, w_ref, m_ref, v_ref, gs_ref, d_ref, m2_ref, v2_ref):
        g = g_ref[0:1, :]
        for i in range(1, N_DEV):
            g = g + g_ref[i:i + 1, :]
        gs_ref[...] = g
        d_ref[...], m2_ref[...], v2_ref[...] = _adamw_math(w_ref[...], g, m_ref[...], v_ref[...])

    return pl.pallas_call(body, name="small_update", out_shape=[jax.ShapeDtypeStruct((1, n), f32)] * 4,
                          compiler_params=_cparams())(gathered, w, m, v)


def _place():
    return lax.axis_index("x"), lax.axis_index("y"), lax.axis_index("c")


def _flip(x, y, c, m):
    return (1 - x if m & 4 else x, 1 - y if m & 2 else y, 1 - c if m & 1 else c)


_VMEM_SPEC = pl.BlockSpec(memory_space=pltpu.VMEM)
_ANY_SPEC = pl.BlockSpec(memory_space=pl.ANY)


def _allgather8(v, name):
    n = v.shape[1]

    def body(v_ref, out_ref, send_sems, recv_sems, local_sem):
        x, y, c = _place()

        def rows(px, py, pc):
            return out_ref.at[pl.ds(pl.multiple_of((4 * px + 2 * py + pc) * 8, 8), 8), :]

        def copy(m, src, dst, to):
            return pltpu.make_async_remote_copy(src_ref=src, dst_ref=dst, send_sem=send_sems.at[m - 1],
                                                recv_sem=recv_sems.at[m - 1], device_id=to, device_id_type=MESH)

        mine = pltpu.make_async_copy(v_ref, rows(x, y, c), local_sem)
        mine.start()
        sends = [copy(m, v_ref, rows(x, y, c), _flip(x, y, c, m)) for m in range(1, N_DEV)]
        for cp in sends:
            cp.start()
        for m in range(1, N_DEV):
            peer = _flip(x, y, c, m)
            copy(m, v_ref, rows(*peer), peer).wait_recv()
        for cp in sends:
            cp.wait_send()
        mine.wait()

    return pl.pallas_call(
        body, name=name, out_shape=jax.ShapeDtypeStruct((8 * N_DEV, n), f32), in_specs=[_VMEM_SPEC],
        out_specs=_VMEM_SPEC,
        scratch_shapes=[pltpu.SemaphoreType.DMA((N_DEV - 1,)), pltpu.SemaphoreType.DMA((N_DEV - 1,)),
                        pltpu.SemaphoreType.DMA],
    )(v)


def _mod_exchange(mod_all):
    def body(src_ref, out_ref, send_sems, recv_sems, local_sem):
        x, y, c = _place()

        def copy(m, src, dst, to):
            return pltpu.make_async_remote_copy(src_ref=src, dst_ref=dst, send_sem=send_sems.at[m - 1],
                                                recv_sem=recv_sems.at[m - 1], device_id=to, device_id_type=MESH)

        mine = pltpu.make_async_copy(src_ref.at[4 * x + 2 * y + c], out_ref.at[2 * x + y], local_sem)
        mine.start()
        sends = []
        for m in range(1, N_CHIPS):
            px, py, pc = _flip(x, y, c, 2 * m)
            sends.append(copy(m, src_ref.at[4 * px + 2 * py + pc], out_ref.at[2 * x + y], (px, py, pc)))
        for cp in sends:
            cp.start()
        for m in range(1, N_CHIPS):
            px, py, pc = _flip(x, y, c, 2 * m)
            copy(m, src_ref.at[0], out_ref.at[2 * px + py], (px, py, pc)).wait_recv()
        for cp in sends:
            cp.wait_send()
        mine.wait()

    return pl.pallas_call(
        body, name="mod_exchange", out_shape=jax.ShapeDtypeStruct((N_CHIPS, 12, 128), f32), in_specs=[_VMEM_SPEC],
        out_specs=_VMEM_SPEC,
        scratch_shapes=[pltpu.SemaphoreType.DMA((N_CHIPS - 1,)), pltpu.SemaphoreType.DMA((N_CHIPS - 1,)),
                        pltpu.SemaphoreType.DMA],
    )(mod_all)


def _gather_weights(blob):
    def body(blob_ref, out_ref, send_sems, recv_sems, local_sem):
        x, y, c = _place()
        sib = (x, y, 1 - c)

        def half(px, py, pc):
            return out_ref.at[2 * px + py, pl.ds(pl.multiple_of(pc * R_HALF, 16), R_HALF), :]

        def copy(k, src, dst, to):
            return pltpu.make_async_remote_copy(src_ref=src, dst_ref=dst, send_sem=send_sems.at[k],
                                                recv_sem=recv_sems.at[k], device_id=to, device_id_type=MESH)

        mine = pltpu.make_async_copy(blob_ref, out_ref.at[2 * x + y], local_sem)
        mine.start()
        my_half = blob_ref.at[pl.ds(pl.multiple_of(c * R_HALF, 16), R_HALF), :]
        first = [copy(m - 1, my_half, half(x, y, c), _flip(x, y, c, 2 * m)) for m in range(1, N_CHIPS)]
        for cp in first:
            cp.start()
        passed = []
        for m in range(1, N_CHIPS):
            px, py, pc = _flip(x, y, c, 2 * m)
            copy(m - 1, my_half, half(px, py, pc), (px, py, pc)).wait_recv()
            fwd = copy(2 + m, half(px, py, pc), half(px, py, pc), sib)
            fwd.start()
            passed.append(fwd)
        for m in range(1, N_CHIPS):
            px, py, pc = _flip(x, y, c, 2 * m)
            copy(2 + m, my_half, half(px, py, 1 - pc), sib).wait_recv()
        for cp in first + passed:
            cp.wait_send()
        mine.wait()

    return pl.pallas_call(
        body, name="gather_weights", out_shape=jax.ShapeDtypeStruct((N_CHIPS, R_END, D_MODEL), bf16),
        in_specs=[_ANY_SPEC], out_specs=_ANY_SPEC,
        scratch_shapes=[pltpu.SemaphoreType.DMA((6,)), pltpu.SemaphoreType.DMA((6,)), pltpu.SemaphoreType.DMA],
    )(blob)


def _rs_to_sibling(gb):
    def body(g_ref, out_ref, send_sems, recv_sems):
        x, y, c = _place()
        cps = [pltpu.make_async_remote_copy(src_ref=g_ref.at[j, 1 - c], dst_ref=out_ref.at[j], send_sem=send_sems.at[j],
                                            recv_sem=recv_sems.at[j], device_id=(x, y, 1 - c), device_id_type=MESH)
               for j in range(N_CHIPS)]
        for cp in cps:
            cp.start()
        for cp in cps:
            cp.wait()

    return pl.pallas_call(
        body, name="rs_to_sibling", out_shape=jax.ShapeDtypeStruct((N_CHIPS, R_HALF, D_MODEL), bf16),
        in_specs=[_ANY_SPEC], out_specs=_ANY_SPEC,
        scratch_shapes=[pltpu.SemaphoreType.DMA((N_CHIPS,)), pltpu.SemaphoreType.DMA((N_CHIPS,))],
    )(gb)


def _rs_to_chips(pb):
    def body(p_ref, out_ref, send_sems, recv_sems):
        x, y, c = _place()
        cps = []
        for m in range(1, N_CHIPS):
            px, py, pc = _flip(x, y, c, 2 * m)
            cps.append(pltpu.make_async_remote_copy(src_ref=p_ref.at[2 * px + py], dst_ref=out_ref.at[m - 1],
                                                    send_sem=send_sems.at[m - 1], recv_sem=recv_sems.at[m - 1],
                                                    device_id=(px, py, pc), device_id_type=MESH))
        for cp in cps:
            cp.start()
        for cp in cps:
            cp.wait()

    return pl.pallas_call(
        body, name="rs_to_chips", out_shape=jax.ShapeDtypeStruct((N_CHIPS - 1, R_HALF, D_MODEL), bf16),
        in_specs=[_ANY_SPEC], out_specs=_ANY_SPEC,
        scratch_shapes=[pltpu.SemaphoreType.DMA((N_CHIPS - 1,)), pltpu.SemaphoreType.DMA((N_CHIPS - 1,))],
    )(pb)


def _rs_share(ghalf):
    def body(h_ref, out_ref, send_sem, recv_sem, local_sem):
        x, y, c = _place()

        def rows(pc):
            return out_ref.at[pl.ds(pl.multiple_of(pc * R_HALF, 8), R_HALF), :]

        mine = pltpu.make_async_copy(h_ref, rows(c), local_sem)
        mine.start()
        cp = pltpu.make_async_remote_copy(src_ref=h_ref, dst_ref=rows(c), send_sem=send_sem, recv_sem=recv_sem,
                                          device_id=(x, y, 1 - c), device_id_type=MESH)
        cp.start()
        pltpu.make_async_remote_copy(src_ref=h_ref, dst_ref=rows(1 - c), send_sem=send_sem, recv_sem=recv_sem,
                                     device_id=(x, y, 1 - c), device_id_type=MESH).wait_recv()
        cp.wait_send()
        mine.wait()

    return pl.pallas_call(
        body, name="rs_share", out_shape=jax.ShapeDtypeStruct((R_END, D_MODEL), f32), in_specs=[_ANY_SPEC],
        out_specs=_ANY_SPEC,
        scratch_shapes=[pltpu.SemaphoreType.DMA, pltpu.SemaphoreType.DMA, pltpu.SemaphoreType.DMA],
    )(ghalf)


RS_TR = 256


def _rs_sum_pair(place, gf, recv):
    def body(pl_ref, g_ref, r_ref, o_ref):
        o_ref[0] = (g_ref[0, 0] + r_ref[0].astype(f32)).astype(bf16)

    return pl.pallas_call(
        body, name="rs_sum_pair",
        grid_spec=pltpu.PrefetchScalarGridSpec(
            num_scalar_prefetch=1, grid=(N_CHIPS, R_HALF // RS_TR),
            in_specs=[pl.BlockSpec((1, 1, RS_TR, D_MODEL), lambda j, i, p: (j, p[0], i, 0)),
                      pl.BlockSpec((1, RS_TR, D_MODEL), lambda j, i, p: (j, i, 0))],
            out_specs=pl.BlockSpec((1, RS_TR, D_MODEL), lambda j, i, p: (j, i, 0))),
        out_shape=jax.ShapeDtypeStruct((N_CHIPS, R_HALF, D_MODEL), bf16),
        compiler_params=_cparams("parallel", "parallel"),
    )(place, gf, recv)


def _rs_sum_chips(place, gf, recv_sib, recv_chips):
    def body(pl_ref, g_ref, r1_ref, r2_ref, o_ref):
        acc = g_ref[0, 0] + r1_ref[0].astype(f32)
        for k in range(N_CHIPS - 1):
            acc = acc + r2_ref[k].astype(f32)
        o_ref[...] = acc

    return pl.pallas_call(
        body, name="rs_sum_chips",
        grid_spec=pltpu.PrefetchScalarGridSpec(
            num_scalar_prefetch=1, grid=(R_HALF // RS_TR,),
            in_specs=[pl.BlockSpec((1, 1, RS_TR, D_MODEL), lambda i, p: (p[1], p[0], i, 0)),
                      pl.BlockSpec((1, RS_TR, D_MODEL), lambda i, p: (p[1], i, 0)),
                      pl.BlockSpec((N_CHIPS - 1, RS_TR, D_MODEL), lambda i, p: (0, i, 0))],
            out_specs=pl.BlockSpec((RS_TR, D_MODEL), lambda i, p: (i, 0))),
        out_shape=jax.ShapeDtypeStruct((R_HALF, D_MODEL), f32),
        compiler_params=_cparams("parallel"),
    )(place, gf, recv_sib, recv_chips)


SMALL_SLOTS = (("ada_b", 6144), ("ln_in_g", 1024), ("ln_in_b", 1024), ("conv_b", 1536), ("dt_bias", 128), ("a_log", 128),
               ("d_skip", 128), ("ssd_norm_w", 1024), ("attn_sinks", 128), ("ln1_g", 1024), ("ln1_b", 1024),
               ("b_ff1", 4096), ("b_ff2", 1024), ("ln2_g", 1024), ("ln2_b", 1024), ("conv_w", 6144), ("loss", 1024))
SMALL_N = sum(n for _, n in SMALL_SLOTS)
assert SMALL_N % 1024 == 0


def _pack_small(vals):
    parts = []
    for name, n in SMALL_SLOTS:
        v = vals.get(name)
        v = jnp.zeros((n,), f32) if v is None else v.reshape(-1).astype(f32)
        parts.append(jnp.pad(v, (0, n - v.shape[0])))
    return jnp.concatenate(parts)


def _unpack_small(vec):
    out, off = {}, 0
    for name, n in SMALL_SLOTS:
        out[name] = vec[off:off + n]
        off += n
    return out


def _pad128(v):
    v = v.reshape(1, -1)
    return jnp.pad(v, ((0, 0), (0, 128 - v.shape[1])))


def _row(v):
    return v.reshape(1, -1)


def _w_in_layout(w):
    return jnp.concatenate([w[:, :2560], w[:, 2576:3856], w[:, 2560:2576], jnp.zeros((D_MODEL, C_END - PROJ_WIDTH), w.dtype)],
                           axis=1)


def _w_in_unlayout(g):
    return jnp.concatenate([g[:, :2560], g[:, C_DT:C_DT + 16], g[:, 2560:C_DT]], axis=1)


def kernel(x, c, ln_in_g, ln_in_b, ada_w, ada_b, w_in, conv_w, conv_b, dt_bias, a_log, d_skip, ssd_norm_w, attn_sinks, w_out, ln1_g, ln1_b, w_ff1, b_ff1, w_ff2, b_ff2, ln2_g, ln2_b, loss_target, m_ln_in_g, m_ln_in_b, m_ada_w, m_ada_b, m_w_in, m_conv_w, m_conv_b, m_dt_bias, m_a_log, m_d_skip, m_ssd_norm_w, m_attn_sinks, m_w_out, m_ln1_g, m_ln1_b, m_w_ff1, m_b_ff1, m_w_ff2, m_b_ff2, m_ln2_g, m_ln2_b, v_ln_in_g, v_ln_in_b, v_ada_w, v_ada_b, v_w_in, v_conv_w, v_conv_b, v_dt_bias, v_a_log, v_d_skip, v_ssd_norm_w, v_attn_sinks, v_w_out, v_ln1_g, v_ln1_b, v_w_ff1, v_b_ff1, v_w_ff2, v_b_ff2, v_ln2_g, v_ln2_b):
    xi, yi, ci = _place()
    chip = 2 * xi + yi
    place = jnp.stack([ci, chip]).astype(jnp.int32)
    x2, tgt = x[0], loss_target[0]
    w_shard_cols = PROJ_WIDTH // N_CHIPS

    cond = jnp.concatenate([c.reshape(-1), conv_w.reshape(-1), jnp.zeros((512,), f32)]).reshape(8, 384)
    cond_all = _allgather8(cond, "gather_cond").reshape(N_DEV, 3072)
    c_all = cond_all[:, :D_MODEL]
    conv_w_full = jnp.concatenate([cond_all[2 * j, D_MODEL:D_MODEL + 1536].reshape(CONV_K, 384) for j in range(N_CHIPS)], axis=1)

    ada_b_mine = lax.dynamic_slice(ada_b, (0, chip * ADA_COLS), (1, ADA_COLS))
    mod_all = _ada_fwd(c_all, ada_w[0], ada_b_mine)
    mod_mine = _mod_exchange(mod_all.reshape(N_DEV, 12, 128)).reshape(6, D_MODEL)
    mod = jnp.concatenate([mod_mine, jnp.zeros((2, D_MODEL), f32)], axis=0)

    blob = jnp.concatenate([jnp.pad(w_in[0], ((0, 0), (0, D_MODEL - w_shard_cols))), w_out[0], w_ff1[0], w_ff2[0]],
                           axis=0).astype(bf16)
    wall = _gather_weights(blob)
    w_in_f = _w_in_layout(jnp.transpose(wall[:, R_IN:R_OUT, :w_shard_cols], (1, 0, 2)).reshape(D_MODEL, PROJ_WIDTH))
    w_out_f = wall[:, R_OUT:R_FF1].reshape(2 * SSD_WIDTH, D_MODEL)
    w_ff1_f = jnp.transpose(wall[:, R_FF1:R_FF2], (1, 0, 2)).reshape(D_MODEL, D_FF)
    w_ff2_f = wall[:, R_FF2:R_END].reshape(D_FF, D_MODEL)

    e_mat = _head_expand()
    dsk_x = jnp.repeat(d_skip[0], HEAD_DIM).reshape(1, SSD_WIDTH)
    dtb, alog = _pad128(dt_bias), _pad128(a_log)
    sinks = attn_sinks[0]
    lng, lnb = _row(ln_in_g), _row(ln_in_b)
    u1, z, xr, xc, q, kv, dtr = _inproj_fwd(x2, mod, lng, lnb, w_in_f, conv_w_full, conv_b)
    y, prev_all = _ssd_fwd(xc, dtr, dtb, alog, dsk_x, e_mat)
    o, lse = _attn_fwd(q, kv, sinks)
    yn, mix, r1 = _outproj_fwd(y, z, o, x2, mod, lng, lnb, ssd_norm_w, w_out_f)

    dr1, u2, s_act, da, df, acc_mlp, db1 = _mlp_fwd_bwd(r1, tgt, mod, ln1_g, ln1_b, ln2_g, ln2_b, w_ff1_f, b_ff1, w_ff2_f, b_ff2)
    g_ff1 = _wgrad(u2, da, "wgrad_ff1")
    g_ff2 = _wgrad(s_act, df, "wgrad_ff2")
    dy, dz, do, dmix, acc_out = _outproj_bwd(dr1, mix, y, z, mod, ssd_norm_w, w_out_f)
    g_out = jnp.concatenate([_wgrad(yn, dmix, "wgrad_out_y"), _wgrad(o, dmix, "wgrad_out_o")], axis=0)
    dq, dkv, dsink = _attn_bwd(q, kv, do, lse, sinks)
    dxc, ddt, acc_ssd, dd_x = _ssd_bwd(xc, dtr, dy, prev_all, dtb, alog, dsk_x, e_mat)
    dxr, acc_conv = _conv_bwd(dxc, xr, conv_w_full, conv_b)
    grad_x, acc_in = _inproj_bwd(dz, dxr, dq, dkv, ddt, dr1, x2, mod, lng, lnb, w_in_f)
    g_in = jnp.concatenate([_wgrad(u1, dz, "wgrad_in_z"), _wgrad(u1, dxr, "wgrad_in_xbc"), _wgrad(u1, dq, "wgrad_in_q"),
                            _wgrad(u1, dkv, "wgrad_in_kv"), _wgrad(u1, ddt.astype(bf16), "wgrad_in_dt")], axis=1)

    dmod = jnp.concatenate([acc_in[1], acc_in[0], acc_out[0], acc_mlp[A_SH2], acc_mlp[A_SC2], acc_mlp[A_G2]])
    small = {
        "ada_b": dmod, "ln_in_g": acc_in[2], "ln_in_b": acc_in[3], "conv_b": acc_conv[4], "dt_bias": acc_ssd[1, :16],
        "a_log": acc_ssd[0, :16] * (-jnp.exp(a_log[0])), "d_skip": jnp.sum(dd_x.reshape(SSD_HEADS, HEAD_DIM), axis=1),
        "ssd_norm_w": acc_out[1], "attn_sinks": dsink[0, :16], "ln1_g": acc_mlp[A_LN1G], "ln1_b": acc_mlp[A_LN1B],
        "b_ff1": db1[0], "b_ff2": acc_mlp[A_B2], "ln2_g": acc_mlp[A_LN2G], "ln2_b": acc_mlp[A_LN2B],
        "conv_w": acc_conv[0:CONV_K], "loss": acc_mlp[A_LOSS],
    }
    small_all = _allgather8(_pack_small(small).reshape(8, SMALL_N // 8), "gather_small").reshape(N_DEV, SMALL_N)
    params = dict(ada_b=ada_b, ln_in_g=ln_in_g, ln_in_b=ln_in_b, conv_b=conv_b, dt_bias=dt_bias, a_log=a_log, d_skip=d_skip,
                  ssd_norm_w=ssd_norm_w, attn_sinks=attn_sinks, ln1_g=ln1_g, ln1_b=ln1_b, b_ff1=b_ff1, b_ff2=b_ff2,
                  ln2_g=ln2_g, ln2_b=ln2_b)
    moms = dict(ada_b=m_ada_b, ln_in_g=m_ln_in_g, ln_in_b=m_ln_in_b, conv_b=m_conv_b, dt_bias=m_dt_bias, a_log=m_a_log,
                d_skip=m_d_skip, ssd_norm_w=m_ssd_norm_w, attn_sinks=m_attn_sinks, ln1_g=m_ln1_g, ln1_b=m_ln1_b,
                b_ff1=m_b_ff1, b_ff2=m_b_ff2, ln2_g=m_ln2_g, ln2_b=m_ln2_b)
    vels = dict(ada_b=v_ada_b, ln_in_g=v_ln_in_g, ln_in_b=v_ln_in_b, conv_b=v_conv_b, dt_bias=v_dt_bias, a_log=v_a_log,
                d_skip=v_d_skip, ssd_norm_w=v_ssd_norm_w, attn_sinks=v_attn_sinks, ln1_g=v_ln1_g, ln1_b=v_ln1_b,
                b_ff1=v_b_ff1, b_ff2=v_b_ff2, ln2_g=v_ln2_g, ln2_b=v_ln2_b)
    gs, dl, m2, v2 = _small_update(small_all, _pack_small(params).reshape(1, -1), _pack_small(moms).reshape(1, -1),
                                   _pack_small(vels).reshape(1, -1))
    gs, dl, m2, v2 = (_unpack_small(t[0]) for t in (gs, dl, m2, v2))
    loss = jnp.sum(gs["loss"])

    dmod_mine = lax.dynamic_slice(small_all[:, :6 * D_MODEL], (0, chip * ADA_COLS), (N_DEV, ADA_COLS))
    g_ada, d_ada, m_ada, v_ada = _ada_bwd(c_all, dmod_mine, ada_w[0], m_ada_w[0], v_ada_w[0])

    g_conv = lax.dynamic_slice(gs["conv_w"].reshape(CONV_K, CONV_DIM), (0, chip * 384), (CONV_K, 384))
    d_conv, m_conv, v_conv = _adamw(conv_w[0], g_conv, m_conv_w[0], v_conv_w[0], "adamw_conv_w")

    g_in_c = jnp.transpose(_w_in_unlayout(g_in).reshape(D_MODEL, N_CHIPS, w_shard_cols), (1, 0, 2))
    gf = jnp.concatenate([jnp.pad(g_in_c, ((0, 0), (0, 0), (0, D_MODEL - w_shard_cols))),
                          g_out.reshape(N_CHIPS, 512, D_MODEL),
                          jnp.transpose(g_ff1.reshape(D_MODEL, N_CHIPS, D_MODEL), (1, 0, 2)),
                          g_ff2.reshape(N_CHIPS, D_MODEL, D_MODEL)], axis=1).reshape(N_CHIPS, 2, R_HALF, D_MODEL)
    from_sib = _rs_to_sibling(gf.astype(bf16))
    from_chips = _rs_to_chips(_rs_sum_pair(place, gf, from_sib))
    g_blob = _rs_share(_rs_sum_chips(place, gf, from_sib, from_chips))
    g_w_in = g_blob[R_IN:R_OUT, :w_shard_cols]
    g_w_out, g_w_ff1, g_w_ff2 = g_blob[R_OUT:R_FF1], g_blob[R_FF1:R_FF2], g_blob[R_FF2:R_END]
    d_w_in, m_w_in2, v_w_in2 = _adamw(w_in[0], g_w_in, m_w_in[0], v_w_in[0], "adamw_w_in")
    d_w_out, m_w_out2, v_w_out2 = _adamw(w_out[0], g_w_out, m_w_out[0], v_w_out[0], "adamw_w_out")
    d_w_ff1, m_w_ff12, v_w_ff12 = _adamw(w_ff1[0], g_w_ff1, m_w_ff1[0], v_w_ff1[0], "adamw_w_ff1")
    d_w_ff2, m_w_ff22, v_w_ff22 = _adamw(w_ff2[0], g_w_ff2, m_w_ff2[0], v_w_ff2[0], "adamw_w_ff2")

    def small_out(t, name, like):
        return t[name][:like.size].reshape(like.shape)

    def outputs(t, big):
        return [small_out(t, "ln_in_g", ln_in_g), small_out(t, "ln_in_b", ln_in_b), big["ada_w"][None],
                small_out(t, "ada_b", ada_b), big["w_in"][None], big["conv_w"][None], small_out(t, "conv_b", conv_b),
                small_out(t, "dt_bias", dt_bias), small_out(t, "a_log", a_log), small_out(t, "d_skip", d_skip),
                small_out(t, "ssd_norm_w", ssd_norm_w), small_out(t, "attn_sinks", attn_sinks), big["w_out"][None],
                small_out(t, "ln1_g", ln1_g), small_out(t, "ln1_b", ln1_b), big["w_ff1"][None], small_out(t, "b_ff1", b_ff1),
                big["w_ff2"][None], small_out(t, "b_ff2", b_ff2), small_out(t, "ln2_g", ln2_g), small_out(t, "ln2_b", ln2_b)]

    grads = outputs(gs, dict(ada_w=g_ada, w_in=g_w_in, conv_w=g_conv, w_out=g_w_out, w_ff1=g_w_ff1, w_ff2=g_w_ff2))
    deltas = outputs(dl, dict(ada_w=d_ada, w_in=d_w_in, conv_w=d_conv, w_out=d_w_out, w_ff1=d_w_ff1, w_ff2=d_w_ff2))
    new_m = outputs(m2, dict(ada_w=m_ada, w_in=m_w_in2, conv_w=m_conv, w_out=m_w_out2, w_ff1=m_w_ff12, w_ff2=m_w_ff22))
    new_v = outputs(v2, dict(ada_w=v_ada, w_in=v_w_in2, conv_w=v_conv, w_out=v_w_out2, w_ff1=v_w_ff12, w_ff2=v_w_ff22))
    return (loss, grad_x[None], *grads, *deltas, *new_m, *new_v)
```

```python
import functools
import math

import numpy as np
import jax
import jax.numpy as jnp
from jax import lax
from jax.experimental import pallas as pl
from jax.experimental.pallas import tpu as pltpu

f32 = jnp.float32
bf16 = jnp.bfloat16

D_MODEL = 1024
SSD_WIDTH = 1024
SSD_HEADS = 16
HEAD_DIM = 64
SSD_STATE = 128
SSD_GROUPS = 2
CHUNK = 128
CONV_K = 4
CONV_DIM = 1536
ATTN_HEADS = 16
D_FF = 4096
PROJ_WIDTH = 3856
ALPHA = 2.0 ** 0.25
LN_EPS = 1e-5
RMS_EPS = 1e-5
ATTN_SCALE = HEAD_DIM ** -0.5
NEG = -1e30

ADAM_LR = 0.001
ADAM_B1 = 0.9
ADAM_B2 = 0.999
ADAM_EPS = 1e-08
ADAM_WD = 0.01
ADAM_STEP = 10

C_Z, C_XBC, C_Q, C_KV, C_DT, C_END = 0, 1024, 2560, 3584, 3840, 3968
R_IN, R_FF1, R_FF2, R_OUT, R_END = 0, 1024, 2048, 3072, 3584
R_HALF = R_END // 2
WG_TM = 512
STAGE_ROWS = 512
N_CHIPS = 4
N_DEV = 8
VMEM_LIMIT = 56 * 1024 * 1024
MESH = pl.DeviceIdType.MESH

ALIBI_SLOPES = tuple(2.0 ** (-8.0 / ATTN_HEADS * (i + 1)) for i in range(ATTN_HEADS))


def _cparams(*sem):
    return pltpu.CompilerParams(dimension_semantics=sem, vmem_limit_bytes=VMEM_LIMIT)


def _sigmoid(x):
    return 1.0 / (1.0 + jnp.exp(-x))


def _softplus(x):
    return jnp.maximum(x, 0.0) + jnp.log1p(jnp.exp(-jnp.abs(x)))


def _ln_stats(x):
    mu = jnp.mean(x, axis=-1, keepdims=True)
    xc = x - mu
    var = jnp.mean(xc * xc, axis=-1, keepdims=True)
    rstd = lax.rsqrt(var + LN_EPS)
    return xc * rstd, rstd


def _ln_bwd(dy, xhat, rstd, g):
    dxh = dy * g
    m1 = jnp.mean(dxh, axis=-1, keepdims=True)
    m2 = jnp.mean(dxh * xhat, axis=-1, keepdims=True)
    return rstd * (dxh - m1 - xhat * m2)


def _dot(a, b):
    return jnp.dot(a, b, preferred_element_type=f32)


def _dot_nt(a, b):
    return lax.dot_general(a, b, (((1,), (1,)), ((), ())), preferred_element_type=f32)


def _dot_tn(a, b):
    return lax.dot_general(a, b, (((0,), (0,)), ((), ())), preferred_element_type=f32)


def _dot_exact(a, b):
    return jnp.dot(a, b, preferred_element_type=f32, precision=lax.Precision.HIGHEST)


def _full(shape):
    nd = len(shape)
    return pl.BlockSpec(shape, lambda *_: (0,) * nd)


def _resident(shape):
    nd = len(shape)
    return pl.BlockSpec(shape, lambda *_: (0,) * nd, pipeline_mode=pl.Buffered(1))


def _rows(tm, n):
    return pl.BlockSpec((tm, n), lambda i: (i, 0))


def _wall_spec(rows, r0):
    return pl.BlockSpec((N_CHIPS, rows, D_MODEL), lambda *_: (0, r0 // rows, 0), pipeline_mode=pl.Buffered(1))


def _inproj_fwd(x, mod, ln_g, ln_b, w_in, conv_w, conv_b):
    L = x.shape[0]
    tm = 256

    def body(x_ref, mod_ref, g_ref, b_ref, w_ref, cw_ref, cb_ref,
             u1_ref, z_ref, xr_ref, xc_ref, q_ref, kv_ref, dt_ref, halo, buf):
        @pl.when(pl.program_id(0) == 0)
        def _():
            halo[...] = jnp.zeros_like(halo)

        xhat, _ = _ln_stats(x_ref[...])
        h0 = xhat * g_ref[...] + b_ref[...]
        u1 = (h0 * (1.0 + mod_ref[1:2, :]) + mod_ref[0:1, :]).astype(bf16)
        u1_ref[...] = u1
        z_ref[...] = _dot(u1, w_ref[:, C_Z:C_XBC])
        xr = _dot(u1, w_ref[:, C_XBC:C_Q])
        xr_ref[...] = xr
        q_ref[...] = _dot(u1, w_ref[:, C_Q:C_KV]).astype(bf16)
        kv_ref[...] = _dot(u1, w_ref[:, C_KV:C_DT]).astype(bf16)
        dt_ref[...] = _dot(u1, w_ref[:, C_DT:C_END])
        buf[0:8, :] = halo[...]
        buf[8:8 + tm, :] = xr
        pre = cb_ref[...] + cw_ref[0:1, :] * buf[5:5 + tm, :]
        for k in range(1, CONV_K):
            pre = pre + cw_ref[k:k + 1, :] * buf[5 + k:5 + k + tm, :]
        xc_ref[...] = pre * _sigmoid(pre)
        halo[...] = xr[tm - 8:tm, :]

    return pl.pallas_call(
        body, name="inproj_fwd", grid=(L // tm,),
        in_specs=[_rows(tm, D_MODEL), _full((8, D_MODEL)), _full((1, D_MODEL)), _full((1, D_MODEL)),
                  _resident((D_MODEL, C_END)), _full((CONV_K, CONV_DIM)), _full((1, CONV_DIM))],
        out_specs=[_rows(tm, D_MODEL), _rows(tm, D_MODEL), _rows(tm, CONV_DIM), _rows(tm, CONV_DIM),
                   _rows(tm, D_MODEL), _rows(tm, 256), _rows(tm, 128)],
        out_shape=[jax.ShapeDtypeStruct((L, D_MODEL), bf16), jax.ShapeDtypeStruct((L, D_MODEL), f32),
                   jax.ShapeDtypeStruct((L, CONV_DIM), f32), jax.ShapeDtypeStruct((L, CONV_DIM), f32),
                   jax.ShapeDtypeStruct((L, D_MODEL), bf16), jax.ShapeDtypeStruct((L, 256), bf16),
                   jax.ShapeDtypeStruct((L, 128), f32)],
        scratch_shapes=[pltpu.VMEM((8, CONV_DIM), f32), pltpu.VMEM((tm + 8, CONV_DIM), f32)],
        compiler_params=_cparams("arbitrary"),
    )(x, mod, ln_g, ln_b, w_in, conv_w, conv_b)


def _head_expand():
    e = np.zeros((128, SSD_WIDTH), np.float32)
    for h in range(SSD_HEADS):
        e[h, h * HEAD_DIM:(h + 1) * HEAD_DIM] = 1.0
    return jnp.asarray(e)


def _ssd_chunk_common(dt_raw, dtb, a_row, e_mat):
    T = CHUNK
    lane = lax.broadcasted_iota(jnp.int32, (T, 128), 1)
    dt = jnp.where(lane < SSD_HEADS, _softplus(dt_raw + dtb), 0.0)
    a = dt * a_row
    r = lax.broadcasted_iota(jnp.int32, (T, T), 0)
    c = lax.broadcasted_iota(jnp.int32, (T, T), 1)
    tril = (c <= r).astype(f32)
    cum = _dot_exact(tril, a)
    dtx = _dot_exact(dt, e_mat)
    cumx = _dot_exact(cum, e_mat)
    return dt, a, cum, dtx, cumx, r, c


def _ssd_fwd(xc, dt_raw, dt_bias, a_log, d_skip_x, e_mat):
    L = xc.shape[0]
    nc = L // CHUNK
    T = CHUNK

    def body(xc_ref, dt_ref, dtb_ref, al_ref, dsk_ref, e_ref, y_ref, prev_ref, st):
        @pl.when(pl.program_id(0) == 0)
        def _():
            st[...] = jnp.zeros_like(st)

        a_row = -jnp.exp(al_ref[...])
        lane1 = lax.broadcasted_iota(jnp.int32, (1, 128), 1)
        a_row = jnp.where(lane1 < SSD_HEADS, a_row, 0.0)
        dt, a, cum, dtx, cumx, r, c = _ssd_chunk_common(dt_ref[...], dtb_ref[...], a_row, e_ref[...])
        cum_t = cum.T
        ex = jnp.exp(cumx)
        last = cumx[T - 1:T, :]
        wx = jnp.exp(last - cumx)
        cdx = jnp.exp(last)
        xs = xc_ref[:, 0:SSD_WIDTH]
        X = xs * dtx
        Xb = X.astype(bf16)
        Xd = (X * wx).astype(bf16)
        prev = st[...]
        prev_ref[0] = prev
        prevb = prev.astype(bf16)
        tri = c <= r
        lane = lax.broadcasted_iota(jnp.int32, (T, 128), 1)
        y_blocks = []
        new_states = []
        for g in range(SSD_GROUPS):
            Bg = xc_ref[:, 1024 + 128 * g:1152 + 128 * g].astype(bf16)
            Cg = xc_ref[:, 1280 + 128 * g:1408 + 128 * g].astype(bf16)
            G = _dot_nt(Cg, Bg)
            yoff = _dot(Cg, prevb[:, 512 * g:512 * (g + 1)])
            new_states.append(_dot_tn(Bg, Xd[:, 512 * g:512 * (g + 1)]))
            for j in range(4):
                blk = 4 * g + j
                Xblk = Xb[:, 128 * blk:128 * (blk + 1)]
                ys = []
                for half in range(2):
                    h = 2 * blk + half
                    seg = jnp.minimum(cum[:, h:h + 1] - cum_t[h:h + 1, :], 0.0)
                    M = jnp.where(tri, G * jnp.exp(seg), 0.0).astype(bf16)
                    ys.append(_dot(M, Xblk))
                yd = jnp.where(lane < HEAD_DIM, ys[0], ys[1])
                sl = slice(128 * blk, 128 * (blk + 1))
                y_blocks.append(yd + ex[:, sl] * yoff[:, 128 * j:128 * (j + 1)] + dsk_ref[:, sl] * xs[:, sl])
        y_ref[...] = jnp.concatenate(y_blocks, axis=1)
        st[...] = prev * cdx + jnp.concatenate(new_states, axis=1)

    return pl.pallas_call(
        body, name="ssd_fwd", grid=(nc,),
        in_specs=[_rows(T, CONV_DIM), _rows(T, 128), _full((1, 128)), _full((1, 128)), _full((1, SSD_WIDTH)),
                  _full((128, SSD_WIDTH))],
        out_specs=[_rows(T, SSD_WIDTH), pl.BlockSpec((1, SSD_STATE, SSD_WIDTH), lambda i: (i, 0, 0))],
        out_shape=[jax.ShapeDtypeStruct((L, SSD_WIDTH), f32), jax.ShapeDtypeStruct((nc, SSD_STATE, SSD_WIDTH), f32)],
        scratch_shapes=[pltpu.VMEM((SSD_STATE, SSD_WIDTH), f32)],
        compiler_params=_cparams("arbitrary"),
    )(xc, dt_raw, dt_bias, a_log, d_skip_x, e_mat)


def _kv_halves(kv_prev, kv_cur, first):
    kv = jnp.concatenate([jnp.where(first, 0.0, kv_prev.astype(f32)), kv_cur.astype(f32)], axis=0)
    lane = lax.broadcasted_iota(jnp.int32, (2 * CHUNK, 128), 1)
    lo = lane < HEAD_DIM
    out = []
    for g in range(2):
        per_half = []
        for half in range(2):
            both = []
            for t in (kv[:, 0:128], kv[:, 128:256]):
                src = t if g == half else pltpu.roll(t, HEAD_DIM, 1)
                both.append(jnp.where(lo if half == 0 else ~lo, src, 0.0).astype(bf16))
            per_half.append(tuple(both))
        out.append(per_half)
    return out


def _attn_masks(first):
    r = lax.broadcasted_iota(jnp.int32, (CHUNK, 2 * CHUNK), 0)
    c = lax.broadcasted_iota(jnp.int32, (CHUNK, 2 * CHUNK), 1)
    dist = r + CHUNK - c
    valid = (dist >= 0) & (dist < CHUNK) & ((c >= CHUNK) | jnp.logical_not(first))
    return dist.astype(f32), valid


def _attn_fwd(q, kv, sinks):
    L = q.shape[0]
    nb = L // CHUNK
    T = CHUNK

    def body(sink_ref, q_ref, kvp_ref, kvc_ref, o_ref, lse_ref):
        first = pl.program_id(0) == 0
        ext = _kv_halves(kvp_ref[...], kvc_ref[...], first)
        dist, valid = _attn_masks(first)
        lane = lax.broadcasted_iota(jnp.int32, (T, 128), 1)
        lse = jnp.zeros((T, 128), f32)
        o_blocks = []
        for blk in range(8):
            qb = q_ref[:, 128 * blk:128 * (blk + 1)]
            acc = None
            for half in range(2):
                h = 2 * blk + half
                k_ext, v_ext = ext[h // 8][half]
                s = _dot_nt(qb, k_ext) * ATTN_SCALE - ALIBI_SLOPES[h] * dist
                s = jnp.where(valid, s, NEG)
                sink = sink_ref[h]
                m = jnp.maximum(jnp.max(s, axis=-1, keepdims=True), sink)
                p = jnp.exp(s - m)
                den = jnp.sum(p, axis=-1, keepdims=True) + jnp.exp(sink - m)
                pn = (p / den).astype(bf16)
                oh = _dot(pn, v_ext)
                acc = oh if acc is None else acc + oh
                lse = jnp.where(lane == h, m + jnp.log(den), lse)
            o_blocks.append(acc.astype(bf16))
        o_ref[...] = jnp.concatenate(o_blocks, axis=1)
        lse_ref[...] = lse

    return pl.pallas_call(
        body, name="attn_fwd", grid=(nb,),
        in_specs=[pl.BlockSpec(memory_space=pltpu.SMEM), _rows(T, D_MODEL),
                  pl.BlockSpec((T, 256), lambda i: (jnp.maximum(i - 1, 0), 0)), _rows(T, 256)],
        out_specs=[_rows(T, D_MODEL), _rows(T, 128)],
        out_shape=[jax.ShapeDtypeStruct((L, D_MODEL), bf16), jax.ShapeDtypeStruct((L, 128), f32)],
        compiler_params=_cparams("arbitrary"),
    )(sinks, q, kv, kv)


def _gated_norm(y, z, w):
    sz = _sigmoid(z)
    hg = y * (z * sz)
    ns, rss = [], []
    for g in range(SSD_GROUPS):
        hs = hg[:, 512 * g:512 * (g + 1)]
        rs = lax.rsqrt(jnp.mean(hs * hs, axis=-1, keepdims=True) + RMS_EPS)
        ns.append(hs * rs)
        rss.append(rs)
    n = jnp.concatenate(ns, axis=1)
    return n * w, n, rss, sz


def _outproj_fwd(y, z, o, x, mod, ln_g, ln_b, norm_w, w_out):
    L = x.shape[0]
    tm = 256

    def body(y_ref, z_ref, o_ref, x_ref, mod_ref, g_ref, b_ref, nw_ref, w_ref, yn_ref, mix_ref, r1_ref):
        yn, _, _, _ = _gated_norm(y_ref[...], z_ref[...], nw_ref[...])
        ynb = yn.astype(bf16)
        yn_ref[...] = ynb
        mix = (_dot(ynb[:, 0:512], w_ref[0]) + _dot(ynb[:, 512:1024], w_ref[1])
               + _dot(o_ref[:, 0:512], w_ref[2]) + _dot(o_ref[:, 512:1024], w_ref[3]))
        mix_ref[...] = mix
        xhat, _ = _ln_stats(x_ref[...])
        h0 = xhat * g_ref[...] + b_ref[...]
        r1_ref[...] = ALPHA * h0 + (1.0 + mod_ref[2:3, :]) * mix

    v = _full((1, D_MODEL))
    return pl.pallas_call(
        body, name="outproj_fwd", grid=(L // tm,),
        in_specs=[_rows(tm, D_MODEL), _rows(tm, D_MODEL), _rows(tm, D_MODEL), _rows(tm, D_MODEL),
                  _full((8, D_MODEL)), v, v, v, _wall_spec(512, R_OUT)],
        out_specs=[_rows(tm, D_MODEL)] * 3,
        out_shape=[jax.ShapeDtypeStruct((L, D_MODEL), bf16), jax.ShapeDtypeStruct((L, D_MODEL), f32),
                   jax.ShapeDtypeStruct((L, D_MODEL), f32)],
        compiler_params=_cparams("parallel"),
    )(y, z, o, x, mod, ln_g, ln_b, norm_w, w_out)


A_LN2G, A_LN2B, A_G2, A_B2, A_SC2, A_SH2, A_LN1G, A_LN1B, A_LOSS = range(9)


def _mlp_fwd_bwd(r1, target, mod, ln1_g, ln1_b, ln2_g, ln2_b, w1, b1, w2, b2):
    L = r1.shape[0]
    tm = 256
    nj = D_FF // 1024

    def body(r1_ref, t_ref, mod_ref, g1_ref, bb1_ref, g2_ref, bb2_ref, w1_ref, b1_ref, w2_ref, b2_ref,
             dr1_ref, u2_ref, s_ref, da_ref, df_ref, acc_ref, db1_ref, hr):
        @pl.when(pl.program_id(0) == 0)
        def _():
            acc_ref[...] = jnp.zeros_like(acc_ref)
            db1_ref[...] = jnp.zeros_like(db1_ref)

        sc2, sh2, gate2 = mod_ref[4:5, :], mod_ref[3:4, :], mod_ref[5:6, :]
        xhat1, rstd1 = _ln_stats(r1_ref[...])
        h1 = xhat1 * g1_ref[...] + bb1_ref[...]
        u2f = h1 * (1.0 + sc2) + sh2
        u2 = u2f.astype(bf16)
        u2_ref[...] = u2
        f = jnp.zeros((tm, D_MODEL), f32) + b2_ref[...]
        for j in range(nj):
            cs = slice(1024 * j, 1024 * (j + 1))
            a = _dot(u2, w1_ref[j]) + b1_ref[:, cs]
            hrj = jnp.maximum(a, 0.0)
            hr[:, cs] = hrj
            sj = (hrj * hrj).astype(bf16)
            s_ref[:, cs] = sj
            f = f + _dot(sj, w2_ref[j])
        r2 = ALPHA * h1 + (1.0 + gate2) * f
        xhat2, rstd2 = _ln_stats(r2)
        h2 = xhat2 * g2_ref[...] + bb2_ref[...]
        diff = h2 - t_ref[...]
        dh2 = diff * (1.0 / D_MODEL)

        def add(row, val):
            acc_ref[row:row + 1, :] += jnp.sum(val, axis=0, keepdims=True)

        add(A_LOSS, diff * diff * (0.5 / D_MODEL))
        add(A_LN2G, dh2 * xhat2)
        add(A_LN2B, dh2)
        dr2 = _ln_bwd(dh2, xhat2, rstd2, g2_ref[...])
        add(A_G2, dr2 * f)
        df = dr2 * (1.0 + gate2)
        add(A_B2, df)
        dfb = df.astype(bf16)
        df_ref[...] = dfb
        du2 = jnp.zeros((tm, D_MODEL), f32)
        for j in range(nj):
            cs = slice(1024 * j, 1024 * (j + 1))
            ds = _dot_nt(dfb, w2_ref[j])
            daj = ds * (2.0 * hr[:, cs])
            db1_ref[:, cs] += jnp.sum(daj, axis=0, keepdims=True)
            dajb = daj.astype(bf16)
            da_ref[:, cs] = dajb
            du2 = du2 + _dot_nt(dajb, w1_ref[j])
        add(A_SC2, du2 * h1)
        add(A_SH2, du2)
        dh1 = ALPHA * dr2 + du2 * (1.0 + sc2)
        add(A_LN1G, dh1 * xhat1)
        add(A_LN1B, dh1)
        dr1_ref[...] = _ln_bwd(dh1, xhat1, rstd1, g1_ref[...])

    v = _full((1, D_MODEL))
    return pl.pallas_call(
        body, name="mlp_fwd_bwd", grid=(L // tm,),
        in_specs=[_rows(tm, D_MODEL), _rows(tm, D_MODEL), _full((8, D_MODEL)), v, v, v, v,
                  _wall_spec(1024, R_FF1), _full((1, D_FF)), _wall_spec(1024, R_FF2), v],
        out_specs=[_rows(tm, D_MODEL), _rows(tm, D_MODEL), _rows(tm, D_FF), _rows(tm, D_FF), _rows(tm, D_MODEL),
                   _full((16, D_MODEL)), _full((1, D_FF))],
        out_shape=[jax.ShapeDtypeStruct((L, D_MODEL), f32), jax.ShapeDtypeStruct((L, D_MODEL), bf16),
                   jax.ShapeDtypeStruct((L, D_FF), bf16), jax.ShapeDtypeStruct((L, D_FF), bf16),
                   jax.ShapeDtypeStruct((L, D_MODEL), bf16), jax.ShapeDtypeStruct((16, D_MODEL), f32),
                   jax.ShapeDtypeStruct((1, D_FF), f32)],
        scratch_shapes=[pltpu.VMEM((tm, D_FF), f32)],
        compiler_params=_cparams("arbitrary"),
    )(r1, target, mod, ln1_g, ln1_b, ln2_g, ln2_b, w1, b1, w2, b2)


def _wgrad(a, b, name):
    L, M = a.shape
    N = b.shape[1]
    tm = min(M, 512)
    tn = next(t for t in (1024, 768, 512, 256, 128) if N % t == 0)

    def body(a_ref, b_ref, o_ref):
        o_ref[...] = _dot_tn(a_ref[...], b_ref[...])

    return pl.pallas_call(
        body, name=name, grid=(M // tm, N // tn),
        in_specs=[pl.BlockSpec((L, tm), lambda i, j: (0, i)), pl.BlockSpec((L, tn), lambda i, j: (0, j))],
        out_specs=pl.BlockSpec((tm, tn), lambda i, j: (i, j)),
        out_shape=jax.ShapeDtypeStruct((M, N), f32),
        compiler_params=_cparams("parallel", "parallel"),
    )(a, b)


def _wgrad_blob(blob, a, b, name, place_of):
    L, M = a.shape
    N = b.shape[1]

    def body(blob_ref, a_ref, b_ref, o_ref):
        o_ref[0] = _dot_tn(a_ref[...], b_ref[...]).astype(bf16)

    return pl.pallas_call(
        body, name=name, grid=(M // WG_TM, N // D_MODEL),
        in_specs=[pl.BlockSpec(memory_space=pl.ANY), pl.BlockSpec((L, WG_TM), lambda t, n: (0, t)),
                  pl.BlockSpec((L, D_MODEL), lambda t, n: (0, n))],
        out_specs=pl.BlockSpec((1, WG_TM, D_MODEL), lambda t, n: (*place_of(t, n), 0)),
        out_shape=jax.ShapeDtypeStruct(blob.shape, bf16), input_output_aliases={0: 0},
        compiler_params=_cparams("parallel", "parallel"),
    )(blob, a, b)


def _outproj_bwd(dr1, mix, y, z, mod, norm_w, w_out):
    L = dr1.shape[0]
    tm = 256

    def body(dr1_ref, mix_ref, y_ref, z_ref, mod_ref, nw_ref, w_ref, dy_ref, dz_ref, do_ref, dmix_ref, acc_ref):
        @pl.when(pl.program_id(0) == 0)
        def _():
            acc_ref[...] = jnp.zeros_like(acc_ref)

        dr1 = dr1_ref[...]
        acc_ref[0:1, :] += jnp.sum(dr1 * mix_ref[...], axis=0, keepdims=True)
        dmix = (dr1 * (1.0 + mod_ref[2:3, :])).astype(bf16)
        dmix_ref[...] = dmix
        dyn = jnp.concatenate([_dot_nt(dmix, w_ref[0]), _dot_nt(dmix, w_ref[1])], axis=1)
        do_ref[...] = jnp.concatenate([_dot_nt(dmix, w_ref[2]), _dot_nt(dmix, w_ref[3])], axis=1).astype(bf16)
        yv, zv = y_ref[...], z_ref[...]
        _, n, rss, sz = _gated_norm(yv, zv, nw_ref[...])
        acc_ref[1:2, :] += jnp.sum(dyn * n, axis=0, keepdims=True)
        dn = dyn * nw_ref[...]
        parts = []
        for g in range(SSD_GROUPS):
            sl = slice(512 * g, 512 * (g + 1))
            dng, ng = dn[:, sl], n[:, sl]
            parts.append(rss[g] * (dng - ng * jnp.mean(dng * ng, axis=-1, keepdims=True)))
        dhg = jnp.concatenate(parts, axis=1)
        dy_ref[...] = dhg * (zv * sz)
        dz_ref[...] = (dhg * yv * (sz * (1.0 + zv * (1.0 - sz)))).astype(bf16)

    return pl.pallas_call(
        body, name="outproj_bwd", grid=(L // tm,),
        in_specs=[_rows(tm, D_MODEL)] * 4 + [_full((8, D_MODEL)), _full((1, D_MODEL)), _wall_spec(512, R_OUT)],
        out_specs=[_rows(tm, D_MODEL)] * 4 + [_full((8, D_MODEL))],
        out_shape=[jax.ShapeDtypeStruct((L, D_MODEL), f32)] + [jax.ShapeDtypeStruct((L, D_MODEL), bf16)] * 3
        + [jax.ShapeDtypeStruct((8, D_MODEL), f32)],
        compiler_params=_cparams("arbitrary"),
    )(dr1, mix, y, z, mod, norm_w, w_out)


def _attn_bwd(q, kv, do, lse, sinks):
    L = q.shape[0]
    nb = L // CHUNK
    T = CHUNK

    def body(sink_ref, q_ref, kvp_ref, kvc_ref, do_ref, lse_ref, dq_ref, dkv_ref, dsink_ref, carry):
        n = pl.program_id(0)

        @pl.when(n == 0)
        def _():
            carry[...] = jnp.zeros_like(carry)
            dsink_ref[...] = jnp.zeros_like(dsink_ref)

        @pl.when(n < nb)
        def _():
            first = n == 0
            ext = _kv_halves(kvp_ref[...], kvc_ref[...], first)
            dist, valid = _attn_masks(first)
            lane1 = lax.broadcasted_iota(jnp.int32, (1, 128), 1)
            lse = lse_ref[...]
            acck = [[None, None], [None, None]]
            accv = [[None, None], [None, None]]
            dsink = jnp.zeros((1, 128), f32)
            dq_blocks = []
            for blk in range(8):
                qb = q_ref[:, 128 * blk:128 * (blk + 1)]
                dob = do_ref[:, 128 * blk:128 * (blk + 1)]
                dq_acc = None
                for half in range(2):
                    h = 2 * blk + half
                    g = h // 8
                    k_ext, v_ext = ext[g][half]
                    s = _dot_nt(qb, k_ext) * ATTN_SCALE - ALIBI_SLOPES[h] * dist
                    lse_h = lse[:, h:h + 1]
                    p = jnp.where(valid, jnp.exp(s - lse_h), 0.0)
                    dp = _dot_nt(dob, v_ext)
                    delta = jnp.sum(p * dp, axis=-1, keepdims=True)
                    ds = (p * (dp - delta) * ATTN_SCALE).astype(bf16)
                    psink = jnp.exp(sink_ref[h] - lse_h)
                    dsink = dsink - jnp.where(lane1 == h, jnp.sum(psink * delta, axis=0, keepdims=True), 0.0)
                    dqh = _dot(ds, k_ext)
                    dq_acc = dqh if dq_acc is None else dq_acc + dqh
                    dkh = _dot_tn(ds, qb)
                    dvh = _dot_tn(p.astype(bf16), dob)
                    acck[g][half] = dkh if acck[g][half] is None else acck[g][half] + dkh
                    accv[g][half] = dvh if accv[g][half] is None else accv[g][half] + dvh
                dq_blocks.append(dq_acc.astype(bf16))
            dq_ref[...] = jnp.concatenate(dq_blocks, axis=1)
            dsink_ref[...] += dsink
            lo = lax.broadcasted_iota(jnp.int32, (2 * T, 128), 1) < HEAD_DIM

            def fold(acc):
                return jnp.where(lo, acc[0][0] + pltpu.roll(acc[0][1], HEAD_DIM, 1),
                                 acc[1][1] + pltpu.roll(acc[1][0], HEAD_DIM, 1))

            dkv = jnp.concatenate([fold(acck), fold(accv)], axis=1)
            dkv_ref[...] = (carry[...] + dkv[0:T, :]).astype(bf16)
            carry[...] = dkv[T:2 * T, :]

        @pl.when(n == nb)
        def _():
            dkv_ref[...] = carry[...].astype(bf16)

    cur = lambda i: (jnp.minimum(i, nb - 1), 0)
    return pl.pallas_call(
        body, name="attn_bwd", grid=(nb + 1,),
        in_specs=[pl.BlockSpec(memory_space=pltpu.SMEM), pl.BlockSpec((T, D_MODEL), cur),
                  pl.BlockSpec((T, 256), lambda i: (jnp.maximum(jnp.minimum(i, nb - 1) - 1, 0), 0)),
                  pl.BlockSpec((T, 256), cur), pl.BlockSpec((T, D_MODEL), cur), pl.BlockSpec((T, 128), cur)],
        out_specs=[pl.BlockSpec((T, D_MODEL), cur), pl.BlockSpec((T, 256), lambda i: (jnp.maximum(i - 1, 0), 0)),
                   _full((1, 128))],
        out_shape=[jax.ShapeDtypeStruct((L, D_MODEL), bf16), jax.ShapeDtypeStruct((L, 256), bf16),
                   jax.ShapeDtypeStruct((1, 128), f32)],
        scratch_shapes=[pltpu.VMEM((T, 256), f32)],
        compiler_params=_cparams("arbitrary"),
    )(sinks, q, kv, kv, do, lse)


def _ssd_bwd(xc, dt_raw, dy, prev_all, dt_bias, a_log, d_skip_x, e_mat):
    L = xc.shape[0]
    nc = L // CHUNK
    T = CHUNK

    def body(xc_ref, dt_ref, dy_ref, prev_ref, dtb_ref, al_ref, dsk_ref, e_ref,
             dxc_ref, ddt_ref, acc_ref, dd_ref, dst, dxs_s):
        @pl.when(pl.program_id(0) == 0)
        def _():
            dst[...] = jnp.zeros_like(dst)
            acc_ref[...] = jnp.zeros_like(acc_ref)
            dd_ref[...] = jnp.zeros_like(dd_ref)

        lane1 = lax.broadcasted_iota(jnp.int32, (1, 128), 1)
        a_row = jnp.where(lane1 < SSD_HEADS, -jnp.exp(al_ref[...]), 0.0)
        e_mat_v = e_ref[...]
        dt, a, cum, dtx, cumx, r, c = _ssd_chunk_common(dt_ref[...], dtb_ref[...], a_row, e_mat_v)
        cum_t = cum.T
        ex = jnp.exp(cumx)
        last = cumx[T - 1:T, :]
        wx = jnp.exp(last - cumx)
        cdx = jnp.exp(last)
        xs = xc_ref[:, 0:SSD_WIDTH]
        X = xs * dtx
        Xb = X.astype(bf16)
        Xdb = (X * wx).astype(bf16)
        dyv = dy_ref[...]
        prev = prev_ref[0]
        prevb = prev.astype(bf16)
        dnew = dst[...]
        dnewb = dnew.astype(bf16)
        tri = c <= r
        lane = lax.broadcasted_iota(jnp.int32, (T, 128), 1)
        sub = lax.broadcasted_iota(jnp.int32, (128, T), 0)
        lo = lane < HEAD_DIM

        def red(vals, g):
            return lax.dot_general(vals, e_mat_v[:, 512 * g:512 * (g + 1)], (((1,), (1,)), ((), ())),
                                   preferred_element_type=f32, precision=lax.Precision.HIGHEST)

        de = jnp.zeros((T, 128), f32)
        dw = jnp.zeros((T, 128), f32)
        ddt_x = jnp.zeros((T, 128), f32)
        dcum_col = jnp.zeros((T, 128), f32)
        dcum_row = jnp.zeros((128, T), f32)
        dprev_parts, dBs, dCs = [], [], []
        for g in range(SSD_GROUPS):
            s5 = slice(512 * g, 512 * (g + 1))
            Bg = xc_ref[:, 1024 + 128 * g:1152 + 128 * g].astype(bf16)
            Cg = xc_ref[:, 1280 + 128 * g:1408 + 128 * g].astype(bf16)
            G = _dot_nt(Cg, Bg)
            Z = _dot(Cg, prevb[:, s5])
            dyg = dyv[:, s5]
            dZb = (dyg * ex[:, s5]).astype(bf16)
            dXd = _dot(Bg, dnewb[:, s5])
            dC = _dot_nt(dZb, prevb[:, s5])
            dB = _dot_nt(Xdb[:, s5], dnewb[:, s5])
            dprev_parts.append(_dot_tn(Cg, dZb) + dnew[:, s5] * cdx[:, s5])
            de = de + red(dyg * Z, g)
            dw = dw + red(dXd * X[:, s5], g)
            dXg = dXd * wx[:, s5]
            dG = jnp.zeros((T, T), f32)
            for j in range(4):
                blk = 4 * g + j
                sl = slice(128 * blk, 128 * (blk + 1))
                Xblk = Xb[:, sl]
                dyblk = dyv[:, sl]
                dyblk_b = dyblk.astype(bf16)
                dxh = []
                for half in range(2):
                    h = 2 * blk + half
                    seg = jnp.minimum(cum[:, h:h + 1] - cum_t[h:h + 1, :], 0.0)
                    Lm = jnp.where(tri, jnp.exp(seg), 0.0)
                    M = G * Lm
                    dyh = jnp.where(lo if half == 0 else ~lo, dyblk, 0.0).astype(bf16)
                    dM = _dot_nt(dyh, Xblk)
                    dG = dG + dM * Lm
                    Q = dM * M
                    dcum_col = dcum_col + jnp.where(lane == h, jnp.sum(Q, axis=1, keepdims=True), 0.0)
                    dcum_row = dcum_row + jnp.where(sub == h, jnp.sum(Q, axis=0, keepdims=True), 0.0)
                    dxh.append(_dot_tn(M.astype(bf16), dyblk_b))
                dXblk = dXg[:, 128 * j:128 * (j + 1)] + jnp.where(lo, dxh[0], dxh[1])
                xsb = xs[:, sl]
                dxs_s[:, sl] = dXblk * dtx[:, sl] + dsk_ref[:, sl] * dyblk
                ddt_x = ddt_x + lax.dot_general(dXblk * xsb, e_mat_v[:, sl], (((1,), (1,)), ((), ())),
                                                preferred_element_type=f32, precision=lax.Precision.HIGHEST)
                dd_ref[:, sl] += jnp.sum(dyblk * xsb, axis=0, keepdims=True)
            dGb = dG.astype(bf16)
            dCs.append(dC + _dot(dGb, Bg))
            dBs.append(dB + _dot_tn(dGb, Cg))
        e16 = jnp.exp(cum)
        cum_last = cum[T - 1:T, :]
        w16 = jnp.exp(cum_last - cum)
        dcd = jnp.sum(dnew * prev, axis=0, keepdims=True)
        dcd16 = red(dcd[:, 0:512], 0) + red(dcd[:, 512:1024], 1)
        dww = dw * w16
        extra = jnp.sum(dww, axis=0, keepdims=True) + dcd16 * jnp.exp(cum_last)
        rowi = lax.broadcasted_iota(jnp.int32, (T, 128), 0)
        dcum = dcum_col - dcum_row.T + de * e16 - dww + jnp.where(rowi == T - 1, extra, 0.0)
        triu = (c >= r).astype(f32)
        da = _dot_exact(triu, dcum)
        ddt = ddt_x + da * a_row
        acc_ref[0:1, :] += jnp.sum(da * dt, axis=0, keepdims=True)
        ddt_raw = jnp.where(lane < SSD_HEADS, ddt * _sigmoid(dt_ref[...] + dtb_ref[...]), 0.0)
        ddt_ref[...] = ddt_raw
        acc_ref[1:2, :] += jnp.sum(ddt_raw, axis=0, keepdims=True)
        dxc_ref[:, 0:SSD_WIDTH] = dxs_s[...]
        dxc_ref[:, 1024:1280] = jnp.concatenate(dBs, axis=1)
        dxc_ref[:, 1280:1536] = jnp.concatenate(dCs, axis=1)
        dst[...] = jnp.concatenate(dprev_parts, axis=1)

    rev = lambda i: (nc - 1 - i, 0)
    return pl.pallas_call(
        body, name="ssd_bwd", grid=(nc,),
        in_specs=[pl.BlockSpec((T, CONV_DIM), rev), pl.BlockSpec((T, 128), rev), pl.BlockSpec((T, SSD_WIDTH), rev),
                  pl.BlockSpec((1, SSD_STATE, SSD_WIDTH), lambda i: (nc - 1 - i, 0, 0)),
                  _full((1, 128)), _full((1, 128)), _full((1, SSD_WIDTH)), _full((128, SSD_WIDTH))],
        out_specs=[pl.BlockSpec((T, CONV_DIM), rev), pl.BlockSpec((T, 128), rev), _full((8, 128)),
                   _full((1, SSD_WIDTH))],
        out_shape=[jax.ShapeDtypeStruct((L, CONV_DIM), f32), jax.ShapeDtypeStruct((L, 128), f32),
                   jax.ShapeDtypeStruct((8, 128), f32), jax.ShapeDtypeStruct((1, SSD_WIDTH), f32)],
        scratch_shapes=[pltpu.VMEM((SSD_STATE, SSD_WIDTH), f32), pltpu.VMEM((T, SSD_WIDTH), f32)],
        compiler_params=_cparams("arbitrary"),
    )(xc, dt_raw, dy, prev_all, dt_bias, a_log, d_skip_x, e_mat)


def _conv_bwd(dxc, xr, conv_w, conv_b):
    L = dxc.shape[0]
    tm = 256
    nt = L // tm

    def body(dxc_ref, xr_ref, xh_ref, cw_ref, cb_ref, dxr_ref, acc_ref, carry, buf, buf2):
        i = pl.program_id(0)

        @pl.when(i == 0)
        def _():
            carry[...] = jnp.zeros_like(carry)
            acc_ref[...] = jnp.zeros_like(acc_ref)

        buf[0:8, :] = jnp.where(i == nt - 1, 0.0, xh_ref[...])
        buf[8:8 + tm, :] = xr_ref[...]
        pre = cb_ref[...] + cw_ref[0:1, :] * buf[5:5 + tm, :]
        for k in range(1, CONV_K):
            pre = pre + cw_ref[k:k + 1, :] * buf[5 + k:5 + k + tm, :]
        sg = _sigmoid(pre)
        dpre = dxc_ref[...] * (sg * (1.0 + pre * (1.0 - sg)))
        acc_ref[4:5, :] += jnp.sum(dpre, axis=0, keepdims=True)
        for k in range(CONV_K):
            acc_ref[k:k + 1, :] += jnp.sum(dpre * buf[5 + k:5 + k + tm, :], axis=0, keepdims=True)
        buf2[0:tm, :] = dpre
        buf2[tm:tm + 8, :] = carry[...]
        du = cw_ref[0:1, :] * buf2[3:3 + tm, :]
        for k in range(1, CONV_K):
            du = du + cw_ref[k:k + 1, :] * buf2[3 - k:3 - k + tm, :]
        dxr_ref[...] = du.astype(bf16)
        carry[...] = dpre[0:8, :]

    rev = lambda i: (nt - 1 - i, 0)
    return pl.pallas_call(
        body, name="conv_bwd", grid=(nt,),
        in_specs=[pl.BlockSpec((tm, CONV_DIM), rev), pl.BlockSpec((tm, CONV_DIM), rev),
                  pl.BlockSpec((8, CONV_DIM), lambda i: (jnp.maximum((nt - 1 - i) * (tm // 8) - 1, 0), 0)),
                  _full((CONV_K, CONV_DIM)), _full((1, CONV_DIM))],
        out_specs=[pl.BlockSpec((tm, CONV_DIM), rev), _full((8, CONV_DIM))],
        out_shape=[jax.ShapeDtypeStruct((L, CONV_DIM), bf16), jax.ShapeDtypeStruct((8, CONV_DIM), f32)],
        scratch_shapes=[pltpu.VMEM((8, CONV_DIM), f32), pltpu.VMEM((tm + 8, CONV_DIM), f32),
                        pltpu.VMEM((tm + 8, CONV_DIM), f32)],
        compiler_params=_cparams("arbitrary"),
    )(dxc, xr, xr, conv_w, conv_b)


def _inproj_bwd(dz, dxr, dq, dkv, ddt, dr1, x, mod, ln_g, ln_b, w_in):
    L = x.shape[0]
    tm = 256

    def body(dz_ref, dxr_ref, dq_ref, dkv_ref, ddt_ref, dr1_ref, x_ref, mod_ref, g_ref, b_ref, w_ref, dx_ref, acc_ref):
        @pl.when(pl.program_id(0) == 0)
        def _():
            acc_ref[...] = jnp.zeros_like(acc_ref)

        du1 = (_dot_nt(dz_ref[...], w_ref[:, C_Z:C_XBC]) + _dot_nt(dxr_ref[...], w_ref[:, C_XBC:C_Q])
               + _dot_nt(dq_ref[...], w_ref[:, C_Q:C_KV]) + _dot_nt(dkv_ref[...], w_ref[:, C_KV:C_DT])
               + _dot_nt(ddt_ref[...].astype(bf16), w_ref[:, C_DT:C_END]))
        xhat, rstd = _ln_stats(x_ref[...])
        h0 = xhat * g_ref[...] + b_ref[...]
        acc_ref[0:1, :] += jnp.sum(du1 * h0, axis=0, keepdims=True)
        acc_ref[1:2, :] += jnp.sum(du1, axis=0, keepdims=True)
        dh0 = du1 * (1.0 + mod_ref[1:2, :]) + ALPHA * dr1_ref[...]
        acc_ref[2:3, :] += jnp.sum(dh0 * xhat, axis=0, keepdims=True)
        acc_ref[3:4, :] += jnp.sum(dh0, axis=0, keepdims=True)
        dx_ref[...] = _ln_bwd(dh0, xhat, rstd, g_ref[...])

    v = _full((1, D_MODEL))
    return pl.pallas_call(
        body, name="inproj_bwd", grid=(L // tm,),
        in_specs=[_rows(tm, D_MODEL), _rows(tm, CONV_DIM), _rows(tm, D_MODEL), _rows(tm, 256), _rows(tm, 128),
                  _rows(tm, D_MODEL), _rows(tm, D_MODEL), _full((8, D_MODEL)), v, v, _resident((D_MODEL, C_END))],
        out_specs=[_rows(tm, D_MODEL), _full((8, D_MODEL))],
        out_shape=[jax.ShapeDtypeStruct((L, D_MODEL), f32), jax.ShapeDtypeStruct((8, D_MODEL), f32)],
        compiler_params=_cparams("arbitrary"),
    )(dz, dxr, dq, dkv, ddt, dr1, x, mod, ln_g, ln_b, w_in)


def _adamw_math(w, g, m, v):
    m = ADAM_B1 * m + (1.0 - ADAM_B1) * g
    v = ADAM_B2 * v + (1.0 - ADAM_B2) * (g * g)
    m_hat = m / (1.0 - ADAM_B1 ** ADAM_STEP)
    v_hat = v / (1.0 - ADAM_B2 ** ADAM_STEP)
    delta = -ADAM_LR * (m_hat / (jnp.sqrt(v_hat) + ADAM_EPS) + ADAM_WD * w)
    return delta, m, v


def _adamw(w, g, m, v, name):
    R, C = w.shape
    tr = R if R <= 256 else 256

    def body(w_ref, g_ref, m_ref, v_ref, d_ref, m2_ref, v2_ref):
        d_ref[...], m2_ref[...], v2_ref[...] = _adamw_math(w_ref[...], g_ref[...], m_ref[...], v_ref[...])

    spec = pl.BlockSpec((tr, C), lambda i: (i, 0))
    return pl.pallas_call(
        body, name=name, grid=(R // tr,), in_specs=[spec] * 4, out_specs=[spec] * 3,
        out_shape=[jax.ShapeDtypeStruct((R, C), f32)] * 3, compiler_params=_cparams("parallel"),
    )(w, g, m, v)


ADA_COLS = 6 * D_MODEL // N_CHIPS
ADA_TN = 512


def _ada_fwd(c_all, ada_w, ada_b):
    def body(c_ref, w_ref, b_ref, o_ref):
        cv = c_ref[...]
        o_ref[...] = _dot_exact(cv * _sigmoid(cv), w_ref[...]) + b_ref[...]

    return pl.pallas_call(
        body, name="ada_fwd", grid=(ADA_COLS // ADA_TN,),
        in_specs=[_full((N_DEV, D_MODEL)), pl.BlockSpec((D_MODEL, ADA_TN), lambda j: (0, j)),
                  pl.BlockSpec((1, ADA_TN), lambda j: (0, j))],
        out_specs=pl.BlockSpec((N_DEV, ADA_TN), lambda j: (0, j)),
        out_shape=jax.ShapeDtypeStruct((N_DEV, ADA_COLS), f32), compiler_params=_cparams("parallel"),
    )(c_all, ada_w, ada_b)


def _ada_bwd(c_all, dmod, w, m, v):
    def body(c_ref, d_ref, w_ref, m_ref, v_ref, g_ref, dl_ref, m2_ref, v2_ref):
        cv = c_ref[...]
        g = lax.dot_general(cv * _sigmoid(cv), d_ref[...], (((0,), (0,)), ((), ())), preferred_element_type=f32,
                            precision=lax.Precision.HIGHEST)
        g_ref[...] = g
        dl_ref[...], m2_ref[...], v2_ref[...] = _adamw_math(w_ref[...], g, m_ref[...], v_ref[...])

    wspec = pl.BlockSpec((D_MODEL, ADA_TN), lambda j: (0, j))
    return pl.pallas_call(
        body, name="ada_bwd", grid=(ADA_COLS // ADA_TN,),
        in_specs=[_full((N_DEV, D_MODEL)), pl.BlockSpec((N_DEV, ADA_TN), lambda j: (0, j)), wspec, wspec, wspec],
        out_specs=[wspec] * 4, out_shape=[jax.ShapeDtypeStruct((D_MODEL, ADA_COLS), f32)] * 4,
        compiler_params=_cparams("parallel"),
    )(c_all, dmod, w, m, v)


def _small_update(gathered, w, m, v):
    n = w.shape[1]

    def body(g_ref, w_ref, m_ref, v_ref, gs_ref, d_ref, m2_ref, v2_ref):
        g = g_ref[0:1, :]
        for i in range(1, N_DEV):
            g = g + g_ref[i:i + 1, :]
        gs_ref[...] = g
        d_ref[...], m2_ref[...], v2_ref[...] = _adamw_math(w_ref[...], g, m_ref[...], v_ref[...])

    return pl.pallas_call(body, name="small_update", out_shape=[jax.ShapeDtypeStruct((1, n), f32)] * 4,
                          compiler_params=_cparams())(gathered, w, m, v)


def _place():
    return lax.axis_index("x"), lax.axis_index("y"), lax.axis_index("c")


def _flip(x, y, c, m):
    return (1 - x if m & 4 else x, 1 - y if m & 2 else y, 1 - c if m & 1 else c)


_VMEM_SPEC = pl.BlockSpec(memory_space=pltpu.VMEM)
_ANY_SPEC = pl.BlockSpec(memory_space=pl.ANY)


def _allgather8(v, name):
    n = v.shape[1]

    def body(v_ref, out_ref, send_sems, recv_sems, local_sem):
        x, y, c = _place()

        def rows(px, py, pc):
            return out_ref.at[pl.ds(pl.multiple_of((4 * px + 2 * py + pc) * 8, 8), 8), :]

        def copy(m, src, dst, to):
            return pltpu.make_async_remote_copy(src_ref=src, dst_ref=dst, send_sem=send_sems.at[m - 1],
                                                recv_sem=recv_sems.at[m - 1], device_id=to, device_id_type=MESH)

        mine = pltpu.make_async_copy(v_ref, rows(x, y, c), local_sem)
        mine.start()
        sends = [copy(m, v_ref, rows(x, y, c), _flip(x, y, c, m)) for m in range(1, N_DEV)]
        for cp in sends:
            cp.start()
        for m in range(1, N_DEV):
            peer = _flip(x, y, c, m)
            copy(m, v_ref, rows(*peer), peer).wait_recv()
        for cp in sends:
            cp.wait_send()
        mine.wait()

    return pl.pallas_call(
        body, name=name, out_shape=jax.ShapeDtypeStruct((8 * N_DEV, n), f32), in_specs=[_VMEM_SPEC],
        out_specs=_VMEM_SPEC,
        scratch_shapes=[pltpu.SemaphoreType.DMA((N_DEV - 1,)), pltpu.SemaphoreType.DMA((N_DEV - 1,)),
                        pltpu.SemaphoreType.DMA],
    )(v)


def _mod_exchange(mod_all):
    def body(src_ref, out_ref, send_sems, recv_sems, local_sem):
        x, y, c = _place()

        def copy(m, src, dst, to):
            return pltpu.make_async_remote_copy(src_ref=src, dst_ref=dst, send_sem=send_sems.at[m - 1],
                                                recv_sem=recv_sems.at[m - 1], device_id=to, device_id_type=MESH)

        mine = pltpu.make_async_copy(src_ref.at[4 * x + 2 * y + c], out_ref.at[2 * x + y], local_sem)
        mine.start()
        sends = []
        for m in range(1, N_CHIPS):
            px, py, pc = _flip(x, y, c, 2 * m)
            sends.append(copy(m, src_ref.at[4 * px + 2 * py + pc], out_ref.at[2 * x + y], (px, py, pc)))
        for cp in sends:
            cp.start()
        for m in range(1, N_CHIPS):
            px, py, pc = _flip(x, y, c, 2 * m)
            copy(m, src_ref.at[0], out_ref.at[2 * px + py], (px, py, pc)).wait_recv()
        for cp in sends:
            cp.wait_send()
        mine.wait()

    return pl.pallas_call(
        body, name="mod_exchange", out_shape=jax.ShapeDtypeStruct((N_CHIPS, 12, 128), f32), in_specs=[_VMEM_SPEC],
        out_specs=_VMEM_SPEC,
        scratch_shapes=[pltpu.SemaphoreType.DMA((N_CHIPS - 1,)), pltpu.SemaphoreType.DMA((N_CHIPS - 1,)),
                        pltpu.SemaphoreType.DMA],
    )(mod_all)


def _gather_weights(blob):
    def body(blob_ref, out_ref, send_sems, recv_sems, local_sem, stage):
        x, y, c = _place()
        sib = (x, y, 1 - c)

        def half(px, py, pc):
            return out_ref.at[2 * px + py, pl.ds(pl.multiple_of(pc * R_HALF, 16), R_HALF), :]

        def copy(k, src, dst, to):
            return pltpu.make_async_remote_copy(src_ref=src, dst_ref=dst, send_sem=send_sems.at[k],
                                                recv_sem=recv_sems.at[k], device_id=to, device_id_type=MESH)

        my_half = blob_ref.at[pl.ds(pl.multiple_of(c * R_HALF, 16), R_HALF), :]
        first = [copy(m - 1, my_half, half(x, y, c), _flip(x, y, c, 2 * m)) for m in range(1, N_CHIPS)]
        for cp in first:
            cp.start()
        for k in range(R_END // STAGE_ROWS):
            rows = pl.ds(STAGE_ROWS * k, STAGE_ROWS)
            cin = pltpu.make_async_copy(blob_ref.at[rows, :], stage, local_sem)
            cin.start()
            cin.wait()
            cout = pltpu.make_async_copy(stage, out_ref.at[2 * x + y, rows, :], local_sem)
            cout.start()
            cout.wait()
        passed = []
        for m in range(1, N_CHIPS):
            px, py, pc = _flip(x, y, c, 2 * m)
            copy(m - 1, my_half, half(px, py, pc), (px, py, pc)).wait_recv()
            fwd = copy(2 + m, half(px, py, pc), half(px, py, pc), sib)
            fwd.start()
            passed.append(fwd)
        for m in range(1, N_CHIPS):
            px, py, pc = _flip(x, y, c, 2 * m)
            copy(2 + m, my_half, half(px, py, 1 - pc), sib).wait_recv()
        for cp in first + passed:
            cp.wait_send()

    return pl.pallas_call(
        body, name="gather_weights", out_shape=jax.ShapeDtypeStruct((N_CHIPS, R_END, D_MODEL), bf16),
        in_specs=[_ANY_SPEC], out_specs=_ANY_SPEC,
        scratch_shapes=[pltpu.SemaphoreType.DMA((6,)), pltpu.SemaphoreType.DMA((6,)), pltpu.SemaphoreType.DMA,
                        pltpu.VMEM((STAGE_ROWS, D_MODEL), bf16)],
    )(blob)


def _rs_to_sibling(gb):
    def body(g_ref, out_ref, send_sems, recv_sems):
        x, y, c = _place()
        cps = [pltpu.make_async_remote_copy(src_ref=g_ref.at[j, 1 - c], dst_ref=out_ref.at[j], send_sem=send_sems.at[j],
                                            recv_sem=recv_sems.at[j], device_id=(x, y, 1 - c), device_id_type=MESH)
               for j in range(N_CHIPS)]
        for cp in cps:
            cp.start()
        for cp in cps:
            cp.wait()

    return pl.pallas_call(
        body, name="rs_to_sibling", out_shape=jax.ShapeDtypeStruct((N_CHIPS, R_HALF, D_MODEL), bf16),
        in_specs=[_ANY_SPEC], out_specs=_ANY_SPEC,
        scratch_shapes=[pltpu.SemaphoreType.DMA((N_CHIPS,)), pltpu.SemaphoreType.DMA((N_CHIPS,))],
    )(gb)


def _rs_to_chips(pb):
    def body(p_ref, out_ref, send_sems, recv_sems):
        x, y, c = _place()
        cps = []
        for m in range(1, N_CHIPS):
            px, py, pc = _flip(x, y, c, 2 * m)
            cps.append(pltpu.make_async_remote_copy(src_ref=p_ref.at[2 * px + py], dst_ref=out_ref.at[m - 1],
                                                    send_sem=send_sems.at[m - 1], recv_sem=recv_sems.at[m - 1],
                                                    device_id=(px, py, pc), device_id_type=MESH))
        for cp in cps:
            cp.start()
        for cp in cps:
            cp.wait()

    return pl.pallas_call(
        body, name="rs_to_chips", out_shape=jax.ShapeDtypeStruct((N_CHIPS - 1, R_HALF, D_MODEL), bf16),
        in_specs=[_ANY_SPEC], out_specs=_ANY_SPEC,
        scratch_shapes=[pltpu.SemaphoreType.DMA((N_CHIPS - 1,)), pltpu.SemaphoreType.DMA((N_CHIPS - 1,))],
    )(pb)


def _rs_share(g):
    def body(g_ref, out_ref, send_sem, recv_sem):
        x, y, c = _place()

        def rows(pc):
            return out_ref.at[pl.ds(pl.multiple_of(pc * R_HALF, 8), R_HALF), :]

        cp = pltpu.make_async_remote_copy(src_ref=rows(c), dst_ref=rows(c), send_sem=send_sem, recv_sem=recv_sem,
                                          device_id=(x, y, 1 - c), device_id_type=MESH)
        cp.start()
        pltpu.make_async_remote_copy(src_ref=rows(c), dst_ref=rows(1 - c), send_sem=send_sem, recv_sem=recv_sem,
                                     device_id=(x, y, 1 - c), device_id_type=MESH).wait_recv()
        cp.wait_send()

    return pl.pallas_call(
        body, name="rs_share", out_shape=jax.ShapeDtypeStruct((R_END, D_MODEL), f32), in_specs=[_ANY_SPEC],
        out_specs=_ANY_SPEC, input_output_aliases={0: 0},
        scratch_shapes=[pltpu.SemaphoreType.DMA, pltpu.SemaphoreType.DMA],
    )(g)


RS_TR = 256


def _rs_sum_pair(place, gf, recv):
    def body(pl_ref, g_ref, r_ref, o_ref):
        o_ref[0] = (g_ref[0, 0].astype(f32) + r_ref[0].astype(f32)).astype(bf16)

    return pl.pallas_call(
        body, name="rs_sum_pair",
        grid_spec=pltpu.PrefetchScalarGridSpec(
            num_scalar_prefetch=1, grid=(N_CHIPS, R_HALF // RS_TR),
            in_specs=[pl.BlockSpec((1, 1, RS_TR, D_MODEL), lambda j, i, p: (j, p[0], i, 0)),
                      pl.BlockSpec((1, RS_TR, D_MODEL), lambda j, i, p: (j, i, 0))],
            out_specs=pl.BlockSpec((1, RS_TR, D_MODEL), lambda j, i, p: (j, i, 0))),
        out_shape=jax.ShapeDtypeStruct((N_CHIPS, R_HALF, D_MODEL), bf16),
        compiler_params=_cparams("parallel", "parallel"),
    )(place, gf, recv)


def _rs_sum_chips(place, gf, recv_sib, recv_chips):
    def body(pl_ref, g_ref, r1_ref, r2_ref, o_ref):
        acc = g_ref[0, 0].astype(f32) + r1_ref[0].astype(f32)
        for k in range(N_CHIPS - 1):
            acc = acc + r2_ref[k].astype(f32)
        o_ref[...] = acc

    nt = R_HALF // RS_TR
    return pl.pallas_call(
        body, name="rs_sum_chips",
        grid_spec=pltpu.PrefetchScalarGridSpec(
            num_scalar_prefetch=1, grid=(nt,),
            in_specs=[pl.BlockSpec((1, 1, RS_TR, D_MODEL), lambda i, p: (p[1], p[0], i, 0)),
                      pl.BlockSpec((1, RS_TR, D_MODEL), lambda i, p: (p[1], i, 0)),
                      pl.BlockSpec((N_CHIPS - 1, RS_TR, D_MODEL), lambda i, p: (0, i, 0))],
            out_specs=pl.BlockSpec((RS_TR, D_MODEL), lambda i, p: (p[0] * nt + i, 0))),
        out_shape=jax.ShapeDtypeStruct((R_END, D_MODEL), f32),
        compiler_params=_cparams("parallel"),
    )(place, gf, recv_sib, recv_chips)


SMALL_SLOTS = (("ada_b", 6144), ("ln_in_g", 1024), ("ln_in_b", 1024), ("conv_b", 1536), ("dt_bias", 128), ("a_log", 128),
               ("d_skip", 128), ("ssd_norm_w", 1024), ("attn_sinks", 128), ("ln1_g", 1024), ("ln1_b", 1024),
               ("b_ff1", 4096), ("b_ff2", 1024), ("ln2_g", 1024), ("ln2_b", 1024), ("conv_w", 6144), ("loss", 1024))
SMALL_N = sum(n for _, n in SMALL_SLOTS)
assert SMALL_N % 1024 == 0


def _pack_small(vals):
    parts = []
    for name, n in SMALL_SLOTS:
        v = vals.get(name)
        v = jnp.zeros((n,), f32) if v is None else v.reshape(-1).astype(f32)
        parts.append(jnp.pad(v, (0, n - v.shape[0])))
    return jnp.concatenate(parts)


def _unpack_small(vec):
    out, off = {}, 0
    for name, n in SMALL_SLOTS:
        out[name] = vec[off:off + n]
        off += n
    return out


def _pad128(v):
    v = v.reshape(1, -1)
    return jnp.pad(v, ((0, 0), (0, 128 - v.shape[1])))


def _row(v):
    return v.reshape(1, -1)


def _w_in_layout(w):
    return jnp.concatenate([w[:, :2560], w[:, 2576:3856], w[:, 2560:2576], jnp.zeros((D_MODEL, C_END - PROJ_WIDTH), w.dtype)],
                           axis=1)


def _w_in_unlayout(g):
    return jnp.concatenate([g[:, :2560], g[:, C_DT:C_DT + 16], g[:, 2560:C_DT]], axis=1)


def kernel(x, c, ln_in_g, ln_in_b, ada_w, ada_b, w_in, conv_w, conv_b, dt_bias, a_log, d_skip, ssd_norm_w, attn_sinks, w_out, ln1_g, ln1_b, w_ff1, b_ff1, w_ff2, b_ff2, ln2_g, ln2_b, loss_target, m_ln_in_g, m_ln_in_b, m_ada_w, m_ada_b, m_w_in, m_conv_w, m_conv_b, m_dt_bias, m_a_log, m_d_skip, m_ssd_norm_w, m_attn_sinks, m_w_out, m_ln1_g, m_ln1_b, m_w_ff1, m_b_ff1, m_w_ff2, m_b_ff2, m_ln2_g, m_ln2_b, v_ln_in_g, v_ln_in_b, v_ada_w, v_ada_b, v_w_in, v_conv_w, v_conv_b, v_dt_bias, v_a_log, v_d_skip, v_ssd_norm_w, v_attn_sinks, v_w_out, v_ln1_g, v_ln1_b, v_w_ff1, v_b_ff1, v_w_ff2, v_b_ff2, v_ln2_g, v_ln2_b):
    xi, yi, ci = _place()
    chip = 2 * xi + yi
    place = jnp.stack([ci, chip]).astype(jnp.int32)
    x2, tgt = x[0], loss_target[0]
    w_shard_cols = PROJ_WIDTH // N_CHIPS

    cond = jnp.concatenate([c.reshape(-1), conv_w.reshape(-1), jnp.zeros((512,), f32)]).reshape(8, 384)
    cond_all = _allgather8(cond, "gather_cond").reshape(N_DEV, 3072)
    c_all = cond_all[:, :D_MODEL]
    conv_w_full = jnp.concatenate([cond_all[2 * j, D_MODEL:D_MODEL + 1536].reshape(CONV_K, 384) for j in range(N_CHIPS)], axis=1)

    ada_b_mine = lax.dynamic_slice(ada_b, (0, chip * ADA_COLS), (1, ADA_COLS))
    mod_all = _ada_fwd(c_all, ada_w[0], ada_b_mine)
    mod_mine = _mod_exchange(mod_all.reshape(N_DEV, 12, 128)).reshape(6, D_MODEL)
    mod = jnp.concatenate([mod_mine, jnp.zeros((2, D_MODEL), f32)], axis=0)

    blob = jnp.concatenate([jnp.pad(w_in[0], ((0, 0), (0, D_MODEL - w_shard_cols))), w_ff1[0], w_ff2[0], w_out[0]],
                           axis=0).astype(bf16)
    wall = _gather_weights(blob)
    w_in_f = _w_in_layout(jnp.transpose(wall[:, R_IN:R_FF1, :w_shard_cols], (1, 0, 2)).reshape(D_MODEL, PROJ_WIDTH))

    e_mat = _head_expand()
    dsk_x = jnp.repeat(d_skip[0], HEAD_DIM).reshape(1, SSD_WIDTH)
    dtb, alog = _pad128(dt_bias), _pad128(a_log)
    sinks = attn_sinks[0]
    lng, lnb = _row(ln_in_g), _row(ln_in_b)
    u1, z, xr, xc, q, kv, dtr = _inproj_fwd(x2, mod, lng, lnb, w_in_f, conv_w_full, conv_b)
    y, prev_all = _ssd_fwd(xc, dtr, dtb, alog, dsk_x, e_mat)
    o, lse = _attn_fwd(q, kv, sinks)
    yn, mix, r1 = _outproj_fwd(y, z, o, x2, mod, lng, lnb, ssd_norm_w, wall)

    dr1, u2, s_act, da, df, acc_mlp, db1 = _mlp_fwd_bwd(r1, tgt, mod, ln1_g, ln1_b, ln2_g, ln2_b, wall, b_ff1, wall, b_ff2)
    gb = jnp.zeros((N_CHIPS, R_END, D_MODEL), bf16)
    gb = _wgrad_blob(gb, u2, da, "wgrad_ff1", lambda t, n: (n, R_FF1 // WG_TM + t))
    gb = _wgrad_blob(gb, s_act, df, "wgrad_ff2", lambda t, n: (t // 2, R_FF2 // WG_TM + t % 2))
    dy, dz, do, dmix, acc_out = _outproj_bwd(dr1, mix, y, z, mod, ssd_norm_w, wall)
    gb = _wgrad_blob(gb, yn, dmix, "wgrad_out_y", lambda t, n: (t, R_OUT // WG_TM))
    gb = _wgrad_blob(gb, o, dmix, "wgrad_out_o", lambda t, n: (2 + t, R_OUT // WG_TM))
    dq, dkv, dsink = _attn_bwd(q, kv, do, lse, sinks)
    dxc, ddt, acc_ssd, dd_x = _ssd_bwd(xc, dtr, dy, prev_all, dtb, alog, dsk_x, e_mat)
    dxr, acc_conv = _conv_bwd(dxc, xr, conv_w_full, conv_b)
    grad_x, acc_in = _inproj_bwd(dz, dxr, dq, dkv, ddt, dr1, x2, mod, lng, lnb, w_in_f)
    g_in = jnp.concatenate([_wgrad(u1, dz, "wgrad_in_z"), _wgrad(u1, dxr, "wgrad_in_xbc"), _wgrad(u1, dq, "wgrad_in_q"),
                            _wgrad(u1, dkv, "wgrad_in_kv"), _wgrad(u1, ddt.astype(bf16), "wgrad_in_dt")], axis=1)

    dmod = jnp.concatenate([acc_in[1], acc_in[0], acc_out[0], acc_mlp[A_SH2], acc_mlp[A_SC2], acc_mlp[A_G2]])
    small = {
        "ada_b": dmod, "ln_in_g": acc_in[2], "ln_in_b": acc_in[3], "conv_b": acc_conv[4], "dt_bias": acc_ssd[1, :16],
        "a_log": acc_ssd[0, :16] * (-jnp.exp(a_log[0])), "d_skip": jnp.sum(dd_x.reshape(SSD_HEADS, HEAD_DIM), axis=1),
        "ssd_norm_w": acc_out[1], "attn_sinks": dsink[0, :16], "ln1_g": acc_mlp[A_LN1G], "ln1_b": acc_mlp[A_LN1B],
        "b_ff1": db1[0], "b_ff2": acc_mlp[A_B2], "ln2_g": acc_mlp[A_LN2G], "ln2_b": acc_mlp[A_LN2B],
        "conv_w": acc_conv[0:CONV_K], "loss": acc_mlp[A_LOSS],
    }
    small_all = _allgather8(_pack_small(small).reshape(8, SMALL_N // 8), "gather_small").reshape(N_DEV, SMALL_N)
    params = dict(ada_b=ada_b, ln_in_g=ln_in_g, ln_in_b=ln_in_b, conv_b=conv_b, dt_bias=dt_bias, a_log=a_log, d_skip=d_skip,
                  ssd_norm_w=ssd_norm_w, attn_sinks=attn_sinks, ln1_g=ln1_g, ln1_b=ln1_b, b_ff1=b_ff1, b_ff2=b_ff2,
                  ln2_g=ln2_g, ln2_b=ln2_b)
    moms = dict(ada_b=m_ada_b, ln_in_g=m_ln_in_g, ln_in_b=m_ln_in_b, conv_b=m_conv_b, dt_bias=m_dt_bias, a_log=m_a_log,
                d_skip=m_d_skip, ssd_norm_w=m_ssd_norm_w, attn_sinks=m_attn_sinks, ln1_g=m_ln1_g, ln1_b=m_ln1_b,
                b_ff1=m_b_ff1, b_ff2=m_b_ff2, ln2_g=m_ln2_g, ln2_b=m_ln2_b)
    vels = dict(ada_b=v_ada_b, ln_in_g=v_ln_in_g, ln_in_b=v_ln_in_b, conv_b=v_conv_b, dt_bias=v_dt_bias, a_log=v_a_log,
                d_skip=v_d_skip, ssd_norm_w=v_ssd_norm_w, attn_sinks=v_attn_sinks, ln1_g=v_ln1_g, ln1_b=v_ln1_b,
                b_ff1=v_b_ff1, b_ff2=v_b_ff2, ln2_g=v_ln2_g, ln2_b=v_ln2_b)
    gs, dl, m2, v2 = _small_update(small_all, _pack_small(params).reshape(1, -1), _pack_small(moms).reshape(1, -1),
                                   _pack_small(vels).reshape(1, -1))
    gs, dl, m2, v2 = (_unpack_small(t[0]) for t in (gs, dl, m2, v2))
    loss = jnp.sum(gs["loss"])

    dmod_mine = lax.dynamic_slice(small_all[:, :6 * D_MODEL], (0, chip * ADA_COLS), (N_DEV, ADA_COLS))
    g_ada, d_ada, m_ada, v_ada = _ada_bwd(c_all, dmod_mine, ada_w[0], m_ada_w[0], v_ada_w[0])

    g_conv = lax.dynamic_slice(gs["conv_w"].reshape(CONV_K, CONV_DIM), (0, chip * 384), (CONV_K, 384))
    d_conv, m_conv, v_conv = _adamw(conv_w[0], g_conv, m_conv_w[0], v_conv_w[0], "adamw_conv_w")

    g_in_c = jnp.transpose(_w_in_unlayout(g_in).astype(bf16).reshape(D_MODEL, N_CHIPS, w_shard_cols), (1, 0, 2))
    gb = lax.dynamic_update_slice(gb, jnp.pad(g_in_c, ((0, 0), (0, 0), (0, D_MODEL - w_shard_cols))), (0, R_IN, 0))
    gb = gb.reshape(N_CHIPS, 2, R_HALF, D_MODEL)
    from_sib = _rs_to_sibling(gb)
    from_chips = _rs_to_chips(_rs_sum_pair(place, gb, from_sib))
    g_blob = _rs_share(_rs_sum_chips(place, gb, from_sib, from_chips))
    g_w_in = g_blob[R_IN:R_FF1, :w_shard_cols]
    g_w_ff1, g_w_ff2, g_w_out = g_blob[R_FF1:R_FF2], g_blob[R_FF2:R_OUT], g_blob[R_OUT:R_END]
    d_w_in, m_w_in2, v_w_in2 = _adamw(w_in[0], g_w_in, m_w_in[0], v_w_in[0], "adamw_w_in")
    d_w_out, m_w_out2, v_w_out2 = _adamw(w_out[0], g_w_out, m_w_out[0], v_w_out[0], "adamw_w_out")
    d_w_ff1, m_w_ff12, v_w_ff12 = _adamw(w_ff1[0], g_w_ff1, m_w_ff1[0], v_w_ff1[0], "adamw_w_ff1")
    d_w_ff2, m_w_ff22, v_w_ff22 = _adamw(w_ff2[0], g_w_ff2, m_w_ff2[0], v_w_ff2[0], "adamw_w_ff2")

    def small_out(t, name, like):
        return t[name][:like.size].reshape(like.shape)

    def outputs(t, big):
        return [small_out(t, "ln_in_g", ln_in_g), small_out(t, "ln_in_b", ln_in_b), big["ada_w"][None],
                small_out(t, "ada_b", ada_b), big["w_in"][None], big["conv_w"][None], small_out(t, "conv_b", conv_b),
                small_out(t, "dt_bias", dt_bias), small_out(t, "a_log", a_log), small_out(t, "d_skip", d_skip),
                small_out(t, "ssd_norm_w", ssd_norm_w), small_out(t, "attn_sinks", attn_sinks), big["w_out"][None],
                small_out(t, "ln1_g", ln1_g), small_out(t, "ln1_b", ln1_b), big["w_ff1"][None], small_out(t, "b_ff1", b_ff1),
                big["w_ff2"][None], small_out(t, "b_ff2", b_ff2), small_out(t, "ln2_g", ln2_g), small_out(t, "ln2_b", ln2_b)]

    grads = outputs(gs, dict(ada_w=g_ada, w_in=g_w_in, conv_w=g_conv, w_out=g_w_out, w_ff1=g_w_ff1, w_ff2=g_w_ff2))
    deltas = outputs(dl, dict(ada_w=d_ada, w_in=d_w_in, conv_w=d_conv, w_out=d_w_out, w_ff1=d_w_ff1, w_ff2=d_w_ff2))
    new_m = outputs(m2, dict(ada_w=m_ada, w_in=m_w_in2, conv_w=m_conv, w_out=m_w_out2, w_ff1=m_w_ff12, w_ff2=m_w_ff22))
    new_v = outputs(v2, dict(ada_w=v_ada, w_in=v_w_in2, conv_w=v_conv, w_out=v_w_out2, w_ff1=v_w_ff12, w_ff2=v_w_ff22))
    return (loss, grad_x[None], *grads, *deltas, *new_m, *new_v)
```

```python
import functools
import math

import numpy as np
import jax
import jax.numpy as jnp
from jax import lax
from jax.experimental import pallas as pl
from jax.experimental.pallas import tpu as pltpu

f32 = jnp.float32
bf16 = jnp.bfloat16

D_MODEL = 1024
SSD_WIDTH = 1024
SSD_HEADS = 16
HEAD_DIM = 64
SSD_STATE = 128
SSD_GROUPS = 2
CHUNK = 128
CONV_K = 4
CONV_DIM = 1536
ATTN_HEADS = 16
D_FF = 4096
PROJ_WIDTH = 3856
ALPHA = 2.0 ** 0.25
LN_EPS = 1e-5
RMS_EPS = 1e-5
ATTN_SCALE = HEAD_DIM ** -0.5
NEG = -1e30

ADAM_LR = 0.001
ADAM_B1 = 0.9
ADAM_B2 = 0.999
ADAM_EPS = 1e-08
ADAM_WD = 0.01
ADAM_STEP = 10

C_Z, C_XBC, C_Q, C_KV, C_DT, C_END = 0, 1024, 2560, 3584, 3840, 3968
R_FF1, R_FF2, R_OUT = 0, 1024, 2048
GA_ROWS = 2048
GB_ROWS = 1536
WG_TM = 512
STAGE_ROWS = 512
N_CHIPS = 4
N_DEV = 8
VMEM_LIMIT = 56 * 1024 * 1024
MESH = pl.DeviceIdType.MESH

ALIBI_SLOPES = tuple(2.0 ** (-8.0 / ATTN_HEADS * (i + 1)) for i in range(ATTN_HEADS))


def _cparams(*sem):
    return pltpu.CompilerParams(dimension_semantics=sem, vmem_limit_bytes=VMEM_LIMIT)


def _sigmoid(x):
    return 1.0 / (1.0 + jnp.exp(-x))


def _softplus(x):
    return jnp.maximum(x, 0.0) + jnp.log1p(jnp.exp(-jnp.abs(x)))


def _ln_stats(x):
    mu = jnp.mean(x, axis=-1, keepdims=True)
    xc = x - mu
    var = jnp.mean(xc * xc, axis=-1, keepdims=True)
    rstd = lax.rsqrt(var + LN_EPS)
    return xc * rstd, rstd


def _ln_bwd(dy, xhat, rstd, g):
    dxh = dy * g
    m1 = jnp.mean(dxh, axis=-1, keepdims=True)
    m2 = jnp.mean(dxh * xhat, axis=-1, keepdims=True)
    return rstd * (dxh - m1 - xhat * m2)


def _dot(a, b):
    return jnp.dot(a, b, preferred_element_type=f32)


def _dot_nt(a, b):
    return lax.dot_general(a, b, (((1,), (1,)), ((), ())), preferred_element_type=f32)


def _dot_tn(a, b):
    return lax.dot_general(a, b, (((0,), (0,)), ((), ())), preferred_element_type=f32)


def _dot_exact(a, b):
    return jnp.dot(a, b, preferred_element_type=f32, precision=lax.Precision.HIGHEST)


def _full(shape):
    nd = len(shape)
    return pl.BlockSpec(shape, lambda *_: (0,) * nd)


def _resident(shape):
    nd = len(shape)
    return pl.BlockSpec(shape, lambda *_: (0,) * nd, pipeline_mode=pl.Buffered(1))


def _rows(tm, n):
    return pl.BlockSpec((tm, n), lambda i: (i, 0))


def _wall_spec(rows, r0):
    return pl.BlockSpec((N_CHIPS, rows, D_MODEL), lambda *_: (0, r0 // rows, 0), pipeline_mode=pl.Buffered(1))


def _inproj_fwd(x, mod, ln_g, ln_b, w_in_t, conv_w, conv_b, blob):
    L = x.shape[0]
    tm = 256
    nt = L // tm
    R = blob.shape[0]

    def body(x_ref, mod_ref, g_ref, b_ref, w_ref, cw_ref, cb_ref, blob_ref,
             u1_ref, z_ref, xr_ref, xc_ref, q_ref, kv_ref, dt_ref, wall_ref, halo, buf, send_sems, recv_sems):
        start, finish = _gather_job(blob_ref, wall_ref, send_sems, recv_sems, R)

        @pl.when(pl.program_id(0) == 0)
        def _():
            halo[...] = jnp.zeros_like(halo)
            start()

        xhat, _ = _ln_stats(x_ref[...])
        h0 = xhat * g_ref[...] + b_ref[...]
        u1 = (h0 * (1.0 + mod_ref[1:2, :]) + mod_ref[0:1, :]).astype(bf16)
        u1_ref[...] = u1
        z_ref[...] = _dot_nt(u1, w_ref[C_Z:C_XBC, :])
        xr = _dot_nt(u1, w_ref[C_XBC:C_Q, :])
        xr_ref[...] = xr
        q_ref[...] = _dot_nt(u1, w_ref[C_Q:C_KV, :]).astype(bf16)
        kv_ref[...] = _dot_nt(u1, w_ref[C_KV:C_DT, :]).astype(bf16)
        dt_ref[...] = _dot_nt(u1, w_ref[C_DT:C_END, :])
        buf[0:8, :] = halo[...]
        buf[8:8 + tm, :] = xr
        pre = cb_ref[...] + cw_ref[0:1, :] * buf[5:5 + tm, :]
        for k in range(1, CONV_K):
            pre = pre + cw_ref[k:k + 1, :] * buf[5 + k:5 + k + tm, :]
        xc_ref[...] = pre * _sigmoid(pre)
        halo[...] = xr[tm - 8:tm, :]

        @pl.when(pl.program_id(0) == nt - 1)
        def _():
            finish()

    return pl.pallas_call(
        body, name="inproj_fwd", grid=(nt,),
        in_specs=[_rows(tm, D_MODEL), _full((8, D_MODEL)), _full((1, D_MODEL)), _full((1, D_MODEL)),
                  _resident((C_END, D_MODEL)), _full((CONV_K, CONV_DIM)), _full((1, CONV_DIM)), _ANY_SPEC],
        out_specs=[_rows(tm, D_MODEL), _rows(tm, D_MODEL), _rows(tm, CONV_DIM), _rows(tm, CONV_DIM),
                   _rows(tm, D_MODEL), _rows(tm, 256), _rows(tm, 128), _ANY_SPEC],
        out_shape=[jax.ShapeDtypeStruct((L, D_MODEL), bf16), jax.ShapeDtypeStruct((L, D_MODEL), f32),
                   jax.ShapeDtypeStruct((L, CONV_DIM), f32), jax.ShapeDtypeStruct((L, CONV_DIM), f32),
                   jax.ShapeDtypeStruct((L, D_MODEL), bf16), jax.ShapeDtypeStruct((L, 256), bf16),
                   jax.ShapeDtypeStruct((L, 128), f32), jax.ShapeDtypeStruct((N_CHIPS, R, D_MODEL), bf16)],
        scratch_shapes=[pltpu.VMEM((8, CONV_DIM), f32), pltpu.VMEM((tm + 8, CONV_DIM), f32),
                        pltpu.SemaphoreType.DMA((6,)), pltpu.SemaphoreType.DMA((6,))],
        compiler_params=_cparams("arbitrary"),
    )(x, mod, ln_g, ln_b, w_in_t, conv_w, conv_b, blob)


def _head_expand():
    e = np.zeros((128, SSD_WIDTH), np.float32)
    for h in range(SSD_HEADS):
        e[h, h * HEAD_DIM:(h + 1) * HEAD_DIM] = 1.0
    return jnp.asarray(e)


def _ssd_chunk_common(dt_raw, dtb, a_row, e_mat):
    T = CHUNK
    lane = lax.broadcasted_iota(jnp.int32, (T, 128), 1)
    dt = jnp.where(lane < SSD_HEADS, _softplus(dt_raw + dtb), 0.0)
    a = dt * a_row
    r = lax.broadcasted_iota(jnp.int32, (T, T), 0)
    c = lax.broadcasted_iota(jnp.int32, (T, T), 1)
    tril = (c <= r).astype(f32)
    cum = _dot_exact(tril, a)
    dtx = _dot_exact(dt, e_mat)
    cumx = _dot_exact(cum, e_mat)
    return dt, a, cum, dtx, cumx, r, c


def _ssd_fwd(xc, dt_raw, dt_bias, a_log, d_skip_x, e_mat):
    L = xc.shape[0]
    nc = L // CHUNK
    T = CHUNK

    def body(xc_ref, dt_ref, dtb_ref, al_ref, dsk_ref, e_ref, y_ref, prev_ref, st):
        @pl.when(pl.program_id(0) == 0)
        def _():
            st[...] = jnp.zeros_like(st)

        a_row = -jnp.exp(al_ref[...])
        lane1 = lax.broadcasted_iota(jnp.int32, (1, 128), 1)
        a_row = jnp.where(lane1 < SSD_HEADS, a_row, 0.0)
        dt, a, cum, dtx, cumx, r, c = _ssd_chunk_common(dt_ref[...], dtb_ref[...], a_row, e_ref[...])
        cum_t = cum.T
        ex = jnp.exp(cumx)
        last = cumx[T - 1:T, :]
        wx = jnp.exp(last - cumx)
        cdx = jnp.exp(last)
        xs = xc_ref[:, 0:SSD_WIDTH]
        X = xs * dtx
        Xb = X.astype(bf16)
        Xd = (X * wx).astype(bf16)
        prev = st[...]
        prev_ref[0] = prev
        prevb = prev.astype(bf16)
        tri = c <= r
        lane = lax.broadcasted_iota(jnp.int32, (T, 128), 1)
        y_blocks = []
        new_states = []
        for g in range(SSD_GROUPS):
            Bg = xc_ref[:, 1024 + 128 * g:1152 + 128 * g].astype(bf16)
            Cg = xc_ref[:, 1280 + 128 * g:1408 + 128 * g].astype(bf16)
            G = _dot_nt(Cg, Bg)
            yoff = _dot(Cg, prevb[:, 512 * g:512 * (g + 1)])
            new_states.append(_dot_tn(Bg, Xd[:, 512 * g:512 * (g + 1)]))
            for j in range(4):
                blk = 4 * g + j
                Xblk = Xb[:, 128 * blk:128 * (blk + 1)]
                ys = []
                for half in range(2):
                    h = 2 * blk + half
                    seg = jnp.minimum(cum[:, h:h + 1] - cum_t[h:h + 1, :], 0.0)
                    M = jnp.where(tri, G * jnp.exp(seg), 0.0).astype(bf16)
                    ys.append(_dot(M, Xblk))
                yd = jnp.where(lane < HEAD_DIM, ys[0], ys[1])
                sl = slice(128 * blk, 128 * (blk + 1))
                y_blocks.append(yd + ex[:, sl] * yoff[:, 128 * j:128 * (j + 1)] + dsk_ref[:, sl] * xs[:, sl])
        y_ref[...] = jnp.concatenate(y_blocks, axis=1)
        st[...] = prev * cdx + jnp.concatenate(new_states, axis=1)

    return pl.pallas_call(
        body, name="ssd_fwd", grid=(nc,),
        in_specs=[_rows(T, CONV_DIM), _rows(T, 128), _full((1, 128)), _full((1, 128)), _full((1, SSD_WIDTH)),
                  _full((128, SSD_WIDTH))],
        out_specs=[_rows(T, SSD_WIDTH), pl.BlockSpec((1, SSD_STATE, SSD_WIDTH), lambda i: (i, 0, 0))],
        out_shape=[jax.ShapeDtypeStruct((L, SSD_WIDTH), f32), jax.ShapeDtypeStruct((nc, SSD_STATE, SSD_WIDTH), f32)],
        scratch_shapes=[pltpu.VMEM((SSD_STATE, SSD_WIDTH), f32)],
        compiler_params=_cparams("arbitrary"),
    )(xc, dt_raw, dt_bias, a_log, d_skip_x, e_mat)


def _kv_halves(kv_prev, kv_cur, first):
    kv = jnp.concatenate([jnp.where(first, 0.0, kv_prev.astype(f32)), kv_cur.astype(f32)], axis=0)
    lane = lax.broadcasted_iota(jnp.int32, (2 * CHUNK, 128), 1)
    lo = lane < HEAD_DIM
    out = []
    for g in range(2):
        per_half = []
        for half in range(2):
            both = []
            for t in (kv[:, 0:128], kv[:, 128:256]):
                src = t if g == half else pltpu.roll(t, HEAD_DIM, 1)
                both.append(jnp.where(lo if half == 0 else ~lo, src, 0.0).astype(bf16))
            per_half.append(tuple(both))
        out.append(per_half)
    return out


def _attn_masks(first):
    r = lax.broadcasted_iota(jnp.int32, (CHUNK, 2 * CHUNK), 0)
    c = lax.broadcasted_iota(jnp.int32, (CHUNK, 2 * CHUNK), 1)
    dist = r + CHUNK - c
    valid = (dist >= 0) & (dist < CHUNK) & ((c >= CHUNK) | jnp.logical_not(first))
    return dist.astype(f32), valid


def _attn_fwd(q, kv, sinks):
    L = q.shape[0]
    nb = L // CHUNK
    T = CHUNK

    def body(sink_ref, q_ref, kvp_ref, kvc_ref, o_ref, lse_ref):
        first = pl.program_id(0) == 0
        ext = _kv_halves(kvp_ref[...], kvc_ref[...], first)
        dist, valid = _attn_masks(first)
        lane = lax.broadcasted_iota(jnp.int32, (T, 128), 1)
        lse = jnp.zeros((T, 128), f32)
        o_blocks = []
        for blk in range(8):
            qb = q_ref[:, 128 * blk:128 * (blk + 1)]
            acc = None
            for half in range(2):
                h = 2 * blk + half
                k_ext, v_ext = ext[h // 8][half]
                s = _dot_nt(qb, k_ext) * ATTN_SCALE - ALIBI_SLOPES[h] * dist
                s = jnp.where(valid, s, NEG)
                sink = sink_ref[h]
                m = jnp.maximum(jnp.max(s, axis=-1, keepdims=True), sink)
                p = jnp.exp(s - m)
                den = jnp.sum(p, axis=-1, keepdims=True) + jnp.exp(sink - m)
                pn = (p / den).astype(bf16)
                oh = _dot(pn, v_ext)
                acc = oh if acc is None else acc + oh
                lse = jnp.where(lane == h, m + jnp.log(den), lse)
            o_blocks.append(acc.astype(bf16))
        o_ref[...] = jnp.concatenate(o_blocks, axis=1)
        lse_ref[...] = lse

    return pl.pallas_call(
        body, name="attn_fwd", grid=(nb,),
        in_specs=[pl.BlockSpec(memory_space=pltpu.SMEM), _rows(T, D_MODEL),
                  pl.BlockSpec((T, 256), lambda i: (jnp.maximum(i - 1, 0), 0)), _rows(T, 256)],
        out_specs=[_rows(T, D_MODEL), _rows(T, 128)],
        out_shape=[jax.ShapeDtypeStruct((L, D_MODEL), bf16), jax.ShapeDtypeStruct((L, 128), f32)],
        compiler_params=_cparams("arbitrary"),
    )(sinks, q, kv, kv)


def _gated_norm(y, z, w):
    sz = _sigmoid(z)
    hg = y * (z * sz)
    ns, rss = [], []
    for g in range(SSD_GROUPS):
        hs = hg[:, 512 * g:512 * (g + 1)]
        rs = lax.rsqrt(jnp.mean(hs * hs, axis=-1, keepdims=True) + RMS_EPS)
        ns.append(hs * rs)
        rss.append(rs)
    n = jnp.concatenate(ns, axis=1)
    return n * w, n, rss, sz


def _outproj_fwd(y, z, o, x, mod, ln_g, ln_b, norm_w, w_out):
    L = x.shape[0]
    tm = 256

    def body(y_ref, z_ref, o_ref, x_ref, mod_ref, g_ref, b_ref, nw_ref, w_ref, yn_ref, mix_ref, r1_ref):
        yn, _, _, _ = _gated_norm(y_ref[...], z_ref[...], nw_ref[...])
        ynb = yn.astype(bf16)
        yn_ref[...] = ynb
        mix = (_dot(ynb[:, 0:512], w_ref[0]) + _dot(ynb[:, 512:1024], w_ref[1])
               + _dot(o_ref[:, 0:512], w_ref[2]) + _dot(o_ref[:, 512:1024], w_ref[3]))
        mix_ref[...] = mix
        xhat, _ = _ln_stats(x_ref[...])
        h0 = xhat * g_ref[...] + b_ref[...]
        r1_ref[...] = ALPHA * h0 + (1.0 + mod_ref[2:3, :]) * mix

    v = _full((1, D_MODEL))
    return pl.pallas_call(
        body, name="outproj_fwd", grid=(L // tm,),
        in_specs=[_rows(tm, D_MODEL), _rows(tm, D_MODEL), _rows(tm, D_MODEL), _rows(tm, D_MODEL),
                  _full((8, D_MODEL)), v, v, v, _wall_spec(512, R_OUT)],
        out_specs=[_rows(tm, D_MODEL)] * 3,
        out_shape=[jax.ShapeDtypeStruct((L, D_MODEL), bf16), jax.ShapeDtypeStruct((L, D_MODEL), f32),
                   jax.ShapeDtypeStruct((L, D_MODEL), f32)],
        compiler_params=_cparams("parallel"),
    )(y, z, o, x, mod, ln_g, ln_b, norm_w, w_out)


A_LN2G, A_LN2B, A_G2, A_B2, A_SC2, A_SH2, A_LN1G, A_LN1B, A_LOSS = range(9)


def _mlp_fwd_bwd(r1, target, mod, ln1_g, ln1_b, ln2_g, ln2_b, w1, b1, w2, b2):
    L = r1.shape[0]
    tm = 256
    nj = D_FF // 1024

    def body(r1_ref, t_ref, mod_ref, g1_ref, bb1_ref, g2_ref, bb2_ref, w1_ref, b1_ref, w2_ref, b2_ref,
             dr1_ref, u2_ref, s_ref, da_ref, df_ref, acc_ref, db1_ref, hr):
        @pl.when(pl.program_id(0) == 0)
        def _():
            acc_ref[...] = jnp.zeros_like(acc_ref)
            db1_ref[...] = jnp.zeros_like(db1_ref)

        sc2, sh2, gate2 = mod_ref[4:5, :], mod_ref[3:4, :], mod_ref[5:6, :]
        xhat1, rstd1 = _ln_stats(r1_ref[...])
        h1 = xhat1 * g1_ref[...] + bb1_ref[...]
        u2f = h1 * (1.0 + sc2) + sh2
        u2 = u2f.astype(bf16)
        u2_ref[...] = u2
        f = jnp.zeros((tm, D_MODEL), f32) + b2_ref[...]
        for j in range(nj):
            cs = slice(1024 * j, 1024 * (j + 1))
            a = _dot(u2, w1_ref[j]) + b1_ref[:, cs]
            hrj = jnp.maximum(a, 0.0)
            hr[:, cs] = hrj
            sj = (hrj * hrj).astype(bf16)
            s_ref[:, cs] = sj
            f = f + _dot(sj, w2_ref[j])
        r2 = ALPHA * h1 + (1.0 + gate2) * f
        xhat2, rstd2 = _ln_stats(r2)
        h2 = xhat2 * g2_ref[...] + bb2_ref[...]
        diff = h2 - t_ref[...]
        dh2 = diff * (1.0 / D_MODEL)

        def add(row, val):
            acc_ref[row:row + 1, :] += jnp.sum(val, axis=0, keepdims=True)

        add(A_LOSS, diff * diff * (0.5 / D_MODEL))
        add(A_LN2G, dh2 * xhat2)
        add(A_LN2B, dh2)
        dr2 = _ln_bwd(dh2, xhat2, rstd2, g2_ref[...])
        add(A_G2, dr2 * f)
        df = dr2 * (1.0 + gate2)
        add(A_B2, df)
        dfb = df.astype(bf16)
        df_ref[...] = dfb
        du2 = jnp.zeros((tm, D_MODEL), f32)
        for j in range(nj):
            cs = slice(1024 * j, 1024 * (j + 1))
            ds = _dot_nt(dfb, w2_ref[j])
            daj = ds * (2.0 * hr[:, cs])
            db1_ref[:, cs] += jnp.sum(daj, axis=0, keepdims=True)
            dajb = daj.astype(bf16)
            da_ref[:, cs] = dajb
            du2 = du2 + _dot_nt(dajb, w1_ref[j])
        add(A_SC2, du2 * h1)
        add(A_SH2, du2)
        dh1 = ALPHA * dr2 + du2 * (1.0 + sc2)
        add(A_LN1G, dh1 * xhat1)
        add(A_LN1B, dh1)
        dr1_ref[...] = _ln_bwd(dh1, xhat1, rstd1, g1_ref[...])

    v = _full((1, D_MODEL))
    return pl.pallas_call(
        body, name="mlp_fwd_bwd", grid=(L // tm,),
        in_specs=[_rows(tm, D_MODEL), _rows(tm, D_MODEL), _full((8, D_MODEL)), v, v, v, v,
                  _wall_spec(1024, R_FF1), _full((1, D_FF)), _wall_spec(1024, R_FF2), v],
        out_specs=[_rows(tm, D_MODEL), _rows(tm, D_MODEL), _rows(tm, D_FF), _rows(tm, D_FF), _rows(tm, D_MODEL),
                   _full((16, D_MODEL)), _full((1, D_FF))],
        out_shape=[jax.ShapeDtypeStruct((L, D_MODEL), f32), jax.ShapeDtypeStruct((L, D_MODEL), bf16),
                   jax.ShapeDtypeStruct((L, D_FF), bf16), jax.ShapeDtypeStruct((L, D_FF), bf16),
                   jax.ShapeDtypeStruct((L, D_MODEL), bf16), jax.ShapeDtypeStruct((16, D_MODEL), f32),
                   jax.ShapeDtypeStruct((1, D_FF), f32)],
        scratch_shapes=[pltpu.VMEM((tm, D_FF), f32)],
        compiler_params=_cparams("arbitrary"),
    )(r1, target, mod, ln1_g, ln1_b, ln2_g, ln2_b, w1, b1, w2, b2)


def _wgrad(a, b, name):
    L, M = a.shape
    N = b.shape[1]
    tm = min(M, 512)
    tn = next(t for t in (1024, 768, 512, 256, 128) if N % t == 0)

    def body(a_ref, b_ref, o_ref):
        o_ref[...] = _dot_tn(a_ref[...], b_ref[...]).astype(bf16)

    return pl.pallas_call(
        body, name=name, grid=(M // tm, N // tn),
        in_specs=[pl.BlockSpec((L, tm), lambda i, j: (0, i)), pl.BlockSpec((L, tn), lambda i, j: (0, j))],
        out_specs=pl.BlockSpec((tm, tn), lambda i, j: (i, j)),
        out_shape=jax.ShapeDtypeStruct((M, N), bf16),
        compiler_params=_cparams("parallel", "parallel"),
    )(a, b)


def _wgrad_blob(blob, a, b, name, place_of):
    L, M = a.shape
    N = b.shape[1]

    def body(blob_ref, a_ref, b_ref, o_ref):
        o_ref[0] = _dot_tn(a_ref[...], b_ref[...]).astype(bf16)

    return pl.pallas_call(
        body, name=name, grid=(M // WG_TM, N // D_MODEL),
        in_specs=[pl.BlockSpec(memory_space=pl.ANY), pl.BlockSpec((L, WG_TM), lambda t, n: (0, t)),
                  pl.BlockSpec((L, D_MODEL), lambda t, n: (0, n))],
        out_specs=pl.BlockSpec((1, WG_TM, D_MODEL), lambda t, n: (*place_of(t, n), 0)),
        out_shape=jax.ShapeDtypeStruct(blob.shape, bf16), input_output_aliases={0: 0},
        compiler_params=_cparams("parallel", "parallel"),
    )(blob, a, b)


def _outproj_bwd(dr1, mix, y, z, mod, norm_w, w_out, gb):
    L = dr1.shape[0]
    tm = 256
    nt = L // tm

    def body(dr1_ref, mix_ref, y_ref, z_ref, mod_ref, nw_ref, w_ref, gb_ref,
             dy_ref, dz_ref, do_ref, dmix_ref, acc_ref, sib_ref, send_sems, recv_sems):
        start, wait = _to_sibling_job(gb_ref, sib_ref, send_sems, recv_sems)

        @pl.when(pl.program_id(0) == 0)
        def _():
            acc_ref[...] = jnp.zeros_like(acc_ref)
            start()

        @pl.when(pl.program_id(0) == nt - 1)
        def _():
            wait()

        dr1 = dr1_ref[...]
        acc_ref[0:1, :] += jnp.sum(dr1 * mix_ref[...], axis=0, keepdims=True)
        dmix = (dr1 * (1.0 + mod_ref[2:3, :])).astype(bf16)
        dmix_ref[...] = dmix
        dyn = jnp.concatenate([_dot_nt(dmix, w_ref[0]), _dot_nt(dmix, w_ref[1])], axis=1)
        do_ref[...] = jnp.concatenate([_dot_nt(dmix, w_ref[2]), _dot_nt(dmix, w_ref[3])], axis=1).astype(bf16)
        yv, zv = y_ref[...], z_ref[...]
        _, n, rss, sz = _gated_norm(yv, zv, nw_ref[...])
        acc_ref[1:2, :] += jnp.sum(dyn * n, axis=0, keepdims=True)
        dn = dyn * nw_ref[...]
        parts = []
        for g in range(SSD_GROUPS):
            sl = slice(512 * g, 512 * (g + 1))
            dng, ng = dn[:, sl], n[:, sl]
            parts.append(rss[g] * (dng - ng * jnp.mean(dng * ng, axis=-1, keepdims=True)))
        dhg = jnp.concatenate(parts, axis=1)
        dy_ref[...] = dhg * (zv * sz)
        dz_ref[...] = (dhg * yv * (sz * (1.0 + zv * (1.0 - sz)))).astype(bf16)

    return pl.pallas_call(
        body, name="outproj_bwd", grid=(nt,),
        in_specs=[_rows(tm, D_MODEL)] * 4 + [_full((8, D_MODEL)), _full((1, D_MODEL)), _wall_spec(512, R_OUT), _ANY_SPEC],
        out_specs=[_rows(tm, D_MODEL)] * 4 + [_full((8, D_MODEL)), _ANY_SPEC],
        out_shape=[jax.ShapeDtypeStruct((L, D_MODEL), f32)] + [jax.ShapeDtypeStruct((L, D_MODEL), bf16)] * 3
        + [jax.ShapeDtypeStruct((8, D_MODEL), f32), jax.ShapeDtypeStruct((N_CHIPS,) + gb.shape[2:], bf16)],
        scratch_shapes=_sems(N_CHIPS),
        compiler_params=_cparams("arbitrary"),
    )(dr1, mix, y, z, mod, norm_w, w_out, gb)


def _attn_bwd(q, kv, do, lse, sinks, pb):
    L = q.shape[0]
    nb = L // CHUNK
    T = CHUNK

    def body(sink_ref, q_ref, kvp_ref, kvc_ref, do_ref, lse_ref, pb_ref, dq_ref, dkv_ref, dsink_ref, chips_ref,
             carry, send_sems, recv_sems):
        n = pl.program_id(0)
        start, wait = _to_chips_job(pb_ref, chips_ref, send_sems, recv_sems)

        @pl.when(n == 0)
        def _():
            carry[...] = jnp.zeros_like(carry)
            dsink_ref[...] = jnp.zeros_like(dsink_ref)
            start()

        @pl.when(n < nb)
        def _():
            first = n == 0
            ext = _kv_halves(kvp_ref[...], kvc_ref[...], first)
            dist, valid = _attn_masks(first)
            lane1 = lax.broadcasted_iota(jnp.int32, (1, 128), 1)
            lse = lse_ref[...]
            acck = [[None, None], [None, None]]
            accv = [[None, None], [None, None]]
            dsink = jnp.zeros((1, 128), f32)
            dq_blocks = []
            for blk in range(8):
                qb = q_ref[:, 128 * blk:128 * (blk + 1)]
                dob = do_ref[:, 128 * blk:128 * (blk + 1)]
                dq_acc = None
                for half in range(2):
                    h = 2 * blk + half
                    g = h // 8
                    k_ext, v_ext = ext[g][half]
                    s = _dot_nt(qb, k_ext) * ATTN_SCALE - ALIBI_SLOPES[h] * dist
                    lse_h = lse[:, h:h + 1]
                    p = jnp.where(valid, jnp.exp(s - lse_h), 0.0)
                    dp = _dot_nt(dob, v_ext)
                    delta = jnp.sum(p * dp, axis=-1, keepdims=True)
                    ds = (p * (dp - delta) * ATTN_SCALE).astype(bf16)
                    psink = jnp.exp(sink_ref[h] - lse_h)
                    dsink = dsink - jnp.where(lane1 == h, jnp.sum(psink * delta, axis=0, keepdims=True), 0.0)
                    dqh = _dot(ds, k_ext)
                    dq_acc = dqh if dq_acc is None else dq_acc + dqh
                    dkh = _dot_tn(ds, qb)
                    dvh = _dot_tn(p.astype(bf16), dob)
                    acck[g][half] = dkh if acck[g][half] is None else acck[g][half] + dkh
                    accv[g][half] = dvh if accv[g][half] is None else accv[g][half] + dvh
                dq_blocks.append(dq_acc.astype(bf16))
            dq_ref[...] = jnp.concatenate(dq_blocks, axis=1)
            dsink_ref[...] += dsink
            lo = lax.broadcasted_iota(jnp.int32, (2 * T, 128), 1) < HEAD_DIM

            def fold(acc):
                return jnp.where(lo, acc[0][0] + pltpu.roll(acc[0][1], HEAD_DIM, 1),
                                 acc[1][1] + pltpu.roll(acc[1][0], HEAD_DIM, 1))

            dkv = jnp.concatenate([fold(acck), fold(accv)], axis=1)
            dkv_ref[...] = (carry[...] + dkv[0:T, :]).astype(bf16)
            carry[...] = dkv[T:2 * T, :]

        @pl.when(n == nb)
        def _():
            dkv_ref[...] = carry[...].astype(bf16)
            wait()

    cur = lambda i: (jnp.minimum(i, nb - 1), 0)
    return pl.pallas_call(
        body, name="attn_bwd", grid=(nb + 1,),
        in_specs=[pl.BlockSpec(memory_space=pltpu.SMEM), pl.BlockSpec((T, D_MODEL), cur),
                  pl.BlockSpec((T, 256), lambda i: (jnp.maximum(jnp.minimum(i, nb - 1) - 1, 0), 0)),
                  pl.BlockSpec((T, 256), cur), pl.BlockSpec((T, D_MODEL), cur), pl.BlockSpec((T, 128), cur), _ANY_SPEC],
        out_specs=[pl.BlockSpec((T, D_MODEL), cur), pl.BlockSpec((T, 256), lambda i: (jnp.maximum(i - 1, 0), 0)),
                   _full((1, 128)), _ANY_SPEC],
        out_shape=[jax.ShapeDtypeStruct((L, D_MODEL), bf16), jax.ShapeDtypeStruct((L, 256), bf16),
                   jax.ShapeDtypeStruct((1, 128), f32), jax.ShapeDtypeStruct((N_CHIPS - 1,) + pb.shape[1:], bf16)],
        scratch_shapes=[pltpu.VMEM((T, 256), f32)] + _sems(N_CHIPS - 1),
        compiler_params=_cparams("arbitrary"),
    )(sinks, q, kv, kv, do, lse, pb)


def _ssd_bwd(xc, dt_raw, dy, prev_all, dt_bias, a_log, d_skip_x, e_mat, g):
    L = xc.shape[0]
    nc = L // CHUNK
    T = CHUNK
    RG = g.shape[0]

    def body(xc_ref, dt_ref, dy_ref, prev_ref, dtb_ref, al_ref, dsk_ref, e_ref, g_in_ref,
             dxc_ref, ddt_ref, acc_ref, dd_ref, g_ref, dst, dxs_s, send_sems, recv_sems):
        start, wait = _share_job(g_ref, send_sems, recv_sems, RG)

        @pl.when(pl.program_id(0) == 0)
        def _():
            dst[...] = jnp.zeros_like(dst)
            acc_ref[...] = jnp.zeros_like(acc_ref)
            dd_ref[...] = jnp.zeros_like(dd_ref)
            start()

        @pl.when(pl.program_id(0) == nc - 1)
        def _():
            wait()

        lane1 = lax.broadcasted_iota(jnp.int32, (1, 128), 1)
        a_row = jnp.where(lane1 < SSD_HEADS, -jnp.exp(al_ref[...]), 0.0)
        e_mat_v = e_ref[...]
        dt, a, cum, dtx, cumx, r, c = _ssd_chunk_common(dt_ref[...], dtb_ref[...], a_row, e_mat_v)
        cum_t = cum.T
        ex = jnp.exp(cumx)
        last = cumx[T - 1:T, :]
        wx = jnp.exp(last - cumx)
        cdx = jnp.exp(last)
        xs = xc_ref[:, 0:SSD_WIDTH]
        X = xs * dtx
        Xb = X.astype(bf16)
        Xdb = (X * wx).astype(bf16)
        dyv = dy_ref[...]
        prev = prev_ref[0]
        prevb = prev.astype(bf16)
        dnew = dst[...]
        dnewb = dnew.astype(bf16)
        tri = c <= r
        lane = lax.broadcasted_iota(jnp.int32, (T, 128), 1)
        sub = lax.broadcasted_iota(jnp.int32, (128, T), 0)
        lo = lane < HEAD_DIM

        def red(vals, g):
            return lax.dot_general(vals, e_mat_v[:, 512 * g:512 * (g + 1)], (((1,), (1,)), ((), ())),
                                   preferred_element_type=f32, precision=lax.Precision.HIGHEST)

        de = jnp.zeros((T, 128), f32)
        dw = jnp.zeros((T, 128), f32)
        ddt_x = jnp.zeros((T, 128), f32)
        dcum_col = jnp.zeros((T, 128), f32)
        dcum_row = jnp.zeros((128, T), f32)
        dprev_parts, dBs, dCs = [], [], []
        for g in range(SSD_GROUPS):
            s5 = slice(512 * g, 512 * (g + 1))
            Bg = xc_ref[:, 1024 + 128 * g:1152 + 128 * g].astype(bf16)
            Cg = xc_ref[:, 1280 + 128 * g:1408 + 128 * g].astype(bf16)
            G = _dot_nt(Cg, Bg)
            Z = _dot(Cg, prevb[:, s5])
            dyg = dyv[:, s5]
            dZb = (dyg * ex[:, s5]).astype(bf16)
            dXd = _dot(Bg, dnewb[:, s5])
            dC = _dot_nt(dZb, prevb[:, s5])
            dB = _dot_nt(Xdb[:, s5], dnewb[:, s5])
            dprev_parts.append(_dot_tn(Cg, dZb) + dnew[:, s5] * cdx[:, s5])
            de = de + red(dyg * Z, g)
            dw = dw + red(dXd * X[:, s5], g)
            dXg = dXd * wx[:, s5]
            dG = jnp.zeros((T, T), f32)
            for j in range(4):
                blk = 4 * g + j
                sl = slice(128 * blk, 128 * (blk + 1))
                Xblk = Xb[:, sl]
                dyblk = dyv[:, sl]
                dyblk_b = dyblk.astype(bf16)
                dxh = []
                for half in range(2):
                    h = 2 * blk + half
                    seg = jnp.minimum(cum[:, h:h + 1] - cum_t[h:h + 1, :], 0.0)
                    Lm = jnp.where(tri, jnp.exp(seg), 0.0)
                    M = G * Lm
                    dyh = jnp.where(lo if half == 0 else ~lo, dyblk, 0.0).astype(bf16)
                    dM = _dot_nt(dyh, Xblk)
                    dG = dG + dM * Lm
                    Q = dM * M
                    dcum_col = dcum_col + jnp.where(lane == h, jnp.sum(Q, axis=1, keepdims=True), 0.0)
                    dcum_row = dcum_row + jnp.where(sub == h, jnp.sum(Q, axis=0, keepdims=True), 0.0)
                    dxh.append(_dot_tn(M.astype(bf16), dyblk_b))
                dXblk = dXg[:, 128 * j:128 * (j + 1)] + jnp.where(lo, dxh[0], dxh[1])
                xsb = xs[:, sl]
                dxs_s[:, sl] = dXblk * dtx[:, sl] + dsk_ref[:, sl] * dyblk
                ddt_x = ddt_x + lax.dot_general(dXblk * xsb, e_mat_v[:, sl], (((1,), (1,)), ((), ())),
                                                preferred_element_type=f32, precision=lax.Precision.HIGHEST)
                dd_ref[:, sl] += jnp.sum(dyblk * xsb, axis=0, keepdims=True)
            dGb = dG.astype(bf16)
            dCs.append(dC + _dot(dGb, Bg))
            dBs.append(dB + _dot_tn(dGb, Cg))
        e16 = jnp.exp(cum)
        cum_last = cum[T - 1:T, :]
        w16 = jnp.exp(cum_last - cum)
        dcd = jnp.sum(dnew * prev, axis=0, keepdims=True)
        dcd16 = red(dcd[:, 0:512], 0) + red(dcd[:, 512:1024], 1)
        dww = dw * w16
        extra = jnp.sum(dww, axis=0, keepdims=True) + dcd16 * jnp.exp(cum_last)
        rowi = lax.broadcasted_iota(jnp.int32, (T, 128), 0)
        dcum = dcum_col - dcum_row.T + de * e16 - dww + jnp.where(rowi == T - 1, extra, 0.0)
        triu = (c >= r).astype(f32)
        da = _dot_exact(triu, dcum)
        ddt = ddt_x + da * a_row
        acc_ref[0:1, :] += jnp.sum(da * dt, axis=0, keepdims=True)
        ddt_raw = jnp.where(lane < SSD_HEADS, ddt * _sigmoid(dt_ref[...] + dtb_ref[...]), 0.0)
        ddt_ref[...] = ddt_raw
        acc_ref[1:2, :] += jnp.sum(ddt_raw, axis=0, keepdims=True)
        dxc_ref[:, 0:SSD_WIDTH] = dxs_s[...]
        dxc_ref[:, 1024:1280] = jnp.concatenate(dBs, axis=1)
        dxc_ref[:, 1280:1536] = jnp.concatenate(dCs, axis=1)
        dst[...] = jnp.concatenate(dprev_parts, axis=1)

    rev = lambda i: (nc - 1 - i, 0)
    return pl.pallas_call(
        body, name="ssd_bwd", grid=(nc,),
        in_specs=[pl.BlockSpec((T, CONV_DIM), rev), pl.BlockSpec((T, 128), rev), pl.BlockSpec((T, SSD_WIDTH), rev),
                  pl.BlockSpec((1, SSD_STATE, SSD_WIDTH), lambda i: (nc - 1 - i, 0, 0)),
                  _full((1, 128)), _full((1, 128)), _full((1, SSD_WIDTH)), _full((128, SSD_WIDTH)), _ANY_SPEC],
        out_specs=[pl.BlockSpec((T, CONV_DIM), rev), pl.BlockSpec((T, 128), rev), _full((8, 128)),
                   _full((1, SSD_WIDTH)), _ANY_SPEC],
        out_shape=[jax.ShapeDtypeStruct((L, CONV_DIM), f32), jax.ShapeDtypeStruct((L, 128), f32),
                   jax.ShapeDtypeStruct((8, 128), f32), jax.ShapeDtypeStruct((1, SSD_WIDTH), f32),
                   jax.ShapeDtypeStruct(g.shape, f32)],
        input_output_aliases={8: 4},
        scratch_shapes=[pltpu.VMEM((SSD_STATE, SSD_WIDTH), f32), pltpu.VMEM((T, SSD_WIDTH), f32)] + _sems(1),
        compiler_params=_cparams("arbitrary"),
    )(xc, dt_raw, dy, prev_all, dt_bias, a_log, d_skip_x, e_mat, g)


def _conv_bwd(dxc, xr, conv_w, conv_b):
    L = dxc.shape[0]
    tm = 256
    nt = L // tm

    def body(dxc_ref, xr_ref, xh_ref, cw_ref, cb_ref, dxr_ref, acc_ref, carry, buf, buf2):
        i = pl.program_id(0)

        @pl.when(i == 0)
        def _():
            carry[...] = jnp.zeros_like(carry)
            acc_ref[...] = jnp.zeros_like(acc_ref)

        buf[0:8, :] = jnp.where(i == nt - 1, 0.0, xh_ref[...])
        buf[8:8 + tm, :] = xr_ref[...]
        pre = cb_ref[...] + cw_ref[0:1, :] * buf[5:5 + tm, :]
        for k in range(1, CONV_K):
            pre = pre + cw_ref[k:k + 1, :] * buf[5 + k:5 + k + tm, :]
        sg = _sigmoid(pre)
        dpre = dxc_ref[...] * (sg * (1.0 + pre * (1.0 - sg)))
        acc_ref[4:5, :] += jnp.sum(dpre, axis=0, keepdims=True)
        for k in range(CONV_K):
            acc_ref[k:k + 1, :] += jnp.sum(dpre * buf[5 + k:5 + k + tm, :], axis=0, keepdims=True)
        buf2[0:tm, :] = dpre
        buf2[tm:tm + 8, :] = carry[...]
        du = cw_ref[0:1, :] * buf2[3:3 + tm, :]
        for k in range(1, CONV_K):
            du = du + cw_ref[k:k + 1, :] * buf2[3 - k:3 - k + tm, :]
        dxr_ref[...] = du.astype(bf16)
        carry[...] = dpre[0:8, :]

    rev = lambda i: (nt - 1 - i, 0)
    return pl.pallas_call(
        body, name="conv_bwd", grid=(nt,),
        in_specs=[pl.BlockSpec((tm, CONV_DIM), rev), pl.BlockSpec((tm, CONV_DIM), rev),
                  pl.BlockSpec((8, CONV_DIM), lambda i: (jnp.maximum((nt - 1 - i) * (tm // 8) - 1, 0), 0)),
                  _full((CONV_K, CONV_DIM)), _full((1, CONV_DIM))],
        out_specs=[pl.BlockSpec((tm, CONV_DIM), rev), _full((8, CONV_DIM))],
        out_shape=[jax.ShapeDtypeStruct((L, CONV_DIM), bf16), jax.ShapeDtypeStruct((8, CONV_DIM), f32)],
        scratch_shapes=[pltpu.VMEM((8, CONV_DIM), f32), pltpu.VMEM((tm + 8, CONV_DIM), f32),
                        pltpu.VMEM((tm + 8, CONV_DIM), f32)],
        compiler_params=_cparams("arbitrary"),
    )(dxc, xr, xr, conv_w, conv_b)


def _inproj_bwd(dz, dxr, dq, dkv, ddt, dr1, x, mod, ln_g, ln_b, w_in):
    L = x.shape[0]
    tm = 256

    def body(dz_ref, dxr_ref, dq_ref, dkv_ref, ddt_ref, dr1_ref, x_ref, mod_ref, g_ref, b_ref, w_ref, dx_ref, acc_ref):
        @pl.when(pl.program_id(0) == 0)
        def _():
            acc_ref[...] = jnp.zeros_like(acc_ref)

        du1 = (_dot(dz_ref[...], w_ref[C_Z:C_XBC, :]) + _dot(dxr_ref[...], w_ref[C_XBC:C_Q, :])
               + _dot(dq_ref[...], w_ref[C_Q:C_KV, :]) + _dot(dkv_ref[...], w_ref[C_KV:C_DT, :])
               + _dot(ddt_ref[...].astype(bf16), w_ref[C_DT:C_END, :]))
        xhat, rstd = _ln_stats(x_ref[...])
        h0 = xhat * g_ref[...] + b_ref[...]
        acc_ref[0:1, :] += jnp.sum(du1 * h0, axis=0, keepdims=True)
        acc_ref[1:2, :] += jnp.sum(du1, axis=0, keepdims=True)
        dh0 = du1 * (1.0 + mod_ref[1:2, :]) + ALPHA * dr1_ref[...]
        acc_ref[2:3, :] += jnp.sum(dh0 * xhat, axis=0, keepdims=True)
        acc_ref[3:4, :] += jnp.sum(dh0, axis=0, keepdims=True)
        dx_ref[...] = _ln_bwd(dh0, xhat, rstd, g_ref[...])

    v = _full((1, D_MODEL))
    return pl.pallas_call(
        body, name="inproj_bwd", grid=(L // tm,),
        in_specs=[_rows(tm, D_MODEL), _rows(tm, CONV_DIM), _rows(tm, D_MODEL), _rows(tm, 256), _rows(tm, 128),
                  _rows(tm, D_MODEL), _rows(tm, D_MODEL), _full((8, D_MODEL)), v, v, _resident((C_END, D_MODEL))],
        out_specs=[_rows(tm, D_MODEL), _full((8, D_MODEL))],
        out_shape=[jax.ShapeDtypeStruct((L, D_MODEL), f32), jax.ShapeDtypeStruct((8, D_MODEL), f32)],
        compiler_params=_cparams("arbitrary"),
    )(dz, dxr, dq, dkv, ddt, dr1, x, mod, ln_g, ln_b, w_in)


def _adamw_math(w, g, m, v):
    m = ADAM_B1 * m + (1.0 - ADAM_B1) * g
    v = ADAM_B2 * v + (1.0 - ADAM_B2) * (g * g)
    m_hat = m / (1.0 - ADAM_B1 ** ADAM_STEP)
    v_hat = v / (1.0 - ADAM_B2 ** ADAM_STEP)
    delta = -ADAM_LR * (m_hat / (jnp.sqrt(v_hat) + ADAM_EPS) + ADAM_WD * w)
    return delta, m, v


def _adamw(w, g, m, v, name):
    R, C = w.shape

    def body(w_ref, g_ref, m_ref, v_ref, d_ref, m2_ref, v2_ref):
        d_ref[...], m2_ref[...], v2_ref[...] = _adamw_math(w_ref[...], g_ref[...], m_ref[...], v_ref[...])

    if R <= 256 or R % 256 == 0:
        tr = min(R, 256)
        spec, steps = pl.BlockSpec((tr, C), lambda i: (i, 0)), R // tr
    else:
        spec, steps = pl.BlockSpec((R, 256), lambda i: (0, i)), C // 256
    return pl.pallas_call(
        body, name=name, grid=(steps,), in_specs=[spec] * 4, out_specs=[spec] * 3,
        out_shape=[jax.ShapeDtypeStruct((R, C), f32)] * 3, compiler_params=_cparams("parallel"),
    )(w, g, m, v)


ADA_COLS = 6 * D_MODEL // N_CHIPS
ADA_TN = 512


def _ada_fwd(c_all, ada_w, ada_b):
    def body(c_ref, w_ref, b_ref, o_ref):
        cv = c_ref[...]
        o_ref[...] = _dot_exact(cv * _sigmoid(cv), w_ref[...]) + b_ref[...]

    return pl.pallas_call(
        body, name="ada_fwd", grid=(ADA_COLS // ADA_TN,),
        in_specs=[_full((N_DEV, D_MODEL)), pl.BlockSpec((D_MODEL, ADA_TN), lambda j: (0, j)),
                  pl.BlockSpec((1, ADA_TN), lambda j: (0, j))],
        out_specs=pl.BlockSpec((N_DEV, ADA_TN), lambda j: (0, j)),
        out_shape=jax.ShapeDtypeStruct((N_DEV, ADA_COLS), f32), compiler_params=_cparams("parallel"),
    )(c_all, ada_w, ada_b)


def _ada_bwd(c_all, dmod, w, m, v):
    def body(c_ref, d_ref, w_ref, m_ref, v_ref, g_ref, dl_ref, m2_ref, v2_ref):
        cv = c_ref[...]
        g = lax.dot_general(cv * _sigmoid(cv), d_ref[...], (((0,), (0,)), ((), ())), preferred_element_type=f32,
                            precision=lax.Precision.HIGHEST)
        g_ref[...] = g
        dl_ref[...], m2_ref[...], v2_ref[...] = _adamw_math(w_ref[...], g, m_ref[...], v_ref[...])

    wspec = pl.BlockSpec((D_MODEL, ADA_TN), lambda j: (0, j))
    return pl.pallas_call(
        body, name="ada_bwd", grid=(ADA_COLS // ADA_TN,),
        in_specs=[_full((N_DEV, D_MODEL)), pl.BlockSpec((N_DEV, ADA_TN), lambda j: (0, j)), wspec, wspec, wspec],
        out_specs=[wspec] * 4, out_shape=[jax.ShapeDtypeStruct((D_MODEL, ADA_COLS), f32)] * 4,
        compiler_params=_cparams("parallel"),
    )(c_all, dmod, w, m, v)


def _small_update(gathered, w, m, v):
    n = w.shape[1]

    def body(g_ref, w_ref, m_ref, v_ref, gs_ref, d_ref, m2_ref, v2_ref):
        g = g_ref[0:1, :]
        for i in range(1, N_DEV):
            g = g + g_ref[i:i + 1, :]
        gs_ref[...] = g
        d_ref[...], m2_ref[...], v2_ref[...] = _adamw_math(w_ref[...], g, m_ref[...], v_ref[...])

    return pl.pallas_call(body, name="small_update", out_shape=[jax.ShapeDtypeStruct((1, n), f32)] * 4,
                          compiler_params=_cparams())(gathered, w, m, v)


def _place():
    return lax.axis_index("x"), lax.axis_index("y"), lax.axis_index("c")


def _flip(x, y, c, m):
    return (1 - x if m & 4 else x, 1 - y if m & 2 else y, 1 - c if m & 1 else c)


_VMEM_SPEC = pl.BlockSpec(memory_space=pltpu.VMEM)
_ANY_SPEC = pl.BlockSpec(memory_space=pl.ANY)


def _allgather8(v, name):
    n = v.shape[1]

    def body(v_ref, out_ref, send_sems, recv_sems, local_sem):
        x, y, c = _place()

        def rows(px, py, pc):
            return out_ref.at[pl.ds(pl.multiple_of((4 * px + 2 * py + pc) * 8, 8), 8), :]

        def copy(m, src, dst, to):
            return pltpu.make_async_remote_copy(src_ref=src, dst_ref=dst, send_sem=send_sems.at[m - 1],
                                                recv_sem=recv_sems.at[m - 1], device_id=to, device_id_type=MESH)

        mine = pltpu.make_async_copy(v_ref, rows(x, y, c), local_sem)
        mine.start()
        sends = [copy(m, v_ref, rows(x, y, c), _flip(x, y, c, m)) for m in range(1, N_DEV)]
        for cp in sends:
            cp.start()
        for m in range(1, N_DEV):
            peer = _flip(x, y, c, m)
            copy(m, v_ref, rows(*peer), peer).wait_recv()
        for cp in sends:
            cp.wait_send()
        mine.wait()

    return pl.pallas_call(
        body, name=name, out_shape=jax.ShapeDtypeStruct((8 * N_DEV, n), f32), in_specs=[_VMEM_SPEC],
        out_specs=_VMEM_SPEC,
        scratch_shapes=[pltpu.SemaphoreType.DMA((N_DEV - 1,)), pltpu.SemaphoreType.DMA((N_DEV - 1,)),
                        pltpu.SemaphoreType.DMA],
    )(v)


def _mod_exchange(mod_all):
    def body(src_ref, out_ref, send_sems, recv_sems, local_sem):
        x, y, c = _place()

        def copy(m, src, dst, to):
            return pltpu.make_async_remote_copy(src_ref=src, dst_ref=dst, send_sem=send_sems.at[m - 1],
                                                recv_sem=recv_sems.at[m - 1], device_id=to, device_id_type=MESH)

        mine = pltpu.make_async_copy(src_ref.at[4 * x + 2 * y + c], out_ref.at[2 * x + y], local_sem)
        mine.start()
        sends = []
        for m in range(1, N_CHIPS):
            px, py, pc = _flip(x, y, c, 2 * m)
            sends.append(copy(m, src_ref.at[4 * px + 2 * py + pc], out_ref.at[2 * x + y], (px, py, pc)))
        for cp in sends:
            cp.start()
        for m in range(1, N_CHIPS):
            px, py, pc = _flip(x, y, c, 2 * m)
            copy(m, src_ref.at[0], out_ref.at[2 * px + py], (px, py, pc)).wait_recv()
        for cp in sends:
            cp.wait_send()
        mine.wait()

    return pl.pallas_call(
        body, name="mod_exchange", out_shape=jax.ShapeDtypeStruct((N_CHIPS, 12, 128), f32), in_specs=[_VMEM_SPEC],
        out_specs=_VMEM_SPEC,
        scratch_shapes=[pltpu.SemaphoreType.DMA((N_CHIPS - 1,)), pltpu.SemaphoreType.DMA((N_CHIPS - 1,)),
                        pltpu.SemaphoreType.DMA],
    )(mod_all)


def _remote(src, dst, send_sems, recv_sems, k, to):
    return pltpu.make_async_remote_copy(src_ref=src, dst_ref=dst, send_sem=send_sems.at[k], recv_sem=recv_sems.at[k],
                                        device_id=to, device_id_type=MESH)


def _gather_job(blob_ref, out_ref, send_sems, recv_sems, R):
    x, y, c = _place()
    sib = (x, y, 1 - c)
    hr = R // 2

    def half(px, py, pc):
        return out_ref.at[2 * px + py, pl.ds(pl.multiple_of(pc * hr, 16), hr), :]

    my_half = blob_ref.at[pl.ds(pl.multiple_of(c * hr, 16), hr), :]

    def first():
        return [_remote(my_half, half(x, y, c), send_sems, recv_sems, m - 1, _flip(x, y, c, 2 * m))
                for m in range(1, N_CHIPS)]

    def start():
        for cp in first():
            cp.start()

    def finish():
        passed = []
        for m in range(1, N_CHIPS):
            px, py, pc = _flip(x, y, c, 2 * m)
            _remote(my_half, half(px, py, pc), send_sems, recv_sems, m - 1, (px, py, pc)).wait_recv()
            fwd = _remote(half(px, py, pc), half(px, py, pc), send_sems, recv_sems, 2 + m, sib)
            fwd.start()
            passed.append(fwd)
        for m in range(1, N_CHIPS):
            px, py, pc = _flip(x, y, c, 2 * m)
            _remote(my_half, half(px, py, 1 - pc), send_sems, recv_sems, 2 + m, sib).wait_recv()
        for cp in first() + passed:
            cp.wait_send()

    return start, finish


def _gather_weights(blob):
    R = blob.shape[0]

    def body(blob_ref, out_ref, send_sems, recv_sems, local_sem, stage):
        x, y, c = _place()
        start, finish = _gather_job(blob_ref, out_ref, send_sems, recv_sems, R)
        start()
        for k in range(R // STAGE_ROWS):
            rows = pl.ds(STAGE_ROWS * k, STAGE_ROWS)
            cin = pltpu.make_async_copy(blob_ref.at[rows, :], stage, local_sem)
            cin.start()
            cin.wait()
            cout = pltpu.make_async_copy(stage, out_ref.at[2 * x + y, rows, :], local_sem)
            cout.start()
            cout.wait()
        finish()

    return pl.pallas_call(
        body, name="gather_weights", out_shape=jax.ShapeDtypeStruct((N_CHIPS, R, D_MODEL), bf16),
        in_specs=[_ANY_SPEC], out_specs=_ANY_SPEC,
        scratch_shapes=[pltpu.SemaphoreType.DMA((6,)), pltpu.SemaphoreType.DMA((6,)), pltpu.SemaphoreType.DMA,
                        pltpu.VMEM((STAGE_ROWS, D_MODEL), bf16)],
    )(blob)


def _to_sibling_job(g_ref, out_ref, send_sems, recv_sems):
    x, y, c = _place()

    def cps():
        return [_remote(g_ref.at[j, 1 - c], out_ref.at[j], send_sems, recv_sems, j, (x, y, 1 - c)) for j in range(N_CHIPS)]

    def start():
        for cp in cps():
            cp.start()

    def wait():
        for cp in cps():
            cp.wait()

    return start, wait


def _to_chips_job(p_ref, out_ref, send_sems, recv_sems):
    x, y, c = _place()

    def cps():
        out = []
        for m in range(1, N_CHIPS):
            px, py, pc = _flip(x, y, c, 2 * m)
            out.append(_remote(p_ref.at[2 * px + py], out_ref.at[m - 1], send_sems, recv_sems, m - 1, (px, py, pc)))
        return out

    def start():
        for cp in cps():
            cp.start()

    def wait():
        for cp in cps():
            cp.wait()

    return start, wait


def _share_job(g_ref, send_sems, recv_sems, R):
    x, y, c = _place()

    def rows(pc):
        return g_ref.at[pl.ds(pl.multiple_of(pc * (R // 2), 8), R // 2), :]

    def start():
        _remote(rows(c), rows(c), send_sems, recv_sems, 0, (x, y, 1 - c)).start()

    def wait():
        _remote(rows(c), rows(1 - c), send_sems, recv_sems, 0, (x, y, 1 - c)).wait_recv()
        _remote(rows(c), rows(c), send_sems, recv_sems, 0, (x, y, 1 - c)).wait_send()

    return start, wait


def _sems(n):
    return [pltpu.SemaphoreType.DMA((n,)), pltpu.SemaphoreType.DMA((n,))]


def _rs_to_sibling(gb):
    def body(g_ref, out_ref, send_sems, recv_sems):
        start, wait = _to_sibling_job(g_ref, out_ref, send_sems, recv_sems)
        start()
        wait()

    return pl.pallas_call(
        body, name="rs_to_sibling", out_shape=jax.ShapeDtypeStruct((N_CHIPS,) + gb.shape[2:], bf16),
        in_specs=[_ANY_SPEC], out_specs=_ANY_SPEC, scratch_shapes=_sems(N_CHIPS),
    )(gb)


def _rs_to_chips(pb):
    def body(p_ref, out_ref, send_sems, recv_sems):
        start, wait = _to_chips_job(p_ref, out_ref, send_sems, recv_sems)
        start()
        wait()

    return pl.pallas_call(
        body, name="rs_to_chips", out_shape=jax.ShapeDtypeStruct((N_CHIPS - 1,) + pb.shape[1:], bf16),
        in_specs=[_ANY_SPEC], out_specs=_ANY_SPEC, scratch_shapes=_sems(N_CHIPS - 1),
    )(pb)


def _rs_share(g):
    R = g.shape[0]

    def body(g_ref, out_ref, send_sems, recv_sems):
        start, wait = _share_job(out_ref, send_sems, recv_sems, R)
        start()
        wait()

    return pl.pallas_call(
        body, name="rs_share", out_shape=jax.ShapeDtypeStruct(g.shape, f32), in_specs=[_ANY_SPEC],
        out_specs=_ANY_SPEC, input_output_aliases={0: 0}, scratch_shapes=_sems(1),
    )(g)


RS_TR = 256


def _rs_sum_pair(place, gb, recv, name):
    hr = gb.shape[2]

    def body(pl_ref, g_ref, r_ref, o_ref):
        o_ref[0] = (g_ref[0, 0].astype(f32) + r_ref[0].astype(f32)).astype(bf16)

    return pl.pallas_call(
        body, name=name,
        grid_spec=pltpu.PrefetchScalarGridSpec(
            num_scalar_prefetch=1, grid=(N_CHIPS, hr // RS_TR),
            in_specs=[pl.BlockSpec((1, 1, RS_TR, D_MODEL), lambda j, i, p: (j, p[0], i, 0)),
                      pl.BlockSpec((1, RS_TR, D_MODEL), lambda j, i, p: (j, i, 0))],
            out_specs=pl.BlockSpec((1, RS_TR, D_MODEL), lambda j, i, p: (j, i, 0))),
        out_shape=jax.ShapeDtypeStruct((N_CHIPS, hr, D_MODEL), bf16),
        compiler_params=_cparams("parallel", "parallel"),
    )(place, gb, recv)


def _rs_sum_chips(place, gb, recv_sib, recv_chips, name):
    hr = gb.shape[2]
    nt = hr // RS_TR

    def body(pl_ref, g_ref, r1_ref, r2_ref, o_ref):
        acc = g_ref[0, 0].astype(f32) + r1_ref[0].astype(f32)
        for k in range(N_CHIPS - 1):
            acc = acc + r2_ref[k].astype(f32)
        o_ref[...] = acc

    return pl.pallas_call(
        body, name=name,
        grid_spec=pltpu.PrefetchScalarGridSpec(
            num_scalar_prefetch=1, grid=(nt,),
            in_specs=[pl.BlockSpec((1, 1, RS_TR, D_MODEL), lambda i, p: (p[1], p[0], i, 0)),
                      pl.BlockSpec((1, RS_TR, D_MODEL), lambda i, p: (p[1], i, 0)),
                      pl.BlockSpec((N_CHIPS - 1, RS_TR, D_MODEL), lambda i, p: (0, i, 0))],
            out_specs=pl.BlockSpec((RS_TR, D_MODEL), lambda i, p: (p[0] * nt + i, 0))),
        out_shape=jax.ShapeDtypeStruct((2 * hr, D_MODEL), f32),
        compiler_params=_cparams("parallel"),
    )(place, gb, recv_sib, recv_chips)


SMALL_SLOTS = (("ada_b", 6144), ("ln_in_g", 1024), ("ln_in_b", 1024), ("conv_b", 1536), ("dt_bias", 128), ("a_log", 128),
               ("d_skip", 128), ("ssd_norm_w", 1024), ("attn_sinks", 128), ("ln1_g", 1024), ("ln1_b", 1024),
               ("b_ff1", 4096), ("b_ff2", 1024), ("ln2_g", 1024), ("ln2_b", 1024), ("conv_w", 6144), ("loss", 1024))
SMALL_N = sum(n for _, n in SMALL_SLOTS)
assert SMALL_N % 1024 == 0


def _pack_small(vals):
    parts = []
    for name, n in SMALL_SLOTS:
        v = vals.get(name)
        v = jnp.zeros((n,), f32) if v is None else v.reshape(-1).astype(f32)
        parts.append(jnp.pad(v, (0, n - v.shape[0])))
    return jnp.concatenate(parts)


def _unpack_small(vec):
    out, off = {}, 0
    for name, n in SMALL_SLOTS:
        out[name] = vec[off:off + n]
        off += n
    return out


def _pad128(v):
    v = v.reshape(1, -1)
    return jnp.pad(v, ((0, 0), (0, 128 - v.shape[1])))


def _row(v):
    return v.reshape(1, -1)


def _w_in_layout(w):
    return jnp.concatenate([w[:2560], w[2576:3856], w[2560:2576], jnp.zeros((C_END - PROJ_WIDTH, D_MODEL), w.dtype)], axis=0)


def kernel(x, c, ln_in_g, ln_in_b, ada_w, ada_b, w_in, conv_w, conv_b, dt_bias, a_log, d_skip, ssd_norm_w, attn_sinks, w_out, ln1_g, ln1_b, w_ff1, b_ff1, w_ff2, b_ff2, ln2_g, ln2_b, loss_target, m_ln_in_g, m_ln_in_b, m_ada_w, m_ada_b, m_w_in, m_conv_w, m_conv_b, m_dt_bias, m_a_log, m_d_skip, m_ssd_norm_w, m_attn_sinks, m_w_out, m_ln1_g, m_ln1_b, m_w_ff1, m_b_ff1, m_w_ff2, m_b_ff2, m_ln2_g, m_ln2_b, v_ln_in_g, v_ln_in_b, v_ada_w, v_ada_b, v_w_in, v_conv_w, v_conv_b, v_dt_bias, v_a_log, v_d_skip, v_ssd_norm_w, v_attn_sinks, v_w_out, v_ln1_g, v_ln1_b, v_w_ff1, v_b_ff1, v_w_ff2, v_b_ff2, v_ln2_g, v_ln2_b):
    xi, yi, ci = _place()
    chip = 2 * xi + yi
    place = jnp.stack([ci, chip]).astype(jnp.int32)
    x2, tgt = x[0], loss_target[0]
    w_shard_cols = PROJ_WIDTH // N_CHIPS

    cond = jnp.concatenate([c.reshape(-1), conv_w.reshape(-1), jnp.zeros((512,), f32)]).reshape(8, 384)
    cond_all = _allgather8(cond, "gather_cond").reshape(N_DEV, 3072)
    c_all = cond_all[:, :D_MODEL]
    conv_w_full = jnp.concatenate([cond_all[2 * j, D_MODEL:D_MODEL + 1536].reshape(CONV_K, 384) for j in range(N_CHIPS)], axis=1)

    ada_b_mine = lax.dynamic_slice(ada_b, (0, chip * ADA_COLS), (1, ADA_COLS))
    mod_all = _ada_fwd(c_all, ada_w[0], ada_b_mine)
    mod_mine = _mod_exchange(mod_all.reshape(N_DEV, 12, 128)).reshape(6, D_MODEL)
    mod = jnp.concatenate([mod_mine, jnp.zeros((2, D_MODEL), f32)], axis=0)

    w_in_t, m_w_in_t, v_w_in_t = w_in[0].T, m_w_in[0].T, v_w_in[0].T

    wall_in = _gather_weights(jnp.pad(w_in_t, ((0, D_MODEL - w_shard_cols), (0, 0))).astype(bf16))
    w_in_f = _w_in_layout(wall_in[:, :w_shard_cols].reshape(PROJ_WIDTH, D_MODEL))
    blob = jnp.concatenate([w_ff1[0], w_ff2[0], w_out[0]], axis=0).astype(bf16)

    e_mat = _head_expand()
    dsk_x = jnp.repeat(d_skip[0], HEAD_DIM).reshape(1, SSD_WIDTH)
    dtb, alog = _pad128(dt_bias), _pad128(a_log)
    sinks = attn_sinks[0]
    lng, lnb = _row(ln_in_g), _row(ln_in_b)
    u1, z, xr, xc, q, kv, dtr, wall = _inproj_fwd(x2, mod, lng, lnb, w_in_f, conv_w_full, conv_b, blob)
    wall = lax.dynamic_update_slice(wall, blob[None], (chip, 0, 0))
    y, prev_all = _ssd_fwd(xc, dtr, dtb, alog, dsk_x, e_mat)
    o, lse = _attn_fwd(q, kv, sinks)
    yn, mix, r1 = _outproj_fwd(y, z, o, x2, mod, lng, lnb, ssd_norm_w, wall)

    dr1, u2, s_act, da, df, acc_mlp, db1 = _mlp_fwd_bwd(r1, tgt, mod, ln1_g, ln1_b, ln2_g, ln2_b, wall, b_ff1, wall, b_ff2)
    ga = jnp.zeros((N_CHIPS, GA_ROWS, D_MODEL), bf16)
    ga = _wgrad_blob(ga, u2, da, "wgrad_ff1", lambda t, n: (n, t))
    ga = _wgrad_blob(ga, s_act, df, "wgrad_ff2", lambda t, n: (t // 2, 2 + t % 2))
    ga = ga.reshape(N_CHIPS, 2, GA_ROWS // 2, D_MODEL)
    dy, dz, do, dmix, acc_out, a_sib = _outproj_bwd(dr1, mix, y, z, mod, ssd_norm_w, wall, ga)
    gb = jnp.zeros((N_CHIPS, GB_ROWS, D_MODEL), bf16)
    gb = _wgrad_blob(gb, yn, dmix, "wgrad_out_y", lambda t, n: (t, 2))
    gb = _wgrad_blob(gb, o, dmix, "wgrad_out_o", lambda t, n: (2 + t, 2))
    a_pair = _rs_sum_pair(place, ga, a_sib, "rs_sum_pair_a")
    dq, dkv, dsink, a_chips = _attn_bwd(q, kv, do, lse, sinks, a_pair)
    g_a = _rs_sum_chips(place, ga, a_sib, a_chips, "rs_sum_chips_a")
    dxc, ddt, acc_ssd, dd_x, g_a = _ssd_bwd(xc, dtr, dy, prev_all, dtb, alog, dsk_x, e_mat, g_a)
    dxr, acc_conv = _conv_bwd(dxc, xr, conv_w_full, conv_b)
    grad_x, acc_in = _inproj_bwd(dz, dxr, dq, dkv, ddt, dr1, x2, mod, lng, lnb, w_in_f)
    g_in_t = jnp.concatenate([_wgrad(dz, u1, "wgrad_in_z"), _wgrad(dxr, u1, "wgrad_in_xbc"),
                              _wgrad(ddt.astype(bf16), u1, "wgrad_in_dt")[:16], _wgrad(dq, u1, "wgrad_in_q"),
                              _wgrad(dkv, u1, "wgrad_in_kv")], axis=0)

    dmod = jnp.concatenate([acc_in[1], acc_in[0], acc_out[0], acc_mlp[A_SH2], acc_mlp[A_SC2], acc_mlp[A_G2]])
    small = {
        "ada_b": dmod, "ln_in_g": acc_in[2], "ln_in_b": acc_in[3], "conv_b": acc_conv[4], "dt_bias": acc_ssd[1, :16],
        "a_log": acc_ssd[0, :16] * (-jnp.exp(a_log[0])), "d_skip": jnp.sum(dd_x.reshape(SSD_HEADS, HEAD_DIM), axis=1),
        "ssd_norm_w": acc_out[1], "attn_sinks": dsink[0, :16], "ln1_g": acc_mlp[A_LN1G], "ln1_b": acc_mlp[A_LN1B],
        "b_ff1": db1[0], "b_ff2": acc_mlp[A_B2], "ln2_g": acc_mlp[A_LN2G], "ln2_b": acc_mlp[A_LN2B],
        "conv_w": acc_conv[0:CONV_K], "loss": acc_mlp[A_LOSS],
    }
    small_all = _allgather8(_pack_small(small).reshape(8, SMALL_N // 8), "gather_small").reshape(N_DEV, SMALL_N)
    params = dict(ada_b=ada_b, ln_in_g=ln_in_g, ln_in_b=ln_in_b, conv_b=conv_b, dt_bias=dt_bias, a_log=a_log, d_skip=d_skip,
                  ssd_norm_w=ssd_norm_w, attn_sinks=attn_sinks, ln1_g=ln1_g, ln1_b=ln1_b, b_ff1=b_ff1, b_ff2=b_ff2,
                  ln2_g=ln2_g, ln2_b=ln2_b)
    moms = dict(ada_b=m_ada_b, ln_in_g=m_ln_in_g, ln_in_b=m_ln_in_b, conv_b=m_conv_b, dt_bias=m_dt_bias, a_log=m_a_log,
                d_skip=m_d_skip, ssd_norm_w=m_ssd_norm_w, attn_sinks=m_attn_sinks, ln1_g=m_ln1_g, ln1_b=m_ln1_b,
                b_ff1=m_b_ff1, b_ff2=m_b_ff2, ln2_g=m_ln2_g, ln2_b=m_ln2_b)
    vels = dict(ada_b=v_ada_b, ln_in_g=v_ln_in_g, ln_in_b=v_ln_in_b, conv_b=v_conv_b, dt_bias=v_dt_bias, a_log=v_a_log,
                d_skip=v_d_skip, ssd_norm_w=v_ssd_norm_w, attn_sinks=v_attn_sinks, ln1_g=v_ln1_g, ln1_b=v_ln1_b,
                b_ff1=v_b_ff1, b_ff2=v_b_ff2, ln2_g=v_ln2_g, ln2_b=v_ln2_b)
    gs, dl, m2, v2 = _small_update(small_all, _pack_small(params).reshape(1, -1), _pack_small(moms).reshape(1, -1),
                                   _pack_small(vels).reshape(1, -1))
    gs, dl, m2, v2 = (_unpack_small(t[0]) for t in (gs, dl, m2, v2))
    loss = jnp.sum(gs["loss"])

    dmod_mine = lax.dynamic_slice(small_all[:, :6 * D_MODEL], (0, chip * ADA_COLS), (N_DEV, ADA_COLS))
    g_ada, d_ada, m_ada, v_ada = _ada_bwd(c_all, dmod_mine, ada_w[0], m_ada_w[0], v_ada_w[0])

    g_conv = lax.dynamic_slice(gs["conv_w"].reshape(CONV_K, CONV_DIM), (0, chip * 384), (CONV_K, 384))
    d_conv, m_conv, v_conv = _adamw(conv_w[0], g_conv, m_conv_w[0], v_conv_w[0], "adamw_conv_w")

    g_in_c = jnp.pad(g_in_t.reshape(N_CHIPS, w_shard_cols, D_MODEL), ((0, 0), (0, D_MODEL - w_shard_cols), (0, 0)))
    gb = lax.dynamic_update_slice(gb, g_in_c, (0, 0, 0)).reshape(N_CHIPS, 2, GB_ROWS // 2, D_MODEL)
    b_sib = _rs_to_sibling(gb)
    b_chips = _rs_to_chips(_rs_sum_pair(place, gb, b_sib, "rs_sum_pair_b"))
    g_b = _rs_share(_rs_sum_chips(place, gb, b_sib, b_chips, "rs_sum_chips_b"))
    g_w_in_t, g_w_out = g_b[:w_shard_cols], g_b[D_MODEL:GB_ROWS]
    g_w_ff1, g_w_ff2 = g_a[R_FF1:R_FF2], g_a[R_FF2:GA_ROWS]
    d_w_in_t, m_w_in2_t, v_w_in2_t = _adamw(w_in_t, g_w_in_t, m_w_in_t, v_w_in_t, "adamw_w_in")
    g_w_in, d_w_in, m_w_in2, v_w_in2 = g_w_in_t.T, d_w_in_t.T, m_w_in2_t.T, v_w_in2_t.T
    d_w_out, m_w_out2, v_w_out2 = _adamw(w_out[0], g_w_out, m_w_out[0], v_w_out[0], "adamw_w_out")
    d_w_ff1, m_w_ff12, v_w_ff12 = _adamw(w_ff1[0], g_w_ff1, m_w_ff1[0], v_w_ff1[0], "adamw_w_ff1")
    d_w_ff2, m_w_ff22, v_w_ff22 = _adamw(w_ff2[0], g_w_ff2, m_w_ff2[0], v_w_ff2[0], "adamw_w_ff2")

    def small_out(t, name, like):
        return t[name][:like.size].reshape(like.shape)

    def outputs(t, big):
        return [small_out(t, "ln_in_g", ln_in_g), small_out(t, "ln_in_b", ln_in_b), big["ada_w"][None],
                small_out(t, "ada_b", ada_b), big["w_in"][None], big["conv_w"][None], small_out(t, "conv_b", conv_b),
                small_out(t, "dt_bias", dt_bias), small_out(t, "a_log", a_log), small_out(t, "d_skip", d_skip),
                small_out(t, "ssd_norm_w", ssd_norm_w), small_out(t, "attn_sinks", attn_sinks), big["w_out"][None],
                small_out(t, "ln1_g", ln1_g), small_out(t, "ln1_b", ln1_b), big["w_ff1"][None], small_out(t, "b_ff1", b_ff1),
                big["w_ff2"][None], small_out(t, "b_ff2", b_ff2), small_out(t, "ln2_g", ln2_g), small_out(t, "ln2_b", ln2_b)]

    grads = outputs(gs, dict(ada_w=g_ada, w_in=g_w_in, conv_w=g_conv, w_out=g_w_out, w_ff1=g_w_ff1, w_ff2=g_w_ff2))
    deltas = outputs(dl, dict(ada_w=d_ada, w_in=d_w_in, conv_w=d_conv, w_out=d_w_out, w_ff1=d_w_ff1, w_ff2=d_w_ff2))
    new_m = outputs(m2, dict(ada_w=m_ada, w_in=m_w_in2, conv_w=m_conv, w_out=m_w_out2, w_ff1=m_w_ff12, w_ff2=m_w_ff22))
    new_v = outputs(v2, dict(ada_w=v_ada, w_in=v_w_in2, conv_w=v_conv, w_out=v_w_out2, w_ff1=v_w_ff12, w_ff2=v_w_ff22))
    return (loss, grad_x[None], *grads, *deltas, *new_m, *new_v)
```

```python
import functools
import math

import numpy as np
import jax
import jax.numpy as jnp
from jax import lax
from jax.experimental import pallas as pl
from jax.experimental.pallas import tpu as pltpu

f32 = jnp.float32
bf16 = jnp.bfloat16

D_MODEL = 1024
SSD_WIDTH = 1024
SSD_HEADS = 16
HEAD_DIM = 64
SSD_STATE = 128
SSD_GROUPS = 2
CHUNK = 128
CONV_K = 4
CONV_DIM = 1536
ATTN_HEADS = 16
D_FF = 4096
PROJ_WIDTH = 3856
ALPHA = 2.0 ** 0.25
LN_EPS = 1e-5
RMS_EPS = 1e-5
ATTN_SCALE = HEAD_DIM ** -0.5
NEG = -1e30

ADAM_LR = 0.001
ADAM_B1 = 0.9
ADAM_B2 = 0.999
ADAM_EPS = 1e-08
ADAM_WD = 0.01
ADAM_STEP = 10

C_Z, C_XBC, C_Q, C_KV, C_DT, C_END = 0, 1024, 2560, 3584, 3840, 3968
GA_ROWS = 2048
GB_ROWS = 1536
WG_TM = 512
STAGE_ROWS = 512
N_CHIPS = 4
N_DEV = 8
VMEM_LIMIT = 56 * 1024 * 1024
MESH = pl.DeviceIdType.MESH

ALIBI_SLOPES = tuple(2.0 ** (-8.0 / ATTN_HEADS * (i + 1)) for i in range(ATTN_HEADS))


def _cparams(*sem):
    return pltpu.CompilerParams(dimension_semantics=sem, vmem_limit_bytes=VMEM_LIMIT)


def _sigmoid(x):
    return 1.0 / (1.0 + jnp.exp(-x))


def _softplus(x):
    return jnp.maximum(x, 0.0) + jnp.log1p(jnp.exp(-jnp.abs(x)))


def _ln_stats(x):
    mu = jnp.mean(x, axis=-1, keepdims=True)
    xc = x - mu
    var = jnp.mean(xc * xc, axis=-1, keepdims=True)
    rstd = lax.rsqrt(var + LN_EPS)
    return xc * rstd, rstd


def _ln_bwd(dy, xhat, rstd, g):
    dxh = dy * g
    m1 = jnp.mean(dxh, axis=-1, keepdims=True)
    m2 = jnp.mean(dxh * xhat, axis=-1, keepdims=True)
    return rstd * (dxh - m1 - xhat * m2)


def _dot(a, b):
    return jnp.dot(a, b, preferred_element_type=f32)


def _dot_nt(a, b):
    return lax.dot_general(a, b, (((1,), (1,)), ((), ())), preferred_element_type=f32)


def _dot_tn(a, b):
    return lax.dot_general(a, b, (((0,), (0,)), ((), ())), preferred_element_type=f32)


def _dot_exact(a, b):
    return jnp.dot(a, b, preferred_element_type=f32, precision=lax.Precision.HIGHEST)


def _split3(v):
    hi = v.astype(bf16)
    r1 = v - hi.astype(f32)
    mid = r1.astype(bf16)
    lo = (r1 - mid.astype(f32)).astype(bf16)
    return hi, mid, lo


def _sel_dot(sel, v):
    hi, mid, lo = _split3(v)
    return _dot(sel, hi) + _dot(sel, mid) + _dot(sel, lo)


def _dot_sel(v, sel):
    hi, mid, lo = _split3(v)
    return _dot(hi, sel) + _dot(mid, sel) + _dot(lo, sel)


def _dot_sel_nt(v, sel):
    hi, mid, lo = _split3(v)
    return _dot_nt(hi, sel) + _dot_nt(mid, sel) + _dot_nt(lo, sel)


def _full(shape):
    nd = len(shape)
    return pl.BlockSpec(shape, lambda *_: (0,) * nd)


def _resident(shape):
    nd = len(shape)
    return pl.BlockSpec(shape, lambda *_: (0,) * nd, pipeline_mode=pl.Buffered(1))


def _rows(tm, n):
    return pl.BlockSpec((tm, n), lambda i: (i, 0))


def _inproj_fwd(x, mod, ln_g, ln_b, w_in_t, conv_w, conv_b, blob):
    L = x.shape[0]
    tm = 256
    nt = L // tm
    R = blob.shape[0]

    def body(x_ref, mod_ref, g_ref, b_ref, w_ref, cw_ref, cb_ref, blob_ref,
             u1_ref, z_ref, xr_ref, xc_ref, q_ref, kv_ref, dt_ref, wall_ref, halo, buf, send_sems, recv_sems):
        start, finish = _gather_job(blob_ref, wall_ref, send_sems, recv_sems, R)

        @pl.when(pl.program_id(0) == 0)
        def _():
            halo[...] = jnp.zeros_like(halo)
            start()

        xhat, _ = _ln_stats(x_ref[...])
        h0 = xhat * g_ref[...] + b_ref[...]
        u1 = (h0 * (1.0 + mod_ref[1:2, :]) + mod_ref[0:1, :]).astype(bf16)
        u1_ref[...] = u1
        z_ref[...] = _dot_nt(u1, w_ref[C_Z:C_XBC, :])
        xr = _dot_nt(u1, w_ref[C_XBC:C_Q, :])
        xr_ref[...] = xr
        q_ref[...] = _dot_nt(u1, w_ref[C_Q:C_KV, :]).astype(bf16)
        kv_ref[...] = _dot_nt(u1, w_ref[C_KV:C_DT, :]).astype(bf16)
        dt_ref[...] = _dot_nt(u1, w_ref[C_DT:C_END, :])
        buf[0:8, :] = halo[...]
        buf[8:8 + tm, :] = xr
        pre = cb_ref[...] + cw_ref[0:1, :] * buf[5:5 + tm, :]
        for k in range(1, CONV_K):
            pre = pre + cw_ref[k:k + 1, :] * buf[5 + k:5 + k + tm, :]
        xc_ref[...] = pre * _sigmoid(pre)
        halo[...] = xr[tm - 8:tm, :]

        @pl.when(pl.program_id(0) == nt - 1)
        def _():
            finish()

    return pl.pallas_call(
        body, name="inproj_fwd", grid=(nt,),
        in_specs=[_rows(tm, D_MODEL), _full((8, D_MODEL)), _full((1, D_MODEL)), _full((1, D_MODEL)),
                  _resident((C_END, D_MODEL)), _full((CONV_K, CONV_DIM)), _full((1, CONV_DIM)), _ANY_SPEC],
        out_specs=[_rows(tm, D_MODEL), _rows(tm, D_MODEL), _rows(tm, CONV_DIM), _rows(tm, CONV_DIM),
                   _rows(tm, D_MODEL), _rows(tm, 256), _rows(tm, 128), _ANY_SPEC],
        out_shape=[jax.ShapeDtypeStruct((L, D_MODEL), bf16), jax.ShapeDtypeStruct((L, D_MODEL), f32),
                   jax.ShapeDtypeStruct((L, CONV_DIM), f32), jax.ShapeDtypeStruct((L, CONV_DIM), f32),
                   jax.ShapeDtypeStruct((L, D_MODEL), bf16), jax.ShapeDtypeStruct((L, 256), bf16),
                   jax.ShapeDtypeStruct((L, 128), f32), jax.ShapeDtypeStruct((N_CHIPS, R, D_MODEL), bf16)],
        scratch_shapes=[pltpu.VMEM((8, CONV_DIM), f32), pltpu.VMEM((tm + 8, CONV_DIM), f32),
                        pltpu.SemaphoreType.DMA((6,)), pltpu.SemaphoreType.DMA((6,))],
        compiler_params=_cparams("arbitrary"),
    )(x, mod, ln_g, ln_b, w_in_t, conv_w, conv_b, blob)


def _head_expand():
    e = np.zeros((128, SSD_WIDTH), np.float32)
    for h in range(SSD_HEADS):
        e[h, h * HEAD_DIM:(h + 1) * HEAD_DIM] = 1.0
    return jnp.asarray(e, dtype=bf16)


def _ssd_chunk_common(dt_raw, dtb, a_row, e_mat):
    T = CHUNK
    lane = lax.broadcasted_iota(jnp.int32, (T, 128), 1)
    dt = jnp.where(lane < SSD_HEADS, _softplus(dt_raw + dtb), 0.0)
    a = dt * a_row
    r = lax.broadcasted_iota(jnp.int32, (T, T), 0)
    c = lax.broadcasted_iota(jnp.int32, (T, T), 1)
    tril = (c <= r).astype(bf16)
    cum = _sel_dot(tril, a)
    dtx = _dot_sel(dt, e_mat)
    cumx = _dot_sel(cum, e_mat)
    return dt, a, cum, dtx, cumx, r, c


def _ssd_fwd(xc, dt_raw, dt_bias, a_log, d_skip_x, e_mat, blob):
    L = xc.shape[0]
    nc = L // CHUNK
    T = CHUNK
    R = blob.shape[0]

    def body(xc_ref, dt_ref, dtb_ref, al_ref, dsk_ref, e_ref, blob_ref, y_ref, prev_ref, wall_ref, st, send_sems, recv_sems):
        start, finish = _gather_job(blob_ref, wall_ref, send_sems, recv_sems, R)

        @pl.when(pl.program_id(0) == 0)
        def _():
            st[...] = jnp.zeros_like(st)
            start()

        @pl.when(pl.program_id(0) == nc - 1)
        def _():
            finish()

        a_row = -jnp.exp(al_ref[...])
        lane1 = lax.broadcasted_iota(jnp.int32, (1, 128), 1)
        a_row = jnp.where(lane1 < SSD_HEADS, a_row, 0.0)
        dt, a, cum, dtx, cumx, r, c = _ssd_chunk_common(dt_ref[...], dtb_ref[...], a_row, e_ref[...])
        cum_t = cum.T
        ex = jnp.exp(cumx)
        last = cumx[T - 1:T, :]
        wx = jnp.exp(last - cumx)
        cdx = jnp.exp(last)
        xs = xc_ref[:, 0:SSD_WIDTH]
        X = xs * dtx
        Xb = X.astype(bf16)
        Xd = (X * wx).astype(bf16)
        prev = st[...]
        prev_ref[0] = prev
        prevb = prev.astype(bf16)
        tri = c <= r
        lane = lax.broadcasted_iota(jnp.int32, (T, 128), 1)
        y_blocks = []
        new_states = []
        for g in range(SSD_GROUPS):
            Bg = xc_ref[:, 1024 + 128 * g:1152 + 128 * g].astype(bf16)
            Cg = xc_ref[:, 1280 + 128 * g:1408 + 128 * g].astype(bf16)
            G = _dot_nt(Cg, Bg)
            yoff = _dot(Cg, prevb[:, 512 * g:512 * (g + 1)])
            new_states.append(_dot_tn(Bg, Xd[:, 512 * g:512 * (g + 1)]))
            for j in range(4):
                blk = 4 * g + j
                Xblk = Xb[:, 128 * blk:128 * (blk + 1)]
                ys = []
                for half in range(2):
                    h = 2 * blk + half
                    seg = jnp.minimum(cum[:, h:h + 1] - cum_t[h:h + 1, :], 0.0)
                    M = jnp.where(tri, G * jnp.exp(seg), 0.0).astype(bf16)
                    ys.append(_dot(M, Xblk))
                yd = jnp.where(lane < HEAD_DIM, ys[0], ys[1])
                sl = slice(128 * blk, 128 * (blk + 1))
                y_blocks.append(yd + ex[:, sl] * yoff[:, 128 * j:128 * (j + 1)] + dsk_ref[:, sl] * xs[:, sl])
        y_ref[...] = jnp.concatenate(y_blocks, axis=1)
        st[...] = prev * cdx + jnp.concatenate(new_states, axis=1)

    return pl.pallas_call(
        body, name="ssd_fwd", grid=(nc,),
        in_specs=[_rows(T, CONV_DIM), _rows(T, 128), _full((1, 128)), _full((1, 128)), _full((1, SSD_WIDTH)),
                  _full((128, SSD_WIDTH)), _ANY_SPEC],
        out_specs=[_rows(T, SSD_WIDTH), pl.BlockSpec((1, SSD_STATE, SSD_WIDTH), lambda i: (i, 0, 0)), _ANY_SPEC],
        out_shape=[jax.ShapeDtypeStruct((L, SSD_WIDTH), f32), jax.ShapeDtypeStruct((nc, SSD_STATE, SSD_WIDTH), f32),
                   jax.ShapeDtypeStruct((N_CHIPS, R, D_MODEL), bf16)],
        scratch_shapes=[pltpu.VMEM((SSD_STATE, SSD_WIDTH), f32)] + _sems(6),
        compiler_params=_cparams("arbitrary"),
    )(xc, dt_raw, dt_bias, a_log, d_skip_x, e_mat, blob)


def _kv_halves(kv_prev, kv_cur, first):
    kv = jnp.concatenate([jnp.where(first, 0.0, kv_prev.astype(f32)), kv_cur.astype(f32)], axis=0)
    lane = lax.broadcasted_iota(jnp.int32, (2 * CHUNK, 128), 1)
    lo = lane < HEAD_DIM
    out = []
    for g in range(2):
        per_half = []
        for half in range(2):
            both = []
            for t in (kv[:, 0:128], kv[:, 128:256]):
                src = t if g == half else pltpu.roll(t, HEAD_DIM, 1)
                both.append(jnp.where(lo if half == 0 else ~lo, src, 0.0).astype(bf16))
            per_half.append(tuple(both))
        out.append(per_half)
    return out


def _attn_masks(first):
    r = lax.broadcasted_iota(jnp.int32, (CHUNK, 2 * CHUNK), 0)
    c = lax.broadcasted_iota(jnp.int32, (CHUNK, 2 * CHUNK), 1)
    dist = r + CHUNK - c
    valid = (dist >= 0) & (dist < CHUNK) & ((c >= CHUNK) | jnp.logical_not(first))
    return dist.astype(f32), valid


def _attn_fwd(q, kv, sinks, blob):
    L = q.shape[0]
    nb = L // CHUNK
    T = CHUNK
    R = blob.shape[0]

    def body(sink_ref, q_ref, kvp_ref, kvc_ref, blob_ref, o_ref, lse_ref, wall_ref, send_sems, recv_sems):
        first = pl.program_id(0) == 0
        start, finish = _gather_job(blob_ref, wall_ref, send_sems, recv_sems, R)

        @pl.when(first)
        def _():
            start()

        @pl.when(pl.program_id(0) == nb - 1)
        def _():
            finish()

        ext = _kv_halves(kvp_ref[...], kvc_ref[...], first)
        dist, valid = _attn_masks(first)
        lane = lax.broadcasted_iota(jnp.int32, (T, 128), 1)
        lse = jnp.zeros((T, 128), f32)
        o_blocks = []
        for blk in range(8):
            qb = q_ref[:, 128 * blk:128 * (blk + 1)]
            acc = None
            for half in range(2):
                h = 2 * blk + half
                k_ext, v_ext = ext[h // 8][half]
                s = _dot_nt(qb, k_ext) * ATTN_SCALE - ALIBI_SLOPES[h] * dist
                s = jnp.where(valid, s, NEG)
                sink = sink_ref[h]
                m = jnp.maximum(jnp.max(s, axis=-1, keepdims=True), sink)
                p = jnp.exp(s - m)
                den = jnp.sum(p, axis=-1, keepdims=True) + jnp.exp(sink - m)
                pn = (p * (1.0 / den)).astype(bf16)
                oh = _dot(pn, v_ext)
                acc = oh if acc is None else acc + oh
                lse = jnp.where(lane == h, m + jnp.log(den), lse)
            o_blocks.append(acc.astype(bf16))
        o_ref[...] = jnp.concatenate(o_blocks, axis=1)
        lse_ref[...] = lse

    return pl.pallas_call(
        body, name="attn_fwd", grid=(nb,),
        in_specs=[pl.BlockSpec(memory_space=pltpu.SMEM), _rows(T, D_MODEL),
                  pl.BlockSpec((T, 256), lambda i: (jnp.maximum(i - 1, 0), 0)), _rows(T, 256), _ANY_SPEC],
        out_specs=[_rows(T, D_MODEL), _rows(T, 128), _ANY_SPEC],
        out_shape=[jax.ShapeDtypeStruct((L, D_MODEL), bf16), jax.ShapeDtypeStruct((L, 128), f32),
                   jax.ShapeDtypeStruct((N_CHIPS, R, D_MODEL), bf16)],
        scratch_shapes=_sems(6),
        compiler_params=_cparams("arbitrary"),
    )(sinks, q, kv, kv, blob)


def _gated_norm(y, z, w):
    sz = _sigmoid(z)
    hg = y * (z * sz)
    ns, rss = [], []
    for g in range(SSD_GROUPS):
        hs = hg[:, 512 * g:512 * (g + 1)]
        rs = lax.rsqrt(jnp.mean(hs * hs, axis=-1, keepdims=True) + RMS_EPS)
        ns.append(hs * rs)
        rss.append(rs)
    n = jnp.concatenate(ns, axis=1)
    return n * w, n, rss, sz


def _outproj_fwd(y, z, o, x, mod, ln_g, ln_b, norm_w, w_out):
    L = x.shape[0]
    tm = 256

    def body(y_ref, z_ref, o_ref, x_ref, mod_ref, g_ref, b_ref, nw_ref, w_ref, yn_ref, mix_ref, r1_ref):
        yn, _, _, _ = _gated_norm(y_ref[...], z_ref[...], nw_ref[...])
        ynb = yn.astype(bf16)
        yn_ref[...] = ynb
        mix = (_dot(ynb[:, 0:512], w_ref[0]) + _dot(ynb[:, 512:1024], w_ref[1])
               + _dot(o_ref[:, 0:512], w_ref[2]) + _dot(o_ref[:, 512:1024], w_ref[3]))
        mix_ref[...] = mix
        xhat, _ = _ln_stats(x_ref[...])
        h0 = xhat * g_ref[...] + b_ref[...]
        r1_ref[...] = ALPHA * h0 + (1.0 + mod_ref[2:3, :]) * mix

    v = _full((1, D_MODEL))
    return pl.pallas_call(
        body, name="outproj_fwd", grid=(L // tm,),
        in_specs=[_rows(tm, D_MODEL), _rows(tm, D_MODEL), _rows(tm, D_MODEL), _rows(tm, D_MODEL),
                  _full((8, D_MODEL)), v, v, v, _resident((N_CHIPS, 512, D_MODEL))],
        out_specs=[_rows(tm, D_MODEL)] * 3,
        out_shape=[jax.ShapeDtypeStruct((L, D_MODEL), bf16), jax.ShapeDtypeStruct((L, D_MODEL), f32),
                   jax.ShapeDtypeStruct((L, D_MODEL), f32)],
        compiler_params=_cparams("parallel"),
    )(y, z, o, x, mod, ln_g, ln_b, norm_w, w_out)


A_LN2G, A_LN2B, A_G2, A_B2, A_SC2, A_SH2, A_LN1G, A_LN1B, A_LOSS = range(9)


def _mlp_fwd_bwd(r1, target, mod, ln1_g, ln1_b, ln2_g, ln2_b, w1, b1, w2, b2):
    L = r1.shape[0]
    tm = 256
    nj = D_FF // 1024

    def body(r1_ref, t_ref, mod_ref, g1_ref, bb1_ref, g2_ref, bb2_ref, w1_ref, b1_ref, w2_ref, b2_ref,
             dr1_ref, u2_ref, s_ref, da_ref, df_ref, acc_ref, db1_ref, hr):
        @pl.when(pl.program_id(0) == 0)
        def _():
            acc_ref[...] = jnp.zeros_like(acc_ref)
            db1_ref[...] = jnp.zeros_like(db1_ref)

        sc2, sh2, gate2 = mod_ref[4:5, :], mod_ref[3:4, :], mod_ref[5:6, :]
        xhat1, rstd1 = _ln_stats(r1_ref[...])
        h1 = xhat1 * g1_ref[...] + bb1_ref[...]
        u2f = h1 * (1.0 + sc2) + sh2
        u2 = u2f.astype(bf16)
        u2_ref[...] = u2
        f = jnp.zeros((tm, D_MODEL), f32) + b2_ref[...]
        for j in range(nj):
            cs = slice(1024 * j, 1024 * (j + 1))
            a = _dot(u2, w1_ref[j]) + b1_ref[:, cs]
            hrj = jnp.maximum(a, 0.0)
            hr[:, cs] = hrj
            sj = (hrj * hrj).astype(bf16)
            s_ref[:, cs] = sj
            f = f + _dot(sj, w2_ref[j])
        r2 = ALPHA * h1 + (1.0 + gate2) * f
        xhat2, rstd2 = _ln_stats(r2)
        h2 = xhat2 * g2_ref[...] + bb2_ref[...]
        diff = h2 - t_ref[...]
        dh2 = diff * (1.0 / D_MODEL)

        def add(row, val):
            acc_ref[row:row + 1, :] += jnp.sum(val, axis=0, keepdims=True)

        add(A_LOSS, diff * diff * (0.5 / D_MODEL))
        add(A_LN2G, dh2 * xhat2)
        add(A_LN2B, dh2)
        dr2 = _ln_bwd(dh2, xhat2, rstd2, g2_ref[...])
        add(A_G2, dr2 * f)
        df = dr2 * (1.0 + gate2)
        add(A_B2, df)
        dfb = df.astype(bf16)
        df_ref[...] = dfb
        du2 = jnp.zeros((tm, D_MODEL), f32)
        for j in range(nj):
            cs = slice(1024 * j, 1024 * (j + 1))
            ds = _dot_nt(dfb, w2_ref[j])
            daj = ds * (2.0 * hr[:, cs])
            db1_ref[:, cs] += jnp.sum(daj, axis=0, keepdims=True)
            dajb = daj.astype(bf16)
            da_ref[:, cs] = dajb
            du2 = du2 + _dot_nt(dajb, w1_ref[j])
        add(A_SC2, du2 * h1)
        add(A_SH2, du2)
        dh1 = ALPHA * dr2 + du2 * (1.0 + sc2)
        add(A_LN1G, dh1 * xhat1)
        add(A_LN1B, dh1)
        dr1_ref[...] = _ln_bwd(dh1, xhat1, rstd1, g1_ref[...])

    v = _full((1, D_MODEL))
    return pl.pallas_call(
        body, name="mlp_fwd_bwd", grid=(L // tm,),
        in_specs=[_rows(tm, D_MODEL), _rows(tm, D_MODEL), _full((8, D_MODEL)), v, v, v, v,
                  _resident((N_CHIPS, D_MODEL, D_MODEL)), _full((1, D_FF)), _resident((N_CHIPS, D_MODEL, D_MODEL)), v],
        out_specs=[_rows(tm, D_MODEL), _rows(tm, D_MODEL), _rows(tm, D_FF), _rows(tm, D_FF), _rows(tm, D_MODEL),
                   _full((16, D_MODEL)), _full((1, D_FF))],
        out_shape=[jax.ShapeDtypeStruct((L, D_MODEL), f32), jax.ShapeDtypeStruct((L, D_MODEL), bf16),
                   jax.ShapeDtypeStruct((L, D_FF), bf16), jax.ShapeDtypeStruct((L, D_FF), bf16),
                   jax.ShapeDtypeStruct((L, D_MODEL), bf16), jax.ShapeDtypeStruct((16, D_MODEL), f32),
                   jax.ShapeDtypeStruct((1, D_FF), f32)],
        scratch_shapes=[pltpu.VMEM((tm, D_FF), f32)],
        compiler_params=_cparams("arbitrary"),
    )(r1, target, mod, ln1_g, ln1_b, ln2_g, ln2_b, w1, b1, w2, b2)


def _wgrad(a, b, name):
    L, M = a.shape
    N = b.shape[1]
    tm = min(M, 512)
    tn = next(t for t in (1024, 768, 512, 256, 128) if N % t == 0)

    def body(a_ref, b_ref, o_ref):
        o_ref[...] = _dot_tn(a_ref[...], b_ref[...]).astype(bf16)

    return pl.pallas_call(
        body, name=name, grid=(M // tm, N // tn),
        in_specs=[pl.BlockSpec((L, tm), lambda i, j: (0, i)), pl.BlockSpec((L, tn), lambda i, j: (0, j))],
        out_specs=pl.BlockSpec((tm, tn), lambda i, j: (i, j)),
        out_shape=jax.ShapeDtypeStruct((M, N), bf16),
        compiler_params=_cparams("parallel", "parallel"),
    )(a, b)


def _wgrad_blob(blob, a, b, name, place_of):
    L, M = a.shape
    N = b.shape[1]

    def body(blob_ref, a_ref, b_ref, o_ref):
        o_ref[0] = _dot_tn(a_ref[...], b_ref[...]).astype(bf16)

    return pl.pallas_call(
        body, name=name, grid=(M // WG_TM, N // D_MODEL),
        in_specs=[pl.BlockSpec(memory_space=pl.ANY), pl.BlockSpec((L, WG_TM), lambda t, n: (0, t)),
                  pl.BlockSpec((L, D_MODEL), lambda t, n: (0, n))],
        out_specs=pl.BlockSpec((1, WG_TM, D_MODEL), lambda t, n: (*place_of(t, n), 0)),
        out_shape=jax.ShapeDtypeStruct(blob.shape, bf16), input_output_aliases={0: 0},
        compiler_params=_cparams("parallel", "parallel"),
    )(blob, a, b)


def _outproj_bwd(dr1, mix, y, z, mod, norm_w, w_out, gb):
    L = dr1.shape[0]
    tm = 256
    nt = L // tm

    def body(dr1_ref, mix_ref, y_ref, z_ref, mod_ref, nw_ref, w_ref, gb_ref,
             dy_ref, dz_ref, do_ref, dmix_ref, acc_ref, sib_ref, send_sems, recv_sems):
        start, wait = _to_sibling_job(gb_ref, sib_ref, send_sems, recv_sems)

        @pl.when(pl.program_id(0) == 0)
        def _():
            acc_ref[...] = jnp.zeros_like(acc_ref)
            start()

        @pl.when(pl.program_id(0) == nt - 1)
        def _():
            wait()

        dr1 = dr1_ref[...]
        acc_ref[0:1, :] += jnp.sum(dr1 * mix_ref[...], axis=0, keepdims=True)
        dmix = (dr1 * (1.0 + mod_ref[2:3, :])).astype(bf16)
        dmix_ref[...] = dmix
        dyn = jnp.concatenate([_dot_nt(dmix, w_ref[0]), _dot_nt(dmix, w_ref[1])], axis=1)
        do_ref[...] = jnp.concatenate([_dot_nt(dmix, w_ref[2]), _dot_nt(dmix, w_ref[3])], axis=1).astype(bf16)
        yv, zv = y_ref[...], z_ref[...]
        _, n, rss, sz = _gated_norm(yv, zv, nw_ref[...])
        acc_ref[1:2, :] += jnp.sum(dyn * n, axis=0, keepdims=True)
        dn = dyn * nw_ref[...]
        parts = []
        for g in range(SSD_GROUPS):
            sl = slice(512 * g, 512 * (g + 1))
            dng, ng = dn[:, sl], n[:, sl]
            parts.append(rss[g] * (dng - ng * jnp.mean(dng * ng, axis=-1, keepdims=True)))
        dhg = jnp.concatenate(parts, axis=1)
        dy_ref[...] = dhg * (zv * sz)
        dz_ref[...] = (dhg * yv * (sz * (1.0 + zv * (1.0 - sz)))).astype(bf16)

    return pl.pallas_call(
        body, name="outproj_bwd", grid=(nt,),
        in_specs=[_rows(tm, D_MODEL)] * 4 + [_full((8, D_MODEL)), _full((1, D_MODEL)), _resident((N_CHIPS, 512, D_MODEL)),
                  _ANY_SPEC],
        out_specs=[_rows(tm, D_MODEL)] * 4 + [_full((8, D_MODEL)), _ANY_SPEC],
        out_shape=[jax.ShapeDtypeStruct((L, D_MODEL), f32)] + [jax.ShapeDtypeStruct((L, D_MODEL), bf16)] * 3
        + [jax.ShapeDtypeStruct((8, D_MODEL), f32), jax.ShapeDtypeStruct((N_CHIPS,) + gb.shape[2:], bf16)],
        scratch_shapes=_sems(N_CHIPS),
        compiler_params=_cparams("arbitrary"),
    )(dr1, mix, y, z, mod, norm_w, w_out, gb)


def _attn_bwd(q, kv, do, lse, sinks, pb):
    L = q.shape[0]
    nb = L // CHUNK
    T = CHUNK

    def body(sink_ref, q_ref, kvp_ref, kvc_ref, do_ref, lse_ref, pb_ref, dq_ref, dkv_ref, dsink_ref, chips_ref,
             carry, send_sems, recv_sems):
        n = pl.program_id(0)
        start, wait = _to_chips_job(pb_ref, chips_ref, send_sems, recv_sems)

        @pl.when(n == 0)
        def _():
            carry[...] = jnp.zeros_like(carry)
            dsink_ref[...] = jnp.zeros_like(dsink_ref)
            start()

        @pl.when(n < nb)
        def _():
            first = n == 0
            ext = _kv_halves(kvp_ref[...], kvc_ref[...], first)
            dist, valid = _attn_masks(first)
            lane1 = lax.broadcasted_iota(jnp.int32, (1, 128), 1)
            lse = lse_ref[...]
            acck = [None, None]
            accv = [None, None]
            dsink = jnp.zeros((1, 128), f32)
            dq_blocks = []
            for blk in range(8):
                qb = q_ref[:, 128 * blk:128 * (blk + 1)]
                dob = do_ref[:, 128 * blk:128 * (blk + 1)]
                qt = qb.astype(f32).T.astype(bf16)
                dot_ = dob.astype(f32).T.astype(bf16)
                dq_acc = None
                for half in range(2):
                    h = 2 * blk + half
                    g = h // 8
                    k_ext, v_ext = ext[g][half]
                    s = _dot_nt(qb, k_ext) * ATTN_SCALE - ALIBI_SLOPES[h] * dist
                    lse_h = lse[:, h:h + 1]
                    p = jnp.where(valid, jnp.exp(s - lse_h), 0.0)
                    dp = _dot_nt(dob, v_ext)
                    delta = jnp.sum(p * dp, axis=-1, keepdims=True)
                    ds = (p * (dp - delta) * ATTN_SCALE).astype(bf16)
                    psink = jnp.exp(sink_ref[h] - lse_h)
                    dsink = dsink - jnp.where(lane1 == h, jnp.sum(psink * delta, axis=0, keepdims=True), 0.0)
                    dqh = _dot(ds, k_ext)
                    dq_acc = dqh if dq_acc is None else dq_acc + dqh
                    rows = slice(HEAD_DIM * half, HEAD_DIM * (half + 1))
                    dkh = _dot(qt[rows, :], ds)
                    dvh = _dot(dot_[rows, :], p.astype(bf16))
                    acck[g] = dkh if acck[g] is None else acck[g] + dkh
                    accv[g] = dvh if accv[g] is None else accv[g] + dvh
                dq_blocks.append(dq_acc.astype(bf16))
            dq_ref[...] = jnp.concatenate(dq_blocks, axis=1)
            dsink_ref[...] += dsink
            dkv = jnp.concatenate([jnp.concatenate(acck, axis=0).T, jnp.concatenate(accv, axis=0).T], axis=1)
            dkv_ref[...] = (carry[...] + dkv[0:T, :]).astype(bf16)
            carry[...] = dkv[T:2 * T, :]

        @pl.when(n == nb)
        def _():
            dkv_ref[...] = carry[...].astype(bf16)
            wait()

    cur = lambda i: (jnp.minimum(i, nb - 1), 0)
    return pl.pallas_call(
        body, name="attn_bwd", grid=(nb + 1,),
        in_specs=[pl.BlockSpec(memory_space=pltpu.SMEM), pl.BlockSpec((T, D_MODEL), cur),
                  pl.BlockSpec((T, 256), lambda i: (jnp.maximum(jnp.minimum(i, nb - 1) - 1, 0), 0)),
                  pl.BlockSpec((T, 256), cur), pl.BlockSpec((T, D_MODEL), cur), pl.BlockSpec((T, 128), cur), _ANY_SPEC],
        out_specs=[pl.BlockSpec((T, D_MODEL), cur), pl.BlockSpec((T, 256), lambda i: (jnp.maximum(i - 1, 0), 0)),
                   _full((1, 128)), _ANY_SPEC],
        out_shape=[jax.ShapeDtypeStruct((L, D_MODEL), bf16), jax.ShapeDtypeStruct((L, 256), bf16),
                   jax.ShapeDtypeStruct((1, 128), f32), jax.ShapeDtypeStruct((N_CHIPS - 1,) + pb.shape[1:], bf16)],
        scratch_shapes=[pltpu.VMEM((T, 256), f32)] + _sems(N_CHIPS - 1),
        compiler_params=_cparams("arbitrary"),
    )(sinks, q, kv, kv, do, lse, pb)


def _ssd_bwd(xc, dt_raw, dy, prev_all, dt_bias, a_log, d_skip_x, e_mat, g):
    L = xc.shape[0]
    nc = L // CHUNK
    T = CHUNK
    RG = g.shape[0]

    def body(xc_ref, dt_ref, dy_ref, prev_ref, dtb_ref, al_ref, dsk_ref, e_ref, g_in_ref,
             dxc_ref, ddt_ref, acc_ref, dd_ref, g_ref, dst, dxs_s, send_sems, recv_sems):
        start, wait = _share_job(g_ref, send_sems, recv_sems, RG)

        @pl.when(pl.program_id(0) == 0)
        def _():
            dst[...] = jnp.zeros_like(dst)
            acc_ref[...] = jnp.zeros_like(acc_ref)
            dd_ref[...] = jnp.zeros_like(dd_ref)
            start()

        @pl.when(pl.program_id(0) == nc - 1)
        def _():
            wait()

        lane1 = lax.broadcasted_iota(jnp.int32, (1, 128), 1)
        a_row = jnp.where(lane1 < SSD_HEADS, -jnp.exp(al_ref[...]), 0.0)
        e_mat_v = e_ref[...]
        dt, a, cum, dtx, cumx, r, c = _ssd_chunk_common(dt_ref[...], dtb_ref[...], a_row, e_mat_v)
        cum_t = cum.T
        ex = jnp.exp(cumx)
        last = cumx[T - 1:T, :]
        wx = jnp.exp(last - cumx)
        cdx = jnp.exp(last)
        xs = xc_ref[:, 0:SSD_WIDTH]
        X = xs * dtx
        Xb = X.astype(bf16)
        Xdb = (X * wx).astype(bf16)
        dyv = dy_ref[...]
        prev = prev_ref[0]
        prevb = prev.astype(bf16)
        dnew = dst[...]
        dnewb = dnew.astype(bf16)
        tri = c <= r
        lane = lax.broadcasted_iota(jnp.int32, (T, 128), 1)
        sub = lax.broadcasted_iota(jnp.int32, (128, T), 0)
        lo = lane < HEAD_DIM

        def red(vals, g):
            return _dot_sel_nt(vals, e_mat_v[:, 512 * g:512 * (g + 1)])

        de = jnp.zeros((T, 128), f32)
        dw = jnp.zeros((T, 128), f32)
        ddt_x = jnp.zeros((T, 128), f32)
        dcum_col = jnp.zeros((T, 128), f32)
        dcum_row = jnp.zeros((128, T), f32)
        dprev_parts, dBs, dCs = [], [], []
        for g in range(SSD_GROUPS):
            s5 = slice(512 * g, 512 * (g + 1))
            Bg = xc_ref[:, 1024 + 128 * g:1152 + 128 * g].astype(bf16)
            Cg = xc_ref[:, 1280 + 128 * g:1408 + 128 * g].astype(bf16)
            G = _dot_nt(Cg, Bg)
            Z = _dot(Cg, prevb[:, s5])
            dyg = dyv[:, s5]
            dZb = (dyg * ex[:, s5]).astype(bf16)
            dXd = _dot(Bg, dnewb[:, s5])
            dC = _dot_nt(dZb, prevb[:, s5])
            dB = _dot_nt(Xdb[:, s5], dnewb[:, s5])
            dprev_parts.append(_dot_tn(Cg, dZb) + dnew[:, s5] * cdx[:, s5])
            de = de + red(dyg * Z, g)
            dw = dw + red(dXd * X[:, s5], g)
            dXg = dXd * wx[:, s5]
            dG = jnp.zeros((T, T), f32)
            for j in range(4):
                blk = 4 * g + j
                sl = slice(128 * blk, 128 * (blk + 1))
                Xblk = Xb[:, sl]
                dyblk = dyv[:, sl]
                dyblk_b = dyblk.astype(bf16)
                dxh = []
                for half in range(2):
                    h = 2 * blk + half
                    seg = jnp.minimum(cum[:, h:h + 1] - cum_t[h:h + 1, :], 0.0)
                    Lm = jnp.where(tri, jnp.exp(seg), 0.0)
                    M = G * Lm
                    dyh = jnp.where(lo if half == 0 else ~lo, dyblk, 0.0).astype(bf16)
                    dM = _dot_nt(dyh, Xblk)
                    dG = dG + dM * Lm
                    Q = dM * M
                    dcum_col = dcum_col + jnp.where(lane == h, jnp.sum(Q, axis=1, keepdims=True), 0.0)
                    dcum_row = dcum_row + jnp.where(sub == h, jnp.sum(Q, axis=0, keepdims=True), 0.0)
                    dxh.append(_dot_tn(M.astype(bf16), dyblk_b))
                dXblk = dXg[:, 128 * j:128 * (j + 1)] + jnp.where(lo, dxh[0], dxh[1])
                xsb = xs[:, sl]
                dxs_s[:, sl] = dXblk * dtx[:, sl] + dsk_ref[:, sl] * dyblk
                ddt_x = ddt_x + _dot_sel_nt(dXblk * xsb, e_mat_v[:, sl])
                dd_ref[:, sl] += jnp.sum(dyblk * xsb, axis=0, keepdims=True)
            dGb = dG.astype(bf16)
            dCs.append(dC + _dot(dGb, Bg))
            dBs.append(dB + _dot_tn(dGb, Cg))
        e16 = jnp.exp(cum)
        cum_last = cum[T - 1:T, :]
        w16 = jnp.exp(cum_last - cum)
        dcd = jnp.sum(dnew * prev, axis=0, keepdims=True)
        dcd16 = red(dcd[:, 0:512], 0) + red(dcd[:, 512:1024], 1)
        dww = dw * w16
        extra = jnp.sum(dww, axis=0, keepdims=True) + dcd16 * jnp.exp(cum_last)
        rowi = lax.broadcasted_iota(jnp.int32, (T, 128), 0)
        dcum = dcum_col - dcum_row.T + de * e16 - dww + jnp.where(rowi == T - 1, extra, 0.0)
        da = _sel_dot((c >= r).astype(bf16), dcum)
        ddt = ddt_x + da * a_row
        acc_ref[0:1, :] += jnp.sum(da * dt, axis=0, keepdims=True)
        ddt_raw = jnp.where(lane < SSD_HEADS, ddt * _sigmoid(dt_ref[...] + dtb_ref[...]), 0.0)
        ddt_ref[...] = ddt_raw
        acc_ref[1:2, :] += jnp.sum(ddt_raw, axis=0, keepdims=True)
        dxc_ref[:, 0:SSD_WIDTH] = dxs_s[...]
        dxc_ref[:, 1024:1280] = jnp.concatenate(dBs, axis=1)
        dxc_ref[:, 1280:1536] = jnp.concatenate(dCs, axis=1)
        dst[...] = jnp.concatenate(dprev_parts, axis=1)

    rev = lambda i: (nc - 1 - i, 0)
    return pl.pallas_call(
        body, name="ssd_bwd", grid=(nc,),
        in_specs=[pl.BlockSpec((T, CONV_DIM), rev), pl.BlockSpec((T, 128), rev), pl.BlockSpec((T, SSD_WIDTH), rev),
                  pl.BlockSpec((1, SSD_STATE, SSD_WIDTH), lambda i: (nc - 1 - i, 0, 0)),
                  _full((1, 128)), _full((1, 128)), _full((1, SSD_WIDTH)), _full((128, SSD_WIDTH)), _ANY_SPEC],
        out_specs=[pl.BlockSpec((T, CONV_DIM), rev), pl.BlockSpec((T, 128), rev), _full((8, 128)),
                   _full((1, SSD_WIDTH)), _ANY_SPEC],
        out_shape=[jax.ShapeDtypeStruct((L, CONV_DIM), f32), jax.ShapeDtypeStruct((L, 128), f32),
                   jax.ShapeDtypeStruct((8, 128), f32), jax.ShapeDtypeStruct((1, SSD_WIDTH), f32),
                   jax.ShapeDtypeStruct(g.shape, f32)],
        input_output_aliases={8: 4},
        scratch_shapes=[pltpu.VMEM((SSD_STATE, SSD_WIDTH), f32), pltpu.VMEM((T, SSD_WIDTH), f32)] + _sems(1),
        compiler_params=_cparams("arbitrary"),
    )(xc, dt_raw, dy, prev_all, dt_bias, a_log, d_skip_x, e_mat, g)


def _conv_bwd(dxc, xr, conv_w, conv_b):
    L = dxc.shape[0]
    tm = 256
    nt = L // tm

    def body(dxc_ref, xr_ref, xh_ref, cw_ref, cb_ref, dxr_ref, acc_ref, carry, buf, buf2):
        i = pl.program_id(0)

        @pl.when(i == 0)
        def _():
            carry[...] = jnp.zeros_like(carry)
            acc_ref[...] = jnp.zeros_like(acc_ref)

        buf[0:8, :] = jnp.where(i == nt - 1, 0.0, xh_ref[...])
        buf[8:8 + tm, :] = xr_ref[...]
        pre = cb_ref[...] + cw_ref[0:1, :] * buf[5:5 + tm, :]
        for k in range(1, CONV_K):
            pre = pre + cw_ref[k:k + 1, :] * buf[5 + k:5 + k + tm, :]
        sg = _sigmoid(pre)
        dpre = dxc_ref[...] * (sg * (1.0 + pre * (1.0 - sg)))
        acc_ref[4:5, :] += jnp.sum(dpre, axis=0, keepdims=True)
        for k in range(CONV_K):
            acc_ref[k:k + 1, :] += jnp.sum(dpre * buf[5 + k:5 + k + tm, :], axis=0, keepdims=True)
        buf2[0:tm, :] = dpre
        buf2[tm:tm + 8, :] = carry[...]
        du = cw_ref[0:1, :] * buf2[3:3 + tm, :]
        for k in range(1, CONV_K):
            du = du + cw_ref[k:k + 1, :] * buf2[3 - k:3 - k + tm, :]
        dxr_ref[...] = du.astype(bf16)
        carry[...] = dpre[0:8, :]

    rev = lambda i: (nt - 1 - i, 0)
    return pl.pallas_call(
        body, name="conv_bwd", grid=(nt,),
        in_specs=[pl.BlockSpec((tm, CONV_DIM), rev), pl.BlockSpec((tm, CONV_DIM), rev),
                  pl.BlockSpec((8, CONV_DIM), lambda i: (jnp.maximum((nt - 1 - i) * (tm // 8) - 1, 0), 0)),
                  _full((CONV_K, CONV_DIM)), _full((1, CONV_DIM))],
        out_specs=[pl.BlockSpec((tm, CONV_DIM), rev), _full((8, CONV_DIM))],
        out_shape=[jax.ShapeDtypeStruct((L, CONV_DIM), bf16), jax.ShapeDtypeStruct((8, CONV_DIM), f32)],
        scratch_shapes=[pltpu.VMEM((8, CONV_DIM), f32), pltpu.VMEM((tm + 8, CONV_DIM), f32),
                        pltpu.VMEM((tm + 8, CONV_DIM), f32)],
        compiler_params=_cparams("arbitrary"),
    )(dxc, xr, xr, conv_w, conv_b)


def _inproj_bwd(dz, dxr, dq, dkv, ddt, dr1, x, mod, ln_g, ln_b, w_in):
    L = x.shape[0]
    tm = 256

    def body(dz_ref, dxr_ref, dq_ref, dkv_ref, ddt_ref, dr1_ref, x_ref, mod_ref, g_ref, b_ref, w_ref, dx_ref, acc_ref):
        @pl.when(pl.program_id(0) == 0)
        def _():
            acc_ref[...] = jnp.zeros_like(acc_ref)

        du1 = (_dot(dz_ref[...], w_ref[C_Z:C_XBC, :]) + _dot(dxr_ref[...], w_ref[C_XBC:C_Q, :])
               + _dot(dq_ref[...], w_ref[C_Q:C_KV, :]) + _dot(dkv_ref[...], w_ref[C_KV:C_DT, :])
               + _dot(ddt_ref[...].astype(bf16), w_ref[C_DT:C_END, :]))
        xhat, rstd = _ln_stats(x_ref[...])
        h0 = xhat * g_ref[...] + b_ref[...]
        acc_ref[0:1, :] += jnp.sum(du1 * h0, axis=0, keepdims=True)
        acc_ref[1:2, :] += jnp.sum(du1, axis=0, keepdims=True)
        dh0 = du1 * (1.0 + mod_ref[1:2, :]) + ALPHA * dr1_ref[...]
        acc_ref[2:3, :] += jnp.sum(dh0 * xhat, axis=0, keepdims=True)
        acc_ref[3:4, :] += jnp.sum(dh0, axis=0, keepdims=True)
        dx_ref[...] = _ln_bwd(dh0, xhat, rstd, g_ref[...])

    v = _full((1, D_MODEL))
    return pl.pallas_call(
        body, name="inproj_bwd", grid=(L // tm,),
        in_specs=[_rows(tm, D_MODEL), _rows(tm, CONV_DIM), _rows(tm, D_MODEL), _rows(tm, 256), _rows(tm, 128),
                  _rows(tm, D_MODEL), _rows(tm, D_MODEL), _full((8, D_MODEL)), v, v, _resident((C_END, D_MODEL))],
        out_specs=[_rows(tm, D_MODEL), _full((8, D_MODEL))],
        out_shape=[jax.ShapeDtypeStruct((L, D_MODEL), f32), jax.ShapeDtypeStruct((8, D_MODEL), f32)],
        compiler_params=_cparams("arbitrary"),
    )(dz, dxr, dq, dkv, ddt, dr1, x, mod, ln_g, ln_b, w_in)


def _adamw_math(w, g, m, v):
    m = ADAM_B1 * m + (1.0 - ADAM_B1) * g
    v = ADAM_B2 * v + (1.0 - ADAM_B2) * (g * g)
    m_hat = m / (1.0 - ADAM_B1 ** ADAM_STEP)
    v_hat = v / (1.0 - ADAM_B2 ** ADAM_STEP)
    delta = -ADAM_LR * (m_hat / (jnp.sqrt(v_hat) + ADAM_EPS) + ADAM_WD * w)
    return delta, m, v


def _adamw(w, g, m, v, name):
    R, C = w.shape

    def body(w_ref, g_ref, m_ref, v_ref, d_ref, m2_ref, v2_ref):
        d_ref[...], m2_ref[...], v2_ref[...] = _adamw_math(w_ref[...], g_ref[...], m_ref[...], v_ref[...])

    if R <= 256 or R % 256 == 0:
        tr = min(R, 256)
        spec, steps = pl.BlockSpec((tr, C), lambda i: (i, 0)), R // tr
    else:
        spec, steps = pl.BlockSpec((R, 256), lambda i: (0, i)), C // 256
    return pl.pallas_call(
        body, name=name, grid=(steps,), in_specs=[spec] * 4, out_specs=[spec] * 3,
        out_shape=[jax.ShapeDtypeStruct((R, C), f32)] * 3, compiler_params=_cparams("parallel"),
    )(w, g, m, v)


ADA_COLS = 6 * D_MODEL // N_CHIPS
ADA_TN = 512


def _ada_fwd(c_all, ada_w, ada_b):
    def body(c_ref, w_ref, b_ref, o_ref):
        cv = c_ref[...]
        o_ref[...] = _dot_exact(cv * _sigmoid(cv), w_ref[...]) + b_ref[...]

    return pl.pallas_call(
        body, name="ada_fwd", grid=(ADA_COLS // ADA_TN,),
        in_specs=[_full((N_DEV, D_MODEL)), pl.BlockSpec((D_MODEL, ADA_TN), lambda j: (0, j)),
                  pl.BlockSpec((1, ADA_TN), lambda j: (0, j))],
        out_specs=pl.BlockSpec((N_DEV, ADA_TN), lambda j: (0, j)),
        out_shape=jax.ShapeDtypeStruct((N_DEV, ADA_COLS), f32), compiler_params=_cparams("parallel"),
    )(c_all, ada_w, ada_b)


def _ada_bwd(c_all, dmod, w, m, v):
    def body(c_ref, d_ref, w_ref, m_ref, v_ref, g_ref, dl_ref, m2_ref, v2_ref):
        cv = c_ref[...]
        g = lax.dot_general(cv * _sigmoid(cv), d_ref[...], (((0,), (0,)), ((), ())), preferred_element_type=f32,
                            precision=lax.Precision.HIGHEST)
        g_ref[...] = g
        dl_ref[...], m2_ref[...], v2_ref[...] = _adamw_math(w_ref[...], g, m_ref[...], v_ref[...])

    wspec = pl.BlockSpec((D_MODEL, ADA_TN), lambda j: (0, j))
    return pl.pallas_call(
        body, name="ada_bwd", grid=(ADA_COLS // ADA_TN,),
        in_specs=[_full((N_DEV, D_MODEL)), pl.BlockSpec((N_DEV, ADA_TN), lambda j: (0, j)), wspec, wspec, wspec],
        out_specs=[wspec] * 4, out_shape=[jax.ShapeDtypeStruct((D_MODEL, ADA_COLS), f32)] * 4,
        compiler_params=_cparams("parallel"),
    )(c_all, dmod, w, m, v)


def _small_update(gathered, w, m, v):
    n = w.shape[1]

    def body(g_ref, w_ref, m_ref, v_ref, gs_ref, d_ref, m2_ref, v2_ref):
        g = g_ref[0:1, :]
        for i in range(1, N_DEV):
            g = g + g_ref[i:i + 1, :]
        gs_ref[...] = g
        d_ref[...], m2_ref[...], v2_ref[...] = _adamw_math(w_ref[...], g, m_ref[...], v_ref[...])

    return pl.pallas_call(body, name="small_update", out_shape=[jax.ShapeDtypeStruct((1, n), f32)] * 4,
                          compiler_params=_cparams())(gathered, w, m, v)


def _place():
    return lax.axis_index("x"), lax.axis_index("y"), lax.axis_index("c")


def _flip(x, y, c, m):
    return (1 - x if m & 4 else x, 1 - y if m & 2 else y, 1 - c if m & 1 else c)


_VMEM_SPEC = pl.BlockSpec(memory_space=pltpu.VMEM)
_ANY_SPEC = pl.BlockSpec(memory_space=pl.ANY)


def _allgather8(v, name):
    n = v.shape[1]

    def body(v_ref, out_ref, send_sems, recv_sems, local_sem):
        x, y, c = _place()

        def rows(px, py, pc):
            return out_ref.at[pl.ds(pl.multiple_of((4 * px + 2 * py + pc) * 8, 8), 8), :]

        def copy(m, src, dst, to):
            return pltpu.make_async_remote_copy(src_ref=src, dst_ref=dst, send_sem=send_sems.at[m - 1],
                                                recv_sem=recv_sems.at[m - 1], device_id=to, device_id_type=MESH)

        mine = pltpu.make_async_copy(v_ref, rows(x, y, c), local_sem)
        mine.start()
        sends = [copy(m, v_ref, rows(x, y, c), _flip(x, y, c, m)) for m in range(1, N_DEV)]
        for cp in sends:
            cp.start()
        for m in range(1, N_DEV):
            peer = _flip(x, y, c, m)
            copy(m, v_ref, rows(*peer), peer).wait_recv()
        for cp in sends:
            cp.wait_send()
        mine.wait()

    return pl.pallas_call(
        body, name=name, out_shape=jax.ShapeDtypeStruct((8 * N_DEV, n), f32), in_specs=[_VMEM_SPEC],
        out_specs=_VMEM_SPEC,
        scratch_shapes=[pltpu.SemaphoreType.DMA((N_DEV - 1,)), pltpu.SemaphoreType.DMA((N_DEV - 1,)),
                        pltpu.SemaphoreType.DMA],
    )(v)


def _mod_exchange(mod_all):
    def body(src_ref, out_ref, send_sems, recv_sems, local_sem):
        x, y, c = _place()

        def copy(m, src, dst, to):
            return pltpu.make_async_remote_copy(src_ref=src, dst_ref=dst, send_sem=send_sems.at[m - 1],
                                                recv_sem=recv_sems.at[m - 1], device_id=to, device_id_type=MESH)

        mine = pltpu.make_async_copy(src_ref.at[4 * x + 2 * y + c], out_ref.at[2 * x + y], local_sem)
        mine.start()
        sends = []
        for m in range(1, N_CHIPS):
            px, py, pc = _flip(x, y, c, 2 * m)
            sends.append(copy(m, src_ref.at[4 * px + 2 * py + pc], out_ref.at[2 * x + y], (px, py, pc)))
        for cp in sends:
            cp.start()
        for m in range(1, N_CHIPS):
            px, py, pc = _flip(x, y, c, 2 * m)
            copy(m, src_ref.at[0], out_ref.at[2 * px + py], (px, py, pc)).wait_recv()
        for cp in sends:
            cp.wait_send()
        mine.wait()

    return pl.pallas_call(
        body, name="mod_exchange", out_shape=jax.ShapeDtypeStruct((N_CHIPS, 12, 128), f32), in_specs=[_VMEM_SPEC],
        out_specs=_VMEM_SPEC,
        scratch_shapes=[pltpu.SemaphoreType.DMA((N_CHIPS - 1,)), pltpu.SemaphoreType.DMA((N_CHIPS - 1,)),
                        pltpu.SemaphoreType.DMA],
    )(mod_all)


def _remote(src, dst, send_sems, recv_sems, k, to):
    return pltpu.make_async_remote_copy(src_ref=src, dst_ref=dst, send_sem=send_sems.at[k], recv_sem=recv_sems.at[k],
                                        device_id=to, device_id_type=MESH)


def _gather_job(blob_ref, out_ref, send_sems, recv_sems, R):
    x, y, c = _place()
    sib = (x, y, 1 - c)
    hr = R // 2

    def half(px, py, pc):
        return out_ref.at[2 * px + py, pl.ds(pl.multiple_of(pc * hr, 16), hr), :]

    my_half = blob_ref.at[pl.ds(pl.multiple_of(c * hr, 16), hr), :]

    def first():
        return [_remote(my_half, half(x, y, c), send_sems, recv_sems, m - 1, _flip(x, y, c, 2 * m))
                for m in range(1, N_CHIPS)]

    def start():
        for cp in first():
            cp.start()

    def finish():
        passed = []
        for m in range(1, N_CHIPS):
            px, py, pc = _flip(x, y, c, 2 * m)
            _remote(my_half, half(px, py, pc), send_sems, recv_sems, m - 1, (px, py, pc)).wait_recv()
            fwd = _remote(half(px, py, pc), half(px, py, pc), send_sems, recv_sems, 2 + m, sib)
            fwd.start()
            passed.append(fwd)
        for m in range(1, N_CHIPS):
            px, py, pc = _flip(x, y, c, 2 * m)
            _remote(my_half, half(px, py, 1 - pc), send_sems, recv_sems, 2 + m, sib).wait_recv()
        for cp in first() + passed:
            cp.wait_send()

    return start, finish


def _gather_weights(blob):
    R = blob.shape[0]

    def body(blob_ref, out_ref, send_sems, recv_sems, local_sem, stage):
        x, y, c = _place()
        start, finish = _gather_job(blob_ref, out_ref, send_sems, recv_sems, R)
        start()
        for k in range(R // STAGE_ROWS):
            rows = pl.ds(STAGE_ROWS * k, STAGE_ROWS)
            cin = pltpu.make_async_copy(blob_ref.at[rows, :], stage, local_sem)
            cin.start()
            cin.wait()
            cout = pltpu.make_async_copy(stage, out_ref.at[2 * x + y, rows, :], local_sem)
            cout.start()
            cout.wait()
        finish()

    return pl.pallas_call(
        body, name="gather_weights", out_shape=jax.ShapeDtypeStruct((N_CHIPS, R, D_MODEL), bf16),
        in_specs=[_ANY_SPEC], out_specs=_ANY_SPEC,
        scratch_shapes=[pltpu.SemaphoreType.DMA((6,)), pltpu.SemaphoreType.DMA((6,)), pltpu.SemaphoreType.DMA,
                        pltpu.VMEM((STAGE_ROWS, D_MODEL), bf16)],
    )(blob)


def _to_sibling_job(g_ref, out_ref, send_sems, recv_sems):
    x, y, c = _place()

    def cps():
        return [_remote(g_ref.at[j, 1 - c], out_ref.at[j], send_sems, recv_sems, j, (x, y, 1 - c)) for j in range(N_CHIPS)]

    def start():
        for cp in cps():
            cp.start()

    def wait():
        for cp in cps():
            cp.wait()

    return start, wait


def _to_chips_job(p_ref, out_ref, send_sems, recv_sems):
    x, y, c = _place()

    def cps():
        out = []
        for m in range(1, N_CHIPS):
            px, py, pc = _flip(x, y, c, 2 * m)
            out.append(_remote(p_ref.at[2 * px + py], out_ref.at[m - 1], send_sems, recv_sems, m - 1, (px, py, pc)))
        return out

    def start():
        for cp in cps():
            cp.start()

    def wait():
        for cp in cps():
            cp.wait()

    return start, wait


def _share_job(g_ref, send_sems, recv_sems, R):
    x, y, c = _place()

    def rows(pc):
        return g_ref.at[pl.ds(pl.multiple_of(pc * (R // 2), 8), R // 2), :]

    def start():
        _remote(rows(c), rows(c), send_sems, recv_sems, 0, (x, y, 1 - c)).start()

    def wait():
        _remote(rows(c), rows(1 - c), send_sems, recv_sems, 0, (x, y, 1 - c)).wait_recv()
        _remote(rows(c), rows(c), send_sems, recv_sems, 0, (x, y, 1 - c)).wait_send()

    return start, wait


def _sems(n):
    return [pltpu.SemaphoreType.DMA((n,)), pltpu.SemaphoreType.DMA((n,))]


def _rs_to_sibling(gb):
    def body(g_ref, out_ref, send_sems, recv_sems):
        start, wait = _to_sibling_job(g_ref, out_ref, send_sems, recv_sems)
        start()
        wait()

    return pl.pallas_call(
        body, name="rs_to_sibling", out_shape=jax.ShapeDtypeStruct((N_CHIPS,) + gb.shape[2:], bf16),
        in_specs=[_ANY_SPEC], out_specs=_ANY_SPEC, scratch_shapes=_sems(N_CHIPS),
    )(gb)


def _rs_to_chips(pb):
    def body(p_ref, out_ref, send_sems, recv_sems):
        start, wait = _to_chips_job(p_ref, out_ref, send_sems, recv_sems)
        start()
        wait()

    return pl.pallas_call(
        body, name="rs_to_chips", out_shape=jax.ShapeDtypeStruct((N_CHIPS - 1,) + pb.shape[1:], bf16),
        in_specs=[_ANY_SPEC], out_specs=_ANY_SPEC, scratch_shapes=_sems(N_CHIPS - 1),
    )(pb)


def _rs_share(g):
    R = g.shape[0]

    def body(g_ref, out_ref, send_sems, recv_sems):
        start, wait = _share_job(out_ref, send_sems, recv_sems, R)
        start()
        wait()

    return pl.pallas_call(
        body, name="rs_share", out_shape=jax.ShapeDtypeStruct(g.shape, f32), in_specs=[_ANY_SPEC],
        out_specs=_ANY_SPEC, input_output_aliases={0: 0}, scratch_shapes=_sems(1),
    )(g)


RS_TR = 256


def _rs_sum_pair(place, gb, recv, name):
    hr = gb.shape[2]

    def body(pl_ref, g_ref, r_ref, o_ref):
        o_ref[0] = (g_ref[0, 0].astype(f32) + r_ref[0].astype(f32)).astype(bf16)

    return pl.pallas_call(
        body, name=name,
        grid_spec=pltpu.PrefetchScalarGridSpec(
            num_scalar_prefetch=1, grid=(N_CHIPS, hr // RS_TR),
            in_specs=[pl.BlockSpec((1, 1, RS_TR, D_MODEL), lambda j, i, p: (j, p[0], i, 0)),
                      pl.BlockSpec((1, RS_TR, D_MODEL), lambda j, i, p: (j, i, 0))],
            out_specs=pl.BlockSpec((1, RS_TR, D_MODEL), lambda j, i, p: (j, i, 0))),
        out_shape=jax.ShapeDtypeStruct((N_CHIPS, hr, D_MODEL), bf16),
        compiler_params=_cparams("parallel", "parallel"),
    )(place, gb, recv)


def _rs_sum_chips(place, gb, recv_sib, recv_chips, name):
    hr = gb.shape[2]
    nt = hr // RS_TR

    def body(pl_ref, g_ref, r1_ref, r2_ref, o_ref):
        acc = g_ref[0, 0].astype(f32) + r1_ref[0].astype(f32)
        for k in range(N_CHIPS - 1):
            acc = acc + r2_ref[k].astype(f32)
        o_ref[...] = acc

    return pl.pallas_call(
        body, name=name,
        grid_spec=pltpu.PrefetchScalarGridSpec(
            num_scalar_prefetch=1, grid=(nt,),
            in_specs=[pl.BlockSpec((1, 1, RS_TR, D_MODEL), lambda i, p: (p[1], p[0], i, 0)),
                      pl.BlockSpec((1, RS_TR, D_MODEL), lambda i, p: (p[1], i, 0)),
                      pl.BlockSpec((N_CHIPS - 1, RS_TR, D_MODEL), lambda i, p: (0, i, 0))],
            out_specs=pl.BlockSpec((RS_TR, D_MODEL), lambda i, p: (p[0] * nt + i, 0))),
        out_shape=jax.ShapeDtypeStruct((2 * hr, D_MODEL), f32),
        compiler_params=_cparams("parallel"),
    )(place, gb, recv_sib, recv_chips)


SMALL_SLOTS = (("ada_b", 6144), ("ln_in_g", 1024), ("ln_in_b", 1024), ("conv_b", 1536), ("dt_bias", 128), ("a_log", 128),
               ("d_skip", 128), ("ssd_norm_w", 1024), ("attn_sinks", 128), ("ln1_g", 1024), ("ln1_b", 1024),
               ("b_ff1", 4096), ("b_ff2", 1024), ("ln2_g", 1024), ("ln2_b", 1024), ("conv_w", 6144), ("loss", 1024))
SMALL_N = sum(n for _, n in SMALL_SLOTS)
assert SMALL_N % 1024 == 0


def _pack_small(vals):
    parts = []
    for name, n in SMALL_SLOTS:
        v = vals.get(name)
        v = jnp.zeros((n,), f32) if v is None else v.reshape(-1).astype(f32)
        parts.append(jnp.pad(v, (0, n - v.shape[0])))
    return jnp.concatenate(parts)


def _unpack_small(vec):
    out, off = {}, 0
    for name, n in SMALL_SLOTS:
        out[name] = vec[off:off + n]
        off += n
    return out


def _pad128(v):
    v = v.reshape(1, -1)
    return jnp.pad(v, ((0, 0), (0, 128 - v.shape[1])))


def _row(v):
    return v.reshape(1, -1)


def _w_in_layout(w):
    return jnp.concatenate([w[:2560], w[2576:3856], w[2560:2576], jnp.zeros((C_END - PROJ_WIDTH, D_MODEL), w.dtype)], axis=0)


def kernel(x, c, ln_in_g, ln_in_b, ada_w, ada_b, w_in, conv_w, conv_b, dt_bias, a_log, d_skip, ssd_norm_w, attn_sinks, w_out, ln1_g, ln1_b, w_ff1, b_ff1, w_ff2, b_ff2, ln2_g, ln2_b, loss_target, m_ln_in_g, m_ln_in_b, m_ada_w, m_ada_b, m_w_in, m_conv_w, m_conv_b, m_dt_bias, m_a_log, m_d_skip, m_ssd_norm_w, m_attn_sinks, m_w_out, m_ln1_g, m_ln1_b, m_w_ff1, m_b_ff1, m_w_ff2, m_b_ff2, m_ln2_g, m_ln2_b, v_ln_in_g, v_ln_in_b, v_ada_w, v_ada_b, v_w_in, v_conv_w, v_conv_b, v_dt_bias, v_a_log, v_d_skip, v_ssd_norm_w, v_attn_sinks, v_w_out, v_ln1_g, v_ln1_b, v_w_ff1, v_b_ff1, v_w_ff2, v_b_ff2, v_ln2_g, v_ln2_b):
    xi, yi, ci = _place()
    chip = 2 * xi + yi
    place = jnp.stack([ci, chip]).astype(jnp.int32)
    x2, tgt = x[0], loss_target[0]
    w_shard_cols = PROJ_WIDTH // N_CHIPS

    cond = jnp.concatenate([c.reshape(-1), conv_w.reshape(-1), jnp.zeros((512,), f32)]).reshape(8, 384)
    cond_all = _allgather8(cond, "gather_cond").reshape(N_DEV, 3072)
    c_all = cond_all[:, :D_MODEL]
    conv_w_full = jnp.concatenate([cond_all[2 * j, D_MODEL:D_MODEL + 1536].reshape(CONV_K, 384) for j in range(N_CHIPS)], axis=1)

    ada_b_mine = lax.dynamic_slice(ada_b, (0, chip * ADA_COLS), (1, ADA_COLS))
    mod_all = _ada_fwd(c_all, ada_w[0], ada_b_mine)
    mod_mine = _mod_exchange(mod_all.reshape(N_DEV, 12, 128)).reshape(6, D_MODEL)
    mod = jnp.concatenate([mod_mine, jnp.zeros((2, D_MODEL), f32)], axis=0)

    w_in_t, m_w_in_t, v_w_in_t = w_in[0].T, m_w_in[0].T, v_w_in[0].T

    wall_in = _gather_weights(jnp.pad(w_in_t, ((0, D_MODEL - w_shard_cols), (0, 0))).astype(bf16))
    w_in_f = _w_in_layout(wall_in[:, :w_shard_cols].reshape(PROJ_WIDTH, D_MODEL))
    b_ff1w, b_ff2w, b_outw = w_ff1[0].astype(bf16), w_ff2[0].astype(bf16), w_out[0].astype(bf16)

    def with_mine(wall, mine):
        return lax.dynamic_update_slice(wall, mine[None], (chip, 0, 0))

    e_mat = _head_expand()
    dsk_x = jnp.repeat(d_skip[0], HEAD_DIM).reshape(1, SSD_WIDTH)
    dtb, alog = _pad128(dt_bias), _pad128(a_log)
    sinks = attn_sinks[0]
    lng, lnb = _row(ln_in_g), _row(ln_in_b)
    u1, z, xr, xc, q, kv, dtr, wall_ff1 = _inproj_fwd(x2, mod, lng, lnb, w_in_f, conv_w_full, conv_b, b_ff1w)
    y, prev_all, wall_ff2 = _ssd_fwd(xc, dtr, dtb, alog, dsk_x, e_mat, b_ff2w)
    o, lse, wall_out = _attn_fwd(q, kv, sinks, b_outw)
    wall_ff1, wall_ff2, wall_out = with_mine(wall_ff1, b_ff1w), with_mine(wall_ff2, b_ff2w), with_mine(wall_out, b_outw)
    yn, mix, r1 = _outproj_fwd(y, z, o, x2, mod, lng, lnb, ssd_norm_w, wall_out)

    dr1, u2, s_act, da, df, acc_mlp, db1 = _mlp_fwd_bwd(r1, tgt, mod, ln1_g, ln1_b, ln2_g, ln2_b, wall_ff1, b_ff1, wall_ff2,
                                                        b_ff2)
    ga = jnp.zeros((N_CHIPS, GA_ROWS, D_MODEL), bf16)
    ga = _wgrad_blob(ga, u2, da, "wgrad_ff1", lambda t, n: (n, t))
    ga = _wgrad_blob(ga, s_act, df, "wgrad_ff2", lambda t, n: (t // 2, 2 + t % 2))
    ga = ga.reshape(N_CHIPS, 2, GA_ROWS // 2, D_MODEL)
    dy, dz, do, dmix, acc_out, a_sib = _outproj_bwd(dr1, mix, y, z, mod, ssd_norm_w, wall_out, ga)
    gb = jnp.zeros((N_CHIPS, GB_ROWS, D_MODEL), bf16)
    gb = _wgrad_blob(gb, yn, dmix, "wgrad_out_y", lambda t, n: (t, 2))
    gb = _wgrad_blob(gb, o, dmix, "wgrad_out_o", lambda t, n: (2 + t, 2))
    a_pair = _rs_sum_pair(place, ga, a_sib, "rs_sum_pair_a")
    dq, dkv, dsink, a_chips = _attn_bwd(q, kv, do, lse, sinks, a_pair)
    g_a = _rs_sum_chips(place, ga, a_sib, a_chips, "rs_sum_chips_a")
    dxc, ddt, acc_ssd, dd_x, g_a = _ssd_bwd(xc, dtr, dy, prev_all, dtb, alog, dsk_x, e_mat, g_a)
    dxr, acc_conv = _conv_bwd(dxc, xr, conv_w_full, conv_b)
    grad_x, acc_in = _inproj_bwd(dz, dxr, dq, dkv, ddt, dr1, x2, mod, lng, lnb, w_in_f)
    g_in_t = jnp.concatenate([_wgrad(dz, u1, "wgrad_in_z"), _wgrad(dxr, u1, "wgrad_in_xbc"),
                              _wgrad(ddt.astype(bf16), u1, "wgrad_in_dt")[:16], _wgrad(dq, u1, "wgrad_in_q"),
                              _wgrad(dkv, u1, "wgrad_in_kv")], axis=0)

    dmod = jnp.concatenate([acc_in[1], acc_in[0], acc_out[0], acc_mlp[A_SH2], acc_mlp[A_SC2], acc_mlp[A_G2]])
    small = {
        "ada_b": dmod, "ln_in_g": acc_in[2], "ln_in_b": acc_in[3], "conv_b": acc_conv[4], "dt_bias": acc_ssd[1, :16],
        "a_log": acc_ssd[0, :16] * (-jnp.exp(a_log[0])), "d_skip": jnp.sum(dd_x.reshape(SSD_HEADS, HEAD_DIM), axis=1),
        "ssd_norm_w": acc_out[1], "attn_sinks": dsink[0, :16], "ln1_g": acc_mlp[A_LN1G], "ln1_b": acc_mlp[A_LN1B],
        "b_ff1": db1[0], "b_ff2": acc_mlp[A_B2], "ln2_g": acc_mlp[A_LN2G], "ln2_b": acc_mlp[A_LN2B],
        "conv_w": acc_conv[0:CONV_K], "loss": acc_mlp[A_LOSS],
    }
    small_all = _allgather8(_pack_small(small).reshape(8, SMALL_N // 8), "gather_small").reshape(N_DEV, SMALL_N)
    params = dict(ada_b=ada_b, ln_in_g=ln_in_g, ln_in_b=ln_in_b, conv_b=conv_b, dt_bias=dt_bias, a_log=a_log, d_skip=d_skip,
                  ssd_norm_w=ssd_norm_w, attn_sinks=attn_sinks, ln1_g=ln1_g, ln1_b=ln1_b, b_ff1=b_ff1, b_ff2=b_ff2,
                  ln2_g=ln2_g, ln2_b=ln2_b)
    moms = dict(ada_b=m_ada_b, ln_in_g=m_ln_in_g, ln_in_b=m_ln_in_b, conv_b=m_conv_b, dt_bias=m_dt_bias, a_log=m_a_log,
                d_skip=m_d_skip, ssd_norm_w=m_ssd_norm_w, attn_sinks=m_attn_sinks, ln1_g=m_ln1_g, ln1_b=m_ln1_b,
                b_ff1=m_b_ff1, b_ff2=m_b_ff2, ln2_g=m_ln2_g, ln2_b=m_ln2_b)
    vels = dict(ada_b=v_ada_b, ln_in_g=v_ln_in_g, ln_in_b=v_ln_in_b, conv_b=v_conv_b, dt_bias=v_dt_bias, a_log=v_a_log,
                d_skip=v_d_skip, ssd_norm_w=v_ssd_norm_w, attn_sinks=v_attn_sinks, ln1_g=v_ln1_g, ln1_b=v_ln1_b,
                b_ff1=v_b_ff1, b_ff2=v_b_ff2, ln2_g=v_ln2_g, ln2_b=v_ln2_b)
    gs, dl, m2, v2 = _small_update(small_all, _pack_small(params).reshape(1, -1), _pack_small(moms).reshape(1, -1),
                                   _pack_small(vels).reshape(1, -1))
    gs, dl, m2, v2 = (_unpack_small(t[0]) for t in (gs, dl, m2, v2))
    loss = jnp.sum(gs["loss"])

    dmod_mine = lax.dynamic_slice(small_all[:, :6 * D_MODEL], (0, chip * ADA_COLS), (N_DEV, ADA_COLS))
    g_ada, d_ada, m_ada, v_ada = _ada_bwd(c_all, dmod_mine, ada_w[0], m_ada_w[0], v_ada_w[0])

    g_conv = lax.dynamic_slice(gs["conv_w"].reshape(CONV_K, CONV_DIM), (0, chip * 384), (CONV_K, 384))
    d_conv, m_conv, v_conv = _adamw(conv_w[0], g_conv, m_conv_w[0], v_conv_w[0], "adamw_conv_w")

    g_in_c = jnp.pad(g_in_t.reshape(N_CHIPS, w_shard_cols, D_MODEL), ((0, 0), (0, D_MODEL - w_shard_cols), (0, 0)))
    gb = lax.dynamic_update_slice(gb, g_in_c, (0, 0, 0)).reshape(N_CHIPS, 2, GB_ROWS // 2, D_MODEL)
    b_sib = _rs_to_sibling(gb)
    b_chips = _rs_to_chips(_rs_sum_pair(place, gb, b_sib, "rs_sum_pair_b"))
    g_b = _rs_share(_rs_sum_chips(place, gb, b_sib, b_chips, "rs_sum_chips_b"))
    g_w_in_t, g_w_out = g_b[:w_shard_cols], g_b[D_MODEL:GB_ROWS]
    g_w_ff1, g_w_ff2 = g_a[:D_MODEL], g_a[D_MODEL:GA_ROWS]
    d_w_in_t, m_w_in2_t, v_w_in2_t = _adamw(w_in_t, g_w_in_t, m_w_in_t, v_w_in_t, "adamw_w_in")
    g_w_in, d_w_in, m_w_in2, v_w_in2 = g_w_in_t.T, d_w_in_t.T, m_w_in2_t.T, v_w_in2_t.T
    d_w_out, m_w_out2, v_w_out2 = _adamw(w_out[0], g_w_out, m_w_out[0], v_w_out[0], "adamw_w_out")
    d_w_ff1, m_w_ff12, v_w_ff12 = _adamw(w_ff1[0], g_w_ff1, m_w_ff1[0], v_w_ff1[0], "adamw_w_ff1")
    d_w_ff2, m_w_ff22, v_w_ff22 = _adamw(w_ff2[0], g_w_ff2, m_w_ff2[0], v_w_ff2[0], "adamw_w_ff2")

    def small_out(t, name, like):
        return t[name][:like.size].reshape(like.shape)

    def outputs(t, big):
        return [small_out(t, "ln_in_g", ln_in_g), small_out(t, "ln_in_b", ln_in_b), big["ada_w"][None],
                small_out(t, "ada_b", ada_b), big["w_in"][None], big["conv_w"][None], small_out(t, "conv_b", conv_b),
                small_out(t, "dt_bias", dt_bias), small_out(t, "a_log", a_log), small_out(t, "d_skip", d_skip),
                small_out(t, "ssd_norm_w", ssd_norm_w), small_out(t, "attn_sinks", attn_sinks), big["w_out"][None],
                small_out(t, "ln1_g", ln1_g), small_out(t, "ln1_b", ln1_b), big["w_ff1"][None], small_out(t, "b_ff1", b_ff1),
                big["w_ff2"][None], small_out(t, "b_ff2", b_ff2), small_out(t, "ln2_g", ln2_g), small_out(t, "ln2_b", ln2_b)]

    grads = outputs(gs, dict(ada_w=g_ada, w_in=g_w_in, conv_w=g_conv, w_out=g_w_out, w_ff1=g_w_ff1, w_ff2=g_w_ff2))
    deltas = outputs(dl, dict(ada_w=d_ada, w_in=d_w_in, conv_w=d_conv, w_out=d_w_out, w_ff1=d_w_ff1, w_ff2=d_w_ff2))
    new_m = outputs(m2, dict(ada_w=m_ada, w_in=m_w_in2, conv_w=m_conv, w_out=m_w_out2, w_ff1=m_w_ff12, w_ff2=m_w_ff22))
    new_v = outputs(v2, dict(ada_w=v_ada, w_in=v_w_in2, conv_w=v_conv, w_out=v_w_out2, w_ff1=v_w_ff12, w_ff2=v_w_ff22))
    return (loss, grad_x[None], *grads, *deltas, *new_m, *new_v)
```

```python
import functools
import math

import numpy as np
import jax
import jax.numpy as jnp
from jax import lax
from jax.experimental import pallas as pl
from jax.experimental.pallas import tpu as pltpu

f32 = jnp.float32
bf16 = jnp.bfloat16

D_MODEL = 1024
SSD_WIDTH = 1024
SSD_HEADS = 16
HEAD_DIM = 64
SSD_STATE = 128
SSD_GROUPS = 2
CHUNK = 128
CONV_K = 4
CONV_DIM = 1536
ATTN_HEADS = 16
D_FF = 4096
PROJ_WIDTH = 3856
ALPHA = 2.0 ** 0.25
LN_EPS = 1e-5
RMS_EPS = 1e-5
ATTN_SCALE = HEAD_DIM ** -0.5
NEG = -1e30

ADAM_LR = 0.001
ADAM_B1 = 0.9
ADAM_B2 = 0.999
ADAM_EPS = 1e-08
ADAM_WD = 0.01
ADAM_STEP = 10

C_Z, C_XBC, C_Q, C_KV, C_DT, C_END = 0, 1024, 2560, 3584, 3840, 3968
GA_ROWS = 2048
GB_ROWS = 1536
WG_TM = 512
STAGE_ROWS = 512
DENSE_TM = 512
N_CHIPS = 4
N_DEV = 8
VMEM_LIMIT = 56 * 1024 * 1024
MESH = pl.DeviceIdType.MESH

ALIBI_SLOPES = tuple(2.0 ** (-8.0 / ATTN_HEADS * (i + 1)) for i in range(ATTN_HEADS))


def _cparams(*sem):
    return pltpu.CompilerParams(dimension_semantics=sem, vmem_limit_bytes=VMEM_LIMIT)


def _sigmoid(x):
    return 1.0 / (1.0 + jnp.exp(-x))


def _softplus(x):
    return jnp.maximum(x, 0.0) + jnp.log1p(jnp.exp(-jnp.abs(x)))


def _ln_stats(x):
    mu = jnp.mean(x, axis=-1, keepdims=True)
    xc = x - mu
    var = jnp.mean(xc * xc, axis=-1, keepdims=True)
    rstd = lax.rsqrt(var + LN_EPS)
    return xc * rstd, rstd


def _ln_bwd(dy, xhat, rstd, g):
    dxh = dy * g
    m1 = jnp.mean(dxh, axis=-1, keepdims=True)
    m2 = jnp.mean(dxh * xhat, axis=-1, keepdims=True)
    return rstd * (dxh - m1 - xhat * m2)


def _dot(a, b):
    return jnp.dot(a, b, preferred_element_type=f32)


def _dot_nt(a, b):
    return lax.dot_general(a, b, (((1,), (1,)), ((), ())), preferred_element_type=f32)


def _dot_tn(a, b):
    return lax.dot_general(a, b, (((0,), (0,)), ((), ())), preferred_element_type=f32)


def _dot_exact(a, b):
    return jnp.dot(a, b, preferred_element_type=f32, precision=lax.Precision.HIGHEST)


def _split3(v):
    hi = v.astype(bf16)
    r1 = v - hi.astype(f32)
    mid = r1.astype(bf16)
    lo = (r1 - mid.astype(f32)).astype(bf16)
    return hi, mid, lo


def _sel_dot(sel, v):
    hi, mid, lo = _split3(v)
    return _dot(sel, hi) + _dot(sel, mid) + _dot(sel, lo)


def _dot_sel(v, sel):
    hi, mid, lo = _split3(v)
    return _dot(hi, sel) + _dot(mid, sel) + _dot(lo, sel)


def _dot_sel_nt(v, sel):
    hi, mid, lo = _split3(v)
    return _dot_nt(hi, sel) + _dot_nt(mid, sel) + _dot_nt(lo, sel)


def _full(shape):
    nd = len(shape)
    return pl.BlockSpec(shape, lambda *_: (0,) * nd)


def _resident(shape):
    nd = len(shape)
    return pl.BlockSpec(shape, lambda *_: (0,) * nd, pipeline_mode=pl.Buffered(1))


def _rows(tm, n):
    return pl.BlockSpec((tm, n), lambda i: (i, 0))


def _inproj_fwd(x, mod, ln_g, ln_b, w_in_t, conv_w, conv_b, blob):
    L = x.shape[0]
    tm = DENSE_TM
    nt = L // tm
    R = blob.shape[0]

    def body(x_ref, mod_ref, g_ref, b_ref, w_ref, cw_ref, cb_ref, blob_ref,
             u1_ref, z_ref, xr_ref, xc_ref, q_ref, kv_ref, dt_ref, wall_ref, halo, buf, send_sems, recv_sems):
        start, finish = _gather_job(blob_ref, wall_ref, send_sems, recv_sems, R)

        @pl.when(pl.program_id(0) == 0)
        def _():
            halo[...] = jnp.zeros_like(halo)
            start()

        xhat, _ = _ln_stats(x_ref[...])
        h0 = xhat * g_ref[...] + b_ref[...]
        u1 = (h0 * (1.0 + mod_ref[1:2, :]) + mod_ref[0:1, :]).astype(bf16)
        u1_ref[...] = u1
        z_ref[...] = _dot_nt(u1, w_ref[C_Z:C_XBC, :])
        xr = _dot_nt(u1, w_ref[C_XBC:C_Q, :])
        xr_ref[...] = xr
        q_ref[...] = _dot_nt(u1, w_ref[C_Q:C_KV, :]).astype(bf16)
        kv_ref[...] = _dot_nt(u1, w_ref[C_KV:C_DT, :]).astype(bf16)
        dt_ref[...] = _dot_nt(u1, w_ref[C_DT:C_END, :])
        buf[0:8, :] = halo[...]
        buf[8:8 + tm, :] = xr
        pre = cb_ref[...] + cw_ref[0:1, :] * buf[5:5 + tm, :]
        for k in range(1, CONV_K):
            pre = pre + cw_ref[k:k + 1, :] * buf[5 + k:5 + k + tm, :]
        xc_ref[...] = pre * _sigmoid(pre)
        halo[...] = xr[tm - 8:tm, :]

        @pl.when(pl.program_id(0) == nt - 1)
        def _():
            finish()

    return pl.pallas_call(
        body, name="inproj_fwd", grid=(nt,),
        in_specs=[_rows(tm, D_MODEL), _full((8, D_MODEL)), _full((1, D_MODEL)), _full((1, D_MODEL)),
                  _resident((C_END, D_MODEL)), _full((CONV_K, CONV_DIM)), _full((1, CONV_DIM)), _ANY_SPEC],
        out_specs=[_rows(tm, D_MODEL), _rows(tm, D_MODEL), _rows(tm, CONV_DIM), _rows(tm, CONV_DIM),
                   _rows(tm, D_MODEL), _rows(tm, 256), _rows(tm, 128), _ANY_SPEC],
        out_shape=[jax.ShapeDtypeStruct((L, D_MODEL), bf16), jax.ShapeDtypeStruct((L, D_MODEL), f32),
                   jax.ShapeDtypeStruct((L, CONV_DIM), f32), jax.ShapeDtypeStruct((L, CONV_DIM), f32),
                   jax.ShapeDtypeStruct((L, D_MODEL), bf16), jax.ShapeDtypeStruct((L, 256), bf16),
                   jax.ShapeDtypeStruct((L, 128), f32), jax.ShapeDtypeStruct((N_CHIPS, R, D_MODEL), bf16)],
        scratch_shapes=[pltpu.VMEM((8, CONV_DIM), f32), pltpu.VMEM((tm + 8, CONV_DIM), f32),
                        pltpu.SemaphoreType.DMA((6,)), pltpu.SemaphoreType.DMA((6,))],
        compiler_params=_cparams("arbitrary"),
    )(x, mod, ln_g, ln_b, w_in_t, conv_w, conv_b, blob)


def _head_expand():
    e = np.zeros((128, SSD_WIDTH), np.float32)
    for h in range(SSD_HEADS):
        e[h, h * HEAD_DIM:(h + 1) * HEAD_DIM] = 1.0
    return jnp.asarray(e, dtype=bf16)


def _ssd_chunk_common(dt_raw, dtb, a_row, e_mat):
    T = CHUNK
    lane = lax.broadcasted_iota(jnp.int32, (T, 128), 1)
    dt = jnp.where(lane < SSD_HEADS, _softplus(dt_raw + dtb), 0.0)
    a = dt * a_row
    r = lax.broadcasted_iota(jnp.int32, (T, T), 0)
    c = lax.broadcasted_iota(jnp.int32, (T, T), 1)
    tril = (c <= r).astype(bf16)
    cum = _sel_dot(tril, a)
    dtx = _dot_sel(dt, e_mat)
    cumx = _dot_sel(cum, e_mat)
    return dt, a, cum, dtx, cumx, r, c


def _ssd_fwd(xc, dt_raw, dt_bias, a_log, d_skip_x, e_mat, blob):
    L = xc.shape[0]
    nc = L // CHUNK
    T = CHUNK
    R = blob.shape[0]

    def body(xc_ref, dt_ref, dtb_ref, al_ref, dsk_ref, e_ref, blob_ref, y_ref, prev_ref, wall_ref, st, send_sems, recv_sems):
        start, finish = _gather_job(blob_ref, wall_ref, send_sems, recv_sems, R)

        @pl.when(pl.program_id(0) == 0)
        def _():
            st[...] = jnp.zeros_like(st)
            start()

        @pl.when(pl.program_id(0) == nc - 1)
        def _():
            finish()

        a_row = -jnp.exp(al_ref[...])
        lane1 = lax.broadcasted_iota(jnp.int32, (1, 128), 1)
        a_row = jnp.where(lane1 < SSD_HEADS, a_row, 0.0)
        dt, a, cum, dtx, cumx, r, c = _ssd_chunk_common(dt_ref[...], dtb_ref[...], a_row, e_ref[...])
        cum_t = cum.T
        ex = jnp.exp(cumx)
        last = cumx[T - 1:T, :]
        wx = jnp.exp(last - cumx)
        cdx = jnp.exp(last)
        xs = xc_ref[:, 0:SSD_WIDTH]
        X = xs * dtx
        Xb = X.astype(bf16)
        Xd = (X * wx).astype(bf16)
        prev = st[...]
        prev_ref[0] = prev
        prevb = prev.astype(bf16)
        tri = c <= r
        lane = lax.broadcasted_iota(jnp.int32, (T, 128), 1)
        y_blocks = []
        new_states = []
        for g in range(SSD_GROUPS):
            Bg = xc_ref[:, 1024 + 128 * g:1152 + 128 * g].astype(bf16)
            Cg = xc_ref[:, 1280 + 128 * g:1408 + 128 * g].astype(bf16)
            G = _dot_nt(Cg, Bg)
            yoff = _dot(Cg, prevb[:, 512 * g:512 * (g + 1)])
            new_states.append(_dot_tn(Bg, Xd[:, 512 * g:512 * (g + 1)]))
            for j in range(4):
                blk = 4 * g + j
                Xblk = Xb[:, 128 * blk:128 * (blk + 1)]
                ys = []
                for half in range(2):
                    h = 2 * blk + half
                    seg = jnp.minimum(cum[:, h:h + 1] - cum_t[h:h + 1, :], 0.0)
                    M = jnp.where(tri, G * jnp.exp(seg), 0.0).astype(bf16)
                    ys.append(_dot(M, Xblk))
                yd = jnp.where(lane < HEAD_DIM, ys[0], ys[1])
                sl = slice(128 * blk, 128 * (blk + 1))
                y_blocks.append(yd + ex[:, sl] * yoff[:, 128 * j:128 * (j + 1)] + dsk_ref[:, sl] * xs[:, sl])
        y_ref[...] = jnp.concatenate(y_blocks, axis=1)
        st[...] = prev * cdx + jnp.concatenate(new_states, axis=1)

    return pl.pallas_call(
        body, name="ssd_fwd", grid=(nc,),
        in_specs=[_rows(T, CONV_DIM), _rows(T, 128), _full((1, 128)), _full((1, 128)), _full((1, SSD_WIDTH)),
                  _full((128, SSD_WIDTH)), _ANY_SPEC],
        out_specs=[_rows(T, SSD_WIDTH), pl.BlockSpec((1, SSD_STATE, SSD_WIDTH), lambda i: (i, 0, 0)), _ANY_SPEC],
        out_shape=[jax.ShapeDtypeStruct((L, SSD_WIDTH), f32), jax.ShapeDtypeStruct((nc, SSD_STATE, SSD_WIDTH), f32),
                   jax.ShapeDtypeStruct((N_CHIPS, R, D_MODEL), bf16)],
        scratch_shapes=[pltpu.VMEM((SSD_STATE, SSD_WIDTH), f32)] + _sems(6),
        compiler_params=_cparams("arbitrary"),
    )(xc, dt_raw, dt_bias, a_log, d_skip_x, e_mat, blob)


def _kv_halves(kv_prev, kv_cur, first):
    kv = jnp.concatenate([jnp.where(first, 0.0, kv_prev.astype(f32)), kv_cur.astype(f32)], axis=0)
    lane = lax.broadcasted_iota(jnp.int32, (2 * CHUNK, 128), 1)
    lo = lane < HEAD_DIM
    out = []
    for g in range(2):
        per_half = []
        for half in range(2):
            both = []
            for t in (kv[:, 0:128], kv[:, 128:256]):
                src = t if g == half else pltpu.roll(t, HEAD_DIM, 1)
                both.append(jnp.where(lo if half == 0 else ~lo, src, 0.0).astype(bf16))
            per_half.append(tuple(both))
        out.append(per_half)
    return out


def _attn_masks(first):
    r = lax.broadcasted_iota(jnp.int32, (CHUNK, 2 * CHUNK), 0)
    c = lax.broadcasted_iota(jnp.int32, (CHUNK, 2 * CHUNK), 1)
    dist = r + CHUNK - c
    valid = (dist >= 0) & (dist < CHUNK) & ((c >= CHUNK) | jnp.logical_not(first))
    return dist.astype(f32), valid


def _attn_fwd(q, kv, sinks, blob):
    L = q.shape[0]
    nb = L // CHUNK
    T = CHUNK
    R = blob.shape[0]

    def body(sink_ref, q_ref, kvp_ref, kvc_ref, blob_ref, o_ref, lse_ref, wall_ref, send_sems, recv_sems):
        first = pl.program_id(0) == 0
        start, finish = _gather_job(blob_ref, wall_ref, send_sems, recv_sems, R)

        @pl.when(first)
        def _():
            start()

        @pl.when(pl.program_id(0) == nb - 1)
        def _():
            finish()

        ext = _kv_halves(kvp_ref[...], kvc_ref[...], first)
        dist, valid = _attn_masks(first)
        lane = lax.broadcasted_iota(jnp.int32, (T, 128), 1)
        lse = jnp.zeros((T, 128), f32)
        o_blocks = []
        for blk in range(8):
            qb = q_ref[:, 128 * blk:128 * (blk + 1)]
            acc = None
            for half in range(2):
                h = 2 * blk + half
                k_ext, v_ext = ext[h // 8][half]
                s = _dot_nt(qb, k_ext) * ATTN_SCALE - ALIBI_SLOPES[h] * dist
                s = jnp.where(valid, s, NEG)
                sink = sink_ref[h]
                m = jnp.maximum(jnp.max(s, axis=-1, keepdims=True), sink)
                p = jnp.exp(s - m)
                den = jnp.sum(p, axis=-1, keepdims=True) + jnp.exp(sink - m)
                pn = (p * (1.0 / den)).astype(bf16)
                oh = _dot(pn, v_ext)
                acc = oh if acc is None else acc + oh
                lse = jnp.where(lane == h, m + jnp.log(den), lse)
            o_blocks.append(acc.astype(bf16))
        o_ref[...] = jnp.concatenate(o_blocks, axis=1)
        lse_ref[...] = lse

    return pl.pallas_call(
        body, name="attn_fwd", grid=(nb,),
        in_specs=[pl.BlockSpec(memory_space=pltpu.SMEM), _rows(T, D_MODEL),
                  pl.BlockSpec((T, 256), lambda i: (jnp.maximum(i - 1, 0), 0)), _rows(T, 256), _ANY_SPEC],
        out_specs=[_rows(T, D_MODEL), _rows(T, 128), _ANY_SPEC],
        out_shape=[jax.ShapeDtypeStruct((L, D_MODEL), bf16), jax.ShapeDtypeStruct((L, 128), f32),
                   jax.ShapeDtypeStruct((N_CHIPS, R, D_MODEL), bf16)],
        scratch_shapes=_sems(6),
        compiler_params=_cparams("arbitrary"),
    )(sinks, q, kv, kv, blob)


def _gated_norm(y, z, w):
    sz = _sigmoid(z)
    hg = y * (z * sz)
    ns, rss = [], []
    for g in range(SSD_GROUPS):
        hs = hg[:, 512 * g:512 * (g + 1)]
        rs = lax.rsqrt(jnp.mean(hs * hs, axis=-1, keepdims=True) + RMS_EPS)
        ns.append(hs * rs)
        rss.append(rs)
    n = jnp.concatenate(ns, axis=1)
    return n * w, n, rss, sz


def _outproj_fwd(y, z, o, x, mod, ln_g, ln_b, norm_w, w_out):
    L = x.shape[0]
    tm = DENSE_TM

    def body(y_ref, z_ref, o_ref, x_ref, mod_ref, g_ref, b_ref, nw_ref, w_ref, yn_ref, mix_ref, r1_ref):
        yn, _, _, _ = _gated_norm(y_ref[...], z_ref[...], nw_ref[...])
        ynb = yn.astype(bf16)
        yn_ref[...] = ynb
        mix = (_dot(ynb[:, 0:512], w_ref[0]) + _dot(ynb[:, 512:1024], w_ref[1])
               + _dot(o_ref[:, 0:512], w_ref[2]) + _dot(o_ref[:, 512:1024], w_ref[3]))
        mix_ref[...] = mix
        xhat, _ = _ln_stats(x_ref[...])
        h0 = xhat * g_ref[...] + b_ref[...]
        r1_ref[...] = ALPHA * h0 + (1.0 + mod_ref[2:3, :]) * mix

    v = _full((1, D_MODEL))
    return pl.pallas_call(
        body, name="outproj_fwd", grid=(L // tm,),
        in_specs=[_rows(tm, D_MODEL), _rows(tm, D_MODEL), _rows(tm, D_MODEL), _rows(tm, D_MODEL),
                  _full((8, D_MODEL)), v, v, v, _resident((N_CHIPS, 512, D_MODEL))],
        out_specs=[_rows(tm, D_MODEL)] * 3,
        out_shape=[jax.ShapeDtypeStruct((L, D_MODEL), bf16), jax.ShapeDtypeStruct((L, D_MODEL), f32),
                   jax.ShapeDtypeStruct((L, D_MODEL), f32)],
        compiler_params=_cparams("parallel"),
    )(y, z, o, x, mod, ln_g, ln_b, norm_w, w_out)


A_LN2G, A_LN2B, A_G2, A_B2, A_SC2, A_SH2, A_LN1G, A_LN1B, A_LOSS = range(9)


def _mlp_fwd_bwd(r1, target, mod, ln1_g, ln1_b, ln2_g, ln2_b, w1, b1, w2, b2):
    L = r1.shape[0]
    tm = 256
    nj = D_FF // 1024

    def body(r1_ref, t_ref, mod_ref, g1_ref, bb1_ref, g2_ref, bb2_ref, w1_ref, b1_ref, w2_ref, b2_ref,
             dr1_ref, u2_ref, s_ref, da_ref, df_ref, acc_ref, db1_ref, hr):
        @pl.when(pl.program_id(0) == 0)
        def _():
            acc_ref[...] = jnp.zeros_like(acc_ref)
            db1_ref[...] = jnp.zeros_like(db1_ref)

        sc2, sh2, gate2 = mod_ref[4:5, :], mod_ref[3:4, :], mod_ref[5:6, :]
        xhat1, rstd1 = _ln_stats(r1_ref[...])
        h1 = xhat1 * g1_ref[...] + bb1_ref[...]
        u2f = h1 * (1.0 + sc2) + sh2
        u2 = u2f.astype(bf16)
        u2_ref[...] = u2
        f = jnp.zeros((tm, D_MODEL), f32) + b2_ref[...]
        for j in range(nj):
            cs = slice(1024 * j, 1024 * (j + 1))
            a = _dot(u2, w1_ref[j]) + b1_ref[:, cs]
            hrj = jnp.maximum(a, 0.0)
            hr[:, cs] = hrj
            sj = (hrj * hrj).astype(bf16)
            s_ref[:, cs] = sj
            f = f + _dot(sj, w2_ref[j])
        r2 = ALPHA * h1 + (1.0 + gate2) * f
        xhat2, rstd2 = _ln_stats(r2)
        h2 = xhat2 * g2_ref[...] + bb2_ref[...]
        diff = h2 - t_ref[...]
        dh2 = diff * (1.0 / D_MODEL)

        def add(row, val):
            acc_ref[row:row + 1, :] += jnp.sum(val, axis=0, keepdims=True)

        add(A_LOSS, diff * diff * (0.5 / D_MODEL))
        add(A_LN2G, dh2 * xhat2)
        add(A_LN2B, dh2)
        dr2 = _ln_bwd(dh2, xhat2, rstd2, g2_ref[...])
        add(A_G2, dr2 * f)
        df = dr2 * (1.0 + gate2)
        add(A_B2, df)
        dfb = df.astype(bf16)
        df_ref[...] = dfb
        du2 = jnp.zeros((tm, D_MODEL), f32)
        for j in range(nj):
            cs = slice(1024 * j, 1024 * (j + 1))
            ds = _dot_nt(dfb, w2_ref[j])
            daj = ds * (2.0 * hr[:, cs])
            db1_ref[:, cs] += jnp.sum(daj, axis=0, keepdims=True)
            dajb = daj.astype(bf16)
            da_ref[:, cs] = dajb
            du2 = du2 + _dot_nt(dajb, w1_ref[j])
        add(A_SC2, du2 * h1)
        add(A_SH2, du2)
        dh1 = ALPHA * dr2 + du2 * (1.0 + sc2)
        add(A_LN1G, dh1 * xhat1)
        add(A_LN1B, dh1)
        dr1_ref[...] = _ln_bwd(dh1, xhat1, rstd1, g1_ref[...])

    v = _full((1, D_MODEL))
    return pl.pallas_call(
        body, name="mlp_fwd_bwd", grid=(L // tm,),
        in_specs=[_rows(tm, D_MODEL), _rows(tm, D_MODEL), _full((8, D_MODEL)), v, v, v, v,
                  _resident((N_CHIPS, D_MODEL, D_MODEL)), _full((1, D_FF)), _resident((N_CHIPS, D_MODEL, D_MODEL)), v],
        out_specs=[_rows(tm, D_MODEL), _rows(tm, D_MODEL), _rows(tm, D_FF), _rows(tm, D_FF), _rows(tm, D_MODEL),
                   _full((16, D_MODEL)), _full((1, D_FF))],
        out_shape=[jax.ShapeDtypeStruct((L, D_MODEL), f32), jax.ShapeDtypeStruct((L, D_MODEL), bf16),
                   jax.ShapeDtypeStruct((L, D_FF), bf16), jax.ShapeDtypeStruct((L, D_FF), bf16),
                   jax.ShapeDtypeStruct((L, D_MODEL), bf16), jax.ShapeDtypeStruct((16, D_MODEL), f32),
                   jax.ShapeDtypeStruct((1, D_FF), f32)],
        scratch_shapes=[pltpu.VMEM((tm, D_FF), f32)],
        compiler_params=_cparams("arbitrary"),
    )(r1, target, mod, ln1_g, ln1_b, ln2_g, ln2_b, w1, b1, w2, b2)


def _wgrad(a, b, name):
    L, M = a.shape
    N = b.shape[1]
    tm = min(M, 512)
    tn = next(t for t in (1024, 768, 512, 256, 128) if N % t == 0)

    def body(a_ref, b_ref, o_ref):
        o_ref[...] = _dot_tn(a_ref[...], b_ref[...]).astype(bf16)

    return pl.pallas_call(
        body, name=name, grid=(M // tm, N // tn),
        in_specs=[pl.BlockSpec((L, tm), lambda i, j: (0, i)), pl.BlockSpec((L, tn), lambda i, j: (0, j))],
        out_specs=pl.BlockSpec((tm, tn), lambda i, j: (i, j)),
        out_shape=jax.ShapeDtypeStruct((M, N), bf16),
        compiler_params=_cparams("parallel", "parallel"),
    )(a, b)


def _wgrad_blob(blob, a, b, name, place_of):
    L, M = a.shape
    N = b.shape[1]

    def body(blob_ref, a_ref, b_ref, o_ref):
        o_ref[0] = _dot_tn(a_ref[...], b_ref[...]).astype(bf16)

    return pl.pallas_call(
        body, name=name, grid=(M // WG_TM, N // D_MODEL),
        in_specs=[pl.BlockSpec(memory_space=pl.ANY), pl.BlockSpec((L, WG_TM), lambda t, n: (0, t)),
                  pl.BlockSpec((L, D_MODEL), lambda t, n: (0, n))],
        out_specs=pl.BlockSpec((1, WG_TM, D_MODEL), lambda t, n: (*place_of(t, n), 0)),
        out_shape=jax.ShapeDtypeStruct(blob.shape, bf16), input_output_aliases={0: 0},
        compiler_params=_cparams("parallel", "parallel"),
    )(blob, a, b)


def _outproj_bwd(dr1, mix, y, z, mod, norm_w, w_out, gb):
    L = dr1.shape[0]
    tm = DENSE_TM
    nt = L // tm

    def body(dr1_ref, mix_ref, y_ref, z_ref, mod_ref, nw_ref, w_ref, gb_ref,
             dy_ref, dz_ref, do_ref, dmix_ref, acc_ref, sib_ref, send_sems, recv_sems):
        start, wait = _to_sibling_job(gb_ref, sib_ref, send_sems, recv_sems)

        @pl.when(pl.program_id(0) == 0)
        def _():
            acc_ref[...] = jnp.zeros_like(acc_ref)
            start()

        @pl.when(pl.program_id(0) == nt - 1)
        def _():
            wait()

        dr1 = dr1_ref[...]
        acc_ref[0:1, :] += jnp.sum(dr1 * mix_ref[...], axis=0, keepdims=True)
        dmix = (dr1 * (1.0 + mod_ref[2:3, :])).astype(bf16)
        dmix_ref[...] = dmix
        dyn = jnp.concatenate([_dot_nt(dmix, w_ref[0]), _dot_nt(dmix, w_ref[1])], axis=1)
        do_ref[...] = jnp.concatenate([_dot_nt(dmix, w_ref[2]), _dot_nt(dmix, w_ref[3])], axis=1).astype(bf16)
        yv, zv = y_ref[...], z_ref[...]
        _, n, rss, sz = _gated_norm(yv, zv, nw_ref[...])
        acc_ref[1:2, :] += jnp.sum(dyn * n, axis=0, keepdims=True)
        dn = dyn * nw_ref[...]
        parts = []
        for g in range(SSD_GROUPS):
            sl = slice(512 * g, 512 * (g + 1))
            dng, ng = dn[:, sl], n[:, sl]
            parts.append(rss[g] * (dng - ng * jnp.mean(dng * ng, axis=-1, keepdims=True)))
        dhg = jnp.concatenate(parts, axis=1)
        dy_ref[...] = dhg * (zv * sz)
        dz_ref[...] = (dhg * yv * (sz * (1.0 + zv * (1.0 - sz)))).astype(bf16)

    return pl.pallas_call(
        body, name="outproj_bwd", grid=(nt,),
        in_specs=[_rows(tm, D_MODEL)] * 4 + [_full((8, D_MODEL)), _full((1, D_MODEL)), _resident((N_CHIPS, 512, D_MODEL)),
                  _ANY_SPEC],
        out_specs=[_rows(tm, D_MODEL)] * 4 + [_full((8, D_MODEL)), _ANY_SPEC],
        out_shape=[jax.ShapeDtypeStruct((L, D_MODEL), f32)] + [jax.ShapeDtypeStruct((L, D_MODEL), bf16)] * 3
        + [jax.ShapeDtypeStruct((8, D_MODEL), f32), jax.ShapeDtypeStruct((N_CHIPS,) + gb.shape[2:], bf16)],
        scratch_shapes=_sems(N_CHIPS),
        compiler_params=_cparams("arbitrary"),
    )(dr1, mix, y, z, mod, norm_w, w_out, gb)


def _attn_bwd(q, kv, do, lse, sinks, pb):
    L = q.shape[0]
    nb = L // CHUNK
    T = CHUNK

    def body(sink_ref, q_ref, kvp_ref, kvc_ref, do_ref, lse_ref, pb_ref, dq_ref, dkv_ref, dsink_ref, chips_ref,
             carry, send_sems, recv_sems):
        n = pl.program_id(0)
        start, wait = _to_chips_job(pb_ref, chips_ref, send_sems, recv_sems)

        @pl.when(n == 0)
        def _():
            carry[...] = jnp.zeros_like(carry)
            dsink_ref[...] = jnp.zeros_like(dsink_ref)
            start()

        @pl.when(n < nb)
        def _():
            first = n == 0
            ext = _kv_halves(kvp_ref[...], kvc_ref[...], first)
            dist, valid = _attn_masks(first)
            lane1 = lax.broadcasted_iota(jnp.int32, (1, 128), 1)
            lse = lse_ref[...]
            acck = [None, None]
            accv = [None, None]
            dsink = jnp.zeros((1, 128), f32)
            dq_blocks = []
            for blk in range(8):
                qb = q_ref[:, 128 * blk:128 * (blk + 1)]
                dob = do_ref[:, 128 * blk:128 * (blk + 1)]
                qt = qb.astype(f32).T.astype(bf16)
                dot_ = dob.astype(f32).T.astype(bf16)
                dq_acc = None
                for half in range(2):
                    h = 2 * blk + half
                    g = h // 8
                    k_ext, v_ext = ext[g][half]
                    s = _dot_nt(qb, k_ext) * ATTN_SCALE - ALIBI_SLOPES[h] * dist
                    lse_h = lse[:, h:h + 1]
                    p = jnp.where(valid, jnp.exp(s - lse_h), 0.0)
                    dp = _dot_nt(dob, v_ext)
                    delta = jnp.sum(p * dp, axis=-1, keepdims=True)
                    ds = (p * (dp - delta) * ATTN_SCALE).astype(bf16)
                    psink = jnp.exp(sink_ref[h] - lse_h)
                    dsink = dsink - jnp.where(lane1 == h, jnp.sum(psink * delta, axis=0, keepdims=True), 0.0)
                    dqh = _dot(ds, k_ext)
                    dq_acc = dqh if dq_acc is None else dq_acc + dqh
                    rows = slice(HEAD_DIM * half, HEAD_DIM * (half + 1))
                    dkh = _dot(qt[rows, :], ds)
                    dvh = _dot(dot_[rows, :], p.astype(bf16))
                    acck[g] = dkh if acck[g] is None else acck[g] + dkh
                    accv[g] = dvh if accv[g] is None else accv[g] + dvh
                dq_blocks.append(dq_acc.astype(bf16))
            dq_ref[...] = jnp.concatenate(dq_blocks, axis=1)
            dsink_ref[...] += dsink
            dkv = jnp.concatenate([jnp.concatenate(acck, axis=0).T, jnp.concatenate(accv, axis=0).T], axis=1)
            dkv_ref[...] = (carry[...] + dkv[0:T, :]).astype(bf16)
            carry[...] = dkv[T:2 * T, :]

        @pl.when(n == nb)
        def _():
            dkv_ref[...] = carry[...].astype(bf16)
            wait()

    cur = lambda i: (jnp.minimum(i, nb - 1), 0)
    return pl.pallas_call(
        body, name="attn_bwd", grid=(nb + 1,),
        in_specs=[pl.BlockSpec(memory_space=pltpu.SMEM), pl.BlockSpec((T, D_MODEL), cur),
                  pl.BlockSpec((T, 256), lambda i: (jnp.maximum(jnp.minimum(i, nb - 1) - 1, 0), 0)),
                  pl.BlockSpec((T, 256), cur), pl.BlockSpec((T, D_MODEL), cur), pl.BlockSpec((T, 128), cur), _ANY_SPEC],
        out_specs=[pl.BlockSpec((T, D_MODEL), cur), pl.BlockSpec((T, 256), lambda i: (jnp.maximum(i - 1, 0), 0)),
                   _full((1, 128)), _ANY_SPEC],
        out_shape=[jax.ShapeDtypeStruct((L, D_MODEL), bf16), jax.ShapeDtypeStruct((L, 256), bf16),
                   jax.ShapeDtypeStruct((1, 128), f32), jax.ShapeDtypeStruct((N_CHIPS - 1,) + pb.shape[1:], bf16)],
        scratch_shapes=[pltpu.VMEM((T, 256), f32)] + _sems(N_CHIPS - 1),
        compiler_params=_cparams("arbitrary"),
    )(sinks, q, kv, kv, do, lse, pb)


def _ssd_bwd(xc, dt_raw, dy, prev_all, dt_bias, a_log, d_skip_x, e_mat, g):
    L = xc.shape[0]
    nc = L // CHUNK
    T = CHUNK
    RG = g.shape[0]

    def body(xc_ref, dt_ref, dy_ref, prev_ref, dtb_ref, al_ref, dsk_ref, e_ref, g_in_ref,
             dxc_ref, ddt_ref, acc_ref, dd_ref, g_ref, dst, dxs_s, send_sems, recv_sems):
        start, wait = _share_job(g_ref, send_sems, recv_sems, RG)

        @pl.when(pl.program_id(0) == 0)
        def _():
            dst[...] = jnp.zeros_like(dst)
            acc_ref[...] = jnp.zeros_like(acc_ref)
            dd_ref[...] = jnp.zeros_like(dd_ref)
            start()

        @pl.when(pl.program_id(0) == nc - 1)
        def _():
            wait()

        lane1 = lax.broadcasted_iota(jnp.int32, (1, 128), 1)
        a_row = jnp.where(lane1 < SSD_HEADS, -jnp.exp(al_ref[...]), 0.0)
        e_mat_v = e_ref[...]
        dt, a, cum, dtx, cumx, r, c = _ssd_chunk_common(dt_ref[...], dtb_ref[...], a_row, e_mat_v)
        cum_t = cum.T
        ex = jnp.exp(cumx)
        last = cumx[T - 1:T, :]
        wx = jnp.exp(last - cumx)
        cdx = jnp.exp(last)
        xs = xc_ref[:, 0:SSD_WIDTH]
        X = xs * dtx
        Xb = X.astype(bf16)
        Xdb = (X * wx).astype(bf16)
        dyv = dy_ref[...]
        prev = prev_ref[0]
        prevb = prev.astype(bf16)
        dnew = dst[...]
        dnewb = dnew.astype(bf16)
        tri = c <= r
        lane = lax.broadcasted_iota(jnp.int32, (T, 128), 1)
        sub = lax.broadcasted_iota(jnp.int32, (128, T), 0)
        lo = lane < HEAD_DIM

        def red(vals, g):
            return _dot_sel_nt(vals, e_mat_v[:, 512 * g:512 * (g + 1)])

        de = jnp.zeros((T, 128), f32)
        dw = jnp.zeros((T, 128), f32)
        ddt_x = jnp.zeros((T, 128), f32)
        dcum_col = jnp.zeros((T, 128), f32)
        dcum_row = jnp.zeros((128, T), f32)
        dprev_parts, dBs, dCs = [], [], []
        for g in range(SSD_GROUPS):
            s5 = slice(512 * g, 512 * (g + 1))
            Bg = xc_ref[:, 1024 + 128 * g:1152 + 128 * g].astype(bf16)
            Cg = xc_ref[:, 1280 + 128 * g:1408 + 128 * g].astype(bf16)
            G = _dot_nt(Cg, Bg)
            Z = _dot(Cg, prevb[:, s5])
            dyg = dyv[:, s5]
            dZb = (dyg * ex[:, s5]).astype(bf16)
            dXd = _dot(Bg, dnewb[:, s5])
            dC = _dot_nt(dZb, prevb[:, s5])
            dB = _dot_nt(Xdb[:, s5], dnewb[:, s5])
            dprev_parts.append(_dot_tn(Cg, dZb) + dnew[:, s5] * cdx[:, s5])
            de = de + red(dyg * Z, g)
            dw = dw + red(dXd * X[:, s5], g)
            dXg = dXd * wx[:, s5]
            dG = jnp.zeros((T, T), f32)
            for j in range(4):
                blk = 4 * g + j
                sl = slice(128 * blk, 128 * (blk + 1))
                Xblk = Xb[:, sl]
                dyblk = dyv[:, sl]
                dyblk_b = dyblk.astype(bf16)
                dxh = []
                for half in range(2):
                    h = 2 * blk + half
                    seg = jnp.minimum(cum[:, h:h + 1] - cum_t[h:h + 1, :], 0.0)
                    Lm = jnp.where(tri, jnp.exp(seg), 0.0)
                    M = G * Lm
                    dyh = jnp.where(lo if half == 0 else ~lo, dyblk, 0.0).astype(bf16)
                    dM = _dot_nt(dyh, Xblk)
                    dG = dG + dM * Lm
                    Q = dM * M
                    dcum_col = dcum_col + jnp.where(lane == h, jnp.sum(Q, axis=1, keepdims=True), 0.0)
                    dcum_row = dcum_row + jnp.where(sub == h, jnp.sum(Q, axis=0, keepdims=True), 0.0)
                    dxh.append(_dot_tn(M.astype(bf16), dyblk_b))
                dXblk = dXg[:, 128 * j:128 * (j + 1)] + jnp.where(lo, dxh[0], dxh[1])
                xsb = xs[:, sl]
                dxs_s[:, sl] = dXblk * dtx[:, sl] + dsk_ref[:, sl] * dyblk
                ddt_x = ddt_x + _dot_sel_nt(dXblk * xsb, e_mat_v[:, sl])
                dd_ref[:, sl] += jnp.sum(dyblk * xsb, axis=0, keepdims=True)
            dGb = dG.astype(bf16)
            dCs.append(dC + _dot(dGb, Bg))
            dBs.append(dB + _dot_tn(dGb, Cg))
        e16 = jnp.exp(cum)
        cum_last = cum[T - 1:T, :]
        w16 = jnp.exp(cum_last - cum)
        dcd = jnp.sum(dnew * prev, axis=0, keepdims=True)
        dcd16 = red(dcd[:, 0:512], 0) + red(dcd[:, 512:1024], 1)
        dww = dw * w16
        extra = jnp.sum(dww, axis=0, keepdims=True) + dcd16 * jnp.exp(cum_last)
        rowi = lax.broadcasted_iota(jnp.int32, (T, 128), 0)
        dcum = dcum_col - dcum_row.T + de * e16 - dww + jnp.where(rowi == T - 1, extra, 0.0)
        da = _sel_dot((c >= r).astype(bf16), dcum)
        ddt = ddt_x + da * a_row
        acc_ref[0:1, :] += jnp.sum(da * dt, axis=0, keepdims=True)
        ddt_raw = jnp.where(lane < SSD_HEADS, ddt * _sigmoid(dt_ref[...] + dtb_ref[...]), 0.0)
        ddt_ref[...] = ddt_raw
        acc_ref[1:2, :] += jnp.sum(ddt_raw, axis=0, keepdims=True)
        dxc_ref[:, 0:SSD_WIDTH] = dxs_s[...]
        dxc_ref[:, 1024:1280] = jnp.concatenate(dBs, axis=1)
        dxc_ref[:, 1280:1536] = jnp.concatenate(dCs, axis=1)
        dst[...] = jnp.concatenate(dprev_parts, axis=1)

    rev = lambda i: (nc - 1 - i, 0)
    return pl.pallas_call(
        body, name="ssd_bwd", grid=(nc,),
        in_specs=[pl.BlockSpec((T, CONV_DIM), rev), pl.BlockSpec((T, 128), rev), pl.BlockSpec((T, SSD_WIDTH), rev),
                  pl.BlockSpec((1, SSD_STATE, SSD_WIDTH), lambda i: (nc - 1 - i, 0, 0)),
                  _full((1, 128)), _full((1, 128)), _full((1, SSD_WIDTH)), _full((128, SSD_WIDTH)), _ANY_SPEC],
        out_specs=[pl.BlockSpec((T, CONV_DIM), rev), pl.BlockSpec((T, 128), rev), _full((8, 128)),
                   _full((1, SSD_WIDTH)), _ANY_SPEC],
        out_shape=[jax.ShapeDtypeStruct((L, CONV_DIM), f32), jax.ShapeDtypeStruct((L, 128), f32),
                   jax.ShapeDtypeStruct((8, 128), f32), jax.ShapeDtypeStruct((1, SSD_WIDTH), f32),
                   jax.ShapeDtypeStruct(g.shape, f32)],
        input_output_aliases={8: 4},
        scratch_shapes=[pltpu.VMEM((SSD_STATE, SSD_WIDTH), f32), pltpu.VMEM((T, SSD_WIDTH), f32)] + _sems(1),
        compiler_params=_cparams("arbitrary"),
    )(xc, dt_raw, dy, prev_all, dt_bias, a_log, d_skip_x, e_mat, g)


def _conv_bwd(dxc, xr, conv_w, conv_b):
    L = dxc.shape[0]
    tm = 256
    nt = L // tm

    def body(dxc_ref, xr_ref, xh_ref, cw_ref, cb_ref, dxr_ref, acc_ref, carry, buf, buf2):
        i = pl.program_id(0)

        @pl.when(i == 0)
        def _():
            carry[...] = jnp.zeros_like(carry)
            acc_ref[...] = jnp.zeros_like(acc_ref)

        buf[0:8, :] = jnp.where(i == nt - 1, 0.0, xh_ref[...])
        buf[8:8 + tm, :] = xr_ref[...]
        pre = cb_ref[...] + cw_ref[0:1, :] * buf[5:5 + tm, :]
        for k in range(1, CONV_K):
            pre = pre + cw_ref[k:k + 1, :] * buf[5 + k:5 + k + tm, :]
        sg = _sigmoid(pre)
        dpre = dxc_ref[...] * (sg * (1.0 + pre * (1.0 - sg)))
        acc_ref[4:5, :] += jnp.sum(dpre, axis=0, keepdims=True)
        for k in range(CONV_K):
            acc_ref[k:k + 1, :] += jnp.sum(dpre * buf[5 + k:5 + k + tm, :], axis=0, keepdims=True)
        buf2[0:tm, :] = dpre
        buf2[tm:tm + 8, :] = carry[...]
        du = cw_ref[0:1, :] * buf2[3:3 + tm, :]
        for k in range(1, CONV_K):
            du = du + cw_ref[k:k + 1, :] * buf2[3 - k:3 - k + tm, :]
        dxr_ref[...] = du.astype(bf16)
        carry[...] = dpre[0:8, :]

    rev = lambda i: (nt - 1 - i, 0)
    return pl.pallas_call(
        body, name="conv_bwd", grid=(nt,),
        in_specs=[pl.BlockSpec((tm, CONV_DIM), rev), pl.BlockSpec((tm, CONV_DIM), rev),
                  pl.BlockSpec((8, CONV_DIM), lambda i: (jnp.maximum((nt - 1 - i) * (tm // 8) - 1, 0), 0)),
                  _full((CONV_K, CONV_DIM)), _full((1, CONV_DIM))],
        out_specs=[pl.BlockSpec((tm, CONV_DIM), rev), _full((8, CONV_DIM))],
        out_shape=[jax.ShapeDtypeStruct((L, CONV_DIM), bf16), jax.ShapeDtypeStruct((8, CONV_DIM), f32)],
        scratch_shapes=[pltpu.VMEM((8, CONV_DIM), f32), pltpu.VMEM((tm + 8, CONV_DIM), f32),
                        pltpu.VMEM((tm + 8, CONV_DIM), f32)],
        compiler_params=_cparams("arbitrary"),
    )(dxc, xr, xr, conv_w, conv_b)


def _inproj_bwd(dz, dxr, dq, dkv, ddt, dr1, x, mod, ln_g, ln_b, w_in, pb):
    L = x.shape[0]
    tm = DENSE_TM
    nt = L // tm

    def body(dz_ref, dxr_ref, dq_ref, dkv_ref, ddt_ref, dr1_ref, x_ref, mod_ref, g_ref, b_ref, w_ref, pb_ref,
             dx_ref, acc_ref, chips_ref, send_sems, recv_sems):
        start, wait = _to_chips_job(pb_ref, chips_ref, send_sems, recv_sems)

        @pl.when(pl.program_id(0) == 0)
        def _():
            acc_ref[...] = jnp.zeros_like(acc_ref)
            start()

        @pl.when(pl.program_id(0) == nt - 1)
        def _():
            wait()

        du1 = (_dot(dz_ref[...], w_ref[C_Z:C_XBC, :]) + _dot(dxr_ref[...], w_ref[C_XBC:C_Q, :])
               + _dot(dq_ref[...], w_ref[C_Q:C_KV, :]) + _dot(dkv_ref[...], w_ref[C_KV:C_DT, :])
               + _dot(ddt_ref[...].astype(bf16), w_ref[C_DT:C_END, :]))
        xhat, rstd = _ln_stats(x_ref[...])
        h0 = xhat * g_ref[...] + b_ref[...]
        acc_ref[0:1, :] += jnp.sum(du1 * h0, axis=0, keepdims=True)
        acc_ref[1:2, :] += jnp.sum(du1, axis=0, keepdims=True)
        dh0 = du1 * (1.0 + mod_ref[1:2, :]) + ALPHA * dr1_ref[...]
        acc_ref[2:3, :] += jnp.sum(dh0 * xhat, axis=0, keepdims=True)
        acc_ref[3:4, :] += jnp.sum(dh0, axis=0, keepdims=True)
        dx_ref[...] = _ln_bwd(dh0, xhat, rstd, g_ref[...])

    v = _full((1, D_MODEL))
    return pl.pallas_call(
        body, name="inproj_bwd", grid=(nt,),
        in_specs=[_rows(tm, D_MODEL), _rows(tm, CONV_DIM), _rows(tm, D_MODEL), _rows(tm, 256), _rows(tm, 128),
                  _rows(tm, D_MODEL), _rows(tm, D_MODEL), _full((8, D_MODEL)), v, v, _resident((C_END, D_MODEL)),
                  _ANY_SPEC],
        out_specs=[_rows(tm, D_MODEL), _full((8, D_MODEL)), _ANY_SPEC],
        out_shape=[jax.ShapeDtypeStruct((L, D_MODEL), f32), jax.ShapeDtypeStruct((8, D_MODEL), f32),
                   jax.ShapeDtypeStruct((N_CHIPS - 1,) + pb.shape[1:], bf16)],
        scratch_shapes=_sems(N_CHIPS - 1),
        compiler_params=_cparams("arbitrary"),
    )(dz, dxr, dq, dkv, ddt, dr1, x, mod, ln_g, ln_b, w_in, pb)


def _adamw_math(w, g, m, v):
    m = ADAM_B1 * m + (1.0 - ADAM_B1) * g
    v = ADAM_B2 * v + (1.0 - ADAM_B2) * (g * g)
    m_hat = m / (1.0 - ADAM_B1 ** ADAM_STEP)
    v_hat = v / (1.0 - ADAM_B2 ** ADAM_STEP)
    delta = -ADAM_LR * (m_hat / (jnp.sqrt(v_hat) + ADAM_EPS) + ADAM_WD * w)
    return delta, m, v


def _adamw(w, g, m, v, name):
    R, C = w.shape

    def body(w_ref, g_ref, m_ref, v_ref, d_ref, m2_ref, v2_ref):
        d_ref[...], m2_ref[...], v2_ref[...] = _adamw_math(w_ref[...], g_ref[...], m_ref[...], v_ref[...])

    if R <= 256 or R % 256 == 0:
        tr = min(R, 256)
        spec, steps = pl.BlockSpec((tr, C), lambda i: (i, 0)), R // tr
    else:
        spec, steps = pl.BlockSpec((R, 256), lambda i: (0, i)), C // 256
    return pl.pallas_call(
        body, name=name, grid=(steps,), in_specs=[spec] * 4, out_specs=[spec] * 3,
        out_shape=[jax.ShapeDtypeStruct((R, C), f32)] * 3, compiler_params=_cparams("parallel"),
    )(w, g, m, v)


ADA_COLS = 6 * D_MODEL // N_CHIPS
ADA_TN = 512


def _ada_fwd(c_all, ada_w, ada_b):
    def body(c_ref, w_ref, b_ref, o_ref):
        cv = c_ref[...]
        o_ref[...] = _dot_exact(cv * _sigmoid(cv), w_ref[...]) + b_ref[...]

    return pl.pallas_call(
        body, name="ada_fwd", grid=(ADA_COLS // ADA_TN,),
        in_specs=[_full((N_DEV, D_MODEL)), pl.BlockSpec((D_MODEL, ADA_TN), lambda j: (0, j)),
                  pl.BlockSpec((1, ADA_TN), lambda j: (0, j))],
        out_specs=pl.BlockSpec((N_DEV, ADA_TN), lambda j: (0, j)),
        out_shape=jax.ShapeDtypeStruct((N_DEV, ADA_COLS), f32), compiler_params=_cparams("parallel"),
    )(c_all, ada_w, ada_b)


def _ada_bwd(c_all, dmod, w, m, v):
    def body(c_ref, d_ref, w_ref, m_ref, v_ref, g_ref, dl_ref, m2_ref, v2_ref):
        cv = c_ref[...]
        g = lax.dot_general(cv * _sigmoid(cv), d_ref[...], (((0,), (0,)), ((), ())), preferred_element_type=f32,
                            precision=lax.Precision.HIGHEST)
        g_ref[...] = g
        dl_ref[...], m2_ref[...], v2_ref[...] = _adamw_math(w_ref[...], g, m_ref[...], v_ref[...])

    wspec = pl.BlockSpec((D_MODEL, ADA_TN), lambda j: (0, j))
    return pl.pallas_call(
        body, name="ada_bwd", grid=(ADA_COLS // ADA_TN,),
        in_specs=[_full((N_DEV, D_MODEL)), pl.BlockSpec((N_DEV, ADA_TN), lambda j: (0, j)), wspec, wspec, wspec],
        out_specs=[wspec] * 4, out_shape=[jax.ShapeDtypeStruct((D_MODEL, ADA_COLS), f32)] * 4,
        compiler_params=_cparams("parallel"),
    )(c_all, dmod, w, m, v)


def _small_update(gathered, w, m, v):
    n = w.shape[1]

    def body(g_ref, w_ref, m_ref, v_ref, gs_ref, d_ref, m2_ref, v2_ref):
        g = g_ref[0:1, :]
        for i in range(1, N_DEV):
            g = g + g_ref[i:i + 1, :]
        gs_ref[...] = g
        d_ref[...], m2_ref[...], v2_ref[...] = _adamw_math(w_ref[...], g, m_ref[...], v_ref[...])

    return pl.pallas_call(body, name="small_update", out_shape=[jax.ShapeDtypeStruct((1, n), f32)] * 4,
                          compiler_params=_cparams())(gathered, w, m, v)


def _place():
    return lax.axis_index("x"), lax.axis_index("y"), lax.axis_index("c")


def _flip(x, y, c, m):
    return (1 - x if m & 4 else x, 1 - y if m & 2 else y, 1 - c if m & 1 else c)


_VMEM_SPEC = pl.BlockSpec(memory_space=pltpu.VMEM)
_ANY_SPEC = pl.BlockSpec(memory_space=pl.ANY)


def _allgather8(v, name):
    n = v.shape[1]

    def body(v_ref, out_ref, send_sems, recv_sems, local_sem):
        x, y, c = _place()

        def rows(px, py, pc):
            return out_ref.at[pl.ds(pl.multiple_of((4 * px + 2 * py + pc) * 8, 8), 8), :]

        def copy(m, src, dst, to):
            return pltpu.make_async_remote_copy(src_ref=src, dst_ref=dst, send_sem=send_sems.at[m - 1],
                                                recv_sem=recv_sems.at[m - 1], device_id=to, device_id_type=MESH)

        mine = pltpu.make_async_copy(v_ref, rows(x, y, c), local_sem)
        mine.start()
        sends = [copy(m, v_ref, rows(x, y, c), _flip(x, y, c, m)) for m in range(1, N_DEV)]
        for cp in sends:
            cp.start()
        for m in range(1, N_DEV):
            peer = _flip(x, y, c, m)
            copy(m, v_ref, rows(*peer), peer).wait_recv()
        for cp in sends:
            cp.wait_send()
        mine.wait()

    return pl.pallas_call(
        body, name=name, out_shape=jax.ShapeDtypeStruct((8 * N_DEV, n), f32), in_specs=[_VMEM_SPEC],
        out_specs=_VMEM_SPEC,
        scratch_shapes=[pltpu.SemaphoreType.DMA((N_DEV - 1,)), pltpu.SemaphoreType.DMA((N_DEV - 1,)),
                        pltpu.SemaphoreType.DMA],
    )(v)


def _mod_exchange(mod_all):
    def body(src_ref, out_ref, send_sems, recv_sems, local_sem):
        x, y, c = _place()

        def copy(m, src, dst, to):
            return pltpu.make_async_remote_copy(src_ref=src, dst_ref=dst, send_sem=send_sems.at[m - 1],
                                                recv_sem=recv_sems.at[m - 1], device_id=to, device_id_type=MESH)

        mine = pltpu.make_async_copy(src_ref.at[4 * x + 2 * y + c], out_ref.at[2 * x + y], local_sem)
        mine.start()
        sends = []
        for m in range(1, N_CHIPS):
            px, py, pc = _flip(x, y, c, 2 * m)
            sends.append(copy(m, src_ref.at[4 * px + 2 * py + pc], out_ref.at[2 * x + y], (px, py, pc)))
        for cp in sends:
            cp.start()
        for m in range(1, N_CHIPS):
            px, py, pc = _flip(x, y, c, 2 * m)
            copy(m, src_ref.at[0], out_ref.at[2 * px + py], (px, py, pc)).wait_recv()
        for cp in sends:
            cp.wait_send()
        mine.wait()

    return pl.pallas_call(
        body, name="mod_exchange", out_shape=jax.ShapeDtypeStruct((N_CHIPS, 12, 128), f32), in_specs=[_VMEM_SPEC],
        out_specs=_VMEM_SPEC,
        scratch_shapes=[pltpu.SemaphoreType.DMA((N_CHIPS - 1,)), pltpu.SemaphoreType.DMA((N_CHIPS - 1,)),
                        pltpu.SemaphoreType.DMA],
    )(mod_all)


def _remote(src, dst, send_sems, recv_sems, k, to):
    return pltpu.make_async_remote_copy(src_ref=src, dst_ref=dst, send_sem=send_sems.at[k], recv_sem=recv_sems.at[k],
                                        device_id=to, device_id_type=MESH)


def _gather_job(blob_ref, out_ref, send_sems, recv_sems, R):
    x, y, c = _place()
    sib = (x, y, 1 - c)
    hr = R // 2

    def half(px, py, pc):
        return out_ref.at[2 * px + py, pl.ds(pl.multiple_of(pc * hr, 16), hr), :]

    my_half = blob_ref.at[pl.ds(pl.multiple_of(c * hr, 16), hr), :]

    def first():
        return [_remote(my_half, half(x, y, c), send_sems, recv_sems, m - 1, _flip(x, y, c, 2 * m))
                for m in range(1, N_CHIPS)]

    def start():
        for cp in first():
            cp.start()

    def finish():
        passed = []
        for m in range(1, N_CHIPS):
            px, py, pc = _flip(x, y, c, 2 * m)
            _remote(my_half, half(px, py, pc), send_sems, recv_sems, m - 1, (px, py, pc)).wait_recv()
            fwd = _remote(half(px, py, pc), half(px, py, pc), send_sems, recv_sems, 2 + m, sib)
            fwd.start()
            passed.append(fwd)
        for m in range(1, N_CHIPS):
            px, py, pc = _flip(x, y, c, 2 * m)
            _remote(my_half, half(px, py, 1 - pc), send_sems, recv_sems, 2 + m, sib).wait_recv()
        for cp in first() + passed:
            cp.wait_send()

    return start, finish


def _gather_weights(blob):
    R = blob.shape[0]

    def body(blob_ref, out_ref, send_sems, recv_sems, local_sem, stage):
        x, y, c = _place()
        start, finish = _gather_job(blob_ref, out_ref, send_sems, recv_sems, R)
        start()
        for k in range(R // STAGE_ROWS):
            rows = pl.ds(STAGE_ROWS * k, STAGE_ROWS)
            cin = pltpu.make_async_copy(blob_ref.at[rows, :], stage, local_sem)
            cin.start()
            cin.wait()
            cout = pltpu.make_async_copy(stage, out_ref.at[2 * x + y, rows, :], local_sem)
            cout.start()
            cout.wait()
        finish()

    return pl.pallas_call(
        body, name="gather_weights", out_shape=jax.ShapeDtypeStruct((N_CHIPS, R, D_MODEL), bf16),
        in_specs=[_ANY_SPEC], out_specs=_ANY_SPEC,
        scratch_shapes=[pltpu.SemaphoreType.DMA((6,)), pltpu.SemaphoreType.DMA((6,)), pltpu.SemaphoreType.DMA,
                        pltpu.VMEM((STAGE_ROWS, D_MODEL), bf16)],
    )(blob)


def _to_sibling_job(g_ref, out_ref, send_sems, recv_sems):
    x, y, c = _place()

    def cps():
        return [_remote(g_ref.at[j, 1 - c], out_ref.at[j], send_sems, recv_sems, j, (x, y, 1 - c)) for j in range(N_CHIPS)]

    def start():
        for cp in cps():
            cp.start()

    def wait():
        for cp in cps():
            cp.wait()

    return start, wait


def _to_chips_job(p_ref, out_ref, send_sems, recv_sems):
    x, y, c = _place()

    def cps():
        out = []
        for m in range(1, N_CHIPS):
            px, py, pc = _flip(x, y, c, 2 * m)
            out.append(_remote(p_ref.at[2 * px + py], out_ref.at[m - 1], send_sems, recv_sems, m - 1, (px, py, pc)))
        return out

    def start():
        for cp in cps():
            cp.start()

    def wait():
        for cp in cps():
            cp.wait()

    return start, wait


def _share_job(g_ref, send_sems, recv_sems, R):
    x, y, c = _place()

    def rows(pc):
        return g_ref.at[pl.ds(pl.multiple_of(pc * (R // 2), 8), R // 2), :]

    def start():
        _remote(rows(c), rows(c), send_sems, recv_sems, 0, (x, y, 1 - c)).start()

    def wait():
        _remote(rows(c), rows(1 - c), send_sems, recv_sems, 0, (x, y, 1 - c)).wait_recv()
        _remote(rows(c), rows(c), send_sems, recv_sems, 0, (x, y, 1 - c)).wait_send()

    return start, wait


def _sems(n):
    return [pltpu.SemaphoreType.DMA((n,)), pltpu.SemaphoreType.DMA((n,))]


def _rs_to_sibling(gb):
    def body(g_ref, out_ref, send_sems, recv_sems):
        start, wait = _to_sibling_job(g_ref, out_ref, send_sems, recv_sems)
        start()
        wait()

    return pl.pallas_call(
        body, name="rs_to_sibling", out_shape=jax.ShapeDtypeStruct((N_CHIPS,) + gb.shape[2:], bf16),
        in_specs=[_ANY_SPEC], out_specs=_ANY_SPEC, scratch_shapes=_sems(N_CHIPS),
    )(gb)


def _rs_share(g):
    R = g.shape[0]

    def body(g_ref, out_ref, send_sems, recv_sems):
        start, wait = _share_job(out_ref, send_sems, recv_sems, R)
        start()
        wait()

    return pl.pallas_call(
        body, name="rs_share", out_shape=jax.ShapeDtypeStruct(g.shape, f32), in_specs=[_ANY_SPEC],
        out_specs=_ANY_SPEC, input_output_aliases={0: 0}, scratch_shapes=_sems(1),
    )(g)


RS_TR = 256


def _rs_sum_pair(place, gb, recv, name):
    hr = gb.shape[2]

    def body(pl_ref, g_ref, r_ref, o_ref):
        o_ref[0] = (g_ref[0, 0].astype(f32) + r_ref[0].astype(f32)).astype(bf16)

    return pl.pallas_call(
        body, name=name,
        grid_spec=pltpu.PrefetchScalarGridSpec(
            num_scalar_prefetch=1, grid=(N_CHIPS, hr // RS_TR),
            in_specs=[pl.BlockSpec((1, 1, RS_TR, D_MODEL), lambda j, i, p: (j, p[0], i, 0)),
                      pl.BlockSpec((1, RS_TR, D_MODEL), lambda j, i, p: (j, i, 0))],
            out_specs=pl.BlockSpec((1, RS_TR, D_MODEL), lambda j, i, p: (j, i, 0))),
        out_shape=jax.ShapeDtypeStruct((N_CHIPS, hr, D_MODEL), bf16),
        compiler_params=_cparams("parallel", "parallel"),
    )(place, gb, recv)


def _rs_sum_chips(place, gb, recv_sib, recv_chips, name):
    hr = gb.shape[2]
    nt = hr // RS_TR

    def body(pl_ref, g_ref, r1_ref, r2_ref, o_ref):
        acc = g_ref[0, 0].astype(f32) + r1_ref[0].astype(f32)
        for k in range(N_CHIPS - 1):
            acc = acc + r2_ref[k].astype(f32)
        o_ref[...] = acc

    return pl.pallas_call(
        body, name=name,
        grid_spec=pltpu.PrefetchScalarGridSpec(
            num_scalar_prefetch=1, grid=(nt,),
            in_specs=[pl.BlockSpec((1, 1, RS_TR, D_MODEL), lambda i, p: (p[1], p[0], i, 0)),
                      pl.BlockSpec((1, RS_TR, D_MODEL), lambda i, p: (p[1], i, 0)),
                      pl.BlockSpec((N_CHIPS - 1, RS_TR, D_MODEL), lambda i, p: (0, i, 0))],
            out_specs=pl.BlockSpec((RS_TR, D_MODEL), lambda i, p: (p[0] * nt + i, 0))),
        out_shape=jax.ShapeDtypeStruct((2 * hr, D_MODEL), f32),
        compiler_params=_cparams("parallel"),
    )(place, gb, recv_sib, recv_chips)


SMALL_SLOTS = (("ada_b", 6144), ("ln_in_g", 1024), ("ln_in_b", 1024), ("conv_b", 1536), ("dt_bias", 128), ("a_log", 128),
               ("d_skip", 128), ("ssd_norm_w", 1024), ("attn_sinks", 128), ("ln1_g", 1024), ("ln1_b", 1024),
               ("b_ff1", 4096), ("b_ff2", 1024), ("ln2_g", 1024), ("ln2_b", 1024), ("conv_w", 6144), ("loss", 1024))
SMALL_N = sum(n for _, n in SMALL_SLOTS)
assert SMALL_N % 1024 == 0


def _pack_small(vals):
    parts = []
    for name, n in SMALL_SLOTS:
        v = vals.get(name)
        v = jnp.zeros((n,), f32) if v is None else v.reshape(-1).astype(f32)
        parts.append(jnp.pad(v, (0, n - v.shape[0])))
    return jnp.concatenate(parts)


def _unpack_small(vec):
    out, off = {}, 0
    for name, n in SMALL_SLOTS:
        out[name] = vec[off:off + n]
        off += n
    return out


def _pad128(v):
    v = v.reshape(1, -1)
    return jnp.pad(v, ((0, 0), (0, 128 - v.shape[1])))


def _row(v):
    return v.reshape(1, -1)


def _w_in_layout(w):
    return jnp.concatenate([w[:2560], w[2576:3856], w[2560:2576], jnp.zeros((C_END - PROJ_WIDTH, D_MODEL), w.dtype)], axis=0)


def kernel(x, c, ln_in_g, ln_in_b, ada_w, ada_b, w_in, conv_w, conv_b, dt_bias, a_log, d_skip, ssd_norm_w, attn_sinks, w_out, ln1_g, ln1_b, w_ff1, b_ff1, w_ff2, b_ff2, ln2_g, ln2_b, loss_target, m_ln_in_g, m_ln_in_b, m_ada_w, m_ada_b, m_w_in, m_conv_w, m_conv_b, m_dt_bias, m_a_log, m_d_skip, m_ssd_norm_w, m_attn_sinks, m_w_out, m_ln1_g, m_ln1_b, m_w_ff1, m_b_ff1, m_w_ff2, m_b_ff2, m_ln2_g, m_ln2_b, v_ln_in_g, v_ln_in_b, v_ada_w, v_ada_b, v_w_in, v_conv_w, v_conv_b, v_dt_bias, v_a_log, v_d_skip, v_ssd_norm_w, v_attn_sinks, v_w_out, v_ln1_g, v_ln1_b, v_w_ff1, v_b_ff1, v_w_ff2, v_b_ff2, v_ln2_g, v_ln2_b):
    xi, yi, ci = _place()
    chip = 2 * xi + yi
    place = jnp.stack([ci, chip]).astype(jnp.int32)
    x2, tgt = x[0], loss_target[0]
    w_shard_cols = PROJ_WIDTH // N_CHIPS

    cond = jnp.concatenate([c.reshape(-1), conv_w.reshape(-1), jnp.zeros((512,), f32)]).reshape(8, 384)
    cond_all = _allgather8(cond, "gather_cond").reshape(N_DEV, 3072)
    c_all = cond_all[:, :D_MODEL]
    conv_w_full = jnp.concatenate([cond_all[2 * j, D_MODEL:D_MODEL + 1536].reshape(CONV_K, 384) for j in range(N_CHIPS)], axis=1)

    ada_b_mine = lax.dynamic_slice(ada_b, (0, chip * ADA_COLS), (1, ADA_COLS))
    mod_all = _ada_fwd(c_all, ada_w[0], ada_b_mine)
    mod_mine = _mod_exchange(mod_all.reshape(N_DEV, 12, 128)).reshape(6, D_MODEL)
    mod = jnp.concatenate([mod_mine, jnp.zeros((2, D_MODEL), f32)], axis=0)

    w_in_t, m_w_in_t, v_w_in_t = w_in[0].T, m_w_in[0].T, v_w_in[0].T

    wall_in = _gather_weights(jnp.pad(w_in_t, ((0, D_MODEL - w_shard_cols), (0, 0))).astype(bf16))
    w_in_f = _w_in_layout(wall_in[:, :w_shard_cols].reshape(PROJ_WIDTH, D_MODEL))
    b_ff1w, b_ff2w, b_outw = w_ff1[0].astype(bf16), w_ff2[0].astype(bf16), w_out[0].astype(bf16)

    def with_mine(wall, mine):
        return lax.dynamic_update_slice(wall, mine[None], (chip, 0, 0))

    e_mat = _head_expand()
    dsk_x = jnp.repeat(d_skip[0], HEAD_DIM).reshape(1, SSD_WIDTH)
    dtb, alog = _pad128(dt_bias), _pad128(a_log)
    sinks = attn_sinks[0]
    lng, lnb = _row(ln_in_g), _row(ln_in_b)
    u1, z, xr, xc, q, kv, dtr, wall_ff1 = _inproj_fwd(x2, mod, lng, lnb, w_in_f, conv_w_full, conv_b, b_ff1w)
    y, prev_all, wall_ff2 = _ssd_fwd(xc, dtr, dtb, alog, dsk_x, e_mat, b_ff2w)
    o, lse, wall_out = _attn_fwd(q, kv, sinks, b_outw)
    wall_ff1, wall_ff2, wall_out = with_mine(wall_ff1, b_ff1w), with_mine(wall_ff2, b_ff2w), with_mine(wall_out, b_outw)
    yn, mix, r1 = _outproj_fwd(y, z, o, x2, mod, lng, lnb, ssd_norm_w, wall_out)

    dr1, u2, s_act, da, df, acc_mlp, db1 = _mlp_fwd_bwd(r1, tgt, mod, ln1_g, ln1_b, ln2_g, ln2_b, wall_ff1, b_ff1, wall_ff2,
                                                        b_ff2)
    ga = jnp.zeros((N_CHIPS, GA_ROWS, D_MODEL), bf16)
    ga = _wgrad_blob(ga, u2, da, "wgrad_ff1", lambda t, n: (n, t))
    ga = _wgrad_blob(ga, s_act, df, "wgrad_ff2", lambda t, n: (t // 2, 2 + t % 2))
    ga = ga.reshape(N_CHIPS, 2, GA_ROWS // 2, D_MODEL)
    dy, dz, do, dmix, acc_out, a_sib = _outproj_bwd(dr1, mix, y, z, mod, ssd_norm_w, wall_out, ga)
    gb = jnp.zeros((N_CHIPS, GB_ROWS, D_MODEL), bf16)
    gb = _wgrad_blob(gb, yn, dmix, "wgrad_out_y", lambda t, n: (t, 2))
    gb = _wgrad_blob(gb, o, dmix, "wgrad_out_o", lambda t, n: (2 + t, 2))
    a_pair = _rs_sum_pair(place, ga, a_sib, "rs_sum_pair_a")
    dq, dkv, dsink, a_chips = _attn_bwd(q, kv, do, lse, sinks, a_pair)
    g_a = _rs_sum_chips(place, ga, a_sib, a_chips, "rs_sum_chips_a")
    dxc, ddt, acc_ssd, dd_x, g_a = _ssd_bwd(xc, dtr, dy, prev_all, dtb, alog, dsk_x, e_mat, g_a)
    dxr, acc_conv = _conv_bwd(dxc, xr, conv_w_full, conv_b)
    g_in_t = jnp.concatenate([_wgrad(dz, u1, "wgrad_in_z"), _wgrad(dxr, u1, "wgrad_in_xbc"),
                              _wgrad(ddt.astype(bf16), u1, "wgrad_in_dt")[:16], _wgrad(dq, u1, "wgrad_in_q"),
                              _wgrad(dkv, u1, "wgrad_in_kv")], axis=0)
    g_in_c = jnp.pad(g_in_t.reshape(N_CHIPS, w_shard_cols, D_MODEL), ((0, 0), (0, D_MODEL - w_shard_cols), (0, 0)))
    gb = lax.dynamic_update_slice(gb, g_in_c, (0, 0, 0)).reshape(N_CHIPS, 2, GB_ROWS // 2, D_MODEL)
    b_sib = _rs_to_sibling(gb)
    b_pair = _rs_sum_pair(place, gb, b_sib, "rs_sum_pair_b")
    grad_x, acc_in, b_chips = _inproj_bwd(dz, dxr, dq, dkv, ddt, dr1, x2, mod, lng, lnb, w_in_f, b_pair)
    g_b = _rs_share(_rs_sum_chips(place, gb, b_sib, b_chips, "rs_sum_chips_b"))

    dmod = jnp.concatenate([acc_in[1], acc_in[0], acc_out[0], acc_mlp[A_SH2], acc_mlp[A_SC2], acc_mlp[A_G2]])
    small = {
        "ada_b": dmod, "ln_in_g": acc_in[2], "ln_in_b": acc_in[3], "conv_b": acc_conv[4], "dt_bias": acc_ssd[1, :16],
        "a_log": acc_ssd[0, :16] * (-jnp.exp(a_log[0])), "d_skip": jnp.sum(dd_x.reshape(SSD_HEADS, HEAD_DIM), axis=1),
        "ssd_norm_w": acc_out[1], "attn_sinks": dsink[0, :16], "ln1_g": acc_mlp[A_LN1G], "ln1_b": acc_mlp[A_LN1B],
        "b_ff1": db1[0], "b_ff2": acc_mlp[A_B2], "ln2_g": acc_mlp[A_LN2G], "ln2_b": acc_mlp[A_LN2B],
        "conv_w": acc_conv[0:CONV_K], "loss": acc_mlp[A_LOSS],
    }
    small_all = _allgather8(_pack_small(small).reshape(8, SMALL_N // 8), "gather_small").reshape(N_DEV, SMALL_N)
    params = dict(ada_b=ada_b, ln_in_g=ln_in_g, ln_in_b=ln_in_b, conv_b=conv_b, dt_bias=dt_bias, a_log=a_log, d_skip=d_skip,
                  ssd_norm_w=ssd_norm_w, attn_sinks=attn_sinks, ln1_g=ln1_g, ln1_b=ln1_b, b_ff1=b_ff1, b_ff2=b_ff2,
                  ln2_g=ln2_g, ln2_b=ln2_b)
    moms = dict(ada_b=m_ada_b, ln_in_g=m_ln_in_g, ln_in_b=m_ln_in_b, conv_b=m_conv_b, dt_bias=m_dt_bias, a_log=m_a_log,
                d_skip=m_d_skip, ssd_norm_w=m_ssd_norm_w, attn_sinks=m_attn_sinks, ln1_g=m_ln1_g, ln1_b=m_ln1_b,
                b_ff1=m_b_ff1, b_ff2=m_b_ff2, ln2_g=m_ln2_g, ln2_b=m_ln2_b)
    vels = dict(ada_b=v_ada_b, ln_in_g=v_ln_in_g, ln_in_b=v_ln_in_b, conv_b=v_conv_b, dt_bias=v_dt_bias, a_log=v_a_log,
                d_skip=v_d_skip, ssd_norm_w=v_ssd_norm_w, attn_sinks=v_attn_sinks, ln1_g=v_ln1_g, ln1_b=v_ln1_b,
                b_ff1=v_b_ff1, b_ff2=v_b_ff2, ln2_g=v_ln2_g, ln2_b=v_ln2_b)
    gs, dl, m2, v2 = _small_update(small_all, _pack_small(params).reshape(1, -1), _pack_small(moms).reshape(1, -1),
                                   _pack_small(vels).reshape(1, -1))
    gs, dl, m2, v2 = (_unpack_small(t[0]) for t in (gs, dl, m2, v2))
    loss = jnp.sum(gs["loss"])

    dmod_mine = lax.dynamic_slice(small_all[:, :6 * D_MODEL], (0, chip * ADA_COLS), (N_DEV, ADA_COLS))
    g_ada, d_ada, m_ada, v_ada = _ada_bwd(c_all, dmod_mine, ada_w[0], m_ada_w[0], v_ada_w[0])

    g_conv = lax.dynamic_slice(gs["conv_w"].reshape(CONV_K, CONV_DIM), (0, chip * 384), (CONV_K, 384))
    d_conv, m_conv, v_conv = _adamw(conv_w[0], g_conv, m_conv_w[0], v_conv_w[0], "adamw_conv_w")

    g_w_in_t, g_w_out = g_b[:w_shard_cols], g_b[D_MODEL:GB_ROWS]
    g_w_ff1, g_w_ff2 = g_a[:D_MODEL], g_a[D_MODEL:GA_ROWS]
    d_w_in_t, m_w_in2_t, v_w_in2_t = _adamw(w_in_t, g_w_in_t, m_w_in_t, v_w_in_t, "adamw_w_in")
    g_w_in, d_w_in, m_w_in2, v_w_in2 = g_w_in_t.T, d_w_in_t.T, m_w_in2_t.T, v_w_in2_t.T
    d_w_out, m_w_out2, v_w_out2 = _adamw(w_out[0], g_w_out, m_w_out[0], v_w_out[0], "adamw_w_out")
    d_w_ff1, m_w_ff12, v_w_ff12 = _adamw(w_ff1[0], g_w_ff1, m_w_ff1[0], v_w_ff1[0], "adamw_w_ff1")
    d_w_ff2, m_w_ff22, v_w_ff22 = _adamw(w_ff2[0], g_w_ff2, m_w_ff2[0], v_w_ff2[0], "adamw_w_ff2")

    def small_out(t, name, like):
        return t[name][:like.size].reshape(like.shape)

    def outputs(t, big):
        return [small_out(t, "ln_in_g", ln_in_g), small_out(t, "ln_in_b", ln_in_b), big["ada_w"][None],
                small_out(t, "ada_b", ada_b), big["w_in"][None], big["conv_w"][None], small_out(t, "conv_b", conv_b),
                small_out(t, "dt_bias", dt_bias), small_out(t, "a_log", a_log), small_out(t, "d_skip", d_skip),
                small_out(t, "ssd_norm_w", ssd_norm_w), small_out(t, "attn_sinks", attn_sinks), big["w_out"][None],
                small_out(t, "ln1_g", ln1_g), small_out(t, "ln1_b", ln1_b), big["w_ff1"][None], small_out(t, "b_ff1", b_ff1),
                big["w_ff2"][None], small_out(t, "b_ff2", b_ff2), small_out(t, "ln2_g", ln2_g), small_out(t, "ln2_b", ln2_b)]

    grads = outputs(gs, dict(ada_w=g_ada, w_in=g_w_in, conv_w=g_conv, w_out=g_w_out, w_ff1=g_w_ff1, w_ff2=g_w_ff2))
    deltas = outputs(dl, dict(ada_w=d_ada, w_in=d_w_in, conv_w=d_conv, w_out=d_w_out, w_ff1=d_w_ff1, w_ff2=d_w_ff2))
    new_m = outputs(m2, dict(ada_w=m_ada, w_in=m_w_in2, conv_w=m_conv, w_out=m_w_out2, w_ff1=m_w_ff12, w_ff2=m_w_ff22))
    new_v = outputs(v2, dict(ada_w=v_ada, w_in=v_w_in2, conv_w=v_conv, w_out=v_w_out2, w_ff1=v_w_ff12, w_ff2=v_w_ff22))
    return (loss, grad_x[None], *grads, *deltas, *new_m, *new_v)
```

```python
import functools
import math

import numpy as np
import jax
import jax.numpy as jnp
from jax import lax
from jax.experimental import pallas as pl
from jax.experimental.pallas import tpu as pltpu

f32 = jnp.float32
bf16 = jnp.bfloat16

D_MODEL = 1024
SSD_WIDTH = 1024
SSD_HEADS = 16
HEAD_DIM = 64
SSD_STATE = 128
SSD_GROUPS = 2
CHUNK = 128
CONV_K = 4
CONV_DIM = 1536
ATTN_HEADS = 16
D_FF = 4096
PROJ_WIDTH = 3856
ALPHA = 2.0 ** 0.25
LN_EPS = 1e-5
RMS_EPS = 1e-5
ATTN_SCALE = HEAD_DIM ** -0.5
NEG = -1e30

ADAM_LR = 0.001
ADAM_B1 = 0.9
ADAM_B2 = 0.999
ADAM_EPS = 1e-08
ADAM_WD = 0.01
ADAM_STEP = 10

C_Z, C_XBC, C_Q, C_KV, C_DT, C_END = 0, 1024, 2560, 3584, 3840, 3968
GA_ROWS = 2048
GB_ROWS = 1536
WG_TM = 512
STAGE_ROWS = 512
ADAMW_BLOCK_ELEMS = 1 << 18
DENSE_TM = 512
N_CHIPS = 4
N_DEV = 8
VMEM_LIMIT = 56 * 1024 * 1024
MESH = pl.DeviceIdType.MESH

ALIBI_SLOPES = tuple(2.0 ** (-8.0 / ATTN_HEADS * (i + 1)) for i in range(ATTN_HEADS))


def _cparams(*sem):
    return pltpu.CompilerParams(dimension_semantics=sem, vmem_limit_bytes=VMEM_LIMIT)


def _sigmoid(x):
    return 1.0 / (1.0 + jnp.exp(-x))


def _softplus(x):
    return jnp.maximum(x, 0.0) + jnp.log1p(jnp.exp(-jnp.abs(x)))


def _ln_stats(x):
    mu = jnp.mean(x, axis=-1, keepdims=True)
    xc = x - mu
    var = jnp.mean(xc * xc, axis=-1, keepdims=True)
    rstd = lax.rsqrt(var + LN_EPS)
    return xc * rstd, rstd


def _ln_bwd(dy, xhat, rstd, g):
    dxh = dy * g
    m1 = jnp.mean(dxh, axis=-1, keepdims=True)
    m2 = jnp.mean(dxh * xhat, axis=-1, keepdims=True)
    return rstd * (dxh - m1 - xhat * m2)


def _dot(a, b):
    return jnp.dot(a, b, preferred_element_type=f32)


def _dot_nt(a, b):
    return lax.dot_general(a, b, (((1,), (1,)), ((), ())), preferred_element_type=f32)


def _dot_tn(a, b):
    return lax.dot_general(a, b, (((0,), (0,)), ((), ())), preferred_element_type=f32)


def _dot_exact(a, b):
    return jnp.dot(a, b, preferred_element_type=f32, precision=lax.Precision.HIGHEST)


def _split3(v):
    hi = v.astype(bf16)
    r1 = v - hi.astype(f32)
    mid = r1.astype(bf16)
    lo = (r1 - mid.astype(f32)).astype(bf16)
    return hi, mid, lo


def _sel_dot(sel, v):
    hi, mid, lo = _split3(v)
    return _dot(sel, hi) + _dot(sel, mid) + _dot(sel, lo)


def _dot_sel(v, sel):
    hi, mid, lo = _split3(v)
    return _dot(hi, sel) + _dot(mid, sel) + _dot(lo, sel)


def _dot_sel_nt(v, sel):
    hi, mid, lo = _split3(v)
    return _dot_nt(hi, sel) + _dot_nt(mid, sel) + _dot_nt(lo, sel)


def _full(shape):
    nd = len(shape)
    return pl.BlockSpec(shape, lambda *_: (0,) * nd)


def _resident(shape):
    nd = len(shape)
    return pl.BlockSpec(shape, lambda *_: (0,) * nd, pipeline_mode=pl.Buffered(1))


def _rows(tm, n):
    return pl.BlockSpec((tm, n), lambda i: (i, 0))


def _inproj_fwd(x, mod, ln_g, ln_b, w_in_t, conv_w, conv_b, blob):
    L = x.shape[0]
    tm = DENSE_TM
    nt = L // tm
    R = blob.shape[0]

    def body(x_ref, mod_ref, g_ref, b_ref, w_ref, cw_ref, cb_ref, blob_ref,
             u1_ref, z_ref, xr_ref, xc_ref, q_ref, kv_ref, dt_ref, wall_ref, halo, buf, send_sems, recv_sems):
        start, finish = _gather_job(blob_ref, wall_ref, send_sems, recv_sems, R)

        @pl.when(pl.program_id(0) == 0)
        def _():
            halo[...] = jnp.zeros_like(halo)
            start()

        xhat, _ = _ln_stats(x_ref[...])
        h0 = xhat * g_ref[...] + b_ref[...]
        u1 = (h0 * (1.0 + mod_ref[1:2, :]) + mod_ref[0:1, :]).astype(bf16)
        u1_ref[...] = u1
        z_ref[...] = _dot_nt(u1, w_ref[C_Z:C_XBC, :])
        xr = _dot_nt(u1, w_ref[C_XBC:C_Q, :])
        xr_ref[...] = xr
        q_ref[...] = _dot_nt(u1, w_ref[C_Q:C_KV, :]).astype(bf16)
        kv_ref[...] = _dot_nt(u1, w_ref[C_KV:C_DT, :]).astype(bf16)
        dt_ref[...] = _dot_nt(u1, w_ref[C_DT:C_END, :])
        buf[0:8, :] = halo[...]
        buf[8:8 + tm, :] = xr
        pre = cb_ref[...] + cw_ref[0:1, :] * buf[5:5 + tm, :]
        for k in range(1, CONV_K):
            pre = pre + cw_ref[k:k + 1, :] * buf[5 + k:5 + k + tm, :]
        xc_ref[...] = pre * _sigmoid(pre)
        halo[...] = xr[tm - 8:tm, :]

        @pl.when(pl.program_id(0) == nt - 1)
        def _():
            finish()

    return pl.pallas_call(
        body, name="inproj_fwd", grid=(nt,),
        in_specs=[_rows(tm, D_MODEL), _full((8, D_MODEL)), _full((1, D_MODEL)), _full((1, D_MODEL)),
                  _resident((C_END, D_MODEL)), _full((CONV_K, CONV_DIM)), _full((1, CONV_DIM)), _ANY_SPEC],
        out_specs=[_rows(tm, D_MODEL), _rows(tm, D_MODEL), _rows(tm, CONV_DIM), _rows(tm, CONV_DIM),
                   _rows(tm, D_MODEL), _rows(tm, 256), _rows(tm, 128), _ANY_SPEC],
        out_shape=[jax.ShapeDtypeStruct((L, D_MODEL), bf16), jax.ShapeDtypeStruct((L, D_MODEL), f32),
                   jax.ShapeDtypeStruct((L, CONV_DIM), f32), jax.ShapeDtypeStruct((L, CONV_DIM), f32),
                   jax.ShapeDtypeStruct((L, D_MODEL), bf16), jax.ShapeDtypeStruct((L, 256), bf16),
                   jax.ShapeDtypeStruct((L, 128), f32), jax.ShapeDtypeStruct((N_CHIPS, R, D_MODEL), bf16)],
        scratch_shapes=[pltpu.VMEM((8, CONV_DIM), f32), pltpu.VMEM((tm + 8, CONV_DIM), f32),
                        pltpu.SemaphoreType.DMA((6,)), pltpu.SemaphoreType.DMA((6,))],
        compiler_params=_cparams("arbitrary"),
    )(x, mod, ln_g, ln_b, w_in_t, conv_w, conv_b, blob)


def _head_expand():
    e = np.zeros((128, SSD_WIDTH), np.float32)
    for h in range(SSD_HEADS):
        e[h, h * HEAD_DIM:(h + 1) * HEAD_DIM] = 1.0
    return jnp.asarray(e, dtype=bf16)


def _ssd_chunk_common(dt_raw, dtb, a_row, e_mat):
    T = CHUNK
    lane = lax.broadcasted_iota(jnp.int32, (T, 128), 1)
    dt = jnp.where(lane < SSD_HEADS, _softplus(dt_raw + dtb), 0.0)
    a = dt * a_row
    r = lax.broadcasted_iota(jnp.int32, (T, T), 0)
    c = lax.broadcasted_iota(jnp.int32, (T, T), 1)
    tril = (c <= r).astype(bf16)
    cum = _sel_dot(tril, a)
    dtx = _dot_sel(dt, e_mat)
    cumx = _dot_sel(cum, e_mat)
    return dt, a, cum, dtx, cumx, r, c


def _ssd_fwd(xc, dt_raw, dt_bias, a_log, d_skip_x, e_mat, blob):
    L = xc.shape[0]
    nc = L // CHUNK
    T = CHUNK
    R = blob.shape[0]

    def body(xc_ref, dt_ref, dtb_ref, al_ref, dsk_ref, e_ref, blob_ref, y_ref, prev_ref, wall_ref, st, send_sems, recv_sems):
        start, finish = _gather_job(blob_ref, wall_ref, send_sems, recv_sems, R)

        @pl.when(pl.program_id(0) == 0)
        def _():
            st[...] = jnp.zeros_like(st)
            start()

        @pl.when(pl.program_id(0) == nc - 1)
        def _():
            finish()

        a_row = -jnp.exp(al_ref[...])
        lane1 = lax.broadcasted_iota(jnp.int32, (1, 128), 1)
        a_row = jnp.where(lane1 < SSD_HEADS, a_row, 0.0)
        dt, a, cum, dtx, cumx, r, c = _ssd_chunk_common(dt_ref[...], dtb_ref[...], a_row, e_ref[...])
        cum_t = cum.T
        ex = jnp.exp(cumx)
        last = cumx[T - 1:T, :]
        wx = jnp.exp(last - cumx)
        cdx = jnp.exp(last)
        xs = xc_ref[:, 0:SSD_WIDTH]
        X = xs * dtx
        Xb = X.astype(bf16)
        Xd = (X * wx).astype(bf16)
        prev = st[...]
        prev_ref[0] = prev
        prevb = prev.astype(bf16)
        tri = c <= r
        lane = lax.broadcasted_iota(jnp.int32, (T, 128), 1)
        y_blocks = []
        new_states = []
        for g in range(SSD_GROUPS):
            Bg = xc_ref[:, 1024 + 128 * g:1152 + 128 * g].astype(bf16)
            Cg = xc_ref[:, 1280 + 128 * g:1408 + 128 * g].astype(bf16)
            G = _dot_nt(Cg, Bg)
            yoff = _dot(Cg, prevb[:, 512 * g:512 * (g + 1)])
            new_states.append(_dot_tn(Bg, Xd[:, 512 * g:512 * (g + 1)]))
            for j in range(4):
                blk = 4 * g + j
                Xblk = Xb[:, 128 * blk:128 * (blk + 1)]
                ys = []
                for half in range(2):
                    h = 2 * blk + half
                    seg = jnp.minimum(cum[:, h:h + 1] - cum_t[h:h + 1, :], 0.0)
                    M = jnp.where(tri, G * jnp.exp(seg), 0.0).astype(bf16)
                    ys.append(_dot(M, Xblk))
                yd = jnp.where(lane < HEAD_DIM, ys[0], ys[1])
                sl = slice(128 * blk, 128 * (blk + 1))
                y_blocks.append(yd + ex[:, sl] * yoff[:, 128 * j:128 * (j + 1)] + dsk_ref[:, sl] * xs[:, sl])
        y_ref[...] = jnp.concatenate(y_blocks, axis=1)
        st[...] = prev * cdx + jnp.concatenate(new_states, axis=1)

    return pl.pallas_call(
        body, name="ssd_fwd", grid=(nc,),
        in_specs=[_rows(T, CONV_DIM), _rows(T, 128), _full((1, 128)), _full((1, 128)), _full((1, SSD_WIDTH)),
                  _full((128, SSD_WIDTH)), _ANY_SPEC],
        out_specs=[_rows(T, SSD_WIDTH), pl.BlockSpec((1, SSD_STATE, SSD_WIDTH), lambda i: (i, 0, 0)), _ANY_SPEC],
        out_shape=[jax.ShapeDtypeStruct((L, SSD_WIDTH), f32), jax.ShapeDtypeStruct((nc, SSD_STATE, SSD_WIDTH), f32),
                   jax.ShapeDtypeStruct((N_CHIPS, R, D_MODEL), bf16)],
        scratch_shapes=[pltpu.VMEM((SSD_STATE, SSD_WIDTH), f32)] + _sems(6),
        compiler_params=_cparams("arbitrary"),
    )(xc, dt_raw, dt_bias, a_log, d_skip_x, e_mat, blob)


def _kv_halves(kv_prev, kv_cur, first):
    kv = jnp.concatenate([jnp.where(first, 0.0, kv_prev.astype(f32)), kv_cur.astype(f32)], axis=0)
    lane = lax.broadcasted_iota(jnp.int32, (2 * CHUNK, 128), 1)
    lo = lane < HEAD_DIM
    out = []
    for g in range(2):
        per_half = []
        for half in range(2):
            both = []
            for t in (kv[:, 0:128], kv[:, 128:256]):
                src = t if g == half else pltpu.roll(t, HEAD_DIM, 1)
                both.append(jnp.where(lo if half == 0 else ~lo, src, 0.0).astype(bf16))
            per_half.append(tuple(both))
        out.append(per_half)
    return out


def _attn_masks(first):
    r = lax.broadcasted_iota(jnp.int32, (CHUNK, 2 * CHUNK), 0)
    c = lax.broadcasted_iota(jnp.int32, (CHUNK, 2 * CHUNK), 1)
    dist = r + CHUNK - c
    valid = (dist >= 0) & (dist < CHUNK) & ((c >= CHUNK) | jnp.logical_not(first))
    return dist.astype(f32), valid


def _attn_fwd(q, kv, sinks, blob):
    L = q.shape[0]
    nb = L // CHUNK
    T = CHUNK
    R = blob.shape[0]

    def body(sink_ref, q_ref, kvp_ref, kvc_ref, blob_ref, o_ref, lse_ref, wall_ref, send_sems, recv_sems):
        first = pl.program_id(0) == 0
        start, finish = _gather_job(blob_ref, wall_ref, send_sems, recv_sems, R)

        @pl.when(first)
        def _():
            start()

        @pl.when(pl.program_id(0) == nb - 1)
        def _():
            finish()

        ext = _kv_halves(kvp_ref[...], kvc_ref[...], first)
        dist, valid = _attn_masks(first)
        lane = lax.broadcasted_iota(jnp.int32, (T, 128), 1)
        lse = jnp.zeros((T, 128), f32)
        o_blocks = []
        for blk in range(8):
            qb = q_ref[:, 128 * blk:128 * (blk + 1)]
            acc = None
            for half in range(2):
                h = 2 * blk + half
                k_ext, v_ext = ext[h // 8][half]
                s = _dot_nt(qb, k_ext) * ATTN_SCALE - ALIBI_SLOPES[h] * dist
                s = jnp.where(valid, s, NEG)
                sink = sink_ref[h]
                m = jnp.maximum(jnp.max(s, axis=-1, keepdims=True), sink)
                p = jnp.exp(s - m)
                den = jnp.sum(p, axis=-1, keepdims=True) + jnp.exp(sink - m)
                pn = (p * (1.0 / den)).astype(bf16)
                oh = _dot(pn, v_ext)
                acc = oh if acc is None else acc + oh
                lse = jnp.where(lane == h, m + jnp.log(den), lse)
            o_blocks.append(acc.astype(bf16))
        o_ref[...] = jnp.concatenate(o_blocks, axis=1)
        lse_ref[...] = lse

    return pl.pallas_call(
        body, name="attn_fwd", grid=(nb,),
        in_specs=[pl.BlockSpec(memory_space=pltpu.SMEM), _rows(T, D_MODEL),
                  pl.BlockSpec((T, 256), lambda i: (jnp.maximum(i - 1, 0), 0)), _rows(T, 256), _ANY_SPEC],
        out_specs=[_rows(T, D_MODEL), _rows(T, 128), _ANY_SPEC],
        out_shape=[jax.ShapeDtypeStruct((L, D_MODEL), bf16), jax.ShapeDtypeStruct((L, 128), f32),
                   jax.ShapeDtypeStruct((N_CHIPS, R, D_MODEL), bf16)],
        scratch_shapes=_sems(6),
        compiler_params=_cparams("arbitrary"),
    )(sinks, q, kv, kv, blob)


def _gated_norm(y, z, w):
    sz = _sigmoid(z)
    hg = y * (z * sz)
    ns, rss = [], []
    for g in range(SSD_GROUPS):
        hs = hg[:, 512 * g:512 * (g + 1)]
        rs = lax.rsqrt(jnp.mean(hs * hs, axis=-1, keepdims=True) + RMS_EPS)
        ns.append(hs * rs)
        rss.append(rs)
    n = jnp.concatenate(ns, axis=1)
    return n * w, n, rss, sz


def _outproj_fwd(y, z, o, x, mod, ln_g, ln_b, norm_w, w_out):
    L = x.shape[0]
    tm = DENSE_TM

    def body(y_ref, z_ref, o_ref, x_ref, mod_ref, g_ref, b_ref, nw_ref, w_ref, yn_ref, mix_ref, r1_ref):
        yn, _, _, _ = _gated_norm(y_ref[...], z_ref[...], nw_ref[...])
        ynb = yn.astype(bf16)
        yn_ref[...] = ynb
        mix = (_dot(ynb[:, 0:512], w_ref[0]) + _dot(ynb[:, 512:1024], w_ref[1])
               + _dot(o_ref[:, 0:512], w_ref[2]) + _dot(o_ref[:, 512:1024], w_ref[3]))
        mix_ref[...] = mix
        xhat, _ = _ln_stats(x_ref[...])
        h0 = xhat * g_ref[...] + b_ref[...]
        r1_ref[...] = ALPHA * h0 + (1.0 + mod_ref[2:3, :]) * mix

    v = _full((1, D_MODEL))
    return pl.pallas_call(
        body, name="outproj_fwd", grid=(L // tm,),
        in_specs=[_rows(tm, D_MODEL), _rows(tm, D_MODEL), _rows(tm, D_MODEL), _rows(tm, D_MODEL),
                  _full((8, D_MODEL)), v, v, v, _resident((N_CHIPS, 512, D_MODEL))],
        out_specs=[_rows(tm, D_MODEL)] * 3,
        out_shape=[jax.ShapeDtypeStruct((L, D_MODEL), bf16), jax.ShapeDtypeStruct((L, D_MODEL), f32),
                   jax.ShapeDtypeStruct((L, D_MODEL), f32)],
        compiler_params=_cparams("parallel"),
    )(y, z, o, x, mod, ln_g, ln_b, norm_w, w_out)


A_LN2G, A_LN2B, A_G2, A_B2, A_SC2, A_SH2, A_LN1G, A_LN1B, A_LOSS = range(9)


def _mlp_fwd_bwd(r1, target, mod, ln1_g, ln1_b, ln2_g, ln2_b, w1, b1, w2, b2):
    L = r1.shape[0]
    tm = 256
    nj = D_FF // 1024

    def body(r1_ref, t_ref, mod_ref, g1_ref, bb1_ref, g2_ref, bb2_ref, w1_ref, b1_ref, w2_ref, b2_ref,
             dr1_ref, u2_ref, s_ref, da_ref, df_ref, acc_ref, db1_ref, hr):
        @pl.when(pl.program_id(0) == 0)
        def _():
            acc_ref[...] = jnp.zeros_like(acc_ref)
            db1_ref[...] = jnp.zeros_like(db1_ref)

        sc2, sh2, gate2 = mod_ref[4:5, :], mod_ref[3:4, :], mod_ref[5:6, :]
        xhat1, rstd1 = _ln_stats(r1_ref[...])
        h1 = xhat1 * g1_ref[...] + bb1_ref[...]
        u2f = h1 * (1.0 + sc2) + sh2
        u2 = u2f.astype(bf16)
        u2_ref[...] = u2
        f = jnp.zeros((tm, D_MODEL), f32) + b2_ref[...]
        for j in range(nj):
            cs = slice(1024 * j, 1024 * (j + 1))
            a = _dot(u2, w1_ref[j]) + b1_ref[:, cs]
            hrj = jnp.maximum(a, 0.0)
            hr[:, cs] = hrj
            sj = (hrj * hrj).astype(bf16)
            s_ref[:, cs] = sj
            f = f + _dot(sj, w2_ref[j])
        r2 = ALPHA * h1 + (1.0 + gate2) * f
        xhat2, rstd2 = _ln_stats(r2)
        h2 = xhat2 * g2_ref[...] + bb2_ref[...]
        diff = h2 - t_ref[...]
        dh2 = diff * (1.0 / D_MODEL)

        def add(row, val):
            acc_ref[row:row + 1, :] += jnp.sum(val, axis=0, keepdims=True)

        add(A_LOSS, diff * diff * (0.5 / D_MODEL))
        add(A_LN2G, dh2 * xhat2)
        add(A_LN2B, dh2)
        dr2 = _ln_bwd(dh2, xhat2, rstd2, g2_ref[...])
        add(A_G2, dr2 * f)
        df = dr2 * (1.0 + gate2)
        add(A_B2, df)
        dfb = df.astype(bf16)
        df_ref[...] = dfb
        du2 = jnp.zeros((tm, D_MODEL), f32)
        for j in range(nj):
            cs = slice(1024 * j, 1024 * (j + 1))
            ds = _dot_nt(dfb, w2_ref[j])
            daj = ds * (2.0 * hr[:, cs])
            db1_ref[:, cs] += jnp.sum(daj, axis=0, keepdims=True)
            dajb = daj.astype(bf16)
            da_ref[:, cs] = dajb
            du2 = du2 + _dot_nt(dajb, w1_ref[j])
        add(A_SC2, du2 * h1)
        add(A_SH2, du2)
        dh1 = ALPHA * dr2 + du2 * (1.0 + sc2)
        add(A_LN1G, dh1 * xhat1)
        add(A_LN1B, dh1)
        dr1_ref[...] = _ln_bwd(dh1, xhat1, rstd1, g1_ref[...])

    v = _full((1, D_MODEL))
    return pl.pallas_call(
        body, name="mlp_fwd_bwd", grid=(L // tm,),
        in_specs=[_rows(tm, D_MODEL), _rows(tm, D_MODEL), _full((8, D_MODEL)), v, v, v, v,
                  _resident((N_CHIPS, D_MODEL, D_MODEL)), _full((1, D_FF)), _resident((N_CHIPS, D_MODEL, D_MODEL)), v],
        out_specs=[_rows(tm, D_MODEL), _rows(tm, D_MODEL), _rows(tm, D_FF), _rows(tm, D_FF), _rows(tm, D_MODEL),
                   _full((16, D_MODEL)), _full((1, D_FF))],
        out_shape=[jax.ShapeDtypeStruct((L, D_MODEL), f32), jax.ShapeDtypeStruct((L, D_MODEL), bf16),
                   jax.ShapeDtypeStruct((L, D_FF), bf16), jax.ShapeDtypeStruct((L, D_FF), bf16),
                   jax.ShapeDtypeStruct((L, D_MODEL), bf16), jax.ShapeDtypeStruct((16, D_MODEL), f32),
                   jax.ShapeDtypeStruct((1, D_FF), f32)],
        scratch_shapes=[pltpu.VMEM((tm, D_FF), f32)],
        compiler_params=_cparams("arbitrary"),
    )(r1, target, mod, ln1_g, ln1_b, ln2_g, ln2_b, w1, b1, w2, b2)


def _wgrad(a, b, name):
    L, M = a.shape
    N = b.shape[1]
    tm = min(M, 512)
    tn = next(t for t in (1024, 768, 512, 256, 128) if N % t == 0)

    def body(a_ref, b_ref, o_ref):
        o_ref[...] = _dot_tn(a_ref[...], b_ref[...]).astype(bf16)

    return pl.pallas_call(
        body, name=name, grid=(M // tm, N // tn),
        in_specs=[pl.BlockSpec((L, tm), lambda i, j: (0, i)), pl.BlockSpec((L, tn), lambda i, j: (0, j))],
        out_specs=pl.BlockSpec((tm, tn), lambda i, j: (i, j)),
        out_shape=jax.ShapeDtypeStruct((M, N), bf16),
        compiler_params=_cparams("parallel", "parallel"),
    )(a, b)


def _wgrad_blob(blob, a, b, name, place_of):
    L, M = a.shape
    N = b.shape[1]

    def body(blob_ref, a_ref, b_ref, o_ref):
        o_ref[0] = _dot_tn(a_ref[...], b_ref[...]).astype(bf16)

    return pl.pallas_call(
        body, name=name, grid=(M // WG_TM, N // D_MODEL),
        in_specs=[pl.BlockSpec(memory_space=pl.ANY), pl.BlockSpec((L, WG_TM), lambda t, n: (0, t)),
                  pl.BlockSpec((L, D_MODEL), lambda t, n: (0, n))],
        out_specs=pl.BlockSpec((1, WG_TM, D_MODEL), lambda t, n: (*place_of(t, n), 0)),
        out_shape=jax.ShapeDtypeStruct(blob.shape, bf16), input_output_aliases={0: 0},
        compiler_params=_cparams("parallel", "parallel"),
    )(blob, a, b)


def _outproj_bwd(dr1, mix, y, z, mod, norm_w, w_out, gb):
    L = dr1.shape[0]
    tm = DENSE_TM
    nt = L // tm

    def body(dr1_ref, mix_ref, y_ref, z_ref, mod_ref, nw_ref, w_ref, gb_ref,
             dy_ref, dz_ref, do_ref, dmix_ref, acc_ref, sib_ref, send_sems, recv_sems):
        start, wait = _to_sibling_job(gb_ref, sib_ref, send_sems, recv_sems)

        @pl.when(pl.program_id(0) == 0)
        def _():
            acc_ref[...] = jnp.zeros_like(acc_ref)
            start()

        @pl.when(pl.program_id(0) == nt - 1)
        def _():
            wait()

        dr1 = dr1_ref[...]
        acc_ref[0:1, :] += jnp.sum(dr1 * mix_ref[...], axis=0, keepdims=True)
        dmix = (dr1 * (1.0 + mod_ref[2:3, :])).astype(bf16)
        dmix_ref[...] = dmix
        dyn = jnp.concatenate([_dot_nt(dmix, w_ref[0]), _dot_nt(dmix, w_ref[1])], axis=1)
        do_ref[...] = jnp.concatenate([_dot_nt(dmix, w_ref[2]), _dot_nt(dmix, w_ref[3])], axis=1).astype(bf16)
        yv, zv = y_ref[...], z_ref[...]
        _, n, rss, sz = _gated_norm(yv, zv, nw_ref[...])
        acc_ref[1:2, :] += jnp.sum(dyn * n, axis=0, keepdims=True)
        dn = dyn * nw_ref[...]
        parts = []
        for g in range(SSD_GROUPS):
            sl = slice(512 * g, 512 * (g + 1))
            dng, ng = dn[:, sl], n[:, sl]
            parts.append(rss[g] * (dng - ng * jnp.mean(dng * ng, axis=-1, keepdims=True)))
        dhg = jnp.concatenate(parts, axis=1)
        dy_ref[...] = dhg * (zv * sz)
        dz_ref[...] = (dhg * yv * (sz * (1.0 + zv * (1.0 - sz)))).astype(bf16)

    return pl.pallas_call(
        body, name="outproj_bwd", grid=(nt,),
        in_specs=[_rows(tm, D_MODEL)] * 4 + [_full((8, D_MODEL)), _full((1, D_MODEL)), _resident((N_CHIPS, 512, D_MODEL)),
                  _ANY_SPEC],
        out_specs=[_rows(tm, D_MODEL)] * 4 + [_full((8, D_MODEL)), _ANY_SPEC],
        out_shape=[jax.ShapeDtypeStruct((L, D_MODEL), f32)] + [jax.ShapeDtypeStruct((L, D_MODEL), bf16)] * 3
        + [jax.ShapeDtypeStruct((8, D_MODEL), f32), jax.ShapeDtypeStruct((N_CHIPS,) + gb.shape[2:], bf16)],
        scratch_shapes=_sems(N_CHIPS),
        compiler_params=_cparams("arbitrary"),
    )(dr1, mix, y, z, mod, norm_w, w_out, gb)


def _attn_bwd(q, kv, do, lse, sinks, pb):
    L = q.shape[0]
    nb = L // CHUNK
    T = CHUNK

    def body(sink_ref, q_ref, kvp_ref, kvc_ref, do_ref, lse_ref, pb_ref, dq_ref, dkv_ref, dsink_ref, chips_ref,
             carry, send_sems, recv_sems):
        n = pl.program_id(0)
        start, wait = _to_chips_job(pb_ref, chips_ref, send_sems, recv_sems)

        @pl.when(n == 0)
        def _():
            carry[...] = jnp.zeros_like(carry)
            dsink_ref[...] = jnp.zeros_like(dsink_ref)
            start()

        @pl.when(n < nb)
        def _():
            first = n == 0
            ext = _kv_halves(kvp_ref[...], kvc_ref[...], first)
            dist, valid = _attn_masks(first)
            lane1 = lax.broadcasted_iota(jnp.int32, (1, 128), 1)
            lse = lse_ref[...]
            acck = [None, None]
            accv = [None, None]
            dsink = jnp.zeros((1, 128), f32)
            dq_blocks = []
            for blk in range(8):
                qb = q_ref[:, 128 * blk:128 * (blk + 1)]
                dob = do_ref[:, 128 * blk:128 * (blk + 1)]
                qt = qb.astype(f32).T.astype(bf16)
                dot_ = dob.astype(f32).T.astype(bf16)
                dq_acc = None
                for half in range(2):
                    h = 2 * blk + half
                    g = h // 8
                    k_ext, v_ext = ext[g][half]
                    s = _dot_nt(qb, k_ext) * ATTN_SCALE - ALIBI_SLOPES[h] * dist
                    lse_h = lse[:, h:h + 1]
                    p = jnp.where(valid, jnp.exp(s - lse_h), 0.0)
                    dp = _dot_nt(dob, v_ext)
                    delta = jnp.sum(p * dp, axis=-1, keepdims=True)
                    ds = (p * (dp - delta) * ATTN_SCALE).astype(bf16)
                    psink = jnp.exp(sink_ref[h] - lse_h)
                    dsink = dsink - jnp.where(lane1 == h, jnp.sum(psink * delta, axis=0, keepdims=True), 0.0)
                    dqh = _dot(ds, k_ext)
                    dq_acc = dqh if dq_acc is None else dq_acc + dqh
                    rows = slice(HEAD_DIM * half, HEAD_DIM * (half + 1))
                    dkh = _dot(qt[rows, :], ds)
                    dvh = _dot(dot_[rows, :], p.astype(bf16))
                    acck[g] = dkh if acck[g] is None else acck[g] + dkh
                    accv[g] = dvh if accv[g] is None else accv[g] + dvh
                dq_blocks.append(dq_acc.astype(bf16))
            dq_ref[...] = jnp.concatenate(dq_blocks, axis=1)
            dsink_ref[...] += dsink
            dkv = jnp.concatenate([jnp.concatenate(acck, axis=0).T, jnp.concatenate(accv, axis=0).T], axis=1)
            dkv_ref[...] = (carry[...] + dkv[0:T, :]).astype(bf16)
            carry[...] = dkv[T:2 * T, :]

        @pl.when(n == nb)
        def _():
            dkv_ref[...] = carry[...].astype(bf16)
            wait()

    cur = lambda i: (jnp.minimum(i, nb - 1), 0)
    return pl.pallas_call(
        body, name="attn_bwd", grid=(nb + 1,),
        in_specs=[pl.BlockSpec(memory_space=pltpu.SMEM), pl.BlockSpec((T, D_MODEL), cur),
                  pl.BlockSpec((T, 256), lambda i: (jnp.maximum(jnp.minimum(i, nb - 1) - 1, 0), 0)),
                  pl.BlockSpec((T, 256), cur), pl.BlockSpec((T, D_MODEL), cur), pl.BlockSpec((T, 128), cur), _ANY_SPEC],
        out_specs=[pl.BlockSpec((T, D_MODEL), cur), pl.BlockSpec((T, 256), lambda i: (jnp.maximum(i - 1, 0), 0)),
                   _full((1, 128)), _ANY_SPEC],
        out_shape=[jax.ShapeDtypeStruct((L, D_MODEL), bf16), jax.ShapeDtypeStruct((L, 256), bf16),
                   jax.ShapeDtypeStruct((1, 128), f32), jax.ShapeDtypeStruct((N_CHIPS - 1,) + pb.shape[1:], bf16)],
        scratch_shapes=[pltpu.VMEM((T, 256), f32)] + _sems(N_CHIPS - 1),
        compiler_params=_cparams("arbitrary"),
    )(sinks, q, kv, kv, do, lse, pb)


def _ssd_bwd(xc, dt_raw, dy, prev_all, dt_bias, a_log, d_skip_x, e_mat, g):
    L = xc.shape[0]
    nc = L // CHUNK
    T = CHUNK
    RG = g.shape[0]

    def body(xc_ref, dt_ref, dy_ref, prev_ref, dtb_ref, al_ref, dsk_ref, e_ref, g_in_ref,
             dxc_ref, ddt_ref, acc_ref, dd_ref, g_ref, dst, dxs_s, send_sems, recv_sems):
        start, wait = _share_job(g_ref, send_sems, recv_sems, RG)

        @pl.when(pl.program_id(0) == 0)
        def _():
            dst[...] = jnp.zeros_like(dst)
            acc_ref[...] = jnp.zeros_like(acc_ref)
            dd_ref[...] = jnp.zeros_like(dd_ref)
            start()

        @pl.when(pl.program_id(0) == nc - 1)
        def _():
            wait()

        lane1 = lax.broadcasted_iota(jnp.int32, (1, 128), 1)
        a_row = jnp.where(lane1 < SSD_HEADS, -jnp.exp(al_ref[...]), 0.0)
        e_mat_v = e_ref[...]
        dt, a, cum, dtx, cumx, r, c = _ssd_chunk_common(dt_ref[...], dtb_ref[...], a_row, e_mat_v)
        cum_t = cum.T
        ex = jnp.exp(cumx)
        last = cumx[T - 1:T, :]
        wx = jnp.exp(last - cumx)
        cdx = jnp.exp(last)
        xs = xc_ref[:, 0:SSD_WIDTH]
        X = xs * dtx
        Xb = X.astype(bf16)
        Xdb = (X * wx).astype(bf16)
        dyv = dy_ref[...]
        prev = prev_ref[0]
        prevb = prev.astype(bf16)
        dnew = dst[...]
        dnewb = dnew.astype(bf16)
        tri = c <= r
        lane = lax.broadcasted_iota(jnp.int32, (T, 128), 1)
        sub = lax.broadcasted_iota(jnp.int32, (128, T), 0)
        lo = lane < HEAD_DIM

        def red(vals, g):
            return _dot_sel_nt(vals, e_mat_v[:, 512 * g:512 * (g + 1)])

        de = jnp.zeros((T, 128), f32)
        dw = jnp.zeros((T, 128), f32)
        ddt_x = jnp.zeros((T, 128), f32)
        dcum_col = jnp.zeros((T, 128), f32)
        dcum_row = jnp.zeros((128, T), f32)
        dprev_parts, dBs, dCs = [], [], []
        for g in range(SSD_GROUPS):
            s5 = slice(512 * g, 512 * (g + 1))
            Bg = xc_ref[:, 1024 + 128 * g:1152 + 128 * g].astype(bf16)
            Cg = xc_ref[:, 1280 + 128 * g:1408 + 128 * g].astype(bf16)
            G = _dot_nt(Cg, Bg)
            Z = _dot(Cg, prevb[:, s5])
            dyg = dyv[:, s5]
            dZb = (dyg * ex[:, s5]).astype(bf16)
            dXd = _dot(Bg, dnewb[:, s5])
            dC = _dot_nt(dZb, prevb[:, s5])
            dB = _dot_nt(Xdb[:, s5], dnewb[:, s5])
            dprev_parts.append(_dot_tn(Cg, dZb) + dnew[:, s5] * cdx[:, s5])
            de = de + red(dyg * Z, g)
            dw = dw + red(dXd * X[:, s5], g)
            dXg = dXd * wx[:, s5]
            dG = jnp.zeros((T, T), f32)
            for j in range(4):
                blk = 4 * g + j
                sl = slice(128 * blk, 128 * (blk + 1))
                Xblk = Xb[:, sl]
                dyblk = dyv[:, sl]
                dyblk_b = dyblk.astype(bf16)
                dxh = []
                for half in range(2):
                    h = 2 * blk + half
                    seg = jnp.minimum(cum[:, h:h + 1] - cum_t[h:h + 1, :], 0.0)
                    Lm = jnp.where(tri, jnp.exp(seg), 0.0)
                    M = G * Lm
                    dyh = jnp.where(lo if half == 0 else ~lo, dyblk, 0.0).astype(bf16)
                    dM = _dot_nt(dyh, Xblk)
                    dG = dG + dM * Lm
                    Q = dM * M
                    dcum_col = dcum_col + jnp.where(lane == h, jnp.sum(Q, axis=1, keepdims=True), 0.0)
                    dcum_row = dcum_row + jnp.where(sub == h, jnp.sum(Q, axis=0, keepdims=True), 0.0)
                    dxh.append(_dot_tn(M.astype(bf16), dyblk_b))
                dXblk = dXg[:, 128 * j:128 * (j + 1)] + jnp.where(lo, dxh[0], dxh[1])
                xsb = xs[:, sl]
                dxs_s[:, sl] = dXblk * dtx[:, sl] + dsk_ref[:, sl] * dyblk
                ddt_x = ddt_x + _dot_sel_nt(dXblk * xsb, e_mat_v[:, sl])
                dd_ref[:, sl] += jnp.sum(dyblk * xsb, axis=0, keepdims=True)
            dGb = dG.astype(bf16)
            dCs.append(dC + _dot(dGb, Bg))
            dBs.append(dB + _dot_tn(dGb, Cg))
        e16 = jnp.exp(cum)
        cum_last = cum[T - 1:T, :]
        w16 = jnp.exp(cum_last - cum)
        dcd = jnp.sum(dnew * prev, axis=0, keepdims=True)
        dcd16 = red(dcd[:, 0:512], 0) + red(dcd[:, 512:1024], 1)
        dww = dw * w16
        extra = jnp.sum(dww, axis=0, keepdims=True) + dcd16 * jnp.exp(cum_last)
        rowi = lax.broadcasted_iota(jnp.int32, (T, 128), 0)
        dcum = dcum_col - dcum_row.T + de * e16 - dww + jnp.where(rowi == T - 1, extra, 0.0)
        da = _sel_dot((c >= r).astype(bf16), dcum)
        ddt = ddt_x + da * a_row
        acc_ref[0:1, :] += jnp.sum(da * dt, axis=0, keepdims=True)
        ddt_raw = jnp.where(lane < SSD_HEADS, ddt * _sigmoid(dt_ref[...] + dtb_ref[...]), 0.0)
        ddt_ref[...] = ddt_raw
        acc_ref[1:2, :] += jnp.sum(ddt_raw, axis=0, keepdims=True)
        dxc_ref[:, 0:SSD_WIDTH] = dxs_s[...]
        dxc_ref[:, 1024:1280] = jnp.concatenate(dBs, axis=1)
        dxc_ref[:, 1280:1536] = jnp.concatenate(dCs, axis=1)
        dst[...] = jnp.concatenate(dprev_parts, axis=1)

    rev = lambda i: (nc - 1 - i, 0)
    return pl.pallas_call(
        body, name="ssd_bwd", grid=(nc,),
        in_specs=[pl.BlockSpec((T, CONV_DIM), rev), pl.BlockSpec((T, 128), rev), pl.BlockSpec((T, SSD_WIDTH), rev),
                  pl.BlockSpec((1, SSD_STATE, SSD_WIDTH), lambda i: (nc - 1 - i, 0, 0)),
                  _full((1, 128)), _full((1, 128)), _full((1, SSD_WIDTH)), _full((128, SSD_WIDTH)), _ANY_SPEC],
        out_specs=[pl.BlockSpec((T, CONV_DIM), rev), pl.BlockSpec((T, 128), rev), _full((8, 128)),
                   _full((1, SSD_WIDTH)), _ANY_SPEC],
        out_shape=[jax.ShapeDtypeStruct((L, CONV_DIM), f32), jax.ShapeDtypeStruct((L, 128), f32),
                   jax.ShapeDtypeStruct((8, 128), f32), jax.ShapeDtypeStruct((1, SSD_WIDTH), f32),
                   jax.ShapeDtypeStruct(g.shape, f32)],
        input_output_aliases={8: 4},
        scratch_shapes=[pltpu.VMEM((SSD_STATE, SSD_WIDTH), f32), pltpu.VMEM((T, SSD_WIDTH), f32)] + _sems(1),
        compiler_params=_cparams("arbitrary"),
    )(xc, dt_raw, dy, prev_all, dt_bias, a_log, d_skip_x, e_mat, g)


def _conv_bwd(dxc, xr, conv_w, conv_b):
    L = dxc.shape[0]
    tm = 256
    nt = L // tm

    def body(dxc_ref, xr_ref, xh_ref, cw_ref, cb_ref, dxr_ref, acc_ref, carry, buf, buf2):
        i = pl.program_id(0)

        @pl.when(i == 0)
        def _():
            carry[...] = jnp.zeros_like(carry)
            acc_ref[...] = jnp.zeros_like(acc_ref)

        buf[0:8, :] = jnp.where(i == nt - 1, 0.0, xh_ref[...])
        buf[8:8 + tm, :] = xr_ref[...]
        pre = cb_ref[...] + cw_ref[0:1, :] * buf[5:5 + tm, :]
        for k in range(1, CONV_K):
            pre = pre + cw_ref[k:k + 1, :] * buf[5 + k:5 + k + tm, :]
        sg = _sigmoid(pre)
        dpre = dxc_ref[...] * (sg * (1.0 + pre * (1.0 - sg)))
        acc_ref[4:5, :] += jnp.sum(dpre, axis=0, keepdims=True)
        for k in range(CONV_K):
            acc_ref[k:k + 1, :] += jnp.sum(dpre * buf[5 + k:5 + k + tm, :], axis=0, keepdims=True)
        buf2[0:tm, :] = dpre
        buf2[tm:tm + 8, :] = carry[...]
        du = cw_ref[0:1, :] * buf2[3:3 + tm, :]
        for k in range(1, CONV_K):
            du = du + cw_ref[k:k + 1, :] * buf2[3 - k:3 - k + tm, :]
        dxr_ref[...] = du.astype(bf16)
        carry[...] = dpre[0:8, :]

    rev = lambda i: (nt - 1 - i, 0)
    return pl.pallas_call(
        body, name="conv_bwd", grid=(nt,),
        in_specs=[pl.BlockSpec((tm, CONV_DIM), rev), pl.BlockSpec((tm, CONV_DIM), rev),
                  pl.BlockSpec((8, CONV_DIM), lambda i: (jnp.maximum((nt - 1 - i) * (tm // 8) - 1, 0), 0)),
                  _full((CONV_K, CONV_DIM)), _full((1, CONV_DIM))],
        out_specs=[pl.BlockSpec((tm, CONV_DIM), rev), _full((8, CONV_DIM))],
        out_shape=[jax.ShapeDtypeStruct((L, CONV_DIM), bf16), jax.ShapeDtypeStruct((8, CONV_DIM), f32)],
        scratch_shapes=[pltpu.VMEM((8, CONV_DIM), f32), pltpu.VMEM((tm + 8, CONV_DIM), f32),
                        pltpu.VMEM((tm + 8, CONV_DIM), f32)],
        compiler_params=_cparams("arbitrary"),
    )(dxc, xr, xr, conv_w, conv_b)


def _inproj_bwd(dz, dxr, dq, dkv, ddt, dr1, x, mod, ln_g, ln_b, w_in, pb):
    L = x.shape[0]
    tm = DENSE_TM
    nt = L // tm

    def body(dz_ref, dxr_ref, dq_ref, dkv_ref, ddt_ref, dr1_ref, x_ref, mod_ref, g_ref, b_ref, w_ref, pb_ref,
             dx_ref, acc_ref, chips_ref, send_sems, recv_sems):
        start, wait = _to_chips_job(pb_ref, chips_ref, send_sems, recv_sems)

        @pl.when(pl.program_id(0) == 0)
        def _():
            acc_ref[...] = jnp.zeros_like(acc_ref)
            start()

        @pl.when(pl.program_id(0) == nt - 1)
        def _():
            wait()

        du1 = (_dot(dz_ref[...], w_ref[C_Z:C_XBC, :]) + _dot(dxr_ref[...], w_ref[C_XBC:C_Q, :])
               + _dot(dq_ref[...], w_ref[C_Q:C_KV, :]) + _dot(dkv_ref[...], w_ref[C_KV:C_DT, :])
               + _dot(ddt_ref[...].astype(bf16), w_ref[C_DT:C_END, :]))
        xhat, rstd = _ln_stats(x_ref[...])
        h0 = xhat * g_ref[...] + b_ref[...]
        acc_ref[0:1, :] += jnp.sum(du1 * h0, axis=0, keepdims=True)
        acc_ref[1:2, :] += jnp.sum(du1, axis=0, keepdims=True)
        dh0 = du1 * (1.0 + mod_ref[1:2, :]) + ALPHA * dr1_ref[...]
        acc_ref[2:3, :] += jnp.sum(dh0 * xhat, axis=0, keepdims=True)
        acc_ref[3:4, :] += jnp.sum(dh0, axis=0, keepdims=True)
        dx_ref[...] = _ln_bwd(dh0, xhat, rstd, g_ref[...])

    v = _full((1, D_MODEL))
    return pl.pallas_call(
        body, name="inproj_bwd", grid=(nt,),
        in_specs=[_rows(tm, D_MODEL), _rows(tm, CONV_DIM), _rows(tm, D_MODEL), _rows(tm, 256), _rows(tm, 128),
                  _rows(tm, D_MODEL), _rows(tm, D_MODEL), _full((8, D_MODEL)), v, v, _resident((C_END, D_MODEL)),
                  _ANY_SPEC],
        out_specs=[_rows(tm, D_MODEL), _full((8, D_MODEL)), _ANY_SPEC],
        out_shape=[jax.ShapeDtypeStruct((L, D_MODEL), f32), jax.ShapeDtypeStruct((8, D_MODEL), f32),
                   jax.ShapeDtypeStruct((N_CHIPS - 1,) + pb.shape[1:], bf16)],
        scratch_shapes=_sems(N_CHIPS - 1),
        compiler_params=_cparams("arbitrary"),
    )(dz, dxr, dq, dkv, ddt, dr1, x, mod, ln_g, ln_b, w_in, pb)


def _adamw_math(w, g, m, v):
    m = ADAM_B1 * m + (1.0 - ADAM_B1) * g
    v = ADAM_B2 * v + (1.0 - ADAM_B2) * (g * g)
    m_hat = m / (1.0 - ADAM_B1 ** ADAM_STEP)
    v_hat = v / (1.0 - ADAM_B2 ** ADAM_STEP)
    delta = -ADAM_LR * (m_hat / (jnp.sqrt(v_hat) + ADAM_EPS) + ADAM_WD * w)
    return delta, m, v


def _adamw(w, g, m, v, name):
    R, C = w.shape

    def body(w_ref, g_ref, m_ref, v_ref, d_ref, m2_ref, v2_ref):
        d_ref[...], m2_ref[...], v2_ref[...] = _adamw_math(w_ref[...], g_ref[...], m_ref[...], v_ref[...])

    cap = max(8, ADAMW_BLOCK_ELEMS // C)
    tr = R if R <= cap else next(t for t in range(cap - cap % 8, 7, -8) if R % t == 0)
    spec = pl.BlockSpec((tr, C), lambda i: (i, 0))
    return pl.pallas_call(
        body, name=name, grid=(R // tr,), in_specs=[spec] * 4, out_specs=[spec] * 3,
        out_shape=[jax.ShapeDtypeStruct((R, C), f32)] * 3, compiler_params=_cparams("parallel"),
    )(w, g, m, v)


ADA_COLS = 6 * D_MODEL // N_CHIPS
ADA_TN = 512


def _ada_fwd(c_all, ada_w, ada_b):
    def body(c_ref, w_ref, b_ref, o_ref):
        cv = c_ref[...]
        o_ref[...] = _dot_exact(cv * _sigmoid(cv), w_ref[...]) + b_ref[...]

    return pl.pallas_call(
        body, name="ada_fwd", grid=(ADA_COLS // ADA_TN,),
        in_specs=[_full((N_DEV, D_MODEL)), pl.BlockSpec((D_MODEL, ADA_TN), lambda j: (0, j)),
                  pl.BlockSpec((1, ADA_TN), lambda j: (0, j))],
        out_specs=pl.BlockSpec((N_DEV, ADA_TN), lambda j: (0, j)),
        out_shape=jax.ShapeDtypeStruct((N_DEV, ADA_COLS), f32), compiler_params=_cparams("parallel"),
    )(c_all, ada_w, ada_b)


def _ada_bwd(c_all, dmod, w, m, v):
    def body(c_ref, d_ref, w_ref, m_ref, v_ref, g_ref, dl_ref, m2_ref, v2_ref):
        cv = c_ref[...]
        g = lax.dot_general(cv * _sigmoid(cv), d_ref[...], (((0,), (0,)), ((), ())), preferred_element_type=f32,
                            precision=lax.Precision.HIGHEST)
        g_ref[...] = g
        dl_ref[...], m2_ref[...], v2_ref[...] = _adamw_math(w_ref[...], g, m_ref[...], v_ref[...])

    wspec = pl.BlockSpec((D_MODEL, ADA_TN), lambda j: (0, j))
    return pl.pallas_call(
        body, name="ada_bwd", grid=(ADA_COLS // ADA_TN,),
        in_specs=[_full((N_DEV, D_MODEL)), pl.BlockSpec((N_DEV, ADA_TN), lambda j: (0, j)), wspec, wspec, wspec],
        out_specs=[wspec] * 4, out_shape=[jax.ShapeDtypeStruct((D_MODEL, ADA_COLS), f32)] * 4,
        compiler_params=_cparams("parallel"),
    )(c_all, dmod, w, m, v)


SMALL_SLOTS = (("ada_b", 6144), ("ln_in_g", 1024), ("ln_in_b", 1024), ("conv_b", 1536), ("dt_bias", 128), ("a_log", 128),
               ("d_skip", 128), ("ssd_norm_w", 1024), ("attn_sinks", 128), ("ln1_g", 1024), ("ln1_b", 1024),
               ("b_ff1", 4096), ("b_ff2", 1024), ("ln2_g", 1024), ("ln2_b", 1024), ("conv_w", 6144), ("loss", 1024))
SMALL_N = sum(n for _, n in SMALL_SLOTS)
SMALL_OFF = {name: sum(n for _, n in SMALL_SLOTS[:i]) for i, (name, _) in enumerate(SMALL_SLOTS)}
SMALL_PARAMS = tuple(name for name, _ in SMALL_SLOTS[:15])
assert SMALL_N % 1024 == 0


def _small_pack(acc_in, acc_out, acc_mlp, db1, acc_conv, acc_ssd, dd_x, dsink, alog, e_mat):
    def body(in_ref, out_ref, mlp_ref, db1_ref, conv_ref, ssd_ref, dd_ref, sink_ref, al_ref, e_ref, o_ref):
        def put(name, val, at=0):
            off = SMALL_OFF[name] + at
            o_ref[:, off:off + val.shape[1]] = val

        for k, row in enumerate((in_ref[1:2, :], in_ref[0:1, :], out_ref[0:1, :], mlp_ref[A_SH2:A_SH2 + 1, :],
                                 mlp_ref[A_SC2:A_SC2 + 1, :], mlp_ref[A_G2:A_G2 + 1, :])):
            put("ada_b", row, D_MODEL * k)
        put("ln_in_g", in_ref[2:3, :])
        put("ln_in_b", in_ref[3:4, :])
        put("conv_b", conv_ref[4:5, :])
        put("dt_bias", ssd_ref[1:2, :])
        put("a_log", ssd_ref[0:1, :] * (-jnp.exp(al_ref[...])))
        put("d_skip", _dot_sel_nt(jnp.broadcast_to(dd_ref[...], (8, SSD_WIDTH)), e_ref[...])[0:1, :])
        put("ssd_norm_w", out_ref[1:2, :])
        put("attn_sinks", sink_ref[...])
        put("ln1_g", mlp_ref[A_LN1G:A_LN1G + 1, :])
        put("ln1_b", mlp_ref[A_LN1B:A_LN1B + 1, :])
        put("b_ff1", db1_ref[...])
        put("b_ff2", mlp_ref[A_B2:A_B2 + 1, :])
        put("ln2_g", mlp_ref[A_LN2G:A_LN2G + 1, :])
        put("ln2_b", mlp_ref[A_LN2B:A_LN2B + 1, :])
        for k in range(CONV_K):
            put("conv_w", conv_ref[k:k + 1, :], CONV_DIM * k)
        put("loss", mlp_ref[A_LOSS:A_LOSS + 1, :])

    return pl.pallas_call(body, name="small_pack", out_shape=jax.ShapeDtypeStruct((1, SMALL_N), f32),
                          compiler_params=_cparams())(acc_in, acc_out, acc_mlp, db1, acc_conv, acc_ssd, dd_x, dsink, alog, e_mat)


def _small_update(gathered, params, moms, vels):
    k = len(SMALL_PARAMS)

    def body(g_ref, *refs):
        w_refs, m_refs, v_refs, outs = refs[:k], refs[k:2 * k], refs[2 * k:3 * k], refs[3 * k:]

        def total(name, n):
            off = SMALL_OFF[name]
            g = g_ref[0:1, off:off + n]
            for i in range(1, N_DEV):
                g = g + g_ref[i:i + 1, off:off + n]
            return g

        for j, name in enumerate(SMALL_PARAMS):
            n = w_refs[j].shape[1]
            g = total(name, max(n, 128))[:, :n]
            outs[4 * j][...] = g
            outs[4 * j + 1][...], outs[4 * j + 2][...], outs[4 * j + 3][...] = _adamw_math(
                w_refs[j][...], g, m_refs[j][...], v_refs[j][...])
        outs[4 * k][...] = total("conv_w", CONV_K * CONV_DIM)
        outs[4 * k + 1][...] = total("loss", D_MODEL)

    shapes = [jax.ShapeDtypeStruct(p.shape, f32) for p in params for _ in range(4)]
    shapes += [jax.ShapeDtypeStruct((1, CONV_K * CONV_DIM), f32), jax.ShapeDtypeStruct((1, D_MODEL), f32)]
    return pl.pallas_call(body, name="small_update", out_shape=shapes,
                          compiler_params=_cparams())(gathered, *params, *moms, *vels)


def _place():
    return lax.axis_index("x"), lax.axis_index("y"), lax.axis_index("c")


def _flip(x, y, c, m):
    return (1 - x if m & 4 else x, 1 - y if m & 2 else y, 1 - c if m & 1 else c)


_VMEM_SPEC = pl.BlockSpec(memory_space=pltpu.VMEM)
_ANY_SPEC = pl.BlockSpec(memory_space=pl.ANY)


def _allgather8(v, name):
    n = v.shape[1]

    def body(v_ref, out_ref, send_sems, recv_sems, local_sem):
        x, y, c = _place()

        def rows(px, py, pc):
            return out_ref.at[pl.ds(pl.multiple_of((4 * px + 2 * py + pc) * 8, 8), 8), :]

        def copy(m, src, dst, to):
            return pltpu.make_async_remote_copy(src_ref=src, dst_ref=dst, send_sem=send_sems.at[m - 1],
                                                recv_sem=recv_sems.at[m - 1], device_id=to, device_id_type=MESH)

        mine = pltpu.make_async_copy(v_ref, rows(x, y, c), local_sem)
        mine.start()
        sends = [copy(m, v_ref, rows(x, y, c), _flip(x, y, c, m)) for m in range(1, N_DEV)]
        for cp in sends:
            cp.start()
        for m in range(1, N_DEV):
            peer = _flip(x, y, c, m)
            copy(m, v_ref, rows(*peer), peer).wait_recv()
        for cp in sends:
            cp.wait_send()
        mine.wait()

    return pl.pallas_call(
        body, name=name, out_shape=jax.ShapeDtypeStruct((8 * N_DEV, n), f32), in_specs=[_VMEM_SPEC],
        out_specs=_VMEM_SPEC,
        scratch_shapes=[pltpu.SemaphoreType.DMA((N_DEV - 1,)), pltpu.SemaphoreType.DMA((N_DEV - 1,)),
                        pltpu.SemaphoreType.DMA],
    )(v)


def _mod_exchange(mod_all):
    def body(src_ref, out_ref, send_sems, recv_sems, local_sem):
        x, y, c = _place()

        def copy(m, src, dst, to):
            return pltpu.make_async_remote_copy(src_ref=src, dst_ref=dst, send_sem=send_sems.at[m - 1],
                                                recv_sem=recv_sems.at[m - 1], device_id=to, device_id_type=MESH)

        mine = pltpu.make_async_copy(src_ref.at[4 * x + 2 * y + c], out_ref.at[2 * x + y], local_sem)
        mine.start()
        sends = []
        for m in range(1, N_CHIPS):
            px, py, pc = _flip(x, y, c, 2 * m)
            sends.append(copy(m, src_ref.at[4 * px + 2 * py + pc], out_ref.at[2 * x + y], (px, py, pc)))
        for cp in sends:
            cp.start()
        for m in range(1, N_CHIPS):
            px, py, pc = _flip(x, y, c, 2 * m)
            copy(m, src_ref.at[0], out_ref.at[2 * px + py], (px, py, pc)).wait_recv()
        for cp in sends:
            cp.wait_send()
        mine.wait()

    return pl.pallas_call(
        body, name="mod_exchange", out_shape=jax.ShapeDtypeStruct((N_CHIPS, 12, 128), f32), in_specs=[_VMEM_SPEC],
        out_specs=_VMEM_SPEC,
        scratch_shapes=[pltpu.SemaphoreType.DMA((N_CHIPS - 1,)), pltpu.SemaphoreType.DMA((N_CHIPS - 1,)),
                        pltpu.SemaphoreType.DMA],
    )(mod_all)


def _remote(src, dst, send_sems, recv_sems, k, to):
    return pltpu.make_async_remote_copy(src_ref=src, dst_ref=dst, send_sem=send_sems.at[k], recv_sem=recv_sems.at[k],
                                        device_id=to, device_id_type=MESH)


def _gather_job(blob_ref, out_ref, send_sems, recv_sems, R):
    x, y, c = _place()
    sib = (x, y, 1 - c)
    hr = R // 2

    def half(px, py, pc):
        return out_ref.at[2 * px + py, pl.ds(pl.multiple_of(pc * hr, 16), hr), :]

    my_half = blob_ref.at[pl.ds(pl.multiple_of(c * hr, 16), hr), :]

    def first():
        return [_remote(my_half, half(x, y, c), send_sems, recv_sems, m - 1, _flip(x, y, c, 2 * m))
                for m in range(1, N_CHIPS)]

    def start():
        for cp in first():
            cp.start()

    def finish():
        passed = []
        for m in range(1, N_CHIPS):
            px, py, pc = _flip(x, y, c, 2 * m)
            _remote(my_half, half(px, py, pc), send_sems, recv_sems, m - 1, (px, py, pc)).wait_recv()
            fwd = _remote(half(px, py, pc), half(px, py, pc), send_sems, recv_sems, 2 + m, sib)
            fwd.start()
            passed.append(fwd)
        for m in range(1, N_CHIPS):
            px, py, pc = _flip(x, y, c, 2 * m)
            _remote(my_half, half(px, py, 1 - pc), send_sems, recv_sems, 2 + m, sib).wait_recv()
        for cp in first() + passed:
            cp.wait_send()

    return start, finish


def _gather_weights(blob):
    R = blob.shape[0]

    def body(blob_ref, out_ref, send_sems, recv_sems, local_sem, stage):
        x, y, c = _place()
        start, finish = _gather_job(blob_ref, out_ref, send_sems, recv_sems, R)
        start()
        for k in range(R // STAGE_ROWS):
            rows = pl.ds(STAGE_ROWS * k, STAGE_ROWS)
            cin = pltpu.make_async_copy(blob_ref.at[rows, :], stage, local_sem)
            cin.start()
            cin.wait()
            cout = pltpu.make_async_copy(stage, out_ref.at[2 * x + y, rows, :], local_sem)
            cout.start()
            cout.wait()
        finish()

    return pl.pallas_call(
        body, name="gather_weights", out_shape=jax.ShapeDtypeStruct((N_CHIPS, R, D_MODEL), bf16),
        in_specs=[_ANY_SPEC], out_specs=_ANY_SPEC,
        scratch_shapes=[pltpu.SemaphoreType.DMA((6,)), pltpu.SemaphoreType.DMA((6,)), pltpu.SemaphoreType.DMA,
                        pltpu.VMEM((STAGE_ROWS, D_MODEL), bf16)],
    )(blob)


def _to_sibling_job(g_ref, out_ref, send_sems, recv_sems):
    x, y, c = _place()

    def cps():
        return [_remote(g_ref.at[j, 1 - c], out_ref.at[j], send_sems, recv_sems, j, (x, y, 1 - c)) for j in range(N_CHIPS)]

    def start():
        for cp in cps():
            cp.start()

    def wait():
        for cp in cps():
            cp.wait()

    return start, wait


def _to_chips_job(p_ref, out_ref, send_sems, recv_sems):
    x, y, c = _place()

    def cps():
        out = []
        for m in range(1, N_CHIPS):
            px, py, pc = _flip(x, y, c, 2 * m)
            out.append(_remote(p_ref.at[2 * px + py], out_ref.at[m - 1], send_sems, recv_sems, m - 1, (px, py, pc)))
        return out

    def start():
        for cp in cps():
            cp.start()

    def wait():
        for cp in cps():
            cp.wait()

    return start, wait


def _share_job(g_ref, send_sems, recv_sems, R):
    x, y, c = _place()

    def rows(pc):
        return g_ref.at[pl.ds(pl.multiple_of(pc * (R // 2), 8), R // 2), :]

    def start():
        _remote(rows(c), rows(c), send_sems, recv_sems, 0, (x, y, 1 - c)).start()

    def wait():
        _remote(rows(c), rows(1 - c), send_sems, recv_sems, 0, (x, y, 1 - c)).wait_recv()
        _remote(rows(c), rows(c), send_sems, recv_sems, 0, (x, y, 1 - c)).wait_send()

    return start, wait


def _sems(n):
    return [pltpu.SemaphoreType.DMA((n,)), pltpu.SemaphoreType.DMA((n,))]


def _rs_to_sibling(gb):
    def body(g_ref, out_ref, send_sems, recv_sems):
        start, wait = _to_sibling_job(g_ref, out_ref, send_sems, recv_sems)
        start()
        wait()

    return pl.pallas_call(
        body, name="rs_to_sibling", out_shape=jax.ShapeDtypeStruct((N_CHIPS,) + gb.shape[2:], bf16),
        in_specs=[_ANY_SPEC], out_specs=_ANY_SPEC, scratch_shapes=_sems(N_CHIPS),
    )(gb)


def _rs_share(g):
    R = g.shape[0]

    def body(g_ref, out_ref, send_sems, recv_sems):
        start, wait = _share_job(out_ref, send_sems, recv_sems, R)
        start()
        wait()

    return pl.pallas_call(
        body, name="rs_share", out_shape=jax.ShapeDtypeStruct(g.shape, f32), in_specs=[_ANY_SPEC],
        out_specs=_ANY_SPEC, input_output_aliases={0: 0}, scratch_shapes=_sems(1),
    )(g)


RS_TR = 256


def _rs_sum_pair(place, gb, recv, name):
    hr = gb.shape[2]

    def body(pl_ref, g_ref, r_ref, o_ref):
        o_ref[0] = (g_ref[0, 0].astype(f32) + r_ref[0].astype(f32)).astype(bf16)

    return pl.pallas_call(
        body, name=name,
        grid_spec=pltpu.PrefetchScalarGridSpec(
            num_scalar_prefetch=1, grid=(N_CHIPS, hr // RS_TR),
            in_specs=[pl.BlockSpec((1, 1, RS_TR, D_MODEL), lambda j, i, p: (j, p[0], i, 0)),
                      pl.BlockSpec((1, RS_TR, D_MODEL), lambda j, i, p: (j, i, 0))],
            out_specs=pl.BlockSpec((1, RS_TR, D_MODEL), lambda j, i, p: (j, i, 0))),
        out_shape=jax.ShapeDtypeStruct((N_CHIPS, hr, D_MODEL), bf16),
        compiler_params=_cparams("parallel", "parallel"),
    )(place, gb, recv)


def _rs_sum_chips(place, gb, recv_sib, recv_chips, name):
    hr = gb.shape[2]
    nt = hr // RS_TR

    def body(pl_ref, g_ref, r1_ref, r2_ref, o_ref):
        acc = g_ref[0, 0].astype(f32) + r1_ref[0].astype(f32)
        for k in range(N_CHIPS - 1):
            acc = acc + r2_ref[k].astype(f32)
        o_ref[...] = acc

    return pl.pallas_call(
        body, name=name,
        grid_spec=pltpu.PrefetchScalarGridSpec(
            num_scalar_prefetch=1, grid=(nt,),
            in_specs=[pl.BlockSpec((1, 1, RS_TR, D_MODEL), lambda i, p: (p[1], p[0], i, 0)),
                      pl.BlockSpec((1, RS_TR, D_MODEL), lambda i, p: (p[1], i, 0)),
                      pl.BlockSpec((N_CHIPS - 1, RS_TR, D_MODEL), lambda i, p: (0, i, 0))],
            out_specs=pl.BlockSpec((RS_TR, D_MODEL), lambda i, p: (p[0] * nt + i, 0))),
        out_shape=jax.ShapeDtypeStruct((2 * hr, D_MODEL), f32),
        compiler_params=_cparams("parallel"),
    )(place, gb, recv_sib, recv_chips)


def _pad128(v):
    v = v.reshape(1, -1)
    return jnp.pad(v, ((0, 0), (0, 128 - v.shape[1])))


def _row(v):
    return v.reshape(1, -1)


W_COLS = PROJ_WIDTH // N_CHIPS
M_XBC, M_DT, M_Q = 1024, 2560, 2576


def _model_rows(blocks, lo, hi):
    out = []
    for j in range(N_CHIPS):
        a, b = max(lo, W_COLS * j), min(hi, W_COLS * (j + 1))
        if a < b:
            out.append(blocks[j, a - W_COLS * j:b - W_COLS * j])
    return out


def _w_in_layout(wall_in):
    parts = (_model_rows(wall_in, 0, M_DT) + _model_rows(wall_in, M_Q, PROJ_WIDTH) + _model_rows(wall_in, M_DT, M_Q)
             + [jnp.zeros((C_END - PROJ_WIDTH, D_MODEL), wall_in.dtype)])
    return jnp.concatenate(parts, axis=0)


def _g_in_blocks(gz, gxbc, gdt, gq, gkv):
    g = jnp.concatenate([gz, gxbc, gdt[:M_Q - M_DT], gq, gkv], axis=0)
    return jnp.pad(g.reshape(N_CHIPS, W_COLS, D_MODEL), ((0, 0), (0, D_MODEL - W_COLS), (0, 0)))


def kernel(x, c, ln_in_g, ln_in_b, ada_w, ada_b, w_in, conv_w, conv_b, dt_bias, a_log, d_skip, ssd_norm_w, attn_sinks, w_out, ln1_g, ln1_b, w_ff1, b_ff1, w_ff2, b_ff2, ln2_g, ln2_b, loss_target, m_ln_in_g, m_ln_in_b, m_ada_w, m_ada_b, m_w_in, m_conv_w, m_conv_b, m_dt_bias, m_a_log, m_d_skip, m_ssd_norm_w, m_attn_sinks, m_w_out, m_ln1_g, m_ln1_b, m_w_ff1, m_b_ff1, m_w_ff2, m_b_ff2, m_ln2_g, m_ln2_b, v_ln_in_g, v_ln_in_b, v_ada_w, v_ada_b, v_w_in, v_conv_w, v_conv_b, v_dt_bias, v_a_log, v_d_skip, v_ssd_norm_w, v_attn_sinks, v_w_out, v_ln1_g, v_ln1_b, v_w_ff1, v_b_ff1, v_w_ff2, v_b_ff2, v_ln2_g, v_ln2_b):
    xi, yi, ci = _place()
    chip = 2 * xi + yi
    place = jnp.stack([ci, chip]).astype(jnp.int32)
    x2, tgt = x[0], loss_target[0]
    w_shard_cols = PROJ_WIDTH // N_CHIPS

    cond = jnp.concatenate([c.reshape(-1), conv_w.reshape(-1), jnp.zeros((512,), f32)]).reshape(8, 384)
    cond_all = _allgather8(cond, "gather_cond").reshape(N_DEV, 3072)
    c_all = cond_all[:, :D_MODEL]
    conv_w_full = jnp.concatenate([cond_all[2 * j, D_MODEL:D_MODEL + 1536].reshape(CONV_K, 384) for j in range(N_CHIPS)], axis=1)

    ada_b_mine = lax.dynamic_slice(ada_b, (0, chip * ADA_COLS), (1, ADA_COLS))
    mod_all = _ada_fwd(c_all, ada_w[0], ada_b_mine)
    mod_mine = _mod_exchange(mod_all.reshape(N_DEV, 12, 128)).reshape(6, D_MODEL)
    mod = jnp.concatenate([mod_mine, jnp.zeros((2, D_MODEL), f32)], axis=0)

    def as_lines(a):
        return jnp.transpose(a, (2, 0, 1)).reshape(W_COLS * D_MODEL // 128, 128)

    def from_lines(a):
        return jnp.transpose(a.reshape(W_COLS, 1, D_MODEL), (1, 2, 0))

    w_in_t = w_in[0].T

    wall_in = _gather_weights(jnp.pad(w_in_t, ((0, D_MODEL - w_shard_cols), (0, 0))).astype(bf16))
    w_in_f = _w_in_layout(wall_in)
    b_ff1w, b_ff2w, b_outw = w_ff1[0].astype(bf16), w_ff2[0].astype(bf16), w_out[0].astype(bf16)

    def with_mine(wall, mine):
        return lax.dynamic_update_slice(wall, mine[None], (chip, 0, 0))

    e_mat = _head_expand()
    dsk_x = jnp.repeat(d_skip[0], HEAD_DIM).reshape(1, SSD_WIDTH)
    dtb, alog = _pad128(dt_bias), _pad128(a_log)
    sinks = attn_sinks[0]
    lng, lnb = _row(ln_in_g), _row(ln_in_b)
    u1, z, xr, xc, q, kv, dtr, wall_ff1 = _inproj_fwd(x2, mod, lng, lnb, w_in_f, conv_w_full, conv_b, b_ff1w)
    y, prev_all, wall_ff2 = _ssd_fwd(xc, dtr, dtb, alog, dsk_x, e_mat, b_ff2w)
    o, lse, wall_out = _attn_fwd(q, kv, sinks, b_outw)
    wall_ff1, wall_ff2, wall_out = with_mine(wall_ff1, b_ff1w), with_mine(wall_ff2, b_ff2w), with_mine(wall_out, b_outw)
    yn, mix, r1 = _outproj_fwd(y, z, o, x2, mod, lng, lnb, ssd_norm_w, wall_out)

    dr1, u2, s_act, da, df, acc_mlp, db1 = _mlp_fwd_bwd(r1, tgt, mod, ln1_g, ln1_b, ln2_g, ln2_b, wall_ff1, b_ff1, wall_ff2,
                                                        b_ff2)
    ga = jnp.zeros((N_CHIPS, GA_ROWS, D_MODEL), bf16)
    ga = _wgrad_blob(ga, u2, da, "wgrad_ff1", lambda t, n: (n, t))
    ga = _wgrad_blob(ga, s_act, df, "wgrad_ff2", lambda t, n: (t // 2, 2 + t % 2))
    ga = ga.reshape(N_CHIPS, 2, GA_ROWS // 2, D_MODEL)
    dy, dz, do, dmix, acc_out, a_sib = _outproj_bwd(dr1, mix, y, z, mod, ssd_norm_w, wall_out, ga)
    gb = jnp.zeros((N_CHIPS, GB_ROWS, D_MODEL), bf16)
    gb = _wgrad_blob(gb, yn, dmix, "wgrad_out_y", lambda t, n: (t, 2))
    gb = _wgrad_blob(gb, o, dmix, "wgrad_out_o", lambda t, n: (2 + t, 2))
    a_pair = _rs_sum_pair(place, ga, a_sib, "rs_sum_pair_a")
    dq, dkv, dsink, a_chips = _attn_bwd(q, kv, do, lse, sinks, a_pair)
    g_a = _rs_sum_chips(place, ga, a_sib, a_chips, "rs_sum_chips_a")
    dxc, ddt, acc_ssd, dd_x, g_a = _ssd_bwd(xc, dtr, dy, prev_all, dtb, alog, dsk_x, e_mat, g_a)
    dxr, acc_conv = _conv_bwd(dxc, xr, conv_w_full, conv_b)
    g_in_c = _g_in_blocks(_wgrad(dz, u1, "wgrad_in_z"), _wgrad(dxr, u1, "wgrad_in_xbc"),
                          _wgrad(ddt.astype(bf16), u1, "wgrad_in_dt"), _wgrad(dq, u1, "wgrad_in_q"),
                          _wgrad(dkv, u1, "wgrad_in_kv"))
    gb = lax.dynamic_update_slice(gb, g_in_c, (0, 0, 0)).reshape(N_CHIPS, 2, GB_ROWS // 2, D_MODEL)
    b_sib = _rs_to_sibling(gb)
    b_pair = _rs_sum_pair(place, gb, b_sib, "rs_sum_pair_b")
    grad_x, acc_in, b_chips = _inproj_bwd(dz, dxr, dq, dkv, ddt, dr1, x2, mod, lng, lnb, w_in_f, b_pair)
    g_b = _rs_share(_rs_sum_chips(place, gb, b_sib, b_chips, "rs_sum_chips_b"))

    packed = _small_pack(acc_in, acc_out, acc_mlp, db1, acc_conv, acc_ssd, dd_x, dsink, alog, e_mat)
    small_all = _allgather8(packed.reshape(8, SMALL_N // 8), "gather_small").reshape(N_DEV, SMALL_N)
    given = dict(ada_b=(ada_b, m_ada_b, v_ada_b), ln_in_g=(ln_in_g, m_ln_in_g, v_ln_in_g), ln_in_b=(ln_in_b, m_ln_in_b, v_ln_in_b),
                 conv_b=(conv_b, m_conv_b, v_conv_b), dt_bias=(dt_bias, m_dt_bias, v_dt_bias), a_log=(a_log, m_a_log, v_a_log),
                 d_skip=(d_skip, m_d_skip, v_d_skip), ssd_norm_w=(ssd_norm_w, m_ssd_norm_w, v_ssd_norm_w),
                 attn_sinks=(attn_sinks, m_attn_sinks, v_attn_sinks), ln1_g=(ln1_g, m_ln1_g, v_ln1_g),
                 ln1_b=(ln1_b, m_ln1_b, v_ln1_b), b_ff1=(b_ff1, m_b_ff1, v_b_ff1), b_ff2=(b_ff2, m_b_ff2, v_b_ff2),
                 ln2_g=(ln2_g, m_ln2_g, v_ln2_g), ln2_b=(ln2_b, m_ln2_b, v_ln2_b))
    upd = _small_update(small_all, *([_row(given[n][i]) for n in SMALL_PARAMS] for i in range(3)))
    small_res = {n: [t.reshape(given[n][0].shape) for t in upd[4 * j:4 * j + 4]] for j, n in enumerate(SMALL_PARAMS)}
    g_conv_all, loss_lanes = upd[4 * len(SMALL_PARAMS)], upd[4 * len(SMALL_PARAMS) + 1]
    loss = jnp.sum(loss_lanes)

    dmod_mine = lax.dynamic_slice(small_all[:, :6 * D_MODEL], (0, chip * ADA_COLS), (N_DEV, ADA_COLS))
    big = {"ada_w": [t[None] for t in _ada_bwd(c_all, dmod_mine, ada_w[0], m_ada_w[0], v_ada_w[0])]}

    g_conv = lax.dynamic_slice(g_conv_all.reshape(CONV_K, CONV_DIM), (0, chip * 384), (CONV_K, 384))
    big["conv_w"] = [t[None] for t in (g_conv, *_adamw(conv_w[0], g_conv, m_conv_w[0], v_conv_w[0], "adamw_conv_w"))]

    g_lin = g_b[:w_shard_cols].reshape(W_COLS * D_MODEL // 128, 128)
    big["w_in"] = [from_lines(t) for t in (g_lin, *_adamw(as_lines(w_in), g_lin, as_lines(m_w_in), as_lines(v_w_in), "adamw_w_in"))]
    for name, g, (w, m, v) in (("w_out", g_b[D_MODEL:GB_ROWS], (w_out, m_w_out, v_w_out)),
                               ("w_ff1", g_a[:D_MODEL], (w_ff1, m_w_ff1, v_w_ff1)),
                               ("w_ff2", g_a[D_MODEL:GA_ROWS], (w_ff2, m_w_ff2, v_w_ff2))):
        big[name] = [t[None] for t in (g, *_adamw(w[0], g, m[0], v[0], "adamw_" + name))]

    order = ("ln_in_g", "ln_in_b", "ada_w", "ada_b", "w_in", "conv_w", "conv_b", "dt_bias", "a_log", "d_skip", "ssd_norm_w",
             "attn_sinks", "w_out", "ln1_g", "ln1_b", "w_ff1", "b_ff1", "w_ff2", "b_ff2", "ln2_g", "ln2_b")
    res = {**small_res, **big}
    return (loss, grad_x[None], *[res[n][k] for k in range(4) for n in order])
```

```python
import functools
import math

import numpy as np
import jax
import jax.numpy as jnp
from jax import lax
from jax.experimental import pallas as pl
from jax.experimental.pallas import tpu as pltpu

f32 = jnp.float32
bf16 = jnp.bfloat16

D_MODEL = 1024
SSD_WIDTH = 1024
SSD_HEADS = 16
HEAD_DIM = 64
SSD_STATE = 128
SSD_GROUPS = 2
CHUNK = 128
CONV_K = 4
CONV_DIM = 1536
ATTN_HEADS = 16
D_FF = 4096
PROJ_WIDTH = 3856
ALPHA = 2.0 ** 0.25
LN_EPS = 1e-5
RMS_EPS = 1e-5
ATTN_SCALE = HEAD_DIM ** -0.5
NEG = -1e30

ADAM_LR = 0.001
ADAM_B1 = 0.9
ADAM_B2 = 0.999
ADAM_EPS = 1e-08
ADAM_WD = 0.01
ADAM_STEP = 10

W_Z, W_XBC, W_Q, W_KV = slice(0, 1024), slice(1024, 2560), slice(2576, 3600), slice(3600, 3856)
W_DT = slice(2560, 2688)
W_DT_ROWS = 16
GA_ROWS = 2048
GB_ROWS = 1536
WG_TM = 512
STAGE_ROWS = 512
ADAMW_BLOCK_ELEMS = 1 << 18
DENSE_TM = 512
N_CHIPS = 4
N_DEV = 8
VMEM_LIMIT = 56 * 1024 * 1024
MESH = pl.DeviceIdType.MESH

ALIBI_SLOPES = tuple(2.0 ** (-8.0 / ATTN_HEADS * (i + 1)) for i in range(ATTN_HEADS))


def _cparams(*sem):
    return pltpu.CompilerParams(dimension_semantics=sem, vmem_limit_bytes=VMEM_LIMIT)


def _sigmoid(x):
    return 1.0 / (1.0 + jnp.exp(-x))


def _softplus(x):
    return jnp.maximum(x, 0.0) + jnp.log1p(jnp.exp(-jnp.abs(x)))


def _ln_stats(x):
    mu = jnp.mean(x, axis=-1, keepdims=True)
    xc = x - mu
    var = jnp.mean(xc * xc, axis=-1, keepdims=True)
    rstd = lax.rsqrt(var + LN_EPS)
    return xc * rstd, rstd


def _ln_bwd(dy, xhat, rstd, g):
    dxh = dy * g
    m1 = jnp.mean(dxh, axis=-1, keepdims=True)
    m2 = jnp.mean(dxh * xhat, axis=-1, keepdims=True)
    return rstd * (dxh - m1 - xhat * m2)


def _dot(a, b):
    return jnp.dot(a, b, preferred_element_type=f32)


def _dot_nt(a, b):
    return lax.dot_general(a, b, (((1,), (1,)), ((), ())), preferred_element_type=f32)


def _dot_tn(a, b):
    return lax.dot_general(a, b, (((0,), (0,)), ((), ())), preferred_element_type=f32)


def _dot_exact(a, b):
    return jnp.dot(a, b, preferred_element_type=f32, precision=lax.Precision.HIGHEST)


def _split3(v):
    hi = v.astype(bf16)
    r1 = v - hi.astype(f32)
    mid = r1.astype(bf16)
    lo = (r1 - mid.astype(f32)).astype(bf16)
    return hi, mid, lo


def _sel_dot(sel, v):
    hi, mid, lo = _split3(v)
    return _dot(sel, hi) + _dot(sel, mid) + _dot(sel, lo)


def _dot_sel(v, sel):
    hi, mid, lo = _split3(v)
    return _dot(hi, sel) + _dot(mid, sel) + _dot(lo, sel)


def _dot_sel_nt(v, sel):
    hi, mid, lo = _split3(v)
    return _dot_nt(hi, sel) + _dot_nt(mid, sel) + _dot_nt(lo, sel)


def _full(shape):
    nd = len(shape)
    return pl.BlockSpec(shape, lambda *_: (0,) * nd)


def _resident(shape):
    nd = len(shape)
    return pl.BlockSpec(shape, lambda *_: (0,) * nd, pipeline_mode=pl.Buffered(1))


def _rows(tm, n):
    return pl.BlockSpec((tm, n), lambda i: (i, 0))


def _inproj_fwd(x, mod, ln_g, ln_b, w_in_t, conv_w, conv_b, blob):
    L = x.shape[0]
    tm = DENSE_TM
    nt = L // tm
    R = blob.shape[0]

    def body(x_ref, mod_ref, g_ref, b_ref, w_ref, cw_ref, cb_ref, blob_ref,
             u1_ref, z_ref, xr_ref, xc_ref, q_ref, kv_ref, dt_ref, wall_ref, halo, buf, send_sems, recv_sems):
        start, finish = _gather_job(blob_ref, wall_ref, send_sems, recv_sems, R)

        @pl.when(pl.program_id(0) == 0)
        def _():
            halo[...] = jnp.zeros_like(halo)
            start()

        xhat, _ = _ln_stats(x_ref[...])
        h0 = xhat * g_ref[...] + b_ref[...]
        u1 = (h0 * (1.0 + mod_ref[1:2, :]) + mod_ref[0:1, :]).astype(bf16)
        u1_ref[...] = u1
        z_ref[...] = _dot_nt(u1, w_ref[W_Z, :])
        xr = _dot_nt(u1, w_ref[W_XBC, :])
        xr_ref[...] = xr
        q_ref[...] = _dot_nt(u1, w_ref[W_Q, :]).astype(bf16)
        kv_ref[...] = _dot_nt(u1, w_ref[W_KV, :]).astype(bf16)
        dt_ref[...] = _dot_nt(u1, w_ref[W_DT, :])
        buf[0:8, :] = halo[...]
        buf[8:8 + tm, :] = xr
        pre = cb_ref[...] + cw_ref[0:1, :] * buf[5:5 + tm, :]
        for k in range(1, CONV_K):
            pre = pre + cw_ref[k:k + 1, :] * buf[5 + k:5 + k + tm, :]
        xc_ref[...] = pre * _sigmoid(pre)
        halo[...] = xr[tm - 8:tm, :]

        @pl.when(pl.program_id(0) == nt - 1)
        def _():
            finish()

    return pl.pallas_call(
        body, name="inproj_fwd", grid=(nt,),
        in_specs=[_rows(tm, D_MODEL), _full((8, D_MODEL)), _full((1, D_MODEL)), _full((1, D_MODEL)),
                  _resident((PROJ_WIDTH, D_MODEL)), _full((CONV_K, CONV_DIM)), _full((1, CONV_DIM)), _ANY_SPEC],
        out_specs=[_rows(tm, D_MODEL), _rows(tm, D_MODEL), _rows(tm, CONV_DIM), _rows(tm, CONV_DIM),
                   _rows(tm, D_MODEL), _rows(tm, 256), _rows(tm, 128), _ANY_SPEC],
        out_shape=[jax.ShapeDtypeStruct((L, D_MODEL), bf16), jax.ShapeDtypeStruct((L, D_MODEL), f32),
                   jax.ShapeDtypeStruct((L, CONV_DIM), f32), jax.ShapeDtypeStruct((L, CONV_DIM), f32),
                   jax.ShapeDtypeStruct((L, D_MODEL), bf16), jax.ShapeDtypeStruct((L, 256), bf16),
                   jax.ShapeDtypeStruct((L, 128), f32), jax.ShapeDtypeStruct((N_CHIPS, R, D_MODEL), bf16)],
        scratch_shapes=[pltpu.VMEM((8, CONV_DIM), f32), pltpu.VMEM((tm + 8, CONV_DIM), f32),
                        pltpu.SemaphoreType.DMA((6,)), pltpu.SemaphoreType.DMA((6,))],
        compiler_params=_cparams("arbitrary"),
    )(x, mod, ln_g, ln_b, w_in_t, conv_w, conv_b, blob)


def _head_expand():
    e = np.zeros((128, SSD_WIDTH), np.float32)
    for h in range(SSD_HEADS):
        e[h, h * HEAD_DIM:(h + 1) * HEAD_DIM] = 1.0
    return jnp.asarray(e, dtype=bf16)


def _ssd_chunk_common(dt_raw, dtb, a_row, e_mat):
    T = CHUNK
    lane = lax.broadcasted_iota(jnp.int32, (T, 128), 1)
    dt = jnp.where(lane < SSD_HEADS, _softplus(dt_raw + dtb), 0.0)
    a = dt * a_row
    r = lax.broadcasted_iota(jnp.int32, (T, T), 0)
    c = lax.broadcasted_iota(jnp.int32, (T, T), 1)
    tril = (c <= r).astype(bf16)
    cum = _sel_dot(tril, a)
    dtx = _dot_sel(dt, e_mat)
    cumx = _dot_sel(cum, e_mat)
    return dt, a, cum, dtx, cumx, r, c


def _ssd_fwd(xc, dt_raw, dt_bias, a_log, d_skip_x, e_mat, blob):
    L = xc.shape[0]
    nc = L // CHUNK
    T = CHUNK
    R = blob.shape[0]

    def body(xc_ref, dt_ref, dtb_ref, al_ref, dsk_ref, e_ref, blob_ref, y_ref, prev_ref, wall_ref, st, send_sems, recv_sems):
        start, finish = _gather_job(blob_ref, wall_ref, send_sems, recv_sems, R)

        @pl.when(pl.program_id(0) == 0)
        def _():
            st[...] = jnp.zeros_like(st)
            start()

        @pl.when(pl.program_id(0) == nc - 1)
        def _():
            finish()

        a_row = -jnp.exp(al_ref[...])
        lane1 = lax.broadcasted_iota(jnp.int32, (1, 128), 1)
        a_row = jnp.where(lane1 < SSD_HEADS, a_row, 0.0)
        dt, a, cum, dtx, cumx, r, c = _ssd_chunk_common(dt_ref[...], dtb_ref[...], a_row, e_ref[...])
        cum_t = cum.T
        ex = jnp.exp(cumx)
        last = cumx[T - 1:T, :]
        wx = jnp.exp(last - cumx)
        cdx = jnp.exp(last)
        xs = xc_ref[:, 0:SSD_WIDTH]
        X = xs * dtx
        Xb = X.astype(bf16)
        Xd = (X * wx).astype(bf16)
        prev = st[...]
        prev_ref[0] = prev
        prevb = prev.astype(bf16)
        tri = c <= r
        lane = lax.broadcasted_iota(jnp.int32, (T, 128), 1)
        y_blocks = []
        new_states = []
        for g in range(SSD_GROUPS):
            Bg = xc_ref[:, 1024 + 128 * g:1152 + 128 * g].astype(bf16)
            Cg = xc_ref[:, 1280 + 128 * g:1408 + 128 * g].astype(bf16)
            G = _dot_nt(Cg, Bg)
            yoff = _dot(Cg, prevb[:, 512 * g:512 * (g + 1)])
            new_states.append(_dot_tn(Bg, Xd[:, 512 * g:512 * (g + 1)]))
            for j in range(4):
                blk = 4 * g + j
                Xblk = Xb[:, 128 * blk:128 * (blk + 1)]
                ys = []
                for half in range(2):
                    h = 2 * blk + half
                    seg = jnp.minimum(cum[:, h:h + 1] - cum_t[h:h + 1, :], 0.0)
                    M = jnp.where(tri, G * jnp.exp(seg), 0.0).astype(bf16)
                    ys.append(_dot(M, Xblk))
                yd = jnp.where(lane < HEAD_DIM, ys[0], ys[1])
                sl = slice(128 * blk, 128 * (blk + 1))
                y_blocks.append(yd + ex[:, sl] * yoff[:, 128 * j:128 * (j + 1)] + dsk_ref[:, sl] * xs[:, sl])
        y_ref[...] = jnp.concatenate(y_blocks, axis=1)
        st[...] = prev * cdx + jnp.concatenate(new_states, axis=1)

    return pl.pallas_call(
        body, name="ssd_fwd", grid=(nc,),
        in_specs=[_rows(T, CONV_DIM), _rows(T, 128), _full((1, 128)), _full((1, 128)), _full((1, SSD_WIDTH)),
                  _full((128, SSD_WIDTH)), _ANY_SPEC],
        out_specs=[_rows(T, SSD_WIDTH), pl.BlockSpec((1, SSD_STATE, SSD_WIDTH), lambda i: (i, 0, 0)), _ANY_SPEC],
        out_shape=[jax.ShapeDtypeStruct((L, SSD_WIDTH), f32), jax.ShapeDtypeStruct((nc, SSD_STATE, SSD_WIDTH), f32),
                   jax.ShapeDtypeStruct((N_CHIPS, R, D_MODEL), bf16)],
        scratch_shapes=[pltpu.VMEM((SSD_STATE, SSD_WIDTH), f32)] + _sems(6),
        compiler_params=_cparams("arbitrary"),
    )(xc, dt_raw, dt_bias, a_log, d_skip_x, e_mat, blob)


def _kv_halves(kv_prev, kv_cur, first):
    kv = jnp.concatenate([jnp.where(first, 0.0, kv_prev.astype(f32)), kv_cur.astype(f32)], axis=0)
    lane = lax.broadcasted_iota(jnp.int32, (2 * CHUNK, 128), 1)
    lo = lane < HEAD_DIM
    out = []
    for g in range(2):
        per_half = []
        for half in range(2):
            both = []
            for t in (kv[:, 0:128], kv[:, 128:256]):
                src = t if g == half else pltpu.roll(t, HEAD_DIM, 1)
                both.append(jnp.where(lo if half == 0 else ~lo, src, 0.0).astype(bf16))
            per_half.append(tuple(both))
        out.append(per_half)
    return out


def _attn_masks(first):
    r = lax.broadcasted_iota(jnp.int32, (CHUNK, 2 * CHUNK), 0)
    c = lax.broadcasted_iota(jnp.int32, (CHUNK, 2 * CHUNK), 1)
    dist = r + CHUNK - c
    valid = (dist >= 0) & (dist < CHUNK) & ((c >= CHUNK) | jnp.logical_not(first))
    return dist.astype(f32), valid


def _attn_fwd(q, kv, sinks, blob):
    L = q.shape[0]
    nb = L // CHUNK
    T = CHUNK
    R = blob.shape[0]

    def body(sink_ref, q_ref, kvp_ref, kvc_ref, blob_ref, o_ref, lse_ref, wall_ref, send_sems, recv_sems):
        first = pl.program_id(0) == 0
        start, finish = _gather_job(blob_ref, wall_ref, send_sems, recv_sems, R)

        @pl.when(first)
        def _():
            start()

        @pl.when(pl.program_id(0) == nb - 1)
        def _():
            finish()

        ext = _kv_halves(kvp_ref[...], kvc_ref[...], first)
        dist, valid = _attn_masks(first)
        lane = lax.broadcasted_iota(jnp.int32, (T, 128), 1)
        lse = jnp.zeros((T, 128), f32)
        o_blocks = []
        for blk in range(8):
            qb = q_ref[:, 128 * blk:128 * (blk + 1)]
            acc = None
            for half in range(2):
                h = 2 * blk + half
                k_ext, v_ext = ext[h // 8][half]
                s = _dot_nt(qb, k_ext) * ATTN_SCALE - ALIBI_SLOPES[h] * dist
                s = jnp.where(valid, s, NEG)
                sink = sink_ref[h]
                m = jnp.maximum(jnp.max(s, axis=-1, keepdims=True), sink)
                p = jnp.exp(s - m)
                den = jnp.sum(p, axis=-1, keepdims=True) + jnp.exp(sink - m)
                pn = (p * (1.0 / den)).astype(bf16)
                oh = _dot(pn, v_ext)
                acc = oh if acc is None else acc + oh
                lse = jnp.where(lane == h, m + jnp.log(den), lse)
            o_blocks.append(acc.astype(bf16))
        o_ref[...] = jnp.concatenate(o_blocks, axis=1)
        lse_ref[...] = lse

    return pl.pallas_call(
        body, name="attn_fwd", grid=(nb,),
        in_specs=[pl.BlockSpec(memory_space=pltpu.SMEM), _rows(T, D_MODEL),
                  pl.BlockSpec((T, 256), lambda i: (jnp.maximum(i - 1, 0), 0)), _rows(T, 256), _ANY_SPEC],
        out_specs=[_rows(T, D_MODEL), _rows(T, 128), _ANY_SPEC],
        out_shape=[jax.ShapeDtypeStruct((L, D_MODEL), bf16), jax.ShapeDtypeStruct((L, 128), f32),
                   jax.ShapeDtypeStruct((N_CHIPS, R, D_MODEL), bf16)],
        scratch_shapes=_sems(6),
        compiler_params=_cparams("arbitrary"),
    )(sinks, q, kv, kv, blob)


def _gated_norm(y, z, w):
    sz = _sigmoid(z)
    hg = y * (z * sz)
    ns, rss = [], []
    for g in range(SSD_GROUPS):
        hs = hg[:, 512 * g:512 * (g + 1)]
        rs = lax.rsqrt(jnp.mean(hs * hs, axis=-1, keepdims=True) + RMS_EPS)
        ns.append(hs * rs)
        rss.append(rs)
    n = jnp.concatenate(ns, axis=1)
    return n * w, n, rss, sz


def _outproj_fwd(y, z, o, x, mod, ln_g, ln_b, norm_w, w_out):
    L = x.shape[0]
    tm = DENSE_TM

    def body(y_ref, z_ref, o_ref, x_ref, mod_ref, g_ref, b_ref, nw_ref, w_ref, yn_ref, mix_ref, r1_ref):
        yn, _, _, _ = _gated_norm(y_ref[...], z_ref[...], nw_ref[...])
        ynb = yn.astype(bf16)
        yn_ref[...] = ynb
        mix = (_dot(ynb[:, 0:512], w_ref[0]) + _dot(ynb[:, 512:1024], w_ref[1])
               + _dot(o_ref[:, 0:512], w_ref[2]) + _dot(o_ref[:, 512:1024], w_ref[3]))
        mix_ref[...] = mix
        xhat, _ = _ln_stats(x_ref[...])
        h0 = xhat * g_ref[...] + b_ref[...]
        r1_ref[...] = ALPHA * h0 + (1.0 + mod_ref[2:3, :]) * mix

    v = _full((1, D_MODEL))
    return pl.pallas_call(
        body, name="outproj_fwd", grid=(L // tm,),
        in_specs=[_rows(tm, D_MODEL), _rows(tm, D_MODEL), _rows(tm, D_MODEL), _rows(tm, D_MODEL),
                  _full((8, D_MODEL)), v, v, v, _resident((N_CHIPS, 512, D_MODEL))],
        out_specs=[_rows(tm, D_MODEL)] * 3,
        out_shape=[jax.ShapeDtypeStruct((L, D_MODEL), bf16), jax.ShapeDtypeStruct((L, D_MODEL), f32),
                   jax.ShapeDtypeStruct((L, D_MODEL), f32)],
        compiler_params=_cparams("parallel"),
    )(y, z, o, x, mod, ln_g, ln_b, norm_w, w_out)


A_LN2G, A_LN2B, A_G2, A_B2, A_SC2, A_SH2, A_LN1G, A_LN1B, A_LOSS = range(9)


def _mlp_fwd_bwd(r1, target, mod, ln1_g, ln1_b, ln2_g, ln2_b, w1, b1, w2, b2):
    L = r1.shape[0]
    tm = 256
    nj = D_FF // 1024

    def body(r1_ref, t_ref, mod_ref, g1_ref, bb1_ref, g2_ref, bb2_ref, w1_ref, b1_ref, w2_ref, b2_ref,
             dr1_ref, u2_ref, s_ref, da_ref, df_ref, acc_ref, db1_ref, hr):
        @pl.when(pl.program_id(0) == 0)
        def _():
            acc_ref[...] = jnp.zeros_like(acc_ref)
            db1_ref[...] = jnp.zeros_like(db1_ref)

        sc2, sh2, gate2 = mod_ref[4:5, :], mod_ref[3:4, :], mod_ref[5:6, :]
        xhat1, rstd1 = _ln_stats(r1_ref[...])
        h1 = xhat1 * g1_ref[...] + bb1_ref[...]
        u2f = h1 * (1.0 + sc2) + sh2
        u2 = u2f.astype(bf16)
        u2_ref[...] = u2
        f = jnp.zeros((tm, D_MODEL), f32) + b2_ref[...]
        for j in range(nj):
            cs = slice(1024 * j, 1024 * (j + 1))
            a = _dot(u2, w1_ref[j]) + b1_ref[:, cs]
            hrj = jnp.maximum(a, 0.0)
            hr[:, cs] = hrj
            sj = (hrj * hrj).astype(bf16)
            s_ref[:, cs] = sj
            f = f + _dot(sj, w2_ref[j])
        r2 = ALPHA * h1 + (1.0 + gate2) * f
        xhat2, rstd2 = _ln_stats(r2)
        h2 = xhat2 * g2_ref[...] + bb2_ref[...]
        diff = h2 - t_ref[...]
        dh2 = diff * (1.0 / D_MODEL)

        def add(row, val):
            acc_ref[row:row + 1, :] += jnp.sum(val, axis=0, keepdims=True)

        add(A_LOSS, diff * diff * (0.5 / D_MODEL))
        add(A_LN2G, dh2 * xhat2)
        add(A_LN2B, dh2)
        dr2 = _ln_bwd(dh2, xhat2, rstd2, g2_ref[...])
        add(A_G2, dr2 * f)
        df = dr2 * (1.0 + gate2)
        add(A_B2, df)
        dfb = df.astype(bf16)
        df_ref[...] = dfb
        du2 = jnp.zeros((tm, D_MODEL), f32)
        for j in range(nj):
            cs = slice(1024 * j, 1024 * (j + 1))
            ds = _dot_nt(dfb, w2_ref[j])
            daj = ds * (2.0 * hr[:, cs])
            db1_ref[:, cs] += jnp.sum(daj, axis=0, keepdims=True)
            dajb = daj.astype(bf16)
            da_ref[:, cs] = dajb
            du2 = du2 + _dot_nt(dajb, w1_ref[j])
        add(A_SC2, du2 * h1)
        add(A_SH2, du2)
        dh1 = ALPHA * dr2 + du2 * (1.0 + sc2)
        add(A_LN1G, dh1 * xhat1)
        add(A_LN1B, dh1)
        dr1_ref[...] = _ln_bwd(dh1, xhat1, rstd1, g1_ref[...])

    v = _full((1, D_MODEL))
    return pl.pallas_call(
        body, name="mlp_fwd_bwd", grid=(L // tm,),
        in_specs=[_rows(tm, D_MODEL), _rows(tm, D_MODEL), _full((8, D_MODEL)), v, v, v, v,
                  _resident((N_CHIPS, D_MODEL, D_MODEL)), _full((1, D_FF)), _resident((N_CHIPS, D_MODEL, D_MODEL)), v],
        out_specs=[_rows(tm, D_MODEL), _rows(tm, D_MODEL), _rows(tm, D_FF), _rows(tm, D_FF), _rows(tm, D_MODEL),
                   _full((16, D_MODEL)), _full((1, D_FF))],
        out_shape=[jax.ShapeDtypeStruct((L, D_MODEL), f32), jax.ShapeDtypeStruct((L, D_MODEL), bf16),
                   jax.ShapeDtypeStruct((L, D_FF), bf16), jax.ShapeDtypeStruct((L, D_FF), bf16),
                   jax.ShapeDtypeStruct((L, D_MODEL), bf16), jax.ShapeDtypeStruct((16, D_MODEL), f32),
                   jax.ShapeDtypeStruct((1, D_FF), f32)],
        scratch_shapes=[pltpu.VMEM((tm, D_FF), f32)],
        compiler_params=_cparams("arbitrary"),
    )(r1, target, mod, ln1_g, ln1_b, ln2_g, ln2_b, w1, b1, w2, b2)


def _wgrad(a, b, name):
    L, M = a.shape
    N = b.shape[1]
    tm = min(M, 512)
    tn = next(t for t in (1024, 768, 512, 256, 128) if N % t == 0)

    def body(a_ref, b_ref, o_ref):
        o_ref[...] = _dot_tn(a_ref[...], b_ref[...]).astype(bf16)

    return pl.pallas_call(
        body, name=name, grid=(M // tm, N // tn),
        in_specs=[pl.BlockSpec((L, tm), lambda i, j: (0, i)), pl.BlockSpec((L, tn), lambda i, j: (0, j))],
        out_specs=pl.BlockSpec((tm, tn), lambda i, j: (i, j)),
        out_shape=jax.ShapeDtypeStruct((M, N), bf16),
        compiler_params=_cparams("parallel", "parallel"),
    )(a, b)


def _wgrad_blob(blob, a, b, name, place_of):
    L, M = a.shape
    N = b.shape[1]

    def body(blob_ref, a_ref, b_ref, o_ref):
        o_ref[0] = _dot_tn(a_ref[...], b_ref[...]).astype(bf16)

    return pl.pallas_call(
        body, name=name, grid=(M // WG_TM, N // D_MODEL),
        in_specs=[pl.BlockSpec(memory_space=pl.ANY), pl.BlockSpec((L, WG_TM), lambda t, n: (0, t)),
                  pl.BlockSpec((L, D_MODEL), lambda t, n: (0, n))],
        out_specs=pl.BlockSpec((1, WG_TM, D_MODEL), lambda t, n: (*place_of(t, n), 0)),
        out_shape=jax.ShapeDtypeStruct(blob.shape, bf16), input_output_aliases={0: 0},
        compiler_params=_cparams("parallel", "parallel"),
    )(blob, a, b)


def _outproj_bwd(dr1, mix, y, z, mod, norm_w, w_out, gb):
    L = dr1.shape[0]
    tm = DENSE_TM
    nt = L // tm

    def body(dr1_ref, mix_ref, y_ref, z_ref, mod_ref, nw_ref, w_ref, gb_ref,
             dy_ref, dz_ref, do_ref, dmix_ref, acc_ref, sib_ref, send_sems, recv_sems):
        start, wait = _to_sibling_job(gb_ref, sib_ref, send_sems, recv_sems)

        @pl.when(pl.program_id(0) == 0)
        def _():
            acc_ref[...] = jnp.zeros_like(acc_ref)
            start()

        @pl.when(pl.program_id(0) == nt - 1)
        def _():
            wait()

        dr1 = dr1_ref[...]
        acc_ref[0:1, :] += jnp.sum(dr1 * mix_ref[...], axis=0, keepdims=True)
        dmix = (dr1 * (1.0 + mod_ref[2:3, :])).astype(bf16)
        dmix_ref[...] = dmix
        dyn = jnp.concatenate([_dot_nt(dmix, w_ref[0]), _dot_nt(dmix, w_ref[1])], axis=1)
        do_ref[...] = jnp.concatenate([_dot_nt(dmix, w_ref[2]), _dot_nt(dmix, w_ref[3])], axis=1).astype(bf16)
        yv, zv = y_ref[...], z_ref[...]
        _, n, rss, sz = _gated_norm(yv, zv, nw_ref[...])
        acc_ref[1:2, :] += jnp.sum(dyn * n, axis=0, keepdims=True)
        dn = dyn * nw_ref[...]
        parts = []
        for g in range(SSD_GROUPS):
            sl = slice(512 * g, 512 * (g + 1))
            dng, ng = dn[:, sl], n[:, sl]
            parts.append(rss[g] * (dng - ng * jnp.mean(dng * ng, axis=-1, keepdims=True)))
        dhg = jnp.concatenate(parts, axis=1)
        dy_ref[...] = dhg * (zv * sz)
        dz_ref[...] = (dhg * yv * (sz * (1.0 + zv * (1.0 - sz)))).astype(bf16)

    return pl.pallas_call(
        body, name="outproj_bwd", grid=(nt,),
        in_specs=[_rows(tm, D_MODEL)] * 4 + [_full((8, D_MODEL)), _full((1, D_MODEL)), _resident((N_CHIPS, 512, D_MODEL)),
                  _ANY_SPEC],
        out_specs=[_rows(tm, D_MODEL)] * 4 + [_full((8, D_MODEL)), _ANY_SPEC],
        out_shape=[jax.ShapeDtypeStruct((L, D_MODEL), f32)] + [jax.ShapeDtypeStruct((L, D_MODEL), bf16)] * 3
        + [jax.ShapeDtypeStruct((8, D_MODEL), f32), jax.ShapeDtypeStruct((N_CHIPS,) + gb.shape[2:], bf16)],
        scratch_shapes=_sems(N_CHIPS),
        compiler_params=_cparams("arbitrary"),
    )(dr1, mix, y, z, mod, norm_w, w_out, gb)


def _attn_bwd(q, kv, do, lse, sinks, pb):
    L = q.shape[0]
    nb = L // CHUNK
    T = CHUNK

    def body(sink_ref, q_ref, kvp_ref, kvc_ref, do_ref, lse_ref, pb_ref, dq_ref, dkv_ref, dsink_ref, chips_ref,
             carry, send_sems, recv_sems):
        n = pl.program_id(0)
        start, wait = _to_chips_job(pb_ref, chips_ref, send_sems, recv_sems)

        @pl.when(n == 0)
        def _():
            carry[...] = jnp.zeros_like(carry)
            dsink_ref[...] = jnp.zeros_like(dsink_ref)
            start()

        @pl.when(n < nb)
        def _():
            first = n == 0
            ext = _kv_halves(kvp_ref[...], kvc_ref[...], first)
            dist, valid = _attn_masks(first)
            lane1 = lax.broadcasted_iota(jnp.int32, (1, 128), 1)
            lse = lse_ref[...]
            acck = [None, None]
            accv = [None, None]
            dsink = jnp.zeros((1, 128), f32)
            dq_blocks = []
            for blk in range(8):
                qb = q_ref[:, 128 * blk:128 * (blk + 1)]
                dob = do_ref[:, 128 * blk:128 * (blk + 1)]
                qt = qb.astype(f32).T.astype(bf16)
                dot_ = dob.astype(f32).T.astype(bf16)
                dq_acc = None
                for half in range(2):
                    h = 2 * blk + half
                    g = h // 8
                    k_ext, v_ext = ext[g][half]
                    s = _dot_nt(qb, k_ext) * ATTN_SCALE - ALIBI_SLOPES[h] * dist
                    lse_h = lse[:, h:h + 1]
                    p = jnp.where(valid, jnp.exp(s - lse_h), 0.0)
                    dp = _dot_nt(dob, v_ext)
                    delta = jnp.sum(p * dp, axis=-1, keepdims=True)
                    ds = (p * (dp - delta) * ATTN_SCALE).astype(bf16)
                    psink = jnp.exp(sink_ref[h] - lse_h)
                    dsink = dsink - jnp.where(lane1 == h, jnp.sum(psink * delta, axis=0, keepdims=True), 0.0)
                    dqh = _dot(ds, k_ext)
                    dq_acc = dqh if dq_acc is None else dq_acc + dqh
                    rows = slice(HEAD_DIM * half, HEAD_DIM * (half + 1))
                    dkh = _dot(qt[rows, :], ds)
                    dvh = _dot(dot_[rows, :], p.astype(bf16))
                    acck[g] = dkh if acck[g] is None else acck[g] + dkh
                    accv[g] = dvh if accv[g] is None else accv[g] + dvh
                dq_blocks.append(dq_acc.astype(bf16))
            dq_ref[...] = jnp.concatenate(dq_blocks, axis=1)
            dsink_ref[...] += dsink
            dkv = jnp.concatenate([jnp.concatenate(acck, axis=0).T, jnp.concatenate(accv, axis=0).T], axis=1)
            dkv_ref[...] = (carry[...] + dkv[0:T, :]).astype(bf16)
            carry[...] = dkv[T:2 * T, :]

        @pl.when(n == nb)
        def _():
            dkv_ref[...] = carry[...].astype(bf16)
            wait()

    cur = lambda i: (jnp.minimum(i, nb - 1), 0)
    return pl.pallas_call(
        body, name="attn_bwd", grid=(nb + 1,),
        in_specs=[pl.BlockSpec(memory_space=pltpu.SMEM), pl.BlockSpec((T, D_MODEL), cur),
                  pl.BlockSpec((T, 256), lambda i: (jnp.maximum(jnp.minimum(i, nb - 1) - 1, 0), 0)),
                  pl.BlockSpec((T, 256), cur), pl.BlockSpec((T, D_MODEL), cur), pl.BlockSpec((T, 128), cur), _ANY_SPEC],
        out_specs=[pl.BlockSpec((T, D_MODEL), cur), pl.BlockSpec((T, 256), lambda i: (jnp.maximum(i - 1, 0), 0)),
                   _full((1, 128)), _ANY_SPEC],
        out_shape=[jax.ShapeDtypeStruct((L, D_MODEL), bf16), jax.ShapeDtypeStruct((L, 256), bf16),
                   jax.ShapeDtypeStruct((1, 128), f32), jax.ShapeDtypeStruct((N_CHIPS - 1,) + pb.shape[1:], bf16)],
        scratch_shapes=[pltpu.VMEM((T, 256), f32)] + _sems(N_CHIPS - 1),
        compiler_params=_cparams("arbitrary"),
    )(sinks, q, kv, kv, do, lse, pb)


def _ssd_bwd(xc, dt_raw, dy, prev_all, dt_bias, a_log, d_skip_x, e_mat, g):
    L = xc.shape[0]
    nc = L // CHUNK
    T = CHUNK
    RG = g.shape[0]

    def body(xc_ref, dt_ref, dy_ref, prev_ref, dtb_ref, al_ref, dsk_ref, e_ref, g_in_ref,
             dxc_ref, ddt_ref, acc_ref, dd_ref, g_ref, dst, dxs_s, send_sems, recv_sems):
        start, wait = _share_job(g_ref, send_sems, recv_sems, RG)

        @pl.when(pl.program_id(0) == 0)
        def _():
            dst[...] = jnp.zeros_like(dst)
            acc_ref[...] = jnp.zeros_like(acc_ref)
            dd_ref[...] = jnp.zeros_like(dd_ref)
            start()

        @pl.when(pl.program_id(0) == nc - 1)
        def _():
            wait()

        lane1 = lax.broadcasted_iota(jnp.int32, (1, 128), 1)
        a_row = jnp.where(lane1 < SSD_HEADS, -jnp.exp(al_ref[...]), 0.0)
        e_mat_v = e_ref[...]
        dt, a, cum, dtx, cumx, r, c = _ssd_chunk_common(dt_ref[...], dtb_ref[...], a_row, e_mat_v)
        cum_t = cum.T
        ex = jnp.exp(cumx)
        last = cumx[T - 1:T, :]
        wx = jnp.exp(last - cumx)
        cdx = jnp.exp(last)
        xs = xc_ref[:, 0:SSD_WIDTH]
        X = xs * dtx
        Xb = X.astype(bf16)
        Xdb = (X * wx).astype(bf16)
        dyv = dy_ref[...]
        prev = prev_ref[0]
        prevb = prev.astype(bf16)
        dnew = dst[...]
        dnewb = dnew.astype(bf16)
        tri = c <= r
        lane = lax.broadcasted_iota(jnp.int32, (T, 128), 1)
        sub = lax.broadcasted_iota(jnp.int32, (128, T), 0)
        lo = lane < HEAD_DIM

        def red(vals, g):
            return _dot_sel_nt(vals, e_mat_v[:, 512 * g:512 * (g + 1)])

        de = jnp.zeros((T, 128), f32)
        dw = jnp.zeros((T, 128), f32)
        ddt_x = jnp.zeros((T, 128), f32)
        dcum_col = jnp.zeros((T, 128), f32)
        dcum_row = jnp.zeros((128, T), f32)
        dprev_parts, dBs, dCs = [], [], []
        for g in range(SSD_GROUPS):
            s5 = slice(512 * g, 512 * (g + 1))
            Bg = xc_ref[:, 1024 + 128 * g:1152 + 128 * g].astype(bf16)
            Cg = xc_ref[:, 1280 + 128 * g:1408 + 128 * g].astype(bf16)
            G = _dot_nt(Cg, Bg)
            Z = _dot(Cg, prevb[:, s5])
            dyg = dyv[:, s5]
            dZb = (dyg * ex[:, s5]).astype(bf16)
            dXd = _dot(Bg, dnewb[:, s5])
            dC = _dot_nt(dZb, prevb[:, s5])
            dB = _dot_nt(Xdb[:, s5], dnewb[:, s5])
            dprev_parts.append(_dot_tn(Cg, dZb) + dnew[:, s5] * cdx[:, s5])
            de = de + red(dyg * Z, g)
            dw = dw + red(dXd * X[:, s5], g)
            dXg = dXd * wx[:, s5]
            dG = jnp.zeros((T, T), f32)
            for j in range(4):
                blk = 4 * g + j
                sl = slice(128 * blk, 128 * (blk + 1))
                Xblk = Xb[:, sl]
                dyblk = dyv[:, sl]
                dyblk_b = dyblk.astype(bf16)
                dxh = []
                for half in range(2):
                    h = 2 * blk + half
                    seg = jnp.minimum(cum[:, h:h + 1] - cum_t[h:h + 1, :], 0.0)
                    Lm = jnp.where(tri, jnp.exp(seg), 0.0)
                    M = G * Lm
                    dyh = jnp.where(lo if half == 0 else ~lo, dyblk, 0.0).astype(bf16)
                    dM = _dot_nt(dyh, Xblk)
                    dG = dG + dM * Lm
                    Q = dM * M
                    dcum_col = dcum_col + jnp.where(lane == h, jnp.sum(Q, axis=1, keepdims=True), 0.0)
                    dcum_row = dcum_row + jnp.where(sub == h, jnp.sum(Q, axis=0, keepdims=True), 0.0)
                    dxh.append(_dot_tn(M.astype(bf16), dyblk_b))
                dXblk = dXg[:, 128 * j:128 * (j + 1)] + jnp.where(lo, dxh[0], dxh[1])
                xsb = xs[:, sl]
                dxs_s[:, sl] = dXblk * dtx[:, sl] + dsk_ref[:, sl] * dyblk
                ddt_x = ddt_x + _dot_sel_nt(dXblk * xsb, e_mat_v[:, sl])
                dd_ref[:, sl] += jnp.sum(dyblk * xsb, axis=0, keepdims=True)
            dGb = dG.astype(bf16)
            dCs.append(dC + _dot(dGb, Bg))
            dBs.append(dB + _dot_tn(dGb, Cg))
        e16 = jnp.exp(cum)
        cum_last = cum[T - 1:T, :]
        w16 = jnp.exp(cum_last - cum)
        dcd = jnp.sum(dnew * prev, axis=0, keepdims=True)
        dcd16 = red(dcd[:, 0:512], 0) + red(dcd[:, 512:1024], 1)
        dww = dw * w16
        extra = jnp.sum(dww, axis=0, keepdims=True) + dcd16 * jnp.exp(cum_last)
        rowi = lax.broadcasted_iota(jnp.int32, (T, 128), 0)
        dcum = dcum_col - dcum_row.T + de * e16 - dww + jnp.where(rowi == T - 1, extra, 0.0)
        da = _sel_dot((c >= r).astype(bf16), dcum)
        ddt = ddt_x + da * a_row
        acc_ref[0:1, :] += jnp.sum(da * dt, axis=0, keepdims=True)
        ddt_raw = jnp.where(lane < SSD_HEADS, ddt * _sigmoid(dt_ref[...] + dtb_ref[...]), 0.0)
        ddt_ref[...] = ddt_raw
        acc_ref[1:2, :] += jnp.sum(ddt_raw, axis=0, keepdims=True)
        dxc_ref[:, 0:SSD_WIDTH] = dxs_s[...]
        dxc_ref[:, 1024:1280] = jnp.concatenate(dBs, axis=1)
        dxc_ref[:, 1280:1536] = jnp.concatenate(dCs, axis=1)
        dst[...] = jnp.concatenate(dprev_parts, axis=1)

    rev = lambda i: (nc - 1 - i, 0)
    return pl.pallas_call(
        body, name="ssd_bwd", grid=(nc,),
        in_specs=[pl.BlockSpec((T, CONV_DIM), rev), pl.BlockSpec((T, 128), rev), pl.BlockSpec((T, SSD_WIDTH), rev),
                  pl.BlockSpec((1, SSD_STATE, SSD_WIDTH), lambda i: (nc - 1 - i, 0, 0)),
                  _full((1, 128)), _full((1, 128)), _full((1, SSD_WIDTH)), _full((128, SSD_WIDTH)), _ANY_SPEC],
        out_specs=[pl.BlockSpec((T, CONV_DIM), rev), pl.BlockSpec((T, 128), rev), _full((8, 128)),
                   _full((1, SSD_WIDTH)), _ANY_SPEC],
        out_shape=[jax.ShapeDtypeStruct((L, CONV_DIM), f32), jax.ShapeDtypeStruct((L, 128), f32),
                   jax.ShapeDtypeStruct((8, 128), f32), jax.ShapeDtypeStruct((1, SSD_WIDTH), f32),
                   jax.ShapeDtypeStruct(g.shape, f32)],
        input_output_aliases={8: 4},
        scratch_shapes=[pltpu.VMEM((SSD_STATE, SSD_WIDTH), f32), pltpu.VMEM((T, SSD_WIDTH), f32)] + _sems(1),
        compiler_params=_cparams("arbitrary"),
    )(xc, dt_raw, dy, prev_all, dt_bias, a_log, d_skip_x, e_mat, g)


def _conv_bwd(dxc, xr, conv_w, conv_b):
    L = dxc.shape[0]
    tm = 256
    nt = L // tm

    def body(dxc_ref, xr_ref, xh_ref, cw_ref, cb_ref, dxr_ref, acc_ref, carry, buf, buf2):
        i = pl.program_id(0)

        @pl.when(i == 0)
        def _():
            carry[...] = jnp.zeros_like(carry)
            acc_ref[...] = jnp.zeros_like(acc_ref)

        buf[0:8, :] = jnp.where(i == nt - 1, 0.0, xh_ref[...])
        buf[8:8 + tm, :] = xr_ref[...]
        pre = cb_ref[...] + cw_ref[0:1, :] * buf[5:5 + tm, :]
        for k in range(1, CONV_K):
            pre = pre + cw_ref[k:k + 1, :] * buf[5 + k:5 + k + tm, :]
        sg = _sigmoid(pre)
        dpre = dxc_ref[...] * (sg * (1.0 + pre * (1.0 - sg)))
        acc_ref[4:5, :] += jnp.sum(dpre, axis=0, keepdims=True)
        for k in range(CONV_K):
            acc_ref[k:k + 1, :] += jnp.sum(dpre * buf[5 + k:5 + k + tm, :], axis=0, keepdims=True)
        buf2[0:tm, :] = dpre
        buf2[tm:tm + 8, :] = carry[...]
        du = cw_ref[0:1, :] * buf2[3:3 + tm, :]
        for k in range(1, CONV_K):
            du = du + cw_ref[k:k + 1, :] * buf2[3 - k:3 - k + tm, :]
        dxr_ref[...] = du.astype(bf16)
        carry[...] = dpre[0:8, :]

    rev = lambda i: (nt - 1 - i, 0)
    return pl.pallas_call(
        body, name="conv_bwd", grid=(nt,),
        in_specs=[pl.BlockSpec((tm, CONV_DIM), rev), pl.BlockSpec((tm, CONV_DIM), rev),
                  pl.BlockSpec((8, CONV_DIM), lambda i: (jnp.maximum((nt - 1 - i) * (tm // 8) - 1, 0), 0)),
                  _full((CONV_K, CONV_DIM)), _full((1, CONV_DIM))],
        out_specs=[pl.BlockSpec((tm, CONV_DIM), rev), _full((8, CONV_DIM))],
        out_shape=[jax.ShapeDtypeStruct((L, CONV_DIM), bf16), jax.ShapeDtypeStruct((8, CONV_DIM), f32)],
        scratch_shapes=[pltpu.VMEM((8, CONV_DIM), f32), pltpu.VMEM((tm + 8, CONV_DIM), f32),
                        pltpu.VMEM((tm + 8, CONV_DIM), f32)],
        compiler_params=_cparams("arbitrary"),
    )(dxc, xr, xr, conv_w, conv_b)


def _inproj_bwd(dz, dxr, dq, dkv, ddt, dr1, x, mod, ln_g, ln_b, w_in, pb):
    L = x.shape[0]
    tm = DENSE_TM
    nt = L // tm

    def body(dz_ref, dxr_ref, dq_ref, dkv_ref, ddt_ref, dr1_ref, x_ref, mod_ref, g_ref, b_ref, w_ref, pb_ref,
             dx_ref, acc_ref, chips_ref, send_sems, recv_sems):
        start, wait = _to_chips_job(pb_ref, chips_ref, send_sems, recv_sems)

        @pl.when(pl.program_id(0) == 0)
        def _():
            acc_ref[...] = jnp.zeros_like(acc_ref)
            start()

        @pl.when(pl.program_id(0) == nt - 1)
        def _():
            wait()

        du1 = (_dot(dz_ref[...], w_ref[W_Z, :]) + _dot(dxr_ref[...], w_ref[W_XBC, :])
               + _dot(dq_ref[...], w_ref[W_Q, :]) + _dot(dkv_ref[...], w_ref[W_KV, :])
               + _dot(ddt_ref[...].astype(bf16), w_ref[W_DT, :]))
        xhat, rstd = _ln_stats(x_ref[...])
        h0 = xhat * g_ref[...] + b_ref[...]
        acc_ref[0:1, :] += jnp.sum(du1 * h0, axis=0, keepdims=True)
        acc_ref[1:2, :] += jnp.sum(du1, axis=0, keepdims=True)
        dh0 = du1 * (1.0 + mod_ref[1:2, :]) + ALPHA * dr1_ref[...]
        acc_ref[2:3, :] += jnp.sum(dh0 * xhat, axis=0, keepdims=True)
        acc_ref[3:4, :] += jnp.sum(dh0, axis=0, keepdims=True)
        dx_ref[...] = _ln_bwd(dh0, xhat, rstd, g_ref[...])

    v = _full((1, D_MODEL))
    return pl.pallas_call(
        body, name="inproj_bwd", grid=(nt,),
        in_specs=[_rows(tm, D_MODEL), _rows(tm, CONV_DIM), _rows(tm, D_MODEL), _rows(tm, 256), _rows(tm, 128),
                  _rows(tm, D_MODEL), _rows(tm, D_MODEL), _full((8, D_MODEL)), v, v, _resident((PROJ_WIDTH, D_MODEL)),
                  _ANY_SPEC],
        out_specs=[_rows(tm, D_MODEL), _full((8, D_MODEL)), _ANY_SPEC],
        out_shape=[jax.ShapeDtypeStruct((L, D_MODEL), f32), jax.ShapeDtypeStruct((8, D_MODEL), f32),
                   jax.ShapeDtypeStruct((N_CHIPS - 1,) + pb.shape[1:], bf16)],
        scratch_shapes=_sems(N_CHIPS - 1),
        compiler_params=_cparams("arbitrary"),
    )(dz, dxr, dq, dkv, ddt, dr1, x, mod, ln_g, ln_b, w_in, pb)


def _adamw_math(w, g, m, v):
    m = ADAM_B1 * m + (1.0 - ADAM_B1) * g
    v = ADAM_B2 * v + (1.0 - ADAM_B2) * (g * g)
    m_hat = m / (1.0 - ADAM_B1 ** ADAM_STEP)
    v_hat = v / (1.0 - ADAM_B2 ** ADAM_STEP)
    delta = -ADAM_LR * (m_hat / (jnp.sqrt(v_hat) + ADAM_EPS) + ADAM_WD * w)
    return delta, m, v


def _adamw(w, g, m, v, name):
    R, C = w.shape

    def body(w_ref, g_ref, m_ref, v_ref, d_ref, m2_ref, v2_ref):
        d_ref[...], m2_ref[...], v2_ref[...] = _adamw_math(w_ref[...], g_ref[...], m_ref[...], v_ref[...])

    cap = max(8, ADAMW_BLOCK_ELEMS // C)
    tr = R if R <= cap else next(t for t in range(cap - cap % 8, 7, -8) if R % t == 0)
    spec = pl.BlockSpec((tr, C), lambda i: (i, 0))
    return pl.pallas_call(
        body, name=name, grid=(R // tr,), in_specs=[spec] * 4, out_specs=[spec] * 3,
        out_shape=[jax.ShapeDtypeStruct((R, C), f32)] * 3, compiler_params=_cparams("parallel"),
    )(w, g, m, v)


def _adamw_rows(w, g, m, v):
    R, _, C = w.shape
    tr = R // 4

    def body(w_ref, g_ref, m_ref, v_ref, d_ref, m2_ref, v2_ref):
        d_ref[...], m2_ref[...], v2_ref[...] = _adamw_math(w_ref[...], g_ref[...], m_ref[...], v_ref[...])

    spec = pl.BlockSpec((tr, 1, C), lambda i: (i, 0, 0))
    return pl.pallas_call(
        body, name="adamw_w_in", grid=(R // tr,), in_specs=[spec] * 4, out_specs=[spec] * 3,
        out_shape=[jax.ShapeDtypeStruct((R, 1, C), f32)] * 3, compiler_params=_cparams("parallel"),
    )(w, g, m, v)


ADA_COLS = 6 * D_MODEL // N_CHIPS
ADA_TN = 512


def _ada_fwd(c_all, ada_w, ada_b):
    def body(c_ref, w_ref, b_ref, o_ref):
        cv = c_ref[...]
        o_ref[...] = _dot_exact(cv * _sigmoid(cv), w_ref[...]) + b_ref[...]

    return pl.pallas_call(
        body, name="ada_fwd", grid=(ADA_COLS // ADA_TN,),
        in_specs=[_full((N_DEV, D_MODEL)), pl.BlockSpec((D_MODEL, ADA_TN), lambda j: (0, j)),
                  pl.BlockSpec((1, ADA_TN), lambda j: (0, j))],
        out_specs=pl.BlockSpec((N_DEV, ADA_TN), lambda j: (0, j)),
        out_shape=jax.ShapeDtypeStruct((N_DEV, ADA_COLS), f32), compiler_params=_cparams("parallel"),
    )(c_all, ada_w, ada_b)


def _ada_bwd(c_all, dmod, w, m, v):
    def body(c_ref, d_ref, w_ref, m_ref, v_ref, g_ref, dl_ref, m2_ref, v2_ref):
        cv = c_ref[...]
        g = lax.dot_general(cv * _sigmoid(cv), d_ref[...], (((0,), (0,)), ((), ())), preferred_element_type=f32,
                            precision=lax.Precision.HIGHEST)
        g_ref[...] = g
        dl_ref[...], m2_ref[...], v2_ref[...] = _adamw_math(w_ref[...], g, m_ref[...], v_ref[...])

    wspec = pl.BlockSpec((D_MODEL, ADA_TN), lambda j: (0, j))
    return pl.pallas_call(
        body, name="ada_bwd", grid=(ADA_COLS // ADA_TN,),
        in_specs=[_full((N_DEV, D_MODEL)), pl.BlockSpec((N_DEV, ADA_TN), lambda j: (0, j)), wspec, wspec, wspec],
        out_specs=[wspec] * 4, out_shape=[jax.ShapeDtypeStruct((D_MODEL, ADA_COLS), f32)] * 4,
        compiler_params=_cparams("parallel"),
    )(c_all, dmod, w, m, v)


SMALL_SLOTS = (("ada_b", 6144), ("ln_in_g", 1024), ("ln_in_b", 1024), ("conv_b", 1536), ("dt_bias", 128), ("a_log", 128),
               ("d_skip", 128), ("ssd_norm_w", 1024), ("attn_sinks", 128), ("ln1_g", 1024), ("ln1_b", 1024),
               ("b_ff1", 4096), ("b_ff2", 1024), ("ln2_g", 1024), ("ln2_b", 1024), ("conv_w", 6144), ("loss", 1024))
SMALL_N = sum(n for _, n in SMALL_SLOTS)
SMALL_OFF = {name: sum(n for _, n in SMALL_SLOTS[:i]) for i, (name, _) in enumerate(SMALL_SLOTS)}
SMALL_PARAMS = tuple(name for name, _ in SMALL_SLOTS[:15])
assert SMALL_N % 1024 == 0


def _small_pack(acc_in, acc_out, acc_mlp, db1, acc_conv, acc_ssd, dd_x, dsink, alog, e_mat):
    def body(in_ref, out_ref, mlp_ref, db1_ref, conv_ref, ssd_ref, dd_ref, sink_ref, al_ref, e_ref, o_ref):
        def put(name, val, at=0):
            off = SMALL_OFF[name] + at
            o_ref[:, off:off + val.shape[1]] = val

        for k, row in enumerate((in_ref[1:2, :], in_ref[0:1, :], out_ref[0:1, :], mlp_ref[A_SH2:A_SH2 + 1, :],
                                 mlp_ref[A_SC2:A_SC2 + 1, :], mlp_ref[A_G2:A_G2 + 1, :])):
            put("ada_b", row, D_MODEL * k)
        put("ln_in_g", in_ref[2:3, :])
        put("ln_in_b", in_ref[3:4, :])
        put("conv_b", conv_ref[4:5, :])
        put("dt_bias", ssd_ref[1:2, :])
        put("a_log", ssd_ref[0:1, :] * (-jnp.exp(al_ref[...])))
        put("d_skip", _dot_sel_nt(jnp.broadcast_to(dd_ref[...], (8, SSD_WIDTH)), e_ref[...])[0:1, :])
        put("ssd_norm_w", out_ref[1:2, :])
        put("attn_sinks", sink_ref[...])
        put("ln1_g", mlp_ref[A_LN1G:A_LN1G + 1, :])
        put("ln1_b", mlp_ref[A_LN1B:A_LN1B + 1, :])
        put("b_ff1", db1_ref[...])
        put("b_ff2", mlp_ref[A_B2:A_B2 + 1, :])
        put("ln2_g", mlp_ref[A_LN2G:A_LN2G + 1, :])
        put("ln2_b", mlp_ref[A_LN2B:A_LN2B + 1, :])
        for k in range(CONV_K):
            put("conv_w", conv_ref[k:k + 1, :], CONV_DIM * k)
        put("loss", mlp_ref[A_LOSS:A_LOSS + 1, :])

    return pl.pallas_call(body, name="small_pack", out_shape=jax.ShapeDtypeStruct((1, SMALL_N), f32),
                          compiler_params=_cparams())(acc_in, acc_out, acc_mlp, db1, acc_conv, acc_ssd, dd_x, dsink, alog, e_mat)


def _small_update(gathered, params, moms, vels):
    k = len(SMALL_PARAMS)

    def body(g_ref, *refs):
        w_refs, m_refs, v_refs, outs = refs[:k], refs[k:2 * k], refs[2 * k:3 * k], refs[3 * k:]

        def total(name, n):
            off = SMALL_OFF[name]
            g = g_ref[0:1, off:off + n]
            for i in range(1, N_DEV):
                g = g + g_ref[i:i + 1, off:off + n]
            return g

        for j, name in enumerate(SMALL_PARAMS):
            n = w_refs[j].shape[1]
            g = total(name, max(n, 128))[:, :n]
            outs[4 * j][...] = g
            outs[4 * j + 1][...], outs[4 * j + 2][...], outs[4 * j + 3][...] = _adamw_math(
                w_refs[j][...], g, m_refs[j][...], v_refs[j][...])
        outs[4 * k][...] = total("conv_w", CONV_K * CONV_DIM)
        outs[4 * k + 1][...] = total("loss", D_MODEL)

    shapes = [jax.ShapeDtypeStruct(p.shape, f32) for p in params for _ in range(4)]
    shapes += [jax.ShapeDtypeStruct((1, CONV_K * CONV_DIM), f32), jax.ShapeDtypeStruct((1, D_MODEL), f32)]
    return pl.pallas_call(body, name="small_update", out_shape=shapes,
                          compiler_params=_cparams())(gathered, *params, *moms, *vels)


def _place():
    return lax.axis_index("x"), lax.axis_index("y"), lax.axis_index("c")


def _flip(x, y, c, m):
    return (1 - x if m & 4 else x, 1 - y if m & 2 else y, 1 - c if m & 1 else c)


_VMEM_SPEC = pl.BlockSpec(memory_space=pltpu.VMEM)
_ANY_SPEC = pl.BlockSpec(memory_space=pl.ANY)


def _allgather8(v, name):
    n = v.shape[1]

    def body(v_ref, out_ref, send_sems, recv_sems, local_sem):
        x, y, c = _place()

        def rows(px, py, pc):
            return out_ref.at[pl.ds(pl.multiple_of((4 * px + 2 * py + pc) * 8, 8), 8), :]

        def copy(m, src, dst, to):
            return pltpu.make_async_remote_copy(src_ref=src, dst_ref=dst, send_sem=send_sems.at[m - 1],
                                                recv_sem=recv_sems.at[m - 1], device_id=to, device_id_type=MESH)

        mine = pltpu.make_async_copy(v_ref, rows(x, y, c), local_sem)
        mine.start()
        sends = [copy(m, v_ref, rows(x, y, c), _flip(x, y, c, m)) for m in range(1, N_DEV)]
        for cp in sends:
            cp.start()
        for m in range(1, N_DEV):
            peer = _flip(x, y, c, m)
            copy(m, v_ref, rows(*peer), peer).wait_recv()
        for cp in sends:
            cp.wait_send()
        mine.wait()

    return pl.pallas_call(
        body, name=name, out_shape=jax.ShapeDtypeStruct((8 * N_DEV, n), f32), in_specs=[_VMEM_SPEC],
        out_specs=_VMEM_SPEC,
        scratch_shapes=[pltpu.SemaphoreType.DMA((N_DEV - 1,)), pltpu.SemaphoreType.DMA((N_DEV - 1,)),
                        pltpu.SemaphoreType.DMA],
    )(v)


def _mod_exchange(mod_all):
    def body(src_ref, out_ref, send_sems, recv_sems, local_sem):
        x, y, c = _place()

        def copy(m, src, dst, to):
            return pltpu.make_async_remote_copy(src_ref=src, dst_ref=dst, send_sem=send_sems.at[m - 1],
                                                recv_sem=recv_sems.at[m - 1], device_id=to, device_id_type=MESH)

        mine = pltpu.make_async_copy(src_ref.at[4 * x + 2 * y + c], out_ref.at[2 * x + y], local_sem)
        mine.start()
        sends = []
        for m in range(1, N_CHIPS):
            px, py, pc = _flip(x, y, c, 2 * m)
            sends.append(copy(m, src_ref.at[4 * px + 2 * py + pc], out_ref.at[2 * x + y], (px, py, pc)))
        for cp in sends:
            cp.start()
        for m in range(1, N_CHIPS):
            px, py, pc = _flip(x, y, c, 2 * m)
            copy(m, src_ref.at[0], out_ref.at[2 * px + py], (px, py, pc)).wait_recv()
        for cp in sends:
            cp.wait_send()
        mine.wait()

    return pl.pallas_call(
        body, name="mod_exchange", out_shape=jax.ShapeDtypeStruct((N_CHIPS, 12, 128), f32), in_specs=[_VMEM_SPEC],
        out_specs=_VMEM_SPEC,
        scratch_shapes=[pltpu.SemaphoreType.DMA((N_CHIPS - 1,)), pltpu.SemaphoreType.DMA((N_CHIPS - 1,)),
                        pltpu.SemaphoreType.DMA],
    )(mod_all)


def _remote(src, dst, send_sems, recv_sems, k, to):
    return pltpu.make_async_remote_copy(src_ref=src, dst_ref=dst, send_sem=send_sems.at[k], recv_sem=recv_sems.at[k],
                                        device_id=to, device_id_type=MESH)


def _gather_job(blob_ref, out_ref, send_sems, recv_sems, R):
    x, y, c = _place()
    sib = (x, y, 1 - c)
    hr = R // 2

    def half(px, py, pc):
        return out_ref.at[2 * px + py, pl.ds(pl.multiple_of(pc * hr, 16), hr), :]

    my_half = blob_ref.at[pl.ds(pl.multiple_of(c * hr, 16), hr), :]

    def first():
        return [_remote(my_half, half(x, y, c), send_sems, recv_sems, m - 1, _flip(x, y, c, 2 * m))
                for m in range(1, N_CHIPS)]

    def start():
        for cp in first():
            cp.start()

    def finish():
        passed = []
        for m in range(1, N_CHIPS):
            px, py, pc = _flip(x, y, c, 2 * m)
            _remote(my_half, half(px, py, pc), send_sems, recv_sems, m - 1, (px, py, pc)).wait_recv()
            fwd = _remote(half(px, py, pc), half(px, py, pc), send_sems, recv_sems, 2 + m, sib)
            fwd.start()
            passed.append(fwd)
        for m in range(1, N_CHIPS):
            px, py, pc = _flip(x, y, c, 2 * m)
            _remote(my_half, half(px, py, 1 - pc), send_sems, recv_sems, 2 + m, sib).wait_recv()
        for cp in first() + passed:
            cp.wait_send()

    return start, finish


def _gather_weights(blob):
    R = blob.shape[0]

    def body(blob_ref, out_ref, send_sems, recv_sems, local_sem, stage):
        x, y, c = _place()
        start, finish = _gather_job(blob_ref, out_ref, send_sems, recv_sems, R)
        start()
        for k in range(R // STAGE_ROWS):
            rows = pl.ds(STAGE_ROWS * k, STAGE_ROWS)
            cin = pltpu.make_async_copy(blob_ref.at[rows, :], stage, local_sem)
            cin.start()
            cin.wait()
            cout = pltpu.make_async_copy(stage, out_ref.at[2 * x + y, rows, :], local_sem)
            cout.start()
            cout.wait()
        finish()

    return pl.pallas_call(
        body, name="gather_weights", out_shape=jax.ShapeDtypeStruct((N_CHIPS, R, D_MODEL), bf16),
        in_specs=[_ANY_SPEC], out_specs=_ANY_SPEC,
        scratch_shapes=[pltpu.SemaphoreType.DMA((6,)), pltpu.SemaphoreType.DMA((6,)), pltpu.SemaphoreType.DMA,
                        pltpu.VMEM((STAGE_ROWS, D_MODEL), bf16)],
    )(blob)


def _to_sibling_job(g_ref, out_ref, send_sems, recv_sems):
    x, y, c = _place()

    def cps():
        return [_remote(g_ref.at[j, 1 - c], out_ref.at[j], send_sems, recv_sems, j, (x, y, 1 - c)) for j in range(N_CHIPS)]

    def start():
        for cp in cps():
            cp.start()

    def wait():
        for cp in cps():
            cp.wait()

    return start, wait


def _to_chips_job(p_ref, out_ref, send_sems, recv_sems):
    x, y, c = _place()

    def cps():
        out = []
        for m in range(1, N_CHIPS):
            px, py, pc = _flip(x, y, c, 2 * m)
            out.append(_remote(p_ref.at[2 * px + py], out_ref.at[m - 1], send_sems, recv_sems, m - 1, (px, py, pc)))
        return out

    def start():
        for cp in cps():
            cp.start()

    def wait():
        for cp in cps():
            cp.wait()

    return start, wait


def _share_job(g_ref, send_sems, recv_sems, R):
    x, y, c = _place()

    def rows(pc):
        return g_ref.at[pl.ds(pl.multiple_of(pc * (R // 2), 8), R // 2), :]

    def start():
        _remote(rows(c), rows(c), send_sems, recv_sems, 0, (x, y, 1 - c)).start()

    def wait():
        _remote(rows(c), rows(1 - c), send_sems, recv_sems, 0, (x, y, 1 - c)).wait_recv()
        _remote(rows(c), rows(c), send_sems, recv_sems, 0, (x, y, 1 - c)).wait_send()

    return start, wait


def _sems(n):
    return [pltpu.SemaphoreType.DMA((n,)), pltpu.SemaphoreType.DMA((n,))]


def _rs_to_sibling(gb):
    def body(g_ref, out_ref, send_sems, recv_sems):
        start, wait = _to_sibling_job(g_ref, out_ref, send_sems, recv_sems)
        start()
        wait()

    return pl.pallas_call(
        body, name="rs_to_sibling", out_shape=jax.ShapeDtypeStruct((N_CHIPS,) + gb.shape[2:], bf16),
        in_specs=[_ANY_SPEC], out_specs=_ANY_SPEC, scratch_shapes=_sems(N_CHIPS),
    )(gb)


def _rs_share(g):
    R = g.shape[0]

    def body(g_ref, out_ref, send_sems, recv_sems):
        start, wait = _share_job(out_ref, send_sems, recv_sems, R)
        start()
        wait()

    return pl.pallas_call(
        body, name="rs_share", out_shape=jax.ShapeDtypeStruct(g.shape, f32), in_specs=[_ANY_SPEC],
        out_specs=_ANY_SPEC, input_output_aliases={0: 0}, scratch_shapes=_sems(1),
    )(g)


RS_TR = 256


def _rs_sum_pair(place, gb, recv, name):
    hr = gb.shape[2]

    def body(pl_ref, g_ref, r_ref, o_ref):
        o_ref[0] = (g_ref[0, 0].astype(f32) + r_ref[0].astype(f32)).astype(bf16)

    return pl.pallas_call(
        body, name=name,
        grid_spec=pltpu.PrefetchScalarGridSpec(
            num_scalar_prefetch=1, grid=(N_CHIPS, hr // RS_TR),
            in_specs=[pl.BlockSpec((1, 1, RS_TR, D_MODEL), lambda j, i, p: (j, p[0], i, 0)),
                      pl.BlockSpec((1, RS_TR, D_MODEL), lambda j, i, p: (j, i, 0))],
            out_specs=pl.BlockSpec((1, RS_TR, D_MODEL), lambda j, i, p: (j, i, 0))),
        out_shape=jax.ShapeDtypeStruct((N_CHIPS, hr, D_MODEL), bf16),
        compiler_params=_cparams("parallel", "parallel"),
    )(place, gb, recv)


def _rs_sum_chips(place, gb, recv_sib, recv_chips, name):
    hr = gb.shape[2]
    nt = hr // RS_TR

    def body(pl_ref, g_ref, r1_ref, r2_ref, o_ref):
        acc = g_ref[0, 0].astype(f32) + r1_ref[0].astype(f32)
        for k in range(N_CHIPS - 1):
            acc = acc + r2_ref[k].astype(f32)
        o_ref[...] = acc

    return pl.pallas_call(
        body, name=name,
        grid_spec=pltpu.PrefetchScalarGridSpec(
            num_scalar_prefetch=1, grid=(nt,),
            in_specs=[pl.BlockSpec((1, 1, RS_TR, D_MODEL), lambda i, p: (p[1], p[0], i, 0)),
                      pl.BlockSpec((1, RS_TR, D_MODEL), lambda i, p: (p[1], i, 0)),
                      pl.BlockSpec((N_CHIPS - 1, RS_TR, D_MODEL), lambda i, p: (0, i, 0))],
            out_specs=pl.BlockSpec((RS_TR, D_MODEL), lambda i, p: (p[0] * nt + i, 0))),
        out_shape=jax.ShapeDtypeStruct((2 * hr, D_MODEL), f32),
        compiler_params=_cparams("parallel"),
    )(place, gb, recv_sib, recv_chips)


def _pad128(v):
    v = v.reshape(1, -1)
    return jnp.pad(v, ((0, 0), (0, 128 - v.shape[1])))


def _row(v):
    return v.reshape(1, -1)


W_COLS = PROJ_WIDTH // N_CHIPS


def _g_in_blocks(gz, gxbc, gdt, gq, gkv):
    g = jnp.concatenate([gz, gxbc, gdt[:W_DT_ROWS], gq, gkv], axis=0)
    return jnp.pad(g.reshape(N_CHIPS, W_COLS, D_MODEL), ((0, 0), (0, D_MODEL - W_COLS), (0, 0)))


def kernel(x, c, ln_in_g, ln_in_b, ada_w, ada_b, w_in, conv_w, conv_b, dt_bias, a_log, d_skip, ssd_norm_w, attn_sinks, w_out, ln1_g, ln1_b, w_ff1, b_ff1, w_ff2, b_ff2, ln2_g, ln2_b, loss_target, m_ln_in_g, m_ln_in_b, m_ada_w, m_ada_b, m_w_in, m_conv_w, m_conv_b, m_dt_bias, m_a_log, m_d_skip, m_ssd_norm_w, m_attn_sinks, m_w_out, m_ln1_g, m_ln1_b, m_w_ff1, m_b_ff1, m_w_ff2, m_b_ff2, m_ln2_g, m_ln2_b, v_ln_in_g, v_ln_in_b, v_ada_w, v_ada_b, v_w_in, v_conv_w, v_conv_b, v_dt_bias, v_a_log, v_d_skip, v_ssd_norm_w, v_attn_sinks, v_w_out, v_ln1_g, v_ln1_b, v_w_ff1, v_b_ff1, v_w_ff2, v_b_ff2, v_ln2_g, v_ln2_b):
    xi, yi, ci = _place()
    chip = 2 * xi + yi
    place = jnp.stack([ci, chip]).astype(jnp.int32)
    x2, tgt = x[0], loss_target[0]
    w_shard_cols = PROJ_WIDTH // N_CHIPS

    cond = jnp.concatenate([c.reshape(-1), conv_w.reshape(-1), jnp.zeros((512,), f32)]).reshape(8, 384)
    cond_all = _allgather8(cond, "gather_cond").reshape(N_DEV, 3072)
    c_all = cond_all[:, :D_MODEL]
    conv_w_full = jnp.concatenate([cond_all[2 * j, D_MODEL:D_MODEL + 1536].reshape(CONV_K, 384) for j in range(N_CHIPS)], axis=1)

    ada_b_mine = lax.dynamic_slice(ada_b, (0, chip * ADA_COLS), (1, ADA_COLS))
    mod_all = _ada_fwd(c_all, ada_w[0], ada_b_mine)
    mod_mine = _mod_exchange(mod_all.reshape(N_DEV, 12, 128)).reshape(6, D_MODEL)
    mod = jnp.concatenate([mod_mine, jnp.zeros((2, D_MODEL), f32)], axis=0)

    def as_rows(a):
        return jnp.transpose(a, (2, 0, 1))

    def from_rows(a):
        return jnp.transpose(a, (1, 2, 0))

    w_in_t = w_in[0].T

    wall_in = _gather_weights(jnp.pad(w_in_t, ((0, D_MODEL - w_shard_cols), (0, 0))).astype(bf16))
    w_in_f = wall_in[:, :W_COLS].reshape(PROJ_WIDTH, D_MODEL)
    b_ff1w, b_ff2w, b_outw = w_ff1[0].astype(bf16), w_ff2[0].astype(bf16), w_out[0].astype(bf16)

    def with_mine(wall, mine):
        return lax.dynamic_update_slice(wall, mine[None], (chip, 0, 0))

    e_mat = _head_expand()
    dsk_x = jnp.repeat(d_skip[0], HEAD_DIM).reshape(1, SSD_WIDTH)
    dtb, alog = _pad128(dt_bias), _pad128(a_log)
    sinks = attn_sinks[0]
    lng, lnb = _row(ln_in_g), _row(ln_in_b)
    u1, z, xr, xc, q, kv, dtr, wall_ff1 = _inproj_fwd(x2, mod, lng, lnb, w_in_f, conv_w_full, conv_b, b_ff1w)
    y, prev_all, wall_ff2 = _ssd_fwd(xc, dtr, dtb, alog, dsk_x, e_mat, b_ff2w)
    o, lse, wall_out = _attn_fwd(q, kv, sinks, b_outw)
    wall_ff1, wall_ff2, wall_out = with_mine(wall_ff1, b_ff1w), with_mine(wall_ff2, b_ff2w), with_mine(wall_out, b_outw)
    yn, mix, r1 = _outproj_fwd(y, z, o, x2, mod, lng, lnb, ssd_norm_w, wall_out)

    dr1, u2, s_act, da, df, acc_mlp, db1 = _mlp_fwd_bwd(r1, tgt, mod, ln1_g, ln1_b, ln2_g, ln2_b, wall_ff1, b_ff1, wall_ff2,
                                                        b_ff2)
    ga = jnp.zeros((N_CHIPS, GA_ROWS, D_MODEL), bf16)
    ga = _wgrad_blob(ga, u2, da, "wgrad_ff1", lambda t, n: (n, t))
    ga = _wgrad_blob(ga, s_act, df, "wgrad_ff2", lambda t, n: (t // 2, 2 + t % 2))
    ga = ga.reshape(N_CHIPS, 2, GA_ROWS // 2, D_MODEL)
    dy, dz, do, dmix, acc_out, a_sib = _outproj_bwd(dr1, mix, y, z, mod, ssd_norm_w, wall_out, ga)
    gb = jnp.zeros((N_CHIPS, GB_ROWS, D_MODEL), bf16)
    gb = _wgrad_blob(gb, yn, dmix, "wgrad_out_y", lambda t, n: (t, 2))
    gb = _wgrad_blob(gb, o, dmix, "wgrad_out_o", lambda t, n: (2 + t, 2))
    a_pair = _rs_sum_pair(place, ga, a_sib, "rs_sum_pair_a")
    dq, dkv, dsink, a_chips = _attn_bwd(q, kv, do, lse, sinks, a_pair)
    g_a = _rs_sum_chips(place, ga, a_sib, a_chips, "rs_sum_chips_a")
    dxc, ddt, acc_ssd, dd_x, g_a = _ssd_bwd(xc, dtr, dy, prev_all, dtb, alog, dsk_x, e_mat, g_a)
    dxr, acc_conv = _conv_bwd(dxc, xr, conv_w_full, conv_b)
    g_in_c = _g_in_blocks(_wgrad(dz, u1, "wgrad_in_z"), _wgrad(dxr, u1, "wgrad_in_xbc"),
                          _wgrad(ddt.astype(bf16), u1, "wgrad_in_dt"), _wgrad(dq, u1, "wgrad_in_q"),
                          _wgrad(dkv, u1, "wgrad_in_kv"))
    gb = lax.dynamic_update_slice(gb, g_in_c, (0, 0, 0)).reshape(N_CHIPS, 2, GB_ROWS // 2, D_MODEL)
    b_sib = _rs_to_sibling(gb)
    b_pair = _rs_sum_pair(place, gb, b_sib, "rs_sum_pair_b")
    grad_x, acc_in, b_chips = _inproj_bwd(dz, dxr, dq, dkv, ddt, dr1, x2, mod, lng, lnb, w_in_f, b_pair)
    g_b = _rs_share(_rs_sum_chips(place, gb, b_sib, b_chips, "rs_sum_chips_b"))

    packed = _small_pack(acc_in, acc_out, acc_mlp, db1, acc_conv, acc_ssd, dd_x, dsink, alog, e_mat)
    small_all = _allgather8(packed.reshape(8, SMALL_N // 8), "gather_small").reshape(N_DEV, SMALL_N)
    given = dict(ada_b=(ada_b, m_ada_b, v_ada_b), ln_in_g=(ln_in_g, m_ln_in_g, v_ln_in_g), ln_in_b=(ln_in_b, m_ln_in_b, v_ln_in_b),
                 conv_b=(conv_b, m_conv_b, v_conv_b), dt_bias=(dt_bias, m_dt_bias, v_dt_bias), a_log=(a_log, m_a_log, v_a_log),
                 d_skip=(d_skip, m_d_skip, v_d_skip), ssd_norm_w=(ssd_norm_w, m_ssd_norm_w, v_ssd_norm_w),
                 attn_sinks=(attn_sinks, m_attn_sinks, v_attn_sinks), ln1_g=(ln1_g, m_ln1_g, v_ln1_g),
                 ln1_b=(ln1_b, m_ln1_b, v_ln1_b), b_ff1=(b_ff1, m_b_ff1, v_b_ff1), b_ff2=(b_ff2, m_b_ff2, v_b_ff2),
                 ln2_g=(ln2_g, m_ln2_g, v_ln2_g), ln2_b=(ln2_b, m_ln2_b, v_ln2_b))
    upd = _small_update(small_all, *([_row(given[n][i]) for n in SMALL_PARAMS] for i in range(3)))
    small_res = {n: [t.reshape(given[n][0].shape) for t in upd[4 * j:4 * j + 4]] for j, n in enumerate(SMALL_PARAMS)}
    g_conv_all, loss_lanes = upd[4 * len(SMALL_PARAMS)], upd[4 * len(SMALL_PARAMS) + 1]
    loss = jnp.sum(loss_lanes)

    dmod_mine = lax.dynamic_slice(small_all[:, :6 * D_MODEL], (0, chip * ADA_COLS), (N_DEV, ADA_COLS))
    big = {"ada_w": [t[None] for t in _ada_bwd(c_all, dmod_mine, ada_w[0], m_ada_w[0], v_ada_w[0])]}

    g_conv = lax.dynamic_slice(g_conv_all.reshape(CONV_K, CONV_DIM), (0, chip * 384), (CONV_K, 384))
    big["conv_w"] = [t[None] for t in (g_conv, *_adamw(conv_w[0], g_conv, m_conv_w[0], v_conv_w[0], "adamw_conv_w"))]

    g_rows = g_b[:W_COLS].reshape(W_COLS, 1, D_MODEL)
    big["w_in"] = [from_rows(t) for t in (g_rows, *_adamw_rows(as_rows(w_in), g_rows, as_rows(m_w_in), as_rows(v_w_in)))]
    for name, g, (w, m, v) in (("w_out", g_b[D_MODEL:GB_ROWS], (w_out, m_w_out, v_w_out)),
                               ("w_ff1", g_a[:D_MODEL], (w_ff1, m_w_ff1, v_w_ff1)),
                               ("w_ff2", g_a[D_MODEL:GA_ROWS], (w_ff2, m_w_ff2, v_w_ff2))):
        big[name] = [t[None] for t in (g, *_adamw(w[0], g, m[0], v[0], "adamw_" + name))]

    order = ("ln_in_g", "ln_in_b", "ada_w", "ada_b", "w_in", "conv_w", "conv_b", "dt_bias", "a_log", "d_skip", "ssd_norm_w",
             "attn_sinks", "w_out", "ln1_g", "ln1_b", "w_ff1", "b_ff1", "w_ff2", "b_ff2", "ln2_g", "ln2_b")
    res = {**small_res, **big}
    return (loss, grad_x[None], *[res[n][k] for k in range(4) for n in order])
```

```python
import functools
import math

import numpy as np
import jax
import jax.numpy as jnp
from jax import lax
from jax.experimental import pallas as pl
from jax.experimental.pallas import tpu as pltpu

f32 = jnp.float32
bf16 = jnp.bfloat16

D_MODEL = 1024
SSD_WIDTH = 1024
SSD_HEADS = 16
HEAD_DIM = 64
SSD_STATE = 128
SSD_GROUPS = 2
CHUNK = 128
CONV_K = 4
CONV_DIM = 1536
ATTN_HEADS = 16
D_FF = 4096
PROJ_WIDTH = 3856
ALPHA = 2.0 ** 0.25
LN_EPS = 1e-5
RMS_EPS = 1e-5
ATTN_SCALE = HEAD_DIM ** -0.5
NEG = -1e30

ADAM_LR = 0.001
ADAM_B1 = 0.9
ADAM_B2 = 0.999
ADAM_EPS = 1e-08
ADAM_WD = 0.01
ADAM_STEP = 10

W_Z, W_XBC, W_Q, W_KV = slice(0, 1024), slice(1024, 2560), slice(2576, 3600), slice(3600, 3856)
W_DT = slice(2560, 2688)
W_DT_ROWS = 16
GA_ROWS = 2048
GB_ROWS = 1024
GC_ROWS = 512
WG_TM = 512
STAGE_ROWS = 512
ADAMW_BLOCK_ELEMS = 1 << 18
DENSE_TM = 512
N_CHIPS = 4
N_DEV = 8
VMEM_LIMIT = 56 * 1024 * 1024
MESH = pl.DeviceIdType.MESH

ALIBI_SLOPES = tuple(2.0 ** (-8.0 / ATTN_HEADS * (i + 1)) for i in range(ATTN_HEADS))


def _cparams(*sem):
    return pltpu.CompilerParams(dimension_semantics=sem, vmem_limit_bytes=VMEM_LIMIT)


def _sigmoid(x):
    return 1.0 / (1.0 + jnp.exp(-x))


def _softplus(x):
    return jnp.maximum(x, 0.0) + jnp.log1p(jnp.exp(-jnp.abs(x)))


def _ln_stats(x):
    mu = jnp.mean(x, axis=-1, keepdims=True)
    xc = x - mu
    var = jnp.mean(xc * xc, axis=-1, keepdims=True)
    rstd = lax.rsqrt(var + LN_EPS)
    return xc * rstd, rstd


def _ln_bwd(dy, xhat, rstd, g):
    dxh = dy * g
    m1 = jnp.mean(dxh, axis=-1, keepdims=True)
    m2 = jnp.mean(dxh * xhat, axis=-1, keepdims=True)
    return rstd * (dxh - m1 - xhat * m2)


def _dot(a, b):
    return jnp.dot(a, b, preferred_element_type=f32)


def _dot_nt(a, b):
    return lax.dot_general(a, b, (((1,), (1,)), ((), ())), preferred_element_type=f32)


def _dot_tn(a, b):
    return lax.dot_general(a, b, (((0,), (0,)), ((), ())), preferred_element_type=f32)


def _dot_exact(a, b):
    return jnp.dot(a, b, preferred_element_type=f32, precision=lax.Precision.HIGHEST)


def _split3(v):
    hi = v.astype(bf16)
    r1 = v - hi.astype(f32)
    mid = r1.astype(bf16)
    lo = (r1 - mid.astype(f32)).astype(bf16)
    return hi, mid, lo


def _sel_dot(sel, v):
    hi, mid, lo = _split3(v)
    return _dot(sel, hi) + _dot(sel, mid) + _dot(sel, lo)


def _dot_sel(v, sel):
    hi, mid, lo = _split3(v)
    return _dot(hi, sel) + _dot(mid, sel) + _dot(lo, sel)


def _dot_sel_nt(v, sel):
    hi, mid, lo = _split3(v)
    return _dot_nt(hi, sel) + _dot_nt(mid, sel) + _dot_nt(lo, sel)


def _full(shape):
    nd = len(shape)
    return pl.BlockSpec(shape, lambda *_: (0,) * nd)


def _resident(shape):
    nd = len(shape)
    return pl.BlockSpec(shape, lambda *_: (0,) * nd, pipeline_mode=pl.Buffered(1))


def _rows(tm, n):
    return pl.BlockSpec((tm, n), lambda i: (i, 0))


def _inproj_fwd(x, mod, ln_g, ln_b, w_in_t, conv_w, conv_b, blob):
    L = x.shape[0]
    tm = DENSE_TM
    nt = L // tm
    R = blob.shape[0]

    def body(x_ref, mod_ref, g_ref, b_ref, w_ref, cw_ref, cb_ref, blob_ref,
             u1_ref, z_ref, xr_ref, xc_ref, q_ref, kv_ref, dt_ref, wall_ref, halo, buf, send_sems, recv_sems):
        start, finish = _gather_job(blob_ref, wall_ref, send_sems, recv_sems, R)

        @pl.when(pl.program_id(0) == 0)
        def _():
            halo[...] = jnp.zeros_like(halo)
            start()

        xhat, _ = _ln_stats(x_ref[...])
        h0 = xhat * g_ref[...] + b_ref[...]
        u1 = (h0 * (1.0 + mod_ref[1:2, :]) + mod_ref[0:1, :]).astype(bf16)
        u1_ref[...] = u1
        z_ref[...] = _dot_nt(u1, w_ref[W_Z, :])
        xr = _dot_nt(u1, w_ref[W_XBC, :])
        xr_ref[...] = xr
        q_ref[...] = _dot_nt(u1, w_ref[W_Q, :]).astype(bf16)
        kv_ref[...] = _dot_nt(u1, w_ref[W_KV, :]).astype(bf16)
        dt_ref[...] = _dot_nt(u1, w_ref[W_DT, :])
        buf[0:8, :] = halo[...]
        buf[8:8 + tm, :] = xr
        pre = cb_ref[...] + cw_ref[0:1, :] * buf[5:5 + tm, :]
        for k in range(1, CONV_K):
            pre = pre + cw_ref[k:k + 1, :] * buf[5 + k:5 + k + tm, :]
        xc_ref[...] = pre * _sigmoid(pre)
        halo[...] = xr[tm - 8:tm, :]

        @pl.when(pl.program_id(0) == nt - 1)
        def _():
            finish()

    return pl.pallas_call(
        body, name="inproj_fwd", grid=(nt,),
        in_specs=[_rows(tm, D_MODEL), _full((8, D_MODEL)), _full((1, D_MODEL)), _full((1, D_MODEL)),
                  _resident((PROJ_WIDTH, D_MODEL)), _full((CONV_K, CONV_DIM)), _full((1, CONV_DIM)), _ANY_SPEC],
        out_specs=[_rows(tm, D_MODEL), _rows(tm, D_MODEL), _rows(tm, CONV_DIM), _rows(tm, CONV_DIM),
                   _rows(tm, D_MODEL), _rows(tm, 256), _rows(tm, 128), _ANY_SPEC],
        out_shape=[jax.ShapeDtypeStruct((L, D_MODEL), bf16), jax.ShapeDtypeStruct((L, D_MODEL), f32),
                   jax.ShapeDtypeStruct((L, CONV_DIM), f32), jax.ShapeDtypeStruct((L, CONV_DIM), f32),
                   jax.ShapeDtypeStruct((L, D_MODEL), bf16), jax.ShapeDtypeStruct((L, 256), bf16),
                   jax.ShapeDtypeStruct((L, 128), f32), jax.ShapeDtypeStruct((N_CHIPS, R, D_MODEL), bf16)],
        scratch_shapes=[pltpu.VMEM((8, CONV_DIM), f32), pltpu.VMEM((tm + 8, CONV_DIM), f32),
                        pltpu.SemaphoreType.DMA((6,)), pltpu.SemaphoreType.DMA((6,))],
        compiler_params=_cparams("arbitrary"),
    )(x, mod, ln_g, ln_b, w_in_t, conv_w, conv_b, blob)


def _head_expand():
    e = np.zeros((128, SSD_WIDTH), np.float32)
    for h in range(SSD_HEADS):
        e[h, h * HEAD_DIM:(h + 1) * HEAD_DIM] = 1.0
    return jnp.asarray(e, dtype=bf16)


def _ssd_chunk_common(dt_raw, dtb, a_row, e_mat):
    T = CHUNK
    lane = lax.broadcasted_iota(jnp.int32, (T, 128), 1)
    dt = jnp.where(lane < SSD_HEADS, _softplus(dt_raw + dtb), 0.0)
    a = dt * a_row
    r = lax.broadcasted_iota(jnp.int32, (T, T), 0)
    c = lax.broadcasted_iota(jnp.int32, (T, T), 1)
    tril = (c <= r).astype(bf16)
    cum = _sel_dot(tril, a)
    dtx = _dot_sel(dt, e_mat)
    cumx = _dot_sel(cum, e_mat)
    return dt, a, cum, dtx, cumx, r, c


def _ssd_fwd(xc, dt_raw, dt_bias, a_log, d_skip_x, e_mat, blob):
    L = xc.shape[0]
    nc = L // CHUNK
    T = CHUNK
    R = blob.shape[0]

    def body(xc_ref, dt_ref, dtb_ref, al_ref, dsk_ref, e_ref, blob_ref, y_ref, prev_ref, wall_ref, st, send_sems, recv_sems):
        start, finish = _gather_job(blob_ref, wall_ref, send_sems, recv_sems, R)

        @pl.when(pl.program_id(0) == 0)
        def _():
            st[...] = jnp.zeros_like(st)
            start()

        @pl.when(pl.program_id(0) == nc - 1)
        def _():
            finish()

        a_row = -jnp.exp(al_ref[...])
        lane1 = lax.broadcasted_iota(jnp.int32, (1, 128), 1)
        a_row = jnp.where(lane1 < SSD_HEADS, a_row, 0.0)
        dt, a, cum, dtx, cumx, r, c = _ssd_chunk_common(dt_ref[...], dtb_ref[...], a_row, e_ref[...])
        cum_t = cum.T
        ex = jnp.exp(cumx)
        last = cumx[T - 1:T, :]
        wx = jnp.exp(last - cumx)
        cdx = jnp.exp(last)
        xs = xc_ref[:, 0:SSD_WIDTH]
        X = xs * dtx
        Xb = X.astype(bf16)
        Xd = (X * wx).astype(bf16)
        prev = st[...]
        prev_ref[0] = prev
        prevb = prev.astype(bf16)
        tri = c <= r
        lane = lax.broadcasted_iota(jnp.int32, (T, 128), 1)
        y_blocks = []
        new_states = []
        for g in range(SSD_GROUPS):
            Bg = xc_ref[:, 1024 + 128 * g:1152 + 128 * g].astype(bf16)
            Cg = xc_ref[:, 1280 + 128 * g:1408 + 128 * g].astype(bf16)
            G = _dot_nt(Cg, Bg)
            yoff = _dot(Cg, prevb[:, 512 * g:512 * (g + 1)])
            new_states.append(_dot_tn(Bg, Xd[:, 512 * g:512 * (g + 1)]))
            for j in range(4):
                blk = 4 * g + j
                Xblk = Xb[:, 128 * blk:128 * (blk + 1)]
                ys = []
                for half in range(2):
                    h = 2 * blk + half
                    seg = jnp.minimum(cum[:, h:h + 1] - cum_t[h:h + 1, :], 0.0)
                    M = jnp.where(tri, G * jnp.exp(seg), 0.0).astype(bf16)
                    ys.append(_dot(M, Xblk))
                yd = jnp.where(lane < HEAD_DIM, ys[0], ys[1])
                sl = slice(128 * blk, 128 * (blk + 1))
                y_blocks.append(yd + ex[:, sl] * yoff[:, 128 * j:128 * (j + 1)] + dsk_ref[:, sl] * xs[:, sl])
        y_ref[...] = jnp.concatenate(y_blocks, axis=1)
        st[...] = prev * cdx + jnp.concatenate(new_states, axis=1)

    return pl.pallas_call(
        body, name="ssd_fwd", grid=(nc,),
        in_specs=[_rows(T, CONV_DIM), _rows(T, 128), _full((1, 128)), _full((1, 128)), _full((1, SSD_WIDTH)),
                  _full((128, SSD_WIDTH)), _ANY_SPEC],
        out_specs=[_rows(T, SSD_WIDTH), pl.BlockSpec((1, SSD_STATE, SSD_WIDTH), lambda i: (i, 0, 0)), _ANY_SPEC],
        out_shape=[jax.ShapeDtypeStruct((L, SSD_WIDTH), f32), jax.ShapeDtypeStruct((nc, SSD_STATE, SSD_WIDTH), f32),
                   jax.ShapeDtypeStruct((N_CHIPS, R, D_MODEL), bf16)],
        scratch_shapes=[pltpu.VMEM((SSD_STATE, SSD_WIDTH), f32)] + _sems(6),
        compiler_params=_cparams("arbitrary"),
    )(xc, dt_raw, dt_bias, a_log, d_skip_x, e_mat, blob)


def _kv_halves(kv_prev, kv_cur, first):
    kv = jnp.concatenate([jnp.where(first, 0.0, kv_prev.astype(f32)), kv_cur.astype(f32)], axis=0)
    lane = lax.broadcasted_iota(jnp.int32, (2 * CHUNK, 128), 1)
    lo = lane < HEAD_DIM
    out = []
    for g in range(2):
        per_half = []
        for half in range(2):
            both = []
            for t in (kv[:, 0:128], kv[:, 128:256]):
                src = t if g == half else pltpu.roll(t, HEAD_DIM, 1)
                both.append(jnp.where(lo if half == 0 else ~lo, src, 0.0).astype(bf16))
            per_half.append(tuple(both))
        out.append(per_half)
    return out


def _attn_masks(first):
    r = lax.broadcasted_iota(jnp.int32, (CHUNK, 2 * CHUNK), 0)
    c = lax.broadcasted_iota(jnp.int32, (CHUNK, 2 * CHUNK), 1)
    dist = r + CHUNK - c
    valid = (dist >= 0) & (dist < CHUNK) & ((c >= CHUNK) | jnp.logical_not(first))
    return dist.astype(f32), valid


def _attn_fwd(q, kv, sinks, blob):
    L = q.shape[0]
    nb = L // CHUNK
    T = CHUNK
    R = blob.shape[0]

    def body(sink_ref, q_ref, kvp_ref, kvc_ref, blob_ref, o_ref, lse_ref, wall_ref, send_sems, recv_sems):
        first = pl.program_id(0) == 0
        start, finish = _gather_job(blob_ref, wall_ref, send_sems, recv_sems, R)

        @pl.when(first)
        def _():
            start()

        @pl.when(pl.program_id(0) == nb - 1)
        def _():
            finish()

        ext = _kv_halves(kvp_ref[...], kvc_ref[...], first)
        dist, valid = _attn_masks(first)
        lane = lax.broadcasted_iota(jnp.int32, (T, 128), 1)
        lse = jnp.zeros((T, 128), f32)
        o_blocks = []
        for blk in range(8):
            qb = q_ref[:, 128 * blk:128 * (blk + 1)]
            acc = None
            for half in range(2):
                h = 2 * blk + half
                k_ext, v_ext = ext[h // 8][half]
                s = _dot_nt(qb, k_ext) * ATTN_SCALE - ALIBI_SLOPES[h] * dist
                s = jnp.where(valid, s, NEG)
                sink = sink_ref[h]
                m = jnp.maximum(jnp.max(s, axis=-1, keepdims=True), sink)
                p = jnp.exp(s - m)
                den = jnp.sum(p, axis=-1, keepdims=True) + jnp.exp(sink - m)
                pn = (p * (1.0 / den)).astype(bf16)
                oh = _dot(pn, v_ext)
                acc = oh if acc is None else acc + oh
                lse = jnp.where(lane == h, m + jnp.log(den), lse)
            o_blocks.append(acc.astype(bf16))
        o_ref[...] = jnp.concatenate(o_blocks, axis=1)
        lse_ref[...] = lse

    return pl.pallas_call(
        body, name="attn_fwd", grid=(nb,),
        in_specs=[pl.BlockSpec(memory_space=pltpu.SMEM), _rows(T, D_MODEL),
                  pl.BlockSpec((T, 256), lambda i: (jnp.maximum(i - 1, 0), 0)), _rows(T, 256), _ANY_SPEC],
        out_specs=[_rows(T, D_MODEL), _rows(T, 128), _ANY_SPEC],
        out_shape=[jax.ShapeDtypeStruct((L, D_MODEL), bf16), jax.ShapeDtypeStruct((L, 128), f32),
                   jax.ShapeDtypeStruct((N_CHIPS, R, D_MODEL), bf16)],
        scratch_shapes=_sems(6),
        compiler_params=_cparams("arbitrary"),
    )(sinks, q, kv, kv, blob)


def _gated_norm(y, z, w):
    sz = _sigmoid(z)
    hg = y * (z * sz)
    ns, rss = [], []
    for g in range(SSD_GROUPS):
        hs = hg[:, 512 * g:512 * (g + 1)]
        rs = lax.rsqrt(jnp.mean(hs * hs, axis=-1, keepdims=True) + RMS_EPS)
        ns.append(hs * rs)
        rss.append(rs)
    n = jnp.concatenate(ns, axis=1)
    return n * w, n, rss, sz


def _outproj_fwd(y, z, o, x, mod, ln_g, ln_b, norm_w, w_out):
    L = x.shape[0]
    tm = DENSE_TM

    def body(y_ref, z_ref, o_ref, x_ref, mod_ref, g_ref, b_ref, nw_ref, w_ref, yn_ref, mix_ref, r1_ref):
        yn, _, _, _ = _gated_norm(y_ref[...], z_ref[...], nw_ref[...])
        ynb = yn.astype(bf16)
        yn_ref[...] = ynb
        mix = (_dot(ynb[:, 0:512], w_ref[0]) + _dot(ynb[:, 512:1024], w_ref[1])
               + _dot(o_ref[:, 0:512], w_ref[2]) + _dot(o_ref[:, 512:1024], w_ref[3]))
        mix_ref[...] = mix
        xhat, _ = _ln_stats(x_ref[...])
        h0 = xhat * g_ref[...] + b_ref[...]
        r1_ref[...] = ALPHA * h0 + (1.0 + mod_ref[2:3, :]) * mix

    v = _full((1, D_MODEL))
    return pl.pallas_call(
        body, name="outproj_fwd", grid=(L // tm,),
        in_specs=[_rows(tm, D_MODEL), _rows(tm, D_MODEL), _rows(tm, D_MODEL), _rows(tm, D_MODEL),
                  _full((8, D_MODEL)), v, v, v, _resident((N_CHIPS, 512, D_MODEL))],
        out_specs=[_rows(tm, D_MODEL)] * 3,
        out_shape=[jax.ShapeDtypeStruct((L, D_MODEL), bf16), jax.ShapeDtypeStruct((L, D_MODEL), f32),
                   jax.ShapeDtypeStruct((L, D_MODEL), f32)],
        compiler_params=_cparams("parallel"),
    )(y, z, o, x, mod, ln_g, ln_b, norm_w, w_out)


A_LN2G, A_LN2B, A_G2, A_B2, A_SC2, A_SH2, A_LN1G, A_LN1B, A_LOSS = range(9)


def _mlp_fwd_bwd(r1, target, mod, ln1_g, ln1_b, ln2_g, ln2_b, w1, b1, w2, b2):
    L = r1.shape[0]
    tm = 256
    nj = D_FF // 1024

    def body(r1_ref, t_ref, mod_ref, g1_ref, bb1_ref, g2_ref, bb2_ref, w1_ref, b1_ref, w2_ref, b2_ref,
             dr1_ref, u2_ref, s_ref, da_ref, df_ref, acc_ref, db1_ref, hr):
        @pl.when(pl.program_id(0) == 0)
        def _():
            acc_ref[...] = jnp.zeros_like(acc_ref)
            db1_ref[...] = jnp.zeros_like(db1_ref)

        sc2, sh2, gate2 = mod_ref[4:5, :], mod_ref[3:4, :], mod_ref[5:6, :]
        xhat1, rstd1 = _ln_stats(r1_ref[...])
        h1 = xhat1 * g1_ref[...] + bb1_ref[...]
        u2f = h1 * (1.0 + sc2) + sh2
        u2 = u2f.astype(bf16)
        u2_ref[...] = u2
        f = jnp.zeros((tm, D_MODEL), f32) + b2_ref[...]
        for j in range(nj):
            cs = slice(1024 * j, 1024 * (j + 1))
            a = _dot(u2, w1_ref[j]) + b1_ref[:, cs]
            hrj = jnp.maximum(a, 0.0)
            hr[:, cs] = hrj
            sj = (hrj * hrj).astype(bf16)
            s_ref[:, cs] = sj
            f = f + _dot(sj, w2_ref[j])
        r2 = ALPHA * h1 + (1.0 + gate2) * f
        xhat2, rstd2 = _ln_stats(r2)
        h2 = xhat2 * g2_ref[...] + bb2_ref[...]
        diff = h2 - t_ref[...]
        dh2 = diff * (1.0 / D_MODEL)

        def add(row, val):
            acc_ref[row:row + 1, :] += jnp.sum(val, axis=0, keepdims=True)

        add(A_LOSS, diff * diff * (0.5 / D_MODEL))
        add(A_LN2G, dh2 * xhat2)
        add(A_LN2B, dh2)
        dr2 = _ln_bwd(dh2, xhat2, rstd2, g2_ref[...])
        add(A_G2, dr2 * f)
        df = dr2 * (1.0 + gate2)
        add(A_B2, df)
        dfb = df.astype(bf16)
        df_ref[...] = dfb
        du2 = jnp.zeros((tm, D_MODEL), f32)
        for j in range(nj):
            cs = slice(1024 * j, 1024 * (j + 1))
            ds = _dot_nt(dfb, w2_ref[j])
            daj = ds * (2.0 * hr[:, cs])
            db1_ref[:, cs] += jnp.sum(daj, axis=0, keepdims=True)
            dajb = daj.astype(bf16)
            da_ref[:, cs] = dajb
            du2 = du2 + _dot_nt(dajb, w1_ref[j])
        add(A_SC2, du2 * h1)
        add(A_SH2, du2)
        dh1 = ALPHA * dr2 + du2 * (1.0 + sc2)
        add(A_LN1G, dh1 * xhat1)
        add(A_LN1B, dh1)
        dr1_ref[...] = _ln_bwd(dh1, xhat1, rstd1, g1_ref[...])

    v = _full((1, D_MODEL))
    return pl.pallas_call(
        body, name="mlp_fwd_bwd", grid=(L // tm,),
        in_specs=[_rows(tm, D_MODEL), _rows(tm, D_MODEL), _full((8, D_MODEL)), v, v, v, v,
                  _resident((N_CHIPS, D_MODEL, D_MODEL)), _full((1, D_FF)), _resident((N_CHIPS, D_MODEL, D_MODEL)), v],
        out_specs=[_rows(tm, D_MODEL), _rows(tm, D_MODEL), _rows(tm, D_FF), _rows(tm, D_FF), _rows(tm, D_MODEL),
                   _full((16, D_MODEL)), _full((1, D_FF))],
        out_shape=[jax.ShapeDtypeStruct((L, D_MODEL), f32), jax.ShapeDtypeStruct((L, D_MODEL), bf16),
                   jax.ShapeDtypeStruct((L, D_FF), bf16), jax.ShapeDtypeStruct((L, D_FF), bf16),
                   jax.ShapeDtypeStruct((L, D_MODEL), bf16), jax.ShapeDtypeStruct((16, D_MODEL), f32),
                   jax.ShapeDtypeStruct((1, D_FF), f32)],
        scratch_shapes=[pltpu.VMEM((tm, D_FF), f32)],
        compiler_params=_cparams("arbitrary"),
    )(r1, target, mod, ln1_g, ln1_b, ln2_g, ln2_b, w1, b1, w2, b2)


def _wgrad(a, b, name):
    L, M = a.shape
    N = b.shape[1]
    tm = min(M, 512)
    tn = next(t for t in (1024, 768, 512, 256, 128) if N % t == 0)

    def body(a_ref, b_ref, o_ref):
        o_ref[...] = _dot_tn(a_ref[...], b_ref[...]).astype(bf16)

    return pl.pallas_call(
        body, name=name, grid=(M // tm, N // tn),
        in_specs=[pl.BlockSpec((L, tm), lambda i, j: (0, i)), pl.BlockSpec((L, tn), lambda i, j: (0, j))],
        out_specs=pl.BlockSpec((tm, tn), lambda i, j: (i, j)),
        out_shape=jax.ShapeDtypeStruct((M, N), bf16),
        compiler_params=_cparams("parallel", "parallel"),
    )(a, b)


def _wgrad_blob(blob, a, b, name, place_of):
    L, M = a.shape
    N = b.shape[1]

    def body(blob_ref, a_ref, b_ref, o_ref):
        o_ref[0] = _dot_tn(a_ref[...], b_ref[...]).astype(bf16)

    return pl.pallas_call(
        body, name=name, grid=(M // WG_TM, N // D_MODEL),
        in_specs=[pl.BlockSpec(memory_space=pl.ANY), pl.BlockSpec((L, WG_TM), lambda t, n: (0, t)),
                  pl.BlockSpec((L, D_MODEL), lambda t, n: (0, n))],
        out_specs=pl.BlockSpec((1, WG_TM, D_MODEL), lambda t, n: (*place_of(t, n), 0)),
        out_shape=jax.ShapeDtypeStruct(blob.shape, bf16), input_output_aliases={0: 0},
        compiler_params=_cparams("parallel", "parallel"),
    )(blob, a, b)


def _outproj_bwd(dr1, mix, y, z, mod, norm_w, w_out, gb):
    L = dr1.shape[0]
    tm = DENSE_TM
    nt = L // tm

    def body(dr1_ref, mix_ref, y_ref, z_ref, mod_ref, nw_ref, w_ref, gb_ref,
             dy_ref, dz_ref, do_ref, dmix_ref, acc_ref, sib_ref, send_sems, recv_sems):
        start, wait = _to_sibling_job(gb_ref, sib_ref, send_sems, recv_sems)

        @pl.when(pl.program_id(0) == 0)
        def _():
            acc_ref[...] = jnp.zeros_like(acc_ref)
            start()

        @pl.when(pl.program_id(0) == nt - 1)
        def _():
            wait()

        dr1 = dr1_ref[...]
        acc_ref[0:1, :] += jnp.sum(dr1 * mix_ref[...], axis=0, keepdims=True)
        dmix = (dr1 * (1.0 + mod_ref[2:3, :])).astype(bf16)
        dmix_ref[...] = dmix
        dyn = jnp.concatenate([_dot_nt(dmix, w_ref[0]), _dot_nt(dmix, w_ref[1])], axis=1)
        do_ref[...] = jnp.concatenate([_dot_nt(dmix, w_ref[2]), _dot_nt(dmix, w_ref[3])], axis=1).astype(bf16)
        yv, zv = y_ref[...], z_ref[...]
        _, n, rss, sz = _gated_norm(yv, zv, nw_ref[...])
        acc_ref[1:2, :] += jnp.sum(dyn * n, axis=0, keepdims=True)
        dn = dyn * nw_ref[...]
        parts = []
        for g in range(SSD_GROUPS):
            sl = slice(512 * g, 512 * (g + 1))
            dng, ng = dn[:, sl], n[:, sl]
            parts.append(rss[g] * (dng - ng * jnp.mean(dng * ng, axis=-1, keepdims=True)))
        dhg = jnp.concatenate(parts, axis=1)
        dy_ref[...] = dhg * (zv * sz)
        dz_ref[...] = (dhg * yv * (sz * (1.0 + zv * (1.0 - sz)))).astype(bf16)

    return pl.pallas_call(
        body, name="outproj_bwd", grid=(nt,),
        in_specs=[_rows(tm, D_MODEL)] * 4 + [_full((8, D_MODEL)), _full((1, D_MODEL)), _resident((N_CHIPS, 512, D_MODEL)),
                  _ANY_SPEC],
        out_specs=[_rows(tm, D_MODEL)] * 4 + [_full((8, D_MODEL)), _ANY_SPEC],
        out_shape=[jax.ShapeDtypeStruct((L, D_MODEL), f32)] + [jax.ShapeDtypeStruct((L, D_MODEL), bf16)] * 3
        + [jax.ShapeDtypeStruct((8, D_MODEL), f32), jax.ShapeDtypeStruct((N_CHIPS,) + gb.shape[2:], bf16)],
        scratch_shapes=_sems(N_CHIPS),
        compiler_params=_cparams("arbitrary"),
    )(dr1, mix, y, z, mod, norm_w, w_out, gb)


def _attn_bwd(q, kv, do, lse, sinks, pb, gb2):
    L = q.shape[0]
    nb = L // CHUNK
    T = CHUNK

    def body(sink_ref, q_ref, kvp_ref, kvc_ref, do_ref, lse_ref, pb_ref, gb2_ref, dq_ref, dkv_ref, dsink_ref, chips_ref,
             sib2_ref, carry, send_sems, recv_sems, send_sems2, recv_sems2):
        n = pl.program_id(0)
        start, wait = _to_chips_job(pb_ref, chips_ref, send_sems, recv_sems)
        start2, wait2 = _to_sibling_job(gb2_ref, sib2_ref, send_sems2, recv_sems2)

        @pl.when(n == 0)
        def _():
            carry[...] = jnp.zeros_like(carry)
            dsink_ref[...] = jnp.zeros_like(dsink_ref)
            start()
            start2()

        @pl.when(n < nb)
        def _():
            first = n == 0
            ext = _kv_halves(kvp_ref[...], kvc_ref[...], first)
            dist, valid = _attn_masks(first)
            lane1 = lax.broadcasted_iota(jnp.int32, (1, 128), 1)
            lse = lse_ref[...]
            acck = [None, None]
            accv = [None, None]
            dsink = jnp.zeros((1, 128), f32)
            dq_blocks = []
            for blk in range(8):
                qb = q_ref[:, 128 * blk:128 * (blk + 1)]
                dob = do_ref[:, 128 * blk:128 * (blk + 1)]
                qt = qb.astype(f32).T.astype(bf16)
                dot_ = dob.astype(f32).T.astype(bf16)
                dq_acc = None
                for half in range(2):
                    h = 2 * blk + half
                    g = h // 8
                    k_ext, v_ext = ext[g][half]
                    s = _dot_nt(qb, k_ext) * ATTN_SCALE - ALIBI_SLOPES[h] * dist
                    lse_h = lse[:, h:h + 1]
                    p = jnp.where(valid, jnp.exp(s - lse_h), 0.0)
                    dp = _dot_nt(dob, v_ext)
                    delta = jnp.sum(p * dp, axis=-1, keepdims=True)
                    ds = (p * (dp - delta) * ATTN_SCALE).astype(bf16)
                    psink = jnp.exp(sink_ref[h] - lse_h)
                    dsink = dsink - jnp.where(lane1 == h, jnp.sum(psink * delta, axis=0, keepdims=True), 0.0)
                    dqh = _dot(ds, k_ext)
                    dq_acc = dqh if dq_acc is None else dq_acc + dqh
                    rows = slice(HEAD_DIM * half, HEAD_DIM * (half + 1))
                    dkh = _dot(qt[rows, :], ds)
                    dvh = _dot(dot_[rows, :], p.astype(bf16))
                    acck[g] = dkh if acck[g] is None else acck[g] + dkh
                    accv[g] = dvh if accv[g] is None else accv[g] + dvh
                dq_blocks.append(dq_acc.astype(bf16))
            dq_ref[...] = jnp.concatenate(dq_blocks, axis=1)
            dsink_ref[...] += dsink
            dkv = jnp.concatenate([jnp.concatenate(acck, axis=0).T, jnp.concatenate(accv, axis=0).T], axis=1)
            dkv_ref[...] = (carry[...] + dkv[0:T, :]).astype(bf16)
            carry[...] = dkv[T:2 * T, :]

        @pl.when(n == nb)
        def _():
            dkv_ref[...] = carry[...].astype(bf16)
            wait()
            wait2()

    cur = lambda i: (jnp.minimum(i, nb - 1), 0)
    return pl.pallas_call(
        body, name="attn_bwd", grid=(nb + 1,),
        in_specs=[pl.BlockSpec(memory_space=pltpu.SMEM), pl.BlockSpec((T, D_MODEL), cur),
                  pl.BlockSpec((T, 256), lambda i: (jnp.maximum(jnp.minimum(i, nb - 1) - 1, 0), 0)),
                  pl.BlockSpec((T, 256), cur), pl.BlockSpec((T, D_MODEL), cur), pl.BlockSpec((T, 128), cur), _ANY_SPEC,
                  _ANY_SPEC],
        out_specs=[pl.BlockSpec((T, D_MODEL), cur), pl.BlockSpec((T, 256), lambda i: (jnp.maximum(i - 1, 0), 0)),
                   _full((1, 128)), _ANY_SPEC, _ANY_SPEC],
        out_shape=[jax.ShapeDtypeStruct((L, D_MODEL), bf16), jax.ShapeDtypeStruct((L, 256), bf16),
                   jax.ShapeDtypeStruct((1, 128), f32), jax.ShapeDtypeStruct((N_CHIPS - 1,) + pb.shape[1:], bf16),
                   jax.ShapeDtypeStruct((N_CHIPS,) + gb2.shape[2:], bf16)],
        scratch_shapes=[pltpu.VMEM((T, 256), f32)] + _sems(N_CHIPS - 1) + _sems(N_CHIPS),
        compiler_params=_cparams("arbitrary"),
    )(sinks, q, kv, kv, do, lse, pb, gb2)


def _ssd_bwd(xc, dt_raw, dy, prev_all, dt_bias, a_log, d_skip_x, e_mat, g, pb2):
    L = xc.shape[0]
    nc = L // CHUNK
    T = CHUNK
    RG = g.shape[0]

    def body(xc_ref, dt_ref, dy_ref, prev_ref, dtb_ref, al_ref, dsk_ref, e_ref, g_in_ref, pb2_ref,
             dxc_ref, ddt_ref, acc_ref, dd_ref, g_ref, chips2_ref, dst, dxs_s, send_sems, recv_sems, send_sems2, recv_sems2):
        start, wait = _share_job(g_ref, send_sems, recv_sems, RG)
        start2, wait2 = _to_chips_job(pb2_ref, chips2_ref, send_sems2, recv_sems2)

        @pl.when(pl.program_id(0) == 0)
        def _():
            dst[...] = jnp.zeros_like(dst)
            acc_ref[...] = jnp.zeros_like(acc_ref)
            dd_ref[...] = jnp.zeros_like(dd_ref)
            start()
            start2()

        @pl.when(pl.program_id(0) == nc - 1)
        def _():
            wait()
            wait2()

        lane1 = lax.broadcasted_iota(jnp.int32, (1, 128), 1)
        a_row = jnp.where(lane1 < SSD_HEADS, -jnp.exp(al_ref[...]), 0.0)
        e_mat_v = e_ref[...]
        dt, a, cum, dtx, cumx, r, c = _ssd_chunk_common(dt_ref[...], dtb_ref[...], a_row, e_mat_v)
        cum_t = cum.T
        ex = jnp.exp(cumx)
        last = cumx[T - 1:T, :]
        wx = jnp.exp(last - cumx)
        cdx = jnp.exp(last)
        xs = xc_ref[:, 0:SSD_WIDTH]
        X = xs * dtx
        Xb = X.astype(bf16)
        Xdb = (X * wx).astype(bf16)
        dyv = dy_ref[...]
        prev = prev_ref[0]
        prevb = prev.astype(bf16)
        dnew = dst[...]
        dnewb = dnew.astype(bf16)
        tri = c <= r
        lane = lax.broadcasted_iota(jnp.int32, (T, 128), 1)
        sub = lax.broadcasted_iota(jnp.int32, (128, T), 0)
        lo = lane < HEAD_DIM

        def red(vals, g):
            return _dot_sel_nt(vals, e_mat_v[:, 512 * g:512 * (g + 1)])

        de = jnp.zeros((T, 128), f32)
        dw = jnp.zeros((T, 128), f32)
        ddt_x = jnp.zeros((T, 128), f32)
        dcum_col = jnp.zeros((T, 128), f32)
        dcum_row = jnp.zeros((128, T), f32)
        dprev_parts, dBs, dCs = [], [], []
        for g in range(SSD_GROUPS):
            s5 = slice(512 * g, 512 * (g + 1))
            Bg = xc_ref[:, 1024 + 128 * g:1152 + 128 * g].astype(bf16)
            Cg = xc_ref[:, 1280 + 128 * g:1408 + 128 * g].astype(bf16)
            G = _dot_nt(Cg, Bg)
            Z = _dot(Cg, prevb[:, s5])
            dyg = dyv[:, s5]
            dZb = (dyg * ex[:, s5]).astype(bf16)
            dXd = _dot(Bg, dnewb[:, s5])
            dC = _dot_nt(dZb, prevb[:, s5])
            dB = _dot_nt(Xdb[:, s5], dnewb[:, s5])
            dprev_parts.append(_dot_tn(Cg, dZb) + dnew[:, s5] * cdx[:, s5])
            de = de + red(dyg * Z, g)
            dw = dw + red(dXd * X[:, s5], g)
            dXg = dXd * wx[:, s5]
            dG = jnp.zeros((T, T), f32)
            for j in range(4):
                blk = 4 * g + j
                sl = slice(128 * blk, 128 * (blk + 1))
                Xblk = Xb[:, sl]
                dyblk = dyv[:, sl]
                dyblk_b = dyblk.astype(bf16)
                dxh = []
                for half in range(2):
                    h = 2 * blk + half
                    seg = jnp.minimum(cum[:, h:h + 1] - cum_t[h:h + 1, :], 0.0)
                    Lm = jnp.where(tri, jnp.exp(seg), 0.0)
                    M = G * Lm
                    dyh = jnp.where(lo if half == 0 else ~lo, dyblk, 0.0).astype(bf16)
                    dM = _dot_nt(dyh, Xblk)
                    dG = dG + dM * Lm
                    Q = dM * M
                    dcum_col = dcum_col + jnp.where(lane == h, jnp.sum(Q, axis=1, keepdims=True), 0.0)
                    dcum_row = dcum_row + jnp.where(sub == h, jnp.sum(Q, axis=0, keepdims=True), 0.0)
                    dxh.append(_dot_tn(M.astype(bf16), dyblk_b))
                dXblk = dXg[:, 128 * j:128 * (j + 1)] + jnp.where(lo, dxh[0], dxh[1])
                xsb = xs[:, sl]
                dxs_s[:, sl] = dXblk * dtx[:, sl] + dsk_ref[:, sl] * dyblk
                ddt_x = ddt_x + _dot_sel_nt(dXblk * xsb, e_mat_v[:, sl])
                dd_ref[:, sl] += jnp.sum(dyblk * xsb, axis=0, keepdims=True)
            dGb = dG.astype(bf16)
            dCs.append(dC + _dot(dGb, Bg))
            dBs.append(dB + _dot_tn(dGb, Cg))
        e16 = jnp.exp(cum)
        cum_last = cum[T - 1:T, :]
        w16 = jnp.exp(cum_last - cum)
        dcd = jnp.sum(dnew * prev, axis=0, keepdims=True)
        dcd16 = red(dcd[:, 0:512], 0) + red(dcd[:, 512:1024], 1)
        dww = dw * w16
        extra = jnp.sum(dww, axis=0, keepdims=True) + dcd16 * jnp.exp(cum_last)
        rowi = lax.broadcasted_iota(jnp.int32, (T, 128), 0)
        dcum = dcum_col - dcum_row.T + de * e16 - dww + jnp.where(rowi == T - 1, extra, 0.0)
        da = _sel_dot((c >= r).astype(bf16), dcum)
        ddt = ddt_x + da * a_row
        acc_ref[0:1, :] += jnp.sum(da * dt, axis=0, keepdims=True)
        ddt_raw = jnp.where(lane < SSD_HEADS, ddt * _sigmoid(dt_ref[...] + dtb_ref[...]), 0.0)
        ddt_ref[...] = ddt_raw
        acc_ref[1:2, :] += jnp.sum(ddt_raw, axis=0, keepdims=True)
        dxc_ref[:, 0:SSD_WIDTH] = dxs_s[...]
        dxc_ref[:, 1024:1280] = jnp.concatenate(dBs, axis=1)
        dxc_ref[:, 1280:1536] = jnp.concatenate(dCs, axis=1)
        dst[...] = jnp.concatenate(dprev_parts, axis=1)

    rev = lambda i: (nc - 1 - i, 0)
    return pl.pallas_call(
        body, name="ssd_bwd", grid=(nc,),
        in_specs=[pl.BlockSpec((T, CONV_DIM), rev), pl.BlockSpec((T, 128), rev), pl.BlockSpec((T, SSD_WIDTH), rev),
                  pl.BlockSpec((1, SSD_STATE, SSD_WIDTH), lambda i: (nc - 1 - i, 0, 0)),
                  _full((1, 128)), _full((1, 128)), _full((1, SSD_WIDTH)), _full((128, SSD_WIDTH)), _ANY_SPEC, _ANY_SPEC],
        out_specs=[pl.BlockSpec((T, CONV_DIM), rev), pl.BlockSpec((T, 128), rev), _full((8, 128)),
                   _full((1, SSD_WIDTH)), _ANY_SPEC, _ANY_SPEC],
        out_shape=[jax.ShapeDtypeStruct((L, CONV_DIM), f32), jax.ShapeDtypeStruct((L, 128), f32),
                   jax.ShapeDtypeStruct((8, 128), f32), jax.ShapeDtypeStruct((1, SSD_WIDTH), f32),
                   jax.ShapeDtypeStruct(g.shape, f32), jax.ShapeDtypeStruct((N_CHIPS - 1,) + pb2.shape[1:], bf16)],
        input_output_aliases={8: 4},
        scratch_shapes=[pltpu.VMEM((SSD_STATE, SSD_WIDTH), f32), pltpu.VMEM((T, SSD_WIDTH), f32)] + _sems(1)
        + _sems(N_CHIPS - 1),
        compiler_params=_cparams("arbitrary"),
    )(xc, dt_raw, dy, prev_all, dt_bias, a_log, d_skip_x, e_mat, g, pb2)


def _conv_bwd(dxc, xr, conv_w, conv_b, g):
    L = dxc.shape[0]
    tm = 256
    nt = L // tm
    RG = g.shape[0]

    def body(dxc_ref, xr_ref, xh_ref, cw_ref, cb_ref, g_in_ref, dxr_ref, acc_ref, g_ref, carry, buf, ext, send_sems, recv_sems):
        i = pl.program_id(0)
        start, wait = _share_job(g_ref, send_sems, recv_sems, RG)

        @pl.when(i == 0)
        def _():
            carry[...] = jnp.zeros_like(carry)
            acc_ref[...] = jnp.zeros_like(acc_ref)
            ext[tm + 16:tm + CHUNK, :] = jnp.zeros((CHUNK - 16, CONV_DIM), bf16)
            start()

        @pl.when(i == nt - 1)
        def _():
            wait()

        buf[0:8, :] = jnp.where(i == nt - 1, 0.0, xh_ref[...])
        u = xr_ref[...]
        buf[8:8 + tm, :] = u
        pre = cb_ref[...] + cw_ref[CONV_K - 1:CONV_K, :] * u
        for k in range(CONV_K - 1):
            pre = pre + cw_ref[k:k + 1, :] * buf[5 + k:5 + k + tm, :]
        sg = _sigmoid(pre)
        dpre = dxc_ref[...] * (sg * (1.0 + pre * (1.0 - sg)))
        acc_ref[4:5, :] += jnp.sum(dpre, axis=0, keepdims=True)
        dpb = dpre.astype(bf16)
        ext[0:tm, :] = dpb
        ext[tm:tm + 16, :] = carry[...]
        acc_ref[CONV_K - 1:CONV_K, :] += jnp.sum(u * dpre, axis=0, keepdims=True)
        du = cw_ref[CONV_K - 1:CONV_K, :] * dpre
        r = lax.broadcasted_iota(jnp.int32, (CHUNK, 2 * CHUNK), 0)
        c = lax.broadcasted_iota(jnp.int32, (CHUNK, 2 * CHUNK), 1)
        for j in range(1, CONV_K):
            move = (c == r + j).astype(bf16)
            up = jnp.concatenate([_dot(move, ext[CHUNK * b:CHUNK * (b + 2), :]) for b in range(tm // CHUNK)], axis=0)
            k = CONV_K - 1 - j
            du = du + cw_ref[k:k + 1, :] * up
            acc_ref[k:k + 1, :] += jnp.sum(u * up, axis=0, keepdims=True)
        dxr_ref[...] = du.astype(bf16)
        carry[...] = dpb[0:16, :]

    rev = lambda i: (nt - 1 - i, 0)
    return pl.pallas_call(
        body, name="conv_bwd", grid=(nt,),
        in_specs=[pl.BlockSpec((tm, CONV_DIM), rev), pl.BlockSpec((tm, CONV_DIM), rev),
                  pl.BlockSpec((8, CONV_DIM), lambda i: (jnp.maximum((nt - 1 - i) * (tm // 8) - 1, 0), 0)),
                  _full((CONV_K, CONV_DIM)), _full((1, CONV_DIM)), _ANY_SPEC],
        out_specs=[pl.BlockSpec((tm, CONV_DIM), rev), _full((8, CONV_DIM)), _ANY_SPEC],
        out_shape=[jax.ShapeDtypeStruct((L, CONV_DIM), bf16), jax.ShapeDtypeStruct((8, CONV_DIM), f32),
                   jax.ShapeDtypeStruct(g.shape, f32)],
        input_output_aliases={5: 2},
        scratch_shapes=[pltpu.VMEM((16, CONV_DIM), bf16), pltpu.VMEM((tm + 8, CONV_DIM), f32),
                        pltpu.VMEM((tm + CHUNK, CONV_DIM), bf16)] + _sems(1),
        compiler_params=_cparams("arbitrary"),
    )(dxc, xr, xr, conv_w, conv_b, g)


def _inproj_bwd(dz, dxr, dq, dkv, ddt, dr1, x, mod, ln_g, ln_b, w_in, pb):
    L = x.shape[0]
    tm = DENSE_TM
    nt = L // tm

    def body(dz_ref, dxr_ref, dq_ref, dkv_ref, ddt_ref, dr1_ref, x_ref, mod_ref, g_ref, b_ref, w_ref, pb_ref,
             dx_ref, acc_ref, chips_ref, send_sems, recv_sems):
        start, wait = _to_chips_job(pb_ref, chips_ref, send_sems, recv_sems)

        @pl.when(pl.program_id(0) == 0)
        def _():
            acc_ref[...] = jnp.zeros_like(acc_ref)
            start()

        @pl.when(pl.program_id(0) == nt - 1)
        def _():
            wait()

        du1 = (_dot(dz_ref[...], w_ref[W_Z, :]) + _dot(dxr_ref[...], w_ref[W_XBC, :])
               + _dot(dq_ref[...], w_ref[W_Q, :]) + _dot(dkv_ref[...], w_ref[W_KV, :])
               + _dot(ddt_ref[...].astype(bf16), w_ref[W_DT, :]))
        xhat, rstd = _ln_stats(x_ref[...])
        h0 = xhat * g_ref[...] + b_ref[...]
        acc_ref[0:1, :] += jnp.sum(du1 * h0, axis=0, keepdims=True)
        acc_ref[1:2, :] += jnp.sum(du1, axis=0, keepdims=True)
        dh0 = du1 * (1.0 + mod_ref[1:2, :]) + ALPHA * dr1_ref[...]
        acc_ref[2:3, :] += jnp.sum(dh0 * xhat, axis=0, keepdims=True)
        acc_ref[3:4, :] += jnp.sum(dh0, axis=0, keepdims=True)
        dx_ref[...] = _ln_bwd(dh0, xhat, rstd, g_ref[...])

    v = _full((1, D_MODEL))
    return pl.pallas_call(
        body, name="inproj_bwd", grid=(nt,),
        in_specs=[_rows(tm, D_MODEL), _rows(tm, CONV_DIM), _rows(tm, D_MODEL), _rows(tm, 256), _rows(tm, 128),
                  _rows(tm, D_MODEL), _rows(tm, D_MODEL), _full((8, D_MODEL)), v, v, _resident((PROJ_WIDTH, D_MODEL)),
                  _ANY_SPEC],
        out_specs=[_rows(tm, D_MODEL), _full((8, D_MODEL)), _ANY_SPEC],
        out_shape=[jax.ShapeDtypeStruct((L, D_MODEL), f32), jax.ShapeDtypeStruct((8, D_MODEL), f32),
                   jax.ShapeDtypeStruct((N_CHIPS - 1,) + pb.shape[1:], bf16)],
        scratch_shapes=_sems(N_CHIPS - 1),
        compiler_params=_cparams("arbitrary"),
    )(dz, dxr, dq, dkv, ddt, dr1, x, mod, ln_g, ln_b, w_in, pb)


def _adamw_math(w, g, m, v):
    m = ADAM_B1 * m + (1.0 - ADAM_B1) * g
    v = ADAM_B2 * v + (1.0 - ADAM_B2) * (g * g)
    m_hat = m / (1.0 - ADAM_B1 ** ADAM_STEP)
    v_hat = v / (1.0 - ADAM_B2 ** ADAM_STEP)
    delta = -ADAM_LR * (m_hat / (jnp.sqrt(v_hat) + ADAM_EPS) + ADAM_WD * w)
    return delta, m, v


def _adamw(w, g, m, v, name):
    R, C = w.shape

    def body(w_ref, g_ref, m_ref, v_ref, d_ref, m2_ref, v2_ref):
        d_ref[...], m2_ref[...], v2_ref[...] = _adamw_math(w_ref[...], g_ref[...], m_ref[...], v_ref[...])

    cap = max(8, ADAMW_BLOCK_ELEMS // C)
    tr = R if R <= cap else next(t for t in range(cap - cap % 8, 7, -8) if R % t == 0)
    spec = pl.BlockSpec((tr, C), lambda i: (i, 0))
    return pl.pallas_call(
        body, name=name, grid=(R // tr,), in_specs=[spec] * 4, out_specs=[spec] * 3,
        out_shape=[jax.ShapeDtypeStruct((R, C), f32)] * 3, compiler_params=_cparams("parallel"),
    )(w, g, m, v)


def _adamw_rows(w, g, m, v):
    R, _, C = w.shape
    tr = R // 4

    def body(w_ref, g_ref, m_ref, v_ref, d_ref, m2_ref, v2_ref):
        d_ref[...], m2_ref[...], v2_ref[...] = _adamw_math(w_ref[...], g_ref[...], m_ref[...], v_ref[...])

    spec = pl.BlockSpec((tr, 1, C), lambda i: (i, 0, 0))
    return pl.pallas_call(
        body, name="adamw_w_in", grid=(R // tr,), in_specs=[spec] * 4, out_specs=[spec] * 3,
        out_shape=[jax.ShapeDtypeStruct((R, 1, C), f32)] * 3, compiler_params=_cparams("parallel"),
    )(w, g, m, v)


ADA_COLS = 6 * D_MODEL // N_CHIPS
ADA_TN = 512


COND_LANES = 512


def _prologue(cond, ada_w, ada_b, blob):
    R = blob.shape[0]

    def body(cond_ref, w_ref, b_ref, blob_ref, call_ref, mod_ref, wall_ref, mod_s, stage, gs, gr, ms, mr, ws, wr, local_sem):
        x, y, c = _place()
        start_w, finish_w = _gather_job(blob_ref, wall_ref, ws, wr, R)
        start_w()

        def rows(ref, px, py, pc):
            return ref.at[pl.ds(pl.multiple_of((4 * px + 2 * py + pc) * 8, 8), 8), :]

        mine = pltpu.make_async_copy(cond_ref, rows(call_ref, x, y, c), local_sem)
        mine.start()
        sends = [_remote(cond_ref, rows(call_ref, x, y, c), gs, gr, m - 1, _flip(x, y, c, m)) for m in range(1, N_DEV)]
        for cp in sends:
            cp.start()
        for m in range(1, N_DEV):
            peer = _flip(x, y, c, m)
            _remote(cond_ref, rows(call_ref, *peer), gs, gr, m - 1, peer).wait_recv()
        for cp in sends:
            cp.wait_send()
        mine.wait()

        lo = jnp.concatenate([call_ref[8 * d:8 * d + 1, :] for d in range(N_DEV)], axis=0)
        hi = jnp.concatenate([call_ref[8 * d + 1:8 * d + 2, :] for d in range(N_DEV)], axis=0)
        mod_all = (_dot_exact(lo * _sigmoid(lo), w_ref[0:COND_LANES, :]) + _dot_exact(hi * _sigmoid(hi), w_ref[COND_LANES:, :])
                   + b_ref[...])
        for d in range(N_DEV):
            mod_s[8 * d:8 * d + 8, :] = jnp.broadcast_to(mod_all[d:d + 1, :], (8, ADA_COLS))

        mine = pltpu.make_async_copy(rows(mod_s, x, y, c), mod_ref.at[2 * x + y], local_sem)
        mine.start()
        sends = []
        for m in range(1, N_CHIPS):
            peer = _flip(x, y, c, 2 * m)
            sends.append(_remote(rows(mod_s, *peer), mod_ref.at[2 * x + y], ms, mr, m - 1, peer))
        for cp in sends:
            cp.start()
        for m in range(1, N_CHIPS):
            px, py, pc = _flip(x, y, c, 2 * m)
            _remote(rows(mod_s, x, y, c), mod_ref.at[2 * px + py], ms, mr, m - 1, (px, py, pc)).wait_recv()
        for cp in sends:
            cp.wait_send()
        mine.wait()

        for k in range(R // STAGE_ROWS):
            part = pl.ds(STAGE_ROWS * k, STAGE_ROWS)
            cin = pltpu.make_async_copy(blob_ref.at[part, :], stage, local_sem)
            cin.start()
            cin.wait()
            cout = pltpu.make_async_copy(stage, wall_ref.at[2 * x + y, part, :], local_sem)
            cout.start()
            cout.wait()
        finish_w()

    return pl.pallas_call(
        body, name="prologue",
        out_shape=[jax.ShapeDtypeStruct((8 * N_DEV, COND_LANES), f32), jax.ShapeDtypeStruct((N_CHIPS, 8, ADA_COLS), f32),
                   jax.ShapeDtypeStruct((N_CHIPS, R, D_MODEL), bf16)],
        in_specs=[_VMEM_SPEC, _VMEM_SPEC, _VMEM_SPEC, _ANY_SPEC], out_specs=[_VMEM_SPEC, _VMEM_SPEC, _ANY_SPEC],
        scratch_shapes=[pltpu.VMEM((8 * N_DEV, ADA_COLS), f32), pltpu.VMEM((STAGE_ROWS, D_MODEL), bf16)]
        + _sems(N_DEV - 1) + _sems(N_CHIPS - 1) + _sems(6) + [pltpu.SemaphoreType.DMA],
        compiler_params=pltpu.CompilerParams(vmem_limit_bytes=VMEM_LIMIT),
    )(cond, ada_w, ada_b, blob)


def _ada_bwd(c_all, dmod, w, m, v):
    def body(c_ref, d_ref, w_ref, m_ref, v_ref, g_ref, dl_ref, m2_ref, v2_ref):
        cv = c_ref[...]
        g = lax.dot_general(cv * _sigmoid(cv), d_ref[...], (((0,), (0,)), ((), ())), preferred_element_type=f32,
                            precision=lax.Precision.HIGHEST)
        g_ref[...] = g
        dl_ref[...], m2_ref[...], v2_ref[...] = _adamw_math(w_ref[...], g, m_ref[...], v_ref[...])

    wspec = pl.BlockSpec((D_MODEL, ADA_TN), lambda j: (0, j))
    return pl.pallas_call(
        body, name="ada_bwd", grid=(ADA_COLS // ADA_TN,),
        in_specs=[_full((N_DEV, D_MODEL)), pl.BlockSpec((N_DEV, ADA_TN), lambda j: (0, j)), wspec, wspec, wspec],
        out_specs=[wspec] * 4, out_shape=[jax.ShapeDtypeStruct((D_MODEL, ADA_COLS), f32)] * 4,
        compiler_params=_cparams("parallel"),
    )(c_all, dmod, w, m, v)


SMALL_SLOTS = (("ada_b", 6144), ("ln_in_g", 1024), ("ln_in_b", 1024), ("conv_b", 1536), ("dt_bias", 128), ("a_log", 128),
               ("d_skip", 128), ("ssd_norm_w", 1024), ("attn_sinks", 128), ("ln1_g", 1024), ("ln1_b", 1024),
               ("b_ff1", 4096), ("b_ff2", 1024), ("ln2_g", 1024), ("ln2_b", 1024), ("conv_w", 6144), ("loss", 1024))
SMALL_N = sum(n for _, n in SMALL_SLOTS)
SMALL_OFF = {name: sum(n for _, n in SMALL_SLOTS[:i]) for i, (name, _) in enumerate(SMALL_SLOTS)}
SMALL_PARAMS = tuple(name for name, _ in SMALL_SLOTS[:15])
assert SMALL_N % 1024 == 0


def _small_pack(acc_in, acc_out, acc_mlp, db1, acc_conv, acc_ssd, dd_x, dsink, alog, e_mat):
    def body(in_ref, out_ref, mlp_ref, db1_ref, conv_ref, ssd_ref, dd_ref, sink_ref, al_ref, e_ref, o_ref):
        def put(name, val, at=0):
            off = SMALL_OFF[name] + at
            o_ref[:, off:off + val.shape[1]] = val

        for k, row in enumerate((in_ref[1:2, :], in_ref[0:1, :], out_ref[0:1, :], mlp_ref[A_SH2:A_SH2 + 1, :],
                                 mlp_ref[A_SC2:A_SC2 + 1, :], mlp_ref[A_G2:A_G2 + 1, :])):
            put("ada_b", row, D_MODEL * k)
        put("ln_in_g", in_ref[2:3, :])
        put("ln_in_b", in_ref[3:4, :])
        put("conv_b", conv_ref[4:5, :])
        put("dt_bias", ssd_ref[1:2, :])
        put("a_log", ssd_ref[0:1, :] * (-jnp.exp(al_ref[...])))
        put("d_skip", _dot_sel_nt(jnp.broadcast_to(dd_ref[...], (8, SSD_WIDTH)), e_ref[...])[0:1, :])
        put("ssd_norm_w", out_ref[1:2, :])
        put("attn_sinks", sink_ref[...])
        put("ln1_g", mlp_ref[A_LN1G:A_LN1G + 1, :])
        put("ln1_b", mlp_ref[A_LN1B:A_LN1B + 1, :])
        put("b_ff1", db1_ref[...])
        put("b_ff2", mlp_ref[A_B2:A_B2 + 1, :])
        put("ln2_g", mlp_ref[A_LN2G:A_LN2G + 1, :])
        put("ln2_b", mlp_ref[A_LN2B:A_LN2B + 1, :])
        for k in range(CONV_K):
            put("conv_w", conv_ref[k:k + 1, :], CONV_DIM * k)
        put("loss", mlp_ref[A_LOSS:A_LOSS + 1, :])

    return pl.pallas_call(body, name="small_pack", out_shape=jax.ShapeDtypeStruct((1, SMALL_N), f32),
                          compiler_params=_cparams())(acc_in, acc_out, acc_mlp, db1, acc_conv, acc_ssd, dd_x, dsink, alog, e_mat)


def _small_update(gathered, params, moms, vels):
    k = len(SMALL_PARAMS)

    def body(g_ref, *refs):
        w_refs, m_refs, v_refs, outs = refs[:k], refs[k:2 * k], refs[2 * k:3 * k], refs[3 * k:]

        def total(name, n):
            off = SMALL_OFF[name]
            g = g_ref[0:1, off:off + n]
            for i in range(1, N_DEV):
                g = g + g_ref[i:i + 1, off:off + n]
            return g

        for j, name in enumerate(SMALL_PARAMS):
            n = w_refs[j].shape[1]
            g = total(name, max(n, 128))[:, :n]
            outs[4 * j][...] = g
            outs[4 * j + 1][...], outs[4 * j + 2][...], outs[4 * j + 3][...] = _adamw_math(
                w_refs[j][...], g, m_refs[j][...], v_refs[j][...])
        outs[4 * k][...] = total("conv_w", CONV_K * CONV_DIM)
        outs[4 * k + 1][...] = total("loss", D_MODEL)

    shapes = [jax.ShapeDtypeStruct(p.shape, f32) for p in params for _ in range(4)]
    shapes += [jax.ShapeDtypeStruct((1, CONV_K * CONV_DIM), f32), jax.ShapeDtypeStruct((1, D_MODEL), f32)]
    return pl.pallas_call(body, name="small_update", out_shape=shapes,
                          compiler_params=_cparams())(gathered, *params, *moms, *vels)


def _place():
    return lax.axis_index("x"), lax.axis_index("y"), lax.axis_index("c")


def _flip(x, y, c, m):
    return (1 - x if m & 4 else x, 1 - y if m & 2 else y, 1 - c if m & 1 else c)


_VMEM_SPEC = pl.BlockSpec(memory_space=pltpu.VMEM)
_ANY_SPEC = pl.BlockSpec(memory_space=pl.ANY)


def _allgather8(v, name):
    n = v.shape[1]

    def body(v_ref, out_ref, send_sems, recv_sems, local_sem):
        x, y, c = _place()

        def rows(px, py, pc):
            return out_ref.at[pl.ds(pl.multiple_of((4 * px + 2 * py + pc) * 8, 8), 8), :]

        def copy(m, src, dst, to):
            return pltpu.make_async_remote_copy(src_ref=src, dst_ref=dst, send_sem=send_sems.at[m - 1],
                                                recv_sem=recv_sems.at[m - 1], device_id=to, device_id_type=MESH)

        mine = pltpu.make_async_copy(v_ref, rows(x, y, c), local_sem)
        mine.start()
        sends = [copy(m, v_ref, rows(x, y, c), _flip(x, y, c, m)) for m in range(1, N_DEV)]
        for cp in sends:
            cp.start()
        for m in range(1, N_DEV):
            peer = _flip(x, y, c, m)
            copy(m, v_ref, rows(*peer), peer).wait_recv()
        for cp in sends:
            cp.wait_send()
        mine.wait()

    return pl.pallas_call(
        body, name=name, out_shape=jax.ShapeDtypeStruct((8 * N_DEV, n), f32), in_specs=[_VMEM_SPEC],
        out_specs=_VMEM_SPEC,
        scratch_shapes=[pltpu.SemaphoreType.DMA((N_DEV - 1,)), pltpu.SemaphoreType.DMA((N_DEV - 1,)),
                        pltpu.SemaphoreType.DMA],
    )(v)


def _remote(src, dst, send_sems, recv_sems, k, to):
    return pltpu.make_async_remote_copy(src_ref=src, dst_ref=dst, send_sem=send_sems.at[k], recv_sem=recv_sems.at[k],
                                        device_id=to, device_id_type=MESH)


def _gather_job(blob_ref, out_ref, send_sems, recv_sems, R):
    x, y, c = _place()
    sib = (x, y, 1 - c)
    hr = R // 2

    def half(px, py, pc):
        return out_ref.at[2 * px + py, pl.ds(pl.multiple_of(pc * hr, 16), hr), :]

    my_half = blob_ref.at[pl.ds(pl.multiple_of(c * hr, 16), hr), :]

    def first():
        return [_remote(my_half, half(x, y, c), send_sems, recv_sems, m - 1, _flip(x, y, c, 2 * m))
                for m in range(1, N_CHIPS)]

    def start():
        for cp in first():
            cp.start()

    def finish():
        passed = []
        for m in range(1, N_CHIPS):
            px, py, pc = _flip(x, y, c, 2 * m)
            _remote(my_half, half(px, py, pc), send_sems, recv_sems, m - 1, (px, py, pc)).wait_recv()
            fwd = _remote(half(px, py, pc), half(px, py, pc), send_sems, recv_sems, 2 + m, sib)
            fwd.start()
            passed.append(fwd)
        for m in range(1, N_CHIPS):
            px, py, pc = _flip(x, y, c, 2 * m)
            _remote(my_half, half(px, py, 1 - pc), send_sems, recv_sems, 2 + m, sib).wait_recv()
        for cp in first() + passed:
            cp.wait_send()

    return start, finish


def _to_sibling_job(g_ref, out_ref, send_sems, recv_sems):
    x, y, c = _place()

    def cps():
        return [_remote(g_ref.at[j, 1 - c], out_ref.at[j], send_sems, recv_sems, j, (x, y, 1 - c)) for j in range(N_CHIPS)]

    def start():
        for cp in cps():
            cp.start()

    def wait():
        for cp in cps():
            cp.wait()

    return start, wait


def _to_chips_job(p_ref, out_ref, send_sems, recv_sems):
    x, y, c = _place()

    def cps():
        out = []
        for m in range(1, N_CHIPS):
            px, py, pc = _flip(x, y, c, 2 * m)
            out.append(_remote(p_ref.at[2 * px + py], out_ref.at[m - 1], send_sems, recv_sems, m - 1, (px, py, pc)))
        return out

    def start():
        for cp in cps():
            cp.start()

    def wait():
        for cp in cps():
            cp.wait()

    return start, wait


def _share_job(g_ref, send_sems, recv_sems, R):
    x, y, c = _place()

    def rows(pc):
        return g_ref.at[pl.ds(pl.multiple_of(pc * (R // 2), 8), R // 2), :]

    def start():
        _remote(rows(c), rows(c), send_sems, recv_sems, 0, (x, y, 1 - c)).start()

    def wait():
        _remote(rows(c), rows(1 - c), send_sems, recv_sems, 0, (x, y, 1 - c)).wait_recv()
        _remote(rows(c), rows(c), send_sems, recv_sems, 0, (x, y, 1 - c)).wait_send()

    return start, wait


def _sems(n):
    return [pltpu.SemaphoreType.DMA((n,)), pltpu.SemaphoreType.DMA((n,))]


def _rs_to_sibling(gb):
    def body(g_ref, out_ref, send_sems, recv_sems):
        start, wait = _to_sibling_job(g_ref, out_ref, send_sems, recv_sems)
        start()
        wait()

    return pl.pallas_call(
        body, name="rs_to_sibling", out_shape=jax.ShapeDtypeStruct((N_CHIPS,) + gb.shape[2:], bf16),
        in_specs=[_ANY_SPEC], out_specs=_ANY_SPEC, scratch_shapes=_sems(N_CHIPS),
    )(gb)


def _rs_share(g):
    R = g.shape[0]

    def body(g_ref, out_ref, send_sems, recv_sems):
        start, wait = _share_job(out_ref, send_sems, recv_sems, R)
        start()
        wait()

    return pl.pallas_call(
        body, name="rs_share", out_shape=jax.ShapeDtypeStruct(g.shape, f32), in_specs=[_ANY_SPEC],
        out_specs=_ANY_SPEC, input_output_aliases={0: 0}, scratch_shapes=_sems(1),
    )(g)


RS_TR = 256


def _rs_sum_pair(place, gb, recv, name):
    hr = gb.shape[2]

    def body(pl_ref, g_ref, r_ref, o_ref):
        o_ref[0] = (g_ref[0, 0].astype(f32) + r_ref[0].astype(f32)).astype(bf16)

    return pl.pallas_call(
        body, name=name,
        grid_spec=pltpu.PrefetchScalarGridSpec(
            num_scalar_prefetch=1, grid=(N_CHIPS, hr // RS_TR),
            in_specs=[pl.BlockSpec((1, 1, RS_TR, D_MODEL), lambda j, i, p: (j, p[0], i, 0)),
                      pl.BlockSpec((1, RS_TR, D_MODEL), lambda j, i, p: (j, i, 0))],
            out_specs=pl.BlockSpec((1, RS_TR, D_MODEL), lambda j, i, p: (j, i, 0))),
        out_shape=jax.ShapeDtypeStruct((N_CHIPS, hr, D_MODEL), bf16),
        compiler_params=_cparams("parallel", "parallel"),
    )(place, gb, recv)


def _rs_sum_chips(place, gb, recv_sib, recv_chips, name):
    hr = gb.shape[2]
    nt = hr // RS_TR

    def body(pl_ref, g_ref, r1_ref, r2_ref, o_ref):
        acc = g_ref[0, 0].astype(f32) + r1_ref[0].astype(f32)
        for k in range(N_CHIPS - 1):
            acc = acc + r2_ref[k].astype(f32)
        o_ref[...] = acc

    return pl.pallas_call(
        body, name=name,
        grid_spec=pltpu.PrefetchScalarGridSpec(
            num_scalar_prefetch=1, grid=(nt,),
            in_specs=[pl.BlockSpec((1, 1, RS_TR, D_MODEL), lambda i, p: (p[1], p[0], i, 0)),
                      pl.BlockSpec((1, RS_TR, D_MODEL), lambda i, p: (p[1], i, 0)),
                      pl.BlockSpec((N_CHIPS - 1, RS_TR, D_MODEL), lambda i, p: (0, i, 0))],
            out_specs=pl.BlockSpec((RS_TR, D_MODEL), lambda i, p: (p[0] * nt + i, 0))),
        out_shape=jax.ShapeDtypeStruct((2 * hr, D_MODEL), f32),
        compiler_params=_cparams("parallel"),
    )(place, gb, recv_sib, recv_chips)


def _pad128(v):
    v = v.reshape(1, -1)
    return jnp.pad(v, ((0, 0), (0, 128 - v.shape[1])))


def _row(v):
    return v.reshape(1, -1)


W_COLS = PROJ_WIDTH // N_CHIPS


def _g_in_blocks(gz, gxbc, gdt, gq, gkv):
    g = jnp.concatenate([gz, gxbc, gdt[:W_DT_ROWS], gq, gkv], axis=0)
    return jnp.pad(g.reshape(N_CHIPS, W_COLS, D_MODEL), ((0, 0), (0, D_MODEL - W_COLS), (0, 0)))


def kernel(x, c, ln_in_g, ln_in_b, ada_w, ada_b, w_in, conv_w, conv_b, dt_bias, a_log, d_skip, ssd_norm_w, attn_sinks, w_out, ln1_g, ln1_b, w_ff1, b_ff1, w_ff2, b_ff2, ln2_g, ln2_b, loss_target, m_ln_in_g, m_ln_in_b, m_ada_w, m_ada_b, m_w_in, m_conv_w, m_conv_b, m_dt_bias, m_a_log, m_d_skip, m_ssd_norm_w, m_attn_sinks, m_w_out, m_ln1_g, m_ln1_b, m_w_ff1, m_b_ff1, m_w_ff2, m_b_ff2, m_ln2_g, m_ln2_b, v_ln_in_g, v_ln_in_b, v_ada_w, v_ada_b, v_w_in, v_conv_w, v_conv_b, v_dt_bias, v_a_log, v_d_skip, v_ssd_norm_w, v_attn_sinks, v_w_out, v_ln1_g, v_ln1_b, v_w_ff1, v_b_ff1, v_w_ff2, v_b_ff2, v_ln2_g, v_ln2_b):
    xi, yi, ci = _place()
    chip = 2 * xi + yi
    place = jnp.stack([ci, chip]).astype(jnp.int32)
    x2, tgt = x[0], loss_target[0]

    def as_rows(a):
        return jnp.transpose(a, (2, 0, 1))

    def from_rows(a):
        return jnp.transpose(a, (1, 2, 0))

    cond = jnp.concatenate([c.reshape(2, COND_LANES), conv_w.reshape(3, COND_LANES), jnp.zeros((3, COND_LANES), f32)], axis=0)
    ada_b_mine = lax.dynamic_slice(ada_b, (0, chip * ADA_COLS), (1, ADA_COLS))
    blob_in = jnp.pad(w_in[0].T, ((0, D_MODEL - W_COLS), (0, 0))).astype(bf16)
    cond_all, mod_rows, wall_in = _prologue(cond, ada_w[0], ada_b_mine, blob_in)
    cond_all = cond_all.reshape(N_DEV, 8, COND_LANES)
    c_all = cond_all[:, 0:2].reshape(N_DEV, D_MODEL)
    conv_w_full = jnp.concatenate([cond_all[2 * j, 2:5].reshape(CONV_K, 384) for j in range(N_CHIPS)], axis=1)
    mod = jnp.concatenate([mod_rows[:, 0].reshape(6, D_MODEL), jnp.zeros((2, D_MODEL), f32)], axis=0)
    w_in_f = wall_in[:, :W_COLS].reshape(PROJ_WIDTH, D_MODEL)
    b_ff1w, b_ff2w, b_outw = w_ff1[0].astype(bf16), w_ff2[0].astype(bf16), w_out[0].astype(bf16)

    def with_mine(wall, mine):
        return lax.dynamic_update_slice(wall, mine[None], (chip, 0, 0))

    e_mat = _head_expand()
    dsk_x = jnp.repeat(d_skip[0], HEAD_DIM).reshape(1, SSD_WIDTH)
    dtb, alog = _pad128(dt_bias), _pad128(a_log)
    sinks = attn_sinks[0]
    lng, lnb = _row(ln_in_g), _row(ln_in_b)
    u1, z, xr, xc, q, kv, dtr, wall_ff1 = _inproj_fwd(x2, mod, lng, lnb, w_in_f, conv_w_full, conv_b, b_ff1w)
    y, prev_all, wall_ff2 = _ssd_fwd(xc, dtr, dtb, alog, dsk_x, e_mat, b_ff2w)
    o, lse, wall_out = _attn_fwd(q, kv, sinks, b_outw)
    wall_ff1, wall_ff2, wall_out = with_mine(wall_ff1, b_ff1w), with_mine(wall_ff2, b_ff2w), with_mine(wall_out, b_outw)
    yn, mix, r1 = _outproj_fwd(y, z, o, x2, mod, lng, lnb, ssd_norm_w, wall_out)

    dr1, u2, s_act, da, df, acc_mlp, db1 = _mlp_fwd_bwd(r1, tgt, mod, ln1_g, ln1_b, ln2_g, ln2_b, wall_ff1, b_ff1, wall_ff2,
                                                        b_ff2)
    ga = jnp.zeros((N_CHIPS, GA_ROWS, D_MODEL), bf16)
    ga = _wgrad_blob(ga, u2, da, "wgrad_ff1", lambda t, n: (n, t))
    ga = _wgrad_blob(ga, s_act, df, "wgrad_ff2", lambda t, n: (t // 2, 2 + t % 2))
    ga = ga.reshape(N_CHIPS, 2, GA_ROWS // 2, D_MODEL)
    dy, dz, do, dmix, acc_out, a_sib = _outproj_bwd(dr1, mix, y, z, mod, ssd_norm_w, wall_out, ga)
    gc = jnp.zeros((N_CHIPS, GC_ROWS, D_MODEL), bf16)
    gc = _wgrad_blob(gc, yn, dmix, "wgrad_out_y", lambda t, n: (t, 0))
    gc = _wgrad_blob(gc, o, dmix, "wgrad_out_o", lambda t, n: (2 + t, 0))
    gc = gc.reshape(N_CHIPS, 2, GC_ROWS // 2, D_MODEL)
    a_pair = _rs_sum_pair(place, ga, a_sib, "rs_sum_pair_a")
    dq, dkv, dsink, a_chips, c_sib = _attn_bwd(q, kv, do, lse, sinks, a_pair, gc)
    g_a = _rs_sum_chips(place, ga, a_sib, a_chips, "rs_sum_chips_a")
    c_pair = _rs_sum_pair(place, gc, c_sib, "rs_sum_pair_c")
    dxc, ddt, acc_ssd, dd_x, g_a, c_chips = _ssd_bwd(xc, dtr, dy, prev_all, dtb, alog, dsk_x, e_mat, g_a, c_pair)
    g_c = _rs_sum_chips(place, gc, c_sib, c_chips, "rs_sum_chips_c")
    dxr, acc_conv, g_c = _conv_bwd(dxc, xr, conv_w_full, conv_b, g_c)
    gb = _g_in_blocks(_wgrad(dz, u1, "wgrad_in_z"), _wgrad(dxr, u1, "wgrad_in_xbc"),
                      _wgrad(ddt.astype(bf16), u1, "wgrad_in_dt"), _wgrad(dq, u1, "wgrad_in_q"),
                      _wgrad(dkv, u1, "wgrad_in_kv")).reshape(N_CHIPS, 2, GB_ROWS // 2, D_MODEL)
    b_sib = _rs_to_sibling(gb)
    b_pair = _rs_sum_pair(place, gb, b_sib, "rs_sum_pair_b")
    grad_x, acc_in, b_chips = _inproj_bwd(dz, dxr, dq, dkv, ddt, dr1, x2, mod, lng, lnb, w_in_f, b_pair)
    g_b = _rs_share(_rs_sum_chips(place, gb, b_sib, b_chips, "rs_sum_chips_b"))

    packed = _small_pack(acc_in, acc_out, acc_mlp, db1, acc_conv, acc_ssd, dd_x, dsink, alog, e_mat)
    small_all = _allgather8(packed.reshape(8, SMALL_N // 8), "gather_small").reshape(N_DEV, SMALL_N)
    given = dict(ada_b=(ada_b, m_ada_b, v_ada_b), ln_in_g=(ln_in_g, m_ln_in_g, v_ln_in_g), ln_in_b=(ln_in_b, m_ln_in_b, v_ln_in_b),
                 conv_b=(conv_b, m_conv_b, v_conv_b), dt_bias=(dt_bias, m_dt_bias, v_dt_bias), a_log=(a_log, m_a_log, v_a_log),
                 d_skip=(d_skip, m_d_skip, v_d_skip), ssd_norm_w=(ssd_norm_w, m_ssd_norm_w, v_ssd_norm_w),
                 attn_sinks=(attn_sinks, m_attn_sinks, v_attn_sinks), ln1_g=(ln1_g, m_ln1_g, v_ln1_g),
                 ln1_b=(ln1_b, m_ln1_b, v_ln1_b), b_ff1=(b_ff1, m_b_ff1, v_b_ff1), b_ff2=(b_ff2, m_b_ff2, v_b_ff2),
                 ln2_g=(ln2_g, m_ln2_g, v_ln2_g), ln2_b=(ln2_b, m_ln2_b, v_ln2_b))
    upd = _small_update(small_all, *([_row(given[n][i]) for n in SMALL_PARAMS] for i in range(3)))
    small_res = {n: [t.reshape(given[n][0].shape) for t in upd[4 * j:4 * j + 4]] for j, n in enumerate(SMALL_PARAMS)}
    g_conv_all, loss_lanes = upd[4 * len(SMALL_PARAMS)], upd[4 * len(SMALL_PARAMS) + 1]
    loss = jnp.sum(loss_lanes)

    dmod_mine = lax.dynamic_slice(small_all[:, :6 * D_MODEL], (0, chip * ADA_COLS), (N_DEV, ADA_COLS))
    big = {"ada_w": [t[None] for t in _ada_bwd(c_all, dmod_mine, ada_w[0], m_ada_w[0], v_ada_w[0])]}

    g_conv = lax.dynamic_slice(g_conv_all.reshape(CONV_K, CONV_DIM), (0, chip * 384), (CONV_K, 384))
    big["conv_w"] = [t[None] for t in (g_conv, *_adamw(conv_w[0], g_conv, m_conv_w[0], v_conv_w[0], "adamw_conv_w"))]

    g_rows = g_b[:W_COLS].reshape(W_COLS, 1, D_MODEL)
    big["w_in"] = [from_rows(t) for t in (g_rows, *_adamw_rows(as_rows(w_in), g_rows, as_rows(m_w_in), as_rows(v_w_in)))]
    for name, g, (w, m, v) in (("w_out", g_c, (w_out, m_w_out, v_w_out)),
                               ("w_ff1", g_a[:D_MODEL], (w_ff1, m_w_ff1, v_w_ff1)),
                               ("w_ff2", g_a[D_MODEL:GA_ROWS], (w_ff2, m_w_ff2, v_w_ff2))):
        big[name] = [t[None] for t in (g, *_adamw(w[0], g, m[0], v[0], "adamw_" + name))]

    order = ("ln_in_g", "ln_in_b", "ada_w", "ada_b", "w_in", "conv_w", "conv_b", "dt_bias", "a_log", "d_skip", "ssd_norm_w",
             "attn_sinks", "w_out", "ln1_g", "ln1_b", "w_ff1", "b_ff1", "w_ff2", "b_ff2", "ln2_g", "ln2_b")
    res = {**small_res, **big}
    return (loss, grad_x[None], *[res[n][k] for k in range(4) for n in order])
```

```python
import functools
import math

import numpy as np
import jax
import jax.numpy as jnp
from jax import lax
from jax.experimental import pallas as pl
from jax.experimental.pallas import tpu as pltpu

f32 = jnp.float32
bf16 = jnp.bfloat16

D_MODEL = 1024
SSD_WIDTH = 1024
SSD_HEADS = 16
HEAD_DIM = 64
SSD_STATE = 128
SSD_GROUPS = 2
CHUNK = 128
CONV_K = 4
CONV_DIM = 1536
ATTN_HEADS = 16
D_FF = 4096
PROJ_WIDTH = 3856
ALPHA = 2.0 ** 0.25
LN_EPS = 1e-5
RMS_EPS = 1e-5
ATTN_SCALE = HEAD_DIM ** -0.5
NEG = -1e30

ADAM_LR = 0.001
ADAM_B1 = 0.9
ADAM_B2 = 0.999
ADAM_EPS = 1e-08
ADAM_WD = 0.01
ADAM_STEP = 10

W_Z, W_XBC, W_Q, W_KV = slice(0, 1024), slice(1024, 2560), slice(2576, 3600), slice(3600, 3856)
W_DT = slice(2560, 2688)
W_DT_ROWS = 16
GA_ROWS = 2048
GB_ROWS = 1024
GC_ROWS = 512
WG_TM = 512
STAGE_ROWS = 512
ADAMW_BLOCK_ELEMS = 1 << 18
DENSE_TM = 512
N_CHIPS = 4
N_DEV = 8
VMEM_LIMIT = 56 * 1024 * 1024
MESH = pl.DeviceIdType.MESH

ALIBI_SLOPES = tuple(2.0 ** (-8.0 / ATTN_HEADS * (i + 1)) for i in range(ATTN_HEADS))


def _cparams(*sem):
    return pltpu.CompilerParams(dimension_semantics=sem, vmem_limit_bytes=VMEM_LIMIT)


def _sigmoid(x):
    return 1.0 / (1.0 + jnp.exp(-x))


def _softplus(x):
    return jnp.maximum(x, 0.0) + jnp.log1p(jnp.exp(-jnp.abs(x)))


def _ln_stats(x):
    mu = jnp.mean(x, axis=-1, keepdims=True)
    xc = x - mu
    var = jnp.mean(xc * xc, axis=-1, keepdims=True)
    rstd = lax.rsqrt(var + LN_EPS)
    return xc * rstd, rstd


def _ln_bwd(dy, xhat, rstd, g):
    dxh = dy * g
    m1 = jnp.mean(dxh, axis=-1, keepdims=True)
    m2 = jnp.mean(dxh * xhat, axis=-1, keepdims=True)
    return rstd * (dxh - m1 - xhat * m2)


def _dot(a, b):
    return jnp.dot(a, b, preferred_element_type=f32)


def _dot_nt(a, b):
    return lax.dot_general(a, b, (((1,), (1,)), ((), ())), preferred_element_type=f32)


def _dot_tn(a, b):
    return lax.dot_general(a, b, (((0,), (0,)), ((), ())), preferred_element_type=f32)


def _dot_exact(a, b):
    return jnp.dot(a, b, preferred_element_type=f32, precision=lax.Precision.HIGHEST)


def _split3(v):
    hi = v.astype(bf16)
    r1 = v - hi.astype(f32)
    mid = r1.astype(bf16)
    lo = (r1 - mid.astype(f32)).astype(bf16)
    return hi, mid, lo


def _sel_dot(sel, v):
    hi, mid, lo = _split3(v)
    return _dot(sel, hi) + _dot(sel, mid) + _dot(sel, lo)


def _dot_sel(v, sel):
    hi, mid, lo = _split3(v)
    return _dot(hi, sel) + _dot(mid, sel) + _dot(lo, sel)


def _dot_sel_nt(v, sel):
    hi, mid, lo = _split3(v)
    return _dot_nt(hi, sel) + _dot_nt(mid, sel) + _dot_nt(lo, sel)


def _full(shape):
    nd = len(shape)
    return pl.BlockSpec(shape, lambda *_: (0,) * nd)


def _resident(shape):
    nd = len(shape)
    return pl.BlockSpec(shape, lambda *_: (0,) * nd, pipeline_mode=pl.Buffered(1))


def _rows(tm, n):
    return pl.BlockSpec((tm, n), lambda i: (i, 0))


def _inproj_fwd(x, mod, ln_g, ln_b, w_in_t, conv_w, conv_b, blob):
    L = x.shape[0]
    tm = DENSE_TM
    nt = L // tm
    R = blob.shape[0]

    def body(x_ref, mod_ref, g_ref, b_ref, w_ref, cw_ref, cb_ref, blob_ref,
             u1_ref, z_ref, xr_ref, xc_ref, q_ref, kv_ref, dt_ref, wall_ref, halo, buf, send_sems, recv_sems):
        start, finish = _gather_job(blob_ref, wall_ref, send_sems, recv_sems, R)

        @pl.when(pl.program_id(0) == 0)
        def _():
            halo[...] = jnp.zeros_like(halo)
            start()

        xhat, _ = _ln_stats(x_ref[...])
        h0 = xhat * g_ref[...] + b_ref[...]
        u1 = (h0 * (1.0 + mod_ref[1:2, :]) + mod_ref[0:1, :]).astype(bf16)
        u1_ref[...] = u1
        z_ref[...] = _dot_nt(u1, w_ref[W_Z, :])
        xr = _dot_nt(u1, w_ref[W_XBC, :])
        xr_ref[...] = xr
        q_ref[...] = _dot_nt(u1, w_ref[W_Q, :]).astype(bf16)
        kv_ref[...] = _dot_nt(u1, w_ref[W_KV, :]).astype(bf16)
        dt_ref[...] = _dot_nt(u1, w_ref[W_DT, :])
        buf[0:8, :] = halo[...]
        buf[8:8 + tm, :] = xr
        pre = cb_ref[...] + cw_ref[0:1, :] * buf[5:5 + tm, :]
        for k in range(1, CONV_K):
            pre = pre + cw_ref[k:k + 1, :] * buf[5 + k:5 + k + tm, :]
        xc_ref[...] = pre * _sigmoid(pre)
        halo[...] = xr[tm - 8:tm, :]

        @pl.when(pl.program_id(0) == nt - 1)
        def _():
            finish()

    return pl.pallas_call(
        body, name="inproj_fwd", grid=(nt,),
        in_specs=[_rows(tm, D_MODEL), _full((8, D_MODEL)), _full((1, D_MODEL)), _full((1, D_MODEL)),
                  _resident((PROJ_WIDTH, D_MODEL)), _full((CONV_K, CONV_DIM)), _full((1, CONV_DIM)), _ANY_SPEC],
        out_specs=[_rows(tm, D_MODEL), _rows(tm, D_MODEL), _rows(tm, CONV_DIM), _rows(tm, CONV_DIM),
                   _rows(tm, D_MODEL), _rows(tm, 256), _rows(tm, 128), _ANY_SPEC],
        out_shape=[jax.ShapeDtypeStruct((L, D_MODEL), bf16), jax.ShapeDtypeStruct((L, D_MODEL), f32),
                   jax.ShapeDtypeStruct((L, CONV_DIM), f32), jax.ShapeDtypeStruct((L, CONV_DIM), f32),
                   jax.ShapeDtypeStruct((L, D_MODEL), bf16), jax.ShapeDtypeStruct((L, 256), bf16),
                   jax.ShapeDtypeStruct((L, 128), f32), jax.ShapeDtypeStruct((N_CHIPS, R, D_MODEL), bf16)],
        scratch_shapes=[pltpu.VMEM((8, CONV_DIM), f32), pltpu.VMEM((tm + 8, CONV_DIM), f32),
                        pltpu.SemaphoreType.DMA((6,)), pltpu.SemaphoreType.DMA((6,))],
        compiler_params=_cparams("arbitrary"),
    )(x, mod, ln_g, ln_b, w_in_t, conv_w, conv_b, blob)


def _head_expand():
    e = np.zeros((128, SSD_WIDTH), np.float32)
    for h in range(SSD_HEADS):
        e[h, h * HEAD_DIM:(h + 1) * HEAD_DIM] = 1.0
    return jnp.asarray(e, dtype=bf16)


def _ssd_chunk_common(dt_raw, dtb, a_row, e_mat):
    T = CHUNK
    lane = lax.broadcasted_iota(jnp.int32, (T, 128), 1)
    dt = jnp.where(lane < SSD_HEADS, _softplus(dt_raw + dtb), 0.0)
    a = dt * a_row
    r = lax.broadcasted_iota(jnp.int32, (T, T), 0)
    c = lax.broadcasted_iota(jnp.int32, (T, T), 1)
    tril = (c <= r).astype(bf16)
    cum = _sel_dot(tril, a)
    dtx = _dot_sel(dt, e_mat)
    cumx = _dot_sel(cum, e_mat)
    return dt, a, cum, dtx, cumx, r, c


def _ssd_fwd(xc, dt_raw, dt_bias, a_log, d_skip_x, e_mat, blob):
    L = xc.shape[0]
    nc = L // CHUNK
    T = CHUNK
    R = blob.shape[0]

    def body(xc_ref, dt_ref, dtb_ref, al_ref, dsk_ref, e_ref, blob_ref, y_ref, prev_ref, wall_ref, st, send_sems, recv_sems):
        start, finish = _gather_job(blob_ref, wall_ref, send_sems, recv_sems, R)

        @pl.when(pl.program_id(0) == 0)
        def _():
            st[...] = jnp.zeros_like(st)
            start()

        @pl.when(pl.program_id(0) == nc - 1)
        def _():
            finish()

        a_row = -jnp.exp(al_ref[...])
        lane1 = lax.broadcasted_iota(jnp.int32, (1, 128), 1)
        a_row = jnp.where(lane1 < SSD_HEADS, a_row, 0.0)
        dt, a, cum, dtx, cumx, r, c = _ssd_chunk_common(dt_ref[...], dtb_ref[...], a_row, e_ref[...])
        cum_t = cum.T
        ex = jnp.exp(cumx)
        last = cumx[T - 1:T, :]
        wx = jnp.exp(last - cumx)
        cdx = jnp.exp(last)
        xs = xc_ref[:, 0:SSD_WIDTH]
        X = xs * dtx
        Xb = X.astype(bf16)
        Xd = (X * wx).astype(bf16)
        prev = st[...]
        prev_ref[0] = prev
        prevb = prev.astype(bf16)
        tri = c <= r
        lane = lax.broadcasted_iota(jnp.int32, (T, 128), 1)
        y_blocks = []
        new_states = []
        for g in range(SSD_GROUPS):
            Bg = xc_ref[:, 1024 + 128 * g:1152 + 128 * g].astype(bf16)
            Cg = xc_ref[:, 1280 + 128 * g:1408 + 128 * g].astype(bf16)
            G = _dot_nt(Cg, Bg)
            yoff = _dot(Cg, prevb[:, 512 * g:512 * (g + 1)])
            new_states.append(_dot_tn(Bg, Xd[:, 512 * g:512 * (g + 1)]))
            for j in range(4):
                blk = 4 * g + j
                Xblk = Xb[:, 128 * blk:128 * (blk + 1)]
                ys = []
                for half in range(2):
                    h = 2 * blk + half
                    seg = jnp.minimum(cum[:, h:h + 1] - cum_t[h:h + 1, :], 0.0)
                    M = jnp.where(tri, G * jnp.exp(seg), 0.0).astype(bf16)
                    ys.append(_dot(M, Xblk))
                yd = jnp.where(lane < HEAD_DIM, ys[0], ys[1])
                sl = slice(128 * blk, 128 * (blk + 1))
                y_blocks.append(yd + ex[:, sl] * yoff[:, 128 * j:128 * (j + 1)] + dsk_ref[:, sl] * xs[:, sl])
        y_ref[...] = jnp.concatenate(y_blocks, axis=1)
        st[...] = prev * cdx + jnp.concatenate(new_states, axis=1)

    return pl.pallas_call(
        body, name="ssd_fwd", grid=(nc,),
        in_specs=[_rows(T, CONV_DIM), _rows(T, 128), _full((1, 128)), _full((1, 128)), _full((1, SSD_WIDTH)),
                  _full((128, SSD_WIDTH)), _ANY_SPEC],
        out_specs=[_rows(T, SSD_WIDTH), pl.BlockSpec((1, SSD_STATE, SSD_WIDTH), lambda i: (i, 0, 0)), _ANY_SPEC],
        out_shape=[jax.ShapeDtypeStruct((L, SSD_WIDTH), f32), jax.ShapeDtypeStruct((nc, SSD_STATE, SSD_WIDTH), f32),
                   jax.ShapeDtypeStruct((N_CHIPS, R, D_MODEL), bf16)],
        scratch_shapes=[pltpu.VMEM((SSD_STATE, SSD_WIDTH), f32)] + _sems(6),
        compiler_params=_cparams("arbitrary"),
    )(xc, dt_raw, dt_bias, a_log, d_skip_x, e_mat, blob)


def _kv_halves(kv_prev, kv_cur, first):
    kv = jnp.concatenate([jnp.where(first, 0.0, kv_prev.astype(f32)), kv_cur.astype(f32)], axis=0)
    lane = lax.broadcasted_iota(jnp.int32, (2 * CHUNK, 128), 1)
    lo = lane < HEAD_DIM
    out = []
    for g in range(2):
        per_half = []
        for half in range(2):
            both = []
            for t in (kv[:, 0:128], kv[:, 128:256]):
                src = t if g == half else pltpu.roll(t, HEAD_DIM, 1)
                both.append(jnp.where(lo if half == 0 else ~lo, src, 0.0).astype(bf16))
            per_half.append(tuple(both))
        out.append(per_half)
    return out


def _attn_masks(first):
    r = lax.broadcasted_iota(jnp.int32, (CHUNK, 2 * CHUNK), 0)
    c = lax.broadcasted_iota(jnp.int32, (CHUNK, 2 * CHUNK), 1)
    dist = r + CHUNK - c
    valid = (dist >= 0) & (dist < CHUNK) & ((c >= CHUNK) | jnp.logical_not(first))
    return dist.astype(f32), valid


def _head_stack(g, half, sink_ref):
    blks = [4 * g + i for i in range(4)]
    heads = [2 * b + half for b in blks]
    slope = jnp.concatenate([jnp.full((CHUNK, 1), ALIBI_SLOPES[h], f32) for h in heads], axis=0)
    sink = jnp.concatenate([jnp.full((CHUNK, 1), sink_ref[h], f32) for h in heads], axis=0)
    return blks, heads, slope, sink


def _attn_fwd(q, kv, sinks, blob):
    L = q.shape[0]
    nb = L // CHUNK
    T = CHUNK
    R = blob.shape[0]

    def body(sink_ref, q_ref, kvp_ref, kvc_ref, blob_ref, o_ref, lse_ref, wall_ref, send_sems, recv_sems):
        first = pl.program_id(0) == 0
        start, finish = _gather_job(blob_ref, wall_ref, send_sems, recv_sems, R)

        @pl.when(first)
        def _():
            start()

        @pl.when(pl.program_id(0) == nb - 1)
        def _():
            finish()

        ext = _kv_halves(kvp_ref[...], kvc_ref[...], first)
        dist, valid = _attn_masks(first)
        lane = lax.broadcasted_iota(jnp.int32, (T, 128), 1)
        lse = jnp.zeros((T, 128), f32)
        o_blocks = []
        for blk in range(8):
            qb = q_ref[:, 128 * blk:128 * (blk + 1)]
            acc = None
            for half in range(2):
                h = 2 * blk + half
                k_ext, v_ext = ext[h // 8][half]
                s = _dot_nt(qb, k_ext) * ATTN_SCALE - ALIBI_SLOPES[h] * dist
                s = jnp.where(valid, s, NEG)
                sink = sink_ref[h]
                m = jnp.maximum(jnp.max(s, axis=-1, keepdims=True), sink)
                p = jnp.exp(s - m)
                den = jnp.sum(p, axis=-1, keepdims=True) + jnp.exp(sink - m)
                pn = (p * (1.0 / den)).astype(bf16)
                oh = _dot(pn, v_ext)
                acc = oh if acc is None else acc + oh
                lse = jnp.where(lane == h, m + jnp.log(den), lse)
            o_blocks.append(acc.astype(bf16))
        o_ref[...] = jnp.concatenate(o_blocks, axis=1)
        lse_ref[...] = lse

    return pl.pallas_call(
        body, name="attn_fwd", grid=(nb,),
        in_specs=[pl.BlockSpec(memory_space=pltpu.SMEM), _rows(T, D_MODEL),
                  pl.BlockSpec((T, 256), lambda i: (jnp.maximum(i - 1, 0), 0)), _rows(T, 256), _ANY_SPEC],
        out_specs=[_rows(T, D_MODEL), _rows(T, 128), _ANY_SPEC],
        out_shape=[jax.ShapeDtypeStruct((L, D_MODEL), bf16), jax.ShapeDtypeStruct((L, 128), f32),
                   jax.ShapeDtypeStruct((N_CHIPS, R, D_MODEL), bf16)],
        scratch_shapes=_sems(6),
        compiler_params=_cparams("arbitrary"),
    )(sinks, q, kv, kv, blob)


def _gated_norm(y, z, w):
    sz = _sigmoid(z)
    hg = y * (z * sz)
    ns, rss = [], []
    for g in range(SSD_GROUPS):
        hs = hg[:, 512 * g:512 * (g + 1)]
        rs = lax.rsqrt(jnp.mean(hs * hs, axis=-1, keepdims=True) + RMS_EPS)
        ns.append(hs * rs)
        rss.append(rs)
    n = jnp.concatenate(ns, axis=1)
    return n * w, n, rss, sz


def _outproj_fwd(y, z, o, x, mod, ln_g, ln_b, norm_w, w_out):
    L = x.shape[0]
    tm = DENSE_TM

    def body(y_ref, z_ref, o_ref, x_ref, mod_ref, g_ref, b_ref, nw_ref, w_ref, yn_ref, mix_ref, r1_ref):
        yn, _, _, _ = _gated_norm(y_ref[...], z_ref[...], nw_ref[...])
        ynb = yn.astype(bf16)
        yn_ref[...] = ynb
        mix = (_dot(ynb[:, 0:512], w_ref[0]) + _dot(ynb[:, 512:1024], w_ref[1])
               + _dot(o_ref[:, 0:512], w_ref[2]) + _dot(o_ref[:, 512:1024], w_ref[3]))
        mix_ref[...] = mix
        xhat, _ = _ln_stats(x_ref[...])
        h0 = xhat * g_ref[...] + b_ref[...]
        r1_ref[...] = ALPHA * h0 + (1.0 + mod_ref[2:3, :]) * mix

    v = _full((1, D_MODEL))
    return pl.pallas_call(
        body, name="outproj_fwd", grid=(L // tm,),
        in_specs=[_rows(tm, D_MODEL), _rows(tm, D_MODEL), _rows(tm, D_MODEL), _rows(tm, D_MODEL),
                  _full((8, D_MODEL)), v, v, v, _resident((N_CHIPS, 512, D_MODEL))],
        out_specs=[_rows(tm, D_MODEL)] * 3,
        out_shape=[jax.ShapeDtypeStruct((L, D_MODEL), bf16), jax.ShapeDtypeStruct((L, D_MODEL), f32),
                   jax.ShapeDtypeStruct((L, D_MODEL), f32)],
        compiler_params=_cparams("parallel"),
    )(y, z, o, x, mod, ln_g, ln_b, norm_w, w_out)


A_LN2G, A_LN2B, A_G2, A_B2, A_SC2, A_SH2, A_LN1G, A_LN1B, A_LOSS = range(9)


def _mlp_fwd_bwd(r1, target, mod, ln1_g, ln1_b, ln2_g, ln2_b, w1, b1, w2, b2):
    L = r1.shape[0]
    tm = 256
    nj = D_FF // 1024

    def body(r1_ref, t_ref, mod_ref, g1_ref, bb1_ref, g2_ref, bb2_ref, w1_ref, b1_ref, w2_ref, b2_ref,
             dr1_ref, u2_ref, s_ref, da_ref, df_ref, acc_ref, db1_ref, hr):
        @pl.when(pl.program_id(0) == 0)
        def _():
            acc_ref[...] = jnp.zeros_like(acc_ref)
            db1_ref[...] = jnp.zeros_like(db1_ref)

        sc2, sh2, gate2 = mod_ref[4:5, :], mod_ref[3:4, :], mod_ref[5:6, :]
        xhat1, rstd1 = _ln_stats(r1_ref[...])
        h1 = xhat1 * g1_ref[...] + bb1_ref[...]
        u2f = h1 * (1.0 + sc2) + sh2
        u2 = u2f.astype(bf16)
        u2_ref[...] = u2
        f = jnp.zeros((tm, D_MODEL), f32) + b2_ref[...]
        for j in range(nj):
            cs = slice(1024 * j, 1024 * (j + 1))
            a = _dot(u2, w1_ref[j]) + b1_ref[:, cs]
            hrj = jnp.maximum(a, 0.0)
            hr[:, cs] = hrj
            sj = (hrj * hrj).astype(bf16)
            s_ref[:, cs] = sj
            f = f + _dot(sj, w2_ref[j])
        r2 = ALPHA * h1 + (1.0 + gate2) * f
        xhat2, rstd2 = _ln_stats(r2)
        h2 = xhat2 * g2_ref[...] + bb2_ref[...]
        diff = h2 - t_ref[...]
        dh2 = diff * (1.0 / D_MODEL)

        def add(row, val):
            acc_ref[row:row + 1, :] += jnp.sum(val, axis=0, keepdims=True)

        add(A_LOSS, diff * diff * (0.5 / D_MODEL))
        add(A_LN2G, dh2 * xhat2)
        add(A_LN2B, dh2)
        dr2 = _ln_bwd(dh2, xhat2, rstd2, g2_ref[...])
        add(A_G2, dr2 * f)
        df = dr2 * (1.0 + gate2)
        add(A_B2, df)
        dfb = df.astype(bf16)
        df_ref[...] = dfb
        du2 = jnp.zeros((tm, D_MODEL), f32)
        for j in range(nj):
            cs = slice(1024 * j, 1024 * (j + 1))
            ds = _dot_nt(dfb, w2_ref[j])
            daj = ds * (2.0 * hr[:, cs])
            db1_ref[:, cs] += jnp.sum(daj, axis=0, keepdims=True)
            dajb = daj.astype(bf16)
            da_ref[:, cs] = dajb
            du2 = du2 + _dot_nt(dajb, w1_ref[j])
        add(A_SC2, du2 * h1)
        add(A_SH2, du2)
        dh1 = ALPHA * dr2 + du2 * (1.0 + sc2)
        add(A_LN1G, dh1 * xhat1)
        add(A_LN1B, dh1)
        dr1_ref[...] = _ln_bwd(dh1, xhat1, rstd1, g1_ref[...])

    v = _full((1, D_MODEL))
    return pl.pallas_call(
        body, name="mlp_fwd_bwd", grid=(L // tm,),
        in_specs=[_rows(tm, D_MODEL), _rows(tm, D_MODEL), _full((8, D_MODEL)), v, v, v, v,
                  _resident((N_CHIPS, D_MODEL, D_MODEL)), _full((1, D_FF)), _resident((N_CHIPS, D_MODEL, D_MODEL)), v],
        out_specs=[_rows(tm, D_MODEL), _rows(tm, D_MODEL), _rows(tm, D_FF), _rows(tm, D_FF), _rows(tm, D_MODEL),
                   _full((16, D_MODEL)), _full((1, D_FF))],
        out_shape=[jax.ShapeDtypeStruct((L, D_MODEL), f32), jax.ShapeDtypeStruct((L, D_MODEL), bf16),
                   jax.ShapeDtypeStruct((L, D_FF), bf16), jax.ShapeDtypeStruct((L, D_FF), bf16),
                   jax.ShapeDtypeStruct((L, D_MODEL), bf16), jax.ShapeDtypeStruct((16, D_MODEL), f32),
                   jax.ShapeDtypeStruct((1, D_FF), f32)],
        scratch_shapes=[pltpu.VMEM((tm, D_FF), f32)],
        compiler_params=_cparams("arbitrary"),
    )(r1, target, mod, ln1_g, ln1_b, ln2_g, ln2_b, w1, b1, w2, b2)


def _wgrad(a, b, name):
    L, M = a.shape
    N = b.shape[1]
    tm = min(M, 512)
    tn = next(t for t in (1024, 768, 512, 256, 128) if N % t == 0)

    def body(a_ref, b_ref, o_ref):
        o_ref[...] = _dot_tn(a_ref[...], b_ref[...]).astype(bf16)

    return pl.pallas_call(
        body, name=name, grid=(M // tm, N // tn),
        in_specs=[pl.BlockSpec((L, tm), lambda i, j: (0, i)), pl.BlockSpec((L, tn), lambda i, j: (0, j))],
        out_specs=pl.BlockSpec((tm, tn), lambda i, j: (i, j)),
        out_shape=jax.ShapeDtypeStruct((M, N), bf16),
        compiler_params=_cparams("parallel", "parallel"),
    )(a, b)


def _wgrad_blob(blob, a, b, name, place_of):
    L, M = a.shape
    N = b.shape[1]

    def body(blob_ref, a_ref, b_ref, o_ref):
        o_ref[0] = _dot_tn(a_ref[...], b_ref[...]).astype(bf16)

    return pl.pallas_call(
        body, name=name, grid=(M // WG_TM, N // D_MODEL),
        in_specs=[pl.BlockSpec(memory_space=pl.ANY), pl.BlockSpec((L, WG_TM), lambda t, n: (0, t)),
                  pl.BlockSpec((L, D_MODEL), lambda t, n: (0, n))],
        out_specs=pl.BlockSpec((1, WG_TM, D_MODEL), lambda t, n: (*place_of(t, n), 0)),
        out_shape=jax.ShapeDtypeStruct(blob.shape, bf16), input_output_aliases={0: 0},
        compiler_params=_cparams("parallel", "parallel"),
    )(blob, a, b)


def _outproj_bwd(dr1, mix, y, z, mod, norm_w, w_out, gb):
    L = dr1.shape[0]
    tm = DENSE_TM
    nt = L // tm

    def body(dr1_ref, mix_ref, y_ref, z_ref, mod_ref, nw_ref, w_ref, gb_ref,
             dy_ref, dz_ref, do_ref, dmix_ref, acc_ref, sib_ref, send_sems, recv_sems):
        start, wait = _to_sibling_job(gb_ref, sib_ref, send_sems, recv_sems)

        @pl.when(pl.program_id(0) == 0)
        def _():
            acc_ref[...] = jnp.zeros_like(acc_ref)
            start()

        @pl.when(pl.program_id(0) == nt - 1)
        def _():
            wait()

        dr1 = dr1_ref[...]
        acc_ref[0:1, :] += jnp.sum(dr1 * mix_ref[...], axis=0, keepdims=True)
        dmix = (dr1 * (1.0 + mod_ref[2:3, :])).astype(bf16)
        dmix_ref[...] = dmix
        dyn = jnp.concatenate([_dot_nt(dmix, w_ref[0]), _dot_nt(dmix, w_ref[1])], axis=1)
        do_ref[...] = jnp.concatenate([_dot_nt(dmix, w_ref[2]), _dot_nt(dmix, w_ref[3])], axis=1).astype(bf16)
        yv, zv = y_ref[...], z_ref[...]
        _, n, rss, sz = _gated_norm(yv, zv, nw_ref[...])
        acc_ref[1:2, :] += jnp.sum(dyn * n, axis=0, keepdims=True)
        dn = dyn * nw_ref[...]
        parts = []
        for g in range(SSD_GROUPS):
            sl = slice(512 * g, 512 * (g + 1))
            dng, ng = dn[:, sl], n[:, sl]
            parts.append(rss[g] * (dng - ng * jnp.mean(dng * ng, axis=-1, keepdims=True)))
        dhg = jnp.concatenate(parts, axis=1)
        dy_ref[...] = dhg * (zv * sz)
        dz_ref[...] = (dhg * yv * (sz * (1.0 + zv * (1.0 - sz)))).astype(bf16)

    return pl.pallas_call(
        body, name="outproj_bwd", grid=(nt,),
        in_specs=[_rows(tm, D_MODEL)] * 4 + [_full((8, D_MODEL)), _full((1, D_MODEL)), _resident((N_CHIPS, 512, D_MODEL)),
                  _ANY_SPEC],
        out_specs=[_rows(tm, D_MODEL)] * 4 + [_full((8, D_MODEL)), _ANY_SPEC],
        out_shape=[jax.ShapeDtypeStruct((L, D_MODEL), f32)] + [jax.ShapeDtypeStruct((L, D_MODEL), bf16)] * 3
        + [jax.ShapeDtypeStruct((8, D_MODEL), f32), jax.ShapeDtypeStruct((N_CHIPS,) + gb.shape[2:], bf16)],
        scratch_shapes=_sems(N_CHIPS),
        compiler_params=_cparams("arbitrary"),
    )(dr1, mix, y, z, mod, norm_w, w_out, gb)


def _attn_bwd(q, kv, do, lse, sinks, pb, gb2):
    L = q.shape[0]
    nb = L // CHUNK
    T = CHUNK

    def body(sink_ref, q_ref, kvp_ref, kvc_ref, do_ref, lse_ref, pb_ref, gb2_ref, dq_ref, dkv_ref, dsink_ref, chips_ref,
             sib2_ref, carry, send_sems, recv_sems, send_sems2, recv_sems2):
        n = pl.program_id(0)
        start, wait = _to_chips_job(pb_ref, chips_ref, send_sems, recv_sems)
        start2, wait2 = _to_sibling_job(gb2_ref, sib2_ref, send_sems2, recv_sems2)

        @pl.when(n == 0)
        def _():
            carry[...] = jnp.zeros_like(carry)
            dsink_ref[...] = jnp.zeros_like(dsink_ref)
            start()
            start2()

        @pl.when(n < nb)
        def _():
            first = n == 0
            ext = _kv_halves(kvp_ref[...], kvc_ref[...], first)
            dist, valid = _attn_masks(first)
            dist4, valid4 = jnp.concatenate([dist] * 4, axis=0), jnp.concatenate([valid] * 4, axis=0)
            lane1 = lax.broadcasted_iota(jnp.int32, (1, 128), 1)
            lse = lse_ref[...]
            qts = [q_ref[:, 128 * b:128 * (b + 1)].astype(f32).T.astype(bf16) for b in range(8)]
            dots = [do_ref[:, 128 * b:128 * (b + 1)].astype(f32).T.astype(bf16) for b in range(8)]
            acck = [None, None]
            accv = [None, None]
            dsink = jnp.zeros((1, 128), f32)
            dq_acc = [None] * 8
            for g in range(2):
                for half in range(2):
                    k_ext, v_ext = ext[g][half]
                    blks, heads, slope, sink = _head_stack(g, half, sink_ref)
                    qs = jnp.concatenate([q_ref[:, 128 * b:128 * (b + 1)] for b in blks], axis=0)
                    dos = jnp.concatenate([do_ref[:, 128 * b:128 * (b + 1)] for b in blks], axis=0)
                    rows = slice(HEAD_DIM * half, HEAD_DIM * (half + 1))
                    qt = jnp.concatenate([qts[b][rows, :] for b in blks], axis=1)
                    dot_ = jnp.concatenate([dots[b][rows, :] for b in blks], axis=1)
                    lse_col = jnp.concatenate([lse[:, h:h + 1] for h in heads], axis=0)
                    s = _dot_nt(qs, k_ext) * ATTN_SCALE - slope * dist4
                    p = jnp.where(valid4, jnp.exp(s - lse_col), 0.0)
                    dp = _dot_nt(dos, v_ext)
                    delta = jnp.sum(p * dp, axis=-1, keepdims=True)
                    ds = (p * (dp - delta) * ATTN_SCALE).astype(bf16)
                    sd = jnp.exp(sink - lse_col) * delta
                    dqs = _dot(ds, k_ext)
                    for i, b in enumerate(blks):
                        seg = slice(T * i, T * (i + 1))
                        dq_acc[b] = dqs[seg, :] if dq_acc[b] is None else dq_acc[b] + dqs[seg, :]
                        dsink = dsink - jnp.where(lane1 == heads[i], jnp.sum(sd[seg, :], axis=0, keepdims=True), 0.0)
                    dk = _dot(qt, ds)
                    dv = _dot(dot_, p.astype(bf16))
                    acck[g] = dk if acck[g] is None else acck[g] + dk
                    accv[g] = dv if accv[g] is None else accv[g] + dv
            dq_ref[...] = jnp.concatenate([a.astype(bf16) for a in dq_acc], axis=1)
            dsink_ref[...] += dsink
            dkv = jnp.concatenate([jnp.concatenate(acck, axis=0).T, jnp.concatenate(accv, axis=0).T], axis=1)
            dkv_ref[...] = (carry[...] + dkv[0:T, :]).astype(bf16)
            carry[...] = dkv[T:2 * T, :]

        @pl.when(n == nb)
        def _():
            dkv_ref[...] = carry[...].astype(bf16)
            wait()
            wait2()

    cur = lambda i: (jnp.minimum(i, nb - 1), 0)
    return pl.pallas_call(
        body, name="attn_bwd", grid=(nb + 1,),
        in_specs=[pl.BlockSpec(memory_space=pltpu.SMEM), pl.BlockSpec((T, D_MODEL), cur),
                  pl.BlockSpec((T, 256), lambda i: (jnp.maximum(jnp.minimum(i, nb - 1) - 1, 0), 0)),
                  pl.BlockSpec((T, 256), cur), pl.BlockSpec((T, D_MODEL), cur), pl.BlockSpec((T, 128), cur), _ANY_SPEC,
                  _ANY_SPEC],
        out_specs=[pl.BlockSpec((T, D_MODEL), cur), pl.BlockSpec((T, 256), lambda i: (jnp.maximum(i - 1, 0), 0)),
                   _full((1, 128)), _ANY_SPEC, _ANY_SPEC],
        out_shape=[jax.ShapeDtypeStruct((L, D_MODEL), bf16), jax.ShapeDtypeStruct((L, 256), bf16),
                   jax.ShapeDtypeStruct((1, 128), f32), jax.ShapeDtypeStruct((N_CHIPS - 1,) + pb.shape[1:], bf16),
                   jax.ShapeDtypeStruct((N_CHIPS,) + gb2.shape[2:], bf16)],
        scratch_shapes=[pltpu.VMEM((T, 256), f32)] + _sems(N_CHIPS - 1) + _sems(N_CHIPS),
        compiler_params=_cparams("arbitrary"),
    )(sinks, q, kv, kv, do, lse, pb, gb2)


def _ssd_bwd(xc, dt_raw, dy, prev_all, dt_bias, a_log, d_skip_x, e_mat, g, pb2):
    L = xc.shape[0]
    nc = L // CHUNK
    T = CHUNK
    RG = g.shape[0]

    def body(xc_ref, dt_ref, dy_ref, prev_ref, dtb_ref, al_ref, dsk_ref, e_ref, g_in_ref, pb2_ref,
             dxc_ref, ddt_ref, acc_ref, dd_ref, g_ref, chips2_ref, dst, dxs_s, send_sems, recv_sems, send_sems2, recv_sems2):
        start, wait = _share_job(g_ref, send_sems, recv_sems, RG)
        start2, wait2 = _to_chips_job(pb2_ref, chips2_ref, send_sems2, recv_sems2)

        @pl.when(pl.program_id(0) == 0)
        def _():
            dst[...] = jnp.zeros_like(dst)
            acc_ref[...] = jnp.zeros_like(acc_ref)
            dd_ref[...] = jnp.zeros_like(dd_ref)
            start()
            start2()

        @pl.when(pl.program_id(0) == nc - 1)
        def _():
            wait()
            wait2()

        lane1 = lax.broadcasted_iota(jnp.int32, (1, 128), 1)
        a_row = jnp.where(lane1 < SSD_HEADS, -jnp.exp(al_ref[...]), 0.0)
        e_mat_v = e_ref[...]
        dt, a, cum, dtx, cumx, r, c = _ssd_chunk_common(dt_ref[...], dtb_ref[...], a_row, e_mat_v)
        cum_t = cum.T
        ex = jnp.exp(cumx)
        last = cumx[T - 1:T, :]
        wx = jnp.exp(last - cumx)
        cdx = jnp.exp(last)
        xs = xc_ref[:, 0:SSD_WIDTH]
        X = xs * dtx
        Xb = X.astype(bf16)
        Xdb = (X * wx).astype(bf16)
        dyv = dy_ref[...]
        prev = prev_ref[0]
        prevb = prev.astype(bf16)
        dnew = dst[...]
        dnewb = dnew.astype(bf16)
        tri = c <= r
        lane = lax.broadcasted_iota(jnp.int32, (T, 128), 1)
        sub = lax.broadcasted_iota(jnp.int32, (128, T), 0)
        lo = lane < HEAD_DIM

        def red(vals, g):
            return _dot_sel_nt(vals, e_mat_v[:, 512 * g:512 * (g + 1)])

        de = jnp.zeros((T, 128), f32)
        dw = jnp.zeros((T, 128), f32)
        ddt_x = jnp.zeros((T, 128), f32)
        dcum_col = jnp.zeros((T, 128), f32)
        dcum_row = jnp.zeros((128, T), f32)
        dprev_parts, dBs, dCs = [], [], []
        for g in range(SSD_GROUPS):
            s5 = slice(512 * g, 512 * (g + 1))
            Bg = xc_ref[:, 1024 + 128 * g:1152 + 128 * g].astype(bf16)
            Cg = xc_ref[:, 1280 + 128 * g:1408 + 128 * g].astype(bf16)
            G = _dot_nt(Cg, Bg)
            Z = _dot(Cg, prevb[:, s5])
            dyg = dyv[:, s5]
            dZb = (dyg * ex[:, s5]).astype(bf16)
            dXd = _dot(Bg, dnewb[:, s5])
            dC = _dot_nt(dZb, prevb[:, s5])
            dB = _dot_nt(Xdb[:, s5], dnewb[:, s5])
            dprev_parts.append(_dot_tn(Cg, dZb) + dnew[:, s5] * cdx[:, s5])
            de = de + red(dyg * Z, g)
            dw = dw + red(dXd * X[:, s5], g)
            dXg = dXd * wx[:, s5]
            dG = jnp.zeros((T, T), f32)
            for j in range(4):
                blk = 4 * g + j
                sl = slice(128 * blk, 128 * (blk + 1))
                Xblk = Xb[:, sl]
                dyblk = dyv[:, sl]
                dyblk_b = dyblk.astype(bf16)
                dxh = []
                for half in range(2):
                    h = 2 * blk + half
                    seg = jnp.minimum(cum[:, h:h + 1] - cum_t[h:h + 1, :], 0.0)
                    Lm = jnp.where(tri, jnp.exp(seg), 0.0)
                    M = G * Lm
                    dyh = jnp.where(lo if half == 0 else ~lo, dyblk, 0.0).astype(bf16)
                    dM = _dot_nt(dyh, Xblk)
                    dG = dG + dM * Lm
                    Q = dM * M
                    dcum_col = dcum_col + jnp.where(lane == h, jnp.sum(Q, axis=1, keepdims=True), 0.0)
                    dcum_row = dcum_row + jnp.where(sub == h, jnp.sum(Q, axis=0, keepdims=True), 0.0)
                    dxh.append(_dot_tn(M.astype(bf16), dyblk_b))
                dXblk = dXg[:, 128 * j:128 * (j + 1)] + jnp.where(lo, dxh[0], dxh[1])
                xsb = xs[:, sl]
                dxs_s[:, sl] = dXblk * dtx[:, sl] + dsk_ref[:, sl] * dyblk
                ddt_x = ddt_x + _dot_sel_nt(dXblk * xsb, e_mat_v[:, sl])
                dd_ref[:, sl] += jnp.sum(dyblk * xsb, axis=0, keepdims=True)
            dGb = dG.astype(bf16)
            dCs.append(dC + _dot(dGb, Bg))
            dBs.append(dB + _dot_tn(dGb, Cg))
        e16 = jnp.exp(cum)
        cum_last = cum[T - 1:T, :]
        w16 = jnp.exp(cum_last - cum)
        dcd = jnp.sum(dnew * prev, axis=0, keepdims=True)
        dcd16 = red(dcd[:, 0:512], 0) + red(dcd[:, 512:1024], 1)
        dww = dw * w16
        extra = jnp.sum(dww, axis=0, keepdims=True) + dcd16 * jnp.exp(cum_last)
        rowi = lax.broadcasted_iota(jnp.int32, (T, 128), 0)
        dcum = dcum_col - dcum_row.T + de * e16 - dww + jnp.where(rowi == T - 1, extra, 0.0)
        da = _sel_dot((c >= r).astype(bf16), dcum)
        ddt = ddt_x + da * a_row
        acc_ref[0:1, :] += jnp.sum(da * dt, axis=0, keepdims=True)
        ddt_raw = jnp.where(lane < SSD_HEADS, ddt * _sigmoid(dt_ref[...] + dtb_ref[...]), 0.0)
        ddt_ref[...] = ddt_raw
        acc_ref[1:2, :] += jnp.sum(ddt_raw, axis=0, keepdims=True)
        dxc_ref[:, 0:SSD_WIDTH] = dxs_s[...]
        dxc_ref[:, 1024:1280] = jnp.concatenate(dBs, axis=1)
        dxc_ref[:, 1280:1536] = jnp.concatenate(dCs, axis=1)
        dst[...] = jnp.concatenate(dprev_parts, axis=1)

    rev = lambda i: (nc - 1 - i, 0)
    return pl.pallas_call(
        body, name="ssd_bwd", grid=(nc,),
        in_specs=[pl.BlockSpec((T, CONV_DIM), rev), pl.BlockSpec((T, 128), rev), pl.BlockSpec((T, SSD_WIDTH), rev),
                  pl.BlockSpec((1, SSD_STATE, SSD_WIDTH), lambda i: (nc - 1 - i, 0, 0)),
                  _full((1, 128)), _full((1, 128)), _full((1, SSD_WIDTH)), _full((128, SSD_WIDTH)), _ANY_SPEC, _ANY_SPEC],
        out_specs=[pl.BlockSpec((T, CONV_DIM), rev), pl.BlockSpec((T, 128), rev), _full((8, 128)),
                   _full((1, SSD_WIDTH)), _ANY_SPEC, _ANY_SPEC],
        out_shape=[jax.ShapeDtypeStruct((L, CONV_DIM), f32), jax.ShapeDtypeStruct((L, 128), f32),
                   jax.ShapeDtypeStruct((8, 128), f32), jax.ShapeDtypeStruct((1, SSD_WIDTH), f32),
                   jax.ShapeDtypeStruct(g.shape, f32), jax.ShapeDtypeStruct((N_CHIPS - 1,) + pb2.shape[1:], bf16)],
        input_output_aliases={8: 4},
        scratch_shapes=[pltpu.VMEM((SSD_STATE, SSD_WIDTH), f32), pltpu.VMEM((T, SSD_WIDTH), f32)] + _sems(1)
        + _sems(N_CHIPS - 1),
        compiler_params=_cparams("arbitrary"),
    )(xc, dt_raw, dy, prev_all, dt_bias, a_log, d_skip_x, e_mat, g, pb2)


def _conv_bwd(dxc, xr, conv_w, conv_b, g):
    L = dxc.shape[0]
    tm = 256
    nt = L // tm
    RG = g.shape[0]

    def body(dxc_ref, xr_ref, xh_ref, cw_ref, cb_ref, g_in_ref, dxr_ref, acc_ref, g_ref, carry, buf, ext, send_sems, recv_sems):
        i = pl.program_id(0)
        start, wait = _share_job(g_ref, send_sems, recv_sems, RG)

        @pl.when(i == 0)
        def _():
            carry[...] = jnp.zeros_like(carry)
            acc_ref[...] = jnp.zeros_like(acc_ref)
            ext[tm + 16:tm + CHUNK, :] = jnp.zeros((CHUNK - 16, CONV_DIM), bf16)
            start()

        @pl.when(i == nt - 1)
        def _():
            wait()

        buf[0:8, :] = jnp.where(i == nt - 1, 0.0, xh_ref[...])
        u = xr_ref[...]
        buf[8:8 + tm, :] = u
        pre = cb_ref[...] + cw_ref[CONV_K - 1:CONV_K, :] * u
        for k in range(CONV_K - 1):
            pre = pre + cw_ref[k:k + 1, :] * buf[5 + k:5 + k + tm, :]
        sg = _sigmoid(pre)
        dpre = dxc_ref[...] * (sg * (1.0 + pre * (1.0 - sg)))
        acc_ref[4:5, :] += jnp.sum(dpre, axis=0, keepdims=True)
        dpb = dpre.astype(bf16)
        ext[0:tm, :] = dpb
        ext[tm:tm + 16, :] = carry[...]
        acc_ref[CONV_K - 1:CONV_K, :] += jnp.sum(u * dpre, axis=0, keepdims=True)
        du = cw_ref[CONV_K - 1:CONV_K, :] * dpre
        r = lax.broadcasted_iota(jnp.int32, (CHUNK, 2 * CHUNK), 0)
        c = lax.broadcasted_iota(jnp.int32, (CHUNK, 2 * CHUNK), 1)
        for j in range(1, CONV_K):
            move = (c == r + j).astype(bf16)
            up = jnp.concatenate([_dot(move, ext[CHUNK * b:CHUNK * (b + 2), :]) for b in range(tm // CHUNK)], axis=0)
            k = CONV_K - 1 - j
            du = du + cw_ref[k:k + 1, :] * up
            acc_ref[k:k + 1, :] += jnp.sum(u * up, axis=0, keepdims=True)
        dxr_ref[...] = du.astype(bf16)
        carry[...] = dpb[0:16, :]

    rev = lambda i: (nt - 1 - i, 0)
    return pl.pallas_call(
        body, name="conv_bwd", grid=(nt,),
        in_specs=[pl.BlockSpec((tm, CONV_DIM), rev), pl.BlockSpec((tm, CONV_DIM), rev),
                  pl.BlockSpec((8, CONV_DIM), lambda i: (jnp.maximum((nt - 1 - i) * (tm // 8) - 1, 0), 0)),
                  _full((CONV_K, CONV_DIM)), _full((1, CONV_DIM)), _ANY_SPEC],
        out_specs=[pl.BlockSpec((tm, CONV_DIM), rev), _full((8, CONV_DIM)), _ANY_SPEC],
        out_shape=[jax.ShapeDtypeStruct((L, CONV_DIM), bf16), jax.ShapeDtypeStruct((8, CONV_DIM), f32),
                   jax.ShapeDtypeStruct(g.shape, f32)],
        input_output_aliases={5: 2},
        scratch_shapes=[pltpu.VMEM((16, CONV_DIM), bf16), pltpu.VMEM((tm + 8, CONV_DIM), f32),
                        pltpu.VMEM((tm + CHUNK, CONV_DIM), bf16)] + _sems(1),
        compiler_params=_cparams("arbitrary"),
    )(dxc, xr, xr, conv_w, conv_b, g)


def _inproj_bwd(dz, dxr, dq, dkv, ddt, dr1, x, mod, ln_g, ln_b, w_in, pb):
    L = x.shape[0]
    tm = DENSE_TM
    nt = L // tm

    def body(dz_ref, dxr_ref, dq_ref, dkv_ref, ddt_ref, dr1_ref, x_ref, mod_ref, g_ref, b_ref, w_ref, pb_ref,
             dx_ref, acc_ref, chips_ref, send_sems, recv_sems):
        start, wait = _to_chips_job(pb_ref, chips_ref, send_sems, recv_sems)

        @pl.when(pl.program_id(0) == 0)
        def _():
            acc_ref[...] = jnp.zeros_like(acc_ref)
            start()

        @pl.when(pl.program_id(0) == nt - 1)
        def _():
            wait()

        du1 = (_dot(dz_ref[...], w_ref[W_Z, :]) + _dot(dxr_ref[...], w_ref[W_XBC, :])
               + _dot(dq_ref[...], w_ref[W_Q, :]) + _dot(dkv_ref[...], w_ref[W_KV, :])
               + _dot(ddt_ref[...].astype(bf16), w_ref[W_DT, :]))
        xhat, rstd = _ln_stats(x_ref[...])
        h0 = xhat * g_ref[...] + b_ref[...]
        acc_ref[0:1, :] += jnp.sum(du1 * h0, axis=0, keepdims=True)
        acc_ref[1:2, :] += jnp.sum(du1, axis=0, keepdims=True)
        dh0 = du1 * (1.0 + mod_ref[1:2, :]) + ALPHA * dr1_ref[...]
        acc_ref[2:3, :] += jnp.sum(dh0 * xhat, axis=0, keepdims=True)
        acc_ref[3:4, :] += jnp.sum(dh0, axis=0, keepdims=True)
        dx_ref[...] = _ln_bwd(dh0, xhat, rstd, g_ref[...])

    v = _full((1, D_MODEL))
    return pl.pallas_call(
        body, name="inproj_bwd", grid=(nt,),
        in_specs=[_rows(tm, D_MODEL), _rows(tm, CONV_DIM), _rows(tm, D_MODEL), _rows(tm, 256), _rows(tm, 128),
                  _rows(tm, D_MODEL), _rows(tm, D_MODEL), _full((8, D_MODEL)), v, v, _resident((PROJ_WIDTH, D_MODEL)),
                  _ANY_SPEC],
        out_specs=[_rows(tm, D_MODEL), _full((8, D_MODEL)), _ANY_SPEC],
        out_shape=[jax.ShapeDtypeStruct((L, D_MODEL), f32), jax.ShapeDtypeStruct((8, D_MODEL), f32),
                   jax.ShapeDtypeStruct((N_CHIPS - 1,) + pb.shape[1:], bf16)],
        scratch_shapes=_sems(N_CHIPS - 1),
        compiler_params=_cparams("arbitrary"),
    )(dz, dxr, dq, dkv, ddt, dr1, x, mod, ln_g, ln_b, w_in, pb)


def _adamw_math(w, g, m, v):
    m = ADAM_B1 * m + (1.0 - ADAM_B1) * g
    v = ADAM_B2 * v + (1.0 - ADAM_B2) * (g * g)
    m_hat = m / (1.0 - ADAM_B1 ** ADAM_STEP)
    v_hat = v / (1.0 - ADAM_B2 ** ADAM_STEP)
    delta = -ADAM_LR * (m_hat / (jnp.sqrt(v_hat) + ADAM_EPS) + ADAM_WD * w)
    return delta, m, v


def _adamw(w, g, m, v, name):
    R, C = w.shape

    def body(w_ref, g_ref, m_ref, v_ref, d_ref, m2_ref, v2_ref):
        d_ref[...], m2_ref[...], v2_ref[...] = _adamw_math(w_ref[...], g_ref[...], m_ref[...], v_ref[...])

    cap = max(8, ADAMW_BLOCK_ELEMS // C)
    tr = R if R <= cap else next(t for t in range(cap - cap % 8, 7, -8) if R % t == 0)
    spec = pl.BlockSpec((tr, C), lambda i: (i, 0))
    return pl.pallas_call(
        body, name=name, grid=(R // tr,), in_specs=[spec] * 4, out_specs=[spec] * 3,
        out_shape=[jax.ShapeDtypeStruct((R, C), f32)] * 3, compiler_params=_cparams("parallel"),
    )(w, g, m, v)


def _adamw_rows(w, g, m, v):
    R, _, C = w.shape
    tr = R // 4

    def body(w_ref, g_ref, m_ref, v_ref, d_ref, m2_ref, v2_ref):
        d_ref[...], m2_ref[...], v2_ref[...] = _adamw_math(w_ref[...], g_ref[...], m_ref[...], v_ref[...])

    spec = pl.BlockSpec((tr, 1, C), lambda i: (i, 0, 0))
    return pl.pallas_call(
        body, name="adamw_w_in", grid=(R // tr,), in_specs=[spec] * 4, out_specs=[spec] * 3,
        out_shape=[jax.ShapeDtypeStruct((R, 1, C), f32)] * 3, compiler_params=_cparams("parallel"),
    )(w, g, m, v)


ADA_COLS = 6 * D_MODEL // N_CHIPS
ADA_TN = 512


COND_LANES = 512


def _prologue(cond, ada_w, ada_b, blob):
    R = blob.shape[0]

    def body(cond_ref, w_ref, b_ref, blob_ref, call_ref, mod_ref, wall_ref, mod_s, stage, gs, gr, ms, mr, ws, wr, local_sem):
        x, y, c = _place()
        start_w, finish_w = _gather_job(blob_ref, wall_ref, ws, wr, R)
        start_w()

        def rows(ref, px, py, pc):
            return ref.at[pl.ds(pl.multiple_of((4 * px + 2 * py + pc) * 8, 8), 8), :]

        mine = pltpu.make_async_copy(cond_ref, rows(call_ref, x, y, c), local_sem)
        mine.start()
        sends = [_remote(cond_ref, rows(call_ref, x, y, c), gs, gr, m - 1, _flip(x, y, c, m)) for m in range(1, N_DEV)]
        for cp in sends:
            cp.start()
        for m in range(1, N_DEV):
            peer = _flip(x, y, c, m)
            _remote(cond_ref, rows(call_ref, *peer), gs, gr, m - 1, peer).wait_recv()
        for cp in sends:
            cp.wait_send()
        mine.wait()

        lo = jnp.concatenate([call_ref[8 * d:8 * d + 1, :] for d in range(N_DEV)], axis=0)
        hi = jnp.concatenate([call_ref[8 * d + 1:8 * d + 2, :] for d in range(N_DEV)], axis=0)
        mod_all = (_dot_exact(lo * _sigmoid(lo), w_ref[0:COND_LANES, :]) + _dot_exact(hi * _sigmoid(hi), w_ref[COND_LANES:, :])
                   + b_ref[...])
        for d in range(N_DEV):
            mod_s[8 * d:8 * d + 8, :] = jnp.broadcast_to(mod_all[d:d + 1, :], (8, ADA_COLS))

        mine = pltpu.make_async_copy(rows(mod_s, x, y, c), mod_ref.at[2 * x + y], local_sem)
        mine.start()
        sends = []
        for m in range(1, N_CHIPS):
            peer = _flip(x, y, c, 2 * m)
            sends.append(_remote(rows(mod_s, *peer), mod_ref.at[2 * x + y], ms, mr, m - 1, peer))
        for cp in sends:
            cp.start()
        for m in range(1, N_CHIPS):
            px, py, pc = _flip(x, y, c, 2 * m)
            _remote(rows(mod_s, x, y, c), mod_ref.at[2 * px + py], ms, mr, m - 1, (px, py, pc)).wait_recv()
        for cp in sends:
            cp.wait_send()
        mine.wait()

        for k in range(R // STAGE_ROWS):
            part = pl.ds(STAGE_ROWS * k, STAGE_ROWS)
            cin = pltpu.make_async_copy(blob_ref.at[part, :], stage, local_sem)
            cin.start()
            cin.wait()
            cout = pltpu.make_async_copy(stage, wall_ref.at[2 * x + y, part, :], local_sem)
            cout.start()
            cout.wait()
        finish_w()

    return pl.pallas_call(
        body, name="prologue",
        out_shape=[jax.ShapeDtypeStruct((8 * N_DEV, COND_LANES), f32), jax.ShapeDtypeStruct((N_CHIPS, 8, ADA_COLS), f32),
                   jax.ShapeDtypeStruct((N_CHIPS, R, D_MODEL), bf16)],
        in_specs=[_VMEM_SPEC, _VMEM_SPEC, _VMEM_SPEC, _ANY_SPEC], out_specs=[_VMEM_SPEC, _VMEM_SPEC, _ANY_SPEC],
        scratch_shapes=[pltpu.VMEM((8 * N_DEV, ADA_COLS), f32), pltpu.VMEM((STAGE_ROWS, D_MODEL), bf16)]
        + _sems(N_DEV - 1) + _sems(N_CHIPS - 1) + _sems(6) + [pltpu.SemaphoreType.DMA],
        compiler_params=pltpu.CompilerParams(vmem_limit_bytes=VMEM_LIMIT),
    )(cond, ada_w, ada_b, blob)


def _ada_bwd(c_all, dmod, w, m, v):
    def body(c_ref, d_ref, w_ref, m_ref, v_ref, g_ref, dl_ref, m2_ref, v2_ref):
        cv = c_ref[...]
        g = lax.dot_general(cv * _sigmoid(cv), d_ref[...], (((0,), (0,)), ((), ())), preferred_element_type=f32,
                            precision=lax.Precision.HIGHEST)
        g_ref[...] = g
        dl_ref[...], m2_ref[...], v2_ref[...] = _adamw_math(w_ref[...], g, m_ref[...], v_ref[...])

    wspec = pl.BlockSpec((D_MODEL, ADA_TN), lambda j: (0, j))
    return pl.pallas_call(
        body, name="ada_bwd", grid=(ADA_COLS // ADA_TN,),
        in_specs=[_full((N_DEV, D_MODEL)), pl.BlockSpec((N_DEV, ADA_TN), lambda j: (0, j)), wspec, wspec, wspec],
        out_specs=[wspec] * 4, out_shape=[jax.ShapeDtypeStruct((D_MODEL, ADA_COLS), f32)] * 4,
        compiler_params=_cparams("parallel"),
    )(c_all, dmod, w, m, v)


SMALL_SLOTS = (("ada_b", 6144), ("ln_in_g", 1024), ("ln_in_b", 1024), ("conv_b", 1536), ("dt_bias", 128), ("a_log", 128),
               ("d_skip", 128), ("ssd_norm_w", 1024), ("attn_sinks", 128), ("ln1_g", 1024), ("ln1_b", 1024),
               ("b_ff1", 4096), ("b_ff2", 1024), ("ln2_g", 1024), ("ln2_b", 1024), ("conv_w", 6144), ("loss", 1024))
SMALL_N = sum(n for _, n in SMALL_SLOTS)
SMALL_OFF = {name: sum(n for _, n in SMALL_SLOTS[:i]) for i, (name, _) in enumerate(SMALL_SLOTS)}
SMALL_PARAMS = tuple(name for name, _ in SMALL_SLOTS[:15])
assert SMALL_N % 1024 == 0


def _small_pack(acc_in, acc_out, acc_mlp, db1, acc_conv, acc_ssd, dd_x, dsink, alog, e_mat):
    def body(in_ref, out_ref, mlp_ref, db1_ref, conv_ref, ssd_ref, dd_ref, sink_ref, al_ref, e_ref, o_ref):
        def put(name, val, at=0):
            off = SMALL_OFF[name] + at
            o_ref[:, off:off + val.shape[1]] = val

        for k, row in enumerate((in_ref[1:2, :], in_ref[0:1, :], out_ref[0:1, :], mlp_ref[A_SH2:A_SH2 + 1, :],
                                 mlp_ref[A_SC2:A_SC2 + 1, :], mlp_ref[A_G2:A_G2 + 1, :])):
            put("ada_b", row, D_MODEL * k)
        put("ln_in_g", in_ref[2:3, :])
        put("ln_in_b", in_ref[3:4, :])
        put("conv_b", conv_ref[4:5, :])
        put("dt_bias", ssd_ref[1:2, :])
        put("a_log", ssd_ref[0:1, :] * (-jnp.exp(al_ref[...])))
        put("d_skip", _dot_sel_nt(jnp.broadcast_to(dd_ref[...], (8, SSD_WIDTH)), e_ref[...])[0:1, :])
        put("ssd_norm_w", out_ref[1:2, :])
        put("attn_sinks", sink_ref[...])
        put("ln1_g", mlp_ref[A_LN1G:A_LN1G + 1, :])
        put("ln1_b", mlp_ref[A_LN1B:A_LN1B + 1, :])
        put("b_ff1", db1_ref[...])
        put("b_ff2", mlp_ref[A_B2:A_B2 + 1, :])
        put("ln2_g", mlp_ref[A_LN2G:A_LN2G + 1, :])
        put("ln2_b", mlp_ref[A_LN2B:A_LN2B + 1, :])
        for k in range(CONV_K):
            put("conv_w", conv_ref[k:k + 1, :], CONV_DIM * k)
        put("loss", mlp_ref[A_LOSS:A_LOSS + 1, :])

    return pl.pallas_call(body, name="small_pack", out_shape=jax.ShapeDtypeStruct((1, SMALL_N), f32),
                          compiler_params=_cparams())(acc_in, acc_out, acc_mlp, db1, acc_conv, acc_ssd, dd_x, dsink, alog, e_mat)


def _small_update(gathered, params, moms, vels):
    k = len(SMALL_PARAMS)

    def body(g_ref, *refs):
        w_refs, m_refs, v_refs, outs = refs[:k], refs[k:2 * k], refs[2 * k:3 * k], refs[3 * k:]

        def total(name, n):
            off = SMALL_OFF[name]
            g = g_ref[0:1, off:off + n]
            for i in range(1, N_DEV):
                g = g + g_ref[i:i + 1, off:off + n]
            return g

        for j, name in enumerate(SMALL_PARAMS):
            n = w_refs[j].shape[1]
            g = total(name, max(n, 128))[:, :n]
            outs[4 * j][...] = g
            outs[4 * j + 1][...], outs[4 * j + 2][...], outs[4 * j + 3][...] = _adamw_math(
                w_refs[j][...], g, m_refs[j][...], v_refs[j][...])
        outs[4 * k][...] = total("conv_w", CONV_K * CONV_DIM)
        outs[4 * k + 1][...] = total("loss", D_MODEL)

    shapes = [jax.ShapeDtypeStruct(p.shape, f32) for p in params for _ in range(4)]
    shapes += [jax.ShapeDtypeStruct((1, CONV_K * CONV_DIM), f32), jax.ShapeDtypeStruct((1, D_MODEL), f32)]
    return pl.pallas_call(body, name="small_update", out_shape=shapes,
                          compiler_params=_cparams())(gathered, *params, *moms, *vels)


def _place():
    return lax.axis_index("x"), lax.axis_index("y"), lax.axis_index("c")


def _flip(x, y, c, m):
    return (1 - x if m & 4 else x, 1 - y if m & 2 else y, 1 - c if m & 1 else c)


_VMEM_SPEC = pl.BlockSpec(memory_space=pltpu.VMEM)
_ANY_SPEC = pl.BlockSpec(memory_space=pl.ANY)


def _allgather8(v, name):
    n = v.shape[1]

    def body(v_ref, out_ref, send_sems, recv_sems, local_sem):
        x, y, c = _place()

        def rows(px, py, pc):
            return out_ref.at[pl.ds(pl.multiple_of((4 * px + 2 * py + pc) * 8, 8), 8), :]

        def copy(m, src, dst, to):
            return pltpu.make_async_remote_copy(src_ref=src, dst_ref=dst, send_sem=send_sems.at[m - 1],
                                                recv_sem=recv_sems.at[m - 1], device_id=to, device_id_type=MESH)

        mine = pltpu.make_async_copy(v_ref, rows(x, y, c), local_sem)
        mine.start()
        sends = [copy(m, v_ref, rows(x, y, c), _flip(x, y, c, m)) for m in range(1, N_DEV)]
        for cp in sends:
            cp.start()
        for m in range(1, N_DEV):
            peer = _flip(x, y, c, m)
            copy(m, v_ref, rows(*peer), peer).wait_recv()
        for cp in sends:
            cp.wait_send()
        mine.wait()

    return pl.pallas_call(
        body, name=name, out_shape=jax.ShapeDtypeStruct((8 * N_DEV, n), f32), in_specs=[_VMEM_SPEC],
        out_specs=_VMEM_SPEC,
        scratch_shapes=[pltpu.SemaphoreType.DMA((N_DEV - 1,)), pltpu.SemaphoreType.DMA((N_DEV - 1,)),
                        pltpu.SemaphoreType.DMA],
    )(v)


def _remote(src, dst, send_sems, recv_sems, k, to):
    return pltpu.make_async_remote_copy(src_ref=src, dst_ref=dst, send_sem=send_sems.at[k], recv_sem=recv_sems.at[k],
                                        device_id=to, device_id_type=MESH)


def _gather_job(blob_ref, out_ref, send_sems, recv_sems, R):
    x, y, c = _place()
    sib = (x, y, 1 - c)
    hr = R // 2

    def half(px, py, pc):
        return out_ref.at[2 * px + py, pl.ds(pl.multiple_of(pc * hr, 16), hr), :]

    my_half = blob_ref.at[pl.ds(pl.multiple_of(c * hr, 16), hr), :]

    def first():
        return [_remote(my_half, half(x, y, c), send_sems, recv_sems, m - 1, _flip(x, y, c, 2 * m))
                for m in range(1, N_CHIPS)]

    def start():
        for cp in first():
            cp.start()

    def finish():
        passed = []
        for m in range(1, N_CHIPS):
            px, py, pc = _flip(x, y, c, 2 * m)
            _remote(my_half, half(px, py, pc), send_sems, recv_sems, m - 1, (px, py, pc)).wait_recv()
            fwd = _remote(half(px, py, pc), half(px, py, pc), send_sems, recv_sems, 2 + m, sib)
            fwd.start()
            passed.append(fwd)
        for m in range(1, N_CHIPS):
            px, py, pc = _flip(x, y, c, 2 * m)
            _remote(my_half, half(px, py, 1 - pc), send_sems, recv_sems, 2 + m, sib).wait_recv()
        for cp in first() + passed:
            cp.wait_send()

    return start, finish


def _to_sibling_job(g_ref, out_ref, send_sems, recv_sems):
    x, y, c = _place()

    def cps():
        return [_remote(g_ref.at[j, 1 - c], out_ref.at[j], send_sems, recv_sems, j, (x, y, 1 - c)) for j in range(N_CHIPS)]

    def start():
        for cp in cps():
            cp.start()

    def wait():
        for cp in cps():
            cp.wait()

    return start, wait


def _to_chips_job(p_ref, out_ref, send_sems, recv_sems):
    x, y, c = _place()

    def cps():
        out = []
        for m in range(1, N_CHIPS):
            px, py, pc = _flip(x, y, c, 2 * m)
            out.append(_remote(p_ref.at[2 * px + py], out_ref.at[m - 1], send_sems, recv_sems, m - 1, (px, py, pc)))
        return out

    def start():
        for cp in cps():
            cp.start()

    def wait():
        for cp in cps():
            cp.wait()

    return start, wait


def _share_job(g_ref, send_sems, recv_sems, R):
    x, y, c = _place()

    def rows(pc):
        return g_ref.at[pl.ds(pl.multiple_of(pc * (R // 2), 8), R // 2), :]

    def start():
        _remote(rows(c), rows(c), send_sems, recv_sems, 0, (x, y, 1 - c)).start()

    def wait():
        _remote(rows(c), rows(1 - c), send_sems, recv_sems, 0, (x, y, 1 - c)).wait_recv()
        _remote(rows(c), rows(c), send_sems, recv_sems, 0, (x, y, 1 - c)).wait_send()

    return start, wait


def _sems(n):
    return [pltpu.SemaphoreType.DMA((n,)), pltpu.SemaphoreType.DMA((n,))]


def _rs_to_sibling(gb):
    def body(g_ref, out_ref, send_sems, recv_sems):
        start, wait = _to_sibling_job(g_ref, out_ref, send_sems, recv_sems)
        start()
        wait()

    return pl.pallas_call(
        body, name="rs_to_sibling", out_shape=jax.ShapeDtypeStruct((N_CHIPS,) + gb.shape[2:], bf16),
        in_specs=[_ANY_SPEC], out_specs=_ANY_SPEC, scratch_shapes=_sems(N_CHIPS),
    )(gb)


def _rs_share(g):
    R = g.shape[0]

    def body(g_ref, out_ref, send_sems, recv_sems):
        start, wait = _share_job(out_ref, send_sems, recv_sems, R)
        start()
        wait()

    return pl.pallas_call(
        body, name="rs_share", out_shape=jax.ShapeDtypeStruct(g.shape, f32), in_specs=[_ANY_SPEC],
        out_specs=_ANY_SPEC, input_output_aliases={0: 0}, scratch_shapes=_sems(1),
    )(g)


RS_TR = 256


def _rs_sum_pair(place, gb, recv, name):
    hr = gb.shape[2]

    def body(pl_ref, g_ref, r_ref, o_ref):
        o_ref[0] = (g_ref[0, 0].astype(f32) + r_ref[0].astype(f32)).astype(bf16)

    return pl.pallas_call(
        body, name=name,
        grid_spec=pltpu.PrefetchScalarGridSpec(
            num_scalar_prefetch=1, grid=(N_CHIPS, hr // RS_TR),
            in_specs=[pl.BlockSpec((1, 1, RS_TR, D_MODEL), lambda j, i, p: (j, p[0], i, 0)),
                      pl.BlockSpec((1, RS_TR, D_MODEL), lambda j, i, p: (j, i, 0))],
            out_specs=pl.BlockSpec((1, RS_TR, D_MODEL), lambda j, i, p: (j, i, 0))),
        out_shape=jax.ShapeDtypeStruct((N_CHIPS, hr, D_MODEL), bf16),
        compiler_params=_cparams("parallel", "parallel"),
    )(place, gb, recv)


def _rs_sum_chips(place, gb, recv_sib, recv_chips, name):
    hr = gb.shape[2]
    nt = hr // RS_TR

    def body(pl_ref, g_ref, r1_ref, r2_ref, o_ref):
        acc = g_ref[0, 0].astype(f32) + r1_ref[0].astype(f32)
        for k in range(N_CHIPS - 1):
            acc = acc + r2_ref[k].astype(f32)
        o_ref[...] = acc

    return pl.pallas_call(
        body, name=name,
        grid_spec=pltpu.PrefetchScalarGridSpec(
            num_scalar_prefetch=1, grid=(nt,),
            in_specs=[pl.BlockSpec((1, 1, RS_TR, D_MODEL), lambda i, p: (p[1], p[0], i, 0)),
                      pl.BlockSpec((1, RS_TR, D_MODEL), lambda i, p: (p[1], i, 0)),
                      pl.BlockSpec((N_CHIPS - 1, RS_TR, D_MODEL), lambda i, p: (0, i, 0))],
            out_specs=pl.BlockSpec((RS_TR, D_MODEL), lambda i, p: (p[0] * nt + i, 0))),
        out_shape=jax.ShapeDtypeStruct((2 * hr, D_MODEL), f32),
        compiler_params=_cparams("parallel"),
    )(place, gb, recv_sib, recv_chips)


def _pad128(v):
    v = v.reshape(1, -1)
    return jnp.pad(v, ((0, 0), (0, 128 - v.shape[1])))


def _row(v):
    return v.reshape(1, -1)


W_COLS = PROJ_WIDTH // N_CHIPS


def _g_in_blocks(gz, gxbc, gdt, gq, gkv):
    g = jnp.concatenate([gz, gxbc, gdt[:W_DT_ROWS], gq, gkv], axis=0)
    return jnp.pad(g.reshape(N_CHIPS, W_COLS, D_MODEL), ((0, 0), (0, D_MODEL - W_COLS), (0, 0)))


def kernel(x, c, ln_in_g, ln_in_b, ada_w, ada_b, w_in, conv_w, conv_b, dt_bias, a_log, d_skip, ssd_norm_w, attn_sinks, w_out, ln1_g, ln1_b, w_ff1, b_ff1, w_ff2, b_ff2, ln2_g, ln2_b, loss_target, m_ln_in_g, m_ln_in_b, m_ada_w, m_ada_b, m_w_in, m_conv_w, m_conv_b, m_dt_bias, m_a_log, m_d_skip, m_ssd_norm_w, m_attn_sinks, m_w_out, m_ln1_g, m_ln1_b, m_w_ff1, m_b_ff1, m_w_ff2, m_b_ff2, m_ln2_g, m_ln2_b, v_ln_in_g, v_ln_in_b, v_ada_w, v_ada_b, v_w_in, v_conv_w, v_conv_b, v_dt_bias, v_a_log, v_d_skip, v_ssd_norm_w, v_attn_sinks, v_w_out, v_ln1_g, v_ln1_b, v_w_ff1, v_b_ff1, v_w_ff2, v_b_ff2, v_ln2_g, v_ln2_b):
    xi, yi, ci = _place()
    chip = 2 * xi + yi
    place = jnp.stack([ci, chip]).astype(jnp.int32)
    x2, tgt = x[0], loss_target[0]

    def as_rows(a):
        return jnp.transpose(a, (2, 0, 1))

    def from_rows(a):
        return jnp.transpose(a, (1, 2, 0))

    cond = jnp.concatenate([c.reshape(2, COND_LANES), conv_w.reshape(3, COND_LANES), jnp.zeros((3, COND_LANES), f32)], axis=0)
    ada_b_mine = lax.dynamic_slice(ada_b, (0, chip * ADA_COLS), (1, ADA_COLS))
    blob_in = jnp.pad(w_in[0].T, ((0, D_MODEL - W_COLS), (0, 0))).astype(bf16)
    cond_all, mod_rows, wall_in = _prologue(cond, ada_w[0], ada_b_mine, blob_in)
    cond_all = cond_all.reshape(N_DEV, 8, COND_LANES)
    c_all = cond_all[:, 0:2].reshape(N_DEV, D_MODEL)
    conv_w_full = jnp.concatenate([cond_all[2 * j, 2:5].reshape(CONV_K, 384) for j in range(N_CHIPS)], axis=1)
    mod = jnp.concatenate([mod_rows[:, 0].reshape(6, D_MODEL), jnp.zeros((2, D_MODEL), f32)], axis=0)
    w_in_f = wall_in[:, :W_COLS].reshape(PROJ_WIDTH, D_MODEL)
    b_ff1w, b_ff2w, b_outw = w_ff1[0].astype(bf16), w_ff2[0].astype(bf16), w_out[0].astype(bf16)

    def with_mine(wall, mine):
        return lax.dynamic_update_slice(wall, mine[None], (chip, 0, 0))

    e_mat = _head_expand()
    dsk_x = jnp.repeat(d_skip[0], HEAD_DIM).reshape(1, SSD_WIDTH)
    dtb, alog = _pad128(dt_bias), _pad128(a_log)
    sinks = attn_sinks[0]
    lng, lnb = _row(ln_in_g), _row(ln_in_b)
    u1, z, xr, xc, q, kv, dtr, wall_ff1 = _inproj_fwd(x2, mod, lng, lnb, w_in_f, conv_w_full, conv_b, b_ff1w)
    y, prev_all, wall_ff2 = _ssd_fwd(xc, dtr, dtb, alog, dsk_x, e_mat, b_ff2w)
    o, lse, wall_out = _attn_fwd(q, kv, sinks, b_outw)
    wall_ff1, wall_ff2, wall_out = with_mine(wall_ff1, b_ff1w), with_mine(wall_ff2, b_ff2w), with_mine(wall_out, b_outw)
    yn, mix, r1 = _outproj_fwd(y, z, o, x2, mod, lng, lnb, ssd_norm_w, wall_out)

    dr1, u2, s_act, da, df, acc_mlp, db1 = _mlp_fwd_bwd(r1, tgt, mod, ln1_g, ln1_b, ln2_g, ln2_b, wall_ff1, b_ff1, wall_ff2,
                                                        b_ff2)
    ga = jnp.zeros((N_CHIPS, GA_ROWS, D_MODEL), bf16)
    ga = _wgrad_blob(ga, u2, da, "wgrad_ff1", lambda t, n: (n, t))
    ga = _wgrad_blob(ga, s_act, df, "wgrad_ff2", lambda t, n: (t // 2, 2 + t % 2))
    ga = ga.reshape(N_CHIPS, 2, GA_ROWS // 2, D_MODEL)
    dy, dz, do, dmix, acc_out, a_sib = _outproj_bwd(dr1, mix, y, z, mod, ssd_norm_w, wall_out, ga)
    gc = jnp.zeros((N_CHIPS, GC_ROWS, D_MODEL), bf16)
    gc = _wgrad_blob(gc, yn, dmix, "wgrad_out_y", lambda t, n: (t, 0))
    gc = _wgrad_blob(gc, o, dmix, "wgrad_out_o", lambda t, n: (2 + t, 0))
    gc = gc.reshape(N_CHIPS, 2, GC_ROWS // 2, D_MODEL)
    a_pair = _rs_sum_pair(place, ga, a_sib, "rs_sum_pair_a")
    dq, dkv, dsink, a_chips, c_sib = _attn_bwd(q, kv, do, lse, sinks, a_pair, gc)
    g_a = _rs_sum_chips(place, ga, a_sib, a_chips, "rs_sum_chips_a")
    c_pair = _rs_sum_pair(place, gc, c_sib, "rs_sum_pair_c")
    dxc, ddt, acc_ssd, dd_x, g_a, c_chips = _ssd_bwd(xc, dtr, dy, prev_all, dtb, alog, dsk_x, e_mat, g_a, c_pair)
    g_c = _rs_sum_chips(place, gc, c_sib, c_chips, "rs_sum_chips_c")
    dxr, acc_conv, g_c = _conv_bwd(dxc, xr, conv_w_full, conv_b, g_c)
    gb = _g_in_blocks(_wgrad(dz, u1, "wgrad_in_z"), _wgrad(dxr, u1, "wgrad_in_xbc"),
                      _wgrad(ddt.astype(bf16), u1, "wgrad_in_dt"), _wgrad(dq, u1, "wgrad_in_q"),
                      _wgrad(dkv, u1, "wgrad_in_kv")).reshape(N_CHIPS, 2, GB_ROWS // 2, D_MODEL)
    b_sib = _rs_to_sibling(gb)
    b_pair = _rs_sum_pair(place, gb, b_sib, "rs_sum_pair_b")
    grad_x, acc_in, b_chips = _inproj_bwd(dz, dxr, dq, dkv, ddt, dr1, x2, mod, lng, lnb, w_in_f, b_pair)
    g_b = _rs_share(_rs_sum_chips(place, gb, b_sib, b_chips, "rs_sum_chips_b"))

    packed = _small_pack(acc_in, acc_out, acc_mlp, db1, acc_conv, acc_ssd, dd_x, dsink, alog, e_mat)
    small_all = _allgather8(packed.reshape(8, SMALL_N // 8), "gather_small").reshape(N_DEV, SMALL_N)
    given = dict(ada_b=(ada_b, m_ada_b, v_ada_b), ln_in_g=(ln_in_g, m_ln_in_g, v_ln_in_g), ln_in_b=(ln_in_b, m_ln_in_b, v_ln_in_b),
                 conv_b=(conv_b, m_conv_b, v_conv_b), dt_bias=(dt_bias, m_dt_bias, v_dt_bias), a_log=(a_log, m_a_log, v_a_log),
                 d_skip=(d_skip, m_d_skip, v_d_skip), ssd_norm_w=(ssd_norm_w, m_ssd_norm_w, v_ssd_norm_w),
                 attn_sinks=(attn_sinks, m_attn_sinks, v_attn_sinks), ln1_g=(ln1_g, m_ln1_g, v_ln1_g),
                 ln1_b=(ln1_b, m_ln1_b, v_ln1_b), b_ff1=(b_ff1, m_b_ff1, v_b_ff1), b_ff2=(b_ff2, m_b_ff2, v_b_ff2),
                 ln2_g=(ln2_g, m_ln2_g, v_ln2_g), ln2_b=(ln2_b, m_ln2_b, v_ln2_b))
    upd = _small_update(small_all, *([_row(given[n][i]) for n in SMALL_PARAMS] for i in range(3)))
    small_res = {n: [t.reshape(given[n][0].shape) for t in upd[4 * j:4 * j + 4]] for j, n in enumerate(SMALL_PARAMS)}
    g_conv_all, loss_lanes = upd[4 * len(SMALL_PARAMS)], upd[4 * len(SMALL_PARAMS) + 1]
    loss = jnp.sum(loss_lanes)

    dmod_mine = lax.dynamic_slice(small_all[:, :6 * D_MODEL], (0, chip * ADA_COLS), (N_DEV, ADA_COLS))
    big = {"ada_w": [t[None] for t in _ada_bwd(c_all, dmod_mine, ada_w[0], m_ada_w[0], v_ada_w[0])]}

    g_conv = lax.dynamic_slice(g_conv_all.reshape(CONV_K, CONV_DIM), (0, chip * 384), (CONV_K, 384))
    big["conv_w"] = [t[None] for t in (g_conv, *_adamw(conv_w[0], g_conv, m_conv_w[0], v_conv_w[0], "adamw_conv_w"))]

    g_rows = g_b[:W_COLS].reshape(W_COLS, 1, D_MODEL)
    big["w_in"] = [from_rows(t) for t in (g_rows, *_adamw_rows(as_rows(w_in), g_rows, as_rows(m_w_in), as_rows(v_w_in)))]
    for name, g, (w, m, v) in (("w_out", g_c, (w_out, m_w_out, v_w_out)),
                               ("w_ff1", g_a[:D_MODEL], (w_ff1, m_w_ff1, v_w_ff1)),
                               ("w_ff2", g_a[D_MODEL:GA_ROWS], (w_ff2, m_w_ff2, v_w_ff2))):
        big[name] = [t[None] for t in (g, *_adamw(w[0], g, m[0], v[0], "adamw_" + name))]

    order = ("ln_in_g", "ln_in_b", "ada_w", "ada_b", "w_in", "conv_w", "conv_b", "dt_bias", "a_log", "d_skip", "ssd_norm_w",
             "attn_sinks", "w_out", "ln1_g", "ln1_b", "w_ff1", "b_ff1", "w_ff2", "b_ff2", "ln2_g", "ln2_b")
    res = {**small_res, **big}
    return (loss, grad_x[None], *[res[n][k] for k in range(4) for n in order])
```

```python
import functools
import math

import numpy as np
import jax
import jax.numpy as jnp
from jax import lax
from jax.experimental import pallas as pl
from jax.experimental.pallas import tpu as pltpu

f32 = jnp.float32
bf16 = jnp.bfloat16

D_MODEL = 1024
SSD_WIDTH = 1024
SSD_HEADS = 16
HEAD_DIM = 64
SSD_STATE = 128
SSD_GROUPS = 2
CHUNK = 128
CONV_K = 4
CONV_DIM = 1536
ATTN_HEADS = 16
D_FF = 4096
PROJ_WIDTH = 3856
ALPHA = 2.0 ** 0.25
LN_EPS = 1e-5
RMS_EPS = 1e-5
ATTN_SCALE = HEAD_DIM ** -0.5
NEG = -1e30

ADAM_LR = 0.001
ADAM_B1 = 0.9
ADAM_B2 = 0.999
ADAM_EPS = 1e-08
ADAM_WD = 0.01
ADAM_STEP = 10

W_Z, W_XBC, W_Q, W_KV = slice(0, 1024), slice(1024, 2560), slice(2576, 3600), slice(3600, 3856)
W_DT = slice(2560, 2688)
W_DT_ROWS = 16
GA_ROWS = 2048
GB_ROWS = 1024
GC_ROWS = 512
WG_TM = 512
STAGE_ROWS = 512
ADAMW_BLOCK_ELEMS = 1 << 18
DENSE_TM = 512
N_CHIPS = 4
N_DEV = 8
VMEM_LIMIT = 56 * 1024 * 1024
MESH = pl.DeviceIdType.MESH

ALIBI_SLOPES = tuple(2.0 ** (-8.0 / ATTN_HEADS * (i + 1)) for i in range(ATTN_HEADS))


def _cparams(*sem):
    return pltpu.CompilerParams(dimension_semantics=sem, vmem_limit_bytes=VMEM_LIMIT)


def _sigmoid(x):
    return 1.0 / (1.0 + jnp.exp(-x))


def _softplus(x):
    return jnp.maximum(x, 0.0) + jnp.log1p(jnp.exp(-jnp.abs(x)))


def _ln_stats(x):
    mu = jnp.mean(x, axis=-1, keepdims=True)
    xc = x - mu
    var = jnp.mean(xc * xc, axis=-1, keepdims=True)
    rstd = lax.rsqrt(var + LN_EPS)
    return xc * rstd, rstd


def _ln_bwd(dy, xhat, rstd, g):
    dxh = dy * g
    m1 = jnp.mean(dxh, axis=-1, keepdims=True)
    m2 = jnp.mean(dxh * xhat, axis=-1, keepdims=True)
    return rstd * (dxh - m1 - xhat * m2)


def _dot(a, b):
    return jnp.dot(a, b, preferred_element_type=f32)


def _dot_nt(a, b):
    return lax.dot_general(a, b, (((1,), (1,)), ((), ())), preferred_element_type=f32)


def _dot_tn(a, b):
    return lax.dot_general(a, b, (((0,), (0,)), ((), ())), preferred_element_type=f32)


def _dot_exact(a, b):
    return jnp.dot(a, b, preferred_element_type=f32, precision=lax.Precision.HIGHEST)


def _split3(v):
    hi = v.astype(bf16)
    r1 = v - hi.astype(f32)
    mid = r1.astype(bf16)
    lo = (r1 - mid.astype(f32)).astype(bf16)
    return hi, mid, lo


def _sel_dot(sel, v):
    hi, mid, lo = _split3(v)
    return _dot(sel, hi) + _dot(sel, mid) + _dot(sel, lo)


def _dot_sel_nt(v, sel):
    hi = v.astype(bf16)
    mid = (v - hi.astype(f32)).astype(bf16)
    return _dot_nt(hi, sel) + _dot_nt(mid, sel)


def _expand_heads(v):
    lane = lax.broadcasted_iota(jnp.int32, (v.shape[0], 128), 1)
    blocks = [jnp.where(lane < HEAD_DIM, v[:, 2 * b:2 * b + 1], v[:, 2 * b + 1:2 * b + 2]) for b in range(SSD_HEADS // 2)]
    return jnp.concatenate(blocks, axis=1)


def _full(shape):
    nd = len(shape)
    return pl.BlockSpec(shape, lambda *_: (0,) * nd)


def _resident(shape):
    nd = len(shape)
    return pl.BlockSpec(shape, lambda *_: (0,) * nd, pipeline_mode=pl.Buffered(1))


def _rows(tm, n):
    return pl.BlockSpec((tm, n), lambda i: (i, 0))


def _inproj_fwd(x, mod, ln_g, ln_b, w_in_t, conv_w, conv_b, blob):
    L = x.shape[0]
    tm = DENSE_TM
    nt = L // tm
    R = blob.shape[0]

    def body(x_ref, mod_ref, g_ref, b_ref, w_ref, cw_ref, cb_ref, blob_ref,
             u1_ref, z_ref, xr_ref, xc_ref, q_ref, kv_ref, dt_ref, wall_ref, halo, buf, send_sems, recv_sems):
        start, finish = _gather_job(blob_ref, wall_ref, send_sems, recv_sems, R)

        @pl.when(pl.program_id(0) == 0)
        def _():
            halo[...] = jnp.zeros_like(halo)
            start()

        xhat, _ = _ln_stats(x_ref[...])
        h0 = xhat * g_ref[...] + b_ref[...]
        u1 = (h0 * (1.0 + mod_ref[1:2, :]) + mod_ref[0:1, :]).astype(bf16)
        u1_ref[...] = u1
        z_ref[...] = _dot_nt(u1, w_ref[W_Z, :])
        xr = _dot_nt(u1, w_ref[W_XBC, :])
        xr_ref[...] = xr
        q_ref[...] = _dot_nt(u1, w_ref[W_Q, :]).astype(bf16)
        kv_ref[...] = _dot_nt(u1, w_ref[W_KV, :]).astype(bf16)
        dt_ref[...] = _dot_nt(u1, w_ref[W_DT, :])
        buf[0:8, :] = halo[...]
        buf[8:8 + tm, :] = xr
        pre = cb_ref[...] + cw_ref[0:1, :] * buf[5:5 + tm, :]
        for k in range(1, CONV_K):
            pre = pre + cw_ref[k:k + 1, :] * buf[5 + k:5 + k + tm, :]
        xc_ref[...] = pre * _sigmoid(pre)
        halo[...] = xr[tm - 8:tm, :]

        @pl.when(pl.program_id(0) == nt - 1)
        def _():
            finish()

    return pl.pallas_call(
        body, name="inproj_fwd", grid=(nt,),
        in_specs=[_rows(tm, D_MODEL), _full((8, D_MODEL)), _full((1, D_MODEL)), _full((1, D_MODEL)),
                  _resident((PROJ_WIDTH, D_MODEL)), _full((CONV_K, CONV_DIM)), _full((1, CONV_DIM)), _ANY_SPEC],
        out_specs=[_rows(tm, D_MODEL), _rows(tm, D_MODEL), _rows(tm, CONV_DIM), _rows(tm, CONV_DIM),
                   _rows(tm, D_MODEL), _rows(tm, 256), _rows(tm, 128), _ANY_SPEC],
        out_shape=[jax.ShapeDtypeStruct((L, D_MODEL), bf16), jax.ShapeDtypeStruct((L, D_MODEL), f32),
                   jax.ShapeDtypeStruct((L, CONV_DIM), f32), jax.ShapeDtypeStruct((L, CONV_DIM), f32),
                   jax.ShapeDtypeStruct((L, D_MODEL), bf16), jax.ShapeDtypeStruct((L, 256), bf16),
                   jax.ShapeDtypeStruct((L, 128), f32), jax.ShapeDtypeStruct((N_CHIPS, R, D_MODEL), bf16)],
        scratch_shapes=[pltpu.VMEM((8, CONV_DIM), f32), pltpu.VMEM((tm + 8, CONV_DIM), f32),
                        pltpu.SemaphoreType.DMA((6,)), pltpu.SemaphoreType.DMA((6,))],
        compiler_params=_cparams("arbitrary"),
    )(x, mod, ln_g, ln_b, w_in_t, conv_w, conv_b, blob)


def _head_expand():
    e = np.zeros((128, SSD_WIDTH), np.float32)
    for h in range(SSD_HEADS):
        e[h, h * HEAD_DIM:(h + 1) * HEAD_DIM] = 1.0
    return jnp.asarray(e, dtype=bf16)


def _ssd_chunk_common(dt_raw, dtb, a_row):
    T = CHUNK
    lane = lax.broadcasted_iota(jnp.int32, (T, 128), 1)
    dt = jnp.where(lane < SSD_HEADS, _softplus(dt_raw + dtb), 0.0)
    a = dt * a_row
    r = lax.broadcasted_iota(jnp.int32, (T, T), 0)
    c = lax.broadcasted_iota(jnp.int32, (T, T), 1)
    tril = (c <= r).astype(bf16)
    cum = _sel_dot(tril, a)
    dtx = _expand_heads(dt)
    cumx = _expand_heads(cum)
    return dt, a, cum, dtx, cumx, r, c


def _ssd_fwd(xc, dt_raw, dt_bias, a_log, d_skip_x, blob):
    L = xc.shape[0]
    nc = L // CHUNK
    T = CHUNK
    R = blob.shape[0]

    def body(xc_ref, dt_ref, dtb_ref, al_ref, dsk_ref, blob_ref, y_ref, prev_ref, wall_ref, st, send_sems, recv_sems):
        start, finish = _gather_job(blob_ref, wall_ref, send_sems, recv_sems, R)

        @pl.when(pl.program_id(0) == 0)
        def _():
            st[...] = jnp.zeros_like(st)
            start()

        @pl.when(pl.program_id(0) == nc - 1)
        def _():
            finish()

        a_row = -jnp.exp(al_ref[...])
        lane1 = lax.broadcasted_iota(jnp.int32, (1, 128), 1)
        a_row = jnp.where(lane1 < SSD_HEADS, a_row, 0.0)
        dt, a, cum, dtx, cumx, r, c = _ssd_chunk_common(dt_ref[...], dtb_ref[...], a_row)
        cum_t = cum.T
        ex = jnp.exp(cumx)
        last = cumx[T - 1:T, :]
        wx = jnp.exp(last - cumx)
        cdx = jnp.exp(last)
        xs = xc_ref[:, 0:SSD_WIDTH]
        X = xs * dtx
        Xb = X.astype(bf16)
        Xd = (X * wx).astype(bf16)
        prev = st[...]
        prev_ref[0] = prev
        prevb = prev.astype(bf16)
        tri = c <= r
        lane = lax.broadcasted_iota(jnp.int32, (T, 128), 1)
        y_blocks = []
        new_states = []
        for g in range(SSD_GROUPS):
            Bg = xc_ref[:, 1024 + 128 * g:1152 + 128 * g].astype(bf16)
            Cg = xc_ref[:, 1280 + 128 * g:1408 + 128 * g].astype(bf16)
            G = _dot_nt(Cg, Bg)
            yoff = _dot(Cg, prevb[:, 512 * g:512 * (g + 1)])
            new_states.append(_dot_tn(Bg, Xd[:, 512 * g:512 * (g + 1)]))
            for j in range(4):
                blk = 4 * g + j
                Xblk = Xb[:, 128 * blk:128 * (blk + 1)]
                ys = []
                for half in range(2):
                    h = 2 * blk + half
                    seg = jnp.minimum(cum[:, h:h + 1] - cum_t[h:h + 1, :], 0.0)
                    M = jnp.where(tri, G * jnp.exp(seg), 0.0).astype(bf16)
                    ys.append(_dot(M, Xblk))
                yd = jnp.where(lane < HEAD_DIM, ys[0], ys[1])
                sl = slice(128 * blk, 128 * (blk + 1))
                y_blocks.append(yd + ex[:, sl] * yoff[:, 128 * j:128 * (j + 1)] + dsk_ref[:, sl] * xs[:, sl])
        y_ref[...] = jnp.concatenate(y_blocks, axis=1)
        st[...] = prev * cdx + jnp.concatenate(new_states, axis=1)

    return pl.pallas_call(
        body, name="ssd_fwd", grid=(nc,),
        in_specs=[_rows(T, CONV_DIM), _rows(T, 128), _full((1, 128)), _full((1, 128)), _full((1, SSD_WIDTH)), _ANY_SPEC],
        out_specs=[_rows(T, SSD_WIDTH), pl.BlockSpec((1, SSD_STATE, SSD_WIDTH), lambda i: (i, 0, 0)), _ANY_SPEC],
        out_shape=[jax.ShapeDtypeStruct((L, SSD_WIDTH), f32), jax.ShapeDtypeStruct((nc, SSD_STATE, SSD_WIDTH), f32),
                   jax.ShapeDtypeStruct((N_CHIPS, R, D_MODEL), bf16)],
        scratch_shapes=[pltpu.VMEM((SSD_STATE, SSD_WIDTH), f32)] + _sems(6),
        compiler_params=_cparams("arbitrary"),
    )(xc, dt_raw, dt_bias, a_log, d_skip_x, blob)


def _kv_halves(kv_prev, kv_cur, first):
    kv = jnp.concatenate([jnp.where(first, 0.0, kv_prev.astype(f32)), kv_cur.astype(f32)], axis=0)
    lane = lax.broadcasted_iota(jnp.int32, (2 * CHUNK, 128), 1)
    lo = lane < HEAD_DIM
    out = []
    for g in range(2):
        per_half = []
        for half in range(2):
            both = []
            for t in (kv[:, 0:128], kv[:, 128:256]):
                src = t if g == half else pltpu.roll(t, HEAD_DIM, 1)
                both.append(jnp.where(lo if half == 0 else ~lo, src, 0.0).astype(bf16))
            per_half.append(tuple(both))
        out.append(per_half)
    return out


def _attn_masks(first):
    r = lax.broadcasted_iota(jnp.int32, (CHUNK, 2 * CHUNK), 0)
    c = lax.broadcasted_iota(jnp.int32, (CHUNK, 2 * CHUNK), 1)
    dist = r + CHUNK - c
    valid = (dist >= 0) & (dist < CHUNK) & ((c >= CHUNK) | jnp.logical_not(first))
    return dist.astype(f32), valid


def _head_stack(g, half, sink_ref):
    blks = [4 * g + i for i in range(4)]
    heads = [2 * b + half for b in blks]
    slope = jnp.concatenate([jnp.full((CHUNK, 1), ALIBI_SLOPES[h], f32) for h in heads], axis=0)
    sink = jnp.concatenate([jnp.full((CHUNK, 1), sink_ref[h], f32) for h in heads], axis=0)
    return blks, heads, slope, sink


def _attn_fwd(q, kv, sinks, blob):
    L = q.shape[0]
    nb = L // CHUNK
    T = CHUNK
    R = blob.shape[0]

    def body(sink_ref, q_ref, kvp_ref, kvc_ref, blob_ref, o_ref, lse_ref, wall_ref, send_sems, recv_sems):
        first = pl.program_id(0) == 0
        start, finish = _gather_job(blob_ref, wall_ref, send_sems, recv_sems, R)

        @pl.when(first)
        def _():
            start()

        @pl.when(pl.program_id(0) == nb - 1)
        def _():
            finish()

        ext = _kv_halves(kvp_ref[...], kvc_ref[...], first)
        dist, valid = _attn_masks(first)
        lane = lax.broadcasted_iota(jnp.int32, (T, 128), 1)
        lse = jnp.zeros((T, 128), f32)
        o_blocks = []
        for blk in range(8):
            qb = q_ref[:, 128 * blk:128 * (blk + 1)]
            acc = None
            for half in range(2):
                h = 2 * blk + half
                k_ext, v_ext = ext[h // 8][half]
                s = _dot_nt(qb, k_ext) * ATTN_SCALE - ALIBI_SLOPES[h] * dist
                s = jnp.where(valid, s, NEG)
                sink = sink_ref[h]
                m = jnp.maximum(jnp.max(s, axis=-1, keepdims=True), sink)
                p = jnp.exp(s - m)
                den = jnp.sum(p, axis=-1, keepdims=True) + jnp.exp(sink - m)
                pn = (p * (1.0 / den)).astype(bf16)
                oh = _dot(pn, v_ext)
                acc = oh if acc is None else acc + oh
                lse = jnp.where(lane == h, m + jnp.log(den), lse)
            o_blocks.append(acc.astype(bf16))
        o_ref[...] = jnp.concatenate(o_blocks, axis=1)
        lse_ref[...] = lse

    return pl.pallas_call(
        body, name="attn_fwd", grid=(nb,),
        in_specs=[pl.BlockSpec(memory_space=pltpu.SMEM), _rows(T, D_MODEL),
                  pl.BlockSpec((T, 256), lambda i: (jnp.maximum(i - 1, 0), 0)), _rows(T, 256), _ANY_SPEC],
        out_specs=[_rows(T, D_MODEL), _rows(T, 128), _ANY_SPEC],
        out_shape=[jax.ShapeDtypeStruct((L, D_MODEL), bf16), jax.ShapeDtypeStruct((L, 128), f32),
                   jax.ShapeDtypeStruct((N_CHIPS, R, D_MODEL), bf16)],
        scratch_shapes=_sems(6),
        compiler_params=_cparams("arbitrary"),
    )(sinks, q, kv, kv, blob)


def _gated_norm(y, z, w):
    sz = _sigmoid(z)
    hg = y * (z * sz)
    ns, rss = [], []
    for g in range(SSD_GROUPS):
        hs = hg[:, 512 * g:512 * (g + 1)]
        rs = lax.rsqrt(jnp.mean(hs * hs, axis=-1, keepdims=True) + RMS_EPS)
        ns.append(hs * rs)
        rss.append(rs)
    n = jnp.concatenate(ns, axis=1)
    return n * w, n, rss, sz


def _outproj_fwd(y, z, o, x, mod, ln_g, ln_b, norm_w, w_out):
    L = x.shape[0]
    tm = DENSE_TM

    def body(y_ref, z_ref, o_ref, x_ref, mod_ref, g_ref, b_ref, nw_ref, w_ref, yn_ref, mix_ref, r1_ref):
        yn, _, _, _ = _gated_norm(y_ref[...], z_ref[...], nw_ref[...])
        ynb = yn.astype(bf16)
        yn_ref[...] = ynb
        mix = (_dot(ynb[:, 0:512], w_ref[0]) + _dot(ynb[:, 512:1024], w_ref[1])
               + _dot(o_ref[:, 0:512], w_ref[2]) + _dot(o_ref[:, 512:1024], w_ref[3]))
        mix_ref[...] = mix
        xhat, _ = _ln_stats(x_ref[...])
        h0 = xhat * g_ref[...] + b_ref[...]
        r1_ref[...] = ALPHA * h0 + (1.0 + mod_ref[2:3, :]) * mix

    v = _full((1, D_MODEL))
    return pl.pallas_call(
        body, name="outproj_fwd", grid=(L // tm,),
        in_specs=[_rows(tm, D_MODEL), _rows(tm, D_MODEL), _rows(tm, D_MODEL), _rows(tm, D_MODEL),
                  _full((8, D_MODEL)), v, v, v, _resident((N_CHIPS, 512, D_MODEL))],
        out_specs=[_rows(tm, D_MODEL)] * 3,
        out_shape=[jax.ShapeDtypeStruct((L, D_MODEL), bf16), jax.ShapeDtypeStruct((L, D_MODEL), f32),
                   jax.ShapeDtypeStruct((L, D_MODEL), f32)],
        compiler_params=_cparams("parallel"),
    )(y, z, o, x, mod, ln_g, ln_b, norm_w, w_out)


A_LN2G, A_LN2B, A_G2, A_B2, A_SC2, A_SH2, A_LN1G, A_LN1B, A_LOSS = range(9)


def _mlp_fwd_bwd(r1, target, mod, ln1_g, ln1_b, ln2_g, ln2_b, w1, b1, w2, b2):
    L = r1.shape[0]
    tm = 256
    nj = D_FF // 1024

    def body(r1_ref, t_ref, mod_ref, g1_ref, bb1_ref, g2_ref, bb2_ref, w1_ref, b1_ref, w2_ref, b2_ref,
             dr1_ref, u2_ref, s_ref, da_ref, df_ref, acc_ref, db1_ref, hr):
        @pl.when(pl.program_id(0) == 0)
        def _():
            acc_ref[...] = jnp.zeros_like(acc_ref)
            db1_ref[...] = jnp.zeros_like(db1_ref)

        sc2, sh2, gate2 = mod_ref[4:5, :], mod_ref[3:4, :], mod_ref[5:6, :]
        xhat1, rstd1 = _ln_stats(r1_ref[...])
        h1 = xhat1 * g1_ref[...] + bb1_ref[...]
        u2f = h1 * (1.0 + sc2) + sh2
        u2 = u2f.astype(bf16)
        u2_ref[...] = u2
        f = jnp.zeros((tm, D_MODEL), f32) + b2_ref[...]
        for j in range(nj):
            cs = slice(1024 * j, 1024 * (j + 1))
            a = _dot(u2, w1_ref[j]) + b1_ref[:, cs]
            hrj = jnp.maximum(a, 0.0)
            hr[:, cs] = hrj
            sj = (hrj * hrj).astype(bf16)
            s_ref[:, cs] = sj
            f = f + _dot(sj, w2_ref[j])
        r2 = ALPHA * h1 + (1.0 + gate2) * f
        xhat2, rstd2 = _ln_stats(r2)
        h2 = xhat2 * g2_ref[...] + bb2_ref[...]
        diff = h2 - t_ref[...]
        dh2 = diff * (1.0 / D_MODEL)

        def add(row, val):
            acc_ref[row:row + 1, :] += jnp.sum(val, axis=0, keepdims=True)

        add(A_LOSS, diff * diff * (0.5 / D_MODEL))
        add(A_LN2G, dh2 * xhat2)
        add(A_LN2B, dh2)
        dr2 = _ln_bwd(dh2, xhat2, rstd2, g2_ref[...])
        add(A_G2, dr2 * f)
        df = dr2 * (1.0 + gate2)
        add(A_B2, df)
        dfb = df.astype(bf16)
        df_ref[...] = dfb
        du2 = jnp.zeros((tm, D_MODEL), f32)
        for j in range(nj):
            cs = slice(1024 * j, 1024 * (j + 1))
            ds = _dot_nt(dfb, w2_ref[j])
            daj = ds * (2.0 * hr[:, cs])
            db1_ref[:, cs] += jnp.sum(daj, axis=0, keepdims=True)
            dajb = daj.astype(bf16)
            da_ref[:, cs] = dajb
            du2 = du2 + _dot_nt(dajb, w1_ref[j])
        add(A_SC2, du2 * h1)
        add(A_SH2, du2)
        dh1 = ALPHA * dr2 + du2 * (1.0 + sc2)
        add(A_LN1G, dh1 * xhat1)
        add(A_LN1B, dh1)
        dr1_ref[...] = _ln_bwd(dh1, xhat1, rstd1, g1_ref[...])

    v = _full((1, D_MODEL))
    return pl.pallas_call(
        body, name="mlp_fwd_bwd", grid=(L // tm,),
        in_specs=[_rows(tm, D_MODEL), _rows(tm, D_MODEL), _full((8, D_MODEL)), v, v, v, v,
                  _resident((N_CHIPS, D_MODEL, D_MODEL)), _full((1, D_FF)), _resident((N_CHIPS, D_MODEL, D_MODEL)), v],
        out_specs=[_rows(tm, D_MODEL), _rows(tm, D_MODEL), _rows(tm, D_FF), _rows(tm, D_FF), _rows(tm, D_MODEL),
                   _full((16, D_MODEL)), _full((1, D_FF))],
        out_shape=[jax.ShapeDtypeStruct((L, D_MODEL), f32), jax.ShapeDtypeStruct((L, D_MODEL), bf16),
                   jax.ShapeDtypeStruct((L, D_FF), bf16), jax.ShapeDtypeStruct((L, D_FF), bf16),
                   jax.ShapeDtypeStruct((L, D_MODEL), bf16), jax.ShapeDtypeStruct((16, D_MODEL), f32),
                   jax.ShapeDtypeStruct((1, D_FF), f32)],
        scratch_shapes=[pltpu.VMEM((tm, D_FF), f32)],
        compiler_params=_cparams("arbitrary"),
    )(r1, target, mod, ln1_g, ln1_b, ln2_g, ln2_b, w1, b1, w2, b2)


def _wgrad(a, b, name):
    L, M = a.shape
    N = b.shape[1]
    tm = min(M, 512)
    tn = next(t for t in (1024, 768, 512, 256, 128) if N % t == 0)

    def body(a_ref, b_ref, o_ref):
        o_ref[...] = _dot_tn(a_ref[...], b_ref[...]).astype(bf16)

    return pl.pallas_call(
        body, name=name, grid=(M // tm, N // tn),
        in_specs=[pl.BlockSpec((L, tm), lambda i, j: (0, i)), pl.BlockSpec((L, tn), lambda i, j: (0, j))],
        out_specs=pl.BlockSpec((tm, tn), lambda i, j: (i, j)),
        out_shape=jax.ShapeDtypeStruct((M, N), bf16),
        compiler_params=_cparams("parallel", "parallel"),
    )(a, b)


def _wgrad_blob(blob, a, b, name, place_of):
    L, M = a.shape
    N = b.shape[1]

    def body(blob_ref, a_ref, b_ref, o_ref):
        o_ref[0] = _dot_tn(a_ref[...], b_ref[...]).astype(bf16)

    return pl.pallas_call(
        body, name=name, grid=(M // WG_TM, N // D_MODEL),
        in_specs=[pl.BlockSpec(memory_space=pl.ANY), pl.BlockSpec((L, WG_TM), lambda t, n: (0, t)),
                  pl.BlockSpec((L, D_MODEL), lambda t, n: (0, n))],
        out_specs=pl.BlockSpec((1, WG_TM, D_MODEL), lambda t, n: (*place_of(t, n), 0)),
        out_shape=jax.ShapeDtypeStruct(blob.shape, bf16), input_output_aliases={0: 0},
        compiler_params=_cparams("parallel", "parallel"),
    )(blob, a, b)


def _outproj_bwd(dr1, mix, y, z, mod, norm_w, w_out, gb):
    L = dr1.shape[0]
    tm = DENSE_TM
    nt = L // tm

    def body(dr1_ref, mix_ref, y_ref, z_ref, mod_ref, nw_ref, w_ref, gb_ref,
             dy_ref, dz_ref, do_ref, dmix_ref, acc_ref, sib_ref, send_sems, recv_sems):
        start, wait = _to_sibling_job(gb_ref, sib_ref, send_sems, recv_sems)

        @pl.when(pl.program_id(0) == 0)
        def _():
            acc_ref[...] = jnp.zeros_like(acc_ref)
            start()

        @pl.when(pl.program_id(0) == nt - 1)
        def _():
            wait()

        dr1 = dr1_ref[...]
        acc_ref[0:1, :] += jnp.sum(dr1 * mix_ref[...], axis=0, keepdims=True)
        dmix = (dr1 * (1.0 + mod_ref[2:3, :])).astype(bf16)
        dmix_ref[...] = dmix
        dyn = jnp.concatenate([_dot_nt(dmix, w_ref[0]), _dot_nt(dmix, w_ref[1])], axis=1)
        do_ref[...] = jnp.concatenate([_dot_nt(dmix, w_ref[2]), _dot_nt(dmix, w_ref[3])], axis=1).astype(bf16)
        yv, zv = y_ref[...], z_ref[...]
        _, n, rss, sz = _gated_norm(yv, zv, nw_ref[...])
        acc_ref[1:2, :] += jnp.sum(dyn * n, axis=0, keepdims=True)
        dn = dyn * nw_ref[...]
        parts = []
        for g in range(SSD_GROUPS):
            sl = slice(512 * g, 512 * (g + 1))
            dng, ng = dn[:, sl], n[:, sl]
            parts.append(rss[g] * (dng - ng * jnp.mean(dng * ng, axis=-1, keepdims=True)))
        dhg = jnp.concatenate(parts, axis=1)
        dy_ref[...] = dhg * (zv * sz)
        dz_ref[...] = (dhg * yv * (sz * (1.0 + zv * (1.0 - sz)))).astype(bf16)

    return pl.pallas_call(
        body, name="outproj_bwd", grid=(nt,),
        in_specs=[_rows(tm, D_MODEL)] * 4 + [_full((8, D_MODEL)), _full((1, D_MODEL)), _resident((N_CHIPS, 512, D_MODEL)),
                  _ANY_SPEC],
        out_specs=[_rows(tm, D_MODEL)] * 4 + [_full((8, D_MODEL)), _ANY_SPEC],
        out_shape=[jax.ShapeDtypeStruct((L, D_MODEL), f32)] + [jax.ShapeDtypeStruct((L, D_MODEL), bf16)] * 3
        + [jax.ShapeDtypeStruct((8, D_MODEL), f32), jax.ShapeDtypeStruct((N_CHIPS,) + gb.shape[2:], bf16)],
        scratch_shapes=_sems(N_CHIPS),
        compiler_params=_cparams("arbitrary"),
    )(dr1, mix, y, z, mod, norm_w, w_out, gb)


def _attn_bwd(q, kv, do, lse, sinks, pb, gb2):
    L = q.shape[0]
    nb = L // CHUNK
    T = CHUNK

    def body(sink_ref, q_ref, kvp_ref, kvc_ref, do_ref, lse_ref, pb_ref, gb2_ref, dq_ref, dkv_ref, dsink_ref, chips_ref,
             sib2_ref, carry, send_sems, recv_sems, send_sems2, recv_sems2):
        n = pl.program_id(0)
        start, wait = _to_chips_job(pb_ref, chips_ref, send_sems, recv_sems)
        start2, wait2 = _to_sibling_job(gb2_ref, sib2_ref, send_sems2, recv_sems2)

        @pl.when(n == 0)
        def _():
            carry[...] = jnp.zeros_like(carry)
            dsink_ref[...] = jnp.zeros_like(dsink_ref)
            start()
            start2()

        @pl.when(n < nb)
        def _():
            first = n == 0
            ext = _kv_halves(kvp_ref[...], kvc_ref[...], first)
            dist, valid = _attn_masks(first)
            dist4, valid4 = jnp.concatenate([dist] * 4, axis=0), jnp.concatenate([valid] * 4, axis=0)
            lane1 = lax.broadcasted_iota(jnp.int32, (1, 128), 1)
            lse = lse_ref[...]
            qts = [q_ref[:, 128 * b:128 * (b + 1)].astype(f32).T.astype(bf16) for b in range(8)]
            dots = [do_ref[:, 128 * b:128 * (b + 1)].astype(f32).T.astype(bf16) for b in range(8)]
            acck = [None, None]
            accv = [None, None]
            dsink = jnp.zeros((1, 128), f32)
            dq_acc = [None] * 8
            for g in range(2):
                for half in range(2):
                    k_ext, v_ext = ext[g][half]
                    blks, heads, slope, sink = _head_stack(g, half, sink_ref)
                    qs = jnp.concatenate([q_ref[:, 128 * b:128 * (b + 1)] for b in blks], axis=0)
                    dos = jnp.concatenate([do_ref[:, 128 * b:128 * (b + 1)] for b in blks], axis=0)
                    rows = slice(HEAD_DIM * half, HEAD_DIM * (half + 1))
                    qt = jnp.concatenate([qts[b][rows, :] for b in blks], axis=1)
                    dot_ = jnp.concatenate([dots[b][rows, :] for b in blks], axis=1)
                    lse_col = jnp.concatenate([lse[:, h:h + 1] for h in heads], axis=0)
                    s = _dot_nt(qs, k_ext) * ATTN_SCALE - slope * dist4
                    p = jnp.where(valid4, jnp.exp(s - lse_col), 0.0)
                    dp = _dot_nt(dos, v_ext)
                    delta = jnp.sum(p * dp, axis=-1, keepdims=True)
                    ds = (p * (dp - delta) * ATTN_SCALE).astype(bf16)
                    sd = jnp.exp(sink - lse_col) * delta
                    dqs = _dot(ds, k_ext)
                    for i, b in enumerate(blks):
                        seg = slice(T * i, T * (i + 1))
                        dq_acc[b] = dqs[seg, :] if dq_acc[b] is None else dq_acc[b] + dqs[seg, :]
                        dsink = dsink - jnp.where(lane1 == heads[i], jnp.sum(sd[seg, :], axis=0, keepdims=True), 0.0)
                    dk = _dot(qt, ds)
                    dv = _dot(dot_, p.astype(bf16))
                    acck[g] = dk if acck[g] is None else acck[g] + dk
                    accv[g] = dv if accv[g] is None else accv[g] + dv
            dq_ref[...] = jnp.concatenate([a.astype(bf16) for a in dq_acc], axis=1)
            dsink_ref[...] += dsink
            dkv = jnp.concatenate([jnp.concatenate(acck, axis=0).T, jnp.concatenate(accv, axis=0).T], axis=1)
            dkv_ref[...] = (carry[...] + dkv[0:T, :]).astype(bf16)
            carry[...] = dkv[T:2 * T, :]

        @pl.when(n == nb)
        def _():
            dkv_ref[...] = carry[...].astype(bf16)
            wait()
            wait2()

    cur = lambda i: (jnp.minimum(i, nb - 1), 0)
    return pl.pallas_call(
        body, name="attn_bwd", grid=(nb + 1,),
        in_specs=[pl.BlockSpec(memory_space=pltpu.SMEM), pl.BlockSpec((T, D_MODEL), cur),
                  pl.BlockSpec((T, 256), lambda i: (jnp.maximum(jnp.minimum(i, nb - 1) - 1, 0), 0)),
                  pl.BlockSpec((T, 256), cur), pl.BlockSpec((T, D_MODEL), cur), pl.BlockSpec((T, 128), cur), _ANY_SPEC,
                  _ANY_SPEC],
        out_specs=[pl.BlockSpec((T, D_MODEL), cur), pl.BlockSpec((T, 256), lambda i: (jnp.maximum(i - 1, 0), 0)),
                   _full((1, 128)), _ANY_SPEC, _ANY_SPEC],
        out_shape=[jax.ShapeDtypeStruct((L, D_MODEL), bf16), jax.ShapeDtypeStruct((L, 256), bf16),
                   jax.ShapeDtypeStruct((1, 128), f32), jax.ShapeDtypeStruct((N_CHIPS - 1,) + pb.shape[1:], bf16),
                   jax.ShapeDtypeStruct((N_CHIPS,) + gb2.shape[2:], bf16)],
        scratch_shapes=[pltpu.VMEM((T, 256), f32)] + _sems(N_CHIPS - 1) + _sems(N_CHIPS),
        compiler_params=_cparams("arbitrary"),
    )(sinks, q, kv, kv, do, lse, pb, gb2)


def _ssd_bwd(xc, dt_raw, dy, prev_all, dt_bias, a_log, d_skip_x, e_mat, g, pb2):
    L = xc.shape[0]
    nc = L // CHUNK
    T = CHUNK
    RG = g.shape[0]

    def body(xc_ref, dt_ref, dy_ref, prev_ref, dtb_ref, al_ref, dsk_ref, e_ref, g_in_ref, pb2_ref,
             dxc_ref, ddt_ref, acc_ref, dd_ref, g_ref, chips2_ref, dst, dxs_s, send_sems, recv_sems, send_sems2, recv_sems2):
        start, wait = _share_job(g_ref, send_sems, recv_sems, RG)
        start2, wait2 = _to_chips_job(pb2_ref, chips2_ref, send_sems2, recv_sems2)

        @pl.when(pl.program_id(0) == 0)
        def _():
            dst[...] = jnp.zeros_like(dst)
            acc_ref[...] = jnp.zeros_like(acc_ref)
            dd_ref[...] = jnp.zeros_like(dd_ref)
            start()
            start2()

        @pl.when(pl.program_id(0) == nc - 1)
        def _():
            wait()
            wait2()

        lane1 = lax.broadcasted_iota(jnp.int32, (1, 128), 1)
        a_row = jnp.where(lane1 < SSD_HEADS, -jnp.exp(al_ref[...]), 0.0)
        e_mat_v = e_ref[...]
        dt, a, cum, dtx, cumx, r, c = _ssd_chunk_common(dt_ref[...], dtb_ref[...], a_row)
        cum_t = cum.T
        ex = jnp.exp(cumx)
        last = cumx[T - 1:T, :]
        wx = jnp.exp(last - cumx)
        cdx = jnp.exp(last)
        xs = xc_ref[:, 0:SSD_WIDTH]
        X = xs * dtx
        Xb = X.astype(bf16)
        Xdb = (X * wx).astype(bf16)
        dyv = dy_ref[...]
        prev = prev_ref[0]
        prevb = prev.astype(bf16)
        dnew = dst[...]
        dnewb = dnew.astype(bf16)
        tri = c <= r
        lane = lax.broadcasted_iota(jnp.int32, (T, 128), 1)
        sub = lax.broadcasted_iota(jnp.int32, (128, T), 0)
        lo = lane < HEAD_DIM

        def red(vals, g):
            return _dot_sel_nt(vals, e_mat_v[:, 512 * g:512 * (g + 1)])

        de = jnp.zeros((T, 128), f32)
        dw = jnp.zeros((T, 128), f32)
        ddt_x = jnp.zeros((T, 128), f32)
        dcum_col = jnp.zeros((T, 128), f32)
        dcum_row = jnp.zeros((128, T), f32)
        dprev_parts, dBs, dCs = [], [], []
        for g in range(SSD_GROUPS):
            s5 = slice(512 * g, 512 * (g + 1))
            Bg = xc_ref[:, 1024 + 128 * g:1152 + 128 * g].astype(bf16)
            Cg = xc_ref[:, 1280 + 128 * g:1408 + 128 * g].astype(bf16)
            G = _dot_nt(Cg, Bg)
            Z = _dot(Cg, prevb[:, s5])
            dyg = dyv[:, s5]
            dZb = (dyg * ex[:, s5]).astype(bf16)
            dXd = _dot(Bg, dnewb[:, s5])
            dC = _dot_nt(dZb, prevb[:, s5])
            dB = _dot_nt(Xdb[:, s5], dnewb[:, s5])
            dprev_parts.append(_dot_tn(Cg, dZb) + dnew[:, s5] * cdx[:, s5])
            de = de + red(dyg * Z, g)
            dw = dw + red(dXd * X[:, s5], g)
            dXg = dXd * wx[:, s5]
            dG = jnp.zeros((T, T), f32)
            for j in range(4):
                blk = 4 * g + j
                sl = slice(128 * blk, 128 * (blk + 1))
                Xblk = Xb[:, sl]
                dyblk = dyv[:, sl]
                dyblk_b = dyblk.astype(bf16)
                dxh = []
                for half in range(2):
                    h = 2 * blk + half
                    seg = jnp.minimum(cum[:, h:h + 1] - cum_t[h:h + 1, :], 0.0)
                    Lm = jnp.where(tri, jnp.exp(seg), 0.0)
                    M = G * Lm
                    dyh = jnp.where(lo if half == 0 else ~lo, dyblk, 0.0).astype(bf16)
                    dM = _dot_nt(dyh, Xblk)
                    dG = dG + dM * Lm
                    Q = dM * M
                    dcum_col = dcum_col + jnp.where(lane == h, jnp.sum(Q, axis=1, keepdims=True), 0.0)
                    dcum_row = dcum_row + jnp.where(sub == h, jnp.sum(Q, axis=0, keepdims=True), 0.0)
                    dxh.append(_dot_tn(M.astype(bf16), dyblk_b))
                dXblk = dXg[:, 128 * j:128 * (j + 1)] + jnp.where(lo, dxh[0], dxh[1])
                xsb = xs[:, sl]
                dxs_s[:, sl] = dXblk * dtx[:, sl] + dsk_ref[:, sl] * dyblk
                ddt_x = ddt_x + _dot_sel_nt(dXblk * xsb, e_mat_v[:, sl])
                dd_ref[:, sl] += jnp.sum(dyblk * xsb, axis=0, keepdims=True)
            dGb = dG.astype(bf16)
            dCs.append(dC + _dot(dGb, Bg))
            dBs.append(dB + _dot_tn(dGb, Cg))
        e16 = jnp.exp(cum)
        cum_last = cum[T - 1:T, :]
        w16 = jnp.exp(cum_last - cum)
        dcd = jnp.sum(dnew * prev, axis=0, keepdims=True)
        dcd16 = red(dcd[:, 0:512], 0) + red(dcd[:, 512:1024], 1)
        dww = dw * w16
        extra = jnp.sum(dww, axis=0, keepdims=True) + dcd16 * jnp.exp(cum_last)
        rowi = lax.broadcasted_iota(jnp.int32, (T, 128), 0)
        dcum = dcum_col - dcum_row.T + de * e16 - dww + jnp.where(rowi == T - 1, extra, 0.0)
        da = _sel_dot((c >= r).astype(bf16), dcum)
        ddt = ddt_x + da * a_row
        acc_ref[0:1, :] += jnp.sum(da * dt, axis=0, keepdims=True)
        ddt_raw = jnp.where(lane < SSD_HEADS, ddt * _sigmoid(dt_ref[...] + dtb_ref[...]), 0.0)
        ddt_ref[...] = ddt_raw
        acc_ref[1:2, :] += jnp.sum(ddt_raw, axis=0, keepdims=True)
        dxc_ref[:, 0:SSD_WIDTH] = dxs_s[...]
        dxc_ref[:, 1024:1280] = jnp.concatenate(dBs, axis=1)
        dxc_ref[:, 1280:1536] = jnp.concatenate(dCs, axis=1)
        dst[...] = jnp.concatenate(dprev_parts, axis=1)

    rev = lambda i: (nc - 1 - i, 0)
    return pl.pallas_call(
        body, name="ssd_bwd", grid=(nc,),
        in_specs=[pl.BlockSpec((T, CONV_DIM), rev), pl.BlockSpec((T, 128), rev), pl.BlockSpec((T, SSD_WIDTH), rev),
                  pl.BlockSpec((1, SSD_STATE, SSD_WIDTH), lambda i: (nc - 1 - i, 0, 0)),
                  _full((1, 128)), _full((1, 128)), _full((1, SSD_WIDTH)), _full((128, SSD_WIDTH)), _ANY_SPEC, _ANY_SPEC],
        out_specs=[pl.BlockSpec((T, CONV_DIM), rev), pl.BlockSpec((T, 128), rev), _full((8, 128)),
                   _full((1, SSD_WIDTH)), _ANY_SPEC, _ANY_SPEC],
        out_shape=[jax.ShapeDtypeStruct((L, CONV_DIM), f32), jax.ShapeDtypeStruct((L, 128), f32),
                   jax.ShapeDtypeStruct((8, 128), f32), jax.ShapeDtypeStruct((1, SSD_WIDTH), f32),
                   jax.ShapeDtypeStruct(g.shape, f32), jax.ShapeDtypeStruct((N_CHIPS - 1,) + pb2.shape[1:], bf16)],
        input_output_aliases={8: 4},
        scratch_shapes=[pltpu.VMEM((SSD_STATE, SSD_WIDTH), f32), pltpu.VMEM((T, SSD_WIDTH), f32)] + _sems(1)
        + _sems(N_CHIPS - 1),
        compiler_params=_cparams("arbitrary"),
    )(xc, dt_raw, dy, prev_all, dt_bias, a_log, d_skip_x, e_mat, g, pb2)


def _conv_bwd(dxc, xr, conv_w, conv_b, g):
    L = dxc.shape[0]
    tm = 256
    nt = L // tm
    RG = g.shape[0]

    def body(dxc_ref, xr_ref, xh_ref, cw_ref, cb_ref, g_in_ref, dxr_ref, acc_ref, g_ref, carry, buf, ext, send_sems, recv_sems):
        i = pl.program_id(0)
        start, wait = _share_job(g_ref, send_sems, recv_sems, RG)

        @pl.when(i == 0)
        def _():
            carry[...] = jnp.zeros_like(carry)
            acc_ref[...] = jnp.zeros_like(acc_ref)
            ext[tm + 16:tm + CHUNK, :] = jnp.zeros((CHUNK - 16, CONV_DIM), bf16)
            start()

        @pl.when(i == nt - 1)
        def _():
            wait()

        buf[0:8, :] = jnp.where(i == nt - 1, 0.0, xh_ref[...])
        u = xr_ref[...]
        buf[8:8 + tm, :] = u
        pre = cb_ref[...] + cw_ref[CONV_K - 1:CONV_K, :] * u
        for k in range(CONV_K - 1):
            pre = pre + cw_ref[k:k + 1, :] * buf[5 + k:5 + k + tm, :]
        sg = _sigmoid(pre)
        dpre = dxc_ref[...] * (sg * (1.0 + pre * (1.0 - sg)))
        acc_ref[4:5, :] += jnp.sum(dpre, axis=0, keepdims=True)
        dpb = dpre.astype(bf16)
        ext[0:tm, :] = dpb
        ext[tm:tm + 16, :] = carry[...]
        acc_ref[CONV_K - 1:CONV_K, :] += jnp.sum(u * dpre, axis=0, keepdims=True)
        du = cw_ref[CONV_K - 1:CONV_K, :] * dpre
        r = lax.broadcasted_iota(jnp.int32, (CHUNK, 2 * CHUNK), 0)
        c = lax.broadcasted_iota(jnp.int32, (CHUNK, 2 * CHUNK), 1)
        for j in range(1, CONV_K):
            move = (c == r + j).astype(bf16)
            up = jnp.concatenate([_dot(move, ext[CHUNK * b:CHUNK * (b + 2), :]) for b in range(tm // CHUNK)], axis=0)
            k = CONV_K - 1 - j
            du = du + cw_ref[k:k + 1, :] * up
            acc_ref[k:k + 1, :] += jnp.sum(u * up, axis=0, keepdims=True)
        dxr_ref[...] = du.astype(bf16)
        carry[...] = dpb[0:16, :]

    rev = lambda i: (nt - 1 - i, 0)
    return pl.pallas_call(
        body, name="conv_bwd", grid=(nt,),
        in_specs=[pl.BlockSpec((tm, CONV_DIM), rev), pl.BlockSpec((tm, CONV_DIM), rev),
                  pl.BlockSpec((8, CONV_DIM), lambda i: (jnp.maximum((nt - 1 - i) * (tm // 8) - 1, 0), 0)),
                  _full((CONV_K, CONV_DIM)), _full((1, CONV_DIM)), _ANY_SPEC],
        out_specs=[pl.BlockSpec((tm, CONV_DIM), rev), _full((8, CONV_DIM)), _ANY_SPEC],
        out_shape=[jax.ShapeDtypeStruct((L, CONV_DIM), bf16), jax.ShapeDtypeStruct((8, CONV_DIM), f32),
                   jax.ShapeDtypeStruct(g.shape, f32)],
        input_output_aliases={5: 2},
        scratch_shapes=[pltpu.VMEM((16, CONV_DIM), bf16), pltpu.VMEM((tm + 8, CONV_DIM), f32),
                        pltpu.VMEM((tm + CHUNK, CONV_DIM), bf16)] + _sems(1),
        compiler_params=_cparams("arbitrary"),
    )(dxc, xr, xr, conv_w, conv_b, g)


def _inproj_bwd(dz, dxr, dq, dkv, ddt, dr1, x, mod, ln_g, ln_b, w_in, pb):
    L = x.shape[0]
    tm = DENSE_TM
    nt = L // tm

    def body(dz_ref, dxr_ref, dq_ref, dkv_ref, ddt_ref, dr1_ref, x_ref, mod_ref, g_ref, b_ref, w_ref, pb_ref,
             dx_ref, acc_ref, chips_ref, send_sems, recv_sems):
        start, wait = _to_chips_job(pb_ref, chips_ref, send_sems, recv_sems)

        @pl.when(pl.program_id(0) == 0)
        def _():
            acc_ref[...] = jnp.zeros_like(acc_ref)
            start()

        @pl.when(pl.program_id(0) == nt - 1)
        def _():
            wait()

        du1 = (_dot(dz_ref[...], w_ref[W_Z, :]) + _dot(dxr_ref[...], w_ref[W_XBC, :])
               + _dot(dq_ref[...], w_ref[W_Q, :]) + _dot(dkv_ref[...], w_ref[W_KV, :])
               + _dot(ddt_ref[...].astype(bf16), w_ref[W_DT, :]))
        xhat, rstd = _ln_stats(x_ref[...])
        h0 = xhat * g_ref[...] + b_ref[...]
        acc_ref[0:1, :] += jnp.sum(du1 * h0, axis=0, keepdims=True)
        acc_ref[1:2, :] += jnp.sum(du1, axis=0, keepdims=True)
        dh0 = du1 * (1.0 + mod_ref[1:2, :]) + ALPHA * dr1_ref[...]
        acc_ref[2:3, :] += jnp.sum(dh0 * xhat, axis=0, keepdims=True)
        acc_ref[3:4, :] += jnp.sum(dh0, axis=0, keepdims=True)
        dx_ref[...] = _ln_bwd(dh0, xhat, rstd, g_ref[...])

    v = _full((1, D_MODEL))
    return pl.pallas_call(
        body, name="inproj_bwd", grid=(nt,),
        in_specs=[_rows(tm, D_MODEL), _rows(tm, CONV_DIM), _rows(tm, D_MODEL), _rows(tm, 256), _rows(tm, 128),
                  _rows(tm, D_MODEL), _rows(tm, D_MODEL), _full((8, D_MODEL)), v, v, _resident((PROJ_WIDTH, D_MODEL)),
                  _ANY_SPEC],
        out_specs=[_rows(tm, D_MODEL), _full((8, D_MODEL)), _ANY_SPEC],
        out_shape=[jax.ShapeDtypeStruct((L, D_MODEL), f32), jax.ShapeDtypeStruct((8, D_MODEL), f32),
                   jax.ShapeDtypeStruct((N_CHIPS - 1,) + pb.shape[1:], bf16)],
        scratch_shapes=_sems(N_CHIPS - 1),
        compiler_params=_cparams("arbitrary"),
    )(dz, dxr, dq, dkv, ddt, dr1, x, mod, ln_g, ln_b, w_in, pb)


def _adamw_math(w, g, m, v):
    m = ADAM_B1 * m + (1.0 - ADAM_B1) * g
    v = ADAM_B2 * v + (1.0 - ADAM_B2) * (g * g)
    m_hat = m / (1.0 - ADAM_B1 ** ADAM_STEP)
    v_hat = v / (1.0 - ADAM_B2 ** ADAM_STEP)
    delta = -ADAM_LR * (m_hat / (jnp.sqrt(v_hat) + ADAM_EPS) + ADAM_WD * w)
    return delta, m, v


def _adamw(w, g, m, v, name):
    R, C = w.shape

    def body(w_ref, g_ref, m_ref, v_ref, d_ref, m2_ref, v2_ref):
        d_ref[...], m2_ref[...], v2_ref[...] = _adamw_math(w_ref[...], g_ref[...], m_ref[...], v_ref[...])

    cap = max(8, ADAMW_BLOCK_ELEMS // C)
    tr = R if R <= cap else next(t for t in range(cap - cap % 8, 7, -8) if R % t == 0)
    spec = pl.BlockSpec((tr, C), lambda i: (i, 0))
    return pl.pallas_call(
        body, name=name, grid=(R // tr,), in_specs=[spec] * 4, out_specs=[spec] * 3,
        out_shape=[jax.ShapeDtypeStruct((R, C), f32)] * 3, compiler_params=_cparams("parallel"),
    )(w, g, m, v)


def _adamw_rows(w, g, m, v):
    R, _, C = w.shape
    tr = R // 4

    def body(w_ref, g_ref, m_ref, v_ref, d_ref, m2_ref, v2_ref):
        d_ref[...], m2_ref[...], v2_ref[...] = _adamw_math(w_ref[...], g_ref[...], m_ref[...], v_ref[...])

    spec = pl.BlockSpec((tr, 1, C), lambda i: (i, 0, 0))
    return pl.pallas_call(
        body, name="adamw_w_in", grid=(R // tr,), in_specs=[spec] * 4, out_specs=[spec] * 3,
        out_shape=[jax.ShapeDtypeStruct((R, 1, C), f32)] * 3, compiler_params=_cparams("parallel"),
    )(w, g, m, v)


ADA_COLS = 6 * D_MODEL // N_CHIPS
ADA_TN = 512


COND_LANES = 512


def _prologue(cond, ada_w, ada_b, blob):
    R = blob.shape[0]

    def body(cond_ref, w_ref, b_ref, blob_ref, call_ref, mod_ref, wall_ref, mod_s, stage, gs, gr, ms, mr, ws, wr, local_sem):
        x, y, c = _place()
        start_w, finish_w = _gather_job(blob_ref, wall_ref, ws, wr, R)
        start_w()

        def rows(ref, px, py, pc):
            return ref.at[pl.ds(pl.multiple_of((4 * px + 2 * py + pc) * 8, 8), 8), :]

        mine = pltpu.make_async_copy(cond_ref, rows(call_ref, x, y, c), local_sem)
        mine.start()
        sends = [_remote(cond_ref, rows(call_ref, x, y, c), gs, gr, m - 1, _flip(x, y, c, m)) for m in range(1, N_DEV)]
        for cp in sends:
            cp.start()
        for m in range(1, N_DEV):
            peer = _flip(x, y, c, m)
            _remote(cond_ref, rows(call_ref, *peer), gs, gr, m - 1, peer).wait_recv()
        for cp in sends:
            cp.wait_send()
        mine.wait()

        lo = jnp.concatenate([call_ref[8 * d:8 * d + 1, :] for d in range(N_DEV)], axis=0)
        hi = jnp.concatenate([call_ref[8 * d + 1:8 * d + 2, :] for d in range(N_DEV)], axis=0)
        mod_all = (_dot_exact(lo * _sigmoid(lo), w_ref[0:COND_LANES, :]) + _dot_exact(hi * _sigmoid(hi), w_ref[COND_LANES:, :])
                   + b_ref[...])
        for d in range(N_DEV):
            mod_s[8 * d:8 * d + 8, :] = jnp.broadcast_to(mod_all[d:d + 1, :], (8, ADA_COLS))

        mine = pltpu.make_async_copy(rows(mod_s, x, y, c), mod_ref.at[2 * x + y], local_sem)
        mine.start()
        sends = []
        for m in range(1, N_CHIPS):
            peer = _flip(x, y, c, 2 * m)
            sends.append(_remote(rows(mod_s, *peer), mod_ref.at[2 * x + y], ms, mr, m - 1, peer))
        for cp in sends:
            cp.start()
        for m in range(1, N_CHIPS):
            px, py, pc = _flip(x, y, c, 2 * m)
            _remote(rows(mod_s, x, y, c), mod_ref.at[2 * px + py], ms, mr, m - 1, (px, py, pc)).wait_recv()
        for cp in sends:
            cp.wait_send()
        mine.wait()

        for k in range(R // STAGE_ROWS):
            part = pl.ds(STAGE_ROWS * k, STAGE_ROWS)
            cin = pltpu.make_async_copy(blob_ref.at[part, :], stage, local_sem)
            cin.start()
            cin.wait()
            cout = pltpu.make_async_copy(stage, wall_ref.at[2 * x + y, part, :], local_sem)
            cout.start()
            cout.wait()
        finish_w()

    return pl.pallas_call(
        body, name="prologue",
        out_shape=[jax.ShapeDtypeStruct((8 * N_DEV, COND_LANES), f32), jax.ShapeDtypeStruct((N_CHIPS, 8, ADA_COLS), f32),
                   jax.ShapeDtypeStruct((N_CHIPS, R, D_MODEL), bf16)],
        in_specs=[_VMEM_SPEC, _VMEM_SPEC, _VMEM_SPEC, _ANY_SPEC], out_specs=[_VMEM_SPEC, _VMEM_SPEC, _ANY_SPEC],
        scratch_shapes=[pltpu.VMEM((8 * N_DEV, ADA_COLS), f32), pltpu.VMEM((STAGE_ROWS, D_MODEL), bf16)]
        + _sems(N_DEV - 1) + _sems(N_CHIPS - 1) + _sems(6) + [pltpu.SemaphoreType.DMA],
        compiler_params=pltpu.CompilerParams(vmem_limit_bytes=VMEM_LIMIT),
    )(cond, ada_w, ada_b, blob)


def _ada_bwd(c_all, dmod, w, m, v):
    def body(c_ref, d_ref, w_ref, m_ref, v_ref, g_ref, dl_ref, m2_ref, v2_ref):
        cv = c_ref[...]
        g = lax.dot_general(cv * _sigmoid(cv), d_ref[...], (((0,), (0,)), ((), ())), preferred_element_type=f32,
                            precision=lax.Precision.HIGHEST)
        g_ref[...] = g
        dl_ref[...], m2_ref[...], v2_ref[...] = _adamw_math(w_ref[...], g, m_ref[...], v_ref[...])

    wspec = pl.BlockSpec((D_MODEL, ADA_TN), lambda j: (0, j))
    return pl.pallas_call(
        body, name="ada_bwd", grid=(ADA_COLS // ADA_TN,),
        in_specs=[_full((N_DEV, D_MODEL)), pl.BlockSpec((N_DEV, ADA_TN), lambda j: (0, j)), wspec, wspec, wspec],
        out_specs=[wspec] * 4, out_shape=[jax.ShapeDtypeStruct((D_MODEL, ADA_COLS), f32)] * 4,
        compiler_params=_cparams("parallel"),
    )(c_all, dmod, w, m, v)


SMALL_SLOTS = (("ada_b", 6144), ("ln_in_g", 1024), ("ln_in_b", 1024), ("conv_b", 1536), ("dt_bias", 128), ("a_log", 128),
               ("d_skip", 128), ("ssd_norm_w", 1024), ("attn_sinks", 128), ("ln1_g", 1024), ("ln1_b", 1024),
               ("b_ff1", 4096), ("b_ff2", 1024), ("ln2_g", 1024), ("ln2_b", 1024), ("conv_w", 6144), ("loss", 1024))
SMALL_N = sum(n for _, n in SMALL_SLOTS)
SMALL_OFF = {name: sum(n for _, n in SMALL_SLOTS[:i]) for i, (name, _) in enumerate(SMALL_SLOTS)}
SMALL_PARAMS = tuple(name for name, _ in SMALL_SLOTS[:15])
assert SMALL_N % 1024 == 0


def _small_pack(acc_in, acc_out, acc_mlp, db1, acc_conv, acc_ssd, dd_x, dsink, alog, e_mat):
    def body(in_ref, out_ref, mlp_ref, db1_ref, conv_ref, ssd_ref, dd_ref, sink_ref, al_ref, e_ref, o_ref):
        def put(name, val, at=0):
            off = SMALL_OFF[name] + at
            o_ref[:, off:off + val.shape[1]] = val

        for k, row in enumerate((in_ref[1:2, :], in_ref[0:1, :], out_ref[0:1, :], mlp_ref[A_SH2:A_SH2 + 1, :],
                                 mlp_ref[A_SC2:A_SC2 + 1, :], mlp_ref[A_G2:A_G2 + 1, :])):
            put("ada_b", row, D_MODEL * k)
        put("ln_in_g", in_ref[2:3, :])
        put("ln_in_b", in_ref[3:4, :])
        put("conv_b", conv_ref[4:5, :])
        put("dt_bias", ssd_ref[1:2, :])
        put("a_log", ssd_ref[0:1, :] * (-jnp.exp(al_ref[...])))
        put("d_skip", _dot_sel_nt(jnp.broadcast_to(dd_ref[...], (8, SSD_WIDTH)), e_ref[...])[0:1, :])
        put("ssd_norm_w", out_ref[1:2, :])
        put("attn_sinks", sink_ref[...])
        put("ln1_g", mlp_ref[A_LN1G:A_LN1G + 1, :])
        put("ln1_b", mlp_ref[A_LN1B:A_LN1B + 1, :])
        put("b_ff1", db1_ref[...])
        put("b_ff2", mlp_ref[A_B2:A_B2 + 1, :])
        put("ln2_g", mlp_ref[A_LN2G:A_LN2G + 1, :])
        put("ln2_b", mlp_ref[A_LN2B:A_LN2B + 1, :])
        for k in range(CONV_K):
            put("conv_w", conv_ref[k:k + 1, :], CONV_DIM * k)
        put("loss", mlp_ref[A_LOSS:A_LOSS + 1, :])

    return pl.pallas_call(body, name="small_pack", out_shape=jax.ShapeDtypeStruct((1, SMALL_N), f32),
                          compiler_params=_cparams())(acc_in, acc_out, acc_mlp, db1, acc_conv, acc_ssd, dd_x, dsink, alog, e_mat)


def _small_update(gathered, params, moms, vels):
    k = len(SMALL_PARAMS)

    def body(g_ref, *refs):
        w_refs, m_refs, v_refs, outs = refs[:k], refs[k:2 * k], refs[2 * k:3 * k], refs[3 * k:]

        def total(name, n):
            off = SMALL_OFF[name]
            g = g_ref[0:1, off:off + n]
            for i in range(1, N_DEV):
                g = g + g_ref[i:i + 1, off:off + n]
            return g

        for j, name in enumerate(SMALL_PARAMS):
            n = w_refs[j].shape[1]
            g = total(name, max(n, 128))[:, :n]
            outs[4 * j][...] = g
            outs[4 * j + 1][...], outs[4 * j + 2][...], outs[4 * j + 3][...] = _adamw_math(
                w_refs[j][...], g, m_refs[j][...], v_refs[j][...])
        outs[4 * k][...] = total("conv_w", CONV_K * CONV_DIM)
        outs[4 * k + 1][...] = total("loss", D_MODEL)

    shapes = [jax.ShapeDtypeStruct(p.shape, f32) for p in params for _ in range(4)]
    shapes += [jax.ShapeDtypeStruct((1, CONV_K * CONV_DIM), f32), jax.ShapeDtypeStruct((1, D_MODEL), f32)]
    return pl.pallas_call(body, name="small_update", out_shape=shapes,
                          compiler_params=_cparams())(gathered, *params, *moms, *vels)


def _place():
    return lax.axis_index("x"), lax.axis_index("y"), lax.axis_index("c")


def _flip(x, y, c, m):
    return (1 - x if m & 4 else x, 1 - y if m & 2 else y, 1 - c if m & 1 else c)


_VMEM_SPEC = pl.BlockSpec(memory_space=pltpu.VMEM)
_ANY_SPEC = pl.BlockSpec(memory_space=pl.ANY)


def _allgather8(v, name):
    n = v.shape[1]

    def body(v_ref, out_ref, send_sems, recv_sems, local_sem):
        x, y, c = _place()

        def rows(px, py, pc):
            return out_ref.at[pl.ds(pl.multiple_of((4 * px + 2 * py + pc) * 8, 8), 8), :]

        def copy(m, src, dst, to):
            return pltpu.make_async_remote_copy(src_ref=src, dst_ref=dst, send_sem=send_sems.at[m - 1],
                                                recv_sem=recv_sems.at[m - 1], device_id=to, device_id_type=MESH)

        mine = pltpu.make_async_copy(v_ref, rows(x, y, c), local_sem)
        mine.start()
        sends = [copy(m, v_ref, rows(x, y, c), _flip(x, y, c, m)) for m in range(1, N_DEV)]
        for cp in sends:
            cp.start()
        for m in range(1, N_DEV):
            peer = _flip(x, y, c, m)
            copy(m, v_ref, rows(*peer), peer).wait_recv()
        for cp in sends:
            cp.wait_send()
        mine.wait()

    return pl.pallas_call(
        body, name=name, out_shape=jax.ShapeDtypeStruct((8 * N_DEV, n), f32), in_specs=[_VMEM_SPEC],
        out_specs=_VMEM_SPEC,
        scratch_shapes=[pltpu.SemaphoreType.DMA((N_DEV - 1,)), pltpu.SemaphoreType.DMA((N_DEV - 1,)),
                        pltpu.SemaphoreType.DMA],
    )(v)


def _remote(src, dst, send_sems, recv_sems, k, to):
    return pltpu.make_async_remote_copy(src_ref=src, dst_ref=dst, send_sem=send_sems.at[k], recv_sem=recv_sems.at[k],
                                        device_id=to, device_id_type=MESH)


def _gather_job(blob_ref, out_ref, send_sems, recv_sems, R):
    x, y, c = _place()
    sib = (x, y, 1 - c)
    hr = R // 2

    def half(px, py, pc):
        return out_ref.at[2 * px + py, pl.ds(pl.multiple_of(pc * hr, 16), hr), :]

    my_half = blob_ref.at[pl.ds(pl.multiple_of(c * hr, 16), hr), :]

    def first():
        return [_remote(my_half, half(x, y, c), send_sems, recv_sems, m - 1, _flip(x, y, c, 2 * m))
                for m in range(1, N_CHIPS)]

    def start():
        for cp in first():
            cp.start()

    def finish():
        passed = []
        for m in range(1, N_CHIPS):
            px, py, pc = _flip(x, y, c, 2 * m)
            _remote(my_half, half(px, py, pc), send_sems, recv_sems, m - 1, (px, py, pc)).wait_recv()
            fwd = _remote(half(px, py, pc), half(px, py, pc), send_sems, recv_sems, 2 + m, sib)
            fwd.start()
            passed.append(fwd)
        for m in range(1, N_CHIPS):
            px, py, pc = _flip(x, y, c, 2 * m)
            _remote(my_half, half(px, py, 1 - pc), send_sems, recv_sems, 2 + m, sib).wait_recv()
        for cp in first() + passed:
            cp.wait_send()

    return start, finish


def _to_sibling_job(g_ref, out_ref, send_sems, recv_sems):
    x, y, c = _place()

    def cps():
        return [_remote(g_ref.at[j, 1 - c], out_ref.at[j], send_sems, recv_sems, j, (x, y, 1 - c)) for j in range(N_CHIPS)]

    def start():
        for cp in cps():
            cp.start()

    def wait():
        for cp in cps():
            cp.wait()

    return start, wait


def _to_chips_job(p_ref, out_ref, send_sems, recv_sems):
    x, y, c = _place()

    def cps():
        out = []
        for m in range(1, N_CHIPS):
            px, py, pc = _flip(x, y, c, 2 * m)
            out.append(_remote(p_ref.at[2 * px + py], out_ref.at[m - 1], send_sems, recv_sems, m - 1, (px, py, pc)))
        return out

    def start():
        for cp in cps():
            cp.start()

    def wait():
        for cp in cps():
            cp.wait()

    return start, wait


def _share_job(g_ref, send_sems, recv_sems, R):
    x, y, c = _place()

    def rows(pc):
        return g_ref.at[pl.ds(pl.multiple_of(pc * (R // 2), 8), R // 2), :]

    def start():
        _remote(rows(c), rows(c), send_sems, recv_sems, 0, (x, y, 1 - c)).start()

    def wait():
        _remote(rows(c), rows(1 - c), send_sems, recv_sems, 0, (x, y, 1 - c)).wait_recv()
        _remote(rows(c), rows(c), send_sems, recv_sems, 0, (x, y, 1 - c)).wait_send()

    return start, wait


def _sems(n):
    return [pltpu.SemaphoreType.DMA((n,)), pltpu.SemaphoreType.DMA((n,))]


def _rs_to_sibling(gb):
    def body(g_ref, out_ref, send_sems, recv_sems):
        start, wait = _to_sibling_job(g_ref, out_ref, send_sems, recv_sems)
        start()
        wait()

    return pl.pallas_call(
        body, name="rs_to_sibling", out_shape=jax.ShapeDtypeStruct((N_CHIPS,) + gb.shape[2:], bf16),
        in_specs=[_ANY_SPEC], out_specs=_ANY_SPEC, scratch_shapes=_sems(N_CHIPS),
    )(gb)


def _rs_share(g):
    R = g.shape[0]

    def body(g_ref, out_ref, send_sems, recv_sems):
        start, wait = _share_job(out_ref, send_sems, recv_sems, R)
        start()
        wait()

    return pl.pallas_call(
        body, name="rs_share", out_shape=jax.ShapeDtypeStruct(g.shape, f32), in_specs=[_ANY_SPEC],
        out_specs=_ANY_SPEC, input_output_aliases={0: 0}, scratch_shapes=_sems(1),
    )(g)


RS_TR = 256


def _rs_sum_pair(place, gb, recv, name):
    hr = gb.shape[2]

    def body(pl_ref, g_ref, r_ref, o_ref):
        o_ref[0] = (g_ref[0, 0].astype(f32) + r_ref[0].astype(f32)).astype(bf16)

    return pl.pallas_call(
        body, name=name,
        grid_spec=pltpu.PrefetchScalarGridSpec(
            num_scalar_prefetch=1, grid=(N_CHIPS, hr // RS_TR),
            in_specs=[pl.BlockSpec((1, 1, RS_TR, D_MODEL), lambda j, i, p: (j, p[0], i, 0)),
                      pl.BlockSpec((1, RS_TR, D_MODEL), lambda j, i, p: (j, i, 0))],
            out_specs=pl.BlockSpec((1, RS_TR, D_MODEL), lambda j, i, p: (j, i, 0))),
        out_shape=jax.ShapeDtypeStruct((N_CHIPS, hr, D_MODEL), bf16),
        compiler_params=_cparams("parallel", "parallel"),
    )(place, gb, recv)


def _rs_sum_chips(place, gb, recv_sib, recv_chips, name):
    hr = gb.shape[2]
    nt = hr // RS_TR

    def body(pl_ref, g_ref, r1_ref, r2_ref, o_ref):
        acc = g_ref[0, 0].astype(f32) + r1_ref[0].astype(f32)
        for k in range(N_CHIPS - 1):
            acc = acc + r2_ref[k].astype(f32)
        o_ref[...] = acc

    return pl.pallas_call(
        body, name=name,
        grid_spec=pltpu.PrefetchScalarGridSpec(
            num_scalar_prefetch=1, grid=(nt,),
            in_specs=[pl.BlockSpec((1, 1, RS_TR, D_MODEL), lambda i, p: (p[1], p[0], i, 0)),
                      pl.BlockSpec((1, RS_TR, D_MODEL), lambda i, p: (p[1], i, 0)),
                      pl.BlockSpec((N_CHIPS - 1, RS_TR, D_MODEL), lambda i, p: (0, i, 0))],
            out_specs=pl.BlockSpec((RS_TR, D_MODEL), lambda i, p: (p[0] * nt + i, 0))),
        out_shape=jax.ShapeDtypeStruct((2 * hr, D_MODEL), f32),
        compiler_params=_cparams("parallel"),
    )(place, gb, recv_sib, recv_chips)


def _pad128(v):
    v = v.reshape(1, -1)
    return jnp.pad(v, ((0, 0), (0, 128 - v.shape[1])))


def _row(v):
    return v.reshape(1, -1)


W_COLS = PROJ_WIDTH // N_CHIPS


def _g_in_blocks(gz, gxbc, gdt, gq, gkv):
    g = jnp.concatenate([gz, gxbc, gdt[:W_DT_ROWS], gq, gkv], axis=0)
    return jnp.pad(g.reshape(N_CHIPS, W_COLS, D_MODEL), ((0, 0), (0, D_MODEL - W_COLS), (0, 0)))


def kernel(x, c, ln_in_g, ln_in_b, ada_w, ada_b, w_in, conv_w, conv_b, dt_bias, a_log, d_skip, ssd_norm_w, attn_sinks, w_out, ln1_g, ln1_b, w_ff1, b_ff1, w_ff2, b_ff2, ln2_g, ln2_b, loss_target, m_ln_in_g, m_ln_in_b, m_ada_w, m_ada_b, m_w_in, m_conv_w, m_conv_b, m_dt_bias, m_a_log, m_d_skip, m_ssd_norm_w, m_attn_sinks, m_w_out, m_ln1_g, m_ln1_b, m_w_ff1, m_b_ff1, m_w_ff2, m_b_ff2, m_ln2_g, m_ln2_b, v_ln_in_g, v_ln_in_b, v_ada_w, v_ada_b, v_w_in, v_conv_w, v_conv_b, v_dt_bias, v_a_log, v_d_skip, v_ssd_norm_w, v_attn_sinks, v_w_out, v_ln1_g, v_ln1_b, v_w_ff1, v_b_ff1, v_w_ff2, v_b_ff2, v_ln2_g, v_ln2_b):
    xi, yi, ci = _place()
    chip = 2 * xi + yi
    place = jnp.stack([ci, chip]).astype(jnp.int32)
    x2, tgt = x[0], loss_target[0]

    def as_rows(a):
        return jnp.transpose(a, (2, 0, 1))

    def from_rows(a):
        return jnp.transpose(a, (1, 2, 0))

    cond = jnp.concatenate([c.reshape(2, COND_LANES), conv_w.reshape(3, COND_LANES), jnp.zeros((3, COND_LANES), f32)], axis=0)
    ada_b_mine = lax.dynamic_slice(ada_b, (0, chip * ADA_COLS), (1, ADA_COLS))
    blob_in = jnp.pad(w_in[0].T, ((0, D_MODEL - W_COLS), (0, 0))).astype(bf16)
    cond_all, mod_rows, wall_in = _prologue(cond, ada_w[0], ada_b_mine, blob_in)
    cond_all = cond_all.reshape(N_DEV, 8, COND_LANES)
    c_all = cond_all[:, 0:2].reshape(N_DEV, D_MODEL)
    conv_w_full = jnp.concatenate([cond_all[2 * j, 2:5].reshape(CONV_K, 384) for j in range(N_CHIPS)], axis=1)
    mod = jnp.concatenate([mod_rows[:, 0].reshape(6, D_MODEL), jnp.zeros((2, D_MODEL), f32)], axis=0)
    w_in_f = wall_in[:, :W_COLS].reshape(PROJ_WIDTH, D_MODEL)
    b_ff1w, b_ff2w, b_outw = w_ff1[0].astype(bf16), w_ff2[0].astype(bf16), w_out[0].astype(bf16)

    def with_mine(wall, mine):
        return lax.dynamic_update_slice(wall, mine[None], (chip, 0, 0))

    e_mat = _head_expand()
    dsk_x = jnp.repeat(d_skip[0], HEAD_DIM).reshape(1, SSD_WIDTH)
    dtb, alog = _pad128(dt_bias), _pad128(a_log)
    sinks = attn_sinks[0]
    lng, lnb = _row(ln_in_g), _row(ln_in_b)
    u1, z, xr, xc, q, kv, dtr, wall_ff1 = _inproj_fwd(x2, mod, lng, lnb, w_in_f, conv_w_full, conv_b, b_ff1w)
    y, prev_all, wall_ff2 = _ssd_fwd(xc, dtr, dtb, alog, dsk_x, b_ff2w)
    o, lse, wall_out = _attn_fwd(q, kv, sinks, b_outw)
    wall_ff1, wall_ff2, wall_out = with_mine(wall_ff1, b_ff1w), with_mine(wall_ff2, b_ff2w), with_mine(wall_out, b_outw)
    yn, mix, r1 = _outproj_fwd(y, z, o, x2, mod, lng, lnb, ssd_norm_w, wall_out)

    dr1, u2, s_act, da, df, acc_mlp, db1 = _mlp_fwd_bwd(r1, tgt, mod, ln1_g, ln1_b, ln2_g, ln2_b, wall_ff1, b_ff1, wall_ff2,
                                                        b_ff2)
    ga = jnp.zeros((N_CHIPS, GA_ROWS, D_MODEL), bf16)
    ga = _wgrad_blob(ga, u2, da, "wgrad_ff1", lambda t, n: (n, t))
    ga = _wgrad_blob(ga, s_act, df, "wgrad_ff2", lambda t, n: (t // 2, 2 + t % 2))
    ga = ga.reshape(N_CHIPS, 2, GA_ROWS // 2, D_MODEL)
    dy, dz, do, dmix, acc_out, a_sib = _outproj_bwd(dr1, mix, y, z, mod, ssd_norm_w, wall_out, ga)
    gc = jnp.zeros((N_CHIPS, GC_ROWS, D_MODEL), bf16)
    gc = _wgrad_blob(gc, yn, dmix, "wgrad_out_y", lambda t, n: (t, 0))
    gc = _wgrad_blob(gc, o, dmix, "wgrad_out_o", lambda t, n: (2 + t, 0))
    gc = gc.reshape(N_CHIPS, 2, GC_ROWS // 2, D_MODEL)
    a_pair = _rs_sum_pair(place, ga, a_sib, "rs_sum_pair_a")
    dq, dkv, dsink, a_chips, c_sib = _attn_bwd(q, kv, do, lse, sinks, a_pair, gc)
    g_a = _rs_sum_chips(place, ga, a_sib, a_chips, "rs_sum_chips_a")
    c_pair = _rs_sum_pair(place, gc, c_sib, "rs_sum_pair_c")
    dxc, ddt, acc_ssd, dd_x, g_a, c_chips = _ssd_bwd(xc, dtr, dy, prev_all, dtb, alog, dsk_x, e_mat, g_a, c_pair)
    g_c = _rs_sum_chips(place, gc, c_sib, c_chips, "rs_sum_chips_c")
    dxr, acc_conv, g_c = _conv_bwd(dxc, xr, conv_w_full, conv_b, g_c)
    gb = _g_in_blocks(_wgrad(dz, u1, "wgrad_in_z"), _wgrad(dxr, u1, "wgrad_in_xbc"),
                      _wgrad(ddt.astype(bf16), u1, "wgrad_in_dt"), _wgrad(dq, u1, "wgrad_in_q"),
                      _wgrad(dkv, u1, "wgrad_in_kv")).reshape(N_CHIPS, 2, GB_ROWS // 2, D_MODEL)
    b_sib = _rs_to_sibling(gb)
    b_pair = _rs_sum_pair(place, gb, b_sib, "rs_sum_pair_b")
    grad_x, acc_in, b_chips = _inproj_bwd(dz, dxr, dq, dkv, ddt, dr1, x2, mod, lng, lnb, w_in_f, b_pair)
    g_b = _rs_share(_rs_sum_chips(place, gb, b_sib, b_chips, "rs_sum_chips_b"))

    packed = _small_pack(acc_in, acc_out, acc_mlp, db1, acc_conv, acc_ssd, dd_x, dsink, alog, e_mat)
    small_all = _allgather8(packed.reshape(8, SMALL_N // 8), "gather_small").reshape(N_DEV, SMALL_N)
    given = dict(ada_b=(ada_b, m_ada_b, v_ada_b), ln_in_g=(ln_in_g, m_ln_in_g, v_ln_in_g), ln_in_b=(ln_in_b, m_ln_in_b, v_ln_in_b),
                 conv_b=(conv_b, m_conv_b, v_conv_b), dt_bias=(dt_bias, m_dt_bias, v_dt_bias), a_log=(a_log, m_a_log, v_a_log),
                 d_skip=(d_skip, m_d_skip, v_d_skip), ssd_norm_w=(ssd_norm_w, m_ssd_norm_w, v_ssd_norm_w),
                 attn_sinks=(attn_sinks, m_attn_sinks, v_attn_sinks), ln1_g=(ln1_g, m_ln1_g, v_ln1_g),
                 ln1_b=(ln1_b, m_ln1_b, v_ln1_b), b_ff1=(b_ff1, m_b_ff1, v_b_ff1), b_ff2=(b_ff2, m_b_ff2, v_b_ff2),
                 ln2_g=(ln2_g, m_ln2_g, v_ln2_g), ln2_b=(ln2_b, m_ln2_b, v_ln2_b))
    upd = _small_update(small_all, *([_row(given[n][i]) for n in SMALL_PARAMS] for i in range(3)))
    small_res = {n: [t.reshape(given[n][0].shape) for t in upd[4 * j:4 * j + 4]] for j, n in enumerate(SMALL_PARAMS)}
    g_conv_all, loss_lanes = upd[4 * len(SMALL_PARAMS)], upd[4 * len(SMALL_PARAMS) + 1]
    loss = jnp.sum(loss_lanes)

    dmod_mine = lax.dynamic_slice(small_all[:, :6 * D_MODEL], (0, chip * ADA_COLS), (N_DEV, ADA_COLS))
    big = {"ada_w": [t[None] for t in _ada_bwd(c_all, dmod_mine, ada_w[0], m_ada_w[0], v_ada_w[0])]}

    g_conv = lax.dynamic_slice(g_conv_all.reshape(CONV_K, CONV_DIM), (0, chip * 384), (CONV_K, 384))
    big["conv_w"] = [t[None] for t in (g_conv, *_adamw(conv_w[0], g_conv, m_conv_w[0], v_conv_w[0], "adamw_conv_w"))]

    g_rows = g_b[:W_COLS].reshape(W_COLS, 1, D_MODEL)
    big["w_in"] = [from_rows(t) for t in (g_rows, *_adamw_rows(as_rows(w_in), g_rows, as_rows(m_w_in), as_rows(v_w_in)))]
    for name, g, (w, m, v) in (("w_out", g_c, (w_out, m_w_out, v_w_out)),
                               ("w_ff1", g_a[:D_MODEL], (w_ff1, m_w_ff1, v_w_ff1)),
                               ("w_ff2", g_a[D_MODEL:GA_ROWS], (w_ff2, m_w_ff2, v_w_ff2))):
        big[name] = [t[None] for t in (g, *_adamw(w[0], g, m[0], v[0], "adamw_" + name))]

    order = ("ln_in_g", "ln_in_b", "ada_w", "ada_b", "w_in", "conv_w", "conv_b", "dt_bias", "a_log", "d_skip", "ssd_norm_w",
             "attn_sinks", "w_out", "ln1_g", "ln1_b", "w_ff1", "b_ff1", "w_ff2", "b_ff2", "ln2_g", "ln2_b")
    res = {**small_res, **big}
    return (loss, grad_x[None], *[res[n][k] for k in range(4) for n in order])
```

```python
import functools
import math

import numpy as np
import jax
import jax.numpy as jnp
from jax import lax
from jax.experimental import pallas as pl
from jax.experimental.pallas import tpu as pltpu

f32 = jnp.float32
bf16 = jnp.bfloat16

D_MODEL = 1024
SSD_WIDTH = 1024
SSD_HEADS = 16
HEAD_DIM = 64
SSD_STATE = 128
SSD_GROUPS = 2
CHUNK = 128
CONV_K = 4
CONV_DIM = 1536
ATTN_HEADS = 16
D_FF = 4096
PROJ_WIDTH = 3856
ALPHA = 2.0 ** 0.25
LN_EPS = 1e-5
RMS_EPS = 1e-5
ATTN_SCALE = HEAD_DIM ** -0.5
NEG = -1e30

ADAM_LR = 0.001
ADAM_B1 = 0.9
ADAM_B2 = 0.999
ADAM_EPS = 1e-08
ADAM_WD = 0.01
ADAM_STEP = 10

W_Z, W_XBC, W_Q, W_KV = slice(0, 1024), slice(1024, 2560), slice(2576, 3600), slice(3600, 3856)
W_DT = slice(2560, 2688)
W_DT_ROWS = 16
GA_ROWS = 2048
GB_ROWS = 1024
GC_ROWS = 512
WG_TM = 512
STAGE_ROWS = 512
ADAMW_BLOCK_ELEMS = 1 << 18
DENSE_TM = 512
N_CHIPS = 4
N_DEV = 8
VMEM_LIMIT = 56 * 1024 * 1024
MESH = pl.DeviceIdType.MESH

ALIBI_SLOPES = tuple(2.0 ** (-8.0 / ATTN_HEADS * (i + 1)) for i in range(ATTN_HEADS))


def _cparams(*sem):
    return pltpu.CompilerParams(dimension_semantics=sem, vmem_limit_bytes=VMEM_LIMIT)


def _sigmoid(x):
    return 1.0 / (1.0 + jnp.exp(-x))


def _softplus(x):
    return jnp.maximum(x, 0.0) + jnp.log1p(jnp.exp(-jnp.abs(x)))


def _ln_stats(x):
    mu = jnp.mean(x, axis=-1, keepdims=True)
    xc = x - mu
    var = jnp.mean(xc * xc, axis=-1, keepdims=True)
    rstd = lax.rsqrt(var + LN_EPS)
    return xc * rstd, rstd


def _ln_bwd(dy, xhat, rstd, g):
    dxh = dy * g
    m1 = jnp.mean(dxh, axis=-1, keepdims=True)
    m2 = jnp.mean(dxh * xhat, axis=-1, keepdims=True)
    return rstd * (dxh - m1 - xhat * m2)


def _dot(a, b):
    return jnp.dot(a, b, preferred_element_type=f32)


def _dot_nt(a, b):
    return lax.dot_general(a, b, (((1,), (1,)), ((), ())), preferred_element_type=f32)


def _dot_tn(a, b):
    return lax.dot_general(a, b, (((0,), (0,)), ((), ())), preferred_element_type=f32)


def _dot_exact(a, b):
    return jnp.dot(a, b, preferred_element_type=f32, precision=lax.Precision.HIGHEST)


def _split3(v):
    hi = v.astype(bf16)
    r1 = v - hi.astype(f32)
    mid = r1.astype(bf16)
    lo = (r1 - mid.astype(f32)).astype(bf16)
    return hi, mid, lo


def _sel_dot(sel, v):
    hi, mid, lo = _split3(v)
    return _dot(sel, hi) + _dot(sel, mid) + _dot(sel, lo)


def _dot_sel_nt(v, sel):
    hi = v.astype(bf16)
    mid = (v - hi.astype(f32)).astype(bf16)
    return _dot_nt(hi, sel) + _dot_nt(mid, sel)


def _expand_heads(v):
    lane = lax.broadcasted_iota(jnp.int32, (v.shape[0], 128), 1)
    blocks = [jnp.where(lane < HEAD_DIM, v[:, 2 * b:2 * b + 1], v[:, 2 * b + 1:2 * b + 2]) for b in range(SSD_HEADS // 2)]
    return jnp.concatenate(blocks, axis=1)


def _full(shape):
    nd = len(shape)
    return pl.BlockSpec(shape, lambda *_: (0,) * nd)


def _resident(shape):
    nd = len(shape)
    return pl.BlockSpec(shape, lambda *_: (0,) * nd, pipeline_mode=pl.Buffered(1))


def _rows(tm, n):
    return pl.BlockSpec((tm, n), lambda i: (i, 0))


def _inproj_fwd(x, mod, ln_g, ln_b, w_in_t, conv_w, conv_b, blob):
    L = x.shape[0]
    tm = DENSE_TM
    nt = L // tm
    R = blob.shape[0]

    def body(x_ref, mod_ref, g_ref, b_ref, w_ref, cw_ref, cb_ref, blob_ref,
             u1_ref, z_ref, xr_ref, xc_ref, q_ref, kv_ref, dt_ref, wall_ref, halo, buf, send_sems, recv_sems):
        start, finish = _gather_job(blob_ref, wall_ref, send_sems, recv_sems, R)

        @pl.when(pl.program_id(0) == 0)
        def _():
            halo[...] = jnp.zeros_like(halo)
            start()

        xhat, _ = _ln_stats(x_ref[...])
        h0 = xhat * g_ref[...] + b_ref[...]
        u1 = (h0 * (1.0 + mod_ref[1:2, :]) + mod_ref[0:1, :]).astype(bf16)
        u1_ref[...] = u1
        z_ref[...] = _dot_nt(u1, w_ref[W_Z, :])
        xr = _dot_nt(u1, w_ref[W_XBC, :])
        xr_ref[...] = xr
        q_ref[...] = _dot_nt(u1, w_ref[W_Q, :]).astype(bf16)
        kv_ref[...] = _dot_nt(u1, w_ref[W_KV, :]).astype(bf16)
        dt_ref[...] = _dot_nt(u1, w_ref[W_DT, :])
        buf[0:8, :] = halo[...]
        buf[8:8 + tm, :] = xr
        pre = cb_ref[...] + cw_ref[0:1, :] * buf[5:5 + tm, :]
        for k in range(1, CONV_K):
            pre = pre + cw_ref[k:k + 1, :] * buf[5 + k:5 + k + tm, :]
        xc_ref[...] = pre * _sigmoid(pre)
        halo[...] = xr[tm - 8:tm, :]

        @pl.when(pl.program_id(0) == nt - 1)
        def _():
            finish()

    return pl.pallas_call(
        body, name="inproj_fwd", grid=(nt,),
        in_specs=[_rows(tm, D_MODEL), _full((8, D_MODEL)), _full((1, D_MODEL)), _full((1, D_MODEL)),
                  _resident((PROJ_WIDTH, D_MODEL)), _full((CONV_K, CONV_DIM)), _full((1, CONV_DIM)), _ANY_SPEC],
        out_specs=[_rows(tm, D_MODEL), _rows(tm, D_MODEL), _rows(tm, CONV_DIM), _rows(tm, CONV_DIM),
                   _rows(tm, D_MODEL), _rows(tm, 256), _rows(tm, 128), _ANY_SPEC],
        out_shape=[jax.ShapeDtypeStruct((L, D_MODEL), bf16), jax.ShapeDtypeStruct((L, D_MODEL), f32),
                   jax.ShapeDtypeStruct((L, CONV_DIM), f32), jax.ShapeDtypeStruct((L, CONV_DIM), f32),
                   jax.ShapeDtypeStruct((L, D_MODEL), bf16), jax.ShapeDtypeStruct((L, 256), bf16),
                   jax.ShapeDtypeStruct((L, 128), f32), jax.ShapeDtypeStruct((N_CHIPS, R, D_MODEL), bf16)],
        scratch_shapes=[pltpu.VMEM((8, CONV_DIM), f32), pltpu.VMEM((tm + 8, CONV_DIM), f32),
                        pltpu.SemaphoreType.DMA((6,)), pltpu.SemaphoreType.DMA((6,))],
        compiler_params=_cparams("arbitrary"),
    )(x, mod, ln_g, ln_b, w_in_t, conv_w, conv_b, blob)


def _head_expand():
    e = np.zeros((128, SSD_WIDTH), np.float32)
    for h in range(SSD_HEADS):
        e[h, h * HEAD_DIM:(h + 1) * HEAD_DIM] = 1.0
    return jnp.asarray(e, dtype=bf16)


def _ssd_chunk_common(dt_raw, dtb, a_row):
    T = CHUNK
    lane = lax.broadcasted_iota(jnp.int32, (T, 128), 1)
    dt = jnp.where(lane < SSD_HEADS, _softplus(dt_raw + dtb), 0.0)
    a = dt * a_row
    r = lax.broadcasted_iota(jnp.int32, (T, T), 0)
    c = lax.broadcasted_iota(jnp.int32, (T, T), 1)
    tril = (c <= r).astype(bf16)
    cum = _sel_dot(tril, a)
    dtx = _expand_heads(dt)
    cumx = _expand_heads(cum)
    return dt, a, cum, dtx, cumx, r, c


def _ssd_fwd(xc, dt_raw, dt_bias, a_log, d_skip_x, blob):
    L = xc.shape[0]
    nc = L // CHUNK
    T = CHUNK
    R = blob.shape[0]

    def body(xc_ref, dt_ref, dtb_ref, al_ref, dsk_ref, blob_ref, y_ref, prev_ref, wall_ref, st, send_sems, recv_sems):
        start, finish = _gather_job(blob_ref, wall_ref, send_sems, recv_sems, R)

        @pl.when(pl.program_id(0) == 0)
        def _():
            st[...] = jnp.zeros_like(st)
            start()

        @pl.when(pl.program_id(0) == nc - 1)
        def _():
            finish()

        a_row = -jnp.exp(al_ref[...])
        lane1 = lax.broadcasted_iota(jnp.int32, (1, 128), 1)
        a_row = jnp.where(lane1 < SSD_HEADS, a_row, 0.0)
        dt, a, cum, dtx, cumx, r, c = _ssd_chunk_common(dt_ref[...], dtb_ref[...], a_row)
        cum_t = cum.T
        ex = jnp.exp(cumx)
        last = cumx[T - 1:T, :]
        wx = jnp.exp(last - cumx)
        cdx = jnp.exp(last)
        xs = xc_ref[:, 0:SSD_WIDTH]
        X = xs * dtx
        Xb = X.astype(bf16)
        Xd = (X * wx).astype(bf16)
        prev = st[...]
        prev_ref[0] = prev
        prevb = prev.astype(bf16)
        tri = c <= r
        lane = lax.broadcasted_iota(jnp.int32, (T, 128), 1)
        y_blocks = []
        new_states = []
        for g in range(SSD_GROUPS):
            Bg = xc_ref[:, 1024 + 128 * g:1152 + 128 * g].astype(bf16)
            Cg = xc_ref[:, 1280 + 128 * g:1408 + 128 * g].astype(bf16)
            G = _dot_nt(Cg, Bg)
            yoff = _dot(Cg, prevb[:, 512 * g:512 * (g + 1)])
            new_states.append(_dot_tn(Bg, Xd[:, 512 * g:512 * (g + 1)]))
            for j in range(4):
                blk = 4 * g + j
                Xblk = Xb[:, 128 * blk:128 * (blk + 1)]
                ys = []
                for half in range(2):
                    h = 2 * blk + half
                    seg = jnp.minimum(cum[:, h:h + 1] - cum_t[h:h + 1, :], 0.0)
                    M = jnp.where(tri, G * jnp.exp(seg), 0.0).astype(bf16)
                    ys.append(_dot(M, Xblk))
                yd = jnp.where(lane < HEAD_DIM, ys[0], ys[1])
                sl = slice(128 * blk, 128 * (blk + 1))
                y_blocks.append(yd + ex[:, sl] * yoff[:, 128 * j:128 * (j + 1)] + dsk_ref[:, sl] * xs[:, sl])
        y_ref[...] = jnp.concatenate(y_blocks, axis=1)
        st[...] = prev * cdx + jnp.concatenate(new_states, axis=1)

    return pl.pallas_call(
        body, name="ssd_fwd", grid=(nc,),
        in_specs=[_rows(T, CONV_DIM), _rows(T, 128), _full((1, 128)), _full((1, 128)), _full((1, SSD_WIDTH)), _ANY_SPEC],
        out_specs=[_rows(T, SSD_WIDTH), pl.BlockSpec((1, SSD_STATE, SSD_WIDTH), lambda i: (i, 0, 0)), _ANY_SPEC],
        out_shape=[jax.ShapeDtypeStruct((L, SSD_WIDTH), f32), jax.ShapeDtypeStruct((nc, SSD_STATE, SSD_WIDTH), f32),
                   jax.ShapeDtypeStruct((N_CHIPS, R, D_MODEL), bf16)],
        scratch_shapes=[pltpu.VMEM((SSD_STATE, SSD_WIDTH), f32)] + _sems(6),
        compiler_params=_cparams("arbitrary"),
    )(xc, dt_raw, dt_bias, a_log, d_skip_x, blob)


def _kv_halves(kv_prev, kv_cur, first):
    kv = jnp.concatenate([jnp.where(first, 0.0, kv_prev.astype(f32)), kv_cur.astype(f32)], axis=0)
    lane = lax.broadcasted_iota(jnp.int32, (2 * CHUNK, 128), 1)
    lo = lane < HEAD_DIM
    out = []
    for g in range(2):
        per_half = []
        for half in range(2):
            both = []
            for t in (kv[:, 0:128], kv[:, 128:256]):
                src = t if g == half else pltpu.roll(t, HEAD_DIM, 1)
                both.append(jnp.where(lo if half == 0 else ~lo, src, 0.0).astype(bf16))
            per_half.append(tuple(both))
        out.append(per_half)
    return out


def _attn_masks(first):
    r = lax.broadcasted_iota(jnp.int32, (CHUNK, 2 * CHUNK), 0)
    c = lax.broadcasted_iota(jnp.int32, (CHUNK, 2 * CHUNK), 1)
    dist = r + CHUNK - c
    valid = (dist >= 0) & (dist < CHUNK) & ((c >= CHUNK) | jnp.logical_not(first))
    return dist.astype(f32), valid


def _head_stack(g, half, sink_ref):
    blks = [4 * g + i for i in range(4)]
    heads = [2 * b + half for b in blks]
    slope = jnp.concatenate([jnp.full((CHUNK, 1), ALIBI_SLOPES[h], f32) for h in heads], axis=0)
    sink = jnp.concatenate([jnp.full((CHUNK, 1), sink_ref[h], f32) for h in heads], axis=0)
    return blks, heads, slope, sink


def _attn_fwd(q, kv, sinks, blob):
    L = q.shape[0]
    nb = L // CHUNK
    T = CHUNK
    R = blob.shape[0]

    def body(sink_ref, q_ref, kvp_ref, kvc_ref, blob_ref, o_ref, lse_ref, wall_ref, send_sems, recv_sems):
        first = pl.program_id(0) == 0
        start, finish = _gather_job(blob_ref, wall_ref, send_sems, recv_sems, R)

        @pl.when(first)
        def _():
            start()

        @pl.when(pl.program_id(0) == nb - 1)
        def _():
            finish()

        ext = _kv_halves(kvp_ref[...], kvc_ref[...], first)
        dist, valid = _attn_masks(first)
        lane = lax.broadcasted_iota(jnp.int32, (T, 128), 1)
        lse = jnp.zeros((T, 128), f32)
        o_blocks = []
        for blk in range(8):
            qb = q_ref[:, 128 * blk:128 * (blk + 1)]
            acc = None
            for half in range(2):
                h = 2 * blk + half
                k_ext, v_ext = ext[h // 8][half]
                s = _dot_nt(qb, k_ext) * ATTN_SCALE - ALIBI_SLOPES[h] * dist
                s = jnp.where(valid, s, NEG)
                sink = sink_ref[h]
                m = jnp.maximum(jnp.max(s, axis=-1, keepdims=True), sink)
                p = jnp.exp(s - m)
                den = jnp.sum(p, axis=-1, keepdims=True) + jnp.exp(sink - m)
                pn = (p * (1.0 / den)).astype(bf16)
                oh = _dot(pn, v_ext)
                acc = oh if acc is None else acc + oh
                lse = jnp.where(lane == h, m + jnp.log(den), lse)
            o_blocks.append(acc.astype(bf16))
        o_ref[...] = jnp.concatenate(o_blocks, axis=1)
        lse_ref[...] = lse

    return pl.pallas_call(
        body, name="attn_fwd", grid=(nb,),
        in_specs=[pl.BlockSpec(memory_space=pltpu.SMEM), _rows(T, D_MODEL),
                  pl.BlockSpec((T, 256), lambda i: (jnp.maximum(i - 1, 0), 0)), _rows(T, 256), _ANY_SPEC],
        out_specs=[_rows(T, D_MODEL), _rows(T, 128), _ANY_SPEC],
        out_shape=[jax.ShapeDtypeStruct((L, D_MODEL), bf16), jax.ShapeDtypeStruct((L, 128), f32),
                   jax.ShapeDtypeStruct((N_CHIPS, R, D_MODEL), bf16)],
        scratch_shapes=_sems(6),
        compiler_params=_cparams("arbitrary"),
    )(sinks, q, kv, kv, blob)


def _gated_norm(y, z, w):
    sz = _sigmoid(z)
    hg = y * (z * sz)
    ns, rss = [], []
    for g in range(SSD_GROUPS):
        hs = hg[:, 512 * g:512 * (g + 1)]
        rs = lax.rsqrt(jnp.mean(hs * hs, axis=-1, keepdims=True) + RMS_EPS)
        ns.append(hs * rs)
        rss.append(rs)
    n = jnp.concatenate(ns, axis=1)
    return n * w, n, rss, sz


def _outproj_fwd(y, z, o, x, mod, ln_g, ln_b, norm_w, w_out):
    L = x.shape[0]
    tm = DENSE_TM

    def body(y_ref, z_ref, o_ref, x_ref, mod_ref, g_ref, b_ref, nw_ref, w_ref, yn_ref, mix_ref, r1_ref):
        yn, _, _, _ = _gated_norm(y_ref[...], z_ref[...], nw_ref[...])
        ynb = yn.astype(bf16)
        yn_ref[...] = ynb
        mix = (_dot(ynb[:, 0:512], w_ref[0]) + _dot(ynb[:, 512:1024], w_ref[1])
               + _dot(o_ref[:, 0:512], w_ref[2]) + _dot(o_ref[:, 512:1024], w_ref[3]))
        mix_ref[...] = mix
        xhat, _ = _ln_stats(x_ref[...])
        h0 = xhat * g_ref[...] + b_ref[...]
        r1_ref[...] = ALPHA * h0 + (1.0 + mod_ref[2:3, :]) * mix

    v = _full((1, D_MODEL))
    return pl.pallas_call(
        body, name="outproj_fwd", grid=(L // tm,),
        in_specs=[_rows(tm, D_MODEL), _rows(tm, D_MODEL), _rows(tm, D_MODEL), _rows(tm, D_MODEL),
                  _full((8, D_MODEL)), v, v, v, _resident((N_CHIPS, 512, D_MODEL))],
        out_specs=[_rows(tm, D_MODEL)] * 3,
        out_shape=[jax.ShapeDtypeStruct((L, D_MODEL), bf16), jax.ShapeDtypeStruct((L, D_MODEL), f32),
                   jax.ShapeDtypeStruct((L, D_MODEL), f32)],
        compiler_params=_cparams("parallel"),
    )(y, z, o, x, mod, ln_g, ln_b, norm_w, w_out)


A_LN2G, A_LN2B, A_G2, A_B2, A_SC2, A_SH2, A_LN1G, A_LN1B, A_LOSS = range(9)


def _mlp_fwd_bwd(r1, target, mod, ln1_g, ln1_b, ln2_g, ln2_b, w1, b1, w2, b2):
    L = r1.shape[0]
    tm = 256
    nj = D_FF // 1024

    def body(r1_ref, t_ref, mod_ref, g1_ref, bb1_ref, g2_ref, bb2_ref, w1_ref, b1_ref, w2_ref, b2_ref,
             dr1_ref, u2_ref, s_ref, da_ref, df_ref, acc_ref, db1_ref, hr):
        @pl.when(pl.program_id(0) == 0)
        def _():
            acc_ref[...] = jnp.zeros_like(acc_ref)
            db1_ref[...] = jnp.zeros_like(db1_ref)

        sc2, sh2, gate2 = mod_ref[4:5, :], mod_ref[3:4, :], mod_ref[5:6, :]
        xhat1, rstd1 = _ln_stats(r1_ref[...])
        h1 = xhat1 * g1_ref[...] + bb1_ref[...]
        u2f = h1 * (1.0 + sc2) + sh2
        u2 = u2f.astype(bf16)
        u2_ref[...] = u2
        f = jnp.zeros((tm, D_MODEL), f32) + b2_ref[...]
        for j in range(nj):
            cs = slice(1024 * j, 1024 * (j + 1))
            a = _dot(u2, w1_ref[j]) + b1_ref[:, cs]
            hrj = jnp.maximum(a, 0.0)
            hr[:, cs] = hrj
            sj = (hrj * hrj).astype(bf16)
            s_ref[:, cs] = sj
            f = f + _dot(sj, w2_ref[j])
        r2 = ALPHA * h1 + (1.0 + gate2) * f
        xhat2, rstd2 = _ln_stats(r2)
        h2 = xhat2 * g2_ref[...] + bb2_ref[...]
        diff = h2 - t_ref[...]
        dh2 = diff * (1.0 / D_MODEL)

        def add(row, val):
            acc_ref[row:row + 1, :] += jnp.sum(val, axis=0, keepdims=True)

        add(A_LOSS, diff * diff * (0.5 / D_MODEL))
        add(A_LN2G, dh2 * xhat2)
        add(A_LN2B, dh2)
        dr2 = _ln_bwd(dh2, xhat2, rstd2, g2_ref[...])
        add(A_G2, dr2 * f)
        df = dr2 * (1.0 + gate2)
        add(A_B2, df)
        dfb = df.astype(bf16)
        df_ref[...] = dfb
        du2 = jnp.zeros((tm, D_MODEL), f32)
        for j in range(nj):
            cs = slice(1024 * j, 1024 * (j + 1))
            ds = _dot_nt(dfb, w2_ref[j])
            daj = ds * (2.0 * hr[:, cs])
            db1_ref[:, cs] += jnp.sum(daj, axis=0, keepdims=True)
            dajb = daj.astype(bf16)
            da_ref[:, cs] = dajb
            du2 = du2 + _dot_nt(dajb, w1_ref[j])
        add(A_SC2, du2 * h1)
        add(A_SH2, du2)
        dh1 = ALPHA * dr2 + du2 * (1.0 + sc2)
        add(A_LN1G, dh1 * xhat1)
        add(A_LN1B, dh1)
        dr1_ref[...] = _ln_bwd(dh1, xhat1, rstd1, g1_ref[...])

    v = _full((1, D_MODEL))
    return pl.pallas_call(
        body, name="mlp_fwd_bwd", grid=(L // tm,),
        in_specs=[_rows(tm, D_MODEL), _rows(tm, D_MODEL), _full((8, D_MODEL)), v, v, v, v,
                  _resident((N_CHIPS, D_MODEL, D_MODEL)), _full((1, D_FF)), _resident((N_CHIPS, D_MODEL, D_MODEL)), v],
        out_specs=[_rows(tm, D_MODEL), _rows(tm, D_MODEL), _rows(tm, D_FF), _rows(tm, D_FF), _rows(tm, D_MODEL),
                   _full((16, D_MODEL)), _full((1, D_FF))],
        out_shape=[jax.ShapeDtypeStruct((L, D_MODEL), f32), jax.ShapeDtypeStruct((L, D_MODEL), bf16),
                   jax.ShapeDtypeStruct((L, D_FF), bf16), jax.ShapeDtypeStruct((L, D_FF), bf16),
                   jax.ShapeDtypeStruct((L, D_MODEL), bf16), jax.ShapeDtypeStruct((16, D_MODEL), f32),
                   jax.ShapeDtypeStruct((1, D_FF), f32)],
        scratch_shapes=[pltpu.VMEM((tm, D_FF), f32)],
        compiler_params=_cparams("arbitrary"),
    )(r1, target, mod, ln1_g, ln1_b, ln2_g, ln2_b, w1, b1, w2, b2)


def _wgrad(a, b, name):
    L, M = a.shape
    N = b.shape[1]
    tm = min(M, 512)
    tn = next(t for t in (1024, 768, 512, 256, 128) if N % t == 0)

    def body(a_ref, b_ref, o_ref):
        o_ref[...] = _dot_tn(a_ref[...], b_ref[...]).astype(bf16)

    return pl.pallas_call(
        body, name=name, grid=(M // tm, N // tn),
        in_specs=[pl.BlockSpec((L, tm), lambda i, j: (0, i)), pl.BlockSpec((L, tn), lambda i, j: (0, j))],
        out_specs=pl.BlockSpec((tm, tn), lambda i, j: (i, j)),
        out_shape=jax.ShapeDtypeStruct((M, N), bf16),
        compiler_params=_cparams("parallel", "parallel"),
    )(a, b)


def _wgrad_blob(blob, a, b, name, place_of):
    L, M = a.shape
    N = b.shape[1]

    def body(blob_ref, a_ref, b_ref, o_ref):
        o_ref[0] = _dot_tn(a_ref[...], b_ref[...]).astype(bf16)

    return pl.pallas_call(
        body, name=name, grid=(M // WG_TM, N // D_MODEL),
        in_specs=[pl.BlockSpec(memory_space=pl.ANY), pl.BlockSpec((L, WG_TM), lambda t, n: (0, t)),
                  pl.BlockSpec((L, D_MODEL), lambda t, n: (0, n))],
        out_specs=pl.BlockSpec((1, WG_TM, D_MODEL), lambda t, n: (*place_of(t, n), 0)),
        out_shape=jax.ShapeDtypeStruct(blob.shape, bf16), input_output_aliases={0: 0},
        compiler_params=_cparams("parallel", "parallel"),
    )(blob, a, b)


def _outproj_bwd(dr1, mix, y, z, mod, norm_w, w_out, gb):
    L = dr1.shape[0]
    tm = DENSE_TM
    nt = L // tm

    def body(dr1_ref, mix_ref, y_ref, z_ref, mod_ref, nw_ref, w_ref, gb_ref,
             dy_ref, dz_ref, do_ref, dmix_ref, acc_ref, sib_ref, send_sems, recv_sems):
        start, wait = _to_sibling_job(gb_ref, sib_ref, send_sems, recv_sems)

        @pl.when(pl.program_id(0) == 0)
        def _():
            acc_ref[...] = jnp.zeros_like(acc_ref)
            start()

        @pl.when(pl.program_id(0) == nt - 1)
        def _():
            wait()

        dr1 = dr1_ref[...]
        acc_ref[0:1, :] += jnp.sum(dr1 * mix_ref[...], axis=0, keepdims=True)
        dmix = (dr1 * (1.0 + mod_ref[2:3, :])).astype(bf16)
        dmix_ref[...] = dmix
        dyn = jnp.concatenate([_dot_nt(dmix, w_ref[0]), _dot_nt(dmix, w_ref[1])], axis=1)
        do_ref[...] = jnp.concatenate([_dot_nt(dmix, w_ref[2]), _dot_nt(dmix, w_ref[3])], axis=1).astype(bf16)
        yv, zv = y_ref[...], z_ref[...]
        _, n, rss, sz = _gated_norm(yv, zv, nw_ref[...])
        acc_ref[1:2, :] += jnp.sum(dyn * n, axis=0, keepdims=True)
        dn = dyn * nw_ref[...]
        parts = []
        for g in range(SSD_GROUPS):
            sl = slice(512 * g, 512 * (g + 1))
            dng, ng = dn[:, sl], n[:, sl]
            parts.append(rss[g] * (dng - ng * jnp.mean(dng * ng, axis=-1, keepdims=True)))
        dhg = jnp.concatenate(parts, axis=1)
        dy_ref[...] = dhg * (zv * sz)
        dz_ref[...] = (dhg * yv * (sz * (1.0 + zv * (1.0 - sz)))).astype(bf16)

    return pl.pallas_call(
        body, name="outproj_bwd", grid=(nt,),
        in_specs=[_rows(tm, D_MODEL)] * 4 + [_full((8, D_MODEL)), _full((1, D_MODEL)), _resident((N_CHIPS, 512, D_MODEL)),
                  _ANY_SPEC],
        out_specs=[_rows(tm, D_MODEL)] * 4 + [_full((8, D_MODEL)), _ANY_SPEC],
        out_shape=[jax.ShapeDtypeStruct((L, D_MODEL), f32)] + [jax.ShapeDtypeStruct((L, D_MODEL), bf16)] * 3
        + [jax.ShapeDtypeStruct((8, D_MODEL), f32), jax.ShapeDtypeStruct((N_CHIPS,) + gb.shape[2:], bf16)],
        scratch_shapes=_sems(N_CHIPS),
        compiler_params=_cparams("arbitrary"),
    )(dr1, mix, y, z, mod, norm_w, w_out, gb)


def _attn_bwd(q, kv, do, lse, sinks, pb, gb2):
    L = q.shape[0]
    nb = L // CHUNK
    T = CHUNK

    def body(sink_ref, q_ref, kvp_ref, kvc_ref, do_ref, lse_ref, pb_ref, gb2_ref, dq_ref, dkv_ref, dsink_ref, chips_ref,
             sib2_ref, carry, send_sems, recv_sems, send_sems2, recv_sems2):
        n = pl.program_id(0)
        start, wait = _to_chips_job(pb_ref, chips_ref, send_sems, recv_sems)
        start2, wait2 = _to_sibling_job(gb2_ref, sib2_ref, send_sems2, recv_sems2)

        @pl.when(n == 0)
        def _():
            carry[...] = jnp.zeros_like(carry)
            dsink_ref[...] = jnp.zeros_like(dsink_ref)
            start()
            start2()

        @pl.when(n < nb)
        def _():
            first = n == 0
            ext = _kv_halves(kvp_ref[...], kvc_ref[...], first)
            dist, valid = _attn_masks(first)
            dist4, valid4 = jnp.concatenate([dist] * 4, axis=0), jnp.concatenate([valid] * 4, axis=0)
            lane1 = lax.broadcasted_iota(jnp.int32, (1, 128), 1)
            lse = lse_ref[...]
            qts = [q_ref[:, 128 * b:128 * (b + 1)].astype(f32).T.astype(bf16) for b in range(8)]
            dots = [do_ref[:, 128 * b:128 * (b + 1)].astype(f32).T.astype(bf16) for b in range(8)]
            acck = [None, None]
            accv = [None, None]
            dsink = jnp.zeros((1, 128), f32)
            dq_acc = [None] * 8
            for g in range(2):
                for half in range(2):
                    k_ext, v_ext = ext[g][half]
                    blks, heads, slope, sink = _head_stack(g, half, sink_ref)
                    qs = jnp.concatenate([q_ref[:, 128 * b:128 * (b + 1)] for b in blks], axis=0)
                    dos = jnp.concatenate([do_ref[:, 128 * b:128 * (b + 1)] for b in blks], axis=0)
                    rows = slice(HEAD_DIM * half, HEAD_DIM * (half + 1))
                    qt = jnp.concatenate([qts[b][rows, :] for b in blks], axis=1)
                    dot_ = jnp.concatenate([dots[b][rows, :] for b in blks], axis=1)
                    lse_col = jnp.concatenate([lse[:, h:h + 1] for h in heads], axis=0)
                    s = _dot_nt(qs, k_ext) * ATTN_SCALE - slope * dist4
                    p = jnp.where(valid4, jnp.exp(s - lse_col), 0.0)
                    dp = _dot_nt(dos, v_ext)
                    delta = jnp.sum(p * dp, axis=-1, keepdims=True)
                    ds = (p * (dp - delta) * ATTN_SCALE).astype(bf16)
                    sd = jnp.exp(sink - lse_col) * delta
                    dqs = _dot(ds, k_ext)
                    for i, b in enumerate(blks):
                        seg = slice(T * i, T * (i + 1))
                        dq_acc[b] = dqs[seg, :] if dq_acc[b] is None else dq_acc[b] + dqs[seg, :]
                        dsink = dsink - jnp.where(lane1 == heads[i], jnp.sum(sd[seg, :], axis=0, keepdims=True), 0.0)
                    dk = _dot(qt, ds)
                    dv = _dot(dot_, p.astype(bf16))
                    acck[g] = dk if acck[g] is None else acck[g] + dk
                    accv[g] = dv if accv[g] is None else accv[g] + dv
            dq_ref[...] = jnp.concatenate([a.astype(bf16) for a in dq_acc], axis=1)
            dsink_ref[...] += dsink
            dkv = jnp.concatenate([jnp.concatenate(acck, axis=0).T, jnp.concatenate(accv, axis=0).T], axis=1)
            dkv_ref[...] = (carry[...] + dkv[0:T, :]).astype(bf16)
            carry[...] = dkv[T:2 * T, :]

        @pl.when(n == nb)
        def _():
            dkv_ref[...] = carry[...].astype(bf16)
            wait()
            wait2()

    cur = lambda i: (jnp.minimum(i, nb - 1), 0)
    return pl.pallas_call(
        body, name="attn_bwd", grid=(nb + 1,),
        in_specs=[pl.BlockSpec(memory_space=pltpu.SMEM), pl.BlockSpec((T, D_MODEL), cur),
                  pl.BlockSpec((T, 256), lambda i: (jnp.maximum(jnp.minimum(i, nb - 1) - 1, 0), 0)),
                  pl.BlockSpec((T, 256), cur), pl.BlockSpec((T, D_MODEL), cur), pl.BlockSpec((T, 128), cur), _ANY_SPEC,
                  _ANY_SPEC],
        out_specs=[pl.BlockSpec((T, D_MODEL), cur), pl.BlockSpec((T, 256), lambda i: (jnp.maximum(i - 1, 0), 0)),
                   _full((1, 128)), _ANY_SPEC, _ANY_SPEC],
        out_shape=[jax.ShapeDtypeStruct((L, D_MODEL), bf16), jax.ShapeDtypeStruct((L, 256), bf16),
                   jax.ShapeDtypeStruct((1, 128), f32), jax.ShapeDtypeStruct((N_CHIPS - 1,) + pb.shape[1:], bf16),
                   jax.ShapeDtypeStruct((N_CHIPS,) + gb2.shape[2:], bf16)],
        scratch_shapes=[pltpu.VMEM((T, 256), f32)] + _sems(N_CHIPS - 1) + _sems(N_CHIPS),
        compiler_params=_cparams("arbitrary"),
    )(sinks, q, kv, kv, do, lse, pb, gb2)


def _ssd_bwd(xc, dt_raw, dy, prev_all, dt_bias, a_log, d_skip_x, e_mat, g, pb2):
    L = xc.shape[0]
    nc = L // CHUNK
    T = CHUNK
    RG = g.shape[0]

    def body(xc_ref, dt_ref, dy_ref, prev_ref, dtb_ref, al_ref, dsk_ref, e_ref, g_in_ref, pb2_ref,
             dxc_ref, ddt_ref, acc_ref, dd_ref, g_ref, chips2_ref, dst, dxs_s, send_sems, recv_sems, send_sems2, recv_sems2):
        start, wait = _share_job(g_ref, send_sems, recv_sems, RG)
        start2, wait2 = _to_chips_job(pb2_ref, chips2_ref, send_sems2, recv_sems2)

        @pl.when(pl.program_id(0) == 0)
        def _():
            dst[...] = jnp.zeros_like(dst)
            acc_ref[...] = jnp.zeros_like(acc_ref)
            dd_ref[...] = jnp.zeros_like(dd_ref)
            start()
            start2()

        @pl.when(pl.program_id(0) == nc - 1)
        def _():
            wait()
            wait2()

        lane1 = lax.broadcasted_iota(jnp.int32, (1, 128), 1)
        a_row = jnp.where(lane1 < SSD_HEADS, -jnp.exp(al_ref[...]), 0.0)
        e_mat_v = e_ref[...]
        dt, a, cum, dtx, cumx, r, c = _ssd_chunk_common(dt_ref[...], dtb_ref[...], a_row)
        cum_t = cum.T
        ex = jnp.exp(cumx)
        last = cumx[T - 1:T, :]
        wx = jnp.exp(last - cumx)
        cdx = jnp.exp(last)
        xs = xc_ref[:, 0:SSD_WIDTH]
        X = xs * dtx
        Xb = X.astype(bf16)
        Xdb = (X * wx).astype(bf16)
        dyv = dy_ref[...]
        prev = prev_ref[0]
        prevb = prev.astype(bf16)
        dnew = dst[...]
        dnewb = dnew.astype(bf16)
        tri = c <= r
        lane = lax.broadcasted_iota(jnp.int32, (T, 128), 1)
        sub = lax.broadcasted_iota(jnp.int32, (128, T), 0)
        lo = lane < HEAD_DIM

        def red(vals, g):
            return _dot_sel_nt(vals, e_mat_v[:, 512 * g:512 * (g + 1)])

        de = jnp.zeros((T, 128), f32)
        dw = jnp.zeros((T, 128), f32)
        ddt_x = jnp.zeros((T, 128), f32)
        dcum_col = jnp.zeros((T, 128), f32)
        dcum_row = jnp.zeros((128, T), f32)
        dprev_parts, dBs, dCs = [], [], []
        for g in range(SSD_GROUPS):
            s5 = slice(512 * g, 512 * (g + 1))
            Bg = xc_ref[:, 1024 + 128 * g:1152 + 128 * g].astype(bf16)
            Cg = xc_ref[:, 1280 + 128 * g:1408 + 128 * g].astype(bf16)
            G = _dot_nt(Cg, Bg)
            Z = _dot(Cg, prevb[:, s5])
            dyg = dyv[:, s5]
            dZb = (dyg * ex[:, s5]).astype(bf16)
            dXd = _dot(Bg, dnewb[:, s5])
            dC = _dot_nt(dZb, prevb[:, s5])
            dB = _dot_nt(Xdb[:, s5], dnewb[:, s5])
            dprev_parts.append(_dot_tn(Cg, dZb) + dnew[:, s5] * cdx[:, s5])
            de = de + red(dyg * Z, g)
            dw = dw + red(dXd * X[:, s5], g)
            dXg = dXd * wx[:, s5]
            dG = jnp.zeros((T, T), f32)
            for j in range(4):
                blk = 4 * g + j
                sl = slice(128 * blk, 128 * (blk + 1))
                Xblk = Xb[:, sl]
                dyblk = dyv[:, sl]
                dyblk_b = dyblk.astype(bf16)
                dxh = []
                for half in range(2):
                    h = 2 * blk + half
                    seg = jnp.minimum(cum[:, h:h + 1] - cum_t[h:h + 1, :], 0.0)
                    Lm = jnp.where(tri, jnp.exp(seg), 0.0)
                    M = G * Lm
                    dyh = jnp.where(lo if half == 0 else ~lo, dyblk, 0.0).astype(bf16)
                    dM = _dot_nt(dyh, Xblk)
                    dG = dG + dM * Lm
                    Q = dM * M
                    dcum_col = dcum_col + jnp.where(lane == h, jnp.sum(Q, axis=1, keepdims=True), 0.0)
                    dcum_row = dcum_row + jnp.where(sub == h, jnp.sum(Q, axis=0, keepdims=True), 0.0)
                    dxh.append(_dot_tn(M.astype(bf16), dyblk_b))
                dXblk = dXg[:, 128 * j:128 * (j + 1)] + jnp.where(lo, dxh[0], dxh[1])
                xsb = xs[:, sl]
                dxs_s[:, sl] = dXblk * dtx[:, sl] + dsk_ref[:, sl] * dyblk
                ddt_x = ddt_x + _dot_sel_nt(dXblk * xsb, e_mat_v[:, sl])
                dd_ref[:, sl] += jnp.sum(dyblk * xsb, axis=0, keepdims=True)
            dGb = dG.astype(bf16)
            dCs.append(dC + _dot(dGb, Bg))
            dBs.append(dB + _dot_tn(dGb, Cg))
        e16 = jnp.exp(cum)
        cum_last = cum[T - 1:T, :]
        w16 = jnp.exp(cum_last - cum)
        dcd = jnp.sum(dnew * prev, axis=0, keepdims=True)
        dcd16 = red(dcd[:, 0:512], 0) + red(dcd[:, 512:1024], 1)
        dww = dw * w16
        extra = jnp.sum(dww, axis=0, keepdims=True) + dcd16 * jnp.exp(cum_last)
        rowi = lax.broadcasted_iota(jnp.int32, (T, 128), 0)
        dcum = dcum_col - dcum_row.T + de * e16 - dww + jnp.where(rowi == T - 1, extra, 0.0)
        da = _sel_dot((c >= r).astype(bf16), dcum)
        ddt = ddt_x + da * a_row
        acc_ref[0:1, :] += jnp.sum(da * dt, axis=0, keepdims=True)
        ddt_raw = jnp.where(lane < SSD_HEADS, ddt * _sigmoid(dt_ref[...] + dtb_ref[...]), 0.0)
        ddt_ref[...] = ddt_raw
        acc_ref[1:2, :] += jnp.sum(ddt_raw, axis=0, keepdims=True)
        dxc_ref[:, 0:SSD_WIDTH] = dxs_s[...]
        dxc_ref[:, 1024:1280] = jnp.concatenate(dBs, axis=1)
        dxc_ref[:, 1280:1536] = jnp.concatenate(dCs, axis=1)
        dst[...] = jnp.concatenate(dprev_parts, axis=1)

    rev = lambda i: (nc - 1 - i, 0)
    return pl.pallas_call(
        body, name="ssd_bwd", grid=(nc,),
        in_specs=[pl.BlockSpec((T, CONV_DIM), rev), pl.BlockSpec((T, 128), rev), pl.BlockSpec((T, SSD_WIDTH), rev),
                  pl.BlockSpec((1, SSD_STATE, SSD_WIDTH), lambda i: (nc - 1 - i, 0, 0)),
                  _full((1, 128)), _full((1, 128)), _full((1, SSD_WIDTH)), _full((128, SSD_WIDTH)), _ANY_SPEC, _ANY_SPEC],
        out_specs=[pl.BlockSpec((T, CONV_DIM), rev), pl.BlockSpec((T, 128), rev), _full((8, 128)),
                   _full((1, SSD_WIDTH)), _ANY_SPEC, _ANY_SPEC],
        out_shape=[jax.ShapeDtypeStruct((L, CONV_DIM), f32), jax.ShapeDtypeStruct((L, 128), f32),
                   jax.ShapeDtypeStruct((8, 128), f32), jax.ShapeDtypeStruct((1, SSD_WIDTH), f32),
                   jax.ShapeDtypeStruct(g.shape, f32), jax.ShapeDtypeStruct((N_CHIPS - 1,) + pb2.shape[1:], bf16)],
        input_output_aliases={8: 4},
        scratch_shapes=[pltpu.VMEM((SSD_STATE, SSD_WIDTH), f32), pltpu.VMEM((T, SSD_WIDTH), f32)] + _sems(1)
        + _sems(N_CHIPS - 1),
        compiler_params=_cparams("arbitrary"),
    )(xc, dt_raw, dy, prev_all, dt_bias, a_log, d_skip_x, e_mat, g, pb2)


def _conv_bwd(dxc, xr, conv_w, conv_b, g):
    L = dxc.shape[0]
    tm = 256
    nt = L // tm
    RG = g.shape[0]

    def body(dxc_ref, xr_ref, xh_ref, cw_ref, cb_ref, g_in_ref, dxr_ref, acc_ref, g_ref, carry, buf, ext, send_sems, recv_sems):
        i = pl.program_id(0)
        start, wait = _share_job(g_ref, send_sems, recv_sems, RG)

        @pl.when(i == 0)
        def _():
            carry[...] = jnp.zeros_like(carry)
            acc_ref[...] = jnp.zeros_like(acc_ref)
            ext[tm + 16:tm + CHUNK, :] = jnp.zeros((CHUNK - 16, CONV_DIM), bf16)
            start()

        @pl.when(i == nt - 1)
        def _():
            wait()

        buf[0:8, :] = jnp.where(i == nt - 1, 0.0, xh_ref[...])
        u = xr_ref[...]
        buf[8:8 + tm, :] = u
        pre = cb_ref[...] + cw_ref[CONV_K - 1:CONV_K, :] * u
        for k in range(CONV_K - 1):
            pre = pre + cw_ref[k:k + 1, :] * buf[5 + k:5 + k + tm, :]
        sg = _sigmoid(pre)
        dpre = dxc_ref[...] * (sg * (1.0 + pre * (1.0 - sg)))
        acc_ref[4:5, :] += jnp.sum(dpre, axis=0, keepdims=True)
        dpb = dpre.astype(bf16)
        ext[0:tm, :] = dpb
        ext[tm:tm + 16, :] = carry[...]
        acc_ref[CONV_K - 1:CONV_K, :] += jnp.sum(u * dpre, axis=0, keepdims=True)
        du = cw_ref[CONV_K - 1:CONV_K, :] * dpre
        r = lax.broadcasted_iota(jnp.int32, (CHUNK, 2 * CHUNK), 0)
        c = lax.broadcasted_iota(jnp.int32, (CHUNK, 2 * CHUNK), 1)
        for j in range(1, CONV_K):
            move = (c == r + j).astype(bf16)
            up = jnp.concatenate([_dot(move, ext[CHUNK * b:CHUNK * (b + 2), :]) for b in range(tm // CHUNK)], axis=0)
            k = CONV_K - 1 - j
            du = du + cw_ref[k:k + 1, :] * up
            acc_ref[k:k + 1, :] += jnp.sum(u * up, axis=0, keepdims=True)
        dxr_ref[...] = du.astype(bf16)
        carry[...] = dpb[0:16, :]

    rev = lambda i: (nt - 1 - i, 0)
    return pl.pallas_call(
        body, name="conv_bwd", grid=(nt,),
        in_specs=[pl.BlockSpec((tm, CONV_DIM), rev), pl.BlockSpec((tm, CONV_DIM), rev),
                  pl.BlockSpec((8, CONV_DIM), lambda i: (jnp.maximum((nt - 1 - i) * (tm // 8) - 1, 0), 0)),
                  _full((CONV_K, CONV_DIM)), _full((1, CONV_DIM)), _ANY_SPEC],
        out_specs=[pl.BlockSpec((tm, CONV_DIM), rev), _full((8, CONV_DIM)), _ANY_SPEC],
        out_shape=[jax.ShapeDtypeStruct((L, CONV_DIM), bf16), jax.ShapeDtypeStruct((8, CONV_DIM), f32),
                   jax.ShapeDtypeStruct(g.shape, f32)],
        input_output_aliases={5: 2},
        scratch_shapes=[pltpu.VMEM((16, CONV_DIM), bf16), pltpu.VMEM((tm + 8, CONV_DIM), f32),
                        pltpu.VMEM((tm + CHUNK, CONV_DIM), bf16)] + _sems(1),
        compiler_params=_cparams("arbitrary"),
    )(dxc, xr, xr, conv_w, conv_b, g)


def _inproj_bwd(dz, dxr, dq, dkv, ddt, dr1, x, mod, ln_g, ln_b, w_in, pb):
    L = x.shape[0]
    tm = DENSE_TM
    nt = L // tm

    def body(dz_ref, dxr_ref, dq_ref, dkv_ref, ddt_ref, dr1_ref, x_ref, mod_ref, g_ref, b_ref, w_ref, pb_ref,
             dx_ref, acc_ref, chips_ref, send_sems, recv_sems):
        start, wait = _to_chips_job(pb_ref, chips_ref, send_sems, recv_sems)

        @pl.when(pl.program_id(0) == 0)
        def _():
            acc_ref[...] = jnp.zeros_like(acc_ref)
            start()

        @pl.when(pl.program_id(0) == nt - 1)
        def _():
            wait()

        du1 = (_dot(dz_ref[...], w_ref[W_Z, :]) + _dot(dxr_ref[...], w_ref[W_XBC, :])
               + _dot(dq_ref[...], w_ref[W_Q, :]) + _dot(dkv_ref[...], w_ref[W_KV, :])
               + _dot(ddt_ref[...].astype(bf16), w_ref[W_DT, :]))
        xhat, rstd = _ln_stats(x_ref[...])
        h0 = xhat * g_ref[...] + b_ref[...]
        acc_ref[0:1, :] += jnp.sum(du1 * h0, axis=0, keepdims=True)
        acc_ref[1:2, :] += jnp.sum(du1, axis=0, keepdims=True)
        dh0 = du1 * (1.0 + mod_ref[1:2, :]) + ALPHA * dr1_ref[...]
        acc_ref[2:3, :] += jnp.sum(dh0 * xhat, axis=0, keepdims=True)
        acc_ref[3:4, :] += jnp.sum(dh0, axis=0, keepdims=True)
        dx_ref[...] = _ln_bwd(dh0, xhat, rstd, g_ref[...])

    v = _full((1, D_MODEL))
    return pl.pallas_call(
        body, name="inproj_bwd", grid=(nt,),
        in_specs=[_rows(tm, D_MODEL), _rows(tm, CONV_DIM), _rows(tm, D_MODEL), _rows(tm, 256), _rows(tm, 128),
                  _rows(tm, D_MODEL), _rows(tm, D_MODEL), _full((8, D_MODEL)), v, v, _resident((PROJ_WIDTH, D_MODEL)),
                  _ANY_SPEC],
        out_specs=[_rows(tm, D_MODEL), _full((8, D_MODEL)), _ANY_SPEC],
        out_shape=[jax.ShapeDtypeStruct((L, D_MODEL), f32), jax.ShapeDtypeStruct((8, D_MODEL), f32),
                   jax.ShapeDtypeStruct((N_CHIPS - 1,) + pb.shape[1:], bf16)],
        scratch_shapes=_sems(N_CHIPS - 1),
        compiler_params=_cparams("arbitrary"),
    )(dz, dxr, dq, dkv, ddt, dr1, x, mod, ln_g, ln_b, w_in, pb)


def _adamw_math(w, g, m, v):
    m = ADAM_B1 * m + (1.0 - ADAM_B1) * g
    v = ADAM_B2 * v + (1.0 - ADAM_B2) * (g * g)
    m_hat = m / (1.0 - ADAM_B1 ** ADAM_STEP)
    v_hat = v / (1.0 - ADAM_B2 ** ADAM_STEP)
    delta = -ADAM_LR * (m_hat / (jnp.sqrt(v_hat) + ADAM_EPS) + ADAM_WD * w)
    return delta, m, v


def _adamw(w, g, m, v, name):
    R, C = w.shape

    def body(w_ref, g_ref, m_ref, v_ref, d_ref, m2_ref, v2_ref):
        d_ref[...], m2_ref[...], v2_ref[...] = _adamw_math(w_ref[...], g_ref[...], m_ref[...], v_ref[...])

    cap = max(8, ADAMW_BLOCK_ELEMS // C)
    tr = R if R <= cap else next(t for t in range(cap - cap % 8, 7, -8) if R % t == 0)
    spec = pl.BlockSpec((tr, C), lambda i: (i, 0))
    return pl.pallas_call(
        body, name=name, grid=(R // tr,), in_specs=[spec] * 4, out_specs=[spec] * 3,
        out_shape=[jax.ShapeDtypeStruct((R, C), f32)] * 3, compiler_params=_cparams("parallel"),
    )(w, g, m, v)


def _adamw_rows(w, g, m, v):
    R, _, C = w.shape
    tr = R // 4

    def body(w_ref, g_ref, m_ref, v_ref, d_ref, m2_ref, v2_ref):
        d_ref[...], m2_ref[...], v2_ref[...] = _adamw_math(w_ref[...], g_ref[...], m_ref[...], v_ref[...])

    spec = pl.BlockSpec((tr, 1, C), lambda i: (i, 0, 0))
    return pl.pallas_call(
        body, name="adamw_w_in", grid=(R // tr,), in_specs=[spec] * 4, out_specs=[spec] * 3,
        out_shape=[jax.ShapeDtypeStruct((R, 1, C), f32)] * 3, compiler_params=_cparams("parallel"),
    )(w, g, m, v)


ADA_COLS = 6 * D_MODEL // N_CHIPS
ADA_TN = 512


COND_LANES = 512


def _prologue(cond, ada_w, ada_b, blob):
    R = blob.shape[0]

    def body(cond_ref, w_ref, b_ref, blob_ref, call_ref, mod_ref, wall_ref, mod_s, stage, gs, gr, ms, mr, ws, wr, local_sem):
        x, y, c = _place()
        start_w, finish_w = _gather_job(blob_ref, wall_ref, ws, wr, R)

        def rows(ref, px, py, pc):
            return ref.at[pl.ds(pl.multiple_of((4 * px + 2 * py + pc) * 8, 8), 8), :]

        mine = pltpu.make_async_copy(cond_ref, rows(call_ref, x, y, c), local_sem)
        mine.start()
        sends = [_remote(cond_ref, rows(call_ref, x, y, c), gs, gr, m - 1, _flip(x, y, c, m)) for m in range(1, N_DEV)]
        for cp in sends:
            cp.start()
        for m in range(1, N_DEV):
            peer = _flip(x, y, c, m)
            _remote(cond_ref, rows(call_ref, *peer), gs, gr, m - 1, peer).wait_recv()
        for cp in sends:
            cp.wait_send()
        mine.wait()
        start_w()

        for k in range(R // STAGE_ROWS):
            part = pl.ds(STAGE_ROWS * k, STAGE_ROWS)
            cin = pltpu.make_async_copy(blob_ref.at[part, :], stage, local_sem)
            cin.start()
            cin.wait()
            cout = pltpu.make_async_copy(stage, wall_ref.at[2 * x + y, part, :], local_sem)
            cout.start()
            cout.wait()

        lo = jnp.concatenate([call_ref[8 * d:8 * d + 1, :] for d in range(N_DEV)], axis=0)
        hi = jnp.concatenate([call_ref[8 * d + 1:8 * d + 2, :] for d in range(N_DEV)], axis=0)
        mod_all = (_dot_exact(lo * _sigmoid(lo), w_ref[0:COND_LANES, :]) + _dot_exact(hi * _sigmoid(hi), w_ref[COND_LANES:, :])
                   + b_ref[...])
        for d in range(N_DEV):
            mod_s[8 * d:8 * d + 8, :] = jnp.broadcast_to(mod_all[d:d + 1, :], (8, ADA_COLS))

        mine = pltpu.make_async_copy(rows(mod_s, x, y, c), mod_ref.at[2 * x + y], local_sem)
        mine.start()
        sends = []
        for m in range(1, N_CHIPS):
            peer = _flip(x, y, c, 2 * m)
            sends.append(_remote(rows(mod_s, *peer), mod_ref.at[2 * x + y], ms, mr, m - 1, peer))
        for cp in sends:
            cp.start()
        for m in range(1, N_CHIPS):
            px, py, pc = _flip(x, y, c, 2 * m)
            _remote(rows(mod_s, x, y, c), mod_ref.at[2 * px + py], ms, mr, m - 1, (px, py, pc)).wait_recv()
        for cp in sends:
            cp.wait_send()
        mine.wait()
        finish_w()

    return pl.pallas_call(
        body, name="prologue",
        out_shape=[jax.ShapeDtypeStruct((8 * N_DEV, COND_LANES), f32), jax.ShapeDtypeStruct((N_CHIPS, 8, ADA_COLS), f32),
                   jax.ShapeDtypeStruct((N_CHIPS, R, D_MODEL), bf16)],
        in_specs=[_VMEM_SPEC, _VMEM_SPEC, _VMEM_SPEC, _ANY_SPEC], out_specs=[_VMEM_SPEC, _VMEM_SPEC, _ANY_SPEC],
        scratch_shapes=[pltpu.VMEM((8 * N_DEV, ADA_COLS), f32), pltpu.VMEM((STAGE_ROWS, D_MODEL), bf16)]
        + _sems(N_DEV - 1) + _sems(N_CHIPS - 1) + _sems(6) + [pltpu.SemaphoreType.DMA],
        compiler_params=pltpu.CompilerParams(vmem_limit_bytes=VMEM_LIMIT),
    )(cond, ada_w, ada_b, blob)


def _ada_bwd(c_all, dmod, w, m, v):
    def body(c_ref, d_ref, w_ref, m_ref, v_ref, g_ref, dl_ref, m2_ref, v2_ref):
        cv = c_ref[...]
        g = lax.dot_general(cv * _sigmoid(cv), d_ref[...], (((0,), (0,)), ((), ())), preferred_element_type=f32,
                            precision=lax.Precision.HIGHEST)
        g_ref[...] = g
        dl_ref[...], m2_ref[...], v2_ref[...] = _adamw_math(w_ref[...], g, m_ref[...], v_ref[...])

    wspec = pl.BlockSpec((D_MODEL, ADA_TN), lambda j: (0, j))
    return pl.pallas_call(
        body, name="ada_bwd", grid=(ADA_COLS // ADA_TN,),
        in_specs=[_full((N_DEV, D_MODEL)), pl.BlockSpec((N_DEV, ADA_TN), lambda j: (0, j)), wspec, wspec, wspec],
        out_specs=[wspec] * 4, out_shape=[jax.ShapeDtypeStruct((D_MODEL, ADA_COLS), f32)] * 4,
        compiler_params=_cparams("parallel"),
    )(c_all, dmod, w, m, v)


SMALL_SLOTS = (("ada_b", 6144), ("ln_in_g", 1024), ("ln_in_b", 1024), ("conv_b", 1536), ("dt_bias", 128), ("a_log", 128),
               ("d_skip", 128), ("ssd_norm_w", 1024), ("attn_sinks", 128), ("ln1_g", 1024), ("ln1_b", 1024),
               ("b_ff1", 4096), ("b_ff2", 1024), ("ln2_g", 1024), ("ln2_b", 1024), ("conv_w", 6144), ("loss", 1024))
SMALL_N = sum(n for _, n in SMALL_SLOTS)
SMALL_OFF = {name: sum(n for _, n in SMALL_SLOTS[:i]) for i, (name, _) in enumerate(SMALL_SLOTS)}
SMALL_PARAMS = tuple(name for name, _ in SMALL_SLOTS[:15])
assert SMALL_N % 1024 == 0


def _small_pack(acc_in, acc_out, acc_mlp, db1, acc_conv, acc_ssd, dd_x, dsink, alog, e_mat):
    def body(in_ref, out_ref, mlp_ref, db1_ref, conv_ref, ssd_ref, dd_ref, sink_ref, al_ref, e_ref, o_ref):
        def put(name, val, at=0):
            off = SMALL_OFF[name] + at
            o_ref[:, off:off + val.shape[1]] = val

        for k, row in enumerate((in_ref[1:2, :], in_ref[0:1, :], out_ref[0:1, :], mlp_ref[A_SH2:A_SH2 + 1, :],
                                 mlp_ref[A_SC2:A_SC2 + 1, :], mlp_ref[A_G2:A_G2 + 1, :])):
            put("ada_b", row, D_MODEL * k)
        put("ln_in_g", in_ref[2:3, :])
        put("ln_in_b", in_ref[3:4, :])
        put("conv_b", conv_ref[4:5, :])
        put("dt_bias", ssd_ref[1:2, :])
        put("a_log", ssd_ref[0:1, :] * (-jnp.exp(al_ref[...])))
        put("d_skip", _dot_sel_nt(jnp.broadcast_to(dd_ref[...], (8, SSD_WIDTH)), e_ref[...])[0:1, :])
        put("ssd_norm_w", out_ref[1:2, :])
        put("attn_sinks", sink_ref[...])
        put("ln1_g", mlp_ref[A_LN1G:A_LN1G + 1, :])
        put("ln1_b", mlp_ref[A_LN1B:A_LN1B + 1, :])
        put("b_ff1", db1_ref[...])
        put("b_ff2", mlp_ref[A_B2:A_B2 + 1, :])
        put("ln2_g", mlp_ref[A_LN2G:A_LN2G + 1, :])
        put("ln2_b", mlp_ref[A_LN2B:A_LN2B + 1, :])
        for k in range(CONV_K):
            put("conv_w", conv_ref[k:k + 1, :], CONV_DIM * k)
        put("loss", mlp_ref[A_LOSS:A_LOSS + 1, :])

    return pl.pallas_call(body, name="small_pack", out_shape=jax.ShapeDtypeStruct((1, SMALL_N), f32),
                          compiler_params=_cparams())(acc_in, acc_out, acc_mlp, db1, acc_conv, acc_ssd, dd_x, dsink, alog, e_mat)


def _small_update(gathered, params, moms, vels):
    k = len(SMALL_PARAMS)

    def body(g_ref, *refs):
        w_refs, m_refs, v_refs, outs = refs[:k], refs[k:2 * k], refs[2 * k:3 * k], refs[3 * k:]

        def total(name, n):
            off = SMALL_OFF[name]
            g = g_ref[0:1, off:off + n]
            for i in range(1, N_DEV):
                g = g + g_ref[i:i + 1, off:off + n]
            return g

        for j, name in enumerate(SMALL_PARAMS):
            n = w_refs[j].shape[1]
            g = total(name, max(n, 128))[:, :n]
            outs[4 * j][...] = g
            outs[4 * j + 1][...], outs[4 * j + 2][...], outs[4 * j + 3][...] = _adamw_math(
                w_refs[j][...], g, m_refs[j][...], v_refs[j][...])
        outs[4 * k][...] = total("conv_w", CONV_K * CONV_DIM)
        outs[4 * k + 1][...] = total("loss", D_MODEL)

    shapes = [jax.ShapeDtypeStruct(p.shape, f32) for p in params for _ in range(4)]
    shapes += [jax.ShapeDtypeStruct((1, CONV_K * CONV_DIM), f32), jax.ShapeDtypeStruct((1, D_MODEL), f32)]
    return pl.pallas_call(body, name="small_update", out_shape=shapes,
                          compiler_params=_cparams())(gathered, *params, *moms, *vels)


def _place():
    return lax.axis_index("x"), lax.axis_index("y"), lax.axis_index("c")


def _flip(x, y, c, m):
    return (1 - x if m & 4 else x, 1 - y if m & 2 else y, 1 - c if m & 1 else c)


_VMEM_SPEC = pl.BlockSpec(memory_space=pltpu.VMEM)
_ANY_SPEC = pl.BlockSpec(memory_space=pl.ANY)


def _allgather8(v, name):
    n = v.shape[1]

    def body(v_ref, out_ref, send_sems, recv_sems, local_sem):
        x, y, c = _place()

        def rows(px, py, pc):
            return out_ref.at[pl.ds(pl.multiple_of((4 * px + 2 * py + pc) * 8, 8), 8), :]

        def copy(m, src, dst, to):
            return pltpu.make_async_remote_copy(src_ref=src, dst_ref=dst, send_sem=send_sems.at[m - 1],
                                                recv_sem=recv_sems.at[m - 1], device_id=to, device_id_type=MESH)

        mine = pltpu.make_async_copy(v_ref, rows(x, y, c), local_sem)
        mine.start()
        sends = [copy(m, v_ref, rows(x, y, c), _flip(x, y, c, m)) for m in range(1, N_DEV)]
        for cp in sends:
            cp.start()
        for m in range(1, N_DEV):
            peer = _flip(x, y, c, m)
            copy(m, v_ref, rows(*peer), peer).wait_recv()
        for cp in sends:
            cp.wait_send()
        mine.wait()

    return pl.pallas_call(
        body, name=name, out_shape=jax.ShapeDtypeStruct((8 * N_DEV, n), f32), in_specs=[_VMEM_SPEC],
        out_specs=_VMEM_SPEC,
        scratch_shapes=[pltpu.SemaphoreType.DMA((N_DEV - 1,)), pltpu.SemaphoreType.DMA((N_DEV - 1,)),
                        pltpu.SemaphoreType.DMA],
    )(v)


def _remote(src, dst, send_sems, recv_sems, k, to):
    return pltpu.make_async_remote_copy(src_ref=src, dst_ref=dst, send_sem=send_sems.at[k], recv_sem=recv_sems.at[k],
                                        device_id=to, device_id_type=MESH)


def _gather_job(blob_ref, out_ref, send_sems, recv_sems, R):
    x, y, c = _place()
    sib = (x, y, 1 - c)
    hr = R // 2

    def half(px, py, pc):
        return out_ref.at[2 * px + py, pl.ds(pl.multiple_of(pc * hr, 16), hr), :]

    my_half = blob_ref.at[pl.ds(pl.multiple_of(c * hr, 16), hr), :]

    def first():
        return [_remote(my_half, half(x, y, c), send_sems, recv_sems, m - 1, _flip(x, y, c, 2 * m))
                for m in range(1, N_CHIPS)]

    def start():
        for cp in first():
            cp.start()

    def finish():
        passed = []
        for m in range(1, N_CHIPS):
            px, py, pc = _flip(x, y, c, 2 * m)
            _remote(my_half, half(px, py, pc), send_sems, recv_sems, m - 1, (px, py, pc)).wait_recv()
            fwd = _remote(half(px, py, pc), half(px, py, pc), send_sems, recv_sems, 2 + m, sib)
            fwd.start()
            passed.append(fwd)
        for m in range(1, N_CHIPS):
            px, py, pc = _flip(x, y, c, 2 * m)
            _remote(my_half, half(px, py, 1 - pc), send_sems, recv_sems, 2 + m, sib).wait_recv()
        for cp in first() + passed:
            cp.wait_send()

    return start, finish


def _to_sibling_job(g_ref, out_ref, send_sems, recv_sems):
    x, y, c = _place()

    def cps():
        return [_remote(g_ref.at[j, 1 - c], out_ref.at[j], send_sems, recv_sems, j, (x, y, 1 - c)) for j in range(N_CHIPS)]

    def start():
        for cp in cps():
            cp.start()

    def wait():
        for cp in cps():
            cp.wait()

    return start, wait


def _to_chips_job(p_ref, out_ref, send_sems, recv_sems):
    x, y, c = _place()

    def cps():
        out = []
        for m in range(1, N_CHIPS):
            px, py, pc = _flip(x, y, c, 2 * m)
            out.append(_remote(p_ref.at[2 * px + py], out_ref.at[m - 1], send_sems, recv_sems, m - 1, (px, py, pc)))
        return out

    def start():
        for cp in cps():
            cp.start()

    def wait():
        for cp in cps():
            cp.wait()

    return start, wait


def _share_job(g_ref, send_sems, recv_sems, R):
    x, y, c = _place()

    def rows(pc):
        return g_ref.at[pl.ds(pl.multiple_of(pc * (R // 2), 8), R // 2), :]

    def start():
        _remote(rows(c), rows(c), send_sems, recv_sems, 0, (x, y, 1 - c)).start()

    def wait():
        _remote(rows(c), rows(1 - c), send_sems, recv_sems, 0, (x, y, 1 - c)).wait_recv()
        _remote(rows(c), rows(c), send_sems, recv_sems, 0, (x, y, 1 - c)).wait_send()

    return start, wait


def _sems(n):
    return [pltpu.SemaphoreType.DMA((n,)), pltpu.SemaphoreType.DMA((n,))]


def _rs_to_sibling(gb):
    def body(g_ref, out_ref, send_sems, recv_sems):
        start, wait = _to_sibling_job(g_ref, out_ref, send_sems, recv_sems)
        start()
        wait()

    return pl.pallas_call(
        body, name="rs_to_sibling", out_shape=jax.ShapeDtypeStruct((N_CHIPS,) + gb.shape[2:], bf16),
        in_specs=[_ANY_SPEC], out_specs=_ANY_SPEC, scratch_shapes=_sems(N_CHIPS),
    )(gb)


def _rs_share(g):
    R = g.shape[0]

    def body(g_ref, out_ref, send_sems, recv_sems):
        start, wait = _share_job(out_ref, send_sems, recv_sems, R)
        start()
        wait()

    return pl.pallas_call(
        body, name="rs_share", out_shape=jax.ShapeDtypeStruct(g.shape, f32), in_specs=[_ANY_SPEC],
        out_specs=_ANY_SPEC, input_output_aliases={0: 0}, scratch_shapes=_sems(1),
    )(g)


RS_TR = 256


def _rs_sum_pair(place, gb, recv, name):
    hr = gb.shape[2]

    def body(pl_ref, g_ref, r_ref, o_ref):
        o_ref[0] = (g_ref[0, 0].astype(f32) + r_ref[0].astype(f32)).astype(bf16)

    return pl.pallas_call(
        body, name=name,
        grid_spec=pltpu.PrefetchScalarGridSpec(
            num_scalar_prefetch=1, grid=(N_CHIPS, hr // RS_TR),
            in_specs=[pl.BlockSpec((1, 1, RS_TR, D_MODEL), lambda j, i, p: (j, p[0], i, 0)),
                      pl.BlockSpec((1, RS_TR, D_MODEL), lambda j, i, p: (j, i, 0))],
            out_specs=pl.BlockSpec((1, RS_TR, D_MODEL), lambda j, i, p: (j, i, 0))),
        out_shape=jax.ShapeDtypeStruct((N_CHIPS, hr, D_MODEL), bf16),
        compiler_params=_cparams("parallel", "parallel"),
    )(place, gb, recv)


def _rs_sum_chips(place, gb, recv_sib, recv_chips, name):
    hr = gb.shape[2]
    nt = hr // RS_TR

    def body(pl_ref, g_ref, r1_ref, r2_ref, o_ref):
        acc = g_ref[0, 0].astype(f32) + r1_ref[0].astype(f32)
        for k in range(N_CHIPS - 1):
            acc = acc + r2_ref[k].astype(f32)
        o_ref[...] = acc

    return pl.pallas_call(
        body, name=name,
        grid_spec=pltpu.PrefetchScalarGridSpec(
            num_scalar_prefetch=1, grid=(nt,),
            in_specs=[pl.BlockSpec((1, 1, RS_TR, D_MODEL), lambda i, p: (p[1], p[0], i, 0)),
                      pl.BlockSpec((1, RS_TR, D_MODEL), lambda i, p: (p[1], i, 0)),
                      pl.BlockSpec((N_CHIPS - 1, RS_TR, D_MODEL), lambda i, p: (0, i, 0))],
            out_specs=pl.BlockSpec((RS_TR, D_MODEL), lambda i, p: (p[0] * nt + i, 0))),
        out_shape=jax.ShapeDtypeStruct((2 * hr, D_MODEL), f32),
        compiler_params=_cparams("parallel"),
    )(place, gb, recv_sib, recv_chips)


def _pad128(v):
    v = v.reshape(1, -1)
    return jnp.pad(v, ((0, 0), (0, 128 - v.shape[1])))


def _row(v):
    return v.reshape(1, -1)


W_COLS = PROJ_WIDTH // N_CHIPS


def _g_in_blocks(gz, gxbc, gdt, gq, gkv):
    g = jnp.concatenate([gz, gxbc, gdt[:W_DT_ROWS], gq, gkv], axis=0)
    return jnp.pad(g.reshape(N_CHIPS, W_COLS, D_MODEL), ((0, 0), (0, D_MODEL - W_COLS), (0, 0)))


def kernel(x, c, ln_in_g, ln_in_b, ada_w, ada_b, w_in, conv_w, conv_b, dt_bias, a_log, d_skip, ssd_norm_w, attn_sinks, w_out, ln1_g, ln1_b, w_ff1, b_ff1, w_ff2, b_ff2, ln2_g, ln2_b, loss_target, m_ln_in_g, m_ln_in_b, m_ada_w, m_ada_b, m_w_in, m_conv_w, m_conv_b, m_dt_bias, m_a_log, m_d_skip, m_ssd_norm_w, m_attn_sinks, m_w_out, m_ln1_g, m_ln1_b, m_w_ff1, m_b_ff1, m_w_ff2, m_b_ff2, m_ln2_g, m_ln2_b, v_ln_in_g, v_ln_in_b, v_ada_w, v_ada_b, v_w_in, v_conv_w, v_conv_b, v_dt_bias, v_a_log, v_d_skip, v_ssd_norm_w, v_attn_sinks, v_w_out, v_ln1_g, v_ln1_b, v_w_ff1, v_b_ff1, v_w_ff2, v_b_ff2, v_ln2_g, v_ln2_b):
    xi, yi, ci = _place()
    chip = 2 * xi + yi
    place = jnp.stack([ci, chip]).astype(jnp.int32)
    x2, tgt = x[0], loss_target[0]

    def as_rows(a):
        return jnp.transpose(a, (2, 0, 1))

    def from_rows(a):
        return jnp.transpose(a, (1, 2, 0))

    cond = jnp.concatenate([c.reshape(2, COND_LANES), conv_w.reshape(3, COND_LANES), jnp.zeros((3, COND_LANES), f32)], axis=0)
    ada_b_mine = lax.dynamic_slice(ada_b, (0, chip * ADA_COLS), (1, ADA_COLS))
    blob_in = jnp.pad(w_in[0].T, ((0, D_MODEL - W_COLS), (0, 0))).astype(bf16)
    cond_all, mod_rows, wall_in = _prologue(cond, ada_w[0], ada_b_mine, blob_in)
    cond_all = cond_all.reshape(N_DEV, 8, COND_LANES)
    c_all = cond_all[:, 0:2].reshape(N_DEV, D_MODEL)
    conv_w_full = jnp.concatenate([cond_all[2 * j, 2:5].reshape(CONV_K, 384) for j in range(N_CHIPS)], axis=1)
    mod = jnp.concatenate([mod_rows[:, 0].reshape(6, D_MODEL), jnp.zeros((2, D_MODEL), f32)], axis=0)
    w_in_f = wall_in[:, :W_COLS].reshape(PROJ_WIDTH, D_MODEL)
    b_ff1w, b_ff2w, b_outw = w_ff1[0].astype(bf16), w_ff2[0].astype(bf16), w_out[0].astype(bf16)

    def with_mine(wall, mine):
        return lax.dynamic_update_slice(wall, mine[None], (chip, 0, 0))

    e_mat = _head_expand()
    dsk_x = jnp.repeat(d_skip[0], HEAD_DIM).reshape(1, SSD_WIDTH)
    dtb, alog = _pad128(dt_bias), _pad128(a_log)
    sinks = attn_sinks[0]
    lng, lnb = _row(ln_in_g), _row(ln_in_b)
    u1, z, xr, xc, q, kv, dtr, wall_ff1 = _inproj_fwd(x2, mod, lng, lnb, w_in_f, conv_w_full, conv_b, b_ff1w)
    y, prev_all, wall_out = _ssd_fwd(xc, dtr, dtb, alog, dsk_x, b_outw)
    o, lse, wall_ff2 = _attn_fwd(q, kv, sinks, b_ff2w)
    wall_ff1, wall_ff2, wall_out = with_mine(wall_ff1, b_ff1w), with_mine(wall_ff2, b_ff2w), with_mine(wall_out, b_outw)
    yn, mix, r1 = _outproj_fwd(y, z, o, x2, mod, lng, lnb, ssd_norm_w, wall_out)

    dr1, u2, s_act, da, df, acc_mlp, db1 = _mlp_fwd_bwd(r1, tgt, mod, ln1_g, ln1_b, ln2_g, ln2_b, wall_ff1, b_ff1, wall_ff2,
                                                        b_ff2)
    ga = jnp.zeros((N_CHIPS, GA_ROWS, D_MODEL), bf16)
    ga = _wgrad_blob(ga, u2, da, "wgrad_ff1", lambda t, n: (n, t))
    ga = _wgrad_blob(ga, s_act, df, "wgrad_ff2", lambda t, n: (t // 2, 2 + t % 2))
    ga = ga.reshape(N_CHIPS, 2, GA_ROWS // 2, D_MODEL)
    dy, dz, do, dmix, acc_out, a_sib = _outproj_bwd(dr1, mix, y, z, mod, ssd_norm_w, wall_out, ga)
    gc = jnp.zeros((N_CHIPS, GC_ROWS, D_MODEL), bf16)
    gc = _wgrad_blob(gc, yn, dmix, "wgrad_out_y", lambda t, n: (t, 0))
    gc = _wgrad_blob(gc, o, dmix, "wgrad_out_o", lambda t, n: (2 + t, 0))
    gc = gc.reshape(N_CHIPS, 2, GC_ROWS // 2, D_MODEL)
    a_pair = _rs_sum_pair(place, ga, a_sib, "rs_sum_pair_a")
    dq, dkv, dsink, a_chips, c_sib = _attn_bwd(q, kv, do, lse, sinks, a_pair, gc)
    g_a = _rs_sum_chips(place, ga, a_sib, a_chips, "rs_sum_chips_a")
    c_pair = _rs_sum_pair(place, gc, c_sib, "rs_sum_pair_c")
    dxc, ddt, acc_ssd, dd_x, g_a, c_chips = _ssd_bwd(xc, dtr, dy, prev_all, dtb, alog, dsk_x, e_mat, g_a, c_pair)
    g_c = _rs_sum_chips(place, gc, c_sib, c_chips, "rs_sum_chips_c")
    dxr, acc_conv, g_c = _conv_bwd(dxc, xr, conv_w_full, conv_b, g_c)
    gb = _g_in_blocks(_wgrad(dz, u1, "wgrad_in_z"), _wgrad(dxr, u1, "wgrad_in_xbc"),
                      _wgrad(ddt.astype(bf16), u1, "wgrad_in_dt"), _wgrad(dq, u1, "wgrad_in_q"),
                      _wgrad(dkv, u1, "wgrad_in_kv")).reshape(N_CHIPS, 2, GB_ROWS // 2, D_MODEL)
    b_sib = _rs_to_sibling(gb)
    b_pair = _rs_sum_pair(place, gb, b_sib, "rs_sum_pair_b")
    grad_x, acc_in, b_chips = _inproj_bwd(dz, dxr, dq, dkv, ddt, dr1, x2, mod, lng, lnb, w_in_f, b_pair)
    g_b = _rs_share(_rs_sum_chips(place, gb, b_sib, b_chips, "rs_sum_chips_b"))

    packed = _small_pack(acc_in, acc_out, acc_mlp, db1, acc_conv, acc_ssd, dd_x, dsink, alog, e_mat)
    small_all = _allgather8(packed.reshape(8, SMALL_N // 8), "gather_small").reshape(N_DEV, SMALL_N)
    given = dict(ada_b=(ada_b, m_ada_b, v_ada_b), ln_in_g=(ln_in_g, m_ln_in_g, v_ln_in_g), ln_in_b=(ln_in_b, m_ln_in_b, v_ln_in_b),
                 conv_b=(conv_b, m_conv_b, v_conv_b), dt_bias=(dt_bias, m_dt_bias, v_dt_bias), a_log=(a_log, m_a_log, v_a_log),
                 d_skip=(d_skip, m_d_skip, v_d_skip), ssd_norm_w=(ssd_norm_w, m_ssd_norm_w, v_ssd_norm_w),
                 attn_sinks=(attn_sinks, m_attn_sinks, v_attn_sinks), ln1_g=(ln1_g, m_ln1_g, v_ln1_g),
                 ln1_b=(ln1_b, m_ln1_b, v_ln1_b), b_ff1=(b_ff1, m_b_ff1, v_b_ff1), b_ff2=(b_ff2, m_b_ff2, v_b_ff2),
                 ln2_g=(ln2_g, m_ln2_g, v_ln2_g), ln2_b=(ln2_b, m_ln2_b, v_ln2_b))
    upd = _small_update(small_all, *([_row(given[n][i]) for n in SMALL_PARAMS] for i in range(3)))
    small_res = {n: [t.reshape(given[n][0].shape) for t in upd[4 * j:4 * j + 4]] for j, n in enumerate(SMALL_PARAMS)}
    g_conv_all, loss_lanes = upd[4 * len(SMALL_PARAMS)], upd[4 * len(SMALL_PARAMS) + 1]
    loss = jnp.sum(loss_lanes)

    dmod_mine = lax.dynamic_slice(small_all[:, :6 * D_MODEL], (0, chip * ADA_COLS), (N_DEV, ADA_COLS))
    big = {"ada_w": [t[None] for t in _ada_bwd(c_all, dmod_mine, ada_w[0], m_ada_w[0], v_ada_w[0])]}

    g_conv = lax.dynamic_slice(g_conv_all.reshape(CONV_K, CONV_DIM), (0, chip * 384), (CONV_K, 384))
    big["conv_w"] = [t[None] for t in (g_conv, *_adamw(conv_w[0], g_conv, m_conv_w[0], v_conv_w[0], "adamw_conv_w"))]

    g_rows = g_b[:W_COLS].reshape(W_COLS, 1, D_MODEL)
    big["w_in"] = [from_rows(t) for t in (g_rows, *_adamw_rows(as_rows(w_in), g_rows, as_rows(m_w_in), as_rows(v_w_in)))]
    for name, g, (w, m, v) in (("w_out", g_c, (w_out, m_w_out, v_w_out)),
                               ("w_ff1", g_a[:D_MODEL], (w_ff1, m_w_ff1, v_w_ff1)),
                               ("w_ff2", g_a[D_MODEL:GA_ROWS], (w_ff2, m_w_ff2, v_w_ff2))):
        big[name] = [t[None] for t in (g, *_adamw(w[0], g, m[0], v[0], "adamw_" + name))]

    order = ("ln_in_g", "ln_in_b", "ada_w", "ada_b", "w_in", "conv_w", "conv_b", "dt_bias", "a_log", "d_skip", "ssd_norm_w",
             "attn_sinks", "w_out", "ln1_g", "ln1_b", "w_ff1", "b_ff1", "w_ff2", "b_ff2", "ln2_g", "ln2_b")
    res = {**small_res, **big}
    return (loss, grad_x[None], *[res[n][k] for k in range(4) for n in order])
```

```python
import functools
import math

import numpy as np
import jax
import jax.numpy as jnp
from jax import lax
from jax.experimental import pallas as pl
from jax.experimental.pallas import tpu as pltpu

f32 = jnp.float32
bf16 = jnp.bfloat16

D_MODEL = 1024
SSD_WIDTH = 1024
SSD_HEADS = 16
HEAD_DIM = 64
SSD_STATE = 128
SSD_GROUPS = 2
CHUNK = 128
CONV_K = 4
CONV_DIM = 1536
ATTN_HEADS = 16
D_FF = 4096
PROJ_WIDTH = 3856
ALPHA = 2.0 ** 0.25
LN_EPS = 1e-5
RMS_EPS = 1e-5
ATTN_SCALE = HEAD_DIM ** -0.5
NEG = -1e30

ADAM_LR = 0.001
ADAM_B1 = 0.9
ADAM_B2 = 0.999
ADAM_EPS = 1e-08
ADAM_WD = 0.01
ADAM_STEP = 10

W_Z, W_XBC, W_Q, W_KV = slice(0, 1024), slice(1024, 2560), slice(2576, 3600), slice(3600, 3856)
W_DT = slice(2560, 2688)
W_DT_ROWS = 16
GA_ROWS = 2048
GB_ROWS = 1024
GC_ROWS = 512
WG_TM = 512
WG_TK = 1024
STAGE_ROWS = 512
ADAMW_BLOCK_ELEMS = 1 << 18
DENSE_TM = 512
N_CHIPS = 4
N_DEV = 8
VMEM_LIMIT = 56 * 1024 * 1024
MESH = pl.DeviceIdType.MESH

ALIBI_SLOPES = tuple(2.0 ** (-8.0 / ATTN_HEADS * (i + 1)) for i in range(ATTN_HEADS))


def _cparams(*sem):
    return pltpu.CompilerParams(dimension_semantics=sem, vmem_limit_bytes=VMEM_LIMIT)


def _sigmoid(x):
    return 1.0 / (1.0 + jnp.exp(-x))


def _softplus(x):
    return jnp.maximum(x, 0.0) + jnp.log1p(jnp.exp(-jnp.abs(x)))


def _ln_stats(x):
    mu = jnp.mean(x, axis=-1, keepdims=True)
    xc = x - mu
    var = jnp.mean(xc * xc, axis=-1, keepdims=True)
    rstd = lax.rsqrt(var + LN_EPS)
    return xc * rstd, rstd


def _ln_bwd(dy, xhat, rstd, g):
    dxh = dy * g
    m1 = jnp.mean(dxh, axis=-1, keepdims=True)
    m2 = jnp.mean(dxh * xhat, axis=-1, keepdims=True)
    return rstd * (dxh - m1 - xhat * m2)


def _dot(a, b):
    return jnp.dot(a, b, preferred_element_type=f32)


def _dot_nt(a, b):
    return lax.dot_general(a, b, (((1,), (1,)), ((), ())), preferred_element_type=f32)


def _dot_tn(a, b):
    return lax.dot_general(a, b, (((0,), (0,)), ((), ())), preferred_element_type=f32)


def _dot_exact(a, b):
    return jnp.dot(a, b, preferred_element_type=f32, precision=lax.Precision.HIGHEST)


def _split3(v):
    hi = v.astype(bf16)
    r1 = v - hi.astype(f32)
    mid = r1.astype(bf16)
    lo = (r1 - mid.astype(f32)).astype(bf16)
    return hi, mid, lo


def _sel_dot(sel, v):
    hi, mid, lo = _split3(v)
    return _dot(sel, hi) + _dot(sel, mid) + _dot(sel, lo)


def _dot_sel_nt(v, sel):
    hi = v.astype(bf16)
    mid = (v - hi.astype(f32)).astype(bf16)
    return _dot_nt(hi, sel) + _dot_nt(mid, sel)


def _expand_heads(v):
    lane = lax.broadcasted_iota(jnp.int32, (v.shape[0], 128), 1)
    blocks = [jnp.where(lane < HEAD_DIM, v[:, 2 * b:2 * b + 1], v[:, 2 * b + 1:2 * b + 2]) for b in range(SSD_HEADS // 2)]
    return jnp.concatenate(blocks, axis=1)


def _full(shape):
    nd = len(shape)
    return pl.BlockSpec(shape, lambda *_: (0,) * nd)


def _resident(shape):
    nd = len(shape)
    return pl.BlockSpec(shape, lambda *_: (0,) * nd, pipeline_mode=pl.Buffered(1))


def _rows(tm, n):
    return pl.BlockSpec((tm, n), lambda i: (i, 0))


def _inproj_fwd(x, mod, ln_g, ln_b, w_in_t, conv_w, conv_b, blob):
    L = x.shape[0]
    tm = DENSE_TM
    nt = L // tm
    R = blob.shape[0]

    def body(x_ref, mod_ref, g_ref, b_ref, w_ref, cw_ref, cb_ref, blob_ref,
             u1_ref, z_ref, xr_ref, xc_ref, q_ref, kv_ref, dt_ref, wall_ref, halo, buf, send_sems, recv_sems):
        start, finish = _gather_job(blob_ref, wall_ref, send_sems, recv_sems, R)

        @pl.when(pl.program_id(0) == 0)
        def _():
            halo[...] = jnp.zeros_like(halo)
            start()

        xhat, _ = _ln_stats(x_ref[...])
        h0 = xhat * g_ref[...] + b_ref[...]
        u1 = (h0 * (1.0 + mod_ref[1:2, :]) + mod_ref[0:1, :]).astype(bf16)
        u1_ref[...] = u1
        z_ref[...] = _dot_nt(u1, w_ref[W_Z, :])
        xr = _dot_nt(u1, w_ref[W_XBC, :])
        xr_ref[...] = xr
        q_ref[...] = (_dot_nt(u1, w_ref[W_Q, :]) * ATTN_SCALE).astype(bf16)
        kv_ref[...] = _dot_nt(u1, w_ref[W_KV, :]).astype(bf16)
        dt_ref[...] = _dot_nt(u1, w_ref[W_DT, :])
        buf[0:8, :] = halo[...]
        buf[8:8 + tm, :] = xr
        pre = cb_ref[...] + cw_ref[0:1, :] * buf[5:5 + tm, :]
        for k in range(1, CONV_K):
            pre = pre + cw_ref[k:k + 1, :] * buf[5 + k:5 + k + tm, :]
        xc_ref[...] = pre * _sigmoid(pre)
        halo[...] = xr[tm - 8:tm, :]

        @pl.when(pl.program_id(0) == nt - 1)
        def _():
            finish()

    return pl.pallas_call(
        body, name="inproj_fwd", grid=(nt,),
        in_specs=[_rows(tm, D_MODEL), _full((8, D_MODEL)), _full((1, D_MODEL)), _full((1, D_MODEL)),
                  _resident((PROJ_WIDTH, D_MODEL)), _full((CONV_K, CONV_DIM)), _full((1, CONV_DIM)), _ANY_SPEC],
        out_specs=[_rows(tm, D_MODEL), _rows(tm, D_MODEL), _rows(tm, CONV_DIM), _rows(tm, CONV_DIM),
                   _rows(tm, D_MODEL), _rows(tm, 256), _rows(tm, 128), _ANY_SPEC],
        out_shape=[jax.ShapeDtypeStruct((L, D_MODEL), bf16), jax.ShapeDtypeStruct((L, D_MODEL), f32),
                   jax.ShapeDtypeStruct((L, CONV_DIM), f32), jax.ShapeDtypeStruct((L, CONV_DIM), f32),
                   jax.ShapeDtypeStruct((L, D_MODEL), bf16), jax.ShapeDtypeStruct((L, 256), bf16),
                   jax.ShapeDtypeStruct((L, 128), f32), jax.ShapeDtypeStruct((N_CHIPS, R, D_MODEL), bf16)],
        scratch_shapes=[pltpu.VMEM((8, CONV_DIM), f32), pltpu.VMEM((tm + 8, CONV_DIM), f32),
                        pltpu.SemaphoreType.DMA((6,)), pltpu.SemaphoreType.DMA((6,))],
        compiler_params=_cparams("arbitrary"),
    )(x, mod, ln_g, ln_b, w_in_t, conv_w, conv_b, blob)


def _head_expand():
    e = np.zeros((128, SSD_WIDTH), np.float32)
    for h in range(SSD_HEADS):
        e[h, h * HEAD_DIM:(h + 1) * HEAD_DIM] = 1.0
    return jnp.asarray(e, dtype=bf16)


def _ssd_chunk_common(dt_raw, dtb, a_row):
    T = CHUNK
    lane = lax.broadcasted_iota(jnp.int32, (T, 128), 1)
    dt = jnp.where(lane < SSD_HEADS, _softplus(dt_raw + dtb), 0.0)
    a = dt * a_row
    r = lax.broadcasted_iota(jnp.int32, (T, T), 0)
    c = lax.broadcasted_iota(jnp.int32, (T, T), 1)
    tril = (c <= r).astype(bf16)
    cum = _sel_dot(tril, a)
    dtx = _expand_heads(dt)
    cumx = _expand_heads(cum)
    return dt, a, cum, dtx, cumx, r, c


def _ssd_fwd(xc, dt_raw, dt_bias, a_log, d_skip_x, blob):
    L = xc.shape[0]
    nc = L // CHUNK
    T = CHUNK
    R = blob.shape[0]

    def body(xc_ref, dt_ref, dtb_ref, al_ref, dsk_ref, blob_ref, y_ref, prev_ref, wall_ref, st, send_sems, recv_sems):
        start, finish = _gather_job(blob_ref, wall_ref, send_sems, recv_sems, R)

        @pl.when(pl.program_id(0) == 0)
        def _():
            st[...] = jnp.zeros_like(st)
            start()

        @pl.when(pl.program_id(0) == nc - 1)
        def _():
            finish()

        a_row = -jnp.exp(al_ref[...])
        lane1 = lax.broadcasted_iota(jnp.int32, (1, 128), 1)
        a_row = jnp.where(lane1 < SSD_HEADS, a_row, 0.0)
        dt, a, cum, dtx, cumx, r, c = _ssd_chunk_common(dt_ref[...], dtb_ref[...], a_row)
        cum_t = cum.T
        ex = jnp.exp(cumx)
        last = cumx[T - 1:T, :]
        wx = jnp.exp(last - cumx)
        cdx = jnp.exp(last)
        xs = xc_ref[:, 0:SSD_WIDTH]
        X = xs * dtx
        Xb = X.astype(bf16)
        Xd = (X * wx).astype(bf16)
        prev = st[...]
        prev_ref[0] = prev
        prevb = prev.astype(bf16)
        tri = c <= r
        lane = lax.broadcasted_iota(jnp.int32, (T, 128), 1)
        y_blocks = []
        new_states = []
        for g in range(SSD_GROUPS):
            Bg = xc_ref[:, 1024 + 128 * g:1152 + 128 * g].astype(bf16)
            Cg = xc_ref[:, 1280 + 128 * g:1408 + 128 * g].astype(bf16)
            G = _dot_nt(Cg, Bg)
            yoff = _dot(Cg, prevb[:, 512 * g:512 * (g + 1)])
            new_states.append(_dot_tn(Bg, Xd[:, 512 * g:512 * (g + 1)]))
            for j in range(4):
                blk = 4 * g + j
                Xblk = Xb[:, 128 * blk:128 * (blk + 1)]
                ys = []
                for half in range(2):
                    h = 2 * blk + half
                    seg = jnp.minimum(cum[:, h:h + 1] - cum_t[h:h + 1, :], 0.0)
                    M = jnp.where(tri, G * jnp.exp(seg), 0.0).astype(bf16)
                    ys.append(_dot(M, Xblk))
                yd = jnp.where(lane < HEAD_DIM, ys[0], ys[1])
                sl = slice(128 * blk, 128 * (blk + 1))
                y_blocks.append(yd + ex[:, sl] * yoff[:, 128 * j:128 * (j + 1)] + dsk_ref[:, sl] * xs[:, sl])
        y_ref[...] = jnp.concatenate(y_blocks, axis=1)
        st[...] = prev * cdx + jnp.concatenate(new_states, axis=1)

    return pl.pallas_call(
        body, name="ssd_fwd", grid=(nc,),
        in_specs=[_rows(T, CONV_DIM), _rows(T, 128), _full((1, 128)), _full((1, 128)), _full((1, SSD_WIDTH)), _ANY_SPEC],
        out_specs=[_rows(T, SSD_WIDTH), pl.BlockSpec((1, SSD_STATE, SSD_WIDTH), lambda i: (i, 0, 0)), _ANY_SPEC],
        out_shape=[jax.ShapeDtypeStruct((L, SSD_WIDTH), f32), jax.ShapeDtypeStruct((nc, SSD_STATE, SSD_WIDTH), f32),
                   jax.ShapeDtypeStruct((N_CHIPS, R, D_MODEL), bf16)],
        scratch_shapes=[pltpu.VMEM((SSD_STATE, SSD_WIDTH), f32)] + _sems(6),
        compiler_params=_cparams("arbitrary"),
    )(xc, dt_raw, dt_bias, a_log, d_skip_x, blob)


def _kv_halves(kv_prev, kv_cur, first):
    kv = jnp.concatenate([jnp.where(first, 0.0, kv_prev.astype(f32)), kv_cur.astype(f32)], axis=0)
    lane = lax.broadcasted_iota(jnp.int32, (2 * CHUNK, 128), 1)
    lo = lane < HEAD_DIM
    out = []
    for g in range(2):
        per_half = []
        for half in range(2):
            both = []
            for t in (kv[:, 0:128], kv[:, 128:256]):
                src = t if g == half else pltpu.roll(t, HEAD_DIM, 1)
                both.append(jnp.where(lo if half == 0 else ~lo, src, 0.0).astype(bf16))
            per_half.append(tuple(both))
        out.append(per_half)
    return out


def _attn_masks(first):
    r = lax.broadcasted_iota(jnp.int32, (CHUNK, 2 * CHUNK), 0)
    c = lax.broadcasted_iota(jnp.int32, (CHUNK, 2 * CHUNK), 1)
    dist = r + CHUNK - c
    valid = (dist >= 0) & (dist < CHUNK) & ((c >= CHUNK) | jnp.logical_not(first))
    return dist.astype(f32), valid


def _head_stack(g, half, sink_ref):
    blks = [4 * g + i for i in range(4)]
    heads = [2 * b + half for b in blks]
    slope = jnp.concatenate([jnp.full((CHUNK, 1), ALIBI_SLOPES[h], f32) for h in heads], axis=0)
    sink = jnp.concatenate([jnp.full((CHUNK, 1), sink_ref[h], f32) for h in heads], axis=0)
    return blks, heads, slope, sink


def _attn_fwd(q, kv, sinks, blob):
    L = q.shape[0]
    nb = L // CHUNK
    T = CHUNK
    R = blob.shape[0]

    def body(sink_ref, q_ref, kvp_ref, kvc_ref, blob_ref, o_ref, lse_ref, wall_ref, send_sems, recv_sems):
        first = pl.program_id(0) == 0
        start, finish = _gather_job(blob_ref, wall_ref, send_sems, recv_sems, R)

        @pl.when(first)
        def _():
            start()

        @pl.when(pl.program_id(0) == nb - 1)
        def _():
            finish()

        ext = _kv_halves(kvp_ref[...], kvc_ref[...], first)
        dist, valid = _attn_masks(first)
        lane = lax.broadcasted_iota(jnp.int32, (T, 128), 1)
        lse = jnp.zeros((T, 128), f32)
        o_blocks = []
        for blk in range(8):
            qb = q_ref[:, 128 * blk:128 * (blk + 1)]
            acc = None
            for half in range(2):
                h = 2 * blk + half
                k_ext, v_ext = ext[h // 8][half]
                s = _dot_nt(qb, k_ext) - ALIBI_SLOPES[h] * dist
                s = jnp.where(valid, s, NEG)
                sink = sink_ref[h]
                m = jnp.maximum(jnp.max(s, axis=-1, keepdims=True), sink)
                p = jnp.exp(s - m)
                den = jnp.sum(p, axis=-1, keepdims=True) + jnp.exp(sink - m)
                pn = (p * (1.0 / den)).astype(bf16)
                oh = _dot(pn, v_ext)
                acc = oh if acc is None else acc + oh
                lse = jnp.where(lane == h, m + jnp.log(den), lse)
            o_blocks.append(acc.astype(bf16))
        o_ref[...] = jnp.concatenate(o_blocks, axis=1)
        lse_ref[...] = lse

    return pl.pallas_call(
        body, name="attn_fwd", grid=(nb,),
        in_specs=[pl.BlockSpec(memory_space=pltpu.SMEM), _rows(T, D_MODEL),
                  pl.BlockSpec((T, 256), lambda i: (jnp.maximum(i - 1, 0), 0)), _rows(T, 256), _ANY_SPEC],
        out_specs=[_rows(T, D_MODEL), _rows(T, 128), _ANY_SPEC],
        out_shape=[jax.ShapeDtypeStruct((L, D_MODEL), bf16), jax.ShapeDtypeStruct((L, 128), f32),
                   jax.ShapeDtypeStruct((N_CHIPS, R, D_MODEL), bf16)],
        scratch_shapes=_sems(6),
        compiler_params=_cparams("arbitrary"),
    )(sinks, q, kv, kv, blob)


def _gated_norm(y, z, w):
    sz = _sigmoid(z)
    hg = y * (z * sz)
    ns, rss = [], []
    for g in range(SSD_GROUPS):
        hs = hg[:, 512 * g:512 * (g + 1)]
        rs = lax.rsqrt(jnp.mean(hs * hs, axis=-1, keepdims=True) + RMS_EPS)
        ns.append(hs * rs)
        rss.append(rs)
    n = jnp.concatenate(ns, axis=1)
    return n * w, n, rss, sz


def _outproj_fwd(y, z, o, x, mod, ln_g, ln_b, norm_w, w_out):
    L = x.shape[0]
    tm = DENSE_TM

    def body(y_ref, z_ref, o_ref, x_ref, mod_ref, g_ref, b_ref, nw_ref, w_ref, yn_ref, mix_ref, r1_ref):
        yn, _, _, _ = _gated_norm(y_ref[...], z_ref[...], nw_ref[...])
        ynb = yn.astype(bf16)
        yn_ref[...] = ynb
        mix = (_dot(ynb[:, 0:512], w_ref[0]) + _dot(ynb[:, 512:1024], w_ref[1])
               + _dot(o_ref[:, 0:512], w_ref[2]) + _dot(o_ref[:, 512:1024], w_ref[3]))
        mix_ref[...] = mix
        xhat, _ = _ln_stats(x_ref[...])
        h0 = xhat * g_ref[...] + b_ref[...]
        r1_ref[...] = ALPHA * h0 + (1.0 + mod_ref[2:3, :]) * mix

    v = _full((1, D_MODEL))
    return pl.pallas_call(
        body, name="outproj_fwd", grid=(L // tm,),
        in_specs=[_rows(tm, D_MODEL), _rows(tm, D_MODEL), _rows(tm, D_MODEL), _rows(tm, D_MODEL),
                  _full((8, D_MODEL)), v, v, v, _resident((N_CHIPS, 512, D_MODEL))],
        out_specs=[_rows(tm, D_MODEL)] * 3,
        out_shape=[jax.ShapeDtypeStruct((L, D_MODEL), bf16), jax.ShapeDtypeStruct((L, D_MODEL), f32),
                   jax.ShapeDtypeStruct((L, D_MODEL), f32)],
        compiler_params=_cparams("parallel"),
    )(y, z, o, x, mod, ln_g, ln_b, norm_w, w_out)


A_LN2G, A_LN2B, A_G2, A_B2, A_SC2, A_SH2, A_LN1G, A_LN1B, A_LOSS = range(9)


def _mlp_fwd_bwd(r1, target, mod, ln1_g, ln1_b, ln2_g, ln2_b, w1, b1, w2, b2):
    L = r1.shape[0]
    tm = 256
    nj = D_FF // 1024

    def body(r1_ref, t_ref, mod_ref, g1_ref, bb1_ref, g2_ref, bb2_ref, w1_ref, b1_ref, w2_ref, b2_ref,
             dr1_ref, u2_ref, s_ref, da_ref, df_ref, acc_ref, db1_ref, hr):
        @pl.when(pl.program_id(0) == 0)
        def _():
            acc_ref[...] = jnp.zeros_like(acc_ref)
            db1_ref[...] = jnp.zeros_like(db1_ref)

        sc2, sh2, gate2 = mod_ref[4:5, :], mod_ref[3:4, :], mod_ref[5:6, :]
        xhat1, rstd1 = _ln_stats(r1_ref[...])
        h1 = xhat1 * g1_ref[...] + bb1_ref[...]
        u2f = h1 * (1.0 + sc2) + sh2
        u2 = u2f.astype(bf16)
        u2_ref[...] = u2
        f = jnp.zeros((tm, D_MODEL), f32) + b2_ref[...]
        for j in range(nj):
            cs = slice(1024 * j, 1024 * (j + 1))
            a = _dot(u2, w1_ref[j]) + b1_ref[:, cs]
            hrj = jnp.maximum(a, 0.0)
            hr[:, cs] = hrj
            sj = (hrj * hrj).astype(bf16)
            s_ref[:, cs] = sj
            f = f + _dot(sj, w2_ref[j])
        r2 = ALPHA * h1 + (1.0 + gate2) * f
        xhat2, rstd2 = _ln_stats(r2)
        h2 = xhat2 * g2_ref[...] + bb2_ref[...]
        diff = h2 - t_ref[...]
        dh2 = diff * (1.0 / D_MODEL)

        def add(row, val):
            acc_ref[row:row + 1, :] += jnp.sum(val, axis=0, keepdims=True)

        add(A_LOSS, diff * diff * (0.5 / D_MODEL))
        add(A_LN2G, dh2 * xhat2)
        add(A_LN2B, dh2)
        dr2 = _ln_bwd(dh2, xhat2, rstd2, g2_ref[...])
        add(A_G2, dr2 * f)
        df = dr2 * (1.0 + gate2)
        add(A_B2, df)
        dfb = df.astype(bf16)
        df_ref[...] = dfb
        du2 = jnp.zeros((tm, D_MODEL), f32)
        for j in range(nj):
            cs = slice(1024 * j, 1024 * (j + 1))
            ds = _dot_nt(dfb, w2_ref[j])
            daj = ds * (2.0 * hr[:, cs])
            db1_ref[:, cs] += jnp.sum(daj, axis=0, keepdims=True)
            dajb = daj.astype(bf16)
            da_ref[:, cs] = dajb
            du2 = du2 + _dot_nt(dajb, w1_ref[j])
        add(A_SC2, du2 * h1)
        add(A_SH2, du2)
        dh1 = ALPHA * dr2 + du2 * (1.0 + sc2)
        add(A_LN1G, dh1 * xhat1)
        add(A_LN1B, dh1)
        dr1_ref[...] = _ln_bwd(dh1, xhat1, rstd1, g1_ref[...])

    v = _full((1, D_MODEL))
    return pl.pallas_call(
        body, name="mlp_fwd_bwd", grid=(L // tm,),
        in_specs=[_rows(tm, D_MODEL), _rows(tm, D_MODEL), _full((8, D_MODEL)), v, v, v, v,
                  _resident((N_CHIPS, D_MODEL, D_MODEL)), _full((1, D_FF)), _resident((N_CHIPS, D_MODEL, D_MODEL)), v],
        out_specs=[_rows(tm, D_MODEL), _rows(tm, D_MODEL), _rows(tm, D_FF), _rows(tm, D_FF), _rows(tm, D_MODEL),
                   _full((16, D_MODEL)), _full((1, D_FF))],
        out_shape=[jax.ShapeDtypeStruct((L, D_MODEL), f32), jax.ShapeDtypeStruct((L, D_MODEL), bf16),
                   jax.ShapeDtypeStruct((L, D_FF), bf16), jax.ShapeDtypeStruct((L, D_FF), bf16),
                   jax.ShapeDtypeStruct((L, D_MODEL), bf16), jax.ShapeDtypeStruct((16, D_MODEL), f32),
                   jax.ShapeDtypeStruct((1, D_FF), f32)],
        scratch_shapes=[pltpu.VMEM((tm, D_FF), f32)],
        compiler_params=_cparams("arbitrary"),
    )(r1, target, mod, ln1_g, ln1_b, ln2_g, ln2_b, w1, b1, w2, b2)


def _wgrad(a, b, name):
    L, M = a.shape
    N = b.shape[1]
    tm = min(M, 512)
    tn = next(t for t in (1024, 768, 512, 256, 128) if N % t == 0)
    tk = min(L, WG_TK)
    nk = L // tk

    def body(a_ref, b_ref, o_ref, acc):
        def store(v):
            o_ref[...] = v

        _accumulate_tn(a_ref, b_ref, acc, nk, store)

    return pl.pallas_call(
        body, name=name, grid=(M // tm, N // tn, nk),
        in_specs=[pl.BlockSpec((tk, tm), lambda i, j, k: (k, i)), pl.BlockSpec((tk, tn), lambda i, j, k: (k, j))],
        out_specs=pl.BlockSpec((tm, tn), lambda i, j, k: (i, j)),
        out_shape=jax.ShapeDtypeStruct((M, N), bf16), scratch_shapes=[pltpu.VMEM((tm, tn), f32)],
        compiler_params=_cparams("parallel", "parallel", "arbitrary"),
    )(a, b)


def _accumulate_tn(a_ref, b_ref, acc, nk, emit):
    k = pl.program_id(2)
    part = _dot_tn(a_ref[...], b_ref[...])

    @pl.when(k == 0)
    def _():
        acc[...] = part

    @pl.when(k > 0)
    def _():
        acc[...] += part

    @pl.when(k == nk - 1)
    def _():
        emit(acc[...].astype(bf16))


def _wgrad_blob(blob, a, b, name, place_of):
    L, M = a.shape
    N = b.shape[1]
    tk = min(L, WG_TK) if M * N <= WG_TM * 4 * D_MODEL else L
    nk = L // tk

    def body(blob_ref, a_ref, b_ref, o_ref, acc):
        def store(v):
            o_ref[0] = v

        _accumulate_tn(a_ref, b_ref, acc, nk, store)

    return pl.pallas_call(
        body, name=name, grid=(M // WG_TM, N // D_MODEL, nk),
        in_specs=[pl.BlockSpec(memory_space=pl.ANY), pl.BlockSpec((tk, WG_TM), lambda t, n, k: (k, t)),
                  pl.BlockSpec((tk, D_MODEL), lambda t, n, k: (k, n))],
        out_specs=pl.BlockSpec((1, WG_TM, D_MODEL), lambda t, n, k: (*place_of(t, n), 0)),
        out_shape=jax.ShapeDtypeStruct(blob.shape, bf16), input_output_aliases={0: 0},
        scratch_shapes=[pltpu.VMEM((WG_TM, D_MODEL), f32)],
        compiler_params=_cparams("parallel", "parallel", "arbitrary"),
    )(blob, a, b)


def _outproj_bwd(dr1, mix, y, z, mod, norm_w, w_out, gb):
    L = dr1.shape[0]
    tm = DENSE_TM
    nt = L // tm

    def body(dr1_ref, mix_ref, y_ref, z_ref, mod_ref, nw_ref, w_ref, gb_ref,
             dy_ref, dz_ref, do_ref, dmix_ref, acc_ref, sib_ref, send_sems, recv_sems):
        start, wait = _to_sibling_job(gb_ref, sib_ref, send_sems, recv_sems)

        @pl.when(pl.program_id(0) == 0)
        def _():
            acc_ref[...] = jnp.zeros_like(acc_ref)
            start()

        @pl.when(pl.program_id(0) == nt - 1)
        def _():
            wait()

        dr1 = dr1_ref[...]
        acc_ref[0:1, :] += jnp.sum(dr1 * mix_ref[...], axis=0, keepdims=True)
        dmix = (dr1 * (1.0 + mod_ref[2:3, :])).astype(bf16)
        dmix_ref[...] = dmix
        dyn = jnp.concatenate([_dot_nt(dmix, w_ref[0]), _dot_nt(dmix, w_ref[1])], axis=1)
        do_ref[...] = jnp.concatenate([_dot_nt(dmix, w_ref[2]), _dot_nt(dmix, w_ref[3])], axis=1).astype(bf16)
        yv, zv = y_ref[...], z_ref[...]
        _, n, rss, sz = _gated_norm(yv, zv, nw_ref[...])
        acc_ref[1:2, :] += jnp.sum(dyn * n, axis=0, keepdims=True)
        dn = dyn * nw_ref[...]
        parts = []
        for g in range(SSD_GROUPS):
            sl = slice(512 * g, 512 * (g + 1))
            dng, ng = dn[:, sl], n[:, sl]
            parts.append(rss[g] * (dng - ng * jnp.mean(dng * ng, axis=-1, keepdims=True)))
        dhg = jnp.concatenate(parts, axis=1)
        dy_ref[...] = dhg * (zv * sz)
        dz_ref[...] = (dhg * yv * (sz * (1.0 + zv * (1.0 - sz)))).astype(bf16)

    return pl.pallas_call(
        body, name="outproj_bwd", grid=(nt,),
        in_specs=[_rows(tm, D_MODEL)] * 4 + [_full((8, D_MODEL)), _full((1, D_MODEL)), _resident((N_CHIPS, 512, D_MODEL)),
                  _ANY_SPEC],
        out_specs=[_rows(tm, D_MODEL)] * 4 + [_full((8, D_MODEL)), _ANY_SPEC],
        out_shape=[jax.ShapeDtypeStruct((L, D_MODEL), f32)] + [jax.ShapeDtypeStruct((L, D_MODEL), bf16)] * 3
        + [jax.ShapeDtypeStruct((8, D_MODEL), f32), jax.ShapeDtypeStruct((N_CHIPS,) + gb.shape[2:], bf16)],
        scratch_shapes=_sems(N_CHIPS),
        compiler_params=_cparams("arbitrary"),
    )(dr1, mix, y, z, mod, norm_w, w_out, gb)


def _attn_bwd(q, kv, do, lse, sinks, pb, gb2):
    L = q.shape[0]
    nb = L // CHUNK
    T = CHUNK

    def body(sink_ref, q_ref, kvp_ref, kvc_ref, do_ref, lse_ref, pb_ref, gb2_ref, dq_ref, dkv_ref, dsink_ref, chips_ref,
             sib2_ref, carry, send_sems, recv_sems, send_sems2, recv_sems2):
        n = pl.program_id(0)
        start, wait = _to_chips_job(pb_ref, chips_ref, send_sems, recv_sems)
        start2, wait2 = _to_sibling_job(gb2_ref, sib2_ref, send_sems2, recv_sems2)

        @pl.when(n == 0)
        def _():
            carry[...] = jnp.zeros_like(carry)
            dsink_ref[...] = jnp.zeros_like(dsink_ref)
            start()
            start2()

        @pl.when(n < nb)
        def _():
            first = n == 0
            ext = _kv_halves(kvp_ref[...], kvc_ref[...], first)
            dist, valid = _attn_masks(first)
            dist4, valid4 = jnp.concatenate([dist] * 4, axis=0), jnp.concatenate([valid] * 4, axis=0)
            lane1 = lax.broadcasted_iota(jnp.int32, (1, 128), 1)
            lse = lse_ref[...]
            qts = [q_ref[:, 128 * b:128 * (b + 1)].astype(f32).T.astype(bf16) for b in range(8)]
            dots = [do_ref[:, 128 * b:128 * (b + 1)].astype(f32).T.astype(bf16) for b in range(8)]
            acck = [None, None]
            accv = [None, None]
            dsink = jnp.zeros((1, 128), f32)
            dq_acc = [None] * 8
            for g in range(2):
                for half in range(2):
                    k_ext, v_ext = ext[g][half]
                    blks, heads, slope, sink = _head_stack(g, half, sink_ref)
                    qs = jnp.concatenate([q_ref[:, 128 * b:128 * (b + 1)] for b in blks], axis=0)
                    dos = jnp.concatenate([do_ref[:, 128 * b:128 * (b + 1)] for b in blks], axis=0)
                    rows = slice(HEAD_DIM * half, HEAD_DIM * (half + 1))
                    qt = jnp.concatenate([qts[b][rows, :] for b in blks], axis=1)
                    dot_ = jnp.concatenate([dots[b][rows, :] for b in blks], axis=1)
                    lse_col = jnp.concatenate([lse[:, h:h + 1] for h in heads], axis=0)
                    s = _dot_nt(qs, k_ext) - slope * dist4
                    p = jnp.where(valid4, jnp.exp(s - lse_col), 0.0)
                    dp = _dot_nt(dos, v_ext)
                    delta = jnp.sum(p * dp, axis=-1, keepdims=True)
                    ds = (p * (dp - delta)).astype(bf16)
                    sd = jnp.exp(sink - lse_col) * delta
                    dqs = _dot(ds, k_ext) * ATTN_SCALE
                    for i, b in enumerate(blks):
                        seg = slice(T * i, T * (i + 1))
                        dq_acc[b] = dqs[seg, :] if dq_acc[b] is None else dq_acc[b] + dqs[seg, :]
                        dsink = dsink - jnp.where(lane1 == heads[i], jnp.sum(sd[seg, :], axis=0, keepdims=True), 0.0)
                    dk = _dot(qt, ds)
                    dv = _dot(dot_, p.astype(bf16))
                    acck[g] = dk if acck[g] is None else acck[g] + dk
                    accv[g] = dv if accv[g] is None else accv[g] + dv
            dq_ref[...] = jnp.concatenate([a.astype(bf16) for a in dq_acc], axis=1)
            dsink_ref[...] += dsink
            dkv = jnp.concatenate([jnp.concatenate(acck, axis=0).T, jnp.concatenate(accv, axis=0).T], axis=1)
            dkv_ref[...] = (carry[...] + dkv[0:T, :]).astype(bf16)
            carry[...] = dkv[T:2 * T, :]

        @pl.when(n == nb)
        def _():
            dkv_ref[...] = carry[...].astype(bf16)
            wait()
            wait2()

    cur = lambda i: (jnp.minimum(i, nb - 1), 0)
    return pl.pallas_call(
        body, name="attn_bwd", grid=(nb + 1,),
        in_specs=[pl.BlockSpec(memory_space=pltpu.SMEM), pl.BlockSpec((T, D_MODEL), cur),
                  pl.BlockSpec((T, 256), lambda i: (jnp.maximum(jnp.minimum(i, nb - 1) - 1, 0), 0)),
                  pl.BlockSpec((T, 256), cur), pl.BlockSpec((T, D_MODEL), cur), pl.BlockSpec((T, 128), cur), _ANY_SPEC,
                  _ANY_SPEC],
        out_specs=[pl.BlockSpec((T, D_MODEL), cur), pl.BlockSpec((T, 256), lambda i: (jnp.maximum(i - 1, 0), 0)),
                   _full((1, 128)), _ANY_SPEC, _ANY_SPEC],
        out_shape=[jax.ShapeDtypeStruct((L, D_MODEL), bf16), jax.ShapeDtypeStruct((L, 256), bf16),
                   jax.ShapeDtypeStruct((1, 128), f32), jax.ShapeDtypeStruct((N_CHIPS - 1,) + pb.shape[1:], bf16),
                   jax.ShapeDtypeStruct((N_CHIPS,) + gb2.shape[2:], bf16)],
        scratch_shapes=[pltpu.VMEM((T, 256), f32)] + _sems(N_CHIPS - 1) + _sems(N_CHIPS),
        compiler_params=_cparams("arbitrary"),
    )(sinks, q, kv, kv, do, lse, pb, gb2)


def _ssd_bwd(xc, dt_raw, dy, prev_all, dt_bias, a_log, d_skip_x, e_mat, g, pb2):
    L = xc.shape[0]
    nc = L // CHUNK
    T = CHUNK
    RG = g.shape[0]

    def body(xc_ref, dt_ref, dy_ref, prev_ref, dtb_ref, al_ref, dsk_ref, e_ref, g_in_ref, pb2_ref,
             dxc_ref, ddt_ref, acc_ref, dd_ref, g_ref, chips2_ref, dst, dxs_s, send_sems, recv_sems, send_sems2, recv_sems2):
        start, wait = _share_job(g_ref, send_sems, recv_sems, RG)
        start2, wait2 = _to_chips_job(pb2_ref, chips2_ref, send_sems2, recv_sems2)

        @pl.when(pl.program_id(0) == 0)
        def _():
            dst[...] = jnp.zeros_like(dst)
            acc_ref[...] = jnp.zeros_like(acc_ref)
            dd_ref[...] = jnp.zeros_like(dd_ref)
            start()
            start2()

        @pl.when(pl.program_id(0) == nc - 1)
        def _():
            wait()
            wait2()

        lane1 = lax.broadcasted_iota(jnp.int32, (1, 128), 1)
        a_row = jnp.where(lane1 < SSD_HEADS, -jnp.exp(al_ref[...]), 0.0)
        e_mat_v = e_ref[...]
        dt, a, cum, dtx, cumx, r, c = _ssd_chunk_common(dt_ref[...], dtb_ref[...], a_row)
        cum_t = cum.T
        ex = jnp.exp(cumx)
        last = cumx[T - 1:T, :]
        wx = jnp.exp(last - cumx)
        cdx = jnp.exp(last)
        xs = xc_ref[:, 0:SSD_WIDTH]
        X = xs * dtx
        Xb = X.astype(bf16)
        Xdb = (X * wx).astype(bf16)
        dyv = dy_ref[...]
        prev = prev_ref[0]
        prevb = prev.astype(bf16)
        dnew = dst[...]
        dnewb = dnew.astype(bf16)
        tri = c <= r
        lane = lax.broadcasted_iota(jnp.int32, (T, 128), 1)
        sub = lax.broadcasted_iota(jnp.int32, (128, T), 0)
        lo = lane < HEAD_DIM

        def red(vals, g):
            return _dot_sel_nt(vals, e_mat_v[:, 512 * g:512 * (g + 1)])

        de = jnp.zeros((T, 128), f32)
        dw = jnp.zeros((T, 128), f32)
        ddt_x = jnp.zeros((T, 128), f32)
        dcum_col = jnp.zeros((T, 128), f32)
        dcum_row = jnp.zeros((128, T), f32)
        dprev_parts, dBs, dCs = [], [], []
        for g in range(SSD_GROUPS):
            s5 = slice(512 * g, 512 * (g + 1))
            Bg = xc_ref[:, 1024 + 128 * g:1152 + 128 * g].astype(bf16)
            Cg = xc_ref[:, 1280 + 128 * g:1408 + 128 * g].astype(bf16)
            G = _dot_nt(Cg, Bg)
            Z = _dot(Cg, prevb[:, s5])
            dyg = dyv[:, s5]
            dZb = (dyg * ex[:, s5]).astype(bf16)
            dXd = _dot(Bg, dnewb[:, s5])
            dC = _dot_nt(dZb, prevb[:, s5])
            dB = _dot_nt(Xdb[:, s5], dnewb[:, s5])
            dprev_parts.append(_dot_tn(Cg, dZb) + dnew[:, s5] * cdx[:, s5])
            de = de + red(dyg * Z, g)
            dw = dw + red(dXd * X[:, s5], g)
            dXg = dXd * wx[:, s5]
            dG = jnp.zeros((T, T), f32)
            for j in range(4):
                blk = 4 * g + j
                sl = slice(128 * blk, 128 * (blk + 1))
                Xblk = Xb[:, sl]
                dyblk = dyv[:, sl]
                dyblk_b = dyblk.astype(bf16)
                dxh = []
                for half in range(2):
                    h = 2 * blk + half
                    seg = jnp.minimum(cum[:, h:h + 1] - cum_t[h:h + 1, :], 0.0)
                    Lm = jnp.where(tri, jnp.exp(seg), 0.0)
                    M = G * Lm
                    dyh = jnp.where(lo if half == 0 else ~lo, dyblk, 0.0).astype(bf16)
                    dM = _dot_nt(dyh, Xblk)
                    dG = dG + dM * Lm
                    Q = dM * M
                    dcum_col = dcum_col + jnp.where(lane == h, jnp.sum(Q, axis=1, keepdims=True), 0.0)
                    dcum_row = dcum_row + jnp.where(sub == h, jnp.sum(Q, axis=0, keepdims=True), 0.0)
                    dxh.append(_dot_tn(M.astype(bf16), dyblk_b))
                dXblk = dXg[:, 128 * j:128 * (j + 1)] + jnp.where(lo, dxh[0], dxh[1])
                xsb = xs[:, sl]
                dxs_s[:, sl] = dXblk * dtx[:, sl] + dsk_ref[:, sl] * dyblk
                ddt_x = ddt_x + _dot_sel_nt(dXblk * xsb, e_mat_v[:, sl])
                dd_ref[:, sl] += jnp.sum(dyblk * xsb, axis=0, keepdims=True)
            dGb = dG.astype(bf16)
            dCs.append(dC + _dot(dGb, Bg))
            dBs.append(dB + _dot_tn(dGb, Cg))
        e16 = jnp.exp(cum)
        cum_last = cum[T - 1:T, :]
        w16 = jnp.exp(cum_last - cum)
        dcd = jnp.sum(dnew * prev, axis=0, keepdims=True)
        dcd16 = red(dcd[:, 0:512], 0) + red(dcd[:, 512:1024], 1)
        dww = dw * w16
        extra = jnp.sum(dww, axis=0, keepdims=True) + dcd16 * jnp.exp(cum_last)
        rowi = lax.broadcasted_iota(jnp.int32, (T, 128), 0)
        dcum = dcum_col - dcum_row.T + de * e16 - dww + jnp.where(rowi == T - 1, extra, 0.0)
        da = _sel_dot((c >= r).astype(bf16), dcum)
        ddt = ddt_x + da * a_row
        acc_ref[0:1, :] += jnp.sum(da * dt, axis=0, keepdims=True)
        ddt_raw = jnp.where(lane < SSD_HEADS, ddt * _sigmoid(dt_ref[...] + dtb_ref[...]), 0.0)
        ddt_ref[...] = ddt_raw
        acc_ref[1:2, :] += jnp.sum(ddt_raw, axis=0, keepdims=True)
        dxc_ref[:, 0:SSD_WIDTH] = dxs_s[...]
        dxc_ref[:, 1024:1280] = jnp.concatenate(dBs, axis=1)
        dxc_ref[:, 1280:1536] = jnp.concatenate(dCs, axis=1)
        dst[...] = jnp.concatenate(dprev_parts, axis=1)

    rev = lambda i: (nc - 1 - i, 0)
    return pl.pallas_call(
        body, name="ssd_bwd", grid=(nc,),
        in_specs=[pl.BlockSpec((T, CONV_DIM), rev), pl.BlockSpec((T, 128), rev), pl.BlockSpec((T, SSD_WIDTH), rev),
                  pl.BlockSpec((1, SSD_STATE, SSD_WIDTH), lambda i: (nc - 1 - i, 0, 0)),
                  _full((1, 128)), _full((1, 128)), _full((1, SSD_WIDTH)), _full((128, SSD_WIDTH)), _ANY_SPEC, _ANY_SPEC],
        out_specs=[pl.BlockSpec((T, CONV_DIM), rev), pl.BlockSpec((T, 128), rev), _full((8, 128)),
                   _full((1, SSD_WIDTH)), _ANY_SPEC, _ANY_SPEC],
        out_shape=[jax.ShapeDtypeStruct((L, CONV_DIM), f32), jax.ShapeDtypeStruct((L, 128), f32),
                   jax.ShapeDtypeStruct((8, 128), f32), jax.ShapeDtypeStruct((1, SSD_WIDTH), f32),
                   jax.ShapeDtypeStruct(g.shape, f32), jax.ShapeDtypeStruct((N_CHIPS - 1,) + pb2.shape[1:], bf16)],
        input_output_aliases={8: 4},
        scratch_shapes=[pltpu.VMEM((SSD_STATE, SSD_WIDTH), f32), pltpu.VMEM((T, SSD_WIDTH), f32)] + _sems(1)
        + _sems(N_CHIPS - 1),
        compiler_params=_cparams("arbitrary"),
    )(xc, dt_raw, dy, prev_all, dt_bias, a_log, d_skip_x, e_mat, g, pb2)


def _conv_bwd(dxc, xr, conv_w, conv_b, g):
    L = dxc.shape[0]
    tm = 256
    nt = L // tm
    RG = g.shape[0]

    def body(dxc_ref, xr_ref, xh_ref, cw_ref, cb_ref, g_in_ref, dxr_ref, acc_ref, g_ref, carry, buf, ext, send_sems, recv_sems):
        i = pl.program_id(0)
        start, wait = _share_job(g_ref, send_sems, recv_sems, RG)

        @pl.when(i == 0)
        def _():
            carry[...] = jnp.zeros_like(carry)
            acc_ref[...] = jnp.zeros_like(acc_ref)
            ext[tm + 16:tm + CHUNK, :] = jnp.zeros((CHUNK - 16, CONV_DIM), bf16)
            start()

        @pl.when(i == nt - 1)
        def _():
            wait()

        buf[0:8, :] = jnp.where(i == nt - 1, 0.0, xh_ref[...])
        u = xr_ref[...]
        buf[8:8 + tm, :] = u
        pre = cb_ref[...] + cw_ref[CONV_K - 1:CONV_K, :] * u
        for k in range(CONV_K - 1):
            pre = pre + cw_ref[k:k + 1, :] * buf[5 + k:5 + k + tm, :]
        sg = _sigmoid(pre)
        dpre = dxc_ref[...] * (sg * (1.0 + pre * (1.0 - sg)))
        acc_ref[4:5, :] += jnp.sum(dpre, axis=0, keepdims=True)
        dpb = dpre.astype(bf16)
        ext[0:tm, :] = dpb
        ext[tm:tm + 16, :] = carry[...]
        acc_ref[CONV_K - 1:CONV_K, :] += jnp.sum(u * dpre, axis=0, keepdims=True)
        du = cw_ref[CONV_K - 1:CONV_K, :] * dpre
        r = lax.broadcasted_iota(jnp.int32, (CHUNK, 2 * CHUNK), 0)
        c = lax.broadcasted_iota(jnp.int32, (CHUNK, 2 * CHUNK), 1)
        for j in range(1, CONV_K):
            move = (c == r + j).astype(bf16)
            up = jnp.concatenate([_dot(move, ext[CHUNK * b:CHUNK * (b + 2), :]) for b in range(tm // CHUNK)], axis=0)
            k = CONV_K - 1 - j
            du = du + cw_ref[k:k + 1, :] * up
            acc_ref[k:k + 1, :] += jnp.sum(u * up, axis=0, keepdims=True)
        dxr_ref[...] = du.astype(bf16)
        carry[...] = dpb[0:16, :]

    rev = lambda i: (nt - 1 - i, 0)
    return pl.pallas_call(
        body, name="conv_bwd", grid=(nt,),
        in_specs=[pl.BlockSpec((tm, CONV_DIM), rev), pl.BlockSpec((tm, CONV_DIM), rev),
                  pl.BlockSpec((8, CONV_DIM), lambda i: (jnp.maximum((nt - 1 - i) * (tm // 8) - 1, 0), 0)),
                  _full((CONV_K, CONV_DIM)), _full((1, CONV_DIM)), _ANY_SPEC],
        out_specs=[pl.BlockSpec((tm, CONV_DIM), rev), _full((8, CONV_DIM)), _ANY_SPEC],
        out_shape=[jax.ShapeDtypeStruct((L, CONV_DIM), bf16), jax.ShapeDtypeStruct((8, CONV_DIM), f32),
                   jax.ShapeDtypeStruct(g.shape, f32)],
        input_output_aliases={5: 2},
        scratch_shapes=[pltpu.VMEM((16, CONV_DIM), bf16), pltpu.VMEM((tm + 8, CONV_DIM), f32),
                        pltpu.VMEM((tm + CHUNK, CONV_DIM), bf16)] + _sems(1),
        compiler_params=_cparams("arbitrary"),
    )(dxc, xr, xr, conv_w, conv_b, g)


def _inproj_bwd(dz, dxr, dq, dkv, ddt, dr1, x, mod, ln_g, ln_b, w_in, pb):
    L = x.shape[0]
    tm = DENSE_TM
    nt = L // tm

    def body(dz_ref, dxr_ref, dq_ref, dkv_ref, ddt_ref, dr1_ref, x_ref, mod_ref, g_ref, b_ref, w_ref, pb_ref,
             dx_ref, acc_ref, chips_ref, send_sems, recv_sems):
        start, wait = _to_chips_job(pb_ref, chips_ref, send_sems, recv_sems)

        @pl.when(pl.program_id(0) == 0)
        def _():
            acc_ref[...] = jnp.zeros_like(acc_ref)
            start()

        @pl.when(pl.program_id(0) == nt - 1)
        def _():
            wait()

        du1 = (_dot(dz_ref[...], w_ref[W_Z, :]) + _dot(dxr_ref[...], w_ref[W_XBC, :])
               + _dot(dq_ref[...], w_ref[W_Q, :]) + _dot(dkv_ref[...], w_ref[W_KV, :])
               + _dot(ddt_ref[...].astype(bf16), w_ref[W_DT, :]))
        xhat, rstd = _ln_stats(x_ref[...])
        h0 = xhat * g_ref[...] + b_ref[...]
        acc_ref[0:1, :] += jnp.sum(du1 * h0, axis=0, keepdims=True)
        acc_ref[1:2, :] += jnp.sum(du1, axis=0, keepdims=True)
        dh0 = du1 * (1.0 + mod_ref[1:2, :]) + ALPHA * dr1_ref[...]
        acc_ref[2:3, :] += jnp.sum(dh0 * xhat, axis=0, keepdims=True)
        acc_ref[3:4, :] += jnp.sum(dh0, axis=0, keepdims=True)
        dx_ref[...] = _ln_bwd(dh0, xhat, rstd, g_ref[...])

    v = _full((1, D_MODEL))
    return pl.pallas_call(
        body, name="inproj_bwd", grid=(nt,),
        in_specs=[_rows(tm, D_MODEL), _rows(tm, CONV_DIM), _rows(tm, D_MODEL), _rows(tm, 256), _rows(tm, 128),
                  _rows(tm, D_MODEL), _rows(tm, D_MODEL), _full((8, D_MODEL)), v, v, _resident((PROJ_WIDTH, D_MODEL)),
                  _ANY_SPEC],
        out_specs=[_rows(tm, D_MODEL), _full((8, D_MODEL)), _ANY_SPEC],
        out_shape=[jax.ShapeDtypeStruct((L, D_MODEL), f32), jax.ShapeDtypeStruct((8, D_MODEL), f32),
                   jax.ShapeDtypeStruct((N_CHIPS - 1,) + pb.shape[1:], bf16)],
        scratch_shapes=_sems(N_CHIPS - 1),
        compiler_params=_cparams("arbitrary"),
    )(dz, dxr, dq, dkv, ddt, dr1, x, mod, ln_g, ln_b, w_in, pb)


def _adamw_math(w, g, m, v):
    m = ADAM_B1 * m + (1.0 - ADAM_B1) * g
    v = ADAM_B2 * v + (1.0 - ADAM_B2) * (g * g)
    m_hat = m / (1.0 - ADAM_B1 ** ADAM_STEP)
    v_hat = v / (1.0 - ADAM_B2 ** ADAM_STEP)
    delta = -ADAM_LR * (m_hat / (jnp.sqrt(v_hat) + ADAM_EPS) + ADAM_WD * w)
    return delta, m, v


def _adamw(w, g, m, v, name):
    R, C = w.shape

    def body(w_ref, g_ref, m_ref, v_ref, d_ref, m2_ref, v2_ref):
        d_ref[...], m2_ref[...], v2_ref[...] = _adamw_math(w_ref[...], g_ref[...], m_ref[...], v_ref[...])

    cap = max(8, ADAMW_BLOCK_ELEMS // C)
    tr = R if R <= cap else next(t for t in range(cap - cap % 8, 7, -8) if R % t == 0)
    spec = pl.BlockSpec((tr, C), lambda i: (i, 0))
    return pl.pallas_call(
        body, name=name, grid=(R // tr,), in_specs=[spec] * 4, out_specs=[spec] * 3,
        out_shape=[jax.ShapeDtypeStruct((R, C), f32)] * 3, compiler_params=_cparams("parallel"),
    )(w, g, m, v)


def _adamw_rows(w, g, m, v):
    R, _, C = w.shape
    tr = R // 4

    def body(w_ref, g_ref, m_ref, v_ref, d_ref, m2_ref, v2_ref):
        d_ref[...], m2_ref[...], v2_ref[...] = _adamw_math(w_ref[...], g_ref[...], m_ref[...], v_ref[...])

    spec = pl.BlockSpec((tr, 1, C), lambda i: (i, 0, 0))
    return pl.pallas_call(
        body, name="adamw_w_in", grid=(R // tr,), in_specs=[spec] * 4, out_specs=[spec] * 3,
        out_shape=[jax.ShapeDtypeStruct((R, 1, C), f32)] * 3, compiler_params=_cparams("parallel"),
    )(w, g, m, v)


ADA_COLS = 6 * D_MODEL // N_CHIPS
ADA_TN = 512


COND_LANES = 512


def _prologue(cond, ada_w, ada_b, blob):
    R = blob.shape[0]

    def body(cond_ref, w_ref, b_ref, blob_ref, call_ref, mod_ref, wall_ref, mod_s, stage, gs, gr, ms, mr, ws, wr, local_sem):
        x, y, c = _place()
        start_w, finish_w = _gather_job(blob_ref, wall_ref, ws, wr, R)

        def rows(ref, px, py, pc):
            return ref.at[pl.ds(pl.multiple_of((4 * px + 2 * py + pc) * 8, 8), 8), :]

        mine = pltpu.make_async_copy(cond_ref, rows(call_ref, x, y, c), local_sem)
        mine.start()
        sends = [_remote(cond_ref, rows(call_ref, x, y, c), gs, gr, m - 1, _flip(x, y, c, m)) for m in range(1, N_DEV)]
        for cp in sends:
            cp.start()
        for m in range(1, N_DEV):
            peer = _flip(x, y, c, m)
            _remote(cond_ref, rows(call_ref, *peer), gs, gr, m - 1, peer).wait_recv()
        for cp in sends:
            cp.wait_send()
        mine.wait()
        start_w()

        for k in range(R // STAGE_ROWS):
            part = pl.ds(STAGE_ROWS * k, STAGE_ROWS)
            cin = pltpu.make_async_copy(blob_ref.at[part, :], stage, local_sem)
            cin.start()
            cin.wait()
            cout = pltpu.make_async_copy(stage, wall_ref.at[2 * x + y, part, :], local_sem)
            cout.start()
            cout.wait()

        lo = jnp.concatenate([call_ref[8 * d:8 * d + 1, :] for d in range(N_DEV)], axis=0)
        hi = jnp.concatenate([call_ref[8 * d + 1:8 * d + 2, :] for d in range(N_DEV)], axis=0)
        mod_all = (_dot_exact(lo * _sigmoid(lo), w_ref[0:COND_LANES, :]) + _dot_exact(hi * _sigmoid(hi), w_ref[COND_LANES:, :])
                   + b_ref[...])
        for d in range(N_DEV):
            mod_s[8 * d:8 * d + 8, :] = jnp.broadcast_to(mod_all[d:d + 1, :], (8, ADA_COLS))

        mine = pltpu.make_async_copy(rows(mod_s, x, y, c), mod_ref.at[2 * x + y], local_sem)
        mine.start()
        sends = []
        for m in range(1, N_CHIPS):
            peer = _flip(x, y, c, 2 * m)
            sends.append(_remote(rows(mod_s, *peer), mod_ref.at[2 * x + y], ms, mr, m - 1, peer))
        for cp in sends:
            cp.start()
        for m in range(1, N_CHIPS):
            px, py, pc = _flip(x, y, c, 2 * m)
            _remote(rows(mod_s, x, y, c), mod_ref.at[2 * px + py], ms, mr, m - 1, (px, py, pc)).wait_recv()
        for cp in sends:
            cp.wait_send()
        mine.wait()
        finish_w()

    return pl.pallas_call(
        body, name="prologue",
        out_shape=[jax.ShapeDtypeStruct((8 * N_DEV, COND_LANES), f32), jax.ShapeDtypeStruct((N_CHIPS, 8, ADA_COLS), f32),
                   jax.ShapeDtypeStruct((N_CHIPS, R, D_MODEL), bf16)],
        in_specs=[_VMEM_SPEC, _VMEM_SPEC, _VMEM_SPEC, _ANY_SPEC], out_specs=[_VMEM_SPEC, _VMEM_SPEC, _ANY_SPEC],
        scratch_shapes=[pltpu.VMEM((8 * N_DEV, ADA_COLS), f32), pltpu.VMEM((STAGE_ROWS, D_MODEL), bf16)]
        + _sems(N_DEV - 1) + _sems(N_CHIPS - 1) + _sems(6) + [pltpu.SemaphoreType.DMA],
        compiler_params=pltpu.CompilerParams(vmem_limit_bytes=VMEM_LIMIT),
    )(cond, ada_w, ada_b, blob)


def _ada_bwd(c_all, dmod, w, m, v):
    def body(c_ref, d_ref, w_ref, m_ref, v_ref, g_ref, dl_ref, m2_ref, v2_ref):
        cv = c_ref[...]
        g = lax.dot_general(cv * _sigmoid(cv), d_ref[...], (((0,), (0,)), ((), ())), preferred_element_type=f32,
                            precision=lax.Precision.HIGHEST)
        g_ref[...] = g
        dl_ref[...], m2_ref[...], v2_ref[...] = _adamw_math(w_ref[...], g, m_ref[...], v_ref[...])

    wspec = pl.BlockSpec((D_MODEL, ADA_TN), lambda j: (0, j))
    return pl.pallas_call(
        body, name="ada_bwd", grid=(ADA_COLS // ADA_TN,),
        in_specs=[_full((N_DEV, D_MODEL)), pl.BlockSpec((N_DEV, ADA_TN), lambda j: (0, j)), wspec, wspec, wspec],
        out_specs=[wspec] * 4, out_shape=[jax.ShapeDtypeStruct((D_MODEL, ADA_COLS), f32)] * 4,
        compiler_params=_cparams("parallel"),
    )(c_all, dmod, w, m, v)


SMALL_SLOTS = (("ada_b", 6144), ("ln_in_g", 1024), ("ln_in_b", 1024), ("conv_b", 1536), ("dt_bias", 128), ("a_log", 128),
               ("d_skip", 128), ("ssd_norm_w", 1024), ("attn_sinks", 128), ("ln1_g", 1024), ("ln1_b", 1024),
               ("b_ff1", 4096), ("b_ff2", 1024), ("ln2_g", 1024), ("ln2_b", 1024), ("conv_w", 6144), ("loss", 1024))
SMALL_N = sum(n for _, n in SMALL_SLOTS)
SMALL_OFF = {name: sum(n for _, n in SMALL_SLOTS[:i]) for i, (name, _) in enumerate(SMALL_SLOTS)}
SMALL_PARAMS = tuple(name for name, _ in SMALL_SLOTS[:15])
assert SMALL_N % 1024 == 0


def _small_pack(acc_in, acc_out, acc_mlp, db1, acc_conv, acc_ssd, dd_x, dsink, alog, e_mat):
    def body(in_ref, out_ref, mlp_ref, db1_ref, conv_ref, ssd_ref, dd_ref, sink_ref, al_ref, e_ref, o_ref):
        def put(name, val, at=0):
            off = SMALL_OFF[name] + at
            o_ref[:, off:off + val.shape[1]] = val

        for k, row in enumerate((in_ref[1:2, :], in_ref[0:1, :], out_ref[0:1, :], mlp_ref[A_SH2:A_SH2 + 1, :],
                                 mlp_ref[A_SC2:A_SC2 + 1, :], mlp_ref[A_G2:A_G2 + 1, :])):
            put("ada_b", row, D_MODEL * k)
        put("ln_in_g", in_ref[2:3, :])
        put("ln_in_b", in_ref[3:4, :])
        put("conv_b", conv_ref[4:5, :])
        put("dt_bias", ssd_ref[1:2, :])
        put("a_log", ssd_ref[0:1, :] * (-jnp.exp(al_ref[...])))
        put("d_skip", _dot_sel_nt(jnp.broadcast_to(dd_ref[...], (8, SSD_WIDTH)), e_ref[...])[0:1, :])
        put("ssd_norm_w", out_ref[1:2, :])
        put("attn_sinks", sink_ref[...])
        put("ln1_g", mlp_ref[A_LN1G:A_LN1G + 1, :])
        put("ln1_b", mlp_ref[A_LN1B:A_LN1B + 1, :])
        put("b_ff1", db1_ref[...])
        put("b_ff2", mlp_ref[A_B2:A_B2 + 1, :])
        put("ln2_g", mlp_ref[A_LN2G:A_LN2G + 1, :])
        put("ln2_b", mlp_ref[A_LN2B:A_LN2B + 1, :])
        for k in range(CONV_K):
            put("conv_w", conv_ref[k:k + 1, :], CONV_DIM * k)
        put("loss", mlp_ref[A_LOSS:A_LOSS + 1, :])

    return pl.pallas_call(body, name="small_pack", out_shape=jax.ShapeDtypeStruct((1, SMALL_N), f32),
                          compiler_params=_cparams())(acc_in, acc_out, acc_mlp, db1, acc_conv, acc_ssd, dd_x, dsink, alog, e_mat)


def _small_update(gathered, params, moms, vels):
    k = len(SMALL_PARAMS)

    def body(g_ref, *refs):
        w_refs, m_refs, v_refs, outs = refs[:k], refs[k:2 * k], refs[2 * k:3 * k], refs[3 * k:]

        def total(name, n):
            off = SMALL_OFF[name]
            g = g_ref[0:1, off:off + n]
            for i in range(1, N_DEV):
                g = g + g_ref[i:i + 1, off:off + n]
            return g

        for j, name in enumerate(SMALL_PARAMS):
            n = w_refs[j].shape[1]
            g = total(name, max(n, 128))[:, :n]
            outs[4 * j][...] = g
            outs[4 * j + 1][...], outs[4 * j + 2][...], outs[4 * j + 3][...] = _adamw_math(
                w_refs[j][...], g, m_refs[j][...], v_refs[j][...])
        outs[4 * k][...] = total("conv_w", CONV_K * CONV_DIM)
        outs[4 * k + 1][...] = total("loss", D_MODEL)

    shapes = [jax.ShapeDtypeStruct(p.shape, f32) for p in params for _ in range(4)]
    shapes += [jax.ShapeDtypeStruct((1, CONV_K * CONV_DIM), f32), jax.ShapeDtypeStruct((1, D_MODEL), f32)]
    return pl.pallas_call(body, name="small_update", out_shape=shapes,
                          compiler_params=_cparams())(gathered, *params, *moms, *vels)


def _place():
    return lax.axis_index("x"), lax.axis_index("y"), lax.axis_index("c")


def _flip(x, y, c, m):
    return (1 - x if m & 4 else x, 1 - y if m & 2 else y, 1 - c if m & 1 else c)


_VMEM_SPEC = pl.BlockSpec(memory_space=pltpu.VMEM)
_ANY_SPEC = pl.BlockSpec(memory_space=pl.ANY)


def _allgather8(v, name):
    n = v.shape[1]

    def body(v_ref, out_ref, send_sems, recv_sems, local_sem):
        x, y, c = _place()

        def rows(px, py, pc):
            return out_ref.at[pl.ds(pl.multiple_of((4 * px + 2 * py + pc) * 8, 8), 8), :]

        def copy(m, src, dst, to):
            return pltpu.make_async_remote_copy(src_ref=src, dst_ref=dst, send_sem=send_sems.at[m - 1],
                                                recv_sem=recv_sems.at[m - 1], device_id=to, device_id_type=MESH)

        mine = pltpu.make_async_copy(v_ref, rows(x, y, c), local_sem)
        mine.start()
        sends = [copy(m, v_ref, rows(x, y, c), _flip(x, y, c, m)) for m in range(1, N_DEV)]
        for cp in sends:
            cp.start()
        for m in range(1, N_DEV):
            peer = _flip(x, y, c, m)
            copy(m, v_ref, rows(*peer), peer).wait_recv()
        for cp in sends:
            cp.wait_send()
        mine.wait()

    return pl.pallas_call(
        body, name=name, out_shape=jax.ShapeDtypeStruct((8 * N_DEV, n), f32), in_specs=[_VMEM_SPEC],
        out_specs=_VMEM_SPEC,
        scratch_shapes=[pltpu.SemaphoreType.DMA((N_DEV - 1,)), pltpu.SemaphoreType.DMA((N_DEV - 1,)),
                        pltpu.SemaphoreType.DMA],
    )(v)


def _remote(src, dst, send_sems, recv_sems, k, to):
    return pltpu.make_async_remote_copy(src_ref=src, dst_ref=dst, send_sem=send_sems.at[k], recv_sem=recv_sems.at[k],
                                        device_id=to, device_id_type=MESH)


def _gather_job(blob_ref, out_ref, send_sems, recv_sems, R):
    x, y, c = _place()
    sib = (x, y, 1 - c)
    hr = R // 2

    def half(px, py, pc):
        return out_ref.at[2 * px + py, pl.ds(pl.multiple_of(pc * hr, 16), hr), :]

    my_half = blob_ref.at[pl.ds(pl.multiple_of(c * hr, 16), hr), :]

    def first():
        return [_remote(my_half, half(x, y, c), send_sems, recv_sems, m - 1, _flip(x, y, c, 2 * m))
                for m in range(1, N_CHIPS)]

    def start():
        for cp in first():
            cp.start()

    def finish():
        passed = []
        for m in range(1, N_CHIPS):
            px, py, pc = _flip(x, y, c, 2 * m)
            _remote(my_half, half(px, py, pc), send_sems, recv_sems, m - 1, (px, py, pc)).wait_recv()
            fwd = _remote(half(px, py, pc), half(px, py, pc), send_sems, recv_sems, 2 + m, sib)
            fwd.start()
            passed.append(fwd)
        for m in range(1, N_CHIPS):
            px, py, pc = _flip(x, y, c, 2 * m)
            _remote(my_half, half(px, py, 1 - pc), send_sems, recv_sems, 2 + m, sib).wait_recv()
        for cp in first() + passed:
            cp.wait_send()

    return start, finish


def _to_sibling_job(g_ref, out_ref, send_sems, recv_sems):
    x, y, c = _place()

    def cps():
        return [_remote(g_ref.at[j, 1 - c], out_ref.at[j], send_sems, recv_sems, j, (x, y, 1 - c)) for j in range(N_CHIPS)]

    def start():
        for cp in cps():
            cp.start()

    def wait():
        for cp in cps():
            cp.wait()

    return start, wait


def _to_chips_job(p_ref, out_ref, send_sems, recv_sems):
    x, y, c = _place()

    def cps():
        out = []
        for m in range(1, N_CHIPS):
            px, py, pc = _flip(x, y, c, 2 * m)
            out.append(_remote(p_ref.at[2 * px + py], out_ref.at[m - 1], send_sems, recv_sems, m - 1, (px, py, pc)))
        return out

    def start():
        for cp in cps():
            cp.start()

    def wait():
        for cp in cps():
            cp.wait()

    return start, wait


def _share_job(g_ref, send_sems, recv_sems, R):
    x, y, c = _place()

    def rows(pc):
        return g_ref.at[pl.ds(pl.multiple_of(pc * (R // 2), 8), R // 2), :]

    def start():
        _remote(rows(c), rows(c), send_sems, recv_sems, 0, (x, y, 1 - c)).start()

    def wait():
        _remote(rows(c), rows(1 - c), send_sems, recv_sems, 0, (x, y, 1 - c)).wait_recv()
        _remote(rows(c), rows(c), send_sems, recv_sems, 0, (x, y, 1 - c)).wait_send()

    return start, wait


def _sems(n):
    return [pltpu.SemaphoreType.DMA((n,)), pltpu.SemaphoreType.DMA((n,))]


def _rs_to_sibling(gb):
    def body(g_ref, out_ref, send_sems, recv_sems):
        start, wait = _to_sibling_job(g_ref, out_ref, send_sems, recv_sems)
        start()
        wait()

    return pl.pallas_call(
        body, name="rs_to_sibling", out_shape=jax.ShapeDtypeStruct((N_CHIPS,) + gb.shape[2:], bf16),
        in_specs=[_ANY_SPEC], out_specs=_ANY_SPEC, scratch_shapes=_sems(N_CHIPS),
    )(gb)


def _rs_share(g):
    R = g.shape[0]

    def body(g_ref, out_ref, send_sems, recv_sems):
        start, wait = _share_job(out_ref, send_sems, recv_sems, R)
        start()
        wait()

    return pl.pallas_call(
        body, name="rs_share", out_shape=jax.ShapeDtypeStruct(g.shape, f32), in_specs=[_ANY_SPEC],
        out_specs=_ANY_SPEC, input_output_aliases={0: 0}, scratch_shapes=_sems(1),
    )(g)


RS_TR_MAX = 512


def _rs_sum_pair(place, gb, recv, name):
    hr = gb.shape[2]
    tr = min(hr, RS_TR_MAX)

    def body(pl_ref, g_ref, r_ref, o_ref):
        o_ref[0] = (g_ref[0, 0].astype(f32) + r_ref[0].astype(f32)).astype(bf16)

    return pl.pallas_call(
        body, name=name,
        grid_spec=pltpu.PrefetchScalarGridSpec(
            num_scalar_prefetch=1, grid=(N_CHIPS, hr // tr),
            in_specs=[pl.BlockSpec((1, 1, tr, D_MODEL), lambda j, i, p: (j, p[0], i, 0)),
                      pl.BlockSpec((1, tr, D_MODEL), lambda j, i, p: (j, i, 0))],
            out_specs=pl.BlockSpec((1, tr, D_MODEL), lambda j, i, p: (j, i, 0))),
        out_shape=jax.ShapeDtypeStruct((N_CHIPS, hr, D_MODEL), bf16),
        compiler_params=_cparams("parallel", "parallel"),
    )(place, gb, recv)


def _rs_sum_chips(place, gb, recv_sib, recv_chips, name):
    hr = gb.shape[2]
    tr = min(hr, RS_TR_MAX)
    nt = hr // tr

    def body(pl_ref, g_ref, r1_ref, r2_ref, o_ref):
        acc = g_ref[0, 0].astype(f32) + r1_ref[0].astype(f32)
        for k in range(N_CHIPS - 1):
            acc = acc + r2_ref[k].astype(f32)
        o_ref[...] = acc

    return pl.pallas_call(
        body, name=name,
        grid_spec=pltpu.PrefetchScalarGridSpec(
            num_scalar_prefetch=1, grid=(nt,),
            in_specs=[pl.BlockSpec((1, 1, tr, D_MODEL), lambda i, p: (p[1], p[0], i, 0)),
                      pl.BlockSpec((1, tr, D_MODEL), lambda i, p: (p[1], i, 0)),
                      pl.BlockSpec((N_CHIPS - 1, tr, D_MODEL), lambda i, p: (0, i, 0))],
            out_specs=pl.BlockSpec((tr, D_MODEL), lambda i, p: (p[0] * nt + i, 0))),
        out_shape=jax.ShapeDtypeStruct((2 * hr, D_MODEL), f32),
        compiler_params=_cparams("parallel"),
    )(place, gb, recv_sib, recv_chips)


def _pad128(v):
    v = v.reshape(1, -1)
    return jnp.pad(v, ((0, 0), (0, 128 - v.shape[1])))


def _row(v):
    return v.reshape(1, -1)


W_COLS = PROJ_WIDTH // N_CHIPS


def _g_in_blocks(gz, gxbc, gdt, gq, gkv):
    g = jnp.concatenate([gz, gxbc, gdt[:W_DT_ROWS], gq, gkv], axis=0)
    return jnp.pad(g.reshape(N_CHIPS, W_COLS, D_MODEL), ((0, 0), (0, D_MODEL - W_COLS), (0, 0)))


def kernel(x, c, ln_in_g, ln_in_b, ada_w, ada_b, w_in, conv_w, conv_b, dt_bias, a_log, d_skip, ssd_norm_w, attn_sinks, w_out, ln1_g, ln1_b, w_ff1, b_ff1, w_ff2, b_ff2, ln2_g, ln2_b, loss_target, m_ln_in_g, m_ln_in_b, m_ada_w, m_ada_b, m_w_in, m_conv_w, m_conv_b, m_dt_bias, m_a_log, m_d_skip, m_ssd_norm_w, m_attn_sinks, m_w_out, m_ln1_g, m_ln1_b, m_w_ff1, m_b_ff1, m_w_ff2, m_b_ff2, m_ln2_g, m_ln2_b, v_ln_in_g, v_ln_in_b, v_ada_w, v_ada_b, v_w_in, v_conv_w, v_conv_b, v_dt_bias, v_a_log, v_d_skip, v_ssd_norm_w, v_attn_sinks, v_w_out, v_ln1_g, v_ln1_b, v_w_ff1, v_b_ff1, v_w_ff2, v_b_ff2, v_ln2_g, v_ln2_b):
    xi, yi, ci = _place()
    chip = 2 * xi + yi
    place = jnp.stack([ci, chip]).astype(jnp.int32)
    x2, tgt = x[0], loss_target[0]

    def as_rows(a):
        return jnp.transpose(a, (2, 0, 1))

    def from_rows(a):
        return jnp.transpose(a, (1, 2, 0))

    cond = jnp.concatenate([c.reshape(2, COND_LANES), conv_w.reshape(3, COND_LANES), jnp.zeros((3, COND_LANES), f32)], axis=0)
    ada_b_mine = lax.dynamic_slice(ada_b, (0, chip * ADA_COLS), (1, ADA_COLS))
    blob_in = jnp.pad(w_in[0].T, ((0, D_MODEL - W_COLS), (0, 0))).astype(bf16)
    cond_all, mod_rows, wall_in = _prologue(cond, ada_w[0], ada_b_mine, blob_in)
    cond_all = cond_all.reshape(N_DEV, 8, COND_LANES)
    c_all = cond_all[:, 0:2].reshape(N_DEV, D_MODEL)
    conv_w_full = jnp.concatenate([cond_all[2 * j, 2:5].reshape(CONV_K, 384) for j in range(N_CHIPS)], axis=1)
    mod = jnp.concatenate([mod_rows[:, 0].reshape(6, D_MODEL), jnp.zeros((2, D_MODEL), f32)], axis=0)
    w_in_f = wall_in[:, :W_COLS].reshape(PROJ_WIDTH, D_MODEL)
    b_ff1w, b_ff2w, b_outw = w_ff1[0].astype(bf16), w_ff2[0].astype(bf16), w_out[0].astype(bf16)

    def with_mine(wall, mine):
        return lax.dynamic_update_slice(wall, mine[None], (chip, 0, 0))

    e_mat = _head_expand()
    dsk_x = jnp.repeat(d_skip[0], HEAD_DIM).reshape(1, SSD_WIDTH)
    dtb, alog = _pad128(dt_bias), _pad128(a_log)
    sinks = attn_sinks[0]
    lng, lnb = _row(ln_in_g), _row(ln_in_b)
    u1, z, xr, xc, q, kv, dtr, wall_ff1 = _inproj_fwd(x2, mod, lng, lnb, w_in_f, conv_w_full, conv_b, b_ff1w)
    y, prev_all, wall_out = _ssd_fwd(xc, dtr, dtb, alog, dsk_x, b_outw)
    o, lse, wall_ff2 = _attn_fwd(q, kv, sinks, b_ff2w)
    wall_ff1, wall_ff2, wall_out = with_mine(wall_ff1, b_ff1w), with_mine(wall_ff2, b_ff2w), with_mine(wall_out, b_outw)
    yn, mix, r1 = _outproj_fwd(y, z, o, x2, mod, lng, lnb, ssd_norm_w, wall_out)

    dr1, u2, s_act, da, df, acc_mlp, db1 = _mlp_fwd_bwd(r1, tgt, mod, ln1_g, ln1_b, ln2_g, ln2_b, wall_ff1, b_ff1, wall_ff2,
                                                        b_ff2)
    ga = jnp.zeros((N_CHIPS, GA_ROWS, D_MODEL), bf16)
    ga = _wgrad_blob(ga, u2, da, "wgrad_ff1", lambda t, n: (n, t))
    ga = _wgrad_blob(ga, s_act, df, "wgrad_ff2", lambda t, n: (t // 2, 2 + t % 2))
    ga = ga.reshape(N_CHIPS, 2, GA_ROWS // 2, D_MODEL)
    dy, dz, do, dmix, acc_out, a_sib = _outproj_bwd(dr1, mix, y, z, mod, ssd_norm_w, wall_out, ga)
    gc = jnp.zeros((N_CHIPS, GC_ROWS, D_MODEL), bf16)
    gc = _wgrad_blob(gc, yn, dmix, "wgrad_out_y", lambda t, n: (t, 0))
    gc = _wgrad_blob(gc, o, dmix, "wgrad_out_o", lambda t, n: (2 + t, 0))
    gc = gc.reshape(N_CHIPS, 2, GC_ROWS // 2, D_MODEL)
    a_pair = _rs_sum_pair(place, ga, a_sib, "rs_sum_pair_a")
    dq, dkv, dsink, a_chips, c_sib = _attn_bwd(q, kv, do, lse, sinks, a_pair, gc)
    g_a = _rs_sum_chips(place, ga, a_sib, a_chips, "rs_sum_chips_a")
    c_pair = _rs_sum_pair(place, gc, c_sib, "rs_sum_pair_c")
    dxc, ddt, acc_ssd, dd_x, g_a, c_chips = _ssd_bwd(xc, dtr, dy, prev_all, dtb, alog, dsk_x, e_mat, g_a, c_pair)
    g_c = _rs_sum_chips(place, gc, c_sib, c_chips, "rs_sum_chips_c")
    dxr, acc_conv, g_c = _conv_bwd(dxc, xr, conv_w_full, conv_b, g_c)
    gb = _g_in_blocks(_wgrad(dz, u1, "wgrad_in_z"), _wgrad(dxr, u1, "wgrad_in_xbc"),
                      _wgrad(ddt.astype(bf16), u1, "wgrad_in_dt"), _wgrad(dq, u1, "wgrad_in_q"),
                      _wgrad(dkv, u1, "wgrad_in_kv")).reshape(N_CHIPS, 2, GB_ROWS // 2, D_MODEL)
    b_sib = _rs_to_sibling(gb)
    b_pair = _rs_sum_pair(place, gb, b_sib, "rs_sum_pair_b")
    grad_x, acc_in, b_chips = _inproj_bwd(dz, dxr, dq, dkv, ddt, dr1, x2, mod, lng, lnb, w_in_f, b_pair)
    g_b = _rs_share(_rs_sum_chips(place, gb, b_sib, b_chips, "rs_sum_chips_b"))

    packed = _small_pack(acc_in, acc_out, acc_mlp, db1, acc_conv, acc_ssd, dd_x, dsink, alog, e_mat)
    small_all = _allgather8(packed.reshape(8, SMALL_N // 8), "gather_small").reshape(N_DEV, SMALL_N)
    given = dict(ada_b=(ada_b, m_ada_b, v_ada_b), ln_in_g=(ln_in_g, m_ln_in_g, v_ln_in_g), ln_in_b=(ln_in_b, m_ln_in_b, v_ln_in_b),
                 conv_b=(conv_b, m_conv_b, v_conv_b), dt_bias=(dt_bias, m_dt_bias, v_dt_bias), a_log=(a_log, m_a_log, v_a_log),
                 d_skip=(d_skip, m_d_skip, v_d_skip), ssd_norm_w=(ssd_norm_w, m_ssd_norm_w, v_ssd_norm_w),
                 attn_sinks=(attn_sinks, m_attn_sinks, v_attn_sinks), ln1_g=(ln1_g, m_ln1_g, v_ln1_g),
                 ln1_b=(ln1_b, m_ln1_b, v_ln1_b), b_ff1=(b_ff1, m_b_ff1, v_b_ff1), b_ff2=(b_ff2, m_b_ff2, v_b_ff2),
                 ln2_g=(ln2_g, m_ln2_g, v_ln2_g), ln2_b=(ln2_b, m_ln2_b, v_ln2_b))
    upd = _small_update(small_all, *([_row(given[n][i]) for n in SMALL_PARAMS] for i in range(3)))
    small_res = {n: [t.reshape(given[n][0].shape) for t in upd[4 * j:4 * j + 4]] for j, n in enumerate(SMALL_PARAMS)}
    g_conv_all, loss_lanes = upd[4 * len(SMALL_PARAMS)], upd[4 * len(SMALL_PARAMS) + 1]
    loss = jnp.sum(loss_lanes)

    dmod_mine = lax.dynamic_slice(small_all[:, :6 * D_MODEL], (0, chip * ADA_COLS), (N_DEV, ADA_COLS))
    big = {"ada_w": [t[None] for t in _ada_bwd(c_all, dmod_mine, ada_w[0], m_ada_w[0], v_ada_w[0])]}

    g_conv = lax.dynamic_slice(g_conv_all.reshape(CONV_K, CONV_DIM), (0, chip * 384), (CONV_K, 384))
    big["conv_w"] = [t[None] for t in (g_conv, *_adamw(conv_w[0], g_conv, m_conv_w[0], v_conv_w[0], "adamw_conv_w"))]

    g_rows = g_b[:W_COLS].reshape(W_COLS, 1, D_MODEL)
    big["w_in"] = [from_rows(t) for t in (g_rows, *_adamw_rows(as_rows(w_in), g_rows, as_rows(m_w_in), as_rows(v_w_in)))]
    for name, g, (w, m, v) in (("w_out", g_c, (w_out, m_w_out, v_w_out)),
                               ("w_ff1", g_a[:D_MODEL], (w_ff1, m_w_ff1, v_w_ff1)),
                               ("w_ff2", g_a[D_MODEL:GA_ROWS], (w_ff2, m_w_ff2, v_w_ff2))):
        big[name] = [t[None] for t in (g, *_adamw(w[0], g, m[0], v[0], "adamw_" + name))]

    order = ("ln_in_g", "ln_in_b", "ada_w", "ada_b", "w_in", "conv_w", "conv_b", "dt_bias", "a_log", "d_skip", "ssd_norm_w",
             "attn_sinks", "w_out", "ln1_g", "ln1_b", "w_ff1", "b_ff1", "w_ff2", "b_ff2", "ln2_g", "ln2_b")
    res = {**small_res, **big}
    return (loss, grad_x[None], *[res[n][k] for k in range(4) for n in order])
```

```python
import functools
import math

import numpy as np
import jax
import jax.numpy as jnp
from jax import lax
from jax.experimental import pallas as pl
from jax.experimental.pallas import tpu as pltpu

f32 = jnp.float32
bf16 = jnp.bfloat16

D_MODEL = 1024
SSD_WIDTH = 1024
SSD_HEADS = 16
HEAD_DIM = 64
SSD_STATE = 128
SSD_GROUPS = 2
CHUNK = 128
CONV_K = 4
CONV_DIM = 1536
ATTN_HEADS = 16
D_FF = 4096
PROJ_WIDTH = 3856
ALPHA = 2.0 ** 0.25
LN_EPS = 1e-5
RMS_EPS = 1e-5
ATTN_SCALE = HEAD_DIM ** -0.5
NEG = -1e30

ADAM_LR = 0.001
ADAM_B1 = 0.9
ADAM_B2 = 0.999
ADAM_EPS = 1e-08
ADAM_WD = 0.01
ADAM_STEP = 10

W_Z, W_XBC, W_Q, W_KV = slice(0, 1024), slice(1024, 2560), slice(2576, 3600), slice(3600, 3856)
W_DT = slice(2560, 2688)
W_DT_ROWS = 16
GA_ROWS = 2048
GB_ROWS = 1024
GC_ROWS = 512
WG_TM = 512
STAGE_ROWS = 512
ADAMW_BLOCK_ELEMS = 1 << 18
MLP_TM = 256
DENSE_TM = 512
N_CHIPS = 4
N_DEV = 8
VMEM_LIMIT = 56 * 1024 * 1024
MESH = pl.DeviceIdType.MESH

ALIBI_SLOPES = tuple(2.0 ** (-8.0 / ATTN_HEADS * (i + 1)) for i in range(ATTN_HEADS))


def _cparams(*sem):
    return pltpu.CompilerParams(dimension_semantics=sem, vmem_limit_bytes=VMEM_LIMIT)


def _sigmoid(x):
    return 1.0 / (1.0 + jnp.exp(-x))


def _softplus(x):
    return jnp.maximum(x, 0.0) + jnp.log1p(jnp.exp(-jnp.abs(x)))


def _ln_stats(x):
    mu = jnp.mean(x, axis=-1, keepdims=True)
    xc = x - mu
    var = jnp.mean(xc * xc, axis=-1, keepdims=True)
    rstd = lax.rsqrt(var + LN_EPS)
    return xc * rstd, rstd


def _ln_bwd(dy, xhat, rstd, g):
    dxh = dy * g
    m1 = jnp.mean(dxh, axis=-1, keepdims=True)
    m2 = jnp.mean(dxh * xhat, axis=-1, keepdims=True)
    return rstd * (dxh - m1 - xhat * m2)


def _dot(a, b):
    return jnp.dot(a, b, preferred_element_type=f32)


def _dot_nt(a, b):
    return lax.dot_general(a, b, (((1,), (1,)), ((), ())), preferred_element_type=f32)


def _dot_tn(a, b):
    return lax.dot_general(a, b, (((0,), (0,)), ((), ())), preferred_element_type=f32)


def _dot_exact(a, b):
    return jnp.dot(a, b, preferred_element_type=f32, precision=lax.Precision.HIGHEST)


def _split3(v):
    hi = v.astype(bf16)
    r1 = v - hi.astype(f32)
    mid = r1.astype(bf16)
    lo = (r1 - mid.astype(f32)).astype(bf16)
    return hi, mid, lo


def _sel_dot(sel, v):
    hi, mid, lo = _split3(v)
    return _dot(sel, hi) + _dot(sel, mid) + _dot(sel, lo)


def _dot_sel_nt(v, sel):
    hi = v.astype(bf16)
    mid = (v - hi.astype(f32)).astype(bf16)
    return _dot_nt(hi, sel) + _dot_nt(mid, sel)


def _expand_heads(v):
    lane = lax.broadcasted_iota(jnp.int32, (v.shape[0], 128), 1)
    blocks = [jnp.where(lane < HEAD_DIM, v[:, 2 * b:2 * b + 1], v[:, 2 * b + 1:2 * b + 2]) for b in range(SSD_HEADS // 2)]
    return jnp.concatenate(blocks, axis=1)


def _full(shape):
    nd = len(shape)
    return pl.BlockSpec(shape, lambda *_: (0,) * nd)


def _resident(shape):
    nd = len(shape)
    return pl.BlockSpec(shape, lambda *_: (0,) * nd, pipeline_mode=pl.Buffered(1))


def _rows(tm, n):
    return pl.BlockSpec((tm, n), lambda i: (i, 0))


def _inproj_fwd(x, mod, ln_g, ln_b, w_in_t, conv_w, conv_b, blob):
    L = x.shape[0]
    tm = DENSE_TM
    nt = L // tm
    R = blob.shape[0]

    def body(x_ref, mod_ref, g_ref, b_ref, w_ref, cw_ref, cb_ref, blob_ref,
             u1_ref, z_ref, xr_ref, xc_ref, q_ref, kv_ref, dt_ref, wall_ref, halo, buf, send_sems, recv_sems):
        start, finish = _gather_job(blob_ref, wall_ref, send_sems, recv_sems, R)

        @pl.when(pl.program_id(0) == 0)
        def _():
            halo[...] = jnp.zeros_like(halo)
            start()

        xhat, _ = _ln_stats(x_ref[...])
        h0 = xhat * g_ref[...] + b_ref[...]
        u1 = (h0 * (1.0 + mod_ref[1:2, :]) + mod_ref[0:1, :]).astype(bf16)
        u1_ref[...] = u1
        z_ref[...] = _dot_nt(u1, w_ref[W_Z, :])
        xr = _dot_nt(u1, w_ref[W_XBC, :])
        xr_ref[...] = xr
        q_ref[...] = _dot_nt(u1, w_ref[W_Q, :]).astype(bf16)
        kv_ref[...] = _dot_nt(u1, w_ref[W_KV, :]).astype(bf16)
        dt_ref[...] = _dot_nt(u1, w_ref[W_DT, :])
        buf[0:8, :] = halo[...]
        buf[8:8 + tm, :] = xr
        pre = cb_ref[...] + cw_ref[0:1, :] * buf[5:5 + tm, :]
        for k in range(1, CONV_K):
            pre = pre + cw_ref[k:k + 1, :] * buf[5 + k:5 + k + tm, :]
        xc_ref[...] = pre * _sigmoid(pre)
        halo[...] = xr[tm - 8:tm, :]

        @pl.when(pl.program_id(0) == nt - 1)
        def _():
            finish()

    return pl.pallas_call(
        body, name="inproj_fwd", grid=(nt,),
        in_specs=[_rows(tm, D_MODEL), _full((8, D_MODEL)), _full((1, D_MODEL)), _full((1, D_MODEL)),
                  _resident((PROJ_WIDTH, D_MODEL)), _full((CONV_K, CONV_DIM)), _full((1, CONV_DIM)), _ANY_SPEC],
        out_specs=[_rows(tm, D_MODEL), _rows(tm, D_MODEL), _rows(tm, CONV_DIM), _rows(tm, CONV_DIM),
                   _rows(tm, D_MODEL), _rows(tm, 256), _rows(tm, 128), _ANY_SPEC],
        out_shape=[jax.ShapeDtypeStruct((L, D_MODEL), bf16), jax.ShapeDtypeStruct((L, D_MODEL), f32),
                   jax.ShapeDtypeStruct((L, CONV_DIM), f32), jax.ShapeDtypeStruct((L, CONV_DIM), f32),
                   jax.ShapeDtypeStruct((L, D_MODEL), bf16), jax.ShapeDtypeStruct((L, 256), bf16),
                   jax.ShapeDtypeStruct((L, 128), f32), jax.ShapeDtypeStruct((N_CHIPS, R, D_MODEL), bf16)],
        scratch_shapes=[pltpu.VMEM((8, CONV_DIM), f32), pltpu.VMEM((tm + 8, CONV_DIM), f32),
                        pltpu.SemaphoreType.DMA((6,)), pltpu.SemaphoreType.DMA((6,))],
        compiler_params=_cparams("arbitrary"),
    )(x, mod, ln_g, ln_b, w_in_t, conv_w, conv_b, blob)


def _head_expand():
    e = np.zeros((128, SSD_WIDTH), np.float32)
    for h in range(SSD_HEADS):
        e[h, h * HEAD_DIM:(h + 1) * HEAD_DIM] = 1.0
    return jnp.asarray(e, dtype=bf16)


def _ssd_chunk_common(dt_raw, dtb, a_row):
    T = CHUNK
    lane = lax.broadcasted_iota(jnp.int32, (T, 128), 1)
    dt = jnp.where(lane < SSD_HEADS, _softplus(dt_raw + dtb), 0.0)
    a = dt * a_row
    r = lax.broadcasted_iota(jnp.int32, (T, T), 0)
    c = lax.broadcasted_iota(jnp.int32, (T, T), 1)
    tril = (c <= r).astype(bf16)
    cum = _sel_dot(tril, a)
    dtx = _expand_heads(dt)
    cumx = _expand_heads(cum)
    return dt, a, cum, dtx, cumx, r, c


def _ssd_fwd(xc, dt_raw, dt_bias, a_log, d_skip_x, blob):
    L = xc.shape[0]
    nc = L // CHUNK
    T = CHUNK
    R = blob.shape[0]

    def body(xc_ref, dt_ref, dtb_ref, al_ref, dsk_ref, blob_ref, y_ref, prev_ref, wall_ref, st, send_sems, recv_sems):
        start, finish = _gather_job(blob_ref, wall_ref, send_sems, recv_sems, R)

        @pl.when(pl.program_id(0) == 0)
        def _():
            st[...] = jnp.zeros_like(st)
            start()

        @pl.when(pl.program_id(0) == nc - 1)
        def _():
            finish()

        a_row = -jnp.exp(al_ref[...])
        lane1 = lax.broadcasted_iota(jnp.int32, (1, 128), 1)
        a_row = jnp.where(lane1 < SSD_HEADS, a_row, 0.0)
        dt, a, cum, dtx, cumx, r, c = _ssd_chunk_common(dt_ref[...], dtb_ref[...], a_row)
        cum_t = cum.T
        ex = jnp.exp(cumx)
        last = cumx[T - 1:T, :]
        wx = jnp.exp(last - cumx)
        cdx = jnp.exp(last)
        xs = xc_ref[:, 0:SSD_WIDTH]
        X = xs * dtx
        Xb = X.astype(bf16)
        Xd = (X * wx).astype(bf16)
        prev = st[...]
        prev_ref[0] = prev
        prevb = prev.astype(bf16)
        tri = c <= r
        lane = lax.broadcasted_iota(jnp.int32, (T, 128), 1)
        y_blocks = []
        new_states = []
        for g in range(SSD_GROUPS):
            Bg = xc_ref[:, 1024 + 128 * g:1152 + 128 * g].astype(bf16)
            Cg = xc_ref[:, 1280 + 128 * g:1408 + 128 * g].astype(bf16)
            G = _dot_nt(Cg, Bg)
            yoff = _dot(Cg, prevb[:, 512 * g:512 * (g + 1)])
            new_states.append(_dot_tn(Bg, Xd[:, 512 * g:512 * (g + 1)]))
            for j in range(4):
                blk = 4 * g + j
                Xblk = Xb[:, 128 * blk:128 * (blk + 1)]
                ys = []
                for half in range(2):
                    h = 2 * blk + half
                    seg = jnp.minimum(cum[:, h:h + 1] - cum_t[h:h + 1, :], 0.0)
                    M = jnp.where(tri, G * jnp.exp(seg), 0.0).astype(bf16)
                    ys.append(_dot(M, Xblk))
                yd = jnp.where(lane < HEAD_DIM, ys[0], ys[1])
                sl = slice(128 * blk, 128 * (blk + 1))
                y_blocks.append(yd + ex[:, sl] * yoff[:, 128 * j:128 * (j + 1)] + dsk_ref[:, sl] * xs[:, sl])
        y_ref[...] = jnp.concatenate(y_blocks, axis=1)
        st[...] = prev * cdx + jnp.concatenate(new_states, axis=1)

    return pl.pallas_call(
        body, name="ssd_fwd", grid=(nc,),
        in_specs=[_rows(T, CONV_DIM), _rows(T, 128), _full((1, 128)), _full((1, 128)), _full((1, SSD_WIDTH)), _ANY_SPEC],
        out_specs=[_rows(T, SSD_WIDTH), pl.BlockSpec((1, SSD_STATE, SSD_WIDTH), lambda i: (i, 0, 0)), _ANY_SPEC],
        out_shape=[jax.ShapeDtypeStruct((L, SSD_WIDTH), f32), jax.ShapeDtypeStruct((nc, SSD_STATE, SSD_WIDTH), f32),
                   jax.ShapeDtypeStruct((N_CHIPS, R, D_MODEL), bf16)],
        scratch_shapes=[pltpu.VMEM((SSD_STATE, SSD_WIDTH), f32)] + _sems(6),
        compiler_params=_cparams("arbitrary"),
    )(xc, dt_raw, dt_bias, a_log, d_skip_x, blob)


def _kv_halves(kv_prev, kv_cur, first):
    kv = jnp.concatenate([jnp.where(first, 0.0, kv_prev.astype(f32)), kv_cur.astype(f32)], axis=0)
    lane = lax.broadcasted_iota(jnp.int32, (2 * CHUNK, 128), 1)
    lo = lane < HEAD_DIM
    out = []
    for g in range(2):
        per_half = []
        for half in range(2):
            both = []
            for t in (kv[:, 0:128], kv[:, 128:256]):
                src = t if g == half else pltpu.roll(t, HEAD_DIM, 1)
                both.append(jnp.where(lo if half == 0 else ~lo, src, 0.0).astype(bf16))
            per_half.append(tuple(both))
        out.append(per_half)
    return out


def _attn_masks(first):
    r = lax.broadcasted_iota(jnp.int32, (CHUNK, 2 * CHUNK), 0)
    c = lax.broadcasted_iota(jnp.int32, (CHUNK, 2 * CHUNK), 1)
    dist = r + CHUNK - c
    valid = (dist >= 0) & (dist < CHUNK) & ((c >= CHUNK) | jnp.logical_not(first))
    return dist.astype(f32), valid


def _head_stack(g, half, sink_ref):
    blks = [4 * g + i for i in range(4)]
    heads = [2 * b + half for b in blks]
    slope = jnp.concatenate([jnp.full((CHUNK, 1), ALIBI_SLOPES[h], f32) for h in heads], axis=0)
    sink = jnp.concatenate([jnp.full((CHUNK, 1), sink_ref[h], f32) for h in heads], axis=0)
    return blks, heads, slope, sink


def _attn_fwd(q, kv, sinks, blob):
    L = q.shape[0]
    nb = L // CHUNK
    T = CHUNK
    R = blob.shape[0]

    def body(sink_ref, q_ref, kvp_ref, kvc_ref, blob_ref, o_ref, lse_ref, wall_ref, send_sems, recv_sems):
        first = pl.program_id(0) == 0
        start, finish = _gather_job(blob_ref, wall_ref, send_sems, recv_sems, R)

        @pl.when(first)
        def _():
            start()

        @pl.when(pl.program_id(0) == nb - 1)
        def _():
            finish()

        ext = _kv_halves(kvp_ref[...], kvc_ref[...], first)
        dist, valid = _attn_masks(first)
        lane = lax.broadcasted_iota(jnp.int32, (T, 128), 1)
        lse = jnp.zeros((T, 128), f32)
        o_blocks = []
        for blk in range(8):
            qb = q_ref[:, 128 * blk:128 * (blk + 1)]
            acc = None
            for half in range(2):
                h = 2 * blk + half
                k_ext, v_ext = ext[h // 8][half]
                s = _dot_nt(qb, k_ext) * ATTN_SCALE - ALIBI_SLOPES[h] * dist
                s = jnp.where(valid, s, NEG)
                sink = sink_ref[h]
                m = jnp.maximum(jnp.max(s, axis=-1, keepdims=True), sink)
                p = jnp.exp(s - m)
                den = jnp.sum(p, axis=-1, keepdims=True) + jnp.exp(sink - m)
                pn = (p * (1.0 / den)).astype(bf16)
                oh = _dot(pn, v_ext)
                acc = oh if acc is None else acc + oh
                lse = jnp.where(lane == h, m + jnp.log(den), lse)
            o_blocks.append(acc.astype(bf16))
        o_ref[...] = jnp.concatenate(o_blocks, axis=1)
        lse_ref[...] = lse

    return pl.pallas_call(
        body, name="attn_fwd", grid=(nb,),
        in_specs=[pl.BlockSpec(memory_space=pltpu.SMEM), _rows(T, D_MODEL),
                  pl.BlockSpec((T, 256), lambda i: (jnp.maximum(i - 1, 0), 0)), _rows(T, 256), _ANY_SPEC],
        out_specs=[_rows(T, D_MODEL), _rows(T, 128), _ANY_SPEC],
        out_shape=[jax.ShapeDtypeStruct((L, D_MODEL), bf16), jax.ShapeDtypeStruct((L, 128), f32),
                   jax.ShapeDtypeStruct((N_CHIPS, R, D_MODEL), bf16)],
        scratch_shapes=_sems(6),
        compiler_params=_cparams("arbitrary"),
    )(sinks, q, kv, kv, blob)


def _gated_norm(y, z, w):
    sz = _sigmoid(z)
    hg = y * (z * sz)
    ns, rss = [], []
    for g in range(SSD_GROUPS):
        hs = hg[:, 512 * g:512 * (g + 1)]
        rs = lax.rsqrt(jnp.mean(hs * hs, axis=-1, keepdims=True) + RMS_EPS)
        ns.append(hs * rs)
        rss.append(rs)
    n = jnp.concatenate(ns, axis=1)
    return n * w, n, rss, sz


def _outproj_fwd(y, z, o, x, mod, ln_g, ln_b, norm_w, w_out):
    L = x.shape[0]
    tm = DENSE_TM

    def body(y_ref, z_ref, o_ref, x_ref, mod_ref, g_ref, b_ref, nw_ref, w_ref, yn_ref, mix_ref, r1_ref):
        yn, _, _, _ = _gated_norm(y_ref[...], z_ref[...], nw_ref[...])
        ynb = yn.astype(bf16)
        yn_ref[...] = ynb
        mix = (_dot(ynb[:, 0:512], w_ref[0]) + _dot(ynb[:, 512:1024], w_ref[1])
               + _dot(o_ref[:, 0:512], w_ref[2]) + _dot(o_ref[:, 512:1024], w_ref[3]))
        mix_ref[...] = mix
        xhat, _ = _ln_stats(x_ref[...])
        h0 = xhat * g_ref[...] + b_ref[...]
        r1_ref[...] = ALPHA * h0 + (1.0 + mod_ref[2:3, :]) * mix

    v = _full((1, D_MODEL))
    return pl.pallas_call(
        body, name="outproj_fwd", grid=(L // tm,),
        in_specs=[_rows(tm, D_MODEL), _rows(tm, D_MODEL), _rows(tm, D_MODEL), _rows(tm, D_MODEL),
                  _full((8, D_MODEL)), v, v, v, _resident((N_CHIPS, 512, D_MODEL))],
        out_specs=[_rows(tm, D_MODEL)] * 3,
        out_shape=[jax.ShapeDtypeStruct((L, D_MODEL), bf16), jax.ShapeDtypeStruct((L, D_MODEL), f32),
                   jax.ShapeDtypeStruct((L, D_MODEL), f32)],
        compiler_params=_cparams("parallel"),
    )(y, z, o, x, mod, ln_g, ln_b, norm_w, w_out)


A_LN2G, A_LN2B, A_G2, A_B2, A_SC2, A_SH2, A_LN1G, A_LN1B, A_LOSS = range(9)


def _mlp_fwd_bwd(r1, target, mod, ln1_g, ln1_b, ln2_g, ln2_b, w1, b1, w2, b2):
    L = r1.shape[0]
    tm = MLP_TM
    nj = D_FF // 1024

    def body(r1_ref, t_ref, mod_ref, g1_ref, bb1_ref, g2_ref, bb2_ref, w1_ref, b1_ref, w2_ref, b2_ref,
             dr1_ref, u2_ref, s_ref, da_ref, df_ref, acc_ref, db1_ref, hr):
        @pl.when(pl.program_id(0) == 0)
        def _():
            acc_ref[...] = jnp.zeros_like(acc_ref)
            db1_ref[...] = jnp.zeros_like(db1_ref)

        sc2, sh2, gate2 = mod_ref[4:5, :], mod_ref[3:4, :], mod_ref[5:6, :]
        xhat1, rstd1 = _ln_stats(r1_ref[...])
        h1 = xhat1 * g1_ref[...] + bb1_ref[...]
        u2f = h1 * (1.0 + sc2) + sh2
        u2 = u2f.astype(bf16)
        u2_ref[...] = u2
        f = jnp.zeros((tm, D_MODEL), f32) + b2_ref[...]
        for j in range(nj):
            cs = slice(1024 * j, 1024 * (j + 1))
            a = _dot(u2, w1_ref[j]) + b1_ref[:, cs]
            hrj = jnp.maximum(a, 0.0)
            hr[:, cs] = hrj.astype(bf16)
            sj = (hrj * hrj).astype(bf16)
            s_ref[:, cs] = sj
            f = f + _dot(sj, w2_ref[j])
        r2 = ALPHA * h1 + (1.0 + gate2) * f
        xhat2, rstd2 = _ln_stats(r2)
        h2 = xhat2 * g2_ref[...] + bb2_ref[...]
        diff = h2 - t_ref[...]
        dh2 = diff * (1.0 / D_MODEL)

        def add(row, val):
            acc_ref[row:row + 1, :] += jnp.sum(val, axis=0, keepdims=True)

        add(A_LOSS, diff * diff * (0.5 / D_MODEL))
        add(A_LN2G, dh2 * xhat2)
        add(A_LN2B, dh2)
        dr2 = _ln_bwd(dh2, xhat2, rstd2, g2_ref[...])
        add(A_G2, dr2 * f)
        df = dr2 * (1.0 + gate2)
        add(A_B2, df)
        dfb = df.astype(bf16)
        df_ref[...] = dfb
        du2 = jnp.zeros((tm, D_MODEL), f32)
        for j in range(nj):
            cs = slice(1024 * j, 1024 * (j + 1))
            ds = _dot_nt(dfb, w2_ref[j])
            daj = ds * (2.0 * hr[:, cs].astype(f32))
            db1_ref[:, cs] += jnp.sum(daj, axis=0, keepdims=True)
            dajb = daj.astype(bf16)
            da_ref[:, cs] = dajb
            du2 = du2 + _dot_nt(dajb, w1_ref[j])
        add(A_SC2, du2 * h1)
        add(A_SH2, du2)
        dh1 = ALPHA * dr2 + du2 * (1.0 + sc2)
        add(A_LN1G, dh1 * xhat1)
        add(A_LN1B, dh1)
        dr1_ref[...] = _ln_bwd(dh1, xhat1, rstd1, g1_ref[...])

    v = _full((1, D_MODEL))
    return pl.pallas_call(
        body, name="mlp_fwd_bwd", grid=(L // tm,),
        in_specs=[_rows(tm, D_MODEL), _rows(tm, D_MODEL), _full((8, D_MODEL)), v, v, v, v,
                  _resident((N_CHIPS, D_MODEL, D_MODEL)), _full((1, D_FF)), _resident((N_CHIPS, D_MODEL, D_MODEL)), v],
        out_specs=[_rows(tm, D_MODEL), _rows(tm, D_MODEL), _rows(tm, D_FF), _rows(tm, D_FF), _rows(tm, D_MODEL),
                   _full((16, D_MODEL)), _full((1, D_FF))],
        out_shape=[jax.ShapeDtypeStruct((L, D_MODEL), f32), jax.ShapeDtypeStruct((L, D_MODEL), bf16),
                   jax.ShapeDtypeStruct((L, D_FF), bf16), jax.ShapeDtypeStruct((L, D_FF), bf16),
                   jax.ShapeDtypeStruct((L, D_MODEL), bf16), jax.ShapeDtypeStruct((16, D_MODEL), f32),
                   jax.ShapeDtypeStruct((1, D_FF), f32)],
        scratch_shapes=[pltpu.VMEM((tm, D_FF), bf16)],
        compiler_params=_cparams("arbitrary"),
    )(r1, target, mod, ln1_g, ln1_b, ln2_g, ln2_b, w1, b1, w2, b2)


def _wgrad(a, b, name):
    L, M = a.shape
    N = b.shape[1]
    tm = min(M, 512)
    tn = next(t for t in (1024, 768, 512, 256, 128) if N % t == 0)

    def body(a_ref, b_ref, o_ref):
        o_ref[...] = _dot_tn(a_ref[...], b_ref[...]).astype(bf16)

    return pl.pallas_call(
        body, name=name, grid=(M // tm, N // tn),
        in_specs=[pl.BlockSpec((L, tm), lambda i, j: (0, i)), pl.BlockSpec((L, tn), lambda i, j: (0, j))],
        out_specs=pl.BlockSpec((tm, tn), lambda i, j: (i, j)),
        out_shape=jax.ShapeDtypeStruct((M, N), bf16),
        compiler_params=_cparams("parallel", "parallel"),
    )(a, b)


def _wgrad_blob(blob, a, b, name, place_of):
    L, M = a.shape
    N = b.shape[1]

    def body(blob_ref, a_ref, b_ref, o_ref):
        o_ref[0] = _dot_tn(a_ref[...], b_ref[...]).astype(bf16)

    return pl.pallas_call(
        body, name=name, grid=(M // WG_TM, N // D_MODEL),
        in_specs=[pl.BlockSpec(memory_space=pl.ANY), pl.BlockSpec((L, WG_TM), lambda t, n: (0, t)),
                  pl.BlockSpec((L, D_MODEL), lambda t, n: (0, n))],
        out_specs=pl.BlockSpec((1, WG_TM, D_MODEL), lambda t, n: (*place_of(t, n), 0)),
        out_shape=jax.ShapeDtypeStruct(blob.shape, bf16), input_output_aliases={0: 0},
        compiler_params=_cparams("parallel", "parallel"),
    )(blob, a, b)


def _outproj_bwd(dr1, mix, y, z, mod, norm_w, w_out, gb):
    L = dr1.shape[0]
    tm = DENSE_TM
    nt = L // tm

    def body(dr1_ref, mix_ref, y_ref, z_ref, mod_ref, nw_ref, w_ref, gb_ref,
             dy_ref, dz_ref, do_ref, dmix_ref, acc_ref, sib_ref, send_sems, recv_sems):
        start, wait = _to_sibling_job(gb_ref, sib_ref, send_sems, recv_sems)

        @pl.when(pl.program_id(0) == 0)
        def _():
            acc_ref[...] = jnp.zeros_like(acc_ref)
            start()

        @pl.when(pl.program_id(0) == nt - 1)
        def _():
            wait()

        dr1 = dr1_ref[...]
        acc_ref[0:1, :] += jnp.sum(dr1 * mix_ref[...], axis=0, keepdims=True)
        dmix = (dr1 * (1.0 + mod_ref[2:3, :])).astype(bf16)
        dmix_ref[...] = dmix
        dyn = jnp.concatenate([_dot_nt(dmix, w_ref[0]), _dot_nt(dmix, w_ref[1])], axis=1)
        do_ref[...] = jnp.concatenate([_dot_nt(dmix, w_ref[2]), _dot_nt(dmix, w_ref[3])], axis=1).astype(bf16)
        yv, zv = y_ref[...], z_ref[...]
        _, n, rss, sz = _gated_norm(yv, zv, nw_ref[...])
        acc_ref[1:2, :] += jnp.sum(dyn * n, axis=0, keepdims=True)
        dn = dyn * nw_ref[...]
        parts = []
        for g in range(SSD_GROUPS):
            sl = slice(512 * g, 512 * (g + 1))
            dng, ng = dn[:, sl], n[:, sl]
            parts.append(rss[g] * (dng - ng * jnp.mean(dng * ng, axis=-1, keepdims=True)))
        dhg = jnp.concatenate(parts, axis=1)
        dy_ref[...] = dhg * (zv * sz)
        dz_ref[...] = (dhg * yv * (sz * (1.0 + zv * (1.0 - sz)))).astype(bf16)

    return pl.pallas_call(
        body, name="outproj_bwd", grid=(nt,),
        in_specs=[_rows(tm, D_MODEL)] * 4 + [_full((8, D_MODEL)), _full((1, D_MODEL)), _resident((N_CHIPS, 512, D_MODEL)),
                  _ANY_SPEC],
        out_specs=[_rows(tm, D_MODEL)] * 4 + [_full((8, D_MODEL)), _ANY_SPEC],
        out_shape=[jax.ShapeDtypeStruct((L, D_MODEL), f32)] + [jax.ShapeDtypeStruct((L, D_MODEL), bf16)] * 3
        + [jax.ShapeDtypeStruct((8, D_MODEL), f32), jax.ShapeDtypeStruct((N_CHIPS,) + gb.shape[2:], bf16)],
        scratch_shapes=_sems(N_CHIPS),
        compiler_params=_cparams("arbitrary"),
    )(dr1, mix, y, z, mod, norm_w, w_out, gb)


def _attn_bwd(q, kv, do, lse, sinks, pb, gb2):
    L = q.shape[0]
    nb = L // CHUNK
    T = CHUNK

    def body(sink_ref, q_ref, kvp_ref, kvc_ref, do_ref, lse_ref, pb_ref, gb2_ref, dq_ref, dkv_ref, dsink_ref, chips_ref,
             sib2_ref, carry, send_sems, recv_sems, send_sems2, recv_sems2):
        n = pl.program_id(0)
        start, wait = _to_chips_job(pb_ref, chips_ref, send_sems, recv_sems)
        start2, wait2 = _to_sibling_job(gb2_ref, sib2_ref, send_sems2, recv_sems2)

        @pl.when(n == 0)
        def _():
            carry[...] = jnp.zeros_like(carry)
            dsink_ref[...] = jnp.zeros_like(dsink_ref)
            start()
            start2()

        @pl.when(n < nb)
        def _():
            first = n == 0
            ext = _kv_halves(kvp_ref[...], kvc_ref[...], first)
            dist, valid = _attn_masks(first)
            dist4, valid4 = jnp.concatenate([dist] * 4, axis=0), jnp.concatenate([valid] * 4, axis=0)
            lane1 = lax.broadcasted_iota(jnp.int32, (1, 128), 1)
            lse = lse_ref[...]
            qts = [q_ref[:, 128 * b:128 * (b + 1)].astype(f32).T.astype(bf16) for b in range(8)]
            dots = [do_ref[:, 128 * b:128 * (b + 1)].astype(f32).T.astype(bf16) for b in range(8)]
            acck = [None, None]
            accv = [None, None]
            dsink = jnp.zeros((1, 128), f32)
            dq_acc = [None] * 8
            for g in range(2):
                for half in range(2):
                    k_ext, v_ext = ext[g][half]
                    blks, heads, slope, sink = _head_stack(g, half, sink_ref)
                    qs = jnp.concatenate([q_ref[:, 128 * b:128 * (b + 1)] for b in blks], axis=0)
                    dos = jnp.concatenate([do_ref[:, 128 * b:128 * (b + 1)] for b in blks], axis=0)
                    rows = slice(HEAD_DIM * half, HEAD_DIM * (half + 1))
                    qt = jnp.concatenate([qts[b][rows, :] for b in blks], axis=1)
                    dot_ = jnp.concatenate([dots[b][rows, :] for b in blks], axis=1)
                    lse_col = jnp.concatenate([lse[:, h:h + 1] for h in heads], axis=0)
                    s = _dot_nt(qs, k_ext) * ATTN_SCALE - slope * dist4
                    p = jnp.where(valid4, jnp.exp(s - lse_col), 0.0)
                    dp = _dot_nt(dos, v_ext)
                    delta = jnp.sum(p * dp, axis=-1, keepdims=True)
                    ds = (p * (dp - delta) * ATTN_SCALE).astype(bf16)
                    sd = jnp.exp(sink - lse_col) * delta
                    dqs = _dot(ds, k_ext)
                    for i, b in enumerate(blks):
                        seg = slice(T * i, T * (i + 1))
                        dq_acc[b] = dqs[seg, :] if dq_acc[b] is None else dq_acc[b] + dqs[seg, :]
                        dsink = dsink - jnp.where(lane1 == heads[i], jnp.sum(sd[seg, :], axis=0, keepdims=True), 0.0)
                    dk = _dot(qt, ds)
                    dv = _dot(dot_, p.astype(bf16))
                    acck[g] = dk if acck[g] is None else acck[g] + dk
                    accv[g] = dv if accv[g] is None else accv[g] + dv
            dq_ref[...] = jnp.concatenate([a.astype(bf16) for a in dq_acc], axis=1)
            dsink_ref[...] += dsink
            dkv = jnp.concatenate([jnp.concatenate(acck, axis=0).T, jnp.concatenate(accv, axis=0).T], axis=1)
            dkv_ref[...] = (carry[...] + dkv[0:T, :]).astype(bf16)
            carry[...] = dkv[T:2 * T, :]

        @pl.when(n == nb)
        def _():
            dkv_ref[...] = carry[...].astype(bf16)
            wait()
            wait2()

    cur = lambda i: (jnp.minimum(i, nb - 1), 0)
    return pl.pallas_call(
        body, name="attn_bwd", grid=(nb + 1,),
        in_specs=[pl.BlockSpec(memory_space=pltpu.SMEM), pl.BlockSpec((T, D_MODEL), cur),
                  pl.BlockSpec((T, 256), lambda i: (jnp.maximum(jnp.minimum(i, nb - 1) - 1, 0), 0)),
                  pl.BlockSpec((T, 256), cur), pl.BlockSpec((T, D_MODEL), cur), pl.BlockSpec((T, 128), cur), _ANY_SPEC,
                  _ANY_SPEC],
        out_specs=[pl.BlockSpec((T, D_MODEL), cur), pl.BlockSpec((T, 256), lambda i: (jnp.maximum(i - 1, 0), 0)),
                   _full((1, 128)), _ANY_SPEC, _ANY_SPEC],
        out_shape=[jax.ShapeDtypeStruct((L, D_MODEL), bf16), jax.ShapeDtypeStruct((L, 256), bf16),
                   jax.ShapeDtypeStruct((1, 128), f32), jax.ShapeDtypeStruct((N_CHIPS - 1,) + pb.shape[1:], bf16),
                   jax.ShapeDtypeStruct((N_CHIPS,) + gb2.shape[2:], bf16)],
        scratch_shapes=[pltpu.VMEM((T, 256), f32)] + _sems(N_CHIPS - 1) + _sems(N_CHIPS),
        compiler_params=_cparams("arbitrary"),
    )(sinks, q, kv, kv, do, lse, pb, gb2)


def _ssd_bwd(xc, dt_raw, dy, prev_all, dt_bias, a_log, d_skip_x, e_mat, g, pb2):
    L = xc.shape[0]
    nc = L // CHUNK
    T = CHUNK
    RG = g.shape[0]

    def body(xc_ref, dt_ref, dy_ref, prev_ref, dtb_ref, al_ref, dsk_ref, e_ref, g_in_ref, pb2_ref,
             dxc_ref, ddt_ref, acc_ref, dd_ref, g_ref, chips2_ref, dst, dxs_s, send_sems, recv_sems, send_sems2, recv_sems2):
        start, wait = _share_job(g_ref, send_sems, recv_sems, RG)
        start2, wait2 = _to_chips_job(pb2_ref, chips2_ref, send_sems2, recv_sems2)

        @pl.when(pl.program_id(0) == 0)
        def _():
            dst[...] = jnp.zeros_like(dst)
            acc_ref[...] = jnp.zeros_like(acc_ref)
            dd_ref[...] = jnp.zeros_like(dd_ref)
            start()
            start2()

        @pl.when(pl.program_id(0) == nc - 1)
        def _():
            wait()
            wait2()

        lane1 = lax.broadcasted_iota(jnp.int32, (1, 128), 1)
        a_row = jnp.where(lane1 < SSD_HEADS, -jnp.exp(al_ref[...]), 0.0)
        e_mat_v = e_ref[...]
        dt, a, cum, dtx, cumx, r, c = _ssd_chunk_common(dt_ref[...], dtb_ref[...], a_row)
        cum_t = cum.T
        ex = jnp.exp(cumx)
        last = cumx[T - 1:T, :]
        wx = jnp.exp(last - cumx)
        cdx = jnp.exp(last)
        xs = xc_ref[:, 0:SSD_WIDTH]
        X = xs * dtx
        Xb = X.astype(bf16)
        Xdb = (X * wx).astype(bf16)
        dyv = dy_ref[...]
        prev = prev_ref[0]
        prevb = prev.astype(bf16)
        dnew = dst[...]
        dnewb = dnew.astype(bf16)
        tri = c <= r
        lane = lax.broadcasted_iota(jnp.int32, (T, 128), 1)
        sub = lax.broadcasted_iota(jnp.int32, (128, T), 0)
        lo = lane < HEAD_DIM

        def red(vals, g):
            return _dot_sel_nt(vals, e_mat_v[:, 512 * g:512 * (g + 1)])

        de = jnp.zeros((T, 128), f32)
        dw = jnp.zeros((T, 128), f32)
        ddt_x = jnp.zeros((T, 128), f32)
        dcum_col = jnp.zeros((T, 128), f32)
        dcum_row = jnp.zeros((128, T), f32)
        dprev_parts, dBs, dCs = [], [], []
        for g in range(SSD_GROUPS):
            s5 = slice(512 * g, 512 * (g + 1))
            Bg = xc_ref[:, 1024 + 128 * g:1152 + 128 * g].astype(bf16)
            Cg = xc_ref[:, 1280 + 128 * g:1408 + 128 * g].astype(bf16)
            G = _dot_nt(Cg, Bg)
            Z = _dot(Cg, prevb[:, s5])
            dyg = dyv[:, s5]
            dZb = (dyg * ex[:, s5]).astype(bf16)
            dXd = _dot(Bg, dnewb[:, s5])
            dC = _dot_nt(dZb, prevb[:, s5])
            dB = _dot_nt(Xdb[:, s5], dnewb[:, s5])
            dprev_parts.append(_dot_tn(Cg, dZb) + dnew[:, s5] * cdx[:, s5])
            de = de + red(dyg * Z, g)
            dw = dw + red(dXd * X[:, s5], g)
            dXg = dXd * wx[:, s5]
            dG = jnp.zeros((T, T), f32)
            for j in range(4):
                blk = 4 * g + j
                sl = slice(128 * blk, 128 * (blk + 1))
                Xblk = Xb[:, sl]
                dyblk = dyv[:, sl]
                dyblk_b = dyblk.astype(bf16)
                dxh = []
                for half in range(2):
                    h = 2 * blk + half
                    seg = jnp.minimum(cum[:, h:h + 1] - cum_t[h:h + 1, :], 0.0)
                    Lm = jnp.where(tri, jnp.exp(seg), 0.0)
                    M = G * Lm
                    dyh = jnp.where(lo if half == 0 else ~lo, dyblk, 0.0).astype(bf16)
                    dM = _dot_nt(dyh, Xblk)
                    dG = dG + dM * Lm
                    Q = dM * M
                    dcum_col = dcum_col + jnp.where(lane == h, jnp.sum(Q, axis=1, keepdims=True), 0.0)
                    dcum_row = dcum_row + jnp.where(sub == h, jnp.sum(Q, axis=0, keepdims=True), 0.0)
                    dxh.append(_dot_tn(M.astype(bf16), dyblk_b))
                dXblk = dXg[:, 128 * j:128 * (j + 1)] + jnp.where(lo, dxh[0], dxh[1])
                xsb = xs[:, sl]
                dxs_s[:, sl] = dXblk * dtx[:, sl] + dsk_ref[:, sl] * dyblk
                ddt_x = ddt_x + _dot_sel_nt(dXblk * xsb, e_mat_v[:, sl])
                dd_ref[:, sl] += jnp.sum(dyblk * xsb, axis=0, keepdims=True)
            dGb = dG.astype(bf16)
            dCs.append(dC + _dot(dGb, Bg))
            dBs.append(dB + _dot_tn(dGb, Cg))
        e16 = jnp.exp(cum)
        cum_last = cum[T - 1:T, :]
        w16 = jnp.exp(cum_last - cum)
        dcd = jnp.sum(dnew * prev, axis=0, keepdims=True)
        dcd16 = red(dcd[:, 0:512], 0) + red(dcd[:, 512:1024], 1)
        dww = dw * w16
        extra = jnp.sum(dww, axis=0, keepdims=True) + dcd16 * jnp.exp(cum_last)
        rowi = lax.broadcasted_iota(jnp.int32, (T, 128), 0)
        dcum = dcum_col - dcum_row.T + de * e16 - dww + jnp.where(rowi == T - 1, extra, 0.0)
        da = _sel_dot((c >= r).astype(bf16), dcum)
        ddt = ddt_x + da * a_row
        acc_ref[0:1, :] += jnp.sum(da * dt, axis=0, keepdims=True)
        ddt_raw = jnp.where(lane < SSD_HEADS, ddt * _sigmoid(dt_ref[...] + dtb_ref[...]), 0.0)
        ddt_ref[...] = ddt_raw
        acc_ref[1:2, :] += jnp.sum(ddt_raw, axis=0, keepdims=True)
        dxc_ref[:, 0:SSD_WIDTH] = dxs_s[...]
        dxc_ref[:, 1024:1280] = jnp.concatenate(dBs, axis=1)
        dxc_ref[:, 1280:1536] = jnp.concatenate(dCs, axis=1)
        dst[...] = jnp.concatenate(dprev_parts, axis=1)

    rev = lambda i: (nc - 1 - i, 0)
    return pl.pallas_call(
        body, name="ssd_bwd", grid=(nc,),
        in_specs=[pl.BlockSpec((T, CONV_DIM), rev), pl.BlockSpec((T, 128), rev), pl.BlockSpec((T, SSD_WIDTH), rev),
                  pl.BlockSpec((1, SSD_STATE, SSD_WIDTH), lambda i: (nc - 1 - i, 0, 0)),
                  _full((1, 128)), _full((1, 128)), _full((1, SSD_WIDTH)), _full((128, SSD_WIDTH)), _ANY_SPEC, _ANY_SPEC],
        out_specs=[pl.BlockSpec((T, CONV_DIM), rev), pl.BlockSpec((T, 128), rev), _full((8, 128)),
                   _full((1, SSD_WIDTH)), _ANY_SPEC, _ANY_SPEC],
        out_shape=[jax.ShapeDtypeStruct((L, CONV_DIM), f32), jax.ShapeDtypeStruct((L, 128), f32),
                   jax.ShapeDtypeStruct((8, 128), f32), jax.ShapeDtypeStruct((1, SSD_WIDTH), f32),
                   jax.ShapeDtypeStruct(g.shape, f32), jax.ShapeDtypeStruct((N_CHIPS - 1,) + pb2.shape[1:], bf16)],
        input_output_aliases={8: 4},
        scratch_shapes=[pltpu.VMEM((SSD_STATE, SSD_WIDTH), f32), pltpu.VMEM((T, SSD_WIDTH), f32)] + _sems(1)
        + _sems(N_CHIPS - 1),
        compiler_params=_cparams("arbitrary"),
    )(xc, dt_raw, dy, prev_all, dt_bias, a_log, d_skip_x, e_mat, g, pb2)


def _conv_bwd(dxc, xr, conv_w, conv_b, g):
    L = dxc.shape[0]
    tm = 256
    nt = L // tm
    RG = g.shape[0]

    def body(dxc_ref, xr_ref, xh_ref, cw_ref, cb_ref, g_in_ref, dxr_ref, acc_ref, g_ref, carry, buf, ext, send_sems, recv_sems):
        i = pl.program_id(0)
        start, wait = _share_job(g_ref, send_sems, recv_sems, RG)

        @pl.when(i == 0)
        def _():
            carry[...] = jnp.zeros_like(carry)
            acc_ref[...] = jnp.zeros_like(acc_ref)
            ext[tm + 16:tm + CHUNK, :] = jnp.zeros((CHUNK - 16, CONV_DIM), bf16)
            start()

        @pl.when(i == nt - 1)
        def _():
            wait()

        buf[0:8, :] = jnp.where(i == nt - 1, 0.0, xh_ref[...])
        u = xr_ref[...]
        buf[8:8 + tm, :] = u
        pre = cb_ref[...] + cw_ref[CONV_K - 1:CONV_K, :] * u
        for k in range(CONV_K - 1):
            pre = pre + cw_ref[k:k + 1, :] * buf[5 + k:5 + k + tm, :]
        sg = _sigmoid(pre)
        dpre = dxc_ref[...] * (sg * (1.0 + pre * (1.0 - sg)))
        acc_ref[4:5, :] += jnp.sum(dpre, axis=0, keepdims=True)
        dpb = dpre.astype(bf16)
        ext[0:tm, :] = dpb
        ext[tm:tm + 16, :] = carry[...]
        acc_ref[CONV_K - 1:CONV_K, :] += jnp.sum(u * dpre, axis=0, keepdims=True)
        du = cw_ref[CONV_K - 1:CONV_K, :] * dpre
        r = lax.broadcasted_iota(jnp.int32, (CHUNK, 2 * CHUNK), 0)
        c = lax.broadcasted_iota(jnp.int32, (CHUNK, 2 * CHUNK), 1)
        for j in range(1, CONV_K):
            move = (c == r + j).astype(bf16)
            up = jnp.concatenate([_dot(move, ext[CHUNK * b:CHUNK * (b + 2), :]) for b in range(tm // CHUNK)], axis=0)
            k = CONV_K - 1 - j
            du = du + cw_ref[k:k + 1, :] * up
            acc_ref[k:k + 1, :] += jnp.sum(u * up, axis=0, keepdims=True)
        dxr_ref[...] = du.astype(bf16)
        carry[...] = dpb[0:16, :]

    rev = lambda i: (nt - 1 - i, 0)
    return pl.pallas_call(
        body, name="conv_bwd", grid=(nt,),
        in_specs=[pl.BlockSpec((tm, CONV_DIM), rev), pl.BlockSpec((tm, CONV_DIM), rev),
                  pl.BlockSpec((8, CONV_DIM), lambda i: (jnp.maximum((nt - 1 - i) * (tm // 8) - 1, 0), 0)),
                  _full((CONV_K, CONV_DIM)), _full((1, CONV_DIM)), _ANY_SPEC],
        out_specs=[pl.BlockSpec((tm, CONV_DIM), rev), _full((8, CONV_DIM)), _ANY_SPEC],
        out_shape=[jax.ShapeDtypeStruct((L, CONV_DIM), bf16), jax.ShapeDtypeStruct((8, CONV_DIM), f32),
                   jax.ShapeDtypeStruct(g.shape, f32)],
        input_output_aliases={5: 2},
        scratch_shapes=[pltpu.VMEM((16, CONV_DIM), bf16), pltpu.VMEM((tm + 8, CONV_DIM), f32),
                        pltpu.VMEM((tm + CHUNK, CONV_DIM), bf16)] + _sems(1),
        compiler_params=_cparams("arbitrary"),
    )(dxc, xr, xr, conv_w, conv_b, g)


def _inproj_bwd(dz, dxr, dq, dkv, ddt, dr1, x, mod, ln_g, ln_b, w_in, pb):
    L = x.shape[0]
    tm = DENSE_TM
    nt = L // tm

    def body(dz_ref, dxr_ref, dq_ref, dkv_ref, ddt_ref, dr1_ref, x_ref, mod_ref, g_ref, b_ref, w_ref, pb_ref,
             dx_ref, acc_ref, chips_ref, send_sems, recv_sems):
        start, wait = _to_chips_job(pb_ref, chips_ref, send_sems, recv_sems)

        @pl.when(pl.program_id(0) == 0)
        def _():
            acc_ref[...] = jnp.zeros_like(acc_ref)
            start()

        @pl.when(pl.program_id(0) == nt - 1)
        def _():
            wait()

        du1 = (_dot(dz_ref[...], w_ref[W_Z, :]) + _dot(dxr_ref[...], w_ref[W_XBC, :])
               + _dot(dq_ref[...], w_ref[W_Q, :]) + _dot(dkv_ref[...], w_ref[W_KV, :])
               + _dot(ddt_ref[...].astype(bf16), w_ref[W_DT, :]))
        xhat, rstd = _ln_stats(x_ref[...])
        h0 = xhat * g_ref[...] + b_ref[...]
        acc_ref[0:1, :] += jnp.sum(du1 * h0, axis=0, keepdims=True)
        acc_ref[1:2, :] += jnp.sum(du1, axis=0, keepdims=True)
        dh0 = du1 * (1.0 + mod_ref[1:2, :]) + ALPHA * dr1_ref[...]
        acc_ref[2:3, :] += jnp.sum(dh0 * xhat, axis=0, keepdims=True)
        acc_ref[3:4, :] += jnp.sum(dh0, axis=0, keepdims=True)
        dx_ref[...] = _ln_bwd(dh0, xhat, rstd, g_ref[...])

    v = _full((1, D_MODEL))
    return pl.pallas_call(
        body, name="inproj_bwd", grid=(nt,),
        in_specs=[_rows(tm, D_MODEL), _rows(tm, CONV_DIM), _rows(tm, D_MODEL), _rows(tm, 256), _rows(tm, 128),
                  _rows(tm, D_MODEL), _rows(tm, D_MODEL), _full((8, D_MODEL)), v, v, _resident((PROJ_WIDTH, D_MODEL)),
                  _ANY_SPEC],
        out_specs=[_rows(tm, D_MODEL), _full((8, D_MODEL)), _ANY_SPEC],
        out_shape=[jax.ShapeDtypeStruct((L, D_MODEL), f32), jax.ShapeDtypeStruct((8, D_MODEL), f32),
                   jax.ShapeDtypeStruct((N_CHIPS - 1,) + pb.shape[1:], bf16)],
        scratch_shapes=_sems(N_CHIPS - 1),
        compiler_params=_cparams("arbitrary"),
    )(dz, dxr, dq, dkv, ddt, dr1, x, mod, ln_g, ln_b, w_in, pb)


def _adamw_math(w, g, m, v):
    m = ADAM_B1 * m + (1.0 - ADAM_B1) * g
    v = ADAM_B2 * v + (1.0 - ADAM_B2) * (g * g)
    m_hat = m / (1.0 - ADAM_B1 ** ADAM_STEP)
    v_hat = v / (1.0 - ADAM_B2 ** ADAM_STEP)
    delta = -ADAM_LR * (m_hat / (jnp.sqrt(v_hat) + ADAM_EPS) + ADAM_WD * w)
    return delta, m, v


def _adamw(w, g, m, v, name):
    R, C = w.shape

    def body(w_ref, g_ref, m_ref, v_ref, d_ref, m2_ref, v2_ref):
        d_ref[...], m2_ref[...], v2_ref[...] = _adamw_math(w_ref[...], g_ref[...], m_ref[...], v_ref[...])

    cap = max(8, ADAMW_BLOCK_ELEMS // C)
    tr = R if R <= cap else next(t for t in range(cap - cap % 8, 7, -8) if R % t == 0)
    spec = pl.BlockSpec((tr, C), lambda i: (i, 0))
    return pl.pallas_call(
        body, name=name, grid=(R // tr,), in_specs=[spec] * 4, out_specs=[spec] * 3,
        out_shape=[jax.ShapeDtypeStruct((R, C), f32)] * 3, compiler_params=_cparams("parallel"),
    )(w, g, m, v)


def _adamw_rows(w, g, m, v):
    R, _, C = w.shape
    tr = R // 4

    def body(w_ref, g_ref, m_ref, v_ref, d_ref, m2_ref, v2_ref):
        d_ref[...], m2_ref[...], v2_ref[...] = _adamw_math(w_ref[...], g_ref[...], m_ref[...], v_ref[...])

    spec = pl.BlockSpec((tr, 1, C), lambda i: (i, 0, 0))
    return pl.pallas_call(
        body, name="adamw_w_in", grid=(R // tr,), in_specs=[spec] * 4, out_specs=[spec] * 3,
        out_shape=[jax.ShapeDtypeStruct((R, 1, C), f32)] * 3, compiler_params=_cparams("parallel"),
    )(w, g, m, v)


ADA_COLS = 6 * D_MODEL // N_CHIPS
ADA_TN = 512


COND_LANES = 512


def _prologue(cond, ada_w, ada_b, blob):
    R = blob.shape[0]

    def body(cond_ref, w_ref, b_ref, blob_ref, call_ref, mod_ref, wall_ref, mod_s, stage, gs, gr, ms, mr, ws, wr, local_sem):
        x, y, c = _place()
        start_w, finish_w = _gather_job(blob_ref, wall_ref, ws, wr, R)

        def rows(ref, px, py, pc):
            return ref.at[pl.ds(pl.multiple_of((4 * px + 2 * py + pc) * 8, 8), 8), :]

        mine = pltpu.make_async_copy(cond_ref, rows(call_ref, x, y, c), local_sem)
        mine.start()
        sends = [_remote(cond_ref, rows(call_ref, x, y, c), gs, gr, m - 1, _flip(x, y, c, m)) for m in range(1, N_DEV)]
        for cp in sends:
            cp.start()
        for m in range(1, N_DEV):
            peer = _flip(x, y, c, m)
            _remote(cond_ref, rows(call_ref, *peer), gs, gr, m - 1, peer).wait_recv()
        for cp in sends:
            cp.wait_send()
        mine.wait()
        start_w()

        for k in range(R // STAGE_ROWS):
            part = pl.ds(STAGE_ROWS * k, STAGE_ROWS)
            cin = pltpu.make_async_copy(blob_ref.at[part, :], stage, local_sem)
            cin.start()
            cin.wait()
            cout = pltpu.make_async_copy(stage, wall_ref.at[2 * x + y, part, :], local_sem)
            cout.start()
            cout.wait()

        lo = jnp.concatenate([call_ref[8 * d:8 * d + 1, :] for d in range(N_DEV)], axis=0)
        hi = jnp.concatenate([call_ref[8 * d + 1:8 * d + 2, :] for d in range(N_DEV)], axis=0)
        mod_all = (_dot_exact(lo * _sigmoid(lo), w_ref[0:COND_LANES, :]) + _dot_exact(hi * _sigmoid(hi), w_ref[COND_LANES:, :])
                   + b_ref[...])
        for d in range(N_DEV):
            mod_s[8 * d:8 * d + 8, :] = jnp.broadcast_to(mod_all[d:d + 1, :], (8, ADA_COLS))

        mine = pltpu.make_async_copy(rows(mod_s, x, y, c), mod_ref.at[2 * x + y], local_sem)
        mine.start()
        sends = []
        for m in range(1, N_CHIPS):
            peer = _flip(x, y, c, 2 * m)
            sends.append(_remote(rows(mod_s, *peer), mod_ref.at[2 * x + y], ms, mr, m - 1, peer))
        for cp in sends:
            cp.start()
        for m in range(1, N_CHIPS):
            px, py, pc = _flip(x, y, c, 2 * m)
            _remote(rows(mod_s, x, y, c), mod_ref.at[2 * px + py], ms, mr, m - 1, (px, py, pc)).wait_recv()
        for cp in sends:
            cp.wait_send()
        mine.wait()
        finish_w()

    return pl.pallas_call(
        body, name="prologue",
        out_shape=[jax.ShapeDtypeStruct((8 * N_DEV, COND_LANES), f32), jax.ShapeDtypeStruct((N_CHIPS, 8, ADA_COLS), f32),
                   jax.ShapeDtypeStruct((N_CHIPS, R, D_MODEL), bf16)],
        in_specs=[_VMEM_SPEC, _VMEM_SPEC, _VMEM_SPEC, _ANY_SPEC], out_specs=[_VMEM_SPEC, _VMEM_SPEC, _ANY_SPEC],
        scratch_shapes=[pltpu.VMEM((8 * N_DEV, ADA_COLS), f32), pltpu.VMEM((STAGE_ROWS, D_MODEL), bf16)]
        + _sems(N_DEV - 1) + _sems(N_CHIPS - 1) + _sems(6) + [pltpu.SemaphoreType.DMA],
        compiler_params=pltpu.CompilerParams(vmem_limit_bytes=VMEM_LIMIT),
    )(cond, ada_w, ada_b, blob)


def _ada_bwd(c_all, dmod, w, m, v):
    def body(c_ref, d_ref, w_ref, m_ref, v_ref, g_ref, dl_ref, m2_ref, v2_ref):
        cv = c_ref[...]
        g = lax.dot_general(cv * _sigmoid(cv), d_ref[...], (((0,), (0,)), ((), ())), preferred_element_type=f32,
                            precision=lax.Precision.HIGHEST)
        g_ref[...] = g
        dl_ref[...], m2_ref[...], v2_ref[...] = _adamw_math(w_ref[...], g, m_ref[...], v_ref[...])

    wspec = pl.BlockSpec((D_MODEL, ADA_TN), lambda j: (0, j))
    return pl.pallas_call(
        body, name="ada_bwd", grid=(ADA_COLS // ADA_TN,),
        in_specs=[_full((N_DEV, D_MODEL)), pl.BlockSpec((N_DEV, ADA_TN), lambda j: (0, j)), wspec, wspec, wspec],
        out_specs=[wspec] * 4, out_shape=[jax.ShapeDtypeStruct((D_MODEL, ADA_COLS), f32)] * 4,
        compiler_params=_cparams("parallel"),
    )(c_all, dmod, w, m, v)


SMALL_SLOTS = (("ada_b", 6144), ("ln_in_g", 1024), ("ln_in_b", 1024), ("conv_b", 1536), ("dt_bias", 128), ("a_log", 128),
               ("d_skip", 128), ("ssd_norm_w", 1024), ("attn_sinks", 128), ("ln1_g", 1024), ("ln1_b", 1024),
               ("b_ff1", 4096), ("b_ff2", 1024), ("ln2_g", 1024), ("ln2_b", 1024), ("conv_w", 6144), ("loss", 1024))
SMALL_N = sum(n for _, n in SMALL_SLOTS)
SMALL_OFF = {name: sum(n for _, n in SMALL_SLOTS[:i]) for i, (name, _) in enumerate(SMALL_SLOTS)}
SMALL_PARAMS = tuple(name for name, _ in SMALL_SLOTS[:15])
assert SMALL_N % 1024 == 0


def _small_pack(acc_in, acc_out, acc_mlp, db1, acc_conv, acc_ssd, dd_x, dsink, alog, e_mat):
    def body(in_ref, out_ref, mlp_ref, db1_ref, conv_ref, ssd_ref, dd_ref, sink_ref, al_ref, e_ref, o_ref):
        def put(name, val, at=0):
            off = SMALL_OFF[name] + at
            o_ref[:, off:off + val.shape[1]] = val

        for k, row in enumerate((in_ref[1:2, :], in_ref[0:1, :], out_ref[0:1, :], mlp_ref[A_SH2:A_SH2 + 1, :],
                                 mlp_ref[A_SC2:A_SC2 + 1, :], mlp_ref[A_G2:A_G2 + 1, :])):
            put("ada_b", row, D_MODEL * k)
        put("ln_in_g", in_ref[2:3, :])
        put("ln_in_b", in_ref[3:4, :])
        put("conv_b", conv_ref[4:5, :])
        put("dt_bias", ssd_ref[1:2, :])
        put("a_log", ssd_ref[0:1, :] * (-jnp.exp(al_ref[...])))
        put("d_skip", _dot_sel_nt(jnp.broadcast_to(dd_ref[...], (8, SSD_WIDTH)), e_ref[...])[0:1, :])
        put("ssd_norm_w", out_ref[1:2, :])
        put("attn_sinks", sink_ref[...])
        put("ln1_g", mlp_ref[A_LN1G:A_LN1G + 1, :])
        put("ln1_b", mlp_ref[A_LN1B:A_LN1B + 1, :])
        put("b_ff1", db1_ref[...])
        put("b_ff2", mlp_ref[A_B2:A_B2 + 1, :])
        put("ln2_g", mlp_ref[A_LN2G:A_LN2G + 1, :])
        put("ln2_b", mlp_ref[A_LN2B:A_LN2B + 1, :])
        for k in range(CONV_K):
            put("conv_w", conv_ref[k:k + 1, :], CONV_DIM * k)
        put("loss", mlp_ref[A_LOSS:A_LOSS + 1, :])

    return pl.pallas_call(body, name="small_pack", out_shape=jax.ShapeDtypeStruct((1, SMALL_N), f32),
                          compiler_params=_cparams())(acc_in, acc_out, acc_mlp, db1, acc_conv, acc_ssd, dd_x, dsink, alog, e_mat)


def _small_update(gathered, params, moms, vels):
    k = len(SMALL_PARAMS)

    def body(g_ref, *refs):
        w_refs, m_refs, v_refs, outs = refs[:k], refs[k:2 * k], refs[2 * k:3 * k], refs[3 * k:]

        def total(name, n):
            off = SMALL_OFF[name]
            g = g_ref[0:1, off:off + n]
            for i in range(1, N_DEV):
                g = g + g_ref[i:i + 1, off:off + n]
            return g

        for j, name in enumerate(SMALL_PARAMS):
            n = w_refs[j].shape[1]
            g = total(name, max(n, 128))[:, :n]
            outs[4 * j][...] = g
            outs[4 * j + 1][...], outs[4 * j + 2][...], outs[4 * j + 3][...] = _adamw_math(
                w_refs[j][...], g, m_refs[j][...], v_refs[j][...])
        outs[4 * k][...] = total("conv_w", CONV_K * CONV_DIM)
        outs[4 * k + 1][...] = total("loss", D_MODEL)

    shapes = [jax.ShapeDtypeStruct(p.shape, f32) for p in params for _ in range(4)]
    shapes += [jax.ShapeDtypeStruct((1, CONV_K * CONV_DIM), f32), jax.ShapeDtypeStruct((1, D_MODEL), f32)]
    return pl.pallas_call(body, name="small_update", out_shape=shapes,
                          compiler_params=_cparams())(gathered, *params, *moms, *vels)


def _place():
    return lax.axis_index("x"), lax.axis_index("y"), lax.axis_index("c")


def _flip(x, y, c, m):
    return (1 - x if m & 4 else x, 1 - y if m & 2 else y, 1 - c if m & 1 else c)


_VMEM_SPEC = pl.BlockSpec(memory_space=pltpu.VMEM)
_ANY_SPEC = pl.BlockSpec(memory_space=pl.ANY)


def _allgather8(v, name):
    n = v.shape[1]

    def body(v_ref, out_ref, send_sems, recv_sems, local_sem):
        x, y, c = _place()

        def rows(px, py, pc):
            return out_ref.at[pl.ds(pl.multiple_of((4 * px + 2 * py + pc) * 8, 8), 8), :]

        def copy(m, src, dst, to):
            return pltpu.make_async_remote_copy(src_ref=src, dst_ref=dst, send_sem=send_sems.at[m - 1],
                                                recv_sem=recv_sems.at[m - 1], device_id=to, device_id_type=MESH)

        mine = pltpu.make_async_copy(v_ref, rows(x, y, c), local_sem)
        mine.start()
        sends = [copy(m, v_ref, rows(x, y, c), _flip(x, y, c, m)) for m in range(1, N_DEV)]
        for cp in sends:
            cp.start()
        for m in range(1, N_DEV):
            peer = _flip(x, y, c, m)
            copy(m, v_ref, rows(*peer), peer).wait_recv()
        for cp in sends:
            cp.wait_send()
        mine.wait()

    return pl.pallas_call(
        body, name=name, out_shape=jax.ShapeDtypeStruct((8 * N_DEV, n), f32), in_specs=[_VMEM_SPEC],
        out_specs=_VMEM_SPEC,
        scratch_shapes=[pltpu.SemaphoreType.DMA((N_DEV - 1,)), pltpu.SemaphoreType.DMA((N_DEV - 1,)),
                        pltpu.SemaphoreType.DMA],
    )(v)


def _remote(src, dst, send_sems, recv_sems, k, to):
    return pltpu.make_async_remote_copy(src_ref=src, dst_ref=dst, send_sem=send_sems.at[k], recv_sem=recv_sems.at[k],
                                        device_id=to, device_id_type=MESH)


def _gather_job(blob_ref, out_ref, send_sems, recv_sems, R):
    x, y, c = _place()
    sib = (x, y, 1 - c)
    hr = R // 2

    def half(px, py, pc):
        return out_ref.at[2 * px + py, pl.ds(pl.multiple_of(pc * hr, 16), hr), :]

    my_half = blob_ref.at[pl.ds(pl.multiple_of(c * hr, 16), hr), :]

    def first():
        return [_remote(my_half, half(x, y, c), send_sems, recv_sems, m - 1, _flip(x, y, c, 2 * m))
                for m in range(1, N_CHIPS)]

    def start():
        for cp in first():
            cp.start()

    def finish():
        passed = []
        for m in range(1, N_CHIPS):
            px, py, pc = _flip(x, y, c, 2 * m)
            _remote(my_half, half(px, py, pc), send_sems, recv_sems, m - 1, (px, py, pc)).wait_recv()
            fwd = _remote(half(px, py, pc), half(px, py, pc), send_sems, recv_sems, 2 + m, sib)
            fwd.start()
            passed.append(fwd)
        for m in range(1, N_CHIPS):
            px, py, pc = _flip(x, y, c, 2 * m)
            _remote(my_half, half(px, py, 1 - pc), send_sems, recv_sems, 2 + m, sib).wait_recv()
        for cp in first() + passed:
            cp.wait_send()

    return start, finish


def _to_sibling_job(g_ref, out_ref, send_sems, recv_sems):
    x, y, c = _place()

    def cps():
        return [_remote(g_ref.at[j, 1 - c], out_ref.at[j], send_sems, recv_sems, j, (x, y, 1 - c)) for j in range(N_CHIPS)]

    def start():
        for cp in cps():
            cp.start()

    def wait():
        for cp in cps():
            cp.wait()

    return start, wait


def _to_chips_job(p_ref, out_ref, send_sems, recv_sems):
    x, y, c = _place()

    def cps():
        out = []
        for m in range(1, N_CHIPS):
            px, py, pc = _flip(x, y, c, 2 * m)
            out.append(_remote(p_ref.at[2 * px + py], out_ref.at[m - 1], send_sems, recv_sems, m - 1, (px, py, pc)))
        return out

    def start():
        for cp in cps():
            cp.start()

    def wait():
        for cp in cps():
            cp.wait()

    return start, wait


def _share_job(g_ref, send_sems, recv_sems, R):
    x, y, c = _place()

    def rows(pc):
        return g_ref.at[pl.ds(pl.multiple_of(pc * (R // 2), 8), R // 2), :]

    def start():
        _remote(rows(c), rows(c), send_sems, recv_sems, 0, (x, y, 1 - c)).start()

    def wait():
        _remote(rows(c), rows(1 - c), send_sems, recv_sems, 0, (x, y, 1 - c)).wait_recv()
        _remote(rows(c), rows(c), send_sems, recv_sems, 0, (x, y, 1 - c)).wait_send()

    return start, wait


def _sems(n):
    return [pltpu.SemaphoreType.DMA((n,)), pltpu.SemaphoreType.DMA((n,))]


def _rs_to_sibling(gb):
    def body(g_ref, out_ref, send_sems, recv_sems):
        start, wait = _to_sibling_job(g_ref, out_ref, send_sems, recv_sems)
        start()
        wait()

    return pl.pallas_call(
        body, name="rs_to_sibling", out_shape=jax.ShapeDtypeStruct((N_CHIPS,) + gb.shape[2:], bf16),
        in_specs=[_ANY_SPEC], out_specs=_ANY_SPEC, scratch_shapes=_sems(N_CHIPS),
    )(gb)


def _rs_share(g):
    R = g.shape[0]

    def body(g_ref, out_ref, send_sems, recv_sems):
        start, wait = _share_job(out_ref, send_sems, recv_sems, R)
        start()
        wait()

    return pl.pallas_call(
        body, name="rs_share", out_shape=jax.ShapeDtypeStruct(g.shape, f32), in_specs=[_ANY_SPEC],
        out_specs=_ANY_SPEC, input_output_aliases={0: 0}, scratch_shapes=_sems(1),
    )(g)


RS_TR_MAX = 512


def _rs_sum_pair(place, gb, recv, name):
    hr = gb.shape[2]
    tr = min(hr, RS_TR_MAX)

    def body(pl_ref, g_ref, r_ref, o_ref):
        o_ref[0] = (g_ref[0, 0].astype(f32) + r_ref[0].astype(f32)).astype(bf16)

    return pl.pallas_call(
        body, name=name,
        grid_spec=pltpu.PrefetchScalarGridSpec(
            num_scalar_prefetch=1, grid=(N_CHIPS, hr // tr),
            in_specs=[pl.BlockSpec((1, 1, tr, D_MODEL), lambda j, i, p: (j, p[0], i, 0)),
                      pl.BlockSpec((1, tr, D_MODEL), lambda j, i, p: (j, i, 0))],
            out_specs=pl.BlockSpec((1, tr, D_MODEL), lambda j, i, p: (j, i, 0))),
        out_shape=jax.ShapeDtypeStruct((N_CHIPS, hr, D_MODEL), bf16),
        compiler_params=_cparams("parallel", "parallel"),
    )(place, gb, recv)


def _rs_sum_chips(place, gb, recv_sib, recv_chips, name):
    hr = gb.shape[2]
    tr = min(hr, RS_TR_MAX)
    nt = hr // tr

    def body(pl_ref, g_ref, r1_ref, r2_ref, o_ref):
        acc = g_ref[0, 0].astype(f32) + r1_ref[0].astype(f32)
        for k in range(N_CHIPS - 1):
            acc = acc + r2_ref[k].astype(f32)
        o_ref[...] = acc

    return pl.pallas_call(
        body, name=name,
        grid_spec=pltpu.PrefetchScalarGridSpec(
            num_scalar_prefetch=1, grid=(nt,),
            in_specs=[pl.BlockSpec((1, 1, tr, D_MODEL), lambda i, p: (p[1], p[0], i, 0)),
                      pl.BlockSpec((1, tr, D_MODEL), lambda i, p: (p[1], i, 0)),
                      pl.BlockSpec((N_CHIPS - 1, tr, D_MODEL), lambda i, p: (0, i, 0))],
            out_specs=pl.BlockSpec((tr, D_MODEL), lambda i, p: (p[0] * nt + i, 0))),
        out_shape=jax.ShapeDtypeStruct((2 * hr, D_MODEL), f32),
        compiler_params=_cparams("parallel"),
    )(place, gb, recv_sib, recv_chips)


def _pad128(v):
    v = v.reshape(1, -1)
    return jnp.pad(v, ((0, 0), (0, 128 - v.shape[1])))


def _row(v):
    return v.reshape(1, -1)


W_COLS = PROJ_WIDTH // N_CHIPS


def _g_in_blocks(gz, gxbc, gdt, gq, gkv):
    g = jnp.concatenate([gz, gxbc, gdt[:W_DT_ROWS], gq, gkv], axis=0)
    return jnp.pad(g.reshape(N_CHIPS, W_COLS, D_MODEL), ((0, 0), (0, D_MODEL - W_COLS), (0, 0)))


def kernel(x, c, ln_in_g, ln_in_b, ada_w, ada_b, w_in, conv_w, conv_b, dt_bias, a_log, d_skip, ssd_norm_w, attn_sinks, w_out, ln1_g, ln1_b, w_ff1, b_ff1, w_ff2, b_ff2, ln2_g, ln2_b, loss_target, m_ln_in_g, m_ln_in_b, m_ada_w, m_ada_b, m_w_in, m_conv_w, m_conv_b, m_dt_bias, m_a_log, m_d_skip, m_ssd_norm_w, m_attn_sinks, m_w_out, m_ln1_g, m_ln1_b, m_w_ff1, m_b_ff1, m_w_ff2, m_b_ff2, m_ln2_g, m_ln2_b, v_ln_in_g, v_ln_in_b, v_ada_w, v_ada_b, v_w_in, v_conv_w, v_conv_b, v_dt_bias, v_a_log, v_d_skip, v_ssd_norm_w, v_attn_sinks, v_w_out, v_ln1_g, v_ln1_b, v_w_ff1, v_b_ff1, v_w_ff2, v_b_ff2, v_ln2_g, v_ln2_b):
    xi, yi, ci = _place()
    chip = 2 * xi + yi
    place = jnp.stack([ci, chip]).astype(jnp.int32)
    x2, tgt = x[0], loss_target[0]

    def as_rows(a):
        return jnp.transpose(a, (2, 0, 1))

    def from_rows(a):
        return jnp.transpose(a, (1, 2, 0))

    cond = jnp.concatenate([c.reshape(2, COND_LANES), conv_w.reshape(3, COND_LANES), jnp.zeros((3, COND_LANES), f32)], axis=0)
    ada_b_mine = lax.dynamic_slice(ada_b, (0, chip * ADA_COLS), (1, ADA_COLS))
    blob_in = jnp.pad(w_in[0].T, ((0, D_MODEL - W_COLS), (0, 0))).astype(bf16)
    cond_all, mod_rows, wall_in = _prologue(cond, ada_w[0], ada_b_mine, blob_in)
    cond_all = cond_all.reshape(N_DEV, 8, COND_LANES)
    c_all = cond_all[:, 0:2].reshape(N_DEV, D_MODEL)
    conv_w_full = jnp.concatenate([cond_all[2 * j, 2:5].reshape(CONV_K, 384) for j in range(N_CHIPS)], axis=1)
    mod = jnp.concatenate([mod_rows[:, 0].reshape(6, D_MODEL), jnp.zeros((2, D_MODEL), f32)], axis=0)
    w_in_f = wall_in[:, :W_COLS].reshape(PROJ_WIDTH, D_MODEL)
    b_ff1w, b_ff2w, b_outw = w_ff1[0].astype(bf16), w_ff2[0].astype(bf16), w_out[0].astype(bf16)

    def with_mine(wall, mine):
        return lax.dynamic_update_slice(wall, mine[None], (chip, 0, 0))

    e_mat = _head_expand()
    dsk_x = jnp.repeat(d_skip[0], HEAD_DIM).reshape(1, SSD_WIDTH)
    dtb, alog = _pad128(dt_bias), _pad128(a_log)
    sinks = attn_sinks[0]
    lng, lnb = _row(ln_in_g), _row(ln_in_b)
    u1, z, xr, xc, q, kv, dtr, wall_ff1 = _inproj_fwd(x2, mod, lng, lnb, w_in_f, conv_w_full, conv_b, b_ff1w)
    y, prev_all, wall_out = _ssd_fwd(xc, dtr, dtb, alog, dsk_x, b_outw)
    o, lse, wall_ff2 = _attn_fwd(q, kv, sinks, b_ff2w)
    wall_ff1, wall_ff2, wall_out = with_mine(wall_ff1, b_ff1w), with_mine(wall_ff2, b_ff2w), with_mine(wall_out, b_outw)
    yn, mix, r1 = _outproj_fwd(y, z, o, x2, mod, lng, lnb, ssd_norm_w, wall_out)

    dr1, u2, s_act, da, df, acc_mlp, db1 = _mlp_fwd_bwd(r1, tgt, mod, ln1_g, ln1_b, ln2_g, ln2_b, wall_ff1, b_ff1, wall_ff2,
                                                        b_ff2)
    ga = jnp.zeros((N_CHIPS, GA_ROWS, D_MODEL), bf16)
    ga = _wgrad_blob(ga, u2, da, "wgrad_ff1", lambda t, n: (n, t))
    ga = _wgrad_blob(ga, s_act, df, "wgrad_ff2", lambda t, n: (t // 2, 2 + t % 2))
    ga = ga.reshape(N_CHIPS, 2, GA_ROWS // 2, D_MODEL)
    dy, dz, do, dmix, acc_out, a_sib = _outproj_bwd(dr1, mix, y, z, mod, ssd_norm_w, wall_out, ga)
    gc = jnp.zeros((N_CHIPS, GC_ROWS, D_MODEL), bf16)
    gc = _wgrad_blob(gc, yn, dmix, "wgrad_out_y", lambda t, n: (t, 0))
    gc = _wgrad_blob(gc, o, dmix, "wgrad_out_o", lambda t, n: (2 + t, 0))
    gc = gc.reshape(N_CHIPS, 2, GC_ROWS // 2, D_MODEL)
    a_pair = _rs_sum_pair(place, ga, a_sib, "rs_sum_pair_a")
    dq, dkv, dsink, a_chips, c_sib = _attn_bwd(q, kv, do, lse, sinks, a_pair, gc)
    g_a = _rs_sum_chips(place, ga, a_sib, a_chips, "rs_sum_chips_a")
    c_pair = _rs_sum_pair(place, gc, c_sib, "rs_sum_pair_c")
    dxc, ddt, acc_ssd, dd_x, g_a, c_chips = _ssd_bwd(xc, dtr, dy, prev_all, dtb, alog, dsk_x, e_mat, g_a, c_pair)
    g_c = _rs_sum_chips(place, gc, c_sib, c_chips, "rs_sum_chips_c")
    dxr, acc_conv, g_c = _conv_bwd(dxc, xr, conv_w_full, conv_b, g_c)
    gb = _g_in_blocks(_wgrad(dz, u1, "wgrad_in_z"), _wgrad(dxr, u1, "wgrad_in_xbc"),
                      _wgrad(ddt.astype(bf16), u1, "wgrad_in_dt"), _wgrad(dq, u1, "wgrad_in_q"),
                      _wgrad(dkv, u1, "wgrad_in_kv")).reshape(N_CHIPS, 2, GB_ROWS // 2, D_MODEL)
    b_sib = _rs_to_sibling(gb)
    b_pair = _rs_sum_pair(place, gb, b_sib, "rs_sum_pair_b")
    grad_x, acc_in, b_chips = _inproj_bwd(dz, dxr, dq, dkv, ddt, dr1, x2, mod, lng, lnb, w_in_f, b_pair)
    g_b = _rs_share(_rs_sum_chips(place, gb, b_sib, b_chips, "rs_sum_chips_b"))

    packed = _small_pack(acc_in, acc_out, acc_mlp, db1, acc_conv, acc_ssd, dd_x, dsink, alog, e_mat)
    small_all = _allgather8(packed.reshape(8, SMALL_N // 8), "gather_small").reshape(N_DEV, SMALL_N)
    given = dict(ada_b=(ada_b, m_ada_b, v_ada_b), ln_in_g=(ln_in_g, m_ln_in_g, v_ln_in_g), ln_in_b=(ln_in_b, m_ln_in_b, v_ln_in_b),
                 conv_b=(conv_b, m_conv_b, v_conv_b), dt_bias=(dt_bias, m_dt_bias, v_dt_bias), a_log=(a_log, m_a_log, v_a_log),
                 d_skip=(d_skip, m_d_skip, v_d_skip), ssd_norm_w=(ssd_norm_w, m_ssd_norm_w, v_ssd_norm_w),
                 attn_sinks=(attn_sinks, m_attn_sinks, v_attn_sinks), ln1_g=(ln1_g, m_ln1_g, v_ln1_g),
                 ln1_b=(ln1_b, m_ln1_b, v_ln1_b), b_ff1=(b_ff1, m_b_ff1, v_b_ff1), b_ff2=(b_ff2, m_b_ff2, v_b_ff2),
                 ln2_g=(ln2_g, m_ln2_g, v_ln2_g), ln2_b=(ln2_b, m_ln2_b, v_ln2_b))
    upd = _small_update(small_all, *([_row(given[n][i]) for n in SMALL_PARAMS] for i in range(3)))
    small_res = {n: [t.reshape(given[n][0].shape) for t in upd[4 * j:4 * j + 4]] for j, n in enumerate(SMALL_PARAMS)}
    g_conv_all, loss_lanes = upd[4 * len(SMALL_PARAMS)], upd[4 * len(SMALL_PARAMS) + 1]
    loss = jnp.sum(loss_lanes)

    dmod_mine = lax.dynamic_slice(small_all[:, :6 * D_MODEL], (0, chip * ADA_COLS), (N_DEV, ADA_COLS))
    big = {"ada_w": [t[None] for t in _ada_bwd(c_all, dmod_mine, ada_w[0], m_ada_w[0], v_ada_w[0])]}

    g_conv = lax.dynamic_slice(g_conv_all.reshape(CONV_K, CONV_DIM), (0, chip * 384), (CONV_K, 384))
    big["conv_w"] = [t[None] for t in (g_conv, *_adamw(conv_w[0], g_conv, m_conv_w[0], v_conv_w[0], "adamw_conv_w"))]

    g_rows = g_b[:W_COLS].reshape(W_COLS, 1, D_MODEL)
    big["w_in"] = [from_rows(t) for t in (g_rows, *_adamw_rows(as_rows(w_in), g_rows, as_rows(m_w_in), as_rows(v_w_in)))]
    for name, g, (w, m, v) in (("w_out", g_c, (w_out, m_w_out, v_w_out)),
                               ("w_ff1", g_a[:D_MODEL], (w_ff1, m_w_ff1, v_w_ff1)),
                               ("w_ff2", g_a[D_MODEL:GA_ROWS], (w_ff2, m_w_ff2, v_w_ff2))):
        big[name] = [t[None] for t in (g, *_adamw(w[0], g, m[0], v[0], "adamw_" + name))]

    order = ("ln_in_g", "ln_in_b", "ada_w", "ada_b", "w_in", "conv_w", "conv_b", "dt_bias", "a_log", "d_skip", "ssd_norm_w",
             "attn_sinks", "w_out", "ln1_g", "ln1_b", "w_ff1", "b_ff1", "w_ff2", "b_ff2", "ln2_g", "ln2_b")
    res = {**small_res, **big}
    return (loss, grad_x[None], *[res[n][k] for k in range(4) for n in order])
```

```python
import functools
import math

import numpy as np
import jax
import jax.numpy as jnp
from jax import lax
from jax.experimental import pallas as pl
from jax.experimental.pallas import tpu as pltpu

f32 = jnp.float32
bf16 = jnp.bfloat16

D_MODEL = 1024
SSD_WIDTH = 1024
SSD_HEADS = 16
HEAD_DIM = 64
SSD_STATE = 128
SSD_GROUPS = 2
CHUNK = 128
CONV_K = 4
CONV_DIM = 1536
ATTN_HEADS = 16
D_FF = 4096
PROJ_WIDTH = 3856
ALPHA = 2.0 ** 0.25
LN_EPS = 1e-5
RMS_EPS = 1e-5
ATTN_SCALE = HEAD_DIM ** -0.5
NEG = -1e30

ADAM_LR = 0.001
ADAM_B1 = 0.9
ADAM_B2 = 0.999
ADAM_EPS = 1e-08
ADAM_WD = 0.01
ADAM_STEP = 10

W_Z, W_XBC, W_Q, W_KV = slice(0, 1024), slice(1024, 2560), slice(2576, 3600), slice(3600, 3856)
W_DT = slice(2560, 2688)
W_DT_ROWS = 16
GA_ROWS = 2048
GB_ROWS = 1024
GC_ROWS = 512
WG_TM = 512
STAGE_ROWS = 512
ADAMW_BLOCK_ELEMS = 1 << 18
MLP_TM = 256
DENSE_TM = 512
N_CHIPS = 4
N_DEV = 8
VMEM_LIMIT = 56 * 1024 * 1024
MESH = pl.DeviceIdType.MESH

ALIBI_SLOPES = tuple(2.0 ** (-8.0 / ATTN_HEADS * (i + 1)) for i in range(ATTN_HEADS))


def _cparams(*sem):
    return pltpu.CompilerParams(dimension_semantics=sem, vmem_limit_bytes=VMEM_LIMIT)


def _sigmoid(x):
    return 1.0 / (1.0 + jnp.exp(-x))


def _softplus(x):
    return jnp.maximum(x, 0.0) + jnp.log1p(jnp.exp(-jnp.abs(x)))


def _ln_stats(x):
    mu = jnp.mean(x, axis=-1, keepdims=True)
    xc = x - mu
    var = jnp.mean(xc * xc, axis=-1, keepdims=True)
    rstd = lax.rsqrt(var + LN_EPS)
    return xc * rstd, rstd


def _ln_bwd(dy, xhat, rstd, g):
    dxh = dy * g
    m1 = jnp.mean(dxh, axis=-1, keepdims=True)
    m2 = jnp.mean(dxh * xhat, axis=-1, keepdims=True)
    return rstd * (dxh - m1 - xhat * m2)


def _dot(a, b):
    return jnp.dot(a, b, preferred_element_type=f32)


def _dot_nt(a, b):
    return lax.dot_general(a, b, (((1,), (1,)), ((), ())), preferred_element_type=f32)


def _dot_tn(a, b):
    return lax.dot_general(a, b, (((0,), (0,)), ((), ())), preferred_element_type=f32)


def _dot_exact(a, b):
    return jnp.dot(a, b, preferred_element_type=f32, precision=lax.Precision.HIGHEST)


def _split3(v):
    hi = v.astype(bf16)
    r1 = v - hi.astype(f32)
    mid = r1.astype(bf16)
    lo = (r1 - mid.astype(f32)).astype(bf16)
    return hi, mid, lo


def _sel_dot(sel, v):
    hi, mid, lo = _split3(v)
    return _dot(sel, hi) + _dot(sel, mid) + _dot(sel, lo)


def _dot_sel_nt(v, sel):
    hi = v.astype(bf16)
    mid = (v - hi.astype(f32)).astype(bf16)
    return _dot_nt(hi, sel) + _dot_nt(mid, sel)


def _expand_heads(v):
    lane = lax.broadcasted_iota(jnp.int32, (v.shape[0], 128), 1)
    blocks = [jnp.where(lane < HEAD_DIM, v[:, 2 * b:2 * b + 1], v[:, 2 * b + 1:2 * b + 2]) for b in range(SSD_HEADS // 2)]
    return jnp.concatenate(blocks, axis=1)


def _full(shape):
    nd = len(shape)
    return pl.BlockSpec(shape, lambda *_: (0,) * nd)


def _resident(shape):
    nd = len(shape)
    return pl.BlockSpec(shape, lambda *_: (0,) * nd, pipeline_mode=pl.Buffered(1))


def _rows(tm, n):
    return pl.BlockSpec((tm, n), lambda i: (i, 0))


def _wall_rows(k):
    return pl.BlockSpec((N_CHIPS, D_MODEL, D_MODEL), lambda *_: (0, k, 0), pipeline_mode=pl.Buffered(1))


def _inproj_fwd(x, mod, ln_g, ln_b, w_in_t, conv_w, conv_b):
    L = x.shape[0]
    tm = DENSE_TM
    nt = L // tm

    def body(x_ref, mod_ref, g_ref, b_ref, w_ref, cw_ref, cb_ref,
             u1_ref, z_ref, xr_ref, xc_ref, q_ref, kv_ref, dt_ref, halo, buf):
        @pl.when(pl.program_id(0) == 0)
        def _():
            halo[...] = jnp.zeros_like(halo)

        xhat, _ = _ln_stats(x_ref[...])
        h0 = xhat * g_ref[...] + b_ref[...]
        u1 = (h0 * (1.0 + mod_ref[1:2, :]) + mod_ref[0:1, :]).astype(bf16)
        u1_ref[...] = u1
        z_ref[...] = _dot_nt(u1, w_ref[W_Z, :])
        xr = _dot_nt(u1, w_ref[W_XBC, :])
        xr_ref[...] = xr
        q_ref[...] = _dot_nt(u1, w_ref[W_Q, :]).astype(bf16)
        kv_ref[...] = _dot_nt(u1, w_ref[W_KV, :]).astype(bf16)
        dt_ref[...] = _dot_nt(u1, w_ref[W_DT, :])
        buf[0:8, :] = halo[...]
        buf[8:8 + tm, :] = xr
        pre = cb_ref[...] + cw_ref[0:1, :] * buf[5:5 + tm, :]
        for k in range(1, CONV_K):
            pre = pre + cw_ref[k:k + 1, :] * buf[5 + k:5 + k + tm, :]
        xc_ref[...] = pre * _sigmoid(pre)
        halo[...] = xr[tm - 8:tm, :]

    return pl.pallas_call(
        body, name="inproj_fwd", grid=(nt,),
        in_specs=[_rows(tm, D_MODEL), _full((8, D_MODEL)), _full((1, D_MODEL)), _full((1, D_MODEL)),
                  _resident((PROJ_WIDTH, D_MODEL)), _full((CONV_K, CONV_DIM)), _full((1, CONV_DIM))],
        out_specs=[_rows(tm, D_MODEL), _rows(tm, D_MODEL), _rows(tm, CONV_DIM), _rows(tm, CONV_DIM),
                   _rows(tm, D_MODEL), _rows(tm, 256), _rows(tm, 128)],
        out_shape=[jax.ShapeDtypeStruct((L, D_MODEL), bf16), jax.ShapeDtypeStruct((L, D_MODEL), f32),
                   jax.ShapeDtypeStruct((L, CONV_DIM), f32), jax.ShapeDtypeStruct((L, CONV_DIM), f32),
                   jax.ShapeDtypeStruct((L, D_MODEL), bf16), jax.ShapeDtypeStruct((L, 256), bf16),
                   jax.ShapeDtypeStruct((L, 128), f32)],
        scratch_shapes=[pltpu.VMEM((8, CONV_DIM), f32), pltpu.VMEM((tm + 8, CONV_DIM), f32)],
        compiler_params=_cparams("arbitrary"),
    )(x, mod, ln_g, ln_b, w_in_t, conv_w, conv_b)


def _head_expand():
    e = np.zeros((128, SSD_WIDTH), np.float32)
    for h in range(SSD_HEADS):
        e[h, h * HEAD_DIM:(h + 1) * HEAD_DIM] = 1.0
    return jnp.asarray(e, dtype=bf16)


def _ssd_chunk_common(dt_raw, dtb, a_row):
    T = CHUNK
    lane = lax.broadcasted_iota(jnp.int32, (T, 128), 1)
    dt = jnp.where(lane < SSD_HEADS, _softplus(dt_raw + dtb), 0.0)
    a = dt * a_row
    r = lax.broadcasted_iota(jnp.int32, (T, T), 0)
    c = lax.broadcasted_iota(jnp.int32, (T, T), 1)
    tril = (c <= r).astype(bf16)
    cum = _sel_dot(tril, a)
    dtx = _expand_heads(dt)
    cumx = _expand_heads(cum)
    return dt, a, cum, dtx, cumx, r, c


def _ssd_fwd(xc, dt_raw, dt_bias, a_log, d_skip_x, blob):
    L = xc.shape[0]
    nc = L // CHUNK
    T = CHUNK
    R = blob.shape[0]

    def body(xc_ref, dt_ref, dtb_ref, al_ref, dsk_ref, blob_ref, y_ref, prev_ref, wall_ref, st, send_sems, recv_sems):
        start, finish = _gather_job(blob_ref, wall_ref, send_sems, recv_sems, R)

        @pl.when(pl.program_id(0) == 0)
        def _():
            st[...] = jnp.zeros_like(st)
            start()

        @pl.when(pl.program_id(0) == nc - 1)
        def _():
            finish()

        a_row = -jnp.exp(al_ref[...])
        lane1 = lax.broadcasted_iota(jnp.int32, (1, 128), 1)
        a_row = jnp.where(lane1 < SSD_HEADS, a_row, 0.0)
        dt, a, cum, dtx, cumx, r, c = _ssd_chunk_common(dt_ref[...], dtb_ref[...], a_row)
        cum_t = cum.T
        ex = jnp.exp(cumx)
        last = cumx[T - 1:T, :]
        wx = jnp.exp(last - cumx)
        cdx = jnp.exp(last)
        xs = xc_ref[:, 0:SSD_WIDTH]
        X = xs * dtx
        Xb = X.astype(bf16)
        Xd = (X * wx).astype(bf16)
        prev = st[...]
        prev_ref[0] = prev
        prevb = prev.astype(bf16)
        tri = c <= r
        lane = lax.broadcasted_iota(jnp.int32, (T, 128), 1)
        y_blocks = []
        new_states = []
        for g in range(SSD_GROUPS):
            Bg = xc_ref[:, 1024 + 128 * g:1152 + 128 * g].astype(bf16)
            Cg = xc_ref[:, 1280 + 128 * g:1408 + 128 * g].astype(bf16)
            G = _dot_nt(Cg, Bg)
            yoff = _dot(Cg, prevb[:, 512 * g:512 * (g + 1)])
            new_states.append(_dot_tn(Bg, Xd[:, 512 * g:512 * (g + 1)]))
            for j in range(4):
                blk = 4 * g + j
                Xblk = Xb[:, 128 * blk:128 * (blk + 1)]
                ys = []
                for half in range(2):
                    h = 2 * blk + half
                    seg = jnp.minimum(cum[:, h:h + 1] - cum_t[h:h + 1, :], 0.0)
                    M = jnp.where(tri, G * jnp.exp(seg), 0.0).astype(bf16)
                    ys.append(_dot(M, Xblk))
                yd = jnp.where(lane < HEAD_DIM, ys[0], ys[1])
                sl = slice(128 * blk, 128 * (blk + 1))
                y_blocks.append(yd + ex[:, sl] * yoff[:, 128 * j:128 * (j + 1)] + dsk_ref[:, sl] * xs[:, sl])
        y_ref[...] = jnp.concatenate(y_blocks, axis=1)
        st[...] = prev * cdx + jnp.concatenate(new_states, axis=1)

    return pl.pallas_call(
        body, name="ssd_fwd", grid=(nc,),
        in_specs=[_rows(T, CONV_DIM), _rows(T, 128), _full((1, 128)), _full((1, 128)), _full((1, SSD_WIDTH)), _ANY_SPEC],
        out_specs=[_rows(T, SSD_WIDTH), pl.BlockSpec((1, SSD_STATE, SSD_WIDTH), lambda i: (i, 0, 0)), _ANY_SPEC],
        out_shape=[jax.ShapeDtypeStruct((L, SSD_WIDTH), f32), jax.ShapeDtypeStruct((nc, SSD_STATE, SSD_WIDTH), f32),
                   jax.ShapeDtypeStruct((N_CHIPS, R, D_MODEL), bf16)],
        scratch_shapes=[pltpu.VMEM((SSD_STATE, SSD_WIDTH), f32)] + _sems(6),
        compiler_params=_cparams("arbitrary"),
    )(xc, dt_raw, dt_bias, a_log, d_skip_x, blob)


def _kv_halves(kv_prev, kv_cur, first):
    kv = jnp.concatenate([jnp.where(first, 0.0, kv_prev.astype(f32)), kv_cur.astype(f32)], axis=0)
    lane = lax.broadcasted_iota(jnp.int32, (2 * CHUNK, 128), 1)
    lo = lane < HEAD_DIM
    out = []
    for g in range(2):
        per_half = []
        for half in range(2):
            both = []
            for t in (kv[:, 0:128], kv[:, 128:256]):
                src = t if g == half else pltpu.roll(t, HEAD_DIM, 1)
                both.append(jnp.where(lo if half == 0 else ~lo, src, 0.0).astype(bf16))
            per_half.append(tuple(both))
        out.append(per_half)
    return out


def _attn_masks(first):
    r = lax.broadcasted_iota(jnp.int32, (CHUNK, 2 * CHUNK), 0)
    c = lax.broadcasted_iota(jnp.int32, (CHUNK, 2 * CHUNK), 1)
    dist = r + CHUNK - c
    valid = (dist >= 0) & (dist < CHUNK) & ((c >= CHUNK) | jnp.logical_not(first))
    return dist.astype(f32), valid


def _head_stack(g, half, sink_ref):
    blks = [4 * g + i for i in range(4)]
    heads = [2 * b + half for b in blks]
    slope = jnp.concatenate([jnp.full((CHUNK, 1), ALIBI_SLOPES[h], f32) for h in heads], axis=0)
    sink = jnp.concatenate([jnp.full((CHUNK, 1), sink_ref[h], f32) for h in heads], axis=0)
    return blks, heads, slope, sink


def _attn_fwd(q, kv, sinks, blob):
    L = q.shape[0]
    nb = L // CHUNK
    T = CHUNK
    R = blob.shape[0]

    def body(sink_ref, q_ref, kvp_ref, kvc_ref, blob_ref, o_ref, lse_ref, wall_ref, send_sems, recv_sems):
        first = pl.program_id(0) == 0
        start, finish = _gather_job(blob_ref, wall_ref, send_sems, recv_sems, R)

        @pl.when(first)
        def _():
            start()

        @pl.when(pl.program_id(0) == nb - 1)
        def _():
            finish()

        ext = _kv_halves(kvp_ref[...], kvc_ref[...], first)
        dist, valid = _attn_masks(first)
        lane = lax.broadcasted_iota(jnp.int32, (T, 128), 1)
        lse = jnp.zeros((T, 128), f32)
        o_blocks = []
        for blk in range(8):
            qb = q_ref[:, 128 * blk:128 * (blk + 1)]
            acc = None
            for half in range(2):
                h = 2 * blk + half
                k_ext, v_ext = ext[h // 8][half]
                s = _dot_nt(qb, k_ext) * ATTN_SCALE - ALIBI_SLOPES[h] * dist
                s = jnp.where(valid, s, NEG)
                sink = sink_ref[h]
                m = jnp.maximum(jnp.max(s, axis=-1, keepdims=True), sink)
                p = jnp.exp(s - m)
                den = jnp.sum(p, axis=-1, keepdims=True) + jnp.exp(sink - m)
                pn = (p * (1.0 / den)).astype(bf16)
                oh = _dot(pn, v_ext)
                acc = oh if acc is None else acc + oh
                lse = jnp.where(lane == h, m + jnp.log(den), lse)
            o_blocks.append(acc.astype(bf16))
        o_ref[...] = jnp.concatenate(o_blocks, axis=1)
        lse_ref[...] = lse

    return pl.pallas_call(
        body, name="attn_fwd", grid=(nb,),
        in_specs=[pl.BlockSpec(memory_space=pltpu.SMEM), _rows(T, D_MODEL),
                  pl.BlockSpec((T, 256), lambda i: (jnp.maximum(i - 1, 0), 0)), _rows(T, 256), _ANY_SPEC],
        out_specs=[_rows(T, D_MODEL), _rows(T, 128), _ANY_SPEC],
        out_shape=[jax.ShapeDtypeStruct((L, D_MODEL), bf16), jax.ShapeDtypeStruct((L, 128), f32),
                   jax.ShapeDtypeStruct((N_CHIPS, R, D_MODEL), bf16)],
        scratch_shapes=_sems(6),
        compiler_params=_cparams("arbitrary"),
    )(sinks, q, kv, kv, blob)


def _gated_norm(y, z, w):
    sz = _sigmoid(z)
    hg = y * (z * sz)
    ns, rss = [], []
    for g in range(SSD_GROUPS):
        hs = hg[:, 512 * g:512 * (g + 1)]
        rs = lax.rsqrt(jnp.mean(hs * hs, axis=-1, keepdims=True) + RMS_EPS)
        ns.append(hs * rs)
        rss.append(rs)
    n = jnp.concatenate(ns, axis=1)
    return n * w, n, rss, sz


def _outproj_fwd(y, z, o, x, mod, ln_g, ln_b, norm_w, w_out):
    L = x.shape[0]
    tm = DENSE_TM

    def body(y_ref, z_ref, o_ref, x_ref, mod_ref, g_ref, b_ref, nw_ref, w_ref, yn_ref, mix_ref, r1_ref):
        yn, _, _, _ = _gated_norm(y_ref[...], z_ref[...], nw_ref[...])
        ynb = yn.astype(bf16)
        yn_ref[...] = ynb
        mix = (_dot(ynb[:, 0:512], w_ref[0]) + _dot(ynb[:, 512:1024], w_ref[1])
               + _dot(o_ref[:, 0:512], w_ref[2]) + _dot(o_ref[:, 512:1024], w_ref[3]))
        mix_ref[...] = mix
        xhat, _ = _ln_stats(x_ref[...])
        h0 = xhat * g_ref[...] + b_ref[...]
        r1_ref[...] = ALPHA * h0 + (1.0 + mod_ref[2:3, :]) * mix

    v = _full((1, D_MODEL))
    return pl.pallas_call(
        body, name="outproj_fwd", grid=(L // tm,),
        in_specs=[_rows(tm, D_MODEL), _rows(tm, D_MODEL), _rows(tm, D_MODEL), _rows(tm, D_MODEL),
                  _full((8, D_MODEL)), v, v, v, _resident((N_CHIPS, 512, D_MODEL))],
        out_specs=[_rows(tm, D_MODEL)] * 3,
        out_shape=[jax.ShapeDtypeStruct((L, D_MODEL), bf16), jax.ShapeDtypeStruct((L, D_MODEL), f32),
                   jax.ShapeDtypeStruct((L, D_MODEL), f32)],
        compiler_params=_cparams("parallel"),
    )(y, z, o, x, mod, ln_g, ln_b, norm_w, w_out)


A_LN2G, A_LN2B, A_G2, A_B2, A_SC2, A_SH2, A_LN1G, A_LN1B, A_LOSS = range(9)


def _mlp_fwd_bwd(r1, target, mod, ln1_g, ln1_b, ln2_g, ln2_b, w1, b1, w2, b2):
    L = r1.shape[0]
    tm = MLP_TM
    nj = D_FF // 1024

    def body(r1_ref, t_ref, mod_ref, g1_ref, bb1_ref, g2_ref, bb2_ref, w1_ref, b1_ref, w2_ref, b2_ref,
             dr1_ref, u2_ref, s_ref, da_ref, df_ref, acc_ref, db1_ref, hr):
        @pl.when(pl.program_id(0) == 0)
        def _():
            acc_ref[...] = jnp.zeros_like(acc_ref)
            db1_ref[...] = jnp.zeros_like(db1_ref)

        sc2, sh2, gate2 = mod_ref[4:5, :], mod_ref[3:4, :], mod_ref[5:6, :]
        xhat1, rstd1 = _ln_stats(r1_ref[...])
        h1 = xhat1 * g1_ref[...] + bb1_ref[...]
        u2f = h1 * (1.0 + sc2) + sh2
        u2 = u2f.astype(bf16)
        u2_ref[...] = u2
        f = jnp.zeros((tm, D_MODEL), f32) + b2_ref[...]
        for j in range(nj):
            cs = slice(1024 * j, 1024 * (j + 1))
            a = _dot(u2, w1_ref[j]) + b1_ref[:, cs]
            hrj = jnp.maximum(a, 0.0)
            hr[:, cs] = hrj.astype(bf16)
            sj = (hrj * hrj).astype(bf16)
            s_ref[:, cs] = sj
            f = f + _dot(sj, w2_ref[j])
        r2 = ALPHA * h1 + (1.0 + gate2) * f
        xhat2, rstd2 = _ln_stats(r2)
        h2 = xhat2 * g2_ref[...] + bb2_ref[...]
        diff = h2 - t_ref[...]
        dh2 = diff * (1.0 / D_MODEL)

        def add(row, val):
            acc_ref[row:row + 1, :] += jnp.sum(val, axis=0, keepdims=True)

        add(A_LOSS, diff * diff * (0.5 / D_MODEL))
        add(A_LN2G, dh2 * xhat2)
        add(A_LN2B, dh2)
        dr2 = _ln_bwd(dh2, xhat2, rstd2, g2_ref[...])
        add(A_G2, dr2 * f)
        df = dr2 * (1.0 + gate2)
        add(A_B2, df)
        dfb = df.astype(bf16)
        df_ref[...] = dfb
        du2 = jnp.zeros((tm, D_MODEL), f32)
        for j in range(nj):
            cs = slice(1024 * j, 1024 * (j + 1))
            ds = _dot_nt(dfb, w2_ref[j])
            daj = ds * (2.0 * hr[:, cs].astype(f32))
            db1_ref[:, cs] += jnp.sum(daj, axis=0, keepdims=True)
            dajb = daj.astype(bf16)
            da_ref[:, cs] = dajb
            du2 = du2 + _dot_nt(dajb, w1_ref[j])
        add(A_SC2, du2 * h1)
        add(A_SH2, du2)
        dh1 = ALPHA * dr2 + du2 * (1.0 + sc2)
        add(A_LN1G, dh1 * xhat1)
        add(A_LN1B, dh1)
        dr1_ref[...] = _ln_bwd(dh1, xhat1, rstd1, g1_ref[...])

    v = _full((1, D_MODEL))
    return pl.pallas_call(
        body, name="mlp_fwd_bwd", grid=(L // tm,),
        in_specs=[_rows(tm, D_MODEL), _rows(tm, D_MODEL), _full((8, D_MODEL)), v, v, v, v,
                  _wall_rows(0), _full((1, D_FF)), _wall_rows(1), v],
        out_specs=[_rows(tm, D_MODEL), _rows(tm, D_MODEL), _rows(tm, D_FF), _rows(tm, D_FF), _rows(tm, D_MODEL),
                   _full((16, D_MODEL)), _full((1, D_FF))],
        out_shape=[jax.ShapeDtypeStruct((L, D_MODEL), f32), jax.ShapeDtypeStruct((L, D_MODEL), bf16),
                   jax.ShapeDtypeStruct((L, D_FF), bf16), jax.ShapeDtypeStruct((L, D_FF), bf16),
                   jax.ShapeDtypeStruct((L, D_MODEL), bf16), jax.ShapeDtypeStruct((16, D_MODEL), f32),
                   jax.ShapeDtypeStruct((1, D_FF), f32)],
        scratch_shapes=[pltpu.VMEM((tm, D_FF), bf16)],
        compiler_params=_cparams("arbitrary"),
    )(r1, target, mod, ln1_g, ln1_b, ln2_g, ln2_b, w1, b1, w2, b2)


def _wgrad(a, b, name):
    L, M = a.shape
    N = b.shape[1]
    tm = min(M, 512)
    tn = next(t for t in (1024, 768, 512, 256, 128) if N % t == 0)

    def body(a_ref, b_ref, o_ref):
        o_ref[...] = _dot_tn(a_ref[...], b_ref[...]).astype(bf16)

    return pl.pallas_call(
        body, name=name, grid=(M // tm, N // tn),
        in_specs=[pl.BlockSpec((L, tm), lambda i, j: (0, i)), pl.BlockSpec((L, tn), lambda i, j: (0, j))],
        out_specs=pl.BlockSpec((tm, tn), lambda i, j: (i, j)),
        out_shape=jax.ShapeDtypeStruct((M, N), bf16),
        compiler_params=_cparams("parallel", "parallel"),
    )(a, b)


def _wgrad_blob(blob, a, b, name, place_of):
    L, M = a.shape
    N = b.shape[1]

    def body(blob_ref, a_ref, b_ref, o_ref):
        o_ref[0] = _dot_tn(a_ref[...], b_ref[...]).astype(bf16)

    return pl.pallas_call(
        body, name=name, grid=(M // WG_TM, N // D_MODEL),
        in_specs=[pl.BlockSpec(memory_space=pl.ANY), pl.BlockSpec((L, WG_TM), lambda t, n: (0, t)),
                  pl.BlockSpec((L, D_MODEL), lambda t, n: (0, n))],
        out_specs=pl.BlockSpec((1, WG_TM, D_MODEL), lambda t, n: (*place_of(t, n), 0)),
        out_shape=jax.ShapeDtypeStruct(blob.shape, bf16), input_output_aliases={0: 0},
        compiler_params=_cparams("parallel", "parallel"),
    )(blob, a, b)


def _outproj_bwd(dr1, mix, y, z, mod, norm_w, w_out, gb):
    L = dr1.shape[0]
    tm = DENSE_TM
    nt = L // tm

    def body(dr1_ref, mix_ref, y_ref, z_ref, mod_ref, nw_ref, w_ref, gb_ref,
             dy_ref, dz_ref, do_ref, dmix_ref, acc_ref, sib_ref, send_sems, recv_sems):
        start, wait = _to_sibling_job(gb_ref, sib_ref, send_sems, recv_sems)

        @pl.when(pl.program_id(0) == 0)
        def _():
            acc_ref[...] = jnp.zeros_like(acc_ref)
            start()

        @pl.when(pl.program_id(0) == nt - 1)
        def _():
            wait()

        dr1 = dr1_ref[...]
        acc_ref[0:1, :] += jnp.sum(dr1 * mix_ref[...], axis=0, keepdims=True)
        dmix = (dr1 * (1.0 + mod_ref[2:3, :])).astype(bf16)
        dmix_ref[...] = dmix
        dyn = jnp.concatenate([_dot_nt(dmix, w_ref[0]), _dot_nt(dmix, w_ref[1])], axis=1)
        do_ref[...] = jnp.concatenate([_dot_nt(dmix, w_ref[2]), _dot_nt(dmix, w_ref[3])], axis=1).astype(bf16)
        yv, zv = y_ref[...], z_ref[...]
        _, n, rss, sz = _gated_norm(yv, zv, nw_ref[...])
        acc_ref[1:2, :] += jnp.sum(dyn * n, axis=0, keepdims=True)
        dn = dyn * nw_ref[...]
        parts = []
        for g in range(SSD_GROUPS):
            sl = slice(512 * g, 512 * (g + 1))
            dng, ng = dn[:, sl], n[:, sl]
            parts.append(rss[g] * (dng - ng * jnp.mean(dng * ng, axis=-1, keepdims=True)))
        dhg = jnp.concatenate(parts, axis=1)
        dy_ref[...] = dhg * (zv * sz)
        dz_ref[...] = (dhg * yv * (sz * (1.0 + zv * (1.0 - sz)))).astype(bf16)

    return pl.pallas_call(
        body, name="outproj_bwd", grid=(nt,),
        in_specs=[_rows(tm, D_MODEL)] * 4 + [_full((8, D_MODEL)), _full((1, D_MODEL)), _resident((N_CHIPS, 512, D_MODEL)),
                  _ANY_SPEC],
        out_specs=[_rows(tm, D_MODEL)] * 4 + [_full((8, D_MODEL)), _ANY_SPEC],
        out_shape=[jax.ShapeDtypeStruct((L, D_MODEL), f32)] + [jax.ShapeDtypeStruct((L, D_MODEL), bf16)] * 3
        + [jax.ShapeDtypeStruct((8, D_MODEL), f32), jax.ShapeDtypeStruct((N_CHIPS,) + gb.shape[2:], bf16)],
        scratch_shapes=_sems(N_CHIPS),
        compiler_params=_cparams("arbitrary"),
    )(dr1, mix, y, z, mod, norm_w, w_out, gb)


def _attn_bwd(q, kv, do, lse, sinks, pb, gb2):
    L = q.shape[0]
    nb = L // CHUNK
    T = CHUNK

    def body(sink_ref, q_ref, kvp_ref, kvc_ref, do_ref, lse_ref, pb_ref, gb2_ref, dq_ref, dkv_ref, dsink_ref, chips_ref,
             sib2_ref, carry, send_sems, recv_sems, send_sems2, recv_sems2):
        n = pl.program_id(0)
        start, wait = _to_chips_job(pb_ref, chips_ref, send_sems, recv_sems)
        start2, wait2 = _to_sibling_job(gb2_ref, sib2_ref, send_sems2, recv_sems2)

        @pl.when(n == 0)
        def _():
            carry[...] = jnp.zeros_like(carry)
            dsink_ref[...] = jnp.zeros_like(dsink_ref)
            start()
            start2()

        @pl.when(n < nb)
        def _():
            first = n == 0
            ext = _kv_halves(kvp_ref[...], kvc_ref[...], first)
            dist, valid = _attn_masks(first)
            dist4, valid4 = jnp.concatenate([dist] * 4, axis=0), jnp.concatenate([valid] * 4, axis=0)
            lane1 = lax.broadcasted_iota(jnp.int32, (1, 128), 1)
            lse = lse_ref[...]
            qts = [q_ref[:, 128 * b:128 * (b + 1)].astype(f32).T.astype(bf16) for b in range(8)]
            dots = [do_ref[:, 128 * b:128 * (b + 1)].astype(f32).T.astype(bf16) for b in range(8)]
            acck = [None, None]
            accv = [None, None]
            dsink = jnp.zeros((1, 128), f32)
            dq_acc = [None] * 8
            for g in range(2):
                for half in range(2):
                    k_ext, v_ext = ext[g][half]
                    blks, heads, slope, sink = _head_stack(g, half, sink_ref)
                    qs = jnp.concatenate([q_ref[:, 128 * b:128 * (b + 1)] for b in blks], axis=0)
                    dos = jnp.concatenate([do_ref[:, 128 * b:128 * (b + 1)] for b in blks], axis=0)
                    rows = slice(HEAD_DIM * half, HEAD_DIM * (half + 1))
                    qt = jnp.concatenate([qts[b][rows, :] for b in blks], axis=1)
                    dot_ = jnp.concatenate([dots[b][rows, :] for b in blks], axis=1)
                    lse_col = jnp.concatenate([lse[:, h:h + 1] for h in heads], axis=0)
                    s = _dot_nt(qs, k_ext) * ATTN_SCALE - slope * dist4
                    p = jnp.where(valid4, jnp.exp(s - lse_col), 0.0)
                    dp = _dot_nt(dos, v_ext)
                    delta = jnp.sum(p * dp, axis=-1, keepdims=True)
                    ds = (p * (dp - delta) * ATTN_SCALE).astype(bf16)
                    sd = jnp.exp(sink - lse_col) * delta
                    dqs = _dot(ds, k_ext)
                    for i, b in enumerate(blks):
                        seg = slice(T * i, T * (i + 1))
                        dq_acc[b] = dqs[seg, :] if dq_acc[b] is None else dq_acc[b] + dqs[seg, :]
                        dsink = dsink - jnp.where(lane1 == heads[i], jnp.sum(sd[seg, :], axis=0, keepdims=True), 0.0)
                    dk = _dot(qt, ds)
                    dv = _dot(dot_, p.astype(bf16))
                    acck[g] = dk if acck[g] is None else acck[g] + dk
                    accv[g] = dv if accv[g] is None else accv[g] + dv
            dq_ref[...] = jnp.concatenate([a.astype(bf16) for a in dq_acc], axis=1)
            dsink_ref[...] += dsink
            dkv = jnp.concatenate([jnp.concatenate(acck, axis=0).T, jnp.concatenate(accv, axis=0).T], axis=1)
            dkv_ref[...] = (carry[...] + dkv[0:T, :]).astype(bf16)
            carry[...] = dkv[T:2 * T, :]

        @pl.when(n == nb)
        def _():
            dkv_ref[...] = carry[...].astype(bf16)
            wait()
            wait2()

    cur = lambda i: (jnp.minimum(i, nb - 1), 0)
    return pl.pallas_call(
        body, name="attn_bwd", grid=(nb + 1,),
        in_specs=[pl.BlockSpec(memory_space=pltpu.SMEM), pl.BlockSpec((T, D_MODEL), cur),
                  pl.BlockSpec((T, 256), lambda i: (jnp.maximum(jnp.minimum(i, nb - 1) - 1, 0), 0)),
                  pl.BlockSpec((T, 256), cur), pl.BlockSpec((T, D_MODEL), cur), pl.BlockSpec((T, 128), cur), _ANY_SPEC,
                  _ANY_SPEC],
        out_specs=[pl.BlockSpec((T, D_MODEL), cur), pl.BlockSpec((T, 256), lambda i: (jnp.maximum(i - 1, 0), 0)),
                   _full((1, 128)), _ANY_SPEC, _ANY_SPEC],
        out_shape=[jax.ShapeDtypeStruct((L, D_MODEL), bf16), jax.ShapeDtypeStruct((L, 256), bf16),
                   jax.ShapeDtypeStruct((1, 128), f32), jax.ShapeDtypeStruct((N_CHIPS - 1,) + pb.shape[1:], bf16),
                   jax.ShapeDtypeStruct((N_CHIPS,) + gb2.shape[2:], bf16)],
        scratch_shapes=[pltpu.VMEM((T, 256), f32)] + _sems(N_CHIPS - 1) + _sems(N_CHIPS),
        compiler_params=_cparams("arbitrary"),
    )(sinks, q, kv, kv, do, lse, pb, gb2)


def _ssd_bwd(xc, dt_raw, dy, prev_all, dt_bias, a_log, d_skip_x, e_mat, g, pb2):
    L = xc.shape[0]
    nc = L // CHUNK
    T = CHUNK
    RG = g.shape[0]

    def body(xc_ref, dt_ref, dy_ref, prev_ref, dtb_ref, al_ref, dsk_ref, e_ref, g_in_ref, pb2_ref,
             dxc_ref, ddt_ref, acc_ref, dd_ref, g_ref, chips2_ref, dst, dxs_s, send_sems, recv_sems, send_sems2, recv_sems2):
        start, wait = _share_job(g_ref, send_sems, recv_sems, RG)
        start2, wait2 = _to_chips_job(pb2_ref, chips2_ref, send_sems2, recv_sems2)

        @pl.when(pl.program_id(0) == 0)
        def _():
            dst[...] = jnp.zeros_like(dst)
            acc_ref[...] = jnp.zeros_like(acc_ref)
            dd_ref[...] = jnp.zeros_like(dd_ref)
            start()
            start2()

        @pl.when(pl.program_id(0) == nc - 1)
        def _():
            wait()
            wait2()

        lane1 = lax.broadcasted_iota(jnp.int32, (1, 128), 1)
        a_row = jnp.where(lane1 < SSD_HEADS, -jnp.exp(al_ref[...]), 0.0)
        e_mat_v = e_ref[...]
        dt, a, cum, dtx, cumx, r, c = _ssd_chunk_common(dt_ref[...], dtb_ref[...], a_row)
        cum_t = cum.T
        ex = jnp.exp(cumx)
        last = cumx[T - 1:T, :]
        wx = jnp.exp(last - cumx)
        cdx = jnp.exp(last)
        xs = xc_ref[:, 0:SSD_WIDTH]
        X = xs * dtx
        Xb = X.astype(bf16)
        Xdb = (X * wx).astype(bf16)
        dyv = dy_ref[...]
        prev = prev_ref[0]
        prevb = prev.astype(bf16)
        dnew = dst[...]
        dnewb = dnew.astype(bf16)
        tri = c <= r
        lane = lax.broadcasted_iota(jnp.int32, (T, 128), 1)
        sub = lax.broadcasted_iota(jnp.int32, (128, T), 0)
        lo = lane < HEAD_DIM

        def red(vals, g):
            return _dot_sel_nt(vals, e_mat_v[:, 512 * g:512 * (g + 1)])

        de = jnp.zeros((T, 128), f32)
        dw = jnp.zeros((T, 128), f32)
        ddt_x = jnp.zeros((T, 128), f32)
        dcum_col = jnp.zeros((T, 128), f32)
        dcum_row = jnp.zeros((128, T), f32)
        dprev_parts, dBs, dCs = [], [], []
        for g in range(SSD_GROUPS):
            s5 = slice(512 * g, 512 * (g + 1))
            Bg = xc_ref[:, 1024 + 128 * g:1152 + 128 * g].astype(bf16)
            Cg = xc_ref[:, 1280 + 128 * g:1408 + 128 * g].astype(bf16)
            G = _dot_nt(Cg, Bg)
            Z = _dot(Cg, prevb[:, s5])
            dyg = dyv[:, s5]
            dZb = (dyg * ex[:, s5]).astype(bf16)
            dXd = _dot(Bg, dnewb[:, s5])
            dC = _dot_nt(dZb, prevb[:, s5])
            dB = _dot_nt(Xdb[:, s5], dnewb[:, s5])
            dprev_parts.append(_dot_tn(Cg, dZb) + dnew[:, s5] * cdx[:, s5])
            de = de + red(dyg * Z, g)
            dw = dw + red(dXd * X[:, s5], g)
            dXg = dXd * wx[:, s5]
            dG = jnp.zeros((T, T), f32)
            for j in range(4):
                blk = 4 * g + j
                sl = slice(128 * blk, 128 * (blk + 1))
                Xblk = Xb[:, sl]
                dyblk = dyv[:, sl]
                dyblk_b = dyblk.astype(bf16)
                dxh = []
                for half in range(2):
                    h = 2 * blk + half
                    seg = jnp.minimum(cum[:, h:h + 1] - cum_t[h:h + 1, :], 0.0)
                    Lm = jnp.where(tri, jnp.exp(seg), 0.0)
                    M = G * Lm
                    dyh = jnp.where(lo if half == 0 else ~lo, dyblk, 0.0).astype(bf16)
                    dM = _dot_nt(dyh, Xblk)
                    dG = dG + dM * Lm
                    Q = dM * M
                    dcum_col = dcum_col + jnp.where(lane == h, jnp.sum(Q, axis=1, keepdims=True), 0.0)
                    dcum_row = dcum_row + jnp.where(sub == h, jnp.sum(Q, axis=0, keepdims=True), 0.0)
                    dxh.append(_dot_tn(M.astype(bf16), dyblk_b))
                dXblk = dXg[:, 128 * j:128 * (j + 1)] + jnp.where(lo, dxh[0], dxh[1])
                xsb = xs[:, sl]
                dxs_s[:, sl] = dXblk * dtx[:, sl] + dsk_ref[:, sl] * dyblk
                ddt_x = ddt_x + _dot_sel_nt(dXblk * xsb, e_mat_v[:, sl])
                dd_ref[:, sl] += jnp.sum(dyblk * xsb, axis=0, keepdims=True)
            dGb = dG.astype(bf16)
            dCs.append(dC + _dot(dGb, Bg))
            dBs.append(dB + _dot_tn(dGb, Cg))
        e16 = jnp.exp(cum)
        cum_last = cum[T - 1:T, :]
        w16 = jnp.exp(cum_last - cum)
        dcd = jnp.sum(dnew * prev, axis=0, keepdims=True)
        dcd16 = red(dcd[:, 0:512], 0) + red(dcd[:, 512:1024], 1)
        dww = dw * w16
        extra = jnp.sum(dww, axis=0, keepdims=True) + dcd16 * jnp.exp(cum_last)
        rowi = lax.broadcasted_iota(jnp.int32, (T, 128), 0)
        dcum = dcum_col - dcum_row.T + de * e16 - dww + jnp.where(rowi == T - 1, extra, 0.0)
        da = _sel_dot((c >= r).astype(bf16), dcum)
        ddt = ddt_x + da * a_row
        acc_ref[0:1, :] += jnp.sum(da * dt, axis=0, keepdims=True)
        ddt_raw = jnp.where(lane < SSD_HEADS, ddt * _sigmoid(dt_ref[...] + dtb_ref[...]), 0.0)
        ddt_ref[...] = ddt_raw
        acc_ref[1:2, :] += jnp.sum(ddt_raw, axis=0, keepdims=True)
        dxc_ref[:, 0:SSD_WIDTH] = dxs_s[...]
        dxc_ref[:, 1024:1280] = jnp.concatenate(dBs, axis=1)
        dxc_ref[:, 1280:1536] = jnp.concatenate(dCs, axis=1)
        dst[...] = jnp.concatenate(dprev_parts, axis=1)

    rev = lambda i: (nc - 1 - i, 0)
    return pl.pallas_call(
        body, name="ssd_bwd", grid=(nc,),
        in_specs=[pl.BlockSpec((T, CONV_DIM), rev), pl.BlockSpec((T, 128), rev), pl.BlockSpec((T, SSD_WIDTH), rev),
                  pl.BlockSpec((1, SSD_STATE, SSD_WIDTH), lambda i: (nc - 1 - i, 0, 0)),
                  _full((1, 128)), _full((1, 128)), _full((1, SSD_WIDTH)), _full((128, SSD_WIDTH)), _ANY_SPEC, _ANY_SPEC],
        out_specs=[pl.BlockSpec((T, CONV_DIM), rev), pl.BlockSpec((T, 128), rev), _full((8, 128)),
                   _full((1, SSD_WIDTH)), _ANY_SPEC, _ANY_SPEC],
        out_shape=[jax.ShapeDtypeStruct((L, CONV_DIM), f32), jax.ShapeDtypeStruct((L, 128), f32),
                   jax.ShapeDtypeStruct((8, 128), f32), jax.ShapeDtypeStruct((1, SSD_WIDTH), f32),
                   jax.ShapeDtypeStruct(g.shape, f32), jax.ShapeDtypeStruct((N_CHIPS - 1,) + pb2.shape[1:], bf16)],
        input_output_aliases={8: 4},
        scratch_shapes=[pltpu.VMEM((SSD_STATE, SSD_WIDTH), f32), pltpu.VMEM((T, SSD_WIDTH), f32)] + _sems(1)
        + _sems(N_CHIPS - 1),
        compiler_params=_cparams("arbitrary"),
    )(xc, dt_raw, dy, prev_all, dt_bias, a_log, d_skip_x, e_mat, g, pb2)


def _conv_bwd(dxc, xr, conv_w, conv_b, g):
    L = dxc.shape[0]
    tm = 256
    nt = L // tm
    RG = g.shape[0]

    def body(dxc_ref, xr_ref, xh_ref, cw_ref, cb_ref, g_in_ref, dxr_ref, acc_ref, g_ref, carry, buf, ext, send_sems, recv_sems):
        i = pl.program_id(0)
        start, wait = _share_job(g_ref, send_sems, recv_sems, RG)

        @pl.when(i == 0)
        def _():
            carry[...] = jnp.zeros_like(carry)
            acc_ref[...] = jnp.zeros_like(acc_ref)
            ext[tm + 16:tm + CHUNK, :] = jnp.zeros((CHUNK - 16, CONV_DIM), bf16)
            start()

        @pl.when(i == nt - 1)
        def _():
            wait()

        buf[0:8, :] = jnp.where(i == nt - 1, 0.0, xh_ref[...])
        u = xr_ref[...]
        buf[8:8 + tm, :] = u
        pre = cb_ref[...] + cw_ref[CONV_K - 1:CONV_K, :] * u
        for k in range(CONV_K - 1):
            pre = pre + cw_ref[k:k + 1, :] * buf[5 + k:5 + k + tm, :]
        sg = _sigmoid(pre)
        dpre = dxc_ref[...] * (sg * (1.0 + pre * (1.0 - sg)))
        acc_ref[4:5, :] += jnp.sum(dpre, axis=0, keepdims=True)
        dpb = dpre.astype(bf16)
        ext[0:tm, :] = dpb
        ext[tm:tm + 16, :] = carry[...]
        acc_ref[CONV_K - 1:CONV_K, :] += jnp.sum(u * dpre, axis=0, keepdims=True)
        du = cw_ref[CONV_K - 1:CONV_K, :] * dpre
        r = lax.broadcasted_iota(jnp.int32, (CHUNK, 2 * CHUNK), 0)
        c = lax.broadcasted_iota(jnp.int32, (CHUNK, 2 * CHUNK), 1)
        for j in range(1, CONV_K):
            move = (c == r + j).astype(bf16)
            up = jnp.concatenate([_dot(move, ext[CHUNK * b:CHUNK * (b + 2), :]) for b in range(tm // CHUNK)], axis=0)
            k = CONV_K - 1 - j
            du = du + cw_ref[k:k + 1, :] * up
            acc_ref[k:k + 1, :] += jnp.sum(u * up, axis=0, keepdims=True)
        dxr_ref[...] = du.astype(bf16)
        carry[...] = dpb[0:16, :]

    rev = lambda i: (nt - 1 - i, 0)
    return pl.pallas_call(
        body, name="conv_bwd", grid=(nt,),
        in_specs=[pl.BlockSpec((tm, CONV_DIM), rev), pl.BlockSpec((tm, CONV_DIM), rev),
                  pl.BlockSpec((8, CONV_DIM), lambda i: (jnp.maximum((nt - 1 - i) * (tm // 8) - 1, 0), 0)),
                  _full((CONV_K, CONV_DIM)), _full((1, CONV_DIM)), _ANY_SPEC],
        out_specs=[pl.BlockSpec((tm, CONV_DIM), rev), _full((8, CONV_DIM)), _ANY_SPEC],
        out_shape=[jax.ShapeDtypeStruct((L, CONV_DIM), bf16), jax.ShapeDtypeStruct((8, CONV_DIM), f32),
                   jax.ShapeDtypeStruct(g.shape, f32)],
        input_output_aliases={5: 2},
        scratch_shapes=[pltpu.VMEM((16, CONV_DIM), bf16), pltpu.VMEM((tm + 8, CONV_DIM), f32),
                        pltpu.VMEM((tm + CHUNK, CONV_DIM), bf16)] + _sems(1),
        compiler_params=_cparams("arbitrary"),
    )(dxc, xr, xr, conv_w, conv_b, g)


def _inproj_bwd(dz, dxr, dq, dkv, ddt, dr1, x, mod, ln_g, ln_b, w_in, pb):
    L = x.shape[0]
    tm = DENSE_TM
    nt = L // tm

    def body(dz_ref, dxr_ref, dq_ref, dkv_ref, ddt_ref, dr1_ref, x_ref, mod_ref, g_ref, b_ref, w_ref, pb_ref,
             dx_ref, acc_ref, chips_ref, send_sems, recv_sems):
        start, wait = _to_chips_job(pb_ref, chips_ref, send_sems, recv_sems)

        @pl.when(pl.program_id(0) == 0)
        def _():
            acc_ref[...] = jnp.zeros_like(acc_ref)
            start()

        @pl.when(pl.program_id(0) == nt - 1)
        def _():
            wait()

        du1 = (_dot(dz_ref[...], w_ref[W_Z, :]) + _dot(dxr_ref[...], w_ref[W_XBC, :])
               + _dot(dq_ref[...], w_ref[W_Q, :]) + _dot(dkv_ref[...], w_ref[W_KV, :])
               + _dot(ddt_ref[...].astype(bf16), w_ref[W_DT, :]))
        xhat, rstd = _ln_stats(x_ref[...])
        h0 = xhat * g_ref[...] + b_ref[...]
        acc_ref[0:1, :] += jnp.sum(du1 * h0, axis=0, keepdims=True)
        acc_ref[1:2, :] += jnp.sum(du1, axis=0, keepdims=True)
        dh0 = du1 * (1.0 + mod_ref[1:2, :]) + ALPHA * dr1_ref[...]
        acc_ref[2:3, :] += jnp.sum(dh0 * xhat, axis=0, keepdims=True)
        acc_ref[3:4, :] += jnp.sum(dh0, axis=0, keepdims=True)
        dx_ref[...] = _ln_bwd(dh0, xhat, rstd, g_ref[...])

    v = _full((1, D_MODEL))
    return pl.pallas_call(
        body, name="inproj_bwd", grid=(nt,),
        in_specs=[_rows(tm, D_MODEL), _rows(tm, CONV_DIM), _rows(tm, D_MODEL), _rows(tm, 256), _rows(tm, 128),
                  _rows(tm, D_MODEL), _rows(tm, D_MODEL), _full((8, D_MODEL)), v, v, _resident((PROJ_WIDTH, D_MODEL)),
                  _ANY_SPEC],
        out_specs=[_rows(tm, D_MODEL), _full((8, D_MODEL)), _ANY_SPEC],
        out_shape=[jax.ShapeDtypeStruct((L, D_MODEL), f32), jax.ShapeDtypeStruct((8, D_MODEL), f32),
                   jax.ShapeDtypeStruct((N_CHIPS - 1,) + pb.shape[1:], bf16)],
        scratch_shapes=_sems(N_CHIPS - 1),
        compiler_params=_cparams("arbitrary"),
    )(dz, dxr, dq, dkv, ddt, dr1, x, mod, ln_g, ln_b, w_in, pb)


def _adamw_math(w, g, m, v):
    m = ADAM_B1 * m + (1.0 - ADAM_B1) * g
    v = ADAM_B2 * v + (1.0 - ADAM_B2) * (g * g)
    m_hat = m / (1.0 - ADAM_B1 ** ADAM_STEP)
    v_hat = v / (1.0 - ADAM_B2 ** ADAM_STEP)
    delta = -ADAM_LR * (m_hat / (jnp.sqrt(v_hat) + ADAM_EPS) + ADAM_WD * w)
    return delta, m, v


def _adamw(w, g, m, v, name):
    R, C = w.shape

    def body(w_ref, g_ref, m_ref, v_ref, d_ref, m2_ref, v2_ref):
        d_ref[...], m2_ref[...], v2_ref[...] = _adamw_math(w_ref[...], g_ref[...], m_ref[...], v_ref[...])

    cap = max(8, ADAMW_BLOCK_ELEMS // C)
    tr = R if R <= cap else next(t for t in range(cap - cap % 8, 7, -8) if R % t == 0)
    spec = pl.BlockSpec((tr, C), lambda i: (i, 0))
    return pl.pallas_call(
        body, name=name, grid=(R // tr,), in_specs=[spec] * 4, out_specs=[spec] * 3,
        out_shape=[jax.ShapeDtypeStruct((R, C), f32)] * 3, compiler_params=_cparams("parallel"),
    )(w, g, m, v)


def _adamw_rows(w, g, m, v):
    R, _, C = w.shape
    tr = R // 4

    def body(w_ref, g_ref, m_ref, v_ref, d_ref, m2_ref, v2_ref):
        d_ref[...], m2_ref[...], v2_ref[...] = _adamw_math(w_ref[...], g_ref[...], m_ref[...], v_ref[...])

    spec = pl.BlockSpec((tr, 1, C), lambda i: (i, 0, 0))
    return pl.pallas_call(
        body, name="adamw_w_in", grid=(R // tr,), in_specs=[spec] * 4, out_specs=[spec] * 3,
        out_shape=[jax.ShapeDtypeStruct((R, 1, C), f32)] * 3, compiler_params=_cparams("parallel"),
    )(w, g, m, v)


ADA_COLS = 6 * D_MODEL // N_CHIPS
ADA_TN = 512


COND_LANES = 512


def _prologue(cond, ada_w, ada_b, blob):
    R = blob.shape[0]

    def body(cond_ref, w_ref, b_ref, blob_ref, call_ref, mod_ref, wall_ref, mod_s, stage, gs, gr, ms, mr, ws, wr, local_sem):
        x, y, c = _place()
        start_w, finish_w = _gather_job(blob_ref, wall_ref, ws, wr, R)

        def rows(ref, px, py, pc):
            return ref.at[pl.ds(pl.multiple_of((4 * px + 2 * py + pc) * 8, 8), 8), :]

        mine = pltpu.make_async_copy(cond_ref, rows(call_ref, x, y, c), local_sem)
        mine.start()
        sends = [_remote(cond_ref, rows(call_ref, x, y, c), gs, gr, m - 1, _flip(x, y, c, m)) for m in range(1, N_DEV)]
        for cp in sends:
            cp.start()
        for m in range(1, N_DEV):
            peer = _flip(x, y, c, m)
            _remote(cond_ref, rows(call_ref, *peer), gs, gr, m - 1, peer).wait_recv()
        for cp in sends:
            cp.wait_send()
        mine.wait()
        start_w()

        for k in range(R // STAGE_ROWS):
            part = pl.ds(STAGE_ROWS * k, STAGE_ROWS)
            cin = pltpu.make_async_copy(blob_ref.at[part, :], stage, local_sem)
            cin.start()
            cin.wait()
            cout = pltpu.make_async_copy(stage, wall_ref.at[2 * x + y, part, :], local_sem)
            cout.start()
            cout.wait()

        lo = jnp.concatenate([call_ref[8 * d:8 * d + 1, :] for d in range(N_DEV)], axis=0)
        hi = jnp.concatenate([call_ref[8 * d + 1:8 * d + 2, :] for d in range(N_DEV)], axis=0)
        mod_all = (_dot_exact(lo * _sigmoid(lo), w_ref[0:COND_LANES, :]) + _dot_exact(hi * _sigmoid(hi), w_ref[COND_LANES:, :])
                   + b_ref[...])
        for d in range(N_DEV):
            mod_s[8 * d:8 * d + 8, :] = jnp.broadcast_to(mod_all[d:d + 1, :], (8, ADA_COLS))

        mine = pltpu.make_async_copy(rows(mod_s, x, y, c), mod_ref.at[2 * x + y], local_sem)
        mine.start()
        sends = []
        for m in range(1, N_CHIPS):
            peer = _flip(x, y, c, 2 * m)
            sends.append(_remote(rows(mod_s, *peer), mod_ref.at[2 * x + y], ms, mr, m - 1, peer))
        for cp in sends:
            cp.start()
        for m in range(1, N_CHIPS):
            px, py, pc = _flip(x, y, c, 2 * m)
            _remote(rows(mod_s, x, y, c), mod_ref.at[2 * px + py], ms, mr, m - 1, (px, py, pc)).wait_recv()
        for cp in sends:
            cp.wait_send()
        mine.wait()
        finish_w()

    return pl.pallas_call(
        body, name="prologue",
        out_shape=[jax.ShapeDtypeStruct((8 * N_DEV, COND_LANES), f32), jax.ShapeDtypeStruct((N_CHIPS, 8, ADA_COLS), f32),
                   jax.ShapeDtypeStruct((N_CHIPS, R, D_MODEL), bf16)],
        in_specs=[_VMEM_SPEC, _VMEM_SPEC, _VMEM_SPEC, _ANY_SPEC], out_specs=[_VMEM_SPEC, _VMEM_SPEC, _ANY_SPEC],
        scratch_shapes=[pltpu.VMEM((8 * N_DEV, ADA_COLS), f32), pltpu.VMEM((STAGE_ROWS, D_MODEL), bf16)]
        + _sems(N_DEV - 1) + _sems(N_CHIPS - 1) + _sems(6) + [pltpu.SemaphoreType.DMA],
        compiler_params=pltpu.CompilerParams(vmem_limit_bytes=VMEM_LIMIT),
    )(cond, ada_w, ada_b, blob)


def _ada_bwd(c_all, dmod, w, m, v):
    def body(c_ref, d_ref, w_ref, m_ref, v_ref, g_ref, dl_ref, m2_ref, v2_ref):
        cv = c_ref[...]
        g = lax.dot_general(cv * _sigmoid(cv), d_ref[...], (((0,), (0,)), ((), ())), preferred_element_type=f32,
                            precision=lax.Precision.HIGHEST)
        g_ref[...] = g
        dl_ref[...], m2_ref[...], v2_ref[...] = _adamw_math(w_ref[...], g, m_ref[...], v_ref[...])

    wspec = pl.BlockSpec((D_MODEL, ADA_TN), lambda j: (0, j))
    return pl.pallas_call(
        body, name="ada_bwd", grid=(ADA_COLS // ADA_TN,),
        in_specs=[_full((N_DEV, D_MODEL)), pl.BlockSpec((N_DEV, ADA_TN), lambda j: (0, j)), wspec, wspec, wspec],
        out_specs=[wspec] * 4, out_shape=[jax.ShapeDtypeStruct((D_MODEL, ADA_COLS), f32)] * 4,
        compiler_params=_cparams("parallel"),
    )(c_all, dmod, w, m, v)


SMALL_SLOTS = (("ada_b", 6144), ("ln_in_g", 1024), ("ln_in_b", 1024), ("conv_b", 1536), ("dt_bias", 128), ("a_log", 128),
               ("d_skip", 128), ("ssd_norm_w", 1024), ("attn_sinks", 128), ("ln1_g", 1024), ("ln1_b", 1024),
               ("b_ff1", 4096), ("b_ff2", 1024), ("ln2_g", 1024), ("ln2_b", 1024), ("conv_w", 6144), ("loss", 1024))
SMALL_N = sum(n for _, n in SMALL_SLOTS)
SMALL_OFF = {name: sum(n for _, n in SMALL_SLOTS[:i]) for i, (name, _) in enumerate(SMALL_SLOTS)}
SMALL_PARAMS = tuple(name for name, _ in SMALL_SLOTS[:15])
assert SMALL_N % 1024 == 0


def _small_pack(acc_in, acc_out, acc_mlp, db1, acc_conv, acc_ssd, dd_x, dsink, alog, e_mat):
    def body(in_ref, out_ref, mlp_ref, db1_ref, conv_ref, ssd_ref, dd_ref, sink_ref, al_ref, e_ref, o_ref):
        def put(name, val, at=0):
            off = SMALL_OFF[name] + at
            o_ref[:, off:off + val.shape[1]] = val

        for k, row in enumerate((in_ref[1:2, :], in_ref[0:1, :], out_ref[0:1, :], mlp_ref[A_SH2:A_SH2 + 1, :],
                                 mlp_ref[A_SC2:A_SC2 + 1, :], mlp_ref[A_G2:A_G2 + 1, :])):
            put("ada_b", row, D_MODEL * k)
        put("ln_in_g", in_ref[2:3, :])
        put("ln_in_b", in_ref[3:4, :])
        put("conv_b", conv_ref[4:5, :])
        put("dt_bias", ssd_ref[1:2, :])
        put("a_log", ssd_ref[0:1, :] * (-jnp.exp(al_ref[...])))
        put("d_skip", _dot_sel_nt(jnp.broadcast_to(dd_ref[...], (8, SSD_WIDTH)), e_ref[...])[0:1, :])
        put("ssd_norm_w", out_ref[1:2, :])
        put("attn_sinks", sink_ref[...])
        put("ln1_g", mlp_ref[A_LN1G:A_LN1G + 1, :])
        put("ln1_b", mlp_ref[A_LN1B:A_LN1B + 1, :])
        put("b_ff1", db1_ref[...])
        put("b_ff2", mlp_ref[A_B2:A_B2 + 1, :])
        put("ln2_g", mlp_ref[A_LN2G:A_LN2G + 1, :])
        put("ln2_b", mlp_ref[A_LN2B:A_LN2B + 1, :])
        for k in range(CONV_K):
            put("conv_w", conv_ref[k:k + 1, :], CONV_DIM * k)
        put("loss", mlp_ref[A_LOSS:A_LOSS + 1, :])

    return pl.pallas_call(body, name="small_pack", out_shape=jax.ShapeDtypeStruct((1, SMALL_N), f32),
                          compiler_params=_cparams())(acc_in, acc_out, acc_mlp, db1, acc_conv, acc_ssd, dd_x, dsink, alog, e_mat)


def _small_update(gathered, params, moms, vels):
    k = len(SMALL_PARAMS)

    def body(g_ref, *refs):
        w_refs, m_refs, v_refs, outs = refs[:k], refs[k:2 * k], refs[2 * k:3 * k], refs[3 * k:]

        def total(name, n):
            off = SMALL_OFF[name]
            g = g_ref[0:1, off:off + n]
            for i in range(1, N_DEV):
                g = g + g_ref[i:i + 1, off:off + n]
            return g

        for j, name in enumerate(SMALL_PARAMS):
            n = w_refs[j].shape[1]
            g = total(name, max(n, 128))[:, :n]
            outs[4 * j][...] = g
            outs[4 * j + 1][...], outs[4 * j + 2][...], outs[4 * j + 3][...] = _adamw_math(
                w_refs[j][...], g, m_refs[j][...], v_refs[j][...])
        outs[4 * k][...] = total("conv_w", CONV_K * CONV_DIM)
        outs[4 * k + 1][...] = total("loss", D_MODEL)

    shapes = [jax.ShapeDtypeStruct(p.shape, f32) for p in params for _ in range(4)]
    shapes += [jax.ShapeDtypeStruct((1, CONV_K * CONV_DIM), f32), jax.ShapeDtypeStruct((1, D_MODEL), f32)]
    return pl.pallas_call(body, name="small_update", out_shape=shapes,
                          compiler_params=_cparams())(gathered, *params, *moms, *vels)


def _place():
    return lax.axis_index("x"), lax.axis_index("y"), lax.axis_index("c")


def _flip(x, y, c, m):
    return (1 - x if m & 4 else x, 1 - y if m & 2 else y, 1 - c if m & 1 else c)


_VMEM_SPEC = pl.BlockSpec(memory_space=pltpu.VMEM)
_ANY_SPEC = pl.BlockSpec(memory_space=pl.ANY)


def _allgather8(v, name):
    n = v.shape[1]

    def body(v_ref, out_ref, send_sems, recv_sems, local_sem):
        x, y, c = _place()

        def rows(px, py, pc):
            return out_ref.at[pl.ds(pl.multiple_of((4 * px + 2 * py + pc) * 8, 8), 8), :]

        def copy(m, src, dst, to):
            return pltpu.make_async_remote_copy(src_ref=src, dst_ref=dst, send_sem=send_sems.at[m - 1],
                                                recv_sem=recv_sems.at[m - 1], device_id=to, device_id_type=MESH)

        mine = pltpu.make_async_copy(v_ref, rows(x, y, c), local_sem)
        mine.start()
        sends = [copy(m, v_ref, rows(x, y, c), _flip(x, y, c, m)) for m in range(1, N_DEV)]
        for cp in sends:
            cp.start()
        for m in range(1, N_DEV):
            peer = _flip(x, y, c, m)
            copy(m, v_ref, rows(*peer), peer).wait_recv()
        for cp in sends:
            cp.wait_send()
        mine.wait()

    return pl.pallas_call(
        body, name=name, out_shape=jax.ShapeDtypeStruct((8 * N_DEV, n), f32), in_specs=[_VMEM_SPEC],
        out_specs=_VMEM_SPEC,
        scratch_shapes=[pltpu.SemaphoreType.DMA((N_DEV - 1,)), pltpu.SemaphoreType.DMA((N_DEV - 1,)),
                        pltpu.SemaphoreType.DMA],
    )(v)


def _remote(src, dst, send_sems, recv_sems, k, to):
    return pltpu.make_async_remote_copy(src_ref=src, dst_ref=dst, send_sem=send_sems.at[k], recv_sem=recv_sems.at[k],
                                        device_id=to, device_id_type=MESH)


def _gather_job(blob_ref, out_ref, send_sems, recv_sems, R):
    x, y, c = _place()
    sib = (x, y, 1 - c)
    hr = R // 2

    def half(px, py, pc):
        return out_ref.at[2 * px + py, pl.ds(pl.multiple_of(pc * hr, 16), hr), :]

    my_half = blob_ref.at[pl.ds(pl.multiple_of(c * hr, 16), hr), :]

    def first():
        return [_remote(my_half, half(x, y, c), send_sems, recv_sems, m - 1, _flip(x, y, c, 2 * m))
                for m in range(1, N_CHIPS)]

    def start():
        for cp in first():
            cp.start()

    def finish():
        passed = []
        for m in range(1, N_CHIPS):
            px, py, pc = _flip(x, y, c, 2 * m)
            _remote(my_half, half(px, py, pc), send_sems, recv_sems, m - 1, (px, py, pc)).wait_recv()
            fwd = _remote(half(px, py, pc), half(px, py, pc), send_sems, recv_sems, 2 + m, sib)
            fwd.start()
            passed.append(fwd)
        for m in range(1, N_CHIPS):
            px, py, pc = _flip(x, y, c, 2 * m)
            _remote(my_half, half(px, py, 1 - pc), send_sems, recv_sems, 2 + m, sib).wait_recv()
        for cp in first() + passed:
            cp.wait_send()

    return start, finish


def _to_sibling_job(g_ref, out_ref, send_sems, recv_sems):
    x, y, c = _place()

    def cps():
        return [_remote(g_ref.at[j, 1 - c], out_ref.at[j], send_sems, recv_sems, j, (x, y, 1 - c)) for j in range(N_CHIPS)]

    def start():
        for cp in cps():
            cp.start()

    def wait():
        for cp in cps():
            cp.wait()

    return start, wait


def _to_chips_job(p_ref, out_ref, send_sems, recv_sems):
    x, y, c = _place()

    def cps():
        out = []
        for m in range(1, N_CHIPS):
            px, py, pc = _flip(x, y, c, 2 * m)
            out.append(_remote(p_ref.at[2 * px + py], out_ref.at[m - 1], send_sems, recv_sems, m - 1, (px, py, pc)))
        return out

    def start():
        for cp in cps():
            cp.start()

    def wait():
        for cp in cps():
            cp.wait()

    return start, wait


def _share_job(g_ref, send_sems, recv_sems, R):
    x, y, c = _place()

    def rows(pc):
        return g_ref.at[pl.ds(pl.multiple_of(pc * (R // 2), 8), R // 2), :]

    def start():
        _remote(rows(c), rows(c), send_sems, recv_sems, 0, (x, y, 1 - c)).start()

    def wait():
        _remote(rows(c), rows(1 - c), send_sems, recv_sems, 0, (x, y, 1 - c)).wait_recv()
        _remote(rows(c), rows(c), send_sems, recv_sems, 0, (x, y, 1 - c)).wait_send()

    return start, wait


def _sems(n):
    return [pltpu.SemaphoreType.DMA((n,)), pltpu.SemaphoreType.DMA((n,))]


def _rs_to_sibling(gb):
    def body(g_ref, out_ref, send_sems, recv_sems):
        start, wait = _to_sibling_job(g_ref, out_ref, send_sems, recv_sems)
        start()
        wait()

    return pl.pallas_call(
        body, name="rs_to_sibling", out_shape=jax.ShapeDtypeStruct((N_CHIPS,) + gb.shape[2:], bf16),
        in_specs=[_ANY_SPEC], out_specs=_ANY_SPEC, scratch_shapes=_sems(N_CHIPS),
    )(gb)


def _rs_share(g):
    R = g.shape[0]

    def body(g_ref, out_ref, send_sems, recv_sems):
        start, wait = _share_job(out_ref, send_sems, recv_sems, R)
        start()
        wait()

    return pl.pallas_call(
        body, name="rs_share", out_shape=jax.ShapeDtypeStruct(g.shape, f32), in_specs=[_ANY_SPEC],
        out_specs=_ANY_SPEC, input_output_aliases={0: 0}, scratch_shapes=_sems(1),
    )(g)


RS_TR_MAX = 512


def _rs_sum_pair(place, gb, recv, name):
    hr = gb.shape[2]
    tr = min(hr, RS_TR_MAX)

    def body(pl_ref, g_ref, r_ref, o_ref):
        o_ref[0] = (g_ref[0, 0].astype(f32) + r_ref[0].astype(f32)).astype(bf16)

    return pl.pallas_call(
        body, name=name,
        grid_spec=pltpu.PrefetchScalarGridSpec(
            num_scalar_prefetch=1, grid=(N_CHIPS, hr // tr),
            in_specs=[pl.BlockSpec((1, 1, tr, D_MODEL), lambda j, i, p: (j, p[0], i, 0)),
                      pl.BlockSpec((1, tr, D_MODEL), lambda j, i, p: (j, i, 0))],
            out_specs=pl.BlockSpec((1, tr, D_MODEL), lambda j, i, p: (j, i, 0))),
        out_shape=jax.ShapeDtypeStruct((N_CHIPS, hr, D_MODEL), bf16),
        compiler_params=_cparams("parallel", "parallel"),
    )(place, gb, recv)


def _rs_sum_chips(place, gb, recv_sib, recv_chips, name):
    hr = gb.shape[2]
    tr = min(hr, RS_TR_MAX)
    nt = hr // tr

    def body(pl_ref, g_ref, r1_ref, r2_ref, o_ref):
        acc = g_ref[0, 0].astype(f32) + r1_ref[0].astype(f32)
        for k in range(N_CHIPS - 1):
            acc = acc + r2_ref[k].astype(f32)
        o_ref[...] = acc

    return pl.pallas_call(
        body, name=name,
        grid_spec=pltpu.PrefetchScalarGridSpec(
            num_scalar_prefetch=1, grid=(nt,),
            in_specs=[pl.BlockSpec((1, 1, tr, D_MODEL), lambda i, p: (p[1], p[0], i, 0)),
                      pl.BlockSpec((1, tr, D_MODEL), lambda i, p: (p[1], i, 0)),
                      pl.BlockSpec((N_CHIPS - 1, tr, D_MODEL), lambda i, p: (0, i, 0))],
            out_specs=pl.BlockSpec((tr, D_MODEL), lambda i, p: (p[0] * nt + i, 0))),
        out_shape=jax.ShapeDtypeStruct((2 * hr, D_MODEL), f32),
        compiler_params=_cparams("parallel"),
    )(place, gb, recv_sib, recv_chips)


def _pad128(v):
    v = v.reshape(1, -1)
    return jnp.pad(v, ((0, 0), (0, 128 - v.shape[1])))


def _row(v):
    return v.reshape(1, -1)


W_COLS = PROJ_WIDTH // N_CHIPS


def _g_in_blocks(gz, gxbc, gdt, gq, gkv):
    g = jnp.concatenate([gz, gxbc, gdt[:W_DT_ROWS], gq, gkv], axis=0)
    return jnp.pad(g.reshape(N_CHIPS, W_COLS, D_MODEL), ((0, 0), (0, D_MODEL - W_COLS), (0, 0)))


def kernel(x, c, ln_in_g, ln_in_b, ada_w, ada_b, w_in, conv_w, conv_b, dt_bias, a_log, d_skip, ssd_norm_w, attn_sinks, w_out, ln1_g, ln1_b, w_ff1, b_ff1, w_ff2, b_ff2, ln2_g, ln2_b, loss_target, m_ln_in_g, m_ln_in_b, m_ada_w, m_ada_b, m_w_in, m_conv_w, m_conv_b, m_dt_bias, m_a_log, m_d_skip, m_ssd_norm_w, m_attn_sinks, m_w_out, m_ln1_g, m_ln1_b, m_w_ff1, m_b_ff1, m_w_ff2, m_b_ff2, m_ln2_g, m_ln2_b, v_ln_in_g, v_ln_in_b, v_ada_w, v_ada_b, v_w_in, v_conv_w, v_conv_b, v_dt_bias, v_a_log, v_d_skip, v_ssd_norm_w, v_attn_sinks, v_w_out, v_ln1_g, v_ln1_b, v_w_ff1, v_b_ff1, v_w_ff2, v_b_ff2, v_ln2_g, v_ln2_b):
    xi, yi, ci = _place()
    chip = 2 * xi + yi
    place = jnp.stack([ci, chip]).astype(jnp.int32)
    x2, tgt = x[0], loss_target[0]

    def as_rows(a):
        return jnp.transpose(a, (2, 0, 1))

    def from_rows(a):
        return jnp.transpose(a, (1, 2, 0))

    cond = jnp.concatenate([c.reshape(2, COND_LANES), conv_w.reshape(3, COND_LANES), jnp.zeros((3, COND_LANES), f32)], axis=0)
    ada_b_mine = lax.dynamic_slice(ada_b, (0, chip * ADA_COLS), (1, ADA_COLS))
    blob_in = jnp.pad(w_in[0].T, ((0, D_MODEL - W_COLS), (0, 0))).astype(bf16)
    cond_all, mod_rows, wall_in = _prologue(cond, ada_w[0], ada_b_mine, blob_in)
    cond_all = cond_all.reshape(N_DEV, 8, COND_LANES)
    c_all = cond_all[:, 0:2].reshape(N_DEV, D_MODEL)
    conv_w_full = jnp.concatenate([cond_all[2 * j, 2:5].reshape(CONV_K, 384) for j in range(N_CHIPS)], axis=1)
    mod = jnp.concatenate([mod_rows[:, 0].reshape(6, D_MODEL), jnp.zeros((2, D_MODEL), f32)], axis=0)
    w_in_f = wall_in[:, :W_COLS].reshape(PROJ_WIDTH, D_MODEL)
    b_ffw = jnp.concatenate([w_ff1[0], w_ff2[0]], axis=0).astype(bf16)
    b_outw = w_out[0].astype(bf16)

    def with_mine(wall, mine):
        return lax.dynamic_update_slice(wall, mine[None], (chip, 0, 0))

    e_mat = _head_expand()
    dsk_x = jnp.repeat(d_skip[0], HEAD_DIM).reshape(1, SSD_WIDTH)
    dtb, alog = _pad128(dt_bias), _pad128(a_log)
    sinks = attn_sinks[0]
    lng, lnb = _row(ln_in_g), _row(ln_in_b)
    u1, z, xr, xc, q, kv, dtr = _inproj_fwd(x2, mod, lng, lnb, w_in_f, conv_w_full, conv_b)
    y, prev_all, wall_out = _ssd_fwd(xc, dtr, dtb, alog, dsk_x, b_outw)
    o, lse, wall_ff = _attn_fwd(q, kv, sinks, b_ffw)
    wall_ff, wall_out = with_mine(wall_ff, b_ffw), with_mine(wall_out, b_outw)
    yn, mix, r1 = _outproj_fwd(y, z, o, x2, mod, lng, lnb, ssd_norm_w, wall_out)

    dr1, u2, s_act, da, df, acc_mlp, db1 = _mlp_fwd_bwd(r1, tgt, mod, ln1_g, ln1_b, ln2_g, ln2_b, wall_ff, b_ff1, wall_ff,
                                                        b_ff2)
    ga = jnp.zeros((N_CHIPS, GA_ROWS, D_MODEL), bf16)
    ga = _wgrad_blob(ga, u2, da, "wgrad_ff1", lambda t, n: (n, t))
    ga = _wgrad_blob(ga, s_act, df, "wgrad_ff2", lambda t, n: (t // 2, 2 + t % 2))
    ga = ga.reshape(N_CHIPS, 2, GA_ROWS // 2, D_MODEL)
    dy, dz, do, dmix, acc_out, a_sib = _outproj_bwd(dr1, mix, y, z, mod, ssd_norm_w, wall_out, ga)
    gc = jnp.zeros((N_CHIPS, GC_ROWS, D_MODEL), bf16)
    gc = _wgrad_blob(gc, yn, dmix, "wgrad_out_y", lambda t, n: (t, 0))
    gc = _wgrad_blob(gc, o, dmix, "wgrad_out_o", lambda t, n: (2 + t, 0))
    gc = gc.reshape(N_CHIPS, 2, GC_ROWS // 2, D_MODEL)
    a_pair = _rs_sum_pair(place, ga, a_sib, "rs_sum_pair_a")
    dq, dkv, dsink, a_chips, c_sib = _attn_bwd(q, kv, do, lse, sinks, a_pair, gc)
    g_a = _rs_sum_chips(place, ga, a_sib, a_chips, "rs_sum_chips_a")
    c_pair = _rs_sum_pair(place, gc, c_sib, "rs_sum_pair_c")
    dxc, ddt, acc_ssd, dd_x, g_a, c_chips = _ssd_bwd(xc, dtr, dy, prev_all, dtb, alog, dsk_x, e_mat, g_a, c_pair)
    g_c = _rs_sum_chips(place, gc, c_sib, c_chips, "rs_sum_chips_c")
    dxr, acc_conv, g_c = _conv_bwd(dxc, xr, conv_w_full, conv_b, g_c)
    gb = _g_in_blocks(_wgrad(dz, u1, "wgrad_in_z"), _wgrad(dxr, u1, "wgrad_in_xbc"),
                      _wgrad(ddt.astype(bf16), u1, "wgrad_in_dt"), _wgrad(dq, u1, "wgrad_in_q"),
                      _wgrad(dkv, u1, "wgrad_in_kv")).reshape(N_CHIPS, 2, GB_ROWS // 2, D_MODEL)
    b_sib = _rs_to_sibling(gb)
    b_pair = _rs_sum_pair(place, gb, b_sib, "rs_sum_pair_b")
    grad_x, acc_in, b_chips = _inproj_bwd(dz, dxr, dq, dkv, ddt, dr1, x2, mod, lng, lnb, w_in_f, b_pair)
    g_b = _rs_share(_rs_sum_chips(place, gb, b_sib, b_chips, "rs_sum_chips_b"))

    packed = _small_pack(acc_in, acc_out, acc_mlp, db1, acc_conv, acc_ssd, dd_x, dsink, alog, e_mat)
    small_all = _allgather8(packed.reshape(8, SMALL_N // 8), "gather_small").reshape(N_DEV, SMALL_N)
    given = dict(ada_b=(ada_b, m_ada_b, v_ada_b), ln_in_g=(ln_in_g, m_ln_in_g, v_ln_in_g), ln_in_b=(ln_in_b, m_ln_in_b, v_ln_in_b),
                 conv_b=(conv_b, m_conv_b, v_conv_b), dt_bias=(dt_bias, m_dt_bias, v_dt_bias), a_log=(a_log, m_a_log, v_a_log),
                 d_skip=(d_skip, m_d_skip, v_d_skip), ssd_norm_w=(ssd_norm_w, m_ssd_norm_w, v_ssd_norm_w),
                 attn_sinks=(attn_sinks, m_attn_sinks, v_attn_sinks), ln1_g=(ln1_g, m_ln1_g, v_ln1_g),
                 ln1_b=(ln1_b, m_ln1_b, v_ln1_b), b_ff1=(b_ff1, m_b_ff1, v_b_ff1), b_ff2=(b_ff2, m_b_ff2, v_b_ff2),
                 ln2_g=(ln2_g, m_ln2_g, v_ln2_g), ln2_b=(ln2_b, m_ln2_b, v_ln2_b))
    upd = _small_update(small_all, *([_row(given[n][i]) for n in SMALL_PARAMS] for i in range(3)))
    small_res = {n: [t.reshape(given[n][0].shape) for t in upd[4 * j:4 * j + 4]] for j, n in enumerate(SMALL_PARAMS)}
    g_conv_all, loss_lanes = upd[4 * len(SMALL_PARAMS)], upd[4 * len(SMALL_PARAMS) + 1]
    loss = jnp.sum(loss_lanes)

    dmod_mine = lax.dynamic_slice(small_all[:, :6 * D_MODEL], (0, chip * ADA_COLS), (N_DEV, ADA_COLS))
    big = {"ada_w": [t[None] for t in _ada_bwd(c_all, dmod_mine, ada_w[0], m_ada_w[0], v_ada_w[0])]}

    g_conv = lax.dynamic_slice(g_conv_all.reshape(CONV_K, CONV_DIM), (0, chip * 384), (CONV_K, 384))
    big["conv_w"] = [t[None] for t in (g_conv, *_adamw(conv_w[0], g_conv, m_conv_w[0], v_conv_w[0], "adamw_conv_w"))]

    g_rows = g_b[:W_COLS].reshape(W_COLS, 1, D_MODEL)
    big["w_in"] = [from_rows(t) for t in (g_rows, *_adamw_rows(as_rows(w_in), g_rows, as_rows(m_w_in), as_rows(v_w_in)))]
    for name, g, (w, m, v) in (("w_out", g_c, (w_out, m_w_out, v_w_out)),
                               ("w_ff1", g_a[:D_MODEL], (w_ff1, m_w_ff1, v_w_ff1)),
                               ("w_ff2", g_a[D_MODEL:GA_ROWS], (w_ff2, m_w_ff2, v_w_ff2))):
        big[name] = [t[None] for t in (g, *_adamw(w[0], g, m[0], v[0], "adamw_" + name))]

    order = ("ln_in_g", "ln_in_b", "ada_w", "ada_b", "w_in", "conv_w", "conv_b", "dt_bias", "a_log", "d_skip", "ssd_norm_w",
             "attn_sinks", "w_out", "ln1_g", "ln1_b", "w_ff1", "b_ff1", "w_ff2", "b_ff2", "ln2_g", "ln2_b")
    res = {**small_res, **big}
    return (loss, grad_x[None], *[res[n][k] for k in range(4) for n in order])
```

```python
import functools
import math

import numpy as np
import jax
import jax.numpy as jnp
from jax import lax
from jax.experimental import pallas as pl
from jax.experimental.pallas import tpu as pltpu

f32 = jnp.float32
bf16 = jnp.bfloat16

D_MODEL = 1024
SSD_WIDTH = 1024
SSD_HEADS = 16
HEAD_DIM = 64
SSD_STATE = 128
SSD_GROUPS = 2
CHUNK = 128
CONV_K = 4
CONV_DIM = 1536
ATTN_HEADS = 16
D_FF = 4096
PROJ_WIDTH = 3856
ALPHA = 2.0 ** 0.25
LN_EPS = 1e-5
RMS_EPS = 1e-5
ATTN_SCALE = HEAD_DIM ** -0.5
NEG = -1e30

ADAM_LR = 0.001
ADAM_B1 = 0.9
ADAM_B2 = 0.999
ADAM_EPS = 1e-08
ADAM_WD = 0.01
ADAM_STEP = 10

W_Z, W_XBC, W_Q, W_KV = slice(0, 1024), slice(1024, 2560), slice(2576, 3600), slice(3600, 3856)
W_DT = slice(2560, 2688)
W_DT_ROWS = 16
GA_ROWS = 2048
GB_ROWS = 1024
GC_ROWS = 512
WG_TM = 512
STAGE_ROWS = 512
ADAMW_BLOCK_ELEMS = 1 << 18
MLP_TM = 256
DENSE_TM = 512
N_CHIPS = 4
N_DEV = 8
VMEM_LIMIT = 56 * 1024 * 1024
MESH = pl.DeviceIdType.MESH

ALIBI_SLOPES = tuple(2.0 ** (-8.0 / ATTN_HEADS * (i + 1)) for i in range(ATTN_HEADS))


def _cparams(*sem):
    return pltpu.CompilerParams(dimension_semantics=sem, vmem_limit_bytes=VMEM_LIMIT)


def _sigmoid(x):
    return 1.0 / (1.0 + jnp.exp(-x))


def _softplus(x):
    return jnp.maximum(x, 0.0) + jnp.log1p(jnp.exp(-jnp.abs(x)))


def _ln_stats(x):
    mu = jnp.mean(x, axis=-1, keepdims=True)
    xc = x - mu
    var = jnp.mean(xc * xc, axis=-1, keepdims=True)
    rstd = lax.rsqrt(var + LN_EPS)
    return xc * rstd, rstd


def _ln_bwd(dy, xhat, rstd, g):
    dxh = dy * g
    m1 = jnp.mean(dxh, axis=-1, keepdims=True)
    m2 = jnp.mean(dxh * xhat, axis=-1, keepdims=True)
    return rstd * (dxh - m1 - xhat * m2)


def _dot(a, b):
    return jnp.dot(a, b, preferred_element_type=f32)


def _dot_nt(a, b):
    return lax.dot_general(a, b, (((1,), (1,)), ((), ())), preferred_element_type=f32)


def _dot_tn(a, b):
    return lax.dot_general(a, b, (((0,), (0,)), ((), ())), preferred_element_type=f32)


def _dot_exact(a, b):
    return jnp.dot(a, b, preferred_element_type=f32, precision=lax.Precision.HIGHEST)


def _split3(v):
    hi = v.astype(bf16)
    r1 = v - hi.astype(f32)
    mid = r1.astype(bf16)
    lo = (r1 - mid.astype(f32)).astype(bf16)
    return hi, mid, lo


def _sel_dot(sel, v):
    hi, mid, lo = _split3(v)
    return _dot(sel, hi) + _dot(sel, mid) + _dot(sel, lo)


def _dot_sel_nt(v, sel):
    hi = v.astype(bf16)
    mid = (v - hi.astype(f32)).astype(bf16)
    return _dot_nt(hi, sel) + _dot_nt(mid, sel)


def _expand_heads(v):
    lane = lax.broadcasted_iota(jnp.int32, (v.shape[0], 128), 1)
    blocks = [jnp.where(lane < HEAD_DIM, v[:, 2 * b:2 * b + 1], v[:, 2 * b + 1:2 * b + 2]) for b in range(SSD_HEADS // 2)]
    return jnp.concatenate(blocks, axis=1)


def _full(shape):
    nd = len(shape)
    return pl.BlockSpec(shape, lambda *_: (0,) * nd)


def _resident(shape):
    nd = len(shape)
    return pl.BlockSpec(shape, lambda *_: (0,) * nd, pipeline_mode=pl.Buffered(1))


def _rows(tm, n):
    return pl.BlockSpec((tm, n), lambda i: (i, 0))


def _wall_rows(k):
    return pl.BlockSpec((N_CHIPS, D_MODEL, D_MODEL), lambda *_: (0, k, 0), pipeline_mode=pl.Buffered(1))


def _inproj_fwd(x, mod, ln_g, ln_b, w_in_t, conv_w, conv_b):
    L = x.shape[0]
    tm = DENSE_TM
    nt = L // tm

    def body(x_ref, mod_ref, g_ref, b_ref, w_ref, cw_ref, cb_ref,
             u1_ref, z_ref, xr_ref, xc_ref, q_ref, kv_ref, dt_ref, halo, buf):
        @pl.when(pl.program_id(0) == 0)
        def _():
            halo[...] = jnp.zeros_like(halo)

        xhat, _ = _ln_stats(x_ref[...])
        h0 = xhat * g_ref[...] + b_ref[...]
        u1 = (h0 * (1.0 + mod_ref[1:2, :]) + mod_ref[0:1, :]).astype(bf16)
        u1_ref[...] = u1
        z_ref[...] = _dot_nt(u1, w_ref[W_Z, :])
        xr = _dot_nt(u1, w_ref[W_XBC, :])
        xr_ref[...] = xr
        q_ref[...] = _dot_nt(u1, w_ref[W_Q, :]).astype(bf16)
        kv_ref[...] = _dot_nt(u1, w_ref[W_KV, :]).astype(bf16)
        dt_ref[...] = _dot_nt(u1, w_ref[W_DT, :])
        buf[0:8, :] = halo[...]
        buf[8:8 + tm, :] = xr
        pre = cb_ref[...] + cw_ref[0:1, :] * buf[5:5 + tm, :]
        for k in range(1, CONV_K):
            pre = pre + cw_ref[k:k + 1, :] * buf[5 + k:5 + k + tm, :]
        xc_ref[...] = pre * _sigmoid(pre)
        halo[...] = xr[tm - 8:tm, :]

    return pl.pallas_call(
        body, name="inproj_fwd", grid=(nt,),
        in_specs=[_rows(tm, D_MODEL), _full((8, D_MODEL)), _full((1, D_MODEL)), _full((1, D_MODEL)),
                  _resident((PROJ_WIDTH, D_MODEL)), _full((CONV_K, CONV_DIM)), _full((1, CONV_DIM))],
        out_specs=[_rows(tm, D_MODEL), _rows(tm, D_MODEL), _rows(tm, CONV_DIM), _rows(tm, CONV_DIM),
                   _rows(tm, D_MODEL), _rows(tm, 256), _rows(tm, 128)],
        out_shape=[jax.ShapeDtypeStruct((L, D_MODEL), bf16), jax.ShapeDtypeStruct((L, D_MODEL), f32),
                   jax.ShapeDtypeStruct((L, CONV_DIM), f32), jax.ShapeDtypeStruct((L, CONV_DIM), f32),
                   jax.ShapeDtypeStruct((L, D_MODEL), bf16), jax.ShapeDtypeStruct((L, 256), bf16),
                   jax.ShapeDtypeStruct((L, 128), f32)],
        scratch_shapes=[pltpu.VMEM((8, CONV_DIM), f32), pltpu.VMEM((tm + 8, CONV_DIM), f32)],
        compiler_params=_cparams("arbitrary"),
    )(x, mod, ln_g, ln_b, w_in_t, conv_w, conv_b)


def _head_expand():
    e = np.zeros((128, SSD_WIDTH), np.float32)
    for h in range(SSD_HEADS):
        e[h, h * HEAD_DIM:(h + 1) * HEAD_DIM] = 1.0
    return jnp.asarray(e, dtype=bf16)


def _ssd_chunk_common(dt_raw, dtb, a_row):
    T = CHUNK
    lane = lax.broadcasted_iota(jnp.int32, (T, 128), 1)
    dt = jnp.where(lane < SSD_HEADS, _softplus(dt_raw + dtb), 0.0)
    a = dt * a_row
    r = lax.broadcasted_iota(jnp.int32, (T, T), 0)
    c = lax.broadcasted_iota(jnp.int32, (T, T), 1)
    tril = (c <= r).astype(bf16)
    cum = _sel_dot(tril, a)
    dtx = _expand_heads(dt)
    cumx = _expand_heads(cum)
    return dt, a, cum, dtx, cumx, r, c


def _ssd_fwd(xc, dt_raw, dt_bias, a_log, d_skip_x, blob):
    L = xc.shape[0]
    nc = L // CHUNK
    T = CHUNK
    R = blob.shape[0]

    def body(xc_ref, dt_ref, dtb_ref, al_ref, dsk_ref, blob_ref, y_ref, prev_ref, wall_ref, st, send_sems, recv_sems):
        start, finish = _gather_job(blob_ref, wall_ref, send_sems, recv_sems, R)

        @pl.when(pl.program_id(0) == 0)
        def _():
            st[...] = jnp.zeros_like(st)
            start()

        @pl.when(pl.program_id(0) == nc - 1)
        def _():
            finish()

        a_row = -jnp.exp(al_ref[...])
        lane1 = lax.broadcasted_iota(jnp.int32, (1, 128), 1)
        a_row = jnp.where(lane1 < SSD_HEADS, a_row, 0.0)
        dt, a, cum, dtx, cumx, r, c = _ssd_chunk_common(dt_ref[...], dtb_ref[...], a_row)
        cum_t = cum.T
        ex = jnp.exp(cumx)
        last = cumx[T - 1:T, :]
        wx = jnp.exp(last - cumx)
        cdx = jnp.exp(last)
        xs = xc_ref[:, 0:SSD_WIDTH]
        X = xs * dtx
        Xb = X.astype(bf16)
        Xd = (X * wx).astype(bf16)
        prev = st[...]
        prev_ref[0] = prev
        prevb = prev.astype(bf16)
        tri = c <= r
        lane = lax.broadcasted_iota(jnp.int32, (T, 128), 1)
        y_blocks = []
        new_states = []
        for g in range(SSD_GROUPS):
            Bg = xc_ref[:, 1024 + 128 * g:1152 + 128 * g].astype(bf16)
            Cg = xc_ref[:, 1280 + 128 * g:1408 + 128 * g].astype(bf16)
            G = _dot_nt(Cg, Bg)
            yoff = _dot(Cg, prevb[:, 512 * g:512 * (g + 1)])
            new_states.append(_dot_tn(Bg, Xd[:, 512 * g:512 * (g + 1)]))
            for j in range(4):
                blk = 4 * g + j
                Xblk = Xb[:, 128 * blk:128 * (blk + 1)]
                ys = []
                for half in range(2):
                    h = 2 * blk + half
                    seg = jnp.minimum(cum[:, h:h + 1] - cum_t[h:h + 1, :], 0.0)
                    M = jnp.where(tri, G * jnp.exp(seg), 0.0).astype(bf16)
                    ys.append(_dot(M, Xblk))
                yd = jnp.where(lane < HEAD_DIM, ys[0], ys[1])
                sl = slice(128 * blk, 128 * (blk + 1))
                y_blocks.append(yd + ex[:, sl] * yoff[:, 128 * j:128 * (j + 1)] + dsk_ref[:, sl] * xs[:, sl])
        y_ref[...] = jnp.concatenate(y_blocks, axis=1)
        st[...] = prev * cdx + jnp.concatenate(new_states, axis=1)

    return pl.pallas_call(
        body, name="ssd_fwd", grid=(nc,),
        in_specs=[_rows(T, CONV_DIM), _rows(T, 128), _full((1, 128)), _full((1, 128)), _full((1, SSD_WIDTH)), _ANY_SPEC],
        out_specs=[_rows(T, SSD_WIDTH), pl.BlockSpec((1, SSD_STATE, SSD_WIDTH), lambda i: (i, 0, 0)), _ANY_SPEC],
        out_shape=[jax.ShapeDtypeStruct((L, SSD_WIDTH), f32), jax.ShapeDtypeStruct((nc, SSD_STATE, SSD_WIDTH), f32),
                   jax.ShapeDtypeStruct((N_CHIPS, R, D_MODEL), bf16)],
        scratch_shapes=[pltpu.VMEM((SSD_STATE, SSD_WIDTH), f32)] + _sems(6),
        compiler_params=_cparams("arbitrary"),
    )(xc, dt_raw, dt_bias, a_log, d_skip_x, blob)


def _kv_halves(kv_prev, kv_cur, first):
    kv = jnp.concatenate([jnp.where(first, 0.0, kv_prev.astype(f32)), kv_cur.astype(f32)], axis=0)
    lane = lax.broadcasted_iota(jnp.int32, (2 * CHUNK, 128), 1)
    lo = lane < HEAD_DIM
    out = []
    for g in range(2):
        per_half = []
        for half in range(2):
            both = []
            for t in (kv[:, 0:128], kv[:, 128:256]):
                src = t if g == half else pltpu.roll(t, HEAD_DIM, 1)
                both.append(jnp.where(lo if half == 0 else ~lo, src, 0.0).astype(bf16))
            per_half.append(tuple(both))
        out.append(per_half)
    return out


def _attn_masks(first):
    r = lax.broadcasted_iota(jnp.int32, (CHUNK, 2 * CHUNK), 0)
    c = lax.broadcasted_iota(jnp.int32, (CHUNK, 2 * CHUNK), 1)
    dist = r + CHUNK - c
    valid = (dist >= 0) & (dist < CHUNK) & ((c >= CHUNK) | jnp.logical_not(first))
    return dist.astype(f32), valid


def _head_stack(g, half, sink_ref):
    blks = [4 * g + i for i in range(4)]
    heads = [2 * b + half for b in blks]
    slope = jnp.concatenate([jnp.full((CHUNK, 1), ALIBI_SLOPES[h], f32) for h in heads], axis=0)
    sink = jnp.concatenate([jnp.full((CHUNK, 1), sink_ref[h], f32) for h in heads], axis=0)
    return blks, heads, slope, sink


def _attn_fwd(q, kv, sinks, blob):
    L = q.shape[0]
    nb = L // CHUNK
    T = CHUNK
    R = blob.shape[0]

    def body(sink_ref, q_ref, kvp_ref, kvc_ref, blob_ref, o_ref, lse_ref, wall_ref, send_sems, recv_sems):
        first = pl.program_id(0) == 0
        start, finish = _gather_job(blob_ref, wall_ref, send_sems, recv_sems, R)

        @pl.when(first)
        def _():
            start()

        @pl.when(pl.program_id(0) == nb - 1)
        def _():
            finish()

        ext = _kv_halves(kvp_ref[...], kvc_ref[...], first)
        dist, valid = _attn_masks(first)
        lane = lax.broadcasted_iota(jnp.int32, (T, 128), 1)
        lse = jnp.zeros((T, 128), f32)
        o_blocks = []
        for blk in range(8):
            qb = q_ref[:, 128 * blk:128 * (blk + 1)]
            acc = None
            for half in range(2):
                h = 2 * blk + half
                k_ext, v_ext = ext[h // 8][half]
                s = _dot_nt(qb, k_ext) * ATTN_SCALE - ALIBI_SLOPES[h] * dist
                s = jnp.where(valid, s, NEG)
                sink = sink_ref[h]
                m = jnp.maximum(jnp.max(s, axis=-1, keepdims=True), sink)
                p = jnp.exp(s - m)
                den = jnp.sum(p, axis=-1, keepdims=True) + jnp.exp(sink - m)
                pn = (p * (1.0 / den)).astype(bf16)
                oh = _dot(pn, v_ext)
                acc = oh if acc is None else acc + oh
                lse = jnp.where(lane == h, m + jnp.log(den), lse)
            o_blocks.append(acc.astype(bf16))
        o_ref[...] = jnp.concatenate(o_blocks, axis=1)
        lse_ref[...] = lse

    return pl.pallas_call(
        body, name="attn_fwd", grid=(nb,),
        in_specs=[pl.BlockSpec(memory_space=pltpu.SMEM), _rows(T, D_MODEL),
                  pl.BlockSpec((T, 256), lambda i: (jnp.maximum(i - 1, 0), 0)), _rows(T, 256), _ANY_SPEC],
        out_specs=[_rows(T, D_MODEL), _rows(T, 128), _ANY_SPEC],
        out_shape=[jax.ShapeDtypeStruct((L, D_MODEL), bf16), jax.ShapeDtypeStruct((L, 128), f32),
                   jax.ShapeDtypeStruct((N_CHIPS, R, D_MODEL), bf16)],
        scratch_shapes=_sems(6),
        compiler_params=_cparams("arbitrary"),
    )(sinks, q, kv, kv, blob)


def _gated_norm(y, z, w):
    sz = _sigmoid(z)
    hg = y * (z * sz)
    ns, rss = [], []
    for g in range(SSD_GROUPS):
        hs = hg[:, 512 * g:512 * (g + 1)]
        rs = lax.rsqrt(jnp.mean(hs * hs, axis=-1, keepdims=True) + RMS_EPS)
        ns.append(hs * rs)
        rss.append(rs)
    n = jnp.concatenate(ns, axis=1)
    return n * w, n, rss, sz


def _outproj_fwd(y, z, o, x, mod, ln_g, ln_b, norm_w, w_out):
    L = x.shape[0]
    tm = DENSE_TM

    def body(y_ref, z_ref, o_ref, x_ref, mod_ref, g_ref, b_ref, nw_ref, w_ref, yn_ref, mix_ref, r1_ref):
        yn, _, _, _ = _gated_norm(y_ref[...], z_ref[...], nw_ref[...])
        ynb = yn.astype(bf16)
        yn_ref[...] = ynb
        mix = (_dot(ynb[:, 0:512], w_ref[0]) + _dot(ynb[:, 512:1024], w_ref[1])
               + _dot(o_ref[:, 0:512], w_ref[2]) + _dot(o_ref[:, 512:1024], w_ref[3]))
        mix_ref[...] = mix
        xhat, _ = _ln_stats(x_ref[...])
        h0 = xhat * g_ref[...] + b_ref[...]
        r1_ref[...] = ALPHA * h0 + (1.0 + mod_ref[2:3, :]) * mix

    v = _full((1, D_MODEL))
    return pl.pallas_call(
        body, name="outproj_fwd", grid=(L // tm,),
        in_specs=[_rows(tm, D_MODEL), _rows(tm, D_MODEL), _rows(tm, D_MODEL), _rows(tm, D_MODEL),
                  _full((8, D_MODEL)), v, v, v, _resident((N_CHIPS, 512, D_MODEL))],
        out_specs=[_rows(tm, D_MODEL)] * 3,
        out_shape=[jax.ShapeDtypeStruct((L, D_MODEL), bf16), jax.ShapeDtypeStruct((L, D_MODEL), f32),
                   jax.ShapeDtypeStruct((L, D_MODEL), f32)],
        compiler_params=_cparams("parallel"),
    )(y, z, o, x, mod, ln_g, ln_b, norm_w, w_out)


A_LN2G, A_LN2B, A_G2, A_B2, A_SC2, A_SH2, A_LN1G, A_LN1B, A_LOSS = range(9)


def _mlp_fwd_bwd(r1, target, mod, ln1_g, ln1_b, ln2_g, ln2_b, w1, b1, w2, b2):
    L = r1.shape[0]
    tm = MLP_TM
    nj = D_FF // 1024

    def body(r1_ref, t_ref, mod_ref, g1_ref, bb1_ref, g2_ref, bb2_ref, w1_ref, b1_ref, w2_ref, b2_ref,
             dr1_ref, u2_ref, s_ref, da_ref, df_ref, acc_ref, db1_ref, hr):
        @pl.when(pl.program_id(0) == 0)
        def _():
            acc_ref[...] = jnp.zeros_like(acc_ref)
            db1_ref[...] = jnp.zeros_like(db1_ref)

        sc2, sh2, gate2 = mod_ref[4:5, :], mod_ref[3:4, :], mod_ref[5:6, :]
        xhat1, rstd1 = _ln_stats(r1_ref[...])
        h1 = xhat1 * g1_ref[...] + bb1_ref[...]
        u2f = h1 * (1.0 + sc2) + sh2
        u2 = u2f.astype(bf16)
        u2_ref[...] = u2
        f = jnp.zeros((tm, D_MODEL), f32) + b2_ref[...]
        for j in range(nj):
            cs = slice(1024 * j, 1024 * (j + 1))
            a = _dot(u2, w1_ref[j]) + b1_ref[:, cs]
            hrj = jnp.maximum(a, 0.0)
            hr[:, cs] = hrj.astype(bf16)
            sj = (hrj * hrj).astype(bf16)
            s_ref[:, cs] = sj
            f = f + _dot(sj, w2_ref[j])
        r2 = ALPHA * h1 + (1.0 + gate2) * f
        xhat2, rstd2 = _ln_stats(r2)
        h2 = xhat2 * g2_ref[...] + bb2_ref[...]
        diff = h2 - t_ref[...]
        dh2 = diff * (1.0 / D_MODEL)

        def add(row, val):
            acc_ref[row:row + 1, :] += jnp.sum(val, axis=0, keepdims=True)

        add(A_LOSS, diff * diff * (0.5 / D_MODEL))
        add(A_LN2G, dh2 * xhat2)
        add(A_LN2B, dh2)
        dr2 = _ln_bwd(dh2, xhat2, rstd2, g2_ref[...])
        add(A_G2, dr2 * f)
        df = dr2 * (1.0 + gate2)
        add(A_B2, df)
        dfb = df.astype(bf16)
        df_ref[...] = dfb
        du2 = jnp.zeros((tm, D_MODEL), f32)
        for j in range(nj):
            cs = slice(1024 * j, 1024 * (j + 1))
            ds = _dot_nt(dfb, w2_ref[j])
            daj = ds * (2.0 * hr[:, cs].astype(f32))
            db1_ref[:, cs] += jnp.sum(daj, axis=0, keepdims=True)
            dajb = daj.astype(bf16)
            da_ref[:, cs] = dajb
            du2 = du2 + _dot_nt(dajb, w1_ref[j])
        add(A_SC2, du2 * h1)
        add(A_SH2, du2)
        dh1 = ALPHA * dr2 + du2 * (1.0 + sc2)
        add(A_LN1G, dh1 * xhat1)
        add(A_LN1B, dh1)
        dr1_ref[...] = _ln_bwd(dh1, xhat1, rstd1, g1_ref[...])

    v = _full((1, D_MODEL))
    return pl.pallas_call(
        body, name="mlp_fwd_bwd", grid=(L // tm,),
        in_specs=[_rows(tm, D_MODEL), _rows(tm, D_MODEL), _full((8, D_MODEL)), v, v, v, v,
                  _wall_rows(0), _full((1, D_FF)), _wall_rows(1), v],
        out_specs=[_rows(tm, D_MODEL), _rows(tm, D_MODEL), _rows(tm, D_FF), _rows(tm, D_FF), _rows(tm, D_MODEL),
                   _full((16, D_MODEL)), _full((1, D_FF))],
        out_shape=[jax.ShapeDtypeStruct((L, D_MODEL), f32), jax.ShapeDtypeStruct((L, D_MODEL), bf16),
                   jax.ShapeDtypeStruct((L, D_FF), bf16), jax.ShapeDtypeStruct((L, D_FF), bf16),
                   jax.ShapeDtypeStruct((L, D_MODEL), bf16), jax.ShapeDtypeStruct((16, D_MODEL), f32),
                   jax.ShapeDtypeStruct((1, D_FF), f32)],
        scratch_shapes=[pltpu.VMEM((tm, D_FF), bf16)],
        compiler_params=_cparams("arbitrary"),
    )(r1, target, mod, ln1_g, ln1_b, ln2_g, ln2_b, w1, b1, w2, b2)


def _wgrad(a, b, name):
    L, M = a.shape
    N = b.shape[1]
    tm = min(M, 512)
    tn = next(t for t in (1024, 768, 512, 256, 128) if N % t == 0)

    def body(a_ref, b_ref, o_ref):
        o_ref[...] = _dot_tn(a_ref[...], b_ref[...]).astype(bf16)

    return pl.pallas_call(
        body, name=name, grid=(M // tm, N // tn),
        in_specs=[pl.BlockSpec((L, tm), lambda i, j: (0, i)), pl.BlockSpec((L, tn), lambda i, j: (0, j))],
        out_specs=pl.BlockSpec((tm, tn), lambda i, j: (i, j)),
        out_shape=jax.ShapeDtypeStruct((M, N), bf16),
        compiler_params=_cparams("parallel", "parallel"),
    )(a, b)


def _wgrad_blob(blob, a, b, name, place_of):
    L, M = a.shape
    N = b.shape[1]

    def body(blob_ref, a_ref, b_ref, o_ref):
        o_ref[0] = _dot_tn(a_ref[...], b_ref[...]).astype(bf16)

    return pl.pallas_call(
        body, name=name, grid=(M // WG_TM, N // D_MODEL),
        in_specs=[pl.BlockSpec(memory_space=pl.ANY), pl.BlockSpec((L, WG_TM), lambda t, n: (0, t)),
                  pl.BlockSpec((L, D_MODEL), lambda t, n: (0, n))],
        out_specs=pl.BlockSpec((1, WG_TM, D_MODEL), lambda t, n: (*place_of(t, n), 0)),
        out_shape=jax.ShapeDtypeStruct(blob.shape, bf16), input_output_aliases={0: 0},
        compiler_params=_cparams("parallel", "parallel"),
    )(blob, a, b)


def _outproj_bwd(dr1, mix, y, z, mod, norm_w, w_out, gb):
    L = dr1.shape[0]
    tm = DENSE_TM
    nt = L // tm

    def body(dr1_ref, mix_ref, y_ref, z_ref, mod_ref, nw_ref, w_ref, gb_ref,
             dy_ref, dz_ref, do_ref, dmix_ref, acc_ref, sib_ref, send_sems, recv_sems):
        start, wait = _to_sibling_job(gb_ref, sib_ref, send_sems, recv_sems)

        @pl.when(pl.program_id(0) == 0)
        def _():
            acc_ref[...] = jnp.zeros_like(acc_ref)
            start()

        @pl.when(pl.program_id(0) == nt - 1)
        def _():
            wait()

        dr1 = dr1_ref[...]
        acc_ref[0:1, :] += jnp.sum(dr1 * mix_ref[...], axis=0, keepdims=True)
        dmix = (dr1 * (1.0 + mod_ref[2:3, :])).astype(bf16)
        dmix_ref[...] = dmix
        dyn = jnp.concatenate([_dot_nt(dmix, w_ref[0]), _dot_nt(dmix, w_ref[1])], axis=1)
        do_ref[...] = jnp.concatenate([_dot_nt(dmix, w_ref[2]), _dot_nt(dmix, w_ref[3])], axis=1).astype(bf16)
        yv, zv = y_ref[...], z_ref[...]
        _, n, rss, sz = _gated_norm(yv, zv, nw_ref[...])
        acc_ref[1:2, :] += jnp.sum(dyn * n, axis=0, keepdims=True)
        dn = dyn * nw_ref[...]
        parts = []
        for g in range(SSD_GROUPS):
            sl = slice(512 * g, 512 * (g + 1))
            dng, ng = dn[:, sl], n[:, sl]
            parts.append(rss[g] * (dng - ng * jnp.mean(dng * ng, axis=-1, keepdims=True)))
        dhg = jnp.concatenate(parts, axis=1)
        dy_ref[...] = dhg * (zv * sz)
        dz_ref[...] = (dhg * yv * (sz * (1.0 + zv * (1.0 - sz)))).astype(bf16)

    return pl.pallas_call(
        body, name="outproj_bwd", grid=(nt,),
        in_specs=[_rows(tm, D_MODEL)] * 4 + [_full((8, D_MODEL)), _full((1, D_MODEL)), _resident((N_CHIPS, 512, D_MODEL)),
                  _ANY_SPEC],
        out_specs=[_rows(tm, D_MODEL)] * 4 + [_full((8, D_MODEL)), _ANY_SPEC],
        out_shape=[jax.ShapeDtypeStruct((L, D_MODEL), f32)] + [jax.ShapeDtypeStruct((L, D_MODEL), bf16)] * 3
        + [jax.ShapeDtypeStruct((8, D_MODEL), f32), jax.ShapeDtypeStruct((N_CHIPS,) + gb.shape[2:], bf16)],
        scratch_shapes=_sems(N_CHIPS),
        compiler_params=_cparams("arbitrary"),
    )(dr1, mix, y, z, mod, norm_w, w_out, gb)


def _attn_bwd(q, kv, do, lse, sinks, pb, gb2):
    L = q.shape[0]
    nb = L // CHUNK
    T = CHUNK

    def body(sink_ref, q_ref, kvp_ref, kvc_ref, do_ref, lse_ref, pb_ref, gb2_ref, dq_ref, dkv_ref, dsink_ref, chips_ref,
             sib2_ref, carry, send_sems, recv_sems, send_sems2, recv_sems2):
        n = pl.program_id(0)
        start, wait = _to_chips_job(pb_ref, chips_ref, send_sems, recv_sems)
        start2, wait2 = _to_sibling_job(gb2_ref, sib2_ref, send_sems2, recv_sems2)

        @pl.when(n == 0)
        def _():
            carry[...] = jnp.zeros_like(carry)
            dsink_ref[...] = jnp.zeros_like(dsink_ref)
            start()
            start2()

        @pl.when(n < nb)
        def _():
            first = n == 0
            ext = _kv_halves(kvp_ref[...], kvc_ref[...], first)
            dist, valid = _attn_masks(first)
            dist4, valid4 = jnp.concatenate([dist] * 4, axis=0), jnp.concatenate([valid] * 4, axis=0)
            lane1 = lax.broadcasted_iota(jnp.int32, (1, 128), 1)
            lse = lse_ref[...]
            qts = [q_ref[:, 128 * b:128 * (b + 1)].astype(f32).T.astype(bf16) for b in range(8)]
            dots = [do_ref[:, 128 * b:128 * (b + 1)].astype(f32).T.astype(bf16) for b in range(8)]
            acck = [None, None]
            accv = [None, None]
            dsink = jnp.zeros((1, 128), f32)
            dq_acc = [None] * 8
            for g in range(2):
                for half in range(2):
                    k_ext, v_ext = ext[g][half]
                    blks, heads, slope, sink = _head_stack(g, half, sink_ref)
                    qs = jnp.concatenate([q_ref[:, 128 * b:128 * (b + 1)] for b in blks], axis=0)
                    dos = jnp.concatenate([do_ref[:, 128 * b:128 * (b + 1)] for b in blks], axis=0)
                    rows = slice(HEAD_DIM * half, HEAD_DIM * (half + 1))
                    qt = jnp.concatenate([qts[b][rows, :] for b in blks], axis=1)
                    dot_ = jnp.concatenate([dots[b][rows, :] for b in blks], axis=1)
                    lse_col = jnp.concatenate([lse[:, h:h + 1] for h in heads], axis=0)
                    s = _dot_nt(qs, k_ext) * ATTN_SCALE - slope * dist4
                    p = jnp.where(valid4, jnp.exp(s - lse_col), 0.0)
                    dp = _dot_nt(dos, v_ext)
                    delta = jnp.sum(p * dp, axis=-1, keepdims=True)
                    ds = (p * (dp - delta) * ATTN_SCALE).astype(bf16)
                    sd = jnp.exp(sink - lse_col) * delta
                    dqs = _dot(ds, k_ext)
                    for i, b in enumerate(blks):
                        seg = slice(T * i, T * (i + 1))
                        dq_acc[b] = dqs[seg, :] if dq_acc[b] is None else dq_acc[b] + dqs[seg, :]
                        dsink = dsink - jnp.where(lane1 == heads[i], jnp.sum(sd[seg, :], axis=0, keepdims=True), 0.0)
                    dk = _dot(qt, ds)
                    dv = _dot(dot_, p.astype(bf16))
                    acck[g] = dk if acck[g] is None else acck[g] + dk
                    accv[g] = dv if accv[g] is None else accv[g] + dv
            dq_ref[...] = jnp.concatenate([a.astype(bf16) for a in dq_acc], axis=1)
            dsink_ref[...] += dsink
            dkv = jnp.concatenate([jnp.concatenate(acck, axis=0).T, jnp.concatenate(accv, axis=0).T], axis=1)
            dkv_ref[...] = (carry[...] + dkv[0:T, :]).astype(bf16)
            carry[...] = dkv[T:2 * T, :]

        @pl.when(n == nb)
        def _():
            dkv_ref[...] = carry[...].astype(bf16)
            wait()
            wait2()

    cur = lambda i: (jnp.minimum(i, nb - 1), 0)
    return pl.pallas_call(
        body, name="attn_bwd", grid=(nb + 1,),
        in_specs=[pl.BlockSpec(memory_space=pltpu.SMEM), pl.BlockSpec((T, D_MODEL), cur),
                  pl.BlockSpec((T, 256), lambda i: (jnp.maximum(jnp.minimum(i, nb - 1) - 1, 0), 0)),
                  pl.BlockSpec((T, 256), cur), pl.BlockSpec((T, D_MODEL), cur), pl.BlockSpec((T, 128), cur), _ANY_SPEC,
                  _ANY_SPEC],
        out_specs=[pl.BlockSpec((T, D_MODEL), cur), pl.BlockSpec((T, 256), lambda i: (jnp.maximum(i - 1, 0), 0)),
                   _full((1, 128)), _ANY_SPEC, _ANY_SPEC],
        out_shape=[jax.ShapeDtypeStruct((L, D_MODEL), bf16), jax.ShapeDtypeStruct((L, 256), bf16),
                   jax.ShapeDtypeStruct((1, 128), f32), jax.ShapeDtypeStruct((N_CHIPS - 1,) + pb.shape[1:], bf16),
                   jax.ShapeDtypeStruct((N_CHIPS,) + gb2.shape[2:], bf16)],
        scratch_shapes=[pltpu.VMEM((T, 256), f32)] + _sems(N_CHIPS - 1) + _sems(N_CHIPS),
        compiler_params=_cparams("arbitrary"),
    )(sinks, q, kv, kv, do, lse, pb, gb2)


def _ssd_bwd(xc, dt_raw, dy, prev_all, dt_bias, a_log, d_skip_x, e_mat, g, pb2):
    L = xc.shape[0]
    nc = L // CHUNK
    T = CHUNK
    RG = g.shape[0]

    def body(xc_ref, dt_ref, dy_ref, prev_ref, dtb_ref, al_ref, dsk_ref, e_ref, g_in_ref, pb2_ref,
             dxc_ref, ddt_ref, acc_ref, dd_ref, g_ref, chips2_ref, dst, dxs_s, send_sems, recv_sems, send_sems2, recv_sems2):
        start, wait = _share_job(g_ref, send_sems, recv_sems, RG)
        start2, wait2 = _to_chips_job(pb2_ref, chips2_ref, send_sems2, recv_sems2)

        @pl.when(pl.program_id(0) == 0)
        def _():
            dst[...] = jnp.zeros_like(dst)
            acc_ref[...] = jnp.zeros_like(acc_ref)
            dd_ref[...] = jnp.zeros_like(dd_ref)
            start()
            start2()

        @pl.when(pl.program_id(0) == nc - 1)
        def _():
            wait()
            wait2()

        lane1 = lax.broadcasted_iota(jnp.int32, (1, 128), 1)
        a_row = jnp.where(lane1 < SSD_HEADS, -jnp.exp(al_ref[...]), 0.0)
        e_mat_v = e_ref[...]
        dt, a, cum, dtx, cumx, r, c = _ssd_chunk_common(dt_ref[...], dtb_ref[...], a_row)
        cum_t = cum.T
        ex = jnp.exp(cumx)
        last = cumx[T - 1:T, :]
        wx = jnp.exp(last - cumx)
        cdx = jnp.exp(last)
        xs = xc_ref[:, 0:SSD_WIDTH]
        X = xs * dtx
        Xb = X.astype(bf16)
        Xdb = (X * wx).astype(bf16)
        dyv = dy_ref[...]
        prev = prev_ref[0]
        prevb = prev.astype(bf16)
        dnew = dst[...]
        dnewb = dnew.astype(bf16)
        tri = c <= r
        lane = lax.broadcasted_iota(jnp.int32, (T, 128), 1)
        sub = lax.broadcasted_iota(jnp.int32, (128, T), 0)
        lo = lane < HEAD_DIM

        def red(vals, g):
            return _dot_sel_nt(vals, e_mat_v[:, 512 * g:512 * (g + 1)])

        de = jnp.zeros((T, 128), f32)
        dw = jnp.zeros((T, 128), f32)
        ddt_x = jnp.zeros((T, 128), f32)
        dcum_col = jnp.zeros((T, 128), f32)
        dcum_row = jnp.zeros((128, T), f32)
        dprev_parts, dBs, dCs = [], [], []
        for g in range(SSD_GROUPS):
            s5 = slice(512 * g, 512 * (g + 1))
            Bg = xc_ref[:, 1024 + 128 * g:1152 + 128 * g].astype(bf16)
            Cg = xc_ref[:, 1280 + 128 * g:1408 + 128 * g].astype(bf16)
            G = _dot_nt(Cg, Bg)
            Z = _dot(Cg, prevb[:, s5])
            dyg = dyv[:, s5]
            dZb = (dyg * ex[:, s5]).astype(bf16)
            dXd = _dot(Bg, dnewb[:, s5])
            dC = _dot_nt(dZb, prevb[:, s5])
            dB = _dot_nt(Xdb[:, s5], dnewb[:, s5])
            dprev_parts.append(_dot_tn(Cg, dZb) + dnew[:, s5] * cdx[:, s5])
            de = de + red(dyg * Z, g)
            dw = dw + red(dXd * X[:, s5], g)
            dXg = dXd * wx[:, s5]
            dG = jnp.zeros((T, T), f32)
            for j in range(4):
                blk = 4 * g + j
                sl = slice(128 * blk, 128 * (blk + 1))
                Xblk = Xb[:, sl]
                dyblk = dyv[:, sl]
                dyblk_b = dyblk.astype(bf16)
                dxh = []
                for half in range(2):
                    h = 2 * blk + half
                    seg = jnp.minimum(cum[:, h:h + 1] - cum_t[h:h + 1, :], 0.0)
                    Lm = jnp.where(tri, jnp.exp(seg), 0.0)
                    M = G * Lm
                    dyh = jnp.where(lo if half == 0 else ~lo, dyblk, 0.0).astype(bf16)
                    dM = _dot_nt(dyh, Xblk)
                    dG = dG + dM * Lm
                    Q = dM * M
                    dcum_col = dcum_col + jnp.where(lane == h, jnp.sum(Q, axis=1, keepdims=True), 0.0)
                    dcum_row = dcum_row + jnp.where(sub == h, jnp.sum(Q, axis=0, keepdims=True), 0.0)
                    dxh.append(_dot_tn(M.astype(bf16), dyblk_b))
                dXblk = dXg[:, 128 * j:128 * (j + 1)] + jnp.where(lo, dxh[0], dxh[1])
                xsb = xs[:, sl]
                dxs_s[:, sl] = dXblk * dtx[:, sl] + dsk_ref[:, sl] * dyblk
                ddt_x = ddt_x + _dot_sel_nt(dXblk * xsb, e_mat_v[:, sl])
                dd_ref[:, sl] += jnp.sum(dyblk * xsb, axis=0, keepdims=True)
            dGb = dG.astype(bf16)
            dCs.append(dC + _dot(dGb, Bg))
            dBs.append(dB + _dot_tn(dGb, Cg))
        e16 = jnp.exp(cum)
        cum_last = cum[T - 1:T, :]
        w16 = jnp.exp(cum_last - cum)
        dcd = jnp.sum(dnew * prev, axis=0, keepdims=True)
        dcd16 = red(dcd[:, 0:512], 0) + red(dcd[:, 512:1024], 1)
        dww = dw * w16
        extra = jnp.sum(dww, axis=0, keepdims=True) + dcd16 * jnp.exp(cum_last)
        rowi = lax.broadcasted_iota(jnp.int32, (T, 128), 0)
        dcum = dcum_col - dcum_row.T + de * e16 - dww + jnp.where(rowi == T - 1, extra, 0.0)
        da = _sel_dot((c >= r).astype(bf16), dcum)
        ddt = ddt_x + da * a_row
        acc_ref[0:1, :] += jnp.sum(da * dt, axis=0, keepdims=True)
        ddt_raw = jnp.where(lane < SSD_HEADS, ddt * _sigmoid(dt_ref[...] + dtb_ref[...]), 0.0)
        ddt_ref[...] = ddt_raw
        acc_ref[1:2, :] += jnp.sum(ddt_raw, axis=0, keepdims=True)
        dxc_ref[:, 0:SSD_WIDTH] = dxs_s[...]
        dxc_ref[:, 1024:1280] = jnp.concatenate(dBs, axis=1)
        dxc_ref[:, 1280:1536] = jnp.concatenate(dCs, axis=1)
        dst[...] = jnp.concatenate(dprev_parts, axis=1)

    rev = lambda i: (nc - 1 - i, 0)
    return pl.pallas_call(
        body, name="ssd_bwd", grid=(nc,),
        in_specs=[pl.BlockSpec((T, CONV_DIM), rev), pl.BlockSpec((T, 128), rev), pl.BlockSpec((T, SSD_WIDTH), rev),
                  pl.BlockSpec((1, SSD_STATE, SSD_WIDTH), lambda i: (nc - 1 - i, 0, 0)),
                  _full((1, 128)), _full((1, 128)), _full((1, SSD_WIDTH)), _full((128, SSD_WIDTH)), _ANY_SPEC, _ANY_SPEC],
        out_specs=[pl.BlockSpec((T, CONV_DIM), rev), pl.BlockSpec((T, 128), rev), _full((8, 128)),
                   _full((1, SSD_WIDTH)), _ANY_SPEC, _ANY_SPEC],
        out_shape=[jax.ShapeDtypeStruct((L, CONV_DIM), f32), jax.ShapeDtypeStruct((L, 128), f32),
                   jax.ShapeDtypeStruct((8, 128), f32), jax.ShapeDtypeStruct((1, SSD_WIDTH), f32),
                   jax.ShapeDtypeStruct(g.shape, f32), jax.ShapeDtypeStruct((N_CHIPS - 1,) + pb2.shape[1:], bf16)],
        input_output_aliases={8: 4},
        scratch_shapes=[pltpu.VMEM((SSD_STATE, SSD_WIDTH), f32), pltpu.VMEM((T, SSD_WIDTH), f32)] + _sems(1)
        + _sems(N_CHIPS - 1),
        compiler_params=_cparams("arbitrary"),
    )(xc, dt_raw, dy, prev_all, dt_bias, a_log, d_skip_x, e_mat, g, pb2)


def _conv_bwd(dxc, xr, conv_w, conv_b, g):
    L = dxc.shape[0]
    tm = 256
    nt = L // tm
    RG = g.shape[0]

    def body(dxc_ref, xr_ref, xh_ref, cw_ref, cb_ref, g_in_ref, dxr_ref, acc_ref, g_ref, carry, buf, ext, send_sems, recv_sems):
        i = pl.program_id(0)
        start, wait = _share_job(g_ref, send_sems, recv_sems, RG)

        @pl.when(i == 0)
        def _():
            carry[...] = jnp.zeros_like(carry)
            acc_ref[...] = jnp.zeros_like(acc_ref)
            ext[tm + 16:tm + CHUNK, :] = jnp.zeros((CHUNK - 16, CONV_DIM), bf16)
            start()

        @pl.when(i == nt - 1)
        def _():
            wait()

        buf[0:8, :] = jnp.where(i == nt - 1, 0.0, xh_ref[...])
        u = xr_ref[...]
        buf[8:8 + tm, :] = u
        pre = cb_ref[...] + cw_ref[CONV_K - 1:CONV_K, :] * u
        for k in range(CONV_K - 1):
            pre = pre + cw_ref[k:k + 1, :] * buf[5 + k:5 + k + tm, :]
        sg = _sigmoid(pre)
        dpre = dxc_ref[...] * (sg * (1.0 + pre * (1.0 - sg)))
        acc_ref[4:5, :] += jnp.sum(dpre, axis=0, keepdims=True)
        dpb = dpre.astype(bf16)
        ext[0:tm, :] = dpb
        ext[tm:tm + 16, :] = carry[...]
        acc_ref[CONV_K - 1:CONV_K, :] += jnp.sum(u * dpre, axis=0, keepdims=True)
        du = cw_ref[CONV_K - 1:CONV_K, :] * dpre
        r = lax.broadcasted_iota(jnp.int32, (CHUNK, 2 * CHUNK), 0)
        c = lax.broadcasted_iota(jnp.int32, (CHUNK, 2 * CHUNK), 1)
        for j in range(1, CONV_K):
            move = (c == r + j).astype(bf16)
            up = jnp.concatenate([_dot(move, ext[CHUNK * b:CHUNK * (b + 2), :]) for b in range(tm // CHUNK)], axis=0)
            k = CONV_K - 1 - j
            du = du + cw_ref[k:k + 1, :] * up
            acc_ref[k:k + 1, :] += jnp.sum(u * up, axis=0, keepdims=True)
        dxr_ref[...] = du.astype(bf16)
        carry[...] = dpb[0:16, :]

    rev = lambda i: (nt - 1 - i, 0)
    return pl.pallas_call(
        body, name="conv_bwd", grid=(nt,),
        in_specs=[pl.BlockSpec((tm, CONV_DIM), rev), pl.BlockSpec((tm, CONV_DIM), rev),
                  pl.BlockSpec((8, CONV_DIM), lambda i: (jnp.maximum((nt - 1 - i) * (tm // 8) - 1, 0), 0)),
                  _full((CONV_K, CONV_DIM)), _full((1, CONV_DIM)), _ANY_SPEC],
        out_specs=[pl.BlockSpec((tm, CONV_DIM), rev), _full((8, CONV_DIM)), _ANY_SPEC],
        out_shape=[jax.ShapeDtypeStruct((L, CONV_DIM), bf16), jax.ShapeDtypeStruct((8, CONV_DIM), f32),
                   jax.ShapeDtypeStruct(g.shape, f32)],
        input_output_aliases={5: 2},
        scratch_shapes=[pltpu.VMEM((16, CONV_DIM), bf16), pltpu.VMEM((tm + 8, CONV_DIM), f32),
                        pltpu.VMEM((tm + CHUNK, CONV_DIM), bf16)] + _sems(1),
        compiler_params=_cparams("arbitrary"),
    )(dxc, xr, xr, conv_w, conv_b, g)


def _inproj_bwd(dz, dxr, dq, dkv, ddt, dr1, x, mod, ln_g, ln_b, w_in, pb):
    L = x.shape[0]
    tm = DENSE_TM
    nt = L // tm

    def body(dz_ref, dxr_ref, dq_ref, dkv_ref, ddt_ref, dr1_ref, x_ref, mod_ref, g_ref, b_ref, w_ref, pb_ref,
             dx_ref, acc_ref, chips_ref, send_sems, recv_sems):
        start, wait = _to_chips_job(pb_ref, chips_ref, send_sems, recv_sems)

        @pl.when(pl.program_id(0) == 0)
        def _():
            acc_ref[...] = jnp.zeros_like(acc_ref)
            start()

        @pl.when(pl.program_id(0) == nt - 1)
        def _():
            wait()

        du1 = (_dot(dz_ref[...], w_ref[W_Z, :]) + _dot(dxr_ref[...], w_ref[W_XBC, :])
               + _dot(dq_ref[...], w_ref[W_Q, :]) + _dot(dkv_ref[...], w_ref[W_KV, :])
               + _dot(ddt_ref[...].astype(bf16), w_ref[W_DT, :]))
        xhat, rstd = _ln_stats(x_ref[...])
        h0 = xhat * g_ref[...] + b_ref[...]
        acc_ref[0:1, :] += jnp.sum(du1 * h0, axis=0, keepdims=True)
        acc_ref[1:2, :] += jnp.sum(du1, axis=0, keepdims=True)
        dh0 = du1 * (1.0 + mod_ref[1:2, :]) + ALPHA * dr1_ref[...]
        acc_ref[2:3, :] += jnp.sum(dh0 * xhat, axis=0, keepdims=True)
        acc_ref[3:4, :] += jnp.sum(dh0, axis=0, keepdims=True)
        dx_ref[...] = _ln_bwd(dh0, xhat, rstd, g_ref[...])

    v = _full((1, D_MODEL))
    return pl.pallas_call(
        body, name="inproj_bwd", grid=(nt,),
        in_specs=[_rows(tm, D_MODEL), _rows(tm, CONV_DIM), _rows(tm, D_MODEL), _rows(tm, 256), _rows(tm, 128),
                  _rows(tm, D_MODEL), _rows(tm, D_MODEL), _full((8, D_MODEL)), v, v, _resident((PROJ_WIDTH, D_MODEL)),
                  _ANY_SPEC],
        out_specs=[_rows(tm, D_MODEL), _full((8, D_MODEL)), _ANY_SPEC],
        out_shape=[jax.ShapeDtypeStruct((L, D_MODEL), f32), jax.ShapeDtypeStruct((8, D_MODEL), f32),
                   jax.ShapeDtypeStruct((N_CHIPS - 1,) + pb.shape[1:], bf16)],
        scratch_shapes=_sems(N_CHIPS - 1),
        compiler_params=_cparams("arbitrary"),
    )(dz, dxr, dq, dkv, ddt, dr1, x, mod, ln_g, ln_b, w_in, pb)


def _adamw_math(w, g, m, v):
    m = ADAM_B1 * m + (1.0 - ADAM_B1) * g
    v = ADAM_B2 * v + (1.0 - ADAM_B2) * (g * g)
    m_hat = m / (1.0 - ADAM_B1 ** ADAM_STEP)
    v_hat = v / (1.0 - ADAM_B2 ** ADAM_STEP)
    delta = -ADAM_LR * (m_hat / (jnp.sqrt(v_hat) + ADAM_EPS) + ADAM_WD * w)
    return delta, m, v


def _adamw(w, g, m, v, name, g_row0=0):
    R, C = w.shape

    def body(w_ref, g_ref, m_ref, v_ref, g2_ref, d_ref, m2_ref, v2_ref):
        gv = g_ref[...]
        g2_ref[...] = gv
        d_ref[...], m2_ref[...], v2_ref[...] = _adamw_math(w_ref[...], gv, m_ref[...], v_ref[...])

    cap = max(8, ADAMW_BLOCK_ELEMS // C)
    tr = R if R <= cap else next(t for t in range(cap - cap % 8, 7, -8) if R % t == 0)
    spec = pl.BlockSpec((tr, C), lambda i: (i, 0))
    g_spec = pl.BlockSpec((tr, C), lambda i: (g_row0 // tr + i, 0))
    return pl.pallas_call(
        body, name=name, grid=(R // tr,), in_specs=[spec, g_spec, spec, spec], out_specs=[spec] * 4,
        out_shape=[jax.ShapeDtypeStruct((R, C), f32)] * 4, compiler_params=_cparams("parallel"),
    )(w, g, m, v)


def _adamw_rows(w, g, m, v):
    R, _, C = w.shape
    tr = R // 4

    def body(w_ref, g_ref, m_ref, v_ref, d_ref, m2_ref, v2_ref):
        d_ref[...], m2_ref[...], v2_ref[...] = _adamw_math(w_ref[...], g_ref[...], m_ref[...], v_ref[...])

    spec = pl.BlockSpec((tr, 1, C), lambda i: (i, 0, 0))
    return pl.pallas_call(
        body, name="adamw_w_in", grid=(R // tr,), in_specs=[spec] * 4, out_specs=[spec] * 3,
        out_shape=[jax.ShapeDtypeStruct((R, 1, C), f32)] * 3, compiler_params=_cparams("parallel"),
    )(w, g, m, v)


ADA_COLS = 6 * D_MODEL // N_CHIPS
ADA_TN = 512


COND_LANES = 512


def _prologue(cond, ada_w, ada_b, blob):
    R = blob.shape[0]

    def body(cond_ref, w_ref, b_ref, blob_ref, call_ref, mod_ref, wall_ref, mod_s, stage, gs, gr, ms, mr, ws, wr, local_sem):
        x, y, c = _place()
        start_w, finish_w = _gather_job(blob_ref, wall_ref, ws, wr, R)

        def rows(ref, px, py, pc):
            return ref.at[pl.ds(pl.multiple_of((4 * px + 2 * py + pc) * 8, 8), 8), :]

        mine = pltpu.make_async_copy(cond_ref, rows(call_ref, x, y, c), local_sem)
        mine.start()
        sends = [_remote(cond_ref, rows(call_ref, x, y, c), gs, gr, m - 1, _flip(x, y, c, m)) for m in range(1, N_DEV)]
        for cp in sends:
            cp.start()
        for m in range(1, N_DEV):
            peer = _flip(x, y, c, m)
            _remote(cond_ref, rows(call_ref, *peer), gs, gr, m - 1, peer).wait_recv()
        for cp in sends:
            cp.wait_send()
        mine.wait()
        start_w()

        for k in range(R // STAGE_ROWS):
            part = pl.ds(STAGE_ROWS * k, STAGE_ROWS)
            cin = pltpu.make_async_copy(blob_ref.at[part, :], stage, local_sem)
            cin.start()
            cin.wait()
            cout = pltpu.make_async_copy(stage, wall_ref.at[2 * x + y, part, :], local_sem)
            cout.start()
            cout.wait()

        lo = jnp.concatenate([call_ref[8 * d:8 * d + 1, :] for d in range(N_DEV)], axis=0)
        hi = jnp.concatenate([call_ref[8 * d + 1:8 * d + 2, :] for d in range(N_DEV)], axis=0)
        mod_all = (_dot_exact(lo * _sigmoid(lo), w_ref[0:COND_LANES, :]) + _dot_exact(hi * _sigmoid(hi), w_ref[COND_LANES:, :])
                   + b_ref[...])
        for d in range(N_DEV):
            mod_s[8 * d:8 * d + 8, :] = jnp.broadcast_to(mod_all[d:d + 1, :], (8, ADA_COLS))

        mine = pltpu.make_async_copy(rows(mod_s, x, y, c), mod_ref.at[2 * x + y], local_sem)
        mine.start()
        sends = []
        for m in range(1, N_CHIPS):
            peer = _flip(x, y, c, 2 * m)
            sends.append(_remote(rows(mod_s, *peer), mod_ref.at[2 * x + y], ms, mr, m - 1, peer))
        for cp in sends:
            cp.start()
        for m in range(1, N_CHIPS):
            px, py, pc = _flip(x, y, c, 2 * m)
            _remote(rows(mod_s, x, y, c), mod_ref.at[2 * px + py], ms, mr, m - 1, (px, py, pc)).wait_recv()
        for cp in sends:
            cp.wait_send()
        mine.wait()
        finish_w()

    return pl.pallas_call(
        body, name="prologue",
        out_shape=[jax.ShapeDtypeStruct((8 * N_DEV, COND_LANES), f32), jax.ShapeDtypeStruct((N_CHIPS, 8, ADA_COLS), f32),
                   jax.ShapeDtypeStruct((N_CHIPS, R, D_MODEL), bf16)],
        in_specs=[_VMEM_SPEC, _VMEM_SPEC, _VMEM_SPEC, _ANY_SPEC], out_specs=[_VMEM_SPEC, _VMEM_SPEC, _ANY_SPEC],
        scratch_shapes=[pltpu.VMEM((8 * N_DEV, ADA_COLS), f32), pltpu.VMEM((STAGE_ROWS, D_MODEL), bf16)]
        + _sems(N_DEV - 1) + _sems(N_CHIPS - 1) + _sems(6) + [pltpu.SemaphoreType.DMA],
        compiler_params=pltpu.CompilerParams(vmem_limit_bytes=VMEM_LIMIT),
    )(cond, ada_w, ada_b, blob)


def _ada_bwd(c_all, dmod, w, m, v):
    def body(c_ref, d_ref, w_ref, m_ref, v_ref, g_ref, dl_ref, m2_ref, v2_ref):
        cv = c_ref[...]
        g = lax.dot_general(cv * _sigmoid(cv), d_ref[...], (((0,), (0,)), ((), ())), preferred_element_type=f32,
                            precision=lax.Precision.HIGHEST)
        g_ref[...] = g
        dl_ref[...], m2_ref[...], v2_ref[...] = _adamw_math(w_ref[...], g, m_ref[...], v_ref[...])

    wspec = pl.BlockSpec((D_MODEL, ADA_TN), lambda j: (0, j))
    return pl.pallas_call(
        body, name="ada_bwd", grid=(ADA_COLS // ADA_TN,),
        in_specs=[_full((N_DEV, D_MODEL)), pl.BlockSpec((N_DEV, ADA_TN), lambda j: (0, j)), wspec, wspec, wspec],
        out_specs=[wspec] * 4, out_shape=[jax.ShapeDtypeStruct((D_MODEL, ADA_COLS), f32)] * 4,
        compiler_params=_cparams("parallel"),
    )(c_all, dmod, w, m, v)


SMALL_SLOTS = (("ada_b", 6144), ("ln_in_g", 1024), ("ln_in_b", 1024), ("conv_b", 1536), ("dt_bias", 128), ("a_log", 128),
               ("d_skip", 128), ("ssd_norm_w", 1024), ("attn_sinks", 128), ("ln1_g", 1024), ("ln1_b", 1024),
               ("b_ff1", 4096), ("b_ff2", 1024), ("ln2_g", 1024), ("ln2_b", 1024), ("conv_w", 6144), ("loss", 1024))
SMALL_N = sum(n for _, n in SMALL_SLOTS)
SMALL_OFF = {name: sum(n for _, n in SMALL_SLOTS[:i]) for i, (name, _) in enumerate(SMALL_SLOTS)}
SMALL_PARAMS = tuple(name for name, _ in SMALL_SLOTS[:15])
assert SMALL_N % 1024 == 0


def _small_pack(acc_in, acc_out, acc_mlp, db1, acc_conv, acc_ssd, dd_x, dsink, alog, e_mat):
    def body(in_ref, out_ref, mlp_ref, db1_ref, conv_ref, ssd_ref, dd_ref, sink_ref, al_ref, e_ref, o_ref):
        def put(name, val, at=0):
            off = SMALL_OFF[name] + at
            o_ref[:, off:off + val.shape[1]] = val

        for k, row in enumerate((in_ref[1:2, :], in_ref[0:1, :], out_ref[0:1, :], mlp_ref[A_SH2:A_SH2 + 1, :],
                                 mlp_ref[A_SC2:A_SC2 + 1, :], mlp_ref[A_G2:A_G2 + 1, :])):
            put("ada_b", row, D_MODEL * k)
        put("ln_in_g", in_ref[2:3, :])
        put("ln_in_b", in_ref[3:4, :])
        put("conv_b", conv_ref[4:5, :])
        put("dt_bias", ssd_ref[1:2, :])
        put("a_log", ssd_ref[0:1, :] * (-jnp.exp(al_ref[...])))
        put("d_skip", _dot_sel_nt(jnp.broadcast_to(dd_ref[...], (8, SSD_WIDTH)), e_ref[...])[0:1, :])
        put("ssd_norm_w", out_ref[1:2, :])
        put("attn_sinks", sink_ref[...])
        put("ln1_g", mlp_ref[A_LN1G:A_LN1G + 1, :])
        put("ln1_b", mlp_ref[A_LN1B:A_LN1B + 1, :])
        put("b_ff1", db1_ref[...])
        put("b_ff2", mlp_ref[A_B2:A_B2 + 1, :])
        put("ln2_g", mlp_ref[A_LN2G:A_LN2G + 1, :])
        put("ln2_b", mlp_ref[A_LN2B:A_LN2B + 1, :])
        for k in range(CONV_K):
            put("conv_w", conv_ref[k:k + 1, :], CONV_DIM * k)
        put("loss", mlp_ref[A_LOSS:A_LOSS + 1, :])

    return pl.pallas_call(body, name="small_pack", out_shape=jax.ShapeDtypeStruct((1, SMALL_N), f32),
                          compiler_params=_cparams())(acc_in, acc_out, acc_mlp, db1, acc_conv, acc_ssd, dd_x, dsink, alog, e_mat)


def _small_update(gathered, params, moms, vels):
    k = len(SMALL_PARAMS)

    def body(g_ref, *refs):
        w_refs, m_refs, v_refs, outs = refs[:k], refs[k:2 * k], refs[2 * k:3 * k], refs[3 * k:]

        def total(name, n):
            off = SMALL_OFF[name]
            g = g_ref[0:1, off:off + n]
            for i in range(1, N_DEV):
                g = g + g_ref[i:i + 1, off:off + n]
            return g

        for j, name in enumerate(SMALL_PARAMS):
            n = w_refs[j].shape[1]
            g = total(name, max(n, 128))[:, :n]
            outs[4 * j][...] = g
            outs[4 * j + 1][...], outs[4 * j + 2][...], outs[4 * j + 3][...] = _adamw_math(
                w_refs[j][...], g, m_refs[j][...], v_refs[j][...])
        outs[4 * k][...] = total("conv_w", CONV_K * CONV_DIM)
        outs[4 * k + 1][...] = total("loss", D_MODEL)

    shapes = [jax.ShapeDtypeStruct(p.shape, f32) for p in params for _ in range(4)]
    shapes += [jax.ShapeDtypeStruct((1, CONV_K * CONV_DIM), f32), jax.ShapeDtypeStruct((1, D_MODEL), f32)]
    return pl.pallas_call(body, name="small_update", out_shape=shapes,
                          compiler_params=_cparams())(gathered, *params, *moms, *vels)


def _place():
    return lax.axis_index("x"), lax.axis_index("y"), lax.axis_index("c")


def _flip(x, y, c, m):
    return (1 - x if m & 4 else x, 1 - y if m & 2 else y, 1 - c if m & 1 else c)


_VMEM_SPEC = pl.BlockSpec(memory_space=pltpu.VMEM)
_ANY_SPEC = pl.BlockSpec(memory_space=pl.ANY)


def _allgather8(v, name):
    n = v.shape[1]

    def body(v_ref, out_ref, send_sems, recv_sems, local_sem):
        x, y, c = _place()

        def rows(px, py, pc):
            return out_ref.at[pl.ds(pl.multiple_of((4 * px + 2 * py + pc) * 8, 8), 8), :]

        def copy(m, src, dst, to):
            return pltpu.make_async_remote_copy(src_ref=src, dst_ref=dst, send_sem=send_sems.at[m - 1],
                                                recv_sem=recv_sems.at[m - 1], device_id=to, device_id_type=MESH)

        mine = pltpu.make_async_copy(v_ref, rows(x, y, c), local_sem)
        mine.start()
        sends = [copy(m, v_ref, rows(x, y, c), _flip(x, y, c, m)) for m in range(1, N_DEV)]
        for cp in sends:
            cp.start()
        for m in range(1, N_DEV):
            peer = _flip(x, y, c, m)
            copy(m, v_ref, rows(*peer), peer).wait_recv()
        for cp in sends:
            cp.wait_send()
        mine.wait()

    return pl.pallas_call(
        body, name=name, out_shape=jax.ShapeDtypeStruct((8 * N_DEV, n), f32), in_specs=[_VMEM_SPEC],
        out_specs=_VMEM_SPEC,
        scratch_shapes=[pltpu.SemaphoreType.DMA((N_DEV - 1,)), pltpu.SemaphoreType.DMA((N_DEV - 1,)),
                        pltpu.SemaphoreType.DMA],
    )(v)


def _remote(src, dst, send_sems, recv_sems, k, to):
    return pltpu.make_async_remote_copy(src_ref=src, dst_ref=dst, send_sem=send_sems.at[k], recv_sem=recv_sems.at[k],
                                        device_id=to, device_id_type=MESH)


def _gather_job(blob_ref, out_ref, send_sems, recv_sems, R):
    x, y, c = _place()
    sib = (x, y, 1 - c)
    hr = R // 2

    def half(px, py, pc):
        return out_ref.at[2 * px + py, pl.ds(pl.multiple_of(pc * hr, 16), hr), :]

    my_half = blob_ref.at[pl.ds(pl.multiple_of(c * hr, 16), hr), :]

    def first():
        return [_remote(my_half, half(x, y, c), send_sems, recv_sems, m - 1, _flip(x, y, c, 2 * m))
                for m in range(1, N_CHIPS)]

    def start():
        for cp in first():
            cp.start()

    def finish():
        passed = []
        for m in range(1, N_CHIPS):
            px, py, pc = _flip(x, y, c, 2 * m)
            _remote(my_half, half(px, py, pc), send_sems, recv_sems, m - 1, (px, py, pc)).wait_recv()
            fwd = _remote(half(px, py, pc), half(px, py, pc), send_sems, recv_sems, 2 + m, sib)
            fwd.start()
            passed.append(fwd)
        for m in range(1, N_CHIPS):
            px, py, pc = _flip(x, y, c, 2 * m)
            _remote(my_half, half(px, py, 1 - pc), send_sems, recv_sems, 2 + m, sib).wait_recv()
        for cp in first() + passed:
            cp.wait_send()

    return start, finish


def _to_sibling_job(g_ref, out_ref, send_sems, recv_sems):
    x, y, c = _place()

    def cps():
        return [_remote(g_ref.at[j, 1 - c], out_ref.at[j], send_sems, recv_sems, j, (x, y, 1 - c)) for j in range(N_CHIPS)]

    def start():
        for cp in cps():
            cp.start()

    def wait():
        for cp in cps():
            cp.wait()

    return start, wait


def _to_chips_job(p_ref, out_ref, send_sems, recv_sems):
    x, y, c = _place()

    def cps():
        out = []
        for m in range(1, N_CHIPS):
            px, py, pc = _flip(x, y, c, 2 * m)
            out.append(_remote(p_ref.at[2 * px + py], out_ref.at[m - 1], send_sems, recv_sems, m - 1, (px, py, pc)))
        return out

    def start():
        for cp in cps():
            cp.start()

    def wait():
        for cp in cps():
            cp.wait()

    return start, wait


def _share_job(g_ref, send_sems, recv_sems, R):
    x, y, c = _place()

    def rows(pc):
        return g_ref.at[pl.ds(pl.multiple_of(pc * (R // 2), 8), R // 2), :]

    def start():
        _remote(rows(c), rows(c), send_sems, recv_sems, 0, (x, y, 1 - c)).start()

    def wait():
        _remote(rows(c), rows(1 - c), send_sems, recv_sems, 0, (x, y, 1 - c)).wait_recv()
        _remote(rows(c), rows(c), send_sems, recv_sems, 0, (x, y, 1 - c)).wait_send()

    return start, wait


def _sems(n):
    return [pltpu.SemaphoreType.DMA((n,)), pltpu.SemaphoreType.DMA((n,))]


def _rs_to_sibling(gb):
    def body(g_ref, out_ref, send_sems, recv_sems):
        start, wait = _to_sibling_job(g_ref, out_ref, send_sems, recv_sems)
        start()
        wait()

    return pl.pallas_call(
        body, name="rs_to_sibling", out_shape=jax.ShapeDtypeStruct((N_CHIPS,) + gb.shape[2:], bf16),
        in_specs=[_ANY_SPEC], out_specs=_ANY_SPEC, scratch_shapes=_sems(N_CHIPS),
    )(gb)


def _rs_share(g):
    R = g.shape[0]

    def body(g_ref, out_ref, send_sems, recv_sems):
        start, wait = _share_job(out_ref, send_sems, recv_sems, R)
        start()
        wait()

    return pl.pallas_call(
        body, name="rs_share", out_shape=jax.ShapeDtypeStruct(g.shape, f32), in_specs=[_ANY_SPEC],
        out_specs=_ANY_SPEC, input_output_aliases={0: 0}, scratch_shapes=_sems(1),
    )(g)


RS_TR_MAX = 512


def _rs_sum_pair(place, gb, recv, name):
    hr = gb.shape[2]
    tr = min(hr, RS_TR_MAX)

    def body(pl_ref, g_ref, r_ref, o_ref):
        o_ref[0] = (g_ref[0, 0].astype(f32) + r_ref[0].astype(f32)).astype(bf16)

    return pl.pallas_call(
        body, name=name,
        grid_spec=pltpu.PrefetchScalarGridSpec(
            num_scalar_prefetch=1, grid=(N_CHIPS, hr // tr),
            in_specs=[pl.BlockSpec((1, 1, tr, D_MODEL), lambda j, i, p: (j, p[0], i, 0)),
                      pl.BlockSpec((1, tr, D_MODEL), lambda j, i, p: (j, i, 0))],
            out_specs=pl.BlockSpec((1, tr, D_MODEL), lambda j, i, p: (j, i, 0))),
        out_shape=jax.ShapeDtypeStruct((N_CHIPS, hr, D_MODEL), bf16),
        compiler_params=_cparams("parallel", "parallel"),
    )(place, gb, recv)


def _rs_sum_chips(place, gb, recv_sib, recv_chips, name):
    hr = gb.shape[2]
    tr = min(hr, RS_TR_MAX)
    nt = hr // tr

    def body(pl_ref, g_ref, r1_ref, r2_ref, o_ref):
        acc = g_ref[0, 0].astype(f32) + r1_ref[0].astype(f32)
        for k in range(N_CHIPS - 1):
            acc = acc + r2_ref[k].astype(f32)
        o_ref[...] = acc

    return pl.pallas_call(
        body, name=name,
        grid_spec=pltpu.PrefetchScalarGridSpec(
            num_scalar_prefetch=1, grid=(nt,),
            in_specs=[pl.BlockSpec((1, 1, tr, D_MODEL), lambda i, p: (p[1], p[0], i, 0)),
                      pl.BlockSpec((1, tr, D_MODEL), lambda i, p: (p[1], i, 0)),
                      pl.BlockSpec((N_CHIPS - 1, tr, D_MODEL), lambda i, p: (0, i, 0))],
            out_specs=pl.BlockSpec((tr, D_MODEL), lambda i, p: (p[0] * nt + i, 0))),
        out_shape=jax.ShapeDtypeStruct((2 * hr, D_MODEL), f32),
        compiler_params=_cparams("parallel"),
    )(place, gb, recv_sib, recv_chips)


def _pad128(v):
    v = v.reshape(1, -1)
    return jnp.pad(v, ((0, 0), (0, 128 - v.shape[1])))


def _row(v):
    return v.reshape(1, -1)


W_COLS = PROJ_WIDTH // N_CHIPS


def _g_in_blocks(gz, gxbc, gdt, gq, gkv):
    g = jnp.concatenate([gz, gxbc, gdt[:W_DT_ROWS], gq, gkv], axis=0)
    return jnp.pad(g.reshape(N_CHIPS, W_COLS, D_MODEL), ((0, 0), (0, D_MODEL - W_COLS), (0, 0)))


def kernel(x, c, ln_in_g, ln_in_b, ada_w, ada_b, w_in, conv_w, conv_b, dt_bias, a_log, d_skip, ssd_norm_w, attn_sinks, w_out, ln1_g, ln1_b, w_ff1, b_ff1, w_ff2, b_ff2, ln2_g, ln2_b, loss_target, m_ln_in_g, m_ln_in_b, m_ada_w, m_ada_b, m_w_in, m_conv_w, m_conv_b, m_dt_bias, m_a_log, m_d_skip, m_ssd_norm_w, m_attn_sinks, m_w_out, m_ln1_g, m_ln1_b, m_w_ff1, m_b_ff1, m_w_ff2, m_b_ff2, m_ln2_g, m_ln2_b, v_ln_in_g, v_ln_in_b, v_ada_w, v_ada_b, v_w_in, v_conv_w, v_conv_b, v_dt_bias, v_a_log, v_d_skip, v_ssd_norm_w, v_attn_sinks, v_w_out, v_ln1_g, v_ln1_b, v_w_ff1, v_b_ff1, v_w_ff2, v_b_ff2, v_ln2_g, v_ln2_b):
    xi, yi, ci = _place()
    chip = 2 * xi + yi
    place = jnp.stack([ci, chip]).astype(jnp.int32)
    x2, tgt = x[0], loss_target[0]

    def as_rows(a):
        return jnp.transpose(a, (2, 0, 1))

    def from_rows(a):
        return jnp.transpose(a, (1, 2, 0))

    cond = jnp.concatenate([c.reshape(2, COND_LANES), conv_w.reshape(3, COND_LANES), jnp.zeros((3, COND_LANES), f32)], axis=0)
    ada_b_mine = lax.dynamic_slice(ada_b, (0, chip * ADA_COLS), (1, ADA_COLS))
    blob_in = jnp.pad(w_in[0].T, ((0, D_MODEL - W_COLS), (0, 0))).astype(bf16)
    cond_all, mod_rows, wall_in = _prologue(cond, ada_w[0], ada_b_mine, blob_in)
    cond_all = cond_all.reshape(N_DEV, 8, COND_LANES)
    c_all = cond_all[:, 0:2].reshape(N_DEV, D_MODEL)
    conv_w_full = jnp.concatenate([cond_all[2 * j, 2:5].reshape(CONV_K, 384) for j in range(N_CHIPS)], axis=1)
    mod = jnp.concatenate([mod_rows[:, 0].reshape(6, D_MODEL), jnp.zeros((2, D_MODEL), f32)], axis=0)
    w_in_f = wall_in[:, :W_COLS].reshape(PROJ_WIDTH, D_MODEL)
    b_ffw = jnp.concatenate([w_ff1[0], w_ff2[0]], axis=0).astype(bf16)
    b_outw = w_out[0].astype(bf16)

    def with_mine(wall, mine):
        return lax.dynamic_update_slice(wall, mine[None], (chip, 0, 0))

    e_mat = _head_expand()
    dsk_x = jnp.repeat(d_skip[0], HEAD_DIM).reshape(1, SSD_WIDTH)
    dtb, alog = _pad128(dt_bias), _pad128(a_log)
    sinks = attn_sinks[0]
    lng, lnb = _row(ln_in_g), _row(ln_in_b)
    u1, z, xr, xc, q, kv, dtr = _inproj_fwd(x2, mod, lng, lnb, w_in_f, conv_w_full, conv_b)
    y, prev_all, wall_out = _ssd_fwd(xc, dtr, dtb, alog, dsk_x, b_outw)
    o, lse, wall_ff = _attn_fwd(q, kv, sinks, b_ffw)
    wall_ff, wall_out = with_mine(wall_ff, b_ffw), with_mine(wall_out, b_outw)
    yn, mix, r1 = _outproj_fwd(y, z, o, x2, mod, lng, lnb, ssd_norm_w, wall_out)

    dr1, u2, s_act, da, df, acc_mlp, db1 = _mlp_fwd_bwd(r1, tgt, mod, ln1_g, ln1_b, ln2_g, ln2_b, wall_ff, b_ff1, wall_ff,
                                                        b_ff2)
    ga = jnp.zeros((N_CHIPS, GA_ROWS, D_MODEL), bf16)
    ga = _wgrad_blob(ga, u2, da, "wgrad_ff1", lambda t, n: (n, t))
    ga = _wgrad_blob(ga, s_act, df, "wgrad_ff2", lambda t, n: (t // 2, 2 + t % 2))
    ga = ga.reshape(N_CHIPS, 2, GA_ROWS // 2, D_MODEL)
    dy, dz, do, dmix, acc_out, a_sib = _outproj_bwd(dr1, mix, y, z, mod, ssd_norm_w, wall_out, ga)
    gc = jnp.zeros((N_CHIPS, GC_ROWS, D_MODEL), bf16)
    gc = _wgrad_blob(gc, yn, dmix, "wgrad_out_y", lambda t, n: (t, 0))
    gc = _wgrad_blob(gc, o, dmix, "wgrad_out_o", lambda t, n: (2 + t, 0))
    gc = gc.reshape(N_CHIPS, 2, GC_ROWS // 2, D_MODEL)
    a_pair = _rs_sum_pair(place, ga, a_sib, "rs_sum_pair_a")
    dq, dkv, dsink, a_chips, c_sib = _attn_bwd(q, kv, do, lse, sinks, a_pair, gc)
    g_a = _rs_sum_chips(place, ga, a_sib, a_chips, "rs_sum_chips_a")
    c_pair = _rs_sum_pair(place, gc, c_sib, "rs_sum_pair_c")
    dxc, ddt, acc_ssd, dd_x, g_a, c_chips = _ssd_bwd(xc, dtr, dy, prev_all, dtb, alog, dsk_x, e_mat, g_a, c_pair)
    g_c = _rs_sum_chips(place, gc, c_sib, c_chips, "rs_sum_chips_c")
    dxr, acc_conv, g_c = _conv_bwd(dxc, xr, conv_w_full, conv_b, g_c)
    gb = _g_in_blocks(_wgrad(dz, u1, "wgrad_in_z"), _wgrad(dxr, u1, "wgrad_in_xbc"),
                      _wgrad(ddt.astype(bf16), u1, "wgrad_in_dt"), _wgrad(dq, u1, "wgrad_in_q"),
                      _wgrad(dkv, u1, "wgrad_in_kv")).reshape(N_CHIPS, 2, GB_ROWS // 2, D_MODEL)
    b_sib = _rs_to_sibling(gb)
    b_pair = _rs_sum_pair(place, gb, b_sib, "rs_sum_pair_b")
    grad_x, acc_in, b_chips = _inproj_bwd(dz, dxr, dq, dkv, ddt, dr1, x2, mod, lng, lnb, w_in_f, b_pair)
    g_b = _rs_share(_rs_sum_chips(place, gb, b_sib, b_chips, "rs_sum_chips_b"))

    packed = _small_pack(acc_in, acc_out, acc_mlp, db1, acc_conv, acc_ssd, dd_x, dsink, alog, e_mat)
    small_all = _allgather8(packed.reshape(8, SMALL_N // 8), "gather_small").reshape(N_DEV, SMALL_N)
    given = dict(ada_b=(ada_b, m_ada_b, v_ada_b), ln_in_g=(ln_in_g, m_ln_in_g, v_ln_in_g), ln_in_b=(ln_in_b, m_ln_in_b, v_ln_in_b),
                 conv_b=(conv_b, m_conv_b, v_conv_b), dt_bias=(dt_bias, m_dt_bias, v_dt_bias), a_log=(a_log, m_a_log, v_a_log),
                 d_skip=(d_skip, m_d_skip, v_d_skip), ssd_norm_w=(ssd_norm_w, m_ssd_norm_w, v_ssd_norm_w),
                 attn_sinks=(attn_sinks, m_attn_sinks, v_attn_sinks), ln1_g=(ln1_g, m_ln1_g, v_ln1_g),
                 ln1_b=(ln1_b, m_ln1_b, v_ln1_b), b_ff1=(b_ff1, m_b_ff1, v_b_ff1), b_ff2=(b_ff2, m_b_ff2, v_b_ff2),
                 ln2_g=(ln2_g, m_ln2_g, v_ln2_g), ln2_b=(ln2_b, m_ln2_b, v_ln2_b))
    upd = _small_update(small_all, *([_row(given[n][i]) for n in SMALL_PARAMS] for i in range(3)))
    small_res = {n: [t.reshape(given[n][0].shape) for t in upd[4 * j:4 * j + 4]] for j, n in enumerate(SMALL_PARAMS)}
    g_conv_all, loss_lanes = upd[4 * len(SMALL_PARAMS)], upd[4 * len(SMALL_PARAMS) + 1]
    loss = jnp.sum(loss_lanes)

    dmod_mine = lax.dynamic_slice(small_all[:, :6 * D_MODEL], (0, chip * ADA_COLS), (N_DEV, ADA_COLS))
    big = {"ada_w": [t[None] for t in _ada_bwd(c_all, dmod_mine, ada_w[0], m_ada_w[0], v_ada_w[0])]}

    g_conv = lax.dynamic_slice(g_conv_all.reshape(CONV_K, CONV_DIM), (0, chip * 384), (CONV_K, 384))
    big["conv_w"] = [t[None] for t in _adamw(conv_w[0], g_conv, m_conv_w[0], v_conv_w[0], "adamw_conv_w")]

    g_rows = g_b[:W_COLS].reshape(W_COLS, 1, D_MODEL)
    big["w_in"] = [from_rows(t) for t in (g_rows, *_adamw_rows(as_rows(w_in), g_rows, as_rows(m_w_in), as_rows(v_w_in)))]
    for name, g, row0, (w, m, v) in (("w_out", g_c, 0, (w_out, m_w_out, v_w_out)), ("w_ff1", g_a, 0, (w_ff1, m_w_ff1, v_w_ff1)),
                                     ("w_ff2", g_a, D_MODEL, (w_ff2, m_w_ff2, v_w_ff2))):
        big[name] = [t[None] for t in _adamw(w[0], g, m[0], v[0], "adamw_" + name, row0)]

    order = ("ln_in_g", "ln_in_b", "ada_w", "ada_b", "w_in", "conv_w", "conv_b", "dt_bias", "a_log", "d_skip", "ssd_norm_w",
             "attn_sinks", "w_out", "ln1_g", "ln1_b", "w_ff1", "b_ff1", "w_ff2", "b_ff2", "ln2_g", "ln2_b")
    res = {**small_res, **big}
    return (loss, grad_x[None], *[res[n][k] for k in range(4) for n in order])
```

```python
import numpy as np
import jax
import jax.numpy as jnp
from jax import lax
from jax.experimental import pallas as pl
from jax.experimental.pallas import tpu as pltpu

f32 = jnp.float32
bf16 = jnp.bfloat16

D_MODEL = 1024
SSD_WIDTH = 1024
SSD_HEADS = 16
HEAD_DIM = 64
SSD_STATE = 128
SSD_GROUPS = 2
CHUNK = 128
CONV_K = 4
CONV_DIM = 1536
ATTN_HEADS = 16
D_FF = 4096
PROJ_WIDTH = 3856
ALPHA = 2.0 ** 0.25
LN_EPS = 1e-5
RMS_EPS = 1e-5
ATTN_SCALE = HEAD_DIM ** -0.5
NEG = -1e30

ADAM_LR = 0.001
ADAM_B1 = 0.9
ADAM_B2 = 0.999
ADAM_EPS = 1e-08
ADAM_WD = 0.01
ADAM_STEP = 10

W_Z, W_XBC, W_Q, W_KV = slice(0, 1024), slice(1024, 2560), slice(2576, 3600), slice(3600, 3856)
W_DT = slice(2560, 2688)
W_DT_ROWS = 16
GA_ROWS = 2048
GB_ROWS = 1024
GC_ROWS = 512
WG_TM = 512
STAGE_ROWS = 512
ADAMW_BLOCK_ELEMS = 1 << 18
MLP_TM = 256
DENSE_TM = 512
N_CHIPS = 4
N_DEV = 8
VMEM_LIMIT = 56 * 1024 * 1024
MESH = pl.DeviceIdType.MESH

ALIBI_SLOPES = tuple(2.0 ** (-8.0 / ATTN_HEADS * (i + 1)) for i in range(ATTN_HEADS))


def _cparams(*sem):
    return pltpu.CompilerParams(dimension_semantics=sem, vmem_limit_bytes=VMEM_LIMIT)


def _sigmoid(x):
    return 1.0 / (1.0 + jnp.exp(-x))


def _softplus(x):
    return jnp.maximum(x, 0.0) + jnp.log1p(jnp.exp(-jnp.abs(x)))


def _ln_stats(x):
    mu = jnp.mean(x, axis=-1, keepdims=True)
    xc = x - mu
    var = jnp.mean(xc * xc, axis=-1, keepdims=True)
    rstd = lax.rsqrt(var + LN_EPS)
    return xc * rstd, rstd


def _ln_bwd(dy, xhat, rstd, g):
    dxh = dy * g
    m1 = jnp.mean(dxh, axis=-1, keepdims=True)
    m2 = jnp.mean(dxh * xhat, axis=-1, keepdims=True)
    return rstd * (dxh - m1 - xhat * m2)


def _dot(a, b):
    return jnp.dot(a, b, preferred_element_type=f32)


def _dot_nt(a, b):
    return lax.dot_general(a, b, (((1,), (1,)), ((), ())), preferred_element_type=f32)


def _dot_tn(a, b):
    return lax.dot_general(a, b, (((0,), (0,)), ((), ())), preferred_element_type=f32)


def _dot_exact(a, b):
    return jnp.dot(a, b, preferred_element_type=f32, precision=lax.Precision.HIGHEST)


def _split3(v):
    hi = v.astype(bf16)
    r1 = v - hi.astype(f32)
    mid = r1.astype(bf16)
    lo = (r1 - mid.astype(f32)).astype(bf16)
    return hi, mid, lo


def _sel_dot(sel, v):
    hi, mid, lo = _split3(v)
    return _dot(sel, hi) + _dot(sel, mid) + _dot(sel, lo)


def _dot_sel_nt(v, sel):
    hi = v.astype(bf16)
    mid = (v - hi.astype(f32)).astype(bf16)
    return _dot_nt(hi, sel) + _dot_nt(mid, sel)


def _expand_heads(v):
    lane = lax.broadcasted_iota(jnp.int32, (v.shape[0], 128), 1)
    blocks = [jnp.where(lane < HEAD_DIM, v[:, 2 * b:2 * b + 1], v[:, 2 * b + 1:2 * b + 2]) for b in range(SSD_HEADS // 2)]
    return jnp.concatenate(blocks, axis=1)


def _full(shape):
    nd = len(shape)
    return pl.BlockSpec(shape, lambda *_: (0,) * nd)


def _resident(shape):
    nd = len(shape)
    return pl.BlockSpec(shape, lambda *_: (0,) * nd, pipeline_mode=pl.Buffered(1))


def _rows(tm, n):
    return pl.BlockSpec((tm, n), lambda i: (i, 0))


def _wall_rows(k):
    return pl.BlockSpec((N_CHIPS, D_MODEL, D_MODEL), lambda *_: (0, k, 0), pipeline_mode=pl.Buffered(1))


def _inproj_fwd(x, mod, ln_g, ln_b, w_in_t, conv_w, conv_b):
    L = x.shape[0]
    tm = DENSE_TM
    nt = L // tm

    def body(x_ref, mod_ref, g_ref, b_ref, w_ref, cw_ref, cb_ref,
             u1_ref, z_ref, xr_ref, xc_ref, q_ref, kv_ref, dt_ref, halo, buf):
        @pl.when(pl.program_id(0) == 0)
        def _():
            halo[...] = jnp.zeros_like(halo)

        xhat, _ = _ln_stats(x_ref[...])
        h0 = xhat * g_ref[...] + b_ref[...]
        u1 = (h0 * (1.0 + mod_ref[1:2, :]) + mod_ref[0:1, :]).astype(bf16)
        u1_ref[...] = u1
        z_ref[...] = _dot_nt(u1, w_ref[W_Z, :])
        xr = _dot_nt(u1, w_ref[W_XBC, :])
        xr_ref[...] = xr
        q_ref[...] = _dot_nt(u1, w_ref[W_Q, :]).astype(bf16)
        kv_ref[...] = _dot_nt(u1, w_ref[W_KV, :]).astype(bf16)
        dt_ref[...] = _dot_nt(u1, w_ref[W_DT, :])
        buf[0:8, :] = halo[...]
        buf[8:8 + tm, :] = xr
        pre = cb_ref[...] + cw_ref[0:1, :] * buf[5:5 + tm, :]
        for k in range(1, CONV_K):
            pre = pre + cw_ref[k:k + 1, :] * buf[5 + k:5 + k + tm, :]
        xc_ref[...] = pre * _sigmoid(pre)
        halo[...] = xr[tm - 8:tm, :]

    return pl.pallas_call(
        body, name="inproj_fwd", grid=(nt,),
        in_specs=[_rows(tm, D_MODEL), _full((8, D_MODEL)), _full((1, D_MODEL)), _full((1, D_MODEL)),
                  _resident((PROJ_WIDTH, D_MODEL)), _full((CONV_K, CONV_DIM)), _full((1, CONV_DIM))],
        out_specs=[_rows(tm, D_MODEL), _rows(tm, D_MODEL), _rows(tm, CONV_DIM), _rows(tm, CONV_DIM),
                   _rows(tm, D_MODEL), _rows(tm, 256), _rows(tm, 128)],
        out_shape=[jax.ShapeDtypeStruct((L, D_MODEL), bf16), jax.ShapeDtypeStruct((L, D_MODEL), f32),
                   jax.ShapeDtypeStruct((L, CONV_DIM), f32), jax.ShapeDtypeStruct((L, CONV_DIM), f32),
                   jax.ShapeDtypeStruct((L, D_MODEL), bf16), jax.ShapeDtypeStruct((L, 256), bf16),
                   jax.ShapeDtypeStruct((L, 128), f32)],
        scratch_shapes=[pltpu.VMEM((8, CONV_DIM), f32), pltpu.VMEM((tm + 8, CONV_DIM), f32)],
        compiler_params=_cparams("arbitrary"),
    )(x, mod, ln_g, ln_b, w_in_t, conv_w, conv_b)


def _head_expand():
    e = np.zeros((128, SSD_WIDTH), np.float32)
    for h in range(SSD_HEADS):
        e[h, h * HEAD_DIM:(h + 1) * HEAD_DIM] = 1.0
    return jnp.asarray(e, dtype=bf16)


def _ssd_chunk_common(dt_raw, dtb, a_row):
    T = CHUNK
    lane = lax.broadcasted_iota(jnp.int32, (T, 128), 1)
    dt = jnp.where(lane < SSD_HEADS, _softplus(dt_raw + dtb), 0.0)
    a = dt * a_row
    r = lax.broadcasted_iota(jnp.int32, (T, T), 0)
    c = lax.broadcasted_iota(jnp.int32, (T, T), 1)
    tril = (c <= r).astype(bf16)
    cum = _sel_dot(tril, a)
    dtx = _expand_heads(dt)
    cumx = _expand_heads(cum)
    return dt, a, cum, dtx, cumx, r, c


def _ssd_fwd(xc, dt_raw, dt_bias, a_log, d_skip_x, blob):
    L = xc.shape[0]
    nc = L // CHUNK
    T = CHUNK
    R = blob.shape[0]

    def body(xc_ref, dt_ref, dtb_ref, al_ref, dsk_ref, blob_ref, y_ref, prev_ref, wall_ref, st, send_sems, recv_sems):
        start, finish = _gather_job(blob_ref, wall_ref, send_sems, recv_sems, R)

        @pl.when(pl.program_id(0) == 0)
        def _():
            st[...] = jnp.zeros_like(st)
            start()

        @pl.when(pl.program_id(0) == nc - 1)
        def _():
            finish()

        a_row = -jnp.exp(al_ref[...])
        lane1 = lax.broadcasted_iota(jnp.int32, (1, 128), 1)
        a_row = jnp.where(lane1 < SSD_HEADS, a_row, 0.0)
        dt, a, cum, dtx, cumx, r, c = _ssd_chunk_common(dt_ref[...], dtb_ref[...], a_row)
        cum_t = cum.T
        ex = jnp.exp(cumx)
        last = cumx[T - 1:T, :]
        wx = jnp.exp(last - cumx)
        cdx = jnp.exp(last)
        xs = xc_ref[:, 0:SSD_WIDTH]
        X = xs * dtx
        Xb = X.astype(bf16)
        Xd = (X * wx).astype(bf16)
        prev = st[...]
        prev_ref[0] = prev
        prevb = prev.astype(bf16)
        tri = c <= r
        lane = lax.broadcasted_iota(jnp.int32, (T, 128), 1)
        y_blocks = []
        new_states = []
        for g in range(SSD_GROUPS):
            Bg = xc_ref[:, 1024 + 128 * g:1152 + 128 * g].astype(bf16)
            Cg = xc_ref[:, 1280 + 128 * g:1408 + 128 * g].astype(bf16)
            G = _dot_nt(Cg, Bg)
            yoff = _dot(Cg, prevb[:, 512 * g:512 * (g + 1)])
            new_states.append(_dot_tn(Bg, Xd[:, 512 * g:512 * (g + 1)]))
            for j in range(4):
                blk = 4 * g + j
                Xblk = Xb[:, 128 * blk:128 * (blk + 1)]
                ys = []
                for half in range(2):
                    h = 2 * blk + half
                    seg = jnp.minimum(cum[:, h:h + 1] - cum_t[h:h + 1, :], 0.0)
                    M = jnp.where(tri, G * jnp.exp(seg), 0.0).astype(bf16)
                    ys.append(_dot(M, Xblk))
                yd = jnp.where(lane < HEAD_DIM, ys[0], ys[1])
                sl = slice(128 * blk, 128 * (blk + 1))
                y_blocks.append(yd + ex[:, sl] * yoff[:, 128 * j:128 * (j + 1)] + dsk_ref[:, sl] * xs[:, sl])
        y_ref[...] = jnp.concatenate(y_blocks, axis=1)
        st[...] = prev * cdx + jnp.concatenate(new_states, axis=1)

    return pl.pallas_call(
        body, name="ssd_fwd", grid=(nc,),
        in_specs=[_rows(T, CONV_DIM), _rows(T, 128), _full((1, 128)), _full((1, 128)), _full((1, SSD_WIDTH)), _ANY_SPEC],
        out_specs=[_rows(T, SSD_WIDTH), pl.BlockSpec((1, SSD_STATE, SSD_WIDTH), lambda i: (i, 0, 0)), _ANY_SPEC],
        out_shape=[jax.ShapeDtypeStruct((L, SSD_WIDTH), f32), jax.ShapeDtypeStruct((nc, SSD_STATE, SSD_WIDTH), f32),
                   jax.ShapeDtypeStruct((N_CHIPS, R, D_MODEL), bf16)],
        scratch_shapes=[pltpu.VMEM((SSD_STATE, SSD_WIDTH), f32)] + _sems(6),
        compiler_params=_cparams("arbitrary"),
    )(xc, dt_raw, dt_bias, a_log, d_skip_x, blob)


def _kv_halves(kv_prev, kv_cur, first):
    kv = jnp.concatenate([jnp.where(first, 0.0, kv_prev.astype(f32)), kv_cur.astype(f32)], axis=0)
    lane = lax.broadcasted_iota(jnp.int32, (2 * CHUNK, 128), 1)
    lo = lane < HEAD_DIM
    out = []
    for g in range(2):
        per_half = []
        for half in range(2):
            both = []
            for t in (kv[:, 0:128], kv[:, 128:256]):
                src = t if g == half else pltpu.roll(t, HEAD_DIM, 1)
                both.append(jnp.where(lo if half == 0 else ~lo, src, 0.0).astype(bf16))
            per_half.append(tuple(both))
        out.append(per_half)
    return out


def _attn_masks(first):
    r = lax.broadcasted_iota(jnp.int32, (CHUNK, 2 * CHUNK), 0)
    c = lax.broadcasted_iota(jnp.int32, (CHUNK, 2 * CHUNK), 1)
    dist = r + CHUNK - c
    valid = (dist >= 0) & (dist < CHUNK) & ((c >= CHUNK) | jnp.logical_not(first))
    return dist.astype(f32), valid


def _head_stack(g, half, sink_ref):
    blks = [4 * g + i for i in range(4)]
    heads = [2 * b + half for b in blks]
    slope = jnp.concatenate([jnp.full((CHUNK, 1), ALIBI_SLOPES[h], f32) for h in heads], axis=0)
    sink = jnp.concatenate([jnp.full((CHUNK, 1), sink_ref[h], f32) for h in heads], axis=0)
    return blks, heads, slope, sink


def _attn_fwd(q, kv, sinks, blob):
    L = q.shape[0]
    nb = L // CHUNK
    T = CHUNK
    R = blob.shape[0]

    def body(sink_ref, q_ref, kvp_ref, kvc_ref, blob_ref, o_ref, lse_ref, wall_ref, send_sems, recv_sems):
        first = pl.program_id(0) == 0
        start, finish = _gather_job(blob_ref, wall_ref, send_sems, recv_sems, R)

        @pl.when(first)
        def _():
            start()

        @pl.when(pl.program_id(0) == nb - 1)
        def _():
            finish()

        ext = _kv_halves(kvp_ref[...], kvc_ref[...], first)
        dist, valid = _attn_masks(first)
        lane = lax.broadcasted_iota(jnp.int32, (T, 128), 1)
        lse = jnp.zeros((T, 128), f32)
        o_blocks = []
        for blk in range(8):
            qb = q_ref[:, 128 * blk:128 * (blk + 1)]
            acc = None
            for half in range(2):
                h = 2 * blk + half
                k_ext, v_ext = ext[h // 8][half]
                s = _dot_nt(qb, k_ext) * ATTN_SCALE - ALIBI_SLOPES[h] * dist
                s = jnp.where(valid, s, NEG)
                sink = sink_ref[h]
                m = jnp.maximum(jnp.max(s, axis=-1, keepdims=True), sink)
                p = jnp.exp(s - m)
                den = jnp.sum(p, axis=-1, keepdims=True) + jnp.exp(sink - m)
                pn = (p * (1.0 / den)).astype(bf16)
                oh = _dot(pn, v_ext)
                acc = oh if acc is None else acc + oh
                lse = jnp.where(lane == h, m + jnp.log(den), lse)
            o_blocks.append(acc.astype(bf16))
        o_ref[...] = jnp.concatenate(o_blocks, axis=1)
        lse_ref[...] = lse

    return pl.pallas_call(
        body, name="attn_fwd", grid=(nb,),
        in_specs=[pl.BlockSpec(memory_space=pltpu.SMEM), _rows(T, D_MODEL),
                  pl.BlockSpec((T, 256), lambda i: (jnp.maximum(i - 1, 0), 0)), _rows(T, 256), _ANY_SPEC],
        out_specs=[_rows(T, D_MODEL), _rows(T, 128), _ANY_SPEC],
        out_shape=[jax.ShapeDtypeStruct((L, D_MODEL), bf16), jax.ShapeDtypeStruct((L, 128), f32),
                   jax.ShapeDtypeStruct((N_CHIPS, R, D_MODEL), bf16)],
        scratch_shapes=_sems(6),
        compiler_params=_cparams("arbitrary"),
    )(sinks, q, kv, kv, blob)


def _gated_norm(y, z, w):
    sz = _sigmoid(z)
    hg = y * (z * sz)
    ns, rss = [], []
    for g in range(SSD_GROUPS):
        hs = hg[:, 512 * g:512 * (g + 1)]
        rs = lax.rsqrt(jnp.mean(hs * hs, axis=-1, keepdims=True) + RMS_EPS)
        ns.append(hs * rs)
        rss.append(rs)
    n = jnp.concatenate(ns, axis=1)
    return n * w, n, rss, sz


def _outproj_fwd(y, z, o, x, mod, ln_g, ln_b, norm_w, w_out):
    L = x.shape[0]
    tm = DENSE_TM

    def body(y_ref, z_ref, o_ref, x_ref, mod_ref, g_ref, b_ref, nw_ref, w_ref, yn_ref, mix_ref, r1_ref):
        yn, _, _, _ = _gated_norm(y_ref[...], z_ref[...], nw_ref[...])
        ynb = yn.astype(bf16)
        yn_ref[...] = ynb
        mix = (_dot(ynb[:, 0:512], w_ref[0]) + _dot(ynb[:, 512:1024], w_ref[1])
               + _dot(o_ref[:, 0:512], w_ref[2]) + _dot(o_ref[:, 512:1024], w_ref[3]))
        mix_ref[...] = mix
        xhat, _ = _ln_stats(x_ref[...])
        h0 = xhat * g_ref[...] + b_ref[...]
        r1_ref[...] = ALPHA * h0 + (1.0 + mod_ref[2:3, :]) * mix

    v = _full((1, D_MODEL))
    return pl.pallas_call(
        body, name="outproj_fwd", grid=(L // tm,),
        in_specs=[_rows(tm, D_MODEL), _rows(tm, D_MODEL), _rows(tm, D_MODEL), _rows(tm, D_MODEL),
                  _full((8, D_MODEL)), v, v, v, _resident((N_CHIPS, 512, D_MODEL))],
        out_specs=[_rows(tm, D_MODEL)] * 3,
        out_shape=[jax.ShapeDtypeStruct((L, D_MODEL), bf16), jax.ShapeDtypeStruct((L, D_MODEL), f32),
                   jax.ShapeDtypeStruct((L, D_MODEL), f32)],
        compiler_params=_cparams("parallel"),
    )(y, z, o, x, mod, ln_g, ln_b, norm_w, w_out)


A_LN2G, A_LN2B, A_G2, A_B2, A_SC2, A_SH2, A_LN1G, A_LN1B, A_LOSS = range(9)


def _mlp_fwd_bwd(r1, target, mod, ln1_g, ln1_b, ln2_g, ln2_b, w1, b1, w2, b2):
    L = r1.shape[0]
    tm = MLP_TM
    nj = D_FF // 1024

    def body(r1_ref, t_ref, mod_ref, g1_ref, bb1_ref, g2_ref, bb2_ref, w1_ref, b1_ref, w2_ref, b2_ref,
             dr1_ref, u2_ref, s_ref, da_ref, df_ref, acc_ref, db1_ref, hr):
        @pl.when(pl.program_id(0) == 0)
        def _():
            acc_ref[...] = jnp.zeros_like(acc_ref)
            db1_ref[...] = jnp.zeros_like(db1_ref)

        sc2, sh2, gate2 = mod_ref[4:5, :], mod_ref[3:4, :], mod_ref[5:6, :]
        xhat1, rstd1 = _ln_stats(r1_ref[...])
        h1 = xhat1 * g1_ref[...] + bb1_ref[...]
        u2f = h1 * (1.0 + sc2) + sh2
        u2 = u2f.astype(bf16)
        u2_ref[...] = u2
        f = jnp.zeros((tm, D_MODEL), f32) + b2_ref[...]
        for j in range(nj):
            cs = slice(1024 * j, 1024 * (j + 1))
            a = _dot(u2, w1_ref[j]) + b1_ref[:, cs]
            hrj = jnp.maximum(a, 0.0)
            hr[:, cs] = hrj.astype(bf16)
            sj = (hrj * hrj).astype(bf16)
            s_ref[:, cs] = sj
            f = f + _dot(sj, w2_ref[j])
        r2 = ALPHA * h1 + (1.0 + gate2) * f
        xhat2, rstd2 = _ln_stats(r2)
        h2 = xhat2 * g2_ref[...] + bb2_ref[...]
        diff = h2 - t_ref[...]
        dh2 = diff * (1.0 / D_MODEL)

        def add(row, val):
            acc_ref[row:row + 1, :] += jnp.sum(val, axis=0, keepdims=True)

        add(A_LOSS, diff * diff * (0.5 / D_MODEL))
        add(A_LN2G, dh2 * xhat2)
        add(A_LN2B, dh2)
        dr2 = _ln_bwd(dh2, xhat2, rstd2, g2_ref[...])
        add(A_G2, dr2 * f)
        df = dr2 * (1.0 + gate2)
        add(A_B2, df)
        dfb = df.astype(bf16)
        df_ref[...] = dfb
        du2 = jnp.zeros((tm, D_MODEL), f32)
        for j in range(nj):
            cs = slice(1024 * j, 1024 * (j + 1))
            ds = _dot_nt(dfb, w2_ref[j])
            daj = ds * (2.0 * hr[:, cs].astype(f32))
            db1_ref[:, cs] += jnp.sum(daj, axis=0, keepdims=True)
            dajb = daj.astype(bf16)
            da_ref[:, cs] = dajb
            du2 = du2 + _dot_nt(dajb, w1_ref[j])
        add(A_SC2, du2 * h1)
        add(A_SH2, du2)
        dh1 = ALPHA * dr2 + du2 * (1.0 + sc2)
        add(A_LN1G, dh1 * xhat1)
        add(A_LN1B, dh1)
        dr1_ref[...] = _ln_bwd(dh1, xhat1, rstd1, g1_ref[...])

    v = _full((1, D_MODEL))
    return pl.pallas_call(
        body, name="mlp_fwd_bwd", grid=(L // tm,),
        in_specs=[_rows(tm, D_MODEL), _rows(tm, D_MODEL), _full((8, D_MODEL)), v, v, v, v,
                  _wall_rows(0), _full((1, D_FF)), _wall_rows(1), v],
        out_specs=[_rows(tm, D_MODEL), _rows(tm, D_MODEL), _rows(tm, D_FF), _rows(tm, D_FF), _rows(tm, D_MODEL),
                   _full((16, D_MODEL)), _full((1, D_FF))],
        out_shape=[jax.ShapeDtypeStruct((L, D_MODEL), f32), jax.ShapeDtypeStruct((L, D_MODEL), bf16),
                   jax.ShapeDtypeStruct((L, D_FF), bf16), jax.ShapeDtypeStruct((L, D_FF), bf16),
                   jax.ShapeDtypeStruct((L, D_MODEL), bf16), jax.ShapeDtypeStruct((16, D_MODEL), f32),
                   jax.ShapeDtypeStruct((1, D_FF), f32)],
        scratch_shapes=[pltpu.VMEM((tm, D_FF), bf16)],
        compiler_params=_cparams("arbitrary"),
    )(r1, target, mod, ln1_g, ln1_b, ln2_g, ln2_b, w1, b1, w2, b2)


def _wgrad(a, b, name):
    L, M = a.shape
    N = b.shape[1]
    tm = min(M, 512)
    tn = next(t for t in (1024, 768, 512, 256, 128) if N % t == 0)

    def body(a_ref, b_ref, o_ref):
        o_ref[...] = _dot_tn(a_ref[...], b_ref[...]).astype(bf16)

    return pl.pallas_call(
        body, name=name, grid=(M // tm, N // tn),
        in_specs=[pl.BlockSpec((L, tm), lambda i, j: (0, i)), pl.BlockSpec((L, tn), lambda i, j: (0, j))],
        out_specs=pl.BlockSpec((tm, tn), lambda i, j: (i, j)),
        out_shape=jax.ShapeDtypeStruct((M, N), bf16),
        compiler_params=_cparams("parallel", "parallel"),
    )(a, b)


def _wgrad_blob(blob, a, b, name, place_of):
    L, M = a.shape
    N = b.shape[1]

    def body(blob_ref, a_ref, b_ref, o_ref):
        o_ref[0] = _dot_tn(a_ref[...], b_ref[...]).astype(bf16)

    return pl.pallas_call(
        body, name=name, grid=(M // WG_TM, N // D_MODEL),
        in_specs=[pl.BlockSpec(memory_space=pl.ANY), pl.BlockSpec((L, WG_TM), lambda t, n: (0, t)),
                  pl.BlockSpec((L, D_MODEL), lambda t, n: (0, n))],
        out_specs=pl.BlockSpec((1, WG_TM, D_MODEL), lambda t, n: (*place_of(t, n), 0)),
        out_shape=jax.ShapeDtypeStruct(blob.shape, bf16), input_output_aliases={0: 0},
        compiler_params=_cparams("parallel", "parallel"),
    )(blob, a, b)


def _outproj_bwd(dr1, mix, y, z, mod, norm_w, w_out, gb):
    L = dr1.shape[0]
    tm = DENSE_TM
    nt = L // tm

    def body(dr1_ref, mix_ref, y_ref, z_ref, mod_ref, nw_ref, w_ref, gb_ref,
             dy_ref, dz_ref, do_ref, dmix_ref, acc_ref, sib_ref, send_sems, recv_sems):
        start, wait = _to_sibling_job(gb_ref, sib_ref, send_sems, recv_sems)

        @pl.when(pl.program_id(0) == 0)
        def _():
            acc_ref[...] = jnp.zeros_like(acc_ref)
            start()

        @pl.when(pl.program_id(0) == nt - 1)
        def _():
            wait()

        dr1 = dr1_ref[...]
        acc_ref[0:1, :] += jnp.sum(dr1 * mix_ref[...], axis=0, keepdims=True)
        dmix = (dr1 * (1.0 + mod_ref[2:3, :])).astype(bf16)
        dmix_ref[...] = dmix
        dyn = jnp.concatenate([_dot_nt(dmix, w_ref[0]), _dot_nt(dmix, w_ref[1])], axis=1)
        do_ref[...] = jnp.concatenate([_dot_nt(dmix, w_ref[2]), _dot_nt(dmix, w_ref[3])], axis=1).astype(bf16)
        yv, zv = y_ref[...], z_ref[...]
        _, n, rss, sz = _gated_norm(yv, zv, nw_ref[...])
        acc_ref[1:2, :] += jnp.sum(dyn * n, axis=0, keepdims=True)
        dn = dyn * nw_ref[...]
        parts = []
        for g in range(SSD_GROUPS):
            sl = slice(512 * g, 512 * (g + 1))
            dng, ng = dn[:, sl], n[:, sl]
            parts.append(rss[g] * (dng - ng * jnp.mean(dng * ng, axis=-1, keepdims=True)))
        dhg = jnp.concatenate(parts, axis=1)
        dy_ref[...] = dhg * (zv * sz)
        dz_ref[...] = (dhg * yv * (sz * (1.0 + zv * (1.0 - sz)))).astype(bf16)

    return pl.pallas_call(
        body, name="outproj_bwd", grid=(nt,),
        in_specs=[_rows(tm, D_MODEL)] * 4 + [_full((8, D_MODEL)), _full((1, D_MODEL)), _resident((N_CHIPS, 512, D_MODEL)),
                  _ANY_SPEC],
        out_specs=[_rows(tm, D_MODEL)] * 4 + [_full((8, D_MODEL)), _ANY_SPEC],
        out_shape=[jax.ShapeDtypeStruct((L, D_MODEL), f32)] + [jax.ShapeDtypeStruct((L, D_MODEL), bf16)] * 3
        + [jax.ShapeDtypeStruct((8, D_MODEL), f32), jax.ShapeDtypeStruct((N_CHIPS,) + gb.shape[2:], bf16)],
        scratch_shapes=_sems(N_CHIPS),
        compiler_params=_cparams("arbitrary"),
    )(dr1, mix, y, z, mod, norm_w, w_out, gb)


def _attn_bwd(q, kv, do, lse, sinks, pb, gb2):
    L = q.shape[0]
    nb = L // CHUNK
    T = CHUNK

    def body(sink_ref, q_ref, kvp_ref, kvc_ref, do_ref, lse_ref, pb_ref, gb2_ref, dq_ref, dkv_ref, dsink_ref, chips_ref,
             sib2_ref, carry, send_sems, recv_sems, send_sems2, recv_sems2):
        n = pl.program_id(0)
        start, wait = _to_chips_job(pb_ref, chips_ref, send_sems, recv_sems)
        start2, wait2 = _to_sibling_job(gb2_ref, sib2_ref, send_sems2, recv_sems2)

        @pl.when(n == 0)
        def _():
            carry[...] = jnp.zeros_like(carry)
            dsink_ref[...] = jnp.zeros_like(dsink_ref)
            start()
            start2()

        @pl.when(n < nb)
        def _():
            first = n == 0
            ext = _kv_halves(kvp_ref[...], kvc_ref[...], first)
            dist, valid = _attn_masks(first)
            dist4, valid4 = jnp.concatenate([dist] * 4, axis=0), jnp.concatenate([valid] * 4, axis=0)
            lane1 = lax.broadcasted_iota(jnp.int32, (1, 128), 1)
            lse = lse_ref[...]
            qts = [q_ref[:, 128 * b:128 * (b + 1)].astype(f32).T.astype(bf16) for b in range(8)]
            dots = [do_ref[:, 128 * b:128 * (b + 1)].astype(f32).T.astype(bf16) for b in range(8)]
            acck = [None, None]
            accv = [None, None]
            dsink = jnp.zeros((1, 128), f32)
            dq_acc = [None] * 8
            for g in range(2):
                for half in range(2):
                    k_ext, v_ext = ext[g][half]
                    blks, heads, slope, sink = _head_stack(g, half, sink_ref)
                    qs = jnp.concatenate([q_ref[:, 128 * b:128 * (b + 1)] for b in blks], axis=0)
                    dos = jnp.concatenate([do_ref[:, 128 * b:128 * (b + 1)] for b in blks], axis=0)
                    rows = slice(HEAD_DIM * half, HEAD_DIM * (half + 1))
                    qt = jnp.concatenate([qts[b][rows, :] for b in blks], axis=1)
                    dot_ = jnp.concatenate([dots[b][rows, :] for b in blks], axis=1)
                    lse_col = jnp.concatenate([lse[:, h:h + 1] for h in heads], axis=0)
                    s = _dot_nt(qs, k_ext) * ATTN_SCALE - slope * dist4
                    p = jnp.where(valid4, jnp.exp(s - lse_col), 0.0)
                    dp = _dot_nt(dos, v_ext)
                    delta = jnp.sum(p * dp, axis=-1, keepdims=True)
                    ds = (p * (dp - delta) * ATTN_SCALE).astype(bf16)
                    sd = jnp.exp(sink - lse_col) * delta
                    dqs = _dot(ds, k_ext)
                    for i, b in enumerate(blks):
                        seg = slice(T * i, T * (i + 1))
                        dq_acc[b] = dqs[seg, :] if dq_acc[b] is None else dq_acc[b] + dqs[seg, :]
                        dsink = dsink - jnp.where(lane1 == heads[i], jnp.sum(sd[seg, :], axis=0, keepdims=True), 0.0)
                    dk = _dot(qt, ds)
                    dv = _dot(dot_, p.astype(bf16))
                    acck[g] = dk if acck[g] is None else acck[g] + dk
                    accv[g] = dv if accv[g] is None else accv[g] + dv
            dq_ref[...] = jnp.concatenate([a.astype(bf16) for a in dq_acc], axis=1)
            dsink_ref[...] += dsink
            dkv = jnp.concatenate([jnp.concatenate(acck, axis=0).T, jnp.concatenate(accv, axis=0).T], axis=1)
            dkv_ref[...] = (carry[...] + dkv[0:T, :]).astype(bf16)
            carry[...] = dkv[T:2 * T, :]

        @pl.when(n == nb)
        def _():
            dkv_ref[...] = carry[...].astype(bf16)
            wait()
            wait2()

    cur = lambda i: (jnp.minimum(i, nb - 1), 0)
    return pl.pallas_call(
        body, name="attn_bwd", grid=(nb + 1,),
        in_specs=[pl.BlockSpec(memory_space=pltpu.SMEM), pl.BlockSpec((T, D_MODEL), cur),
                  pl.BlockSpec((T, 256), lambda i: (jnp.maximum(jnp.minimum(i, nb - 1) - 1, 0), 0)),
                  pl.BlockSpec((T, 256), cur), pl.BlockSpec((T, D_MODEL), cur), pl.BlockSpec((T, 128), cur), _ANY_SPEC,
                  _ANY_SPEC],
        out_specs=[pl.BlockSpec((T, D_MODEL), cur), pl.BlockSpec((T, 256), lambda i: (jnp.maximum(i - 1, 0), 0)),
                   _full((1, 128)), _ANY_SPEC, _ANY_SPEC],
        out_shape=[jax.ShapeDtypeStruct((L, D_MODEL), bf16), jax.ShapeDtypeStruct((L, 256), bf16),
                   jax.ShapeDtypeStruct((1, 128), f32), jax.ShapeDtypeStruct((N_CHIPS - 1,) + pb.shape[1:], bf16),
                   jax.ShapeDtypeStruct((N_CHIPS,) + gb2.shape[2:], bf16)],
        scratch_shapes=[pltpu.VMEM((T, 256), f32)] + _sems(N_CHIPS - 1) + _sems(N_CHIPS),
        compiler_params=_cparams("arbitrary"),
    )(sinks, q, kv, kv, do, lse, pb, gb2)


def _ssd_bwd(xc, dt_raw, dy, prev_all, dt_bias, a_log, d_skip_x, e_mat, g, pb2):
    L = xc.shape[0]
    nc = L // CHUNK
    T = CHUNK
    RG = g.shape[0]

    def body(xc_ref, dt_ref, dy_ref, prev_ref, dtb_ref, al_ref, dsk_ref, e_ref, g_in_ref, pb2_ref,
             dxc_ref, ddt_ref, acc_ref, dd_ref, g_ref, chips2_ref, dst, dxs_s, send_sems, recv_sems, send_sems2, recv_sems2):
        start, wait = _share_job(g_ref, send_sems, recv_sems, RG)
        start2, wait2 = _to_chips_job(pb2_ref, chips2_ref, send_sems2, recv_sems2)

        @pl.when(pl.program_id(0) == 0)
        def _():
            dst[...] = jnp.zeros_like(dst)
            acc_ref[...] = jnp.zeros_like(acc_ref)
            dd_ref[...] = jnp.zeros_like(dd_ref)
            start()
            start2()

        @pl.when(pl.program_id(0) == nc - 1)
        def _():
            wait()
            wait2()

        lane1 = lax.broadcasted_iota(jnp.int32, (1, 128), 1)
        a_row = jnp.where(lane1 < SSD_HEADS, -jnp.exp(al_ref[...]), 0.0)
        e_mat_v = e_ref[...]
        dt, a, cum, dtx, cumx, r, c = _ssd_chunk_common(dt_ref[...], dtb_ref[...], a_row)
        cum_t = cum.T
        ex = jnp.exp(cumx)
        last = cumx[T - 1:T, :]
        wx = jnp.exp(last - cumx)
        cdx = jnp.exp(last)
        xs = xc_ref[:, 0:SSD_WIDTH]
        X = xs * dtx
        Xb = X.astype(bf16)
        Xdb = (X * wx).astype(bf16)
        dyv = dy_ref[...]
        prev = prev_ref[0]
        prevb = prev.astype(bf16)
        dnew = dst[...]
        dnewb = dnew.astype(bf16)
        tri = c <= r
        lane = lax.broadcasted_iota(jnp.int32, (T, 128), 1)
        sub = lax.broadcasted_iota(jnp.int32, (128, T), 0)
        lo = lane < HEAD_DIM

        def red(vals, g):
            return _dot_sel_nt(vals, e_mat_v[:, 512 * g:512 * (g + 1)])

        de = jnp.zeros((T, 128), f32)
        dw = jnp.zeros((T, 128), f32)
        ddt_x = jnp.zeros((T, 128), f32)
        dcum_col = jnp.zeros((T, 128), f32)
        dcum_row = jnp.zeros((128, T), f32)
        dprev_parts, dBs, dCs = [], [], []
        for g in range(SSD_GROUPS):
            s5 = slice(512 * g, 512 * (g + 1))
            Bg = xc_ref[:, 1024 + 128 * g:1152 + 128 * g].astype(bf16)
            Cg = xc_ref[:, 1280 + 128 * g:1408 + 128 * g].astype(bf16)
            G = _dot_nt(Cg, Bg)
            Z = _dot(Cg, prevb[:, s5])
            dyg = dyv[:, s5]
            dZb = (dyg * ex[:, s5]).astype(bf16)
            dXd = _dot(Bg, dnewb[:, s5])
            dC = _dot_nt(dZb, prevb[:, s5])
            dB = _dot_nt(Xdb[:, s5], dnewb[:, s5])
            dprev_parts.append(_dot_tn(Cg, dZb) + dnew[:, s5] * cdx[:, s5])
            de = de + red(dyg * Z, g)
            dw = dw + red(dXd * X[:, s5], g)
            dXg = dXd * wx[:, s5]
            dG = jnp.zeros((T, T), f32)
            for j in range(4):
                blk = 4 * g + j
                sl = slice(128 * blk, 128 * (blk + 1))
                Xblk = Xb[:, sl]
                dyblk = dyv[:, sl]
                dyblk_b = dyblk.astype(bf16)
                dxh = []
                for half in range(2):
                    h = 2 * blk + half
                    seg = jnp.minimum(cum[:, h:h + 1] - cum_t[h:h + 1, :], 0.0)
                    Lm = jnp.where(tri, jnp.exp(seg), 0.0)
                    M = G * Lm
                    dyh = jnp.where(lo if half == 0 else ~lo, dyblk, 0.0).astype(bf16)
                    dM = _dot_nt(dyh, Xblk)
                    dG = dG + dM * Lm
                    Q = dM * M
                    dcum_col = dcum_col + jnp.where(lane == h, jnp.sum(Q, axis=1, keepdims=True), 0.0)
                    dcum_row = dcum_row + jnp.where(sub == h, jnp.sum(Q, axis=0, keepdims=True), 0.0)
                    dxh.append(_dot_tn(M.astype(bf16), dyblk_b))
                dXblk = dXg[:, 128 * j:128 * (j + 1)] + jnp.where(lo, dxh[0], dxh[1])
                xsb = xs[:, sl]
                dxs_s[:, sl] = dXblk * dtx[:, sl] + dsk_ref[:, sl] * dyblk
                ddt_x = ddt_x + _dot_sel_nt(dXblk * xsb, e_mat_v[:, sl])
                dd_ref[:, sl] += jnp.sum(dyblk * xsb, axis=0, keepdims=True)
            dGb = dG.astype(bf16)
            dCs.append(dC + _dot(dGb, Bg))
            dBs.append(dB + _dot_tn(dGb, Cg))
        e16 = jnp.exp(cum)
        cum_last = cum[T - 1:T, :]
        w16 = jnp.exp(cum_last - cum)
        dcd = jnp.sum(dnew * prev, axis=0, keepdims=True)
        dcd16 = red(dcd[:, 0:512], 0) + red(dcd[:, 512:1024], 1)
        dww = dw * w16
        extra = jnp.sum(dww, axis=0, keepdims=True) + dcd16 * jnp.exp(cum_last)
        rowi = lax.broadcasted_iota(jnp.int32, (T, 128), 0)
        dcum = dcum_col - dcum_row.T + de * e16 - dww + jnp.where(rowi == T - 1, extra, 0.0)
        da = _sel_dot((c >= r).astype(bf16), dcum)
        ddt = ddt_x + da * a_row
        acc_ref[0:1, :] += jnp.sum(da * dt, axis=0, keepdims=True)
        ddt_raw = jnp.where(lane < SSD_HEADS, ddt * _sigmoid(dt_ref[...] + dtb_ref[...]), 0.0)
        ddt_ref[...] = ddt_raw
        acc_ref[1:2, :] += jnp.sum(ddt_raw, axis=0, keepdims=True)
        dxc_ref[:, 0:SSD_WIDTH] = dxs_s[...]
        dxc_ref[:, 1024:1280] = jnp.concatenate(dBs, axis=1)
        dxc_ref[:, 1280:1536] = jnp.concatenate(dCs, axis=1)
        dst[...] = jnp.concatenate(dprev_parts, axis=1)

    rev = lambda i: (nc - 1 - i, 0)
    return pl.pallas_call(
        body, name="ssd_bwd", grid=(nc,),
        in_specs=[pl.BlockSpec((T, CONV_DIM), rev), pl.BlockSpec((T, 128), rev), pl.BlockSpec((T, SSD_WIDTH), rev),
                  pl.BlockSpec((1, SSD_STATE, SSD_WIDTH), lambda i: (nc - 1 - i, 0, 0)),
                  _full((1, 128)), _full((1, 128)), _full((1, SSD_WIDTH)), _full((128, SSD_WIDTH)), _ANY_SPEC, _ANY_SPEC],
        out_specs=[pl.BlockSpec((T, CONV_DIM), rev), pl.BlockSpec((T, 128), rev), _full((8, 128)),
                   _full((1, SSD_WIDTH)), _ANY_SPEC, _ANY_SPEC],
        out_shape=[jax.ShapeDtypeStruct((L, CONV_DIM), f32), jax.ShapeDtypeStruct((L, 128), f32),
                   jax.ShapeDtypeStruct((8, 128), f32), jax.ShapeDtypeStruct((1, SSD_WIDTH), f32),
                   jax.ShapeDtypeStruct(g.shape, f32), jax.ShapeDtypeStruct((N_CHIPS - 1,) + pb2.shape[1:], bf16)],
        input_output_aliases={8: 4},
        scratch_shapes=[pltpu.VMEM((SSD_STATE, SSD_WIDTH), f32), pltpu.VMEM((T, SSD_WIDTH), f32)] + _sems(1)
        + _sems(N_CHIPS - 1),
        compiler_params=_cparams("arbitrary"),
    )(xc, dt_raw, dy, prev_all, dt_bias, a_log, d_skip_x, e_mat, g, pb2)


def _conv_bwd(dxc, xr, conv_w, conv_b, g):
    L = dxc.shape[0]
    tm = 256
    nt = L // tm
    RG = g.shape[0]

    def body(dxc_ref, xr_ref, xh_ref, cw_ref, cb_ref, g_in_ref, dxr_ref, acc_ref, g_ref, carry, buf, ext, send_sems, recv_sems):
        i = pl.program_id(0)
        start, wait = _share_job(g_ref, send_sems, recv_sems, RG)

        @pl.when(i == 0)
        def _():
            carry[...] = jnp.zeros_like(carry)
            acc_ref[...] = jnp.zeros_like(acc_ref)
            ext[tm + 16:tm + CHUNK, :] = jnp.zeros((CHUNK - 16, CONV_DIM), bf16)
            start()

        @pl.when(i == nt - 1)
        def _():
            wait()

        buf[0:8, :] = jnp.where(i == nt - 1, 0.0, xh_ref[...])
        u = xr_ref[...]
        buf[8:8 + tm, :] = u
        pre = cb_ref[...] + cw_ref[CONV_K - 1:CONV_K, :] * u
        for k in range(CONV_K - 1):
            pre = pre + cw_ref[k:k + 1, :] * buf[5 + k:5 + k + tm, :]
        sg = _sigmoid(pre)
        dpre = dxc_ref[...] * (sg * (1.0 + pre * (1.0 - sg)))
        acc_ref[4:5, :] += jnp.sum(dpre, axis=0, keepdims=True)
        dpb = dpre.astype(bf16)
        ext[0:tm, :] = dpb
        ext[tm:tm + 16, :] = carry[...]
        acc_ref[CONV_K - 1:CONV_K, :] += jnp.sum(u * dpre, axis=0, keepdims=True)
        du = cw_ref[CONV_K - 1:CONV_K, :] * dpre
        r = lax.broadcasted_iota(jnp.int32, (CHUNK, 2 * CHUNK), 0)
        c = lax.broadcasted_iota(jnp.int32, (CHUNK, 2 * CHUNK), 1)
        for j in range(1, CONV_K):
            move = (c == r + j).astype(bf16)
            up = jnp.concatenate([_dot(move, ext[CHUNK * b:CHUNK * (b + 2), :]) for b in range(tm // CHUNK)], axis=0)
            k = CONV_K - 1 - j
            du = du + cw_ref[k:k + 1, :] * up
            acc_ref[k:k + 1, :] += jnp.sum(u * up, axis=0, keepdims=True)
        dxr_ref[...] = du.astype(bf16)
        carry[...] = dpb[0:16, :]

    rev = lambda i: (nt - 1 - i, 0)
    return pl.pallas_call(
        body, name="conv_bwd", grid=(nt,),
        in_specs=[pl.BlockSpec((tm, CONV_DIM), rev), pl.BlockSpec((tm, CONV_DIM), rev),
                  pl.BlockSpec((8, CONV_DIM), lambda i: (jnp.maximum((nt - 1 - i) * (tm // 8) - 1, 0), 0)),
                  _full((CONV_K, CONV_DIM)), _full((1, CONV_DIM)), _ANY_SPEC],
        out_specs=[pl.BlockSpec((tm, CONV_DIM), rev), _full((8, CONV_DIM)), _ANY_SPEC],
        out_shape=[jax.ShapeDtypeStruct((L, CONV_DIM), bf16), jax.ShapeDtypeStruct((8, CONV_DIM), f32),
                   jax.ShapeDtypeStruct(g.shape, f32)],
        input_output_aliases={5: 2},
        scratch_shapes=[pltpu.VMEM((16, CONV_DIM), bf16), pltpu.VMEM((tm + 8, CONV_DIM), f32),
                        pltpu.VMEM((tm + CHUNK, CONV_DIM), bf16)] + _sems(1),
        compiler_params=_cparams("arbitrary"),
    )(dxc, xr, xr, conv_w, conv_b, g)


def _inproj_bwd(dz, dxr, dq, dkv, ddt, dr1, x, mod, ln_g, ln_b, w_in, pb):
    L = x.shape[0]
    tm = DENSE_TM
    nt = L // tm

    def body(dz_ref, dxr_ref, dq_ref, dkv_ref, ddt_ref, dr1_ref, x_ref, mod_ref, g_ref, b_ref, w_ref, pb_ref,
             dx_ref, acc_ref, chips_ref, send_sems, recv_sems):
        start, wait = _to_chips_job(pb_ref, chips_ref, send_sems, recv_sems)

        @pl.when(pl.program_id(0) == 0)
        def _():
            acc_ref[...] = jnp.zeros_like(acc_ref)
            start()

        @pl.when(pl.program_id(0) == nt - 1)
        def _():
            wait()

        du1 = (_dot(dz_ref[...], w_ref[W_Z, :]) + _dot(dxr_ref[...], w_ref[W_XBC, :])
               + _dot(dq_ref[...], w_ref[W_Q, :]) + _dot(dkv_ref[...], w_ref[W_KV, :])
               + _dot(ddt_ref[...].astype(bf16), w_ref[W_DT, :]))
        xhat, rstd = _ln_stats(x_ref[...])
        h0 = xhat * g_ref[...] + b_ref[...]
        acc_ref[0:1, :] += jnp.sum(du1 * h0, axis=0, keepdims=True)
        acc_ref[1:2, :] += jnp.sum(du1, axis=0, keepdims=True)
        dh0 = du1 * (1.0 + mod_ref[1:2, :]) + ALPHA * dr1_ref[...]
        acc_ref[2:3, :] += jnp.sum(dh0 * xhat, axis=0, keepdims=True)
        acc_ref[3:4, :] += jnp.sum(dh0, axis=0, keepdims=True)
        dx_ref[...] = _ln_bwd(dh0, xhat, rstd, g_ref[...])

    v = _full((1, D_MODEL))
    return pl.pallas_call(
        body, name="inproj_bwd", grid=(nt,),
        in_specs=[_rows(tm, D_MODEL), _rows(tm, CONV_DIM), _rows(tm, D_MODEL), _rows(tm, 256), _rows(tm, 128),
                  _rows(tm, D_MODEL), _rows(tm, D_MODEL), _full((8, D_MODEL)), v, v, _resident((PROJ_WIDTH, D_MODEL)),
                  _ANY_SPEC],
        out_specs=[_rows(tm, D_MODEL), _full((8, D_MODEL)), _ANY_SPEC],
        out_shape=[jax.ShapeDtypeStruct((L, D_MODEL), f32), jax.ShapeDtypeStruct((8, D_MODEL), f32),
                   jax.ShapeDtypeStruct((N_CHIPS - 1,) + pb.shape[1:], bf16)],
        scratch_shapes=_sems(N_CHIPS - 1),
        compiler_params=_cparams("arbitrary"),
    )(dz, dxr, dq, dkv, ddt, dr1, x, mod, ln_g, ln_b, w_in, pb)


def _adamw_math(w, g, m, v):
    m = ADAM_B1 * m + (1.0 - ADAM_B1) * g
    v = ADAM_B2 * v + (1.0 - ADAM_B2) * (g * g)
    m_hat = m / (1.0 - ADAM_B1 ** ADAM_STEP)
    v_hat = v / (1.0 - ADAM_B2 ** ADAM_STEP)
    delta = -ADAM_LR * (m_hat / (jnp.sqrt(v_hat) + ADAM_EPS) + ADAM_WD * w)
    return delta, m, v


def _adamw(w, g, m, v, name, g_row0=0):
    R, C = w.shape

    def body(w_ref, g_ref, m_ref, v_ref, g2_ref, d_ref, m2_ref, v2_ref):
        gv = g_ref[...]
        g2_ref[...] = gv
        d_ref[...], m2_ref[...], v2_ref[...] = _adamw_math(w_ref[...], gv, m_ref[...], v_ref[...])

    cap = max(8, ADAMW_BLOCK_ELEMS // C)
    tr = R if R <= cap else next(t for t in range(cap - cap % 8, 7, -8) if R % t == 0)
    spec = pl.BlockSpec((tr, C), lambda i: (i, 0))
    g_spec = pl.BlockSpec((tr, C), lambda i: (g_row0 // tr + i, 0))
    return pl.pallas_call(
        body, name=name, grid=(R // tr,), in_specs=[spec, g_spec, spec, spec], out_specs=[spec] * 4,
        out_shape=[jax.ShapeDtypeStruct((R, C), f32)] * 4, compiler_params=_cparams("parallel"),
    )(w, g, m, v)


def _adamw_rows(w, g, m, v):
    R, _, C = w.shape
    tr = R // 4

    def body(w_ref, g_ref, m_ref, v_ref, d_ref, m2_ref, v2_ref):
        d_ref[...], m2_ref[...], v2_ref[...] = _adamw_math(w_ref[...], g_ref[...], m_ref[...], v_ref[...])

    spec = pl.BlockSpec((tr, 1, C), lambda i: (i, 0, 0))
    return pl.pallas_call(
        body, name="adamw_w_in", grid=(R // tr,), in_specs=[spec] * 4, out_specs=[spec] * 3,
        out_shape=[jax.ShapeDtypeStruct((R, 1, C), f32)] * 3, compiler_params=_cparams("parallel"),
    )(w, g, m, v)


ADA_COLS = 6 * D_MODEL // N_CHIPS
ADA_TN = 512


COND_LANES = 512


def _prologue(cond, ada_w, ada_b, blob):
    R = blob.shape[0]

    def body(cond_ref, w_ref, b_ref, blob_ref, call_ref, mod_ref, wall_ref, mod_s, stage, gs, gr, ms, mr, ws, wr, local_sem):
        x, y, c = _place()
        start_w, finish_w = _gather_job(blob_ref, wall_ref, ws, wr, R)

        def rows(ref, px, py, pc):
            return ref.at[pl.ds(pl.multiple_of((4 * px + 2 * py + pc) * 8, 8), 8), :]

        mine = pltpu.make_async_copy(cond_ref, rows(call_ref, x, y, c), local_sem)
        mine.start()
        sends = [_remote(cond_ref, rows(call_ref, x, y, c), gs, gr, m - 1, _flip(x, y, c, m)) for m in range(1, N_DEV)]
        for cp in sends:
            cp.start()
        for m in range(1, N_DEV):
            peer = _flip(x, y, c, m)
            _remote(cond_ref, rows(call_ref, *peer), gs, gr, m - 1, peer).wait_recv()
        for cp in sends:
            cp.wait_send()
        mine.wait()
        start_w()

        for k in range(R // STAGE_ROWS):
            part = pl.ds(STAGE_ROWS * k, STAGE_ROWS)
            cin = pltpu.make_async_copy(blob_ref.at[part, :], stage, local_sem)
            cin.start()
            cin.wait()
            cout = pltpu.make_async_copy(stage, wall_ref.at[2 * x + y, part, :], local_sem)
            cout.start()
            cout.wait()

        lo = jnp.concatenate([call_ref[8 * d:8 * d + 1, :] for d in range(N_DEV)], axis=0)
        hi = jnp.concatenate([call_ref[8 * d + 1:8 * d + 2, :] for d in range(N_DEV)], axis=0)
        mod_all = (_dot_exact(lo * _sigmoid(lo), w_ref[0:COND_LANES, :]) + _dot_exact(hi * _sigmoid(hi), w_ref[COND_LANES:, :])
                   + b_ref[...])
        for d in range(N_DEV):
            mod_s[8 * d:8 * d + 8, :] = jnp.broadcast_to(mod_all[d:d + 1, :], (8, ADA_COLS))

        mine = pltpu.make_async_copy(rows(mod_s, x, y, c), mod_ref.at[2 * x + y], local_sem)
        mine.start()
        sends = []
        for m in range(1, N_CHIPS):
            peer = _flip(x, y, c, 2 * m)
            sends.append(_remote(rows(mod_s, *peer), mod_ref.at[2 * x + y], ms, mr, m - 1, peer))
        for cp in sends:
            cp.start()
        for m in range(1, N_CHIPS):
            px, py, pc = _flip(x, y, c, 2 * m)
            _remote(rows(mod_s, x, y, c), mod_ref.at[2 * px + py], ms, mr, m - 1, (px, py, pc)).wait_recv()
        for cp in sends:
            cp.wait_send()
        mine.wait()
        finish_w()

    return pl.pallas_call(
        body, name="prologue",
        out_shape=[jax.ShapeDtypeStruct((8 * N_DEV, COND_LANES), f32), jax.ShapeDtypeStruct((N_CHIPS, 8, ADA_COLS), f32),
                   jax.ShapeDtypeStruct((N_CHIPS, R, D_MODEL), bf16)],
        in_specs=[_VMEM_SPEC, _VMEM_SPEC, _VMEM_SPEC, _ANY_SPEC], out_specs=[_VMEM_SPEC, _VMEM_SPEC, _ANY_SPEC],
        scratch_shapes=[pltpu.VMEM((8 * N_DEV, ADA_COLS), f32), pltpu.VMEM((STAGE_ROWS, D_MODEL), bf16)]
        + _sems(N_DEV - 1) + _sems(N_CHIPS - 1) + _sems(6) + [pltpu.SemaphoreType.DMA],
        compiler_params=pltpu.CompilerParams(vmem_limit_bytes=VMEM_LIMIT),
    )(cond, ada_w, ada_b, blob)


def _ada_bwd(c_all, dmod, w, m, v):
    def body(c_ref, d_ref, w_ref, m_ref, v_ref, g_ref, dl_ref, m2_ref, v2_ref):
        cv = c_ref[...]
        g = lax.dot_general(cv * _sigmoid(cv), d_ref[...], (((0,), (0,)), ((), ())), preferred_element_type=f32,
                            precision=lax.Precision.HIGHEST)
        g_ref[...] = g
        dl_ref[...], m2_ref[...], v2_ref[...] = _adamw_math(w_ref[...], g, m_ref[...], v_ref[...])

    wspec = pl.BlockSpec((D_MODEL, ADA_TN), lambda j: (0, j))
    return pl.pallas_call(
        body, name="ada_bwd", grid=(ADA_COLS // ADA_TN,),
        in_specs=[_full((N_DEV, D_MODEL)), pl.BlockSpec((N_DEV, ADA_TN), lambda j: (0, j)), wspec, wspec, wspec],
        out_specs=[wspec] * 4, out_shape=[jax.ShapeDtypeStruct((D_MODEL, ADA_COLS), f32)] * 4,
        compiler_params=_cparams("parallel"),
    )(c_all, dmod, w, m, v)


SMALL_SLOTS = (("ada_b", 6144), ("ln_in_g", 1024), ("ln_in_b", 1024), ("conv_b", 1536), ("dt_bias", 128), ("a_log", 128),
               ("d_skip", 128), ("ssd_norm_w", 1024), ("attn_sinks", 128), ("ln1_g", 1024), ("ln1_b", 1024),
               ("b_ff1", 4096), ("b_ff2", 1024), ("ln2_g", 1024), ("ln2_b", 1024), ("conv_w", 6144), ("loss", 1024))
SMALL_N = sum(n for _, n in SMALL_SLOTS)
SMALL_OFF = {name: sum(n for _, n in SMALL_SLOTS[:i]) for i, (name, _) in enumerate(SMALL_SLOTS)}
SMALL_PARAMS = tuple(name for name, _ in SMALL_SLOTS[:15])
assert SMALL_N % 1024 == 0


def _small_pack(acc_in, acc_out, acc_mlp, db1, acc_conv, acc_ssd, dd_x, dsink, alog, e_mat):
    def body(in_ref, out_ref, mlp_ref, db1_ref, conv_ref, ssd_ref, dd_ref, sink_ref, al_ref, e_ref, o_ref):
        def put(name, val, at=0):
            off = SMALL_OFF[name] + at
            o_ref[:, off:off + val.shape[1]] = val

        for k, row in enumerate((in_ref[1:2, :], in_ref[0:1, :], out_ref[0:1, :], mlp_ref[A_SH2:A_SH2 + 1, :],
                                 mlp_ref[A_SC2:A_SC2 + 1, :], mlp_ref[A_G2:A_G2 + 1, :])):
            put("ada_b", row, D_MODEL * k)
        put("ln_in_g", in_ref[2:3, :])
        put("ln_in_b", in_ref[3:4, :])
        put("conv_b", conv_ref[4:5, :])
        put("dt_bias", ssd_ref[1:2, :])
        put("a_log", ssd_ref[0:1, :] * (-jnp.exp(al_ref[...])))
        put("d_skip", _dot_sel_nt(jnp.broadcast_to(dd_ref[...], (8, SSD_WIDTH)), e_ref[...])[0:1, :])
        put("ssd_norm_w", out_ref[1:2, :])
        put("attn_sinks", sink_ref[...])
        put("ln1_g", mlp_ref[A_LN1G:A_LN1G + 1, :])
        put("ln1_b", mlp_ref[A_LN1B:A_LN1B + 1, :])
        put("b_ff1", db1_ref[...])
        put("b_ff2", mlp_ref[A_B2:A_B2 + 1, :])
        put("ln2_g", mlp_ref[A_LN2G:A_LN2G + 1, :])
        put("ln2_b", mlp_ref[A_LN2B:A_LN2B + 1, :])
        for k in range(CONV_K):
            put("conv_w", conv_ref[k:k + 1, :], CONV_DIM * k)
        put("loss", mlp_ref[A_LOSS:A_LOSS + 1, :])

    return pl.pallas_call(body, name="small_pack", out_shape=jax.ShapeDtypeStruct((1, SMALL_N), f32),
                          compiler_params=_cparams())(acc_in, acc_out, acc_mlp, db1, acc_conv, acc_ssd, dd_x, dsink, alog, e_mat)


def _small_update(gathered, params, moms, vels):
    k = len(SMALL_PARAMS)

    def body(g_ref, *refs):
        w_refs, m_refs, v_refs, outs = refs[:k], refs[k:2 * k], refs[2 * k:3 * k], refs[3 * k:]

        def total(name, n):
            off = SMALL_OFF[name]
            g = g_ref[0:1, off:off + n]
            for i in range(1, N_DEV):
                g = g + g_ref[i:i + 1, off:off + n]
            return g

        for j, name in enumerate(SMALL_PARAMS):
            n = w_refs[j].shape[1]
            g = total(name, max(n, 128))[:, :n]
            outs[4 * j][...] = g
            outs[4 * j + 1][...], outs[4 * j + 2][...], outs[4 * j + 3][...] = _adamw_math(
                w_refs[j][...], g, m_refs[j][...], v_refs[j][...])
        outs[4 * k][...] = total("conv_w", CONV_K * CONV_DIM)
        outs[4 * k + 1][...] = total("loss", D_MODEL)

    shapes = [jax.ShapeDtypeStruct(p.shape, f32) for p in params for _ in range(4)]
    shapes += [jax.ShapeDtypeStruct((1, CONV_K * CONV_DIM), f32), jax.ShapeDtypeStruct((1, D_MODEL), f32)]
    return pl.pallas_call(body, name="small_update", out_shape=shapes,
                          compiler_params=_cparams())(gathered, *params, *moms, *vels)


def _place():
    return lax.axis_index("x"), lax.axis_index("y"), lax.axis_index("c")


def _flip(x, y, c, m):
    return (1 - x if m & 4 else x, 1 - y if m & 2 else y, 1 - c if m & 1 else c)


_VMEM_SPEC = pl.BlockSpec(memory_space=pltpu.VMEM)
_ANY_SPEC = pl.BlockSpec(memory_space=pl.ANY)


def _allgather8(v, name):
    n = v.shape[1]

    def body(v_ref, out_ref, send_sems, recv_sems, local_sem):
        x, y, c = _place()

        def rows(px, py, pc):
            return out_ref.at[pl.ds(pl.multiple_of((4 * px + 2 * py + pc) * 8, 8), 8), :]

        def copy(m, src, dst, to):
            return pltpu.make_async_remote_copy(src_ref=src, dst_ref=dst, send_sem=send_sems.at[m - 1],
                                                recv_sem=recv_sems.at[m - 1], device_id=to, device_id_type=MESH)

        mine = pltpu.make_async_copy(v_ref, rows(x, y, c), local_sem)
        mine.start()
        sends = [copy(m, v_ref, rows(x, y, c), _flip(x, y, c, m)) for m in range(1, N_DEV)]
        for cp in sends:
            cp.start()
        for m in range(1, N_DEV):
            peer = _flip(x, y, c, m)
            copy(m, v_ref, rows(*peer), peer).wait_recv()
        for cp in sends:
            cp.wait_send()
        mine.wait()

    return pl.pallas_call(
        body, name=name, out_shape=jax.ShapeDtypeStruct((8 * N_DEV, n), f32), in_specs=[_VMEM_SPEC],
        out_specs=_VMEM_SPEC,
        scratch_shapes=[pltpu.SemaphoreType.DMA((N_DEV - 1,)), pltpu.SemaphoreType.DMA((N_DEV - 1,)),
                        pltpu.SemaphoreType.DMA],
    )(v)


def _remote(src, dst, send_sems, recv_sems, k, to):
    return pltpu.make_async_remote_copy(src_ref=src, dst_ref=dst, send_sem=send_sems.at[k], recv_sem=recv_sems.at[k],
                                        device_id=to, device_id_type=MESH)


def _gather_job(blob_ref, out_ref, send_sems, recv_sems, R):
    x, y, c = _place()
    sib = (x, y, 1 - c)
    hr = R // 2

    def half(px, py, pc):
        return out_ref.at[2 * px + py, pl.ds(pl.multiple_of(pc * hr, 16), hr), :]

    my_half = blob_ref.at[pl.ds(pl.multiple_of(c * hr, 16), hr), :]

    def first():
        return [_remote(my_half, half(x, y, c), send_sems, recv_sems, m - 1, _flip(x, y, c, 2 * m))
                for m in range(1, N_CHIPS)]

    def start():
        for cp in first():
            cp.start()

    def finish():
        passed = []
        for m in range(1, N_CHIPS):
            px, py, pc = _flip(x, y, c, 2 * m)
            _remote(my_half, half(px, py, pc), send_sems, recv_sems, m - 1, (px, py, pc)).wait_recv()
            fwd = _remote(half(px, py, pc), half(px, py, pc), send_sems, recv_sems, 2 + m, sib)
            fwd.start()
            passed.append(fwd)
        for m in range(1, N_CHIPS):
            px, py, pc = _flip(x, y, c, 2 * m)
            _remote(my_half, half(px, py, 1 - pc), send_sems, recv_sems, 2 + m, sib).wait_recv()
        for cp in first() + passed:
            cp.wait_send()

    return start, finish


def _to_sibling_job(g_ref, out_ref, send_sems, recv_sems):
    x, y, c = _place()

    def cps():
        return [_remote(g_ref.at[j, 1 - c], out_ref.at[j], send_sems, recv_sems, j, (x, y, 1 - c)) for j in range(N_CHIPS)]

    def start():
        for cp in cps():
            cp.start()

    def wait():
        for cp in cps():
            cp.wait()

    return start, wait


def _to_chips_job(p_ref, out_ref, send_sems, recv_sems):
    x, y, c = _place()

    def cps():
        out = []
        for m in range(1, N_CHIPS):
            px, py, pc = _flip(x, y, c, 2 * m)
            out.append(_remote(p_ref.at[2 * px + py], out_ref.at[m - 1], send_sems, recv_sems, m - 1, (px, py, pc)))
        return out

    def start():
        for cp in cps():
            cp.start()

    def wait():
        for cp in cps():
            cp.wait()

    return start, wait


def _share_job(g_ref, send_sems, recv_sems, R):
    x, y, c = _place()

    def rows(pc):
        return g_ref.at[pl.ds(pl.multiple_of(pc * (R // 2), 8), R // 2), :]

    def start():
        _remote(rows(c), rows(c), send_sems, recv_sems, 0, (x, y, 1 - c)).start()

    def wait():
        _remote(rows(c), rows(1 - c), send_sems, recv_sems, 0, (x, y, 1 - c)).wait_recv()
        _remote(rows(c), rows(c), send_sems, recv_sems, 0, (x, y, 1 - c)).wait_send()

    return start, wait


def _sems(n):
    return [pltpu.SemaphoreType.DMA((n,)), pltpu.SemaphoreType.DMA((n,))]


def _rs_to_sibling(gb):
    def body(g_ref, out_ref, send_sems, recv_sems):
        start, wait = _to_sibling_job(g_ref, out_ref, send_sems, recv_sems)
        start()
        wait()

    return pl.pallas_call(
        body, name="rs_to_sibling", out_shape=jax.ShapeDtypeStruct((N_CHIPS,) + gb.shape[2:], bf16),
        in_specs=[_ANY_SPEC], out_specs=_ANY_SPEC, scratch_shapes=_sems(N_CHIPS),
    )(gb)


def _rs_share(g):
    R = g.shape[0]

    def body(g_ref, out_ref, send_sems, recv_sems):
        start, wait = _share_job(out_ref, send_sems, recv_sems, R)
        start()
        wait()

    return pl.pallas_call(
        body, name="rs_share", out_shape=jax.ShapeDtypeStruct(g.shape, f32), in_specs=[_ANY_SPEC],
        out_specs=_ANY_SPEC, input_output_aliases={0: 0}, scratch_shapes=_sems(1),
    )(g)


RS_TR_MAX = 512


def _rs_sum_pair(place, gb, recv, name):
    hr = gb.shape[2]
    tr = min(hr, RS_TR_MAX)

    def body(pl_ref, g_ref, r_ref, o_ref):
        o_ref[0] = (g_ref[0, 0].astype(f32) + r_ref[0].astype(f32)).astype(bf16)

    return pl.pallas_call(
        body, name=name,
        grid_spec=pltpu.PrefetchScalarGridSpec(
            num_scalar_prefetch=1, grid=(N_CHIPS, hr // tr),
            in_specs=[pl.BlockSpec((1, 1, tr, D_MODEL), lambda j, i, p: (j, p[0], i, 0)),
                      pl.BlockSpec((1, tr, D_MODEL), lambda j, i, p: (j, i, 0))],
            out_specs=pl.BlockSpec((1, tr, D_MODEL), lambda j, i, p: (j, i, 0))),
        out_shape=jax.ShapeDtypeStruct((N_CHIPS, hr, D_MODEL), bf16),
        compiler_params=_cparams("parallel", "parallel"),
    )(place, gb, recv)


def _rs_sum_chips(place, gb, recv_sib, recv_chips, name):
    hr = gb.shape[2]
    tr = min(hr, RS_TR_MAX)
    nt = hr // tr

    def body(pl_ref, g_ref, r1_ref, r2_ref, o_ref):
        acc = g_ref[0, 0].astype(f32) + r1_ref[0].astype(f32)
        for k in range(N_CHIPS - 1):
            acc = acc + r2_ref[k].astype(f32)
        o_ref[...] = acc

    return pl.pallas_call(
        body, name=name,
        grid_spec=pltpu.PrefetchScalarGridSpec(
            num_scalar_prefetch=1, grid=(nt,),
            in_specs=[pl.BlockSpec((1, 1, tr, D_MODEL), lambda i, p: (p[1], p[0], i, 0)),
                      pl.BlockSpec((1, tr, D_MODEL), lambda i, p: (p[1], i, 0)),
                      pl.BlockSpec((N_CHIPS - 1, tr, D_MODEL), lambda i, p: (0, i, 0))],
            out_specs=pl.BlockSpec((tr, D_MODEL), lambda i, p: (p[0] * nt + i, 0))),
        out_shape=jax.ShapeDtypeStruct((2 * hr, D_MODEL), f32),
        compiler_params=_cparams("parallel"),
    )(place, gb, recv_sib, recv_chips)


def _pad128(v):
    v = v.reshape(1, -1)
    return jnp.pad(v, ((0, 0), (0, 128 - v.shape[1])))


def _row(v):
    return v.reshape(1, -1)


W_COLS = PROJ_WIDTH // N_CHIPS


def _g_in_blocks(gz, gxbc, gdt, gq, gkv):
    g = jnp.concatenate([gz, gxbc, gdt[:W_DT_ROWS], gq, gkv], axis=0)
    return jnp.pad(g.reshape(N_CHIPS, W_COLS, D_MODEL), ((0, 0), (0, D_MODEL - W_COLS), (0, 0)))


def kernel(x, c, ln_in_g, ln_in_b, ada_w, ada_b, w_in, conv_w, conv_b, dt_bias, a_log, d_skip, ssd_norm_w, attn_sinks, w_out, ln1_g, ln1_b, w_ff1, b_ff1, w_ff2, b_ff2, ln2_g, ln2_b, loss_target, m_ln_in_g, m_ln_in_b, m_ada_w, m_ada_b, m_w_in, m_conv_w, m_conv_b, m_dt_bias, m_a_log, m_d_skip, m_ssd_norm_w, m_attn_sinks, m_w_out, m_ln1_g, m_ln1_b, m_w_ff1, m_b_ff1, m_w_ff2, m_b_ff2, m_ln2_g, m_ln2_b, v_ln_in_g, v_ln_in_b, v_ada_w, v_ada_b, v_w_in, v_conv_w, v_conv_b, v_dt_bias, v_a_log, v_d_skip, v_ssd_norm_w, v_attn_sinks, v_w_out, v_ln1_g, v_ln1_b, v_w_ff1, v_b_ff1, v_w_ff2, v_b_ff2, v_ln2_g, v_ln2_b):
    xi, yi, ci = _place()
    chip = 2 * xi + yi
    place = jnp.stack([ci, chip]).astype(jnp.int32)
    x2, tgt = x[0], loss_target[0]

    def as_rows(a):
        return jnp.transpose(a, (2, 0, 1))

    def from_rows(a):
        return jnp.transpose(a, (1, 2, 0))

    cond = jnp.concatenate([c.reshape(2, COND_LANES), conv_w.reshape(3, COND_LANES), jnp.zeros((3, COND_LANES), f32)], axis=0)
    ada_b_mine = lax.dynamic_slice(ada_b, (0, chip * ADA_COLS), (1, ADA_COLS))
    blob_in = jnp.pad(w_in[0].T, ((0, D_MODEL - W_COLS), (0, 0))).astype(bf16)
    cond_all, mod_rows, wall_in = _prologue(cond, ada_w[0], ada_b_mine, blob_in)
    cond_all = cond_all.reshape(N_DEV, 8, COND_LANES)
    c_all = cond_all[:, 0:2].reshape(N_DEV, D_MODEL)
    conv_w_full = jnp.concatenate([cond_all[2 * j, 2:5].reshape(CONV_K, 384) for j in range(N_CHIPS)], axis=1)
    mod = jnp.concatenate([mod_rows[:, 0].reshape(6, D_MODEL), jnp.zeros((2, D_MODEL), f32)], axis=0)
    w_in_f = wall_in[:, :W_COLS].reshape(PROJ_WIDTH, D_MODEL)
    b_ffw = jnp.concatenate([w_ff1[0], w_ff2[0]], axis=0).astype(bf16)
    b_outw = w_out[0].astype(bf16)

    def with_mine(wall, mine):
        return lax.dynamic_update_slice(wall, mine[None], (chip, 0, 0))

    e_mat = _head_expand()
    dsk_x = jnp.repeat(d_skip[0], HEAD_DIM).reshape(1, SSD_WIDTH)
    dtb, alog = _pad128(dt_bias), _pad128(a_log)
    sinks = attn_sinks[0]
    lng, lnb = _row(ln_in_g), _row(ln_in_b)
    u1, z, xr, xc, q, kv, dtr = _inproj_fwd(x2, mod, lng, lnb, w_in_f, conv_w_full, conv_b)
    y, prev_all, wall_out = _ssd_fwd(xc, dtr, dtb, alog, dsk_x, b_outw)
    o, lse, wall_ff = _attn_fwd(q, kv, sinks, b_ffw)
    wall_ff, wall_out = with_mine(wall_ff, b_ffw), with_mine(wall_out, b_outw)
    yn, mix, r1 = _outproj_fwd(y, z, o, x2, mod, lng, lnb, ssd_norm_w, wall_out)

    dr1, u2, s_act, da, df, acc_mlp, db1 = _mlp_fwd_bwd(r1, tgt, mod, ln1_g, ln1_b, ln2_g, ln2_b, wall_ff, b_ff1, wall_ff,
                                                        b_ff2)
    ga = lax.empty((N_CHIPS, GA_ROWS, D_MODEL), bf16)
    ga = _wgrad_blob(ga, u2, da, "wgrad_ff1", lambda t, n: (n, t))
    ga = _wgrad_blob(ga, s_act, df, "wgrad_ff2", lambda t, n: (t // 2, 2 + t % 2))
    ga = ga.reshape(N_CHIPS, 2, GA_ROWS // 2, D_MODEL)
    dy, dz, do, dmix, acc_out, a_sib = _outproj_bwd(dr1, mix, y, z, mod, ssd_norm_w, wall_out, ga)
    gc = lax.empty((N_CHIPS, GC_ROWS, D_MODEL), bf16)
    gc = _wgrad_blob(gc, yn, dmix, "wgrad_out_y", lambda t, n: (t, 0))
    gc = _wgrad_blob(gc, o, dmix, "wgrad_out_o", lambda t, n: (2 + t, 0))
    gc = gc.reshape(N_CHIPS, 2, GC_ROWS // 2, D_MODEL)
    a_pair = _rs_sum_pair(place, ga, a_sib, "rs_sum_pair_a")
    dq, dkv, dsink, a_chips, c_sib = _attn_bwd(q, kv, do, lse, sinks, a_pair, gc)
    g_a = _rs_sum_chips(place, ga, a_sib, a_chips, "rs_sum_chips_a")
    c_pair = _rs_sum_pair(place, gc, c_sib, "rs_sum_pair_c")
    dxc, ddt, acc_ssd, dd_x, g_a, c_chips = _ssd_bwd(xc, dtr, dy, prev_all, dtb, alog, dsk_x, e_mat, g_a, c_pair)
    g_c = _rs_sum_chips(place, gc, c_sib, c_chips, "rs_sum_chips_c")
    dxr, acc_conv, g_c = _conv_bwd(dxc, xr, conv_w_full, conv_b, g_c)
    gb = _g_in_blocks(_wgrad(dz, u1, "wgrad_in_z"), _wgrad(dxr, u1, "wgrad_in_xbc"),
                      _wgrad(ddt.astype(bf16), u1, "wgrad_in_dt"), _wgrad(dq, u1, "wgrad_in_q"),
                      _wgrad(dkv, u1, "wgrad_in_kv")).reshape(N_CHIPS, 2, GB_ROWS // 2, D_MODEL)
    b_sib = _rs_to_sibling(gb)
    b_pair = _rs_sum_pair(place, gb, b_sib, "rs_sum_pair_b")
    grad_x, acc_in, b_chips = _inproj_bwd(dz, dxr, dq, dkv, ddt, dr1, x2, mod, lng, lnb, w_in_f, b_pair)
    g_b = _rs_share(_rs_sum_chips(place, gb, b_sib, b_chips, "rs_sum_chips_b"))

    packed = _small_pack(acc_in, acc_out, acc_mlp, db1, acc_conv, acc_ssd, dd_x, dsink, alog, e_mat)
    small_all = _allgather8(packed.reshape(8, SMALL_N // 8), "gather_small").reshape(N_DEV, SMALL_N)
    given = dict(ada_b=(ada_b, m_ada_b, v_ada_b), ln_in_g=(ln_in_g, m_ln_in_g, v_ln_in_g), ln_in_b=(ln_in_b, m_ln_in_b, v_ln_in_b),
                 conv_b=(conv_b, m_conv_b, v_conv_b), dt_bias=(dt_bias, m_dt_bias, v_dt_bias), a_log=(a_log, m_a_log, v_a_log),
                 d_skip=(d_skip, m_d_skip, v_d_skip), ssd_norm_w=(ssd_norm_w, m_ssd_norm_w, v_ssd_norm_w),
                 attn_sinks=(attn_sinks, m_attn_sinks, v_attn_sinks), ln1_g=(ln1_g, m_ln1_g, v_ln1_g),
                 ln1_b=(ln1_b, m_ln1_b, v_ln1_b), b_ff1=(b_ff1, m_b_ff1, v_b_ff1), b_ff2=(b_ff2, m_b_ff2, v_b_ff2),
                 ln2_g=(ln2_g, m_ln2_g, v_ln2_g), ln2_b=(ln2_b, m_ln2_b, v_ln2_b))
    upd = _small_update(small_all, *([_row(given[n][i]) for n in SMALL_PARAMS] for i in range(3)))
    small_res = {n: [t.reshape(given[n][0].shape) for t in upd[4 * j:4 * j + 4]] for j, n in enumerate(SMALL_PARAMS)}
    g_conv_all, loss_lanes = upd[4 * len(SMALL_PARAMS)], upd[4 * len(SMALL_PARAMS) + 1]
    loss = jnp.sum(loss_lanes)

    dmod_mine = lax.dynamic_slice(small_all[:, :6 * D_MODEL], (0, chip * ADA_COLS), (N_DEV, ADA_COLS))
    big = {"ada_w": [t[None] for t in _ada_bwd(c_all, dmod_mine, ada_w[0], m_ada_w[0], v_ada_w[0])]}

    g_conv = lax.dynamic_slice(g_conv_all.reshape(CONV_K, CONV_DIM), (0, chip * 384), (CONV_K, 384))
    big["conv_w"] = [t[None] for t in _adamw(conv_w[0], g_conv, m_conv_w[0], v_conv_w[0], "adamw_conv_w")]

    g_rows = g_b[:W_COLS].reshape(W_COLS, 1, D_MODEL)
    big["w_in"] = [from_rows(t) for t in (g_rows, *_adamw_rows(as_rows(w_in), g_rows, as_rows(m_w_in), as_rows(v_w_in)))]
    for name, g, row0, (w, m, v) in (("w_out", g_c, 0, (w_out, m_w_out, v_w_out)), ("w_ff1", g_a, 0, (w_ff1, m_w_ff1, v_w_ff1)),
                                     ("w_ff2", g_a, D_MODEL, (w_ff2, m_w_ff2, v_w_ff2))):
        big[name] = [t[None] for t in _adamw(w[0], g, m[0], v[0], "adamw_" + name, row0)]

    order = ("ln_in_g", "ln_in_b", "ada_w", "ada_b", "w_in", "conv_w", "conv_b", "dt_bias", "a_log", "d_skip", "ssd_norm_w",
             "attn_sinks", "w_out", "ln1_g", "ln1_b", "w_ff1", "b_ff1", "w_ff2", "b_ff2", "ln2_g", "ln2_b")
    res = {**small_res, **big}
    return (loss, grad_x[None], *[res[n][k] for k in range(4) for n in order])
```

```python
import numpy as np
import jax
import jax.numpy as jnp
from jax import lax
from jax.experimental import pallas as pl
from jax.experimental.pallas import tpu as pltpu

f32 = jnp.float32
bf16 = jnp.bfloat16

D_MODEL = 1024
SSD_WIDTH = 1024
SSD_HEADS = 16
HEAD_DIM = 64
SSD_STATE = 128
SSD_GROUPS = 2
CHUNK = 128
CONV_K = 4
CONV_DIM = 1536
ATTN_HEADS = 16
D_FF = 4096
PROJ_WIDTH = 3856
ALPHA = 2.0 ** 0.25
LN_EPS = 1e-5
RMS_EPS = 1e-5
ATTN_SCALE = HEAD_DIM ** -0.5
NEG = -1e30

ADAM_LR = 0.001
ADAM_B1 = 0.9
ADAM_B2 = 0.999
ADAM_EPS = 1e-08
ADAM_WD = 0.01
ADAM_STEP = 10

W_Z, W_XBC, W_Q, W_KV = slice(0, 1024), slice(1024, 2560), slice(2576, 3600), slice(3600, 3856)
W_DT = slice(2560, 2688)
W_DT_ROWS = 16
GA_ROWS = 2048
GB_ROWS = 1024
GC_ROWS = 512
WG_TM = 512
STAGE_ROWS = 512
ADAMW_BLOCK_ELEMS = 1 << 18
MLP_TM = 256
DENSE_TM = 512
N_CHIPS = 4
N_DEV = 8
VMEM_LIMIT = 56 * 1024 * 1024
MESH = pl.DeviceIdType.MESH

ALIBI_SLOPES = tuple(2.0 ** (-8.0 / ATTN_HEADS * (i + 1)) for i in range(ATTN_HEADS))


def _cparams(*sem):
    return pltpu.CompilerParams(dimension_semantics=sem, vmem_limit_bytes=VMEM_LIMIT)


def _sigmoid(x):
    return 1.0 / (1.0 + jnp.exp(-x))


def _softplus(x):
    return jnp.maximum(x, 0.0) + jnp.log1p(jnp.exp(-jnp.abs(x)))


def _ln_stats(x):
    mu = jnp.mean(x, axis=-1, keepdims=True)
    xc = x - mu
    var = jnp.mean(xc * xc, axis=-1, keepdims=True)
    rstd = lax.rsqrt(var + LN_EPS)
    return xc * rstd, rstd


def _ln_bwd(dy, xhat, rstd, g):
    dxh = dy * g
    m1 = jnp.mean(dxh, axis=-1, keepdims=True)
    m2 = jnp.mean(dxh * xhat, axis=-1, keepdims=True)
    return rstd * (dxh - m1 - xhat * m2)


def _dot(a, b):
    return jnp.dot(a, b, preferred_element_type=f32)


def _dot_nt(a, b):
    return lax.dot_general(a, b, (((1,), (1,)), ((), ())), preferred_element_type=f32)


def _dot_tn(a, b):
    return lax.dot_general(a, b, (((0,), (0,)), ((), ())), preferred_element_type=f32)


def _dot_exact(a, b):
    return jnp.dot(a, b, preferred_element_type=f32, precision=lax.Precision.HIGHEST)


def _split3(v):
    hi = v.astype(bf16)
    r1 = v - hi.astype(f32)
    mid = r1.astype(bf16)
    lo = (r1 - mid.astype(f32)).astype(bf16)
    return hi, mid, lo


def _sel_dot(sel, v):
    hi, mid, lo = _split3(v)
    return _dot(sel, hi) + _dot(sel, mid) + _dot(sel, lo)


def _dot_sel_nt(v, sel):
    hi = v.astype(bf16)
    mid = (v - hi.astype(f32)).astype(bf16)
    return _dot_nt(hi, sel) + _dot_nt(mid, sel)


def _expand_heads(v):
    lane = lax.broadcasted_iota(jnp.int32, (v.shape[0], 128), 1)
    blocks = [jnp.where(lane < HEAD_DIM, v[:, 2 * b:2 * b + 1], v[:, 2 * b + 1:2 * b + 2]) for b in range(SSD_HEADS // 2)]
    return jnp.concatenate(blocks, axis=1)


def _full(shape):
    nd = len(shape)
    return pl.BlockSpec(shape, lambda *_: (0,) * nd)


def _resident(shape):
    nd = len(shape)
    return pl.BlockSpec(shape, lambda *_: (0,) * nd, pipeline_mode=pl.Buffered(1))


def _rows(tm, n):
    return pl.BlockSpec((tm, n), lambda i: (i, 0))


def _wall_rows(k):
    return pl.BlockSpec((N_CHIPS, D_MODEL, D_MODEL), lambda *_: (0, k, 0), pipeline_mode=pl.Buffered(1))


def _inproj_fwd(x, mod, ln_g, ln_b, w_in_t, conv_w, conv_b):
    L = x.shape[0]
    tm = DENSE_TM
    nt = L // tm

    def body(x_ref, mod_ref, g_ref, b_ref, w_ref, cw_ref, cb_ref,
             u1_ref, z_ref, xr_ref, xc_ref, q_ref, kv_ref, dt_ref, halo, buf):
        @pl.when(pl.program_id(0) == 0)
        def _():
            halo[...] = jnp.zeros_like(halo)

        xhat, _ = _ln_stats(x_ref[...])
        h0 = xhat * g_ref[...] + b_ref[...]
        u1 = (h0 * (1.0 + mod_ref[1:2, :]) + mod_ref[0:1, :]).astype(bf16)
        u1_ref[...] = u1
        z_ref[...] = _dot_nt(u1, w_ref[W_Z, :])
        xr = _dot_nt(u1, w_ref[W_XBC, :])
        xr_ref[...] = xr
        q_ref[...] = _dot_nt(u1, w_ref[W_Q, :]).astype(bf16)
        kv_ref[...] = _dot_nt(u1, w_ref[W_KV, :]).astype(bf16)
        dt_ref[...] = _dot_nt(u1, w_ref[W_DT, :])
        buf[0:8, :] = halo[...]
        buf[8:8 + tm, :] = xr
        pre = cb_ref[...] + cw_ref[0:1, :] * buf[5:5 + tm, :]
        for k in range(1, CONV_K):
            pre = pre + cw_ref[k:k + 1, :] * buf[5 + k:5 + k + tm, :]
        xc_ref[...] = pre * _sigmoid(pre)
        halo[...] = xr[tm - 8:tm, :]

    return pl.pallas_call(
        body, name="inproj_fwd", grid=(nt,),
        in_specs=[_rows(tm, D_MODEL), _full((8, D_MODEL)), _full((1, D_MODEL)), _full((1, D_MODEL)),
                  _resident((PROJ_WIDTH, D_MODEL)), _full((CONV_K, CONV_DIM)), _full((1, CONV_DIM))],
        out_specs=[_rows(tm, D_MODEL), _rows(tm, D_MODEL), _rows(tm, CONV_DIM), _rows(tm, CONV_DIM),
                   _rows(tm, D_MODEL), _rows(tm, 256), _rows(tm, 128)],
        out_shape=[jax.ShapeDtypeStruct((L, D_MODEL), bf16), jax.ShapeDtypeStruct((L, D_MODEL), f32),
                   jax.ShapeDtypeStruct((L, CONV_DIM), f32), jax.ShapeDtypeStruct((L, CONV_DIM), f32),
                   jax.ShapeDtypeStruct((L, D_MODEL), bf16), jax.ShapeDtypeStruct((L, 256), bf16),
                   jax.ShapeDtypeStruct((L, 128), f32)],
        scratch_shapes=[pltpu.VMEM((8, CONV_DIM), f32), pltpu.VMEM((tm + 8, CONV_DIM), f32)],
        compiler_params=_cparams("arbitrary"),
    )(x, mod, ln_g, ln_b, w_in_t, conv_w, conv_b)


def _head_expand():
    e = np.zeros((128, SSD_WIDTH), np.float32)
    for h in range(SSD_HEADS):
        e[h, h * HEAD_DIM:(h + 1) * HEAD_DIM] = 1.0
    return jnp.asarray(e, dtype=bf16)


def _ssd_chunk_common(dt_raw, dtb, a_row):
    T = CHUNK
    lane = lax.broadcasted_iota(jnp.int32, (T, 128), 1)
    dt = jnp.where(lane < SSD_HEADS, _softplus(dt_raw + dtb), 0.0)
    a = dt * a_row
    r = lax.broadcasted_iota(jnp.int32, (T, T), 0)
    c = lax.broadcasted_iota(jnp.int32, (T, T), 1)
    tril = (c <= r).astype(bf16)
    cum = _sel_dot(tril, a)
    dtx = _expand_heads(dt)
    cumx = _expand_heads(cum)
    return dt, a, cum, dtx, cumx, r, c


def _ssd_fwd(xc, dt_raw, dt_bias, a_log, d_skip_x, blob):
    L = xc.shape[0]
    nc = L // CHUNK
    T = CHUNK
    R = blob.shape[0]

    def body(xc_ref, dt_ref, dtb_ref, al_ref, dsk_ref, blob_ref, y_ref, prev_ref, wall_ref, st, send_sems, recv_sems):
        start, finish = _gather_job(blob_ref, wall_ref, send_sems, recv_sems, R)

        @pl.when(pl.program_id(0) == 0)
        def _():
            st[...] = jnp.zeros_like(st)
            start()

        @pl.when(pl.program_id(0) == nc - 1)
        def _():
            finish()

        a_row = -jnp.exp(al_ref[...])
        lane1 = lax.broadcasted_iota(jnp.int32, (1, 128), 1)
        a_row = jnp.where(lane1 < SSD_HEADS, a_row, 0.0)
        dt, a, cum, dtx, cumx, r, c = _ssd_chunk_common(dt_ref[...], dtb_ref[...], a_row)
        cum_t = cum.T
        ex = jnp.exp(cumx)
        last = cumx[T - 1:T, :]
        wx = jnp.exp(last - cumx)
        cdx = jnp.exp(last)
        xs = xc_ref[:, 0:SSD_WIDTH]
        X = xs * dtx
        Xb = X.astype(bf16)
        Xd = (X * wx).astype(bf16)
        prev = st[...]
        prev_ref[0] = prev
        prevb = prev.astype(bf16)
        tri = c <= r
        lane = lax.broadcasted_iota(jnp.int32, (T, 128), 1)
        y_blocks = []
        new_states = []
        for g in range(SSD_GROUPS):
            Bg = xc_ref[:, 1024 + 128 * g:1152 + 128 * g].astype(bf16)
            Cg = xc_ref[:, 1280 + 128 * g:1408 + 128 * g].astype(bf16)
            G = _dot_nt(Cg, Bg)
            yoff = _dot(Cg, prevb[:, 512 * g:512 * (g + 1)])
            new_states.append(_dot_tn(Bg, Xd[:, 512 * g:512 * (g + 1)]))
            for j in range(4):
                blk = 4 * g + j
                Xblk = Xb[:, 128 * blk:128 * (blk + 1)]
                ys = []
                for half in range(2):
                    h = 2 * blk + half
                    seg = jnp.minimum(cum[:, h:h + 1] - cum_t[h:h + 1, :], 0.0)
                    M = jnp.where(tri, G * jnp.exp(seg), 0.0).astype(bf16)
                    ys.append(_dot(M, Xblk))
                yd = jnp.where(lane < HEAD_DIM, ys[0], ys[1])
                sl = slice(128 * blk, 128 * (blk + 1))
                y_blocks.append(yd + ex[:, sl] * yoff[:, 128 * j:128 * (j + 1)] + dsk_ref[:, sl] * xs[:, sl])
        y_ref[...] = jnp.concatenate(y_blocks, axis=1)
        st[...] = prev * cdx + jnp.concatenate(new_states, axis=1)

    return pl.pallas_call(
        body, name="ssd_fwd", grid=(nc,),
        in_specs=[_rows(T, CONV_DIM), _rows(T, 128), _full((1, 128)), _full((1, 128)), _full((1, SSD_WIDTH)), _ANY_SPEC],
        out_specs=[_rows(T, SSD_WIDTH), pl.BlockSpec((1, SSD_STATE, SSD_WIDTH), lambda i: (i, 0, 0)), _ANY_SPEC],
        out_shape=[jax.ShapeDtypeStruct((L, SSD_WIDTH), f32), jax.ShapeDtypeStruct((nc, SSD_STATE, SSD_WIDTH), f32),
                   jax.ShapeDtypeStruct((N_CHIPS, R, D_MODEL), bf16)],
        scratch_shapes=[pltpu.VMEM((SSD_STATE, SSD_WIDTH), f32)] + _sems(6),
        compiler_params=_cparams("arbitrary"),
    )(xc, dt_raw, dt_bias, a_log, d_skip_x, blob)


def _kv_halves(kv_prev, kv_cur, first):
    kv = jnp.concatenate([jnp.where(first, 0.0, kv_prev.astype(f32)), kv_cur.astype(f32)], axis=0)
    lane = lax.broadcasted_iota(jnp.int32, (2 * CHUNK, 128), 1)
    lo = lane < HEAD_DIM
    out = []
    for g in range(2):
        per_half = []
        for half in range(2):
            both = []
            for t in (kv[:, 0:128], kv[:, 128:256]):
                src = t if g == half else pltpu.roll(t, HEAD_DIM, 1)
                both.append(jnp.where(lo if half == 0 else ~lo, src, 0.0).astype(bf16))
            per_half.append(tuple(both))
        out.append(per_half)
    return out


def _attn_masks(first):
    r = lax.broadcasted_iota(jnp.int32, (CHUNK, 2 * CHUNK), 0)
    c = lax.broadcasted_iota(jnp.int32, (CHUNK, 2 * CHUNK), 1)
    dist = r + CHUNK - c
    valid = (dist >= 0) & (dist < CHUNK) & ((c >= CHUNK) | jnp.logical_not(first))
    return dist.astype(f32), valid


def _head_stack(g, half, sink_ref):
    blks = [4 * g + i for i in range(4)]
    heads = [2 * b + half for b in blks]
    slope = jnp.concatenate([jnp.full((CHUNK, 1), ALIBI_SLOPES[h], f32) for h in heads], axis=0)
    sink = jnp.concatenate([jnp.full((CHUNK, 1), sink_ref[h], f32) for h in heads], axis=0)
    return blks, heads, slope, sink


def _attn_fwd(q, kv, sinks, blob):
    L = q.shape[0]
    nb = L // CHUNK
    T = CHUNK
    R = blob.shape[0]

    def body(sink_ref, q_ref, kvp_ref, kvc_ref, blob_ref, o_ref, lse_ref, wall_ref, send_sems, recv_sems):
        first = pl.program_id(0) == 0
        start, finish = _gather_job(blob_ref, wall_ref, send_sems, recv_sems, R)

        @pl.when(first)
        def _():
            start()

        @pl.when(pl.program_id(0) == nb - 1)
        def _():
            finish()

        ext = _kv_halves(kvp_ref[...], kvc_ref[...], first)
        dist, valid = _attn_masks(first)
        lane = lax.broadcasted_iota(jnp.int32, (T, 128), 1)
        lse = jnp.zeros((T, 128), f32)
        o_blocks = []
        for blk in range(8):
            qb = q_ref[:, 128 * blk:128 * (blk + 1)]
            acc = None
            for half in range(2):
                h = 2 * blk + half
                k_ext, v_ext = ext[h // 8][half]
                s = _dot_nt(qb, k_ext) * ATTN_SCALE - ALIBI_SLOPES[h] * dist
                s = jnp.where(valid, s, NEG)
                sink = sink_ref[h]
                m = jnp.maximum(jnp.max(s, axis=-1, keepdims=True), sink)
                p = jnp.exp(s - m)
                den = jnp.sum(p, axis=-1, keepdims=True) + jnp.exp(sink - m)
                pn = (p * (1.0 / den)).astype(bf16)
                oh = _dot(pn, v_ext)
                acc = oh if acc is None else acc + oh
                lse = jnp.where(lane == h, m + jnp.log(den), lse)
            o_blocks.append(acc.astype(bf16))
        o_ref[...] = jnp.concatenate(o_blocks, axis=1)
        lse_ref[...] = lse

    return pl.pallas_call(
        body, name="attn_fwd", grid=(nb,),
        in_specs=[pl.BlockSpec(memory_space=pltpu.SMEM), _rows(T, D_MODEL),
                  pl.BlockSpec((T, 256), lambda i: (jnp.maximum(i - 1, 0), 0)), _rows(T, 256), _ANY_SPEC],
        out_specs=[_rows(T, D_MODEL), _rows(T, 128), _ANY_SPEC],
        out_shape=[jax.ShapeDtypeStruct((L, D_MODEL), bf16), jax.ShapeDtypeStruct((L, 128), f32),
                   jax.ShapeDtypeStruct((N_CHIPS, R, D_MODEL), bf16)],
        scratch_shapes=_sems(6),
        compiler_params=_cparams("arbitrary"),
    )(sinks, q, kv, kv, blob)


def _gated_norm(y, z, w):
    sz = _sigmoid(z)
    hg = y * (z * sz)
    ns, rss = [], []
    for g in range(SSD_GROUPS):
        hs = hg[:, 512 * g:512 * (g + 1)]
        rs = lax.rsqrt(jnp.mean(hs * hs, axis=-1, keepdims=True) + RMS_EPS)
        ns.append(hs * rs)
        rss.append(rs)
    n = jnp.concatenate(ns, axis=1)
    return n * w, n, rss, sz


def _outproj_fwd(y, z, o, x, mod, ln_g, ln_b, norm_w, w_out):
    L = x.shape[0]
    tm = DENSE_TM

    def body(y_ref, z_ref, o_ref, x_ref, mod_ref, g_ref, b_ref, nw_ref, w_ref, yn_ref, mix_ref, r1_ref):
        yn, _, _, _ = _gated_norm(y_ref[...], z_ref[...], nw_ref[...])
        ynb = yn.astype(bf16)
        yn_ref[...] = ynb
        mix = (_dot(ynb[:, 0:512], w_ref[0]) + _dot(ynb[:, 512:1024], w_ref[1])
               + _dot(o_ref[:, 0:512], w_ref[2]) + _dot(o_ref[:, 512:1024], w_ref[3]))
        mix_ref[...] = mix
        xhat, _ = _ln_stats(x_ref[...])
        h0 = xhat * g_ref[...] + b_ref[...]
        r1_ref[...] = ALPHA * h0 + (1.0 + mod_ref[2:3, :]) * mix

    v = _full((1, D_MODEL))
    return pl.pallas_call(
        body, name="outproj_fwd", grid=(L // tm,),
        in_specs=[_rows(tm, D_MODEL), _rows(tm, D_MODEL), _rows(tm, D_MODEL), _rows(tm, D_MODEL),
                  _full((8, D_MODEL)), v, v, v, _resident((N_CHIPS, 512, D_MODEL))],
        out_specs=[_rows(tm, D_MODEL)] * 3,
        out_shape=[jax.ShapeDtypeStruct((L, D_MODEL), bf16), jax.ShapeDtypeStruct((L, D_MODEL), f32),
                   jax.ShapeDtypeStruct((L, D_MODEL), f32)],
        compiler_params=_cparams("parallel"),
    )(y, z, o, x, mod, ln_g, ln_b, norm_w, w_out)


A_LN2G, A_LN2B, A_G2, A_B2, A_SC2, A_SH2, A_LN1G, A_LN1B, A_LOSS = range(9)


def _mlp_fwd_bwd(r1, target, mod, ln1_g, ln1_b, ln2_g, ln2_b, w1, b1, w2, b2):
    L = r1.shape[0]
    tm = MLP_TM
    nj = D_FF // 1024

    def body(r1_ref, t_ref, mod_ref, g1_ref, bb1_ref, g2_ref, bb2_ref, w1_ref, b1_ref, w2_ref, b2_ref,
             dr1_ref, u2_ref, s_ref, da_ref, df_ref, acc_ref, db1_ref, hr):
        @pl.when(pl.program_id(0) == 0)
        def _():
            acc_ref[...] = jnp.zeros_like(acc_ref)
            db1_ref[...] = jnp.zeros_like(db1_ref)

        sc2, sh2, gate2 = mod_ref[4:5, :], mod_ref[3:4, :], mod_ref[5:6, :]
        xhat1, rstd1 = _ln_stats(r1_ref[...])
        h1 = xhat1 * g1_ref[...] + bb1_ref[...]
        u2f = h1 * (1.0 + sc2) + sh2
        u2 = u2f.astype(bf16)
        u2_ref[...] = u2
        f = jnp.zeros((tm, D_MODEL), f32) + b2_ref[...]
        for j in range(nj):
            cs = slice(1024 * j, 1024 * (j + 1))
            a = _dot(u2, w1_ref[j]) + b1_ref[:, cs]
            hrj = jnp.maximum(a, 0.0)
            hr[:, cs] = hrj.astype(bf16)
            sj = (hrj * hrj).astype(bf16)
            s_ref[:, cs] = sj
            f = f + _dot(sj, w2_ref[j])
        r2 = ALPHA * h1 + (1.0 + gate2) * f
        xhat2, rstd2 = _ln_stats(r2)
        h2 = xhat2 * g2_ref[...] + bb2_ref[...]
        diff = h2 - t_ref[...]
        dh2 = diff * (1.0 / D_MODEL)

        def add(row, val):
            acc_ref[row:row + 1, :] += jnp.sum(val, axis=0, keepdims=True)

        add(A_LOSS, diff * diff * (0.5 / D_MODEL))
        add(A_LN2G, dh2 * xhat2)
        add(A_LN2B, dh2)
        dr2 = _ln_bwd(dh2, xhat2, rstd2, g2_ref[...])
        add(A_G2, dr2 * f)
        df = dr2 * (1.0 + gate2)
        add(A_B2, df)
        dfb = df.astype(bf16)
        df_ref[...] = dfb
        du2 = jnp.zeros((tm, D_MODEL), f32)
        for j in range(nj):
            cs = slice(1024 * j, 1024 * (j + 1))
            ds = _dot_nt(dfb, w2_ref[j])
            daj = ds * (2.0 * hr[:, cs].astype(f32))
            db1_ref[:, cs] += jnp.sum(daj, axis=0, keepdims=True)
            dajb = daj.astype(bf16)
            da_ref[:, cs] = dajb
            du2 = du2 + _dot_nt(dajb, w1_ref[j])
        add(A_SC2, du2 * h1)
        add(A_SH2, du2)
        dh1 = ALPHA * dr2 + du2 * (1.0 + sc2)
        add(A_LN1G, dh1 * xhat1)
        add(A_LN1B, dh1)
        dr1_ref[...] = _ln_bwd(dh1, xhat1, rstd1, g1_ref[...])

    v = _full((1, D_MODEL))
    return pl.pallas_call(
        body, name="mlp_fwd_bwd", grid=(L // tm,),
        in_specs=[_rows(tm, D_MODEL), _rows(tm, D_MODEL), _full((8, D_MODEL)), v, v, v, v,
                  _wall_rows(0), _full((1, D_FF)), _wall_rows(1), v],
        out_specs=[_rows(tm, D_MODEL), _rows(tm, D_MODEL), _rows(tm, D_FF), _rows(tm, D_FF), _rows(tm, D_MODEL),
                   _full((16, D_MODEL)), _full((1, D_FF))],
        out_shape=[jax.ShapeDtypeStruct((L, D_MODEL), f32), jax.ShapeDtypeStruct((L, D_MODEL), bf16),
                   jax.ShapeDtypeStruct((L, D_FF), bf16), jax.ShapeDtypeStruct((L, D_FF), bf16),
                   jax.ShapeDtypeStruct((L, D_MODEL), bf16), jax.ShapeDtypeStruct((16, D_MODEL), f32),
                   jax.ShapeDtypeStruct((1, D_FF), f32)],
        scratch_shapes=[pltpu.VMEM((tm, D_FF), bf16)],
        compiler_params=_cparams("arbitrary"),
    )(r1, target, mod, ln1_g, ln1_b, ln2_g, ln2_b, w1, b1, w2, b2)


def _wgrad(a, b, name):
    L, M = a.shape
    N = b.shape[1]
    tm = min(M, 512)
    tn = next(t for t in (1024, 768, 512, 256, 128) if N % t == 0)

    def body(a_ref, b_ref, o_ref):
        o_ref[...] = _dot_tn(a_ref[...], b_ref[...]).astype(bf16)

    return pl.pallas_call(
        body, name=name, grid=(M // tm, N // tn),
        in_specs=[pl.BlockSpec((L, tm), lambda i, j: (0, i)), pl.BlockSpec((L, tn), lambda i, j: (0, j))],
        out_specs=pl.BlockSpec((tm, tn), lambda i, j: (i, j)),
        out_shape=jax.ShapeDtypeStruct((M, N), bf16),
        compiler_params=_cparams("parallel", "parallel"),
    )(a, b)


def _wgrad_blob(blob, a, b, name, place_of):
    L, M = a.shape
    N = b.shape[1]

    def body(blob_ref, a_ref, b_ref, o_ref):
        o_ref[0] = _dot_tn(a_ref[...], b_ref[...]).astype(bf16)

    return pl.pallas_call(
        body, name=name, grid=(M // WG_TM, N // D_MODEL),
        in_specs=[pl.BlockSpec(memory_space=pl.ANY), pl.BlockSpec((L, WG_TM), lambda t, n: (0, t)),
                  pl.BlockSpec((L, D_MODEL), lambda t, n: (0, n))],
        out_specs=pl.BlockSpec((1, WG_TM, D_MODEL), lambda t, n: (*place_of(t, n), 0)),
        out_shape=jax.ShapeDtypeStruct(blob.shape, bf16), input_output_aliases={0: 0},
        compiler_params=_cparams("parallel", "parallel"),
    )(blob, a, b)


def _outproj_bwd(dr1, mix, y, z, mod, norm_w, w_out, gb):
    L = dr1.shape[0]
    tm = DENSE_TM
    nt = L // tm

    def body(dr1_ref, mix_ref, y_ref, z_ref, mod_ref, nw_ref, w_ref, gb_ref,
             dy_ref, dz_ref, do_ref, dmix_ref, acc_ref, sib_ref, send_sems, recv_sems):
        start, wait = _to_sibling_job(gb_ref, sib_ref, send_sems, recv_sems)

        @pl.when(pl.program_id(0) == 0)
        def _():
            acc_ref[...] = jnp.zeros_like(acc_ref)
            start()

        @pl.when(pl.program_id(0) == nt - 1)
        def _():
            wait()

        dr1 = dr1_ref[...]
        acc_ref[0:1, :] += jnp.sum(dr1 * mix_ref[...], axis=0, keepdims=True)
        dmix = (dr1 * (1.0 + mod_ref[2:3, :])).astype(bf16)
        dmix_ref[...] = dmix
        dyn = jnp.concatenate([_dot_nt(dmix, w_ref[0]), _dot_nt(dmix, w_ref[1])], axis=1)
        do_ref[...] = jnp.concatenate([_dot_nt(dmix, w_ref[2]), _dot_nt(dmix, w_ref[3])], axis=1).astype(bf16)
        yv, zv = y_ref[...], z_ref[...]
        _, n, rss, sz = _gated_norm(yv, zv, nw_ref[...])
        acc_ref[1:2, :] += jnp.sum(dyn * n, axis=0, keepdims=True)
        dn = dyn * nw_ref[...]
        parts = []
        for g in range(SSD_GROUPS):
            sl = slice(512 * g, 512 * (g + 1))
            dng, ng = dn[:, sl], n[:, sl]
            parts.append(rss[g] * (dng - ng * jnp.mean(dng * ng, axis=-1, keepdims=True)))
        dhg = jnp.concatenate(parts, axis=1)
        dy_ref[...] = dhg * (zv * sz)
        dz_ref[...] = (dhg * yv * (sz * (1.0 + zv * (1.0 - sz)))).astype(bf16)

    return pl.pallas_call(
        body, name="outproj_bwd", grid=(nt,),
        in_specs=[_rows(tm, D_MODEL)] * 4 + [_full((8, D_MODEL)), _full((1, D_MODEL)), _resident((N_CHIPS, 512, D_MODEL)),
                  _ANY_SPEC],
        out_specs=[_rows(tm, D_MODEL)] * 4 + [_full((8, D_MODEL)), _ANY_SPEC],
        out_shape=[jax.ShapeDtypeStruct((L, D_MODEL), f32)] + [jax.ShapeDtypeStruct((L, D_MODEL), bf16)] * 3
        + [jax.ShapeDtypeStruct((8, D_MODEL), f32), jax.ShapeDtypeStruct((N_CHIPS,) + gb.shape[2:], bf16)],
        scratch_shapes=_sems(N_CHIPS),
        compiler_params=_cparams("arbitrary"),
    )(dr1, mix, y, z, mod, norm_w, w_out, gb)


def _attn_bwd(q, kv, do, lse, sinks, pb, gb2):
    L = q.shape[0]
    nb = L // CHUNK
    T = CHUNK

    def body(sink_ref, q_ref, kvp_ref, kvc_ref, do_ref, lse_ref, pb_ref, gb2_ref, dq_ref, dkv_ref, dsink_ref, chips_ref,
             sib2_ref, carry, send_sems, recv_sems, send_sems2, recv_sems2):
        n = pl.program_id(0)
        start, wait = _to_chips_job(pb_ref, chips_ref, send_sems, recv_sems)
        start2, wait2 = _to_sibling_job(gb2_ref, sib2_ref, send_sems2, recv_sems2)

        @pl.when(n == 0)
        def _():
            carry[...] = jnp.zeros_like(carry)
            dsink_ref[...] = jnp.zeros_like(dsink_ref)
            start()
            start2()

        @pl.when(n < nb)
        def _():
            first = n == 0
            ext = _kv_halves(kvp_ref[...], kvc_ref[...], first)
            dist, valid = _attn_masks(first)
            dist4, valid4 = jnp.concatenate([dist] * 4, axis=0), jnp.concatenate([valid] * 4, axis=0)
            lane1 = lax.broadcasted_iota(jnp.int32, (1, 128), 1)
            lse = lse_ref[...]
            qts = [q_ref[:, 128 * b:128 * (b + 1)].astype(f32).T.astype(bf16) for b in range(8)]
            dots = [do_ref[:, 128 * b:128 * (b + 1)].astype(f32).T.astype(bf16) for b in range(8)]
            acck = [None, None]
            accv = [None, None]
            dsink = jnp.zeros((1, 128), f32)
            dq_acc = [None] * 8
            for g in range(2):
                for half in range(2):
                    k_ext, v_ext = ext[g][half]
                    blks, heads, slope, sink = _head_stack(g, half, sink_ref)
                    qs = jnp.concatenate([q_ref[:, 128 * b:128 * (b + 1)] for b in blks], axis=0)
                    dos = jnp.concatenate([do_ref[:, 128 * b:128 * (b + 1)] for b in blks], axis=0)
                    rows = slice(HEAD_DIM * half, HEAD_DIM * (half + 1))
                    qt = jnp.concatenate([qts[b][rows, :] for b in blks], axis=1)
                    dot_ = jnp.concatenate([dots[b][rows, :] for b in blks], axis=1)
                    lse_col = jnp.concatenate([lse[:, h:h + 1] for h in heads], axis=0)
                    s = _dot_nt(qs, k_ext) * ATTN_SCALE - slope * dist4
                    p = jnp.where(valid4, jnp.exp(s - lse_col), 0.0)
                    dp = _dot_nt(dos, v_ext)
                    delta = jnp.sum(p * dp, axis=-1, keepdims=True)
                    ds = (p * (dp - delta) * ATTN_SCALE).astype(bf16)
                    sd = jnp.exp(sink - lse_col) * delta
                    dqs = _dot(ds, k_ext)
                    for i, b in enumerate(blks):
                        seg = slice(T * i, T * (i + 1))
                        dq_acc[b] = dqs[seg, :] if dq_acc[b] is None else dq_acc[b] + dqs[seg, :]
                        dsink = dsink - jnp.where(lane1 == heads[i], jnp.sum(sd[seg, :], axis=0, keepdims=True), 0.0)
                    dk = _dot(qt, ds)
                    dv = _dot(dot_, p.astype(bf16))
                    acck[g] = dk if acck[g] is None else acck[g] + dk
                    accv[g] = dv if accv[g] is None else accv[g] + dv
            dq_ref[...] = jnp.concatenate([a.astype(bf16) for a in dq_acc], axis=1)
            dsink_ref[...] += dsink
            dkv = jnp.concatenate([jnp.concatenate(acck, axis=0).T, jnp.concatenate(accv, axis=0).T], axis=1)
            dkv_ref[...] = (carry[...] + dkv[0:T, :]).astype(bf16)
            carry[...] = dkv[T:2 * T, :]

        @pl.when(n == nb)
        def _():
            dkv_ref[...] = carry[...].astype(bf16)
            wait()
            wait2()

    cur = lambda i: (jnp.minimum(i, nb - 1), 0)
    return pl.pallas_call(
        body, name="attn_bwd", grid=(nb + 1,),
        in_specs=[pl.BlockSpec(memory_space=pltpu.SMEM), pl.BlockSpec((T, D_MODEL), cur),
                  pl.BlockSpec((T, 256), lambda i: (jnp.maximum(jnp.minimum(i, nb - 1) - 1, 0), 0)),
                  pl.BlockSpec((T, 256), cur), pl.BlockSpec((T, D_MODEL), cur), pl.BlockSpec((T, 128), cur), _ANY_SPEC,
                  _ANY_SPEC],
        out_specs=[pl.BlockSpec((T, D_MODEL), cur), pl.BlockSpec((T, 256), lambda i: (jnp.maximum(i - 1, 0), 0)),
                   _full((1, 128)), _ANY_SPEC, _ANY_SPEC],
        out_shape=[jax.ShapeDtypeStruct((L, D_MODEL), bf16), jax.ShapeDtypeStruct((L, 256), bf16),
                   jax.ShapeDtypeStruct((1, 128), f32), jax.ShapeDtypeStruct((N_CHIPS - 1,) + pb.shape[1:], bf16),
                   jax.ShapeDtypeStruct((N_CHIPS,) + gb2.shape[2:], bf16)],
        scratch_shapes=[pltpu.VMEM((T, 256), f32)] + _sems(N_CHIPS - 1) + _sems(N_CHIPS),
        compiler_params=_cparams("arbitrary"),
    )(sinks, q, kv, kv, do, lse, pb, gb2)


def _ssd_bwd(xc, dt_raw, dy, prev_all, dt_bias, a_log, d_skip_x, e_mat, g, pb2):
    L = xc.shape[0]
    nc = L // CHUNK
    T = CHUNK
    RG = g.shape[0]

    def body(xc_ref, dt_ref, dy_ref, prev_ref, dtb_ref, al_ref, dsk_ref, e_ref, g_in_ref, pb2_ref,
             dxc_ref, ddt_ref, acc_ref, dd_ref, g_ref, chips2_ref, dst, dxs_s, send_sems, recv_sems, send_sems2, recv_sems2):
        start, wait = _share_job(g_ref, send_sems, recv_sems, RG)
        start2, wait2 = _to_chips_job(pb2_ref, chips2_ref, send_sems2, recv_sems2)

        @pl.when(pl.program_id(0) == 0)
        def _():
            dst[...] = jnp.zeros_like(dst)
            acc_ref[...] = jnp.zeros_like(acc_ref)
            dd_ref[...] = jnp.zeros_like(dd_ref)
            start()
            start2()

        @pl.when(pl.program_id(0) == nc - 1)
        def _():
            wait()
            wait2()

        lane1 = lax.broadcasted_iota(jnp.int32, (1, 128), 1)
        a_row = jnp.where(lane1 < SSD_HEADS, -jnp.exp(al_ref[...]), 0.0)
        e_mat_v = e_ref[...]
        dt, a, cum, dtx, cumx, r, c = _ssd_chunk_common(dt_ref[...], dtb_ref[...], a_row)
        cum_t = cum.T
        ex = jnp.exp(cumx)
        last = cumx[T - 1:T, :]
        wx = jnp.exp(last - cumx)
        cdx = jnp.exp(last)
        xs = xc_ref[:, 0:SSD_WIDTH]
        X = xs * dtx
        Xb = X.astype(bf16)
        Xdb = (X * wx).astype(bf16)
        dyv = dy_ref[...]
        prev = prev_ref[0]
        prevb = prev.astype(bf16)
        dnew = dst[...]
        dnewb = dnew.astype(bf16)
        tri = c <= r
        lane = lax.broadcasted_iota(jnp.int32, (T, 128), 1)
        sub = lax.broadcasted_iota(jnp.int32, (128, T), 0)
        lo = lane < HEAD_DIM

        def red(vals, g):
            return _dot_sel_nt(vals, e_mat_v[:, 512 * g:512 * (g + 1)])

        de = jnp.zeros((T, 128), f32)
        dw = jnp.zeros((T, 128), f32)
        ddt_x = jnp.zeros((T, 128), f32)
        dcum_col = jnp.zeros((T, 128), f32)
        dcum_row = jnp.zeros((128, T), f32)
        dprev_parts, dBs, dCs = [], [], []
        for g in range(SSD_GROUPS):
            s5 = slice(512 * g, 512 * (g + 1))
            Bg = xc_ref[:, 1024 + 128 * g:1152 + 128 * g].astype(bf16)
            Cg = xc_ref[:, 1280 + 128 * g:1408 + 128 * g].astype(bf16)
            G = _dot_nt(Cg, Bg)
            Z = _dot(Cg, prevb[:, s5])
            dyg = dyv[:, s5]
            dZb = (dyg * ex[:, s5]).astype(bf16)
            dXd = _dot(Bg, dnewb[:, s5])
            dC = _dot_nt(dZb, prevb[:, s5])
            dB = _dot_nt(Xdb[:, s5], dnewb[:, s5])
            dprev_parts.append(_dot_tn(Cg, dZb) + dnew[:, s5] * cdx[:, s5])
            de = de + red(dyg * Z, g)
            dw = dw + red(dXd * X[:, s5], g)
            dXg = dXd * wx[:, s5]
            dG = jnp.zeros((T, T), f32)
            for j in range(4):
                blk = 4 * g + j
                sl = slice(128 * blk, 128 * (blk + 1))
                Xblk = Xb[:, sl]
                dyblk = dyv[:, sl]
                dyblk_b = dyblk.astype(bf16)
                dxh = []
                for half in range(2):
                    h = 2 * blk + half
                    seg = jnp.minimum(cum[:, h:h + 1] - cum_t[h:h + 1, :], 0.0)
                    Lm = jnp.where(tri, jnp.exp(seg), 0.0)
                    M = G * Lm
                    dyh = jnp.where(lo if half == 0 else ~lo, dyblk, 0.0).astype(bf16)
                    dM = _dot_nt(dyh, Xblk)
                    dG = dG + dM * Lm
                    Q = dM * M
                    dcum_col = dcum_col + jnp.where(lane == h, jnp.sum(Q, axis=1, keepdims=True), 0.0)
                    dcum_row = dcum_row + jnp.where(sub == h, jnp.sum(Q, axis=0, keepdims=True), 0.0)
                    dxh.append(_dot_tn(M.astype(bf16), dyblk_b))
                dXblk = dXg[:, 128 * j:128 * (j + 1)] + jnp.where(lo, dxh[0], dxh[1])
                xsb = xs[:, sl]
                dxs_s[:, sl] = dXblk * dtx[:, sl] + dsk_ref[:, sl] * dyblk
                ddt_x = ddt_x + _dot_sel_nt(dXblk * xsb, e_mat_v[:, sl])
                dd_ref[:, sl] += jnp.sum(dyblk * xsb, axis=0, keepdims=True)
            dGb = dG.astype(bf16)
            dCs.append(dC + _dot(dGb, Bg))
            dBs.append(dB + _dot_tn(dGb, Cg))
        e16 = jnp.exp(cum)
        cum_last = cum[T - 1:T, :]
        w16 = jnp.exp(cum_last - cum)
        dcd = jnp.sum(dnew * prev, axis=0, keepdims=True)
        dcd16 = red(dcd[:, 0:512], 0) + red(dcd[:, 512:1024], 1)
        dww = dw * w16
        extra = jnp.sum(dww, axis=0, keepdims=True) + dcd16 * jnp.exp(cum_last)
        rowi = lax.broadcasted_iota(jnp.int32, (T, 128), 0)
        dcum = dcum_col - dcum_row.T + de * e16 - dww + jnp.where(rowi == T - 1, extra, 0.0)
        da = _sel_dot((c >= r).astype(bf16), dcum)
        ddt = ddt_x + da * a_row
        acc_ref[0:1, :] += jnp.sum(da * dt, axis=0, keepdims=True)
        ddt_raw = jnp.where(lane < SSD_HEADS, ddt * _sigmoid(dt_ref[...] + dtb_ref[...]), 0.0)
        ddt_ref[...] = ddt_raw
        acc_ref[1:2, :] += jnp.sum(ddt_raw, axis=0, keepdims=True)
        dxc_ref[:, 0:SSD_WIDTH] = dxs_s[...]
        dxc_ref[:, 1024:1280] = jnp.concatenate(dBs, axis=1)
        dxc_ref[:, 1280:1536] = jnp.concatenate(dCs, axis=1)
        dst[...] = jnp.concatenate(dprev_parts, axis=1)

    rev = lambda i: (nc - 1 - i, 0)
    return pl.pallas_call(
        body, name="ssd_bwd", grid=(nc,),
        in_specs=[pl.BlockSpec((T, CONV_DIM), rev), pl.BlockSpec((T, 128), rev), pl.BlockSpec((T, SSD_WIDTH), rev),
                  pl.BlockSpec((1, SSD_STATE, SSD_WIDTH), lambda i: (nc - 1 - i, 0, 0)),
                  _full((1, 128)), _full((1, 128)), _full((1, SSD_WIDTH)), _full((128, SSD_WIDTH)), _ANY_SPEC, _ANY_SPEC],
        out_specs=[pl.BlockSpec((T, CONV_DIM), rev), pl.BlockSpec((T, 128), rev), _full((8, 128)),
                   _full((1, SSD_WIDTH)), _ANY_SPEC, _ANY_SPEC],
        out_shape=[jax.ShapeDtypeStruct((L, CONV_DIM), f32), jax.ShapeDtypeStruct((L, 128), f32),
                   jax.ShapeDtypeStruct((8, 128), f32), jax.ShapeDtypeStruct((1, SSD_WIDTH), f32),
                   jax.ShapeDtypeStruct(g.shape, f32), jax.ShapeDtypeStruct((N_CHIPS - 1,) + pb2.shape[1:], bf16)],
        input_output_aliases={8: 4},
        scratch_shapes=[pltpu.VMEM((SSD_STATE, SSD_WIDTH), f32), pltpu.VMEM((T, SSD_WIDTH), f32)] + _sems(1)
        + _sems(N_CHIPS - 1),
        compiler_params=_cparams("arbitrary"),
    )(xc, dt_raw, dy, prev_all, dt_bias, a_log, d_skip_x, e_mat, g, pb2)


def _conv_bwd(dxc, xr, conv_w, conv_b, g):
    L = dxc.shape[0]
    tm = 256
    nt = L // tm
    RG = g.shape[0]

    def body(dxc_ref, xr_ref, xh_ref, cw_ref, cb_ref, g_in_ref, dxr_ref, acc_ref, g_ref, carry, buf, ext, send_sems, recv_sems):
        i = pl.program_id(0)
        start, wait = _share_job(g_ref, send_sems, recv_sems, RG)

        @pl.when(i == 0)
        def _():
            carry[...] = jnp.zeros_like(carry)
            acc_ref[...] = jnp.zeros_like(acc_ref)
            ext[tm + 16:tm + CHUNK, :] = jnp.zeros((CHUNK - 16, CONV_DIM), bf16)
            start()

        @pl.when(i == nt - 1)
        def _():
            wait()

        buf[0:8, :] = jnp.where(i == nt - 1, 0.0, xh_ref[...])
        u = xr_ref[...]
        buf[8:8 + tm, :] = u
        pre = cb_ref[...] + cw_ref[CONV_K - 1:CONV_K, :] * u
        for k in range(CONV_K - 1):
            pre = pre + cw_ref[k:k + 1, :] * buf[5 + k:5 + k + tm, :]
        sg = _sigmoid(pre)
        dpre = dxc_ref[...] * (sg * (1.0 + pre * (1.0 - sg)))
        acc_ref[4:5, :] += jnp.sum(dpre, axis=0, keepdims=True)
        dpb = dpre.astype(bf16)
        ext[0:tm, :] = dpb
        ext[tm:tm + 16, :] = carry[...]
        acc_ref[CONV_K - 1:CONV_K, :] += jnp.sum(u * dpre, axis=0, keepdims=True)
        du = cw_ref[CONV_K - 1:CONV_K, :] * dpre
        r = lax.broadcasted_iota(jnp.int32, (CHUNK, 2 * CHUNK), 0)
        c = lax.broadcasted_iota(jnp.int32, (CHUNK, 2 * CHUNK), 1)
        for j in range(1, CONV_K):
            move = (c == r + j).astype(bf16)
            up = jnp.concatenate([_dot(move, ext[CHUNK * b:CHUNK * (b + 2), :]) for b in range(tm // CHUNK)], axis=0)
            k = CONV_K - 1 - j
            du = du + cw_ref[k:k + 1, :] * up
            acc_ref[k:k + 1, :] += jnp.sum(u * up, axis=0, keepdims=True)
        dxr_ref[...] = du.astype(bf16)
        carry[...] = dpb[0:16, :]

    rev = lambda i: (nt - 1 - i, 0)
    return pl.pallas_call(
        body, name="conv_bwd", grid=(nt,),
        in_specs=[pl.BlockSpec((tm, CONV_DIM), rev), pl.BlockSpec((tm, CONV_DIM), rev),
                  pl.BlockSpec((8, CONV_DIM), lambda i: (jnp.maximum((nt - 1 - i) * (tm // 8) - 1, 0), 0)),
                  _full((CONV_K, CONV_DIM)), _full((1, CONV_DIM)), _ANY_SPEC],
        out_specs=[pl.BlockSpec((tm, CONV_DIM), rev), _full((8, CONV_DIM)), _ANY_SPEC],
        out_shape=[jax.ShapeDtypeStruct((L, CONV_DIM), bf16), jax.ShapeDtypeStruct((8, CONV_DIM), f32),
                   jax.ShapeDtypeStruct(g.shape, f32)],
        input_output_aliases={5: 2},
        scratch_shapes=[pltpu.VMEM((16, CONV_DIM), bf16), pltpu.VMEM((tm + 8, CONV_DIM), f32),
                        pltpu.VMEM((tm + CHUNK, CONV_DIM), bf16)] + _sems(1),
        compiler_params=_cparams("arbitrary"),
    )(dxc, xr, xr, conv_w, conv_b, g)


def _inproj_bwd(dz, dxr, dq, dkv, ddt, dr1, x, mod, ln_g, ln_b, w_in, pb):
    L = x.shape[0]
    tm = DENSE_TM
    nt = L // tm

    def body(dz_ref, dxr_ref, dq_ref, dkv_ref, ddt_ref, dr1_ref, x_ref, mod_ref, g_ref, b_ref, w_ref, pb_ref,
             dx_ref, acc_ref, chips_ref, send_sems, recv_sems):
        start, wait = _to_chips_job(pb_ref, chips_ref, send_sems, recv_sems)

        @pl.when(pl.program_id(0) == 0)
        def _():
            acc_ref[...] = jnp.zeros_like(acc_ref)
            start()

        @pl.when(pl.program_id(0) == nt - 1)
        def _():
            wait()

        du1 = (_dot(dz_ref[...], w_ref[W_Z, :]) + _dot(dxr_ref[...], w_ref[W_XBC, :])
               + _dot(dq_ref[...], w_ref[W_Q, :]) + _dot(dkv_ref[...], w_ref[W_KV, :])
               + _dot(ddt_ref[...].astype(bf16), w_ref[W_DT, :]))
        xhat, rstd = _ln_stats(x_ref[...])
        h0 = xhat * g_ref[...] + b_ref[...]
        acc_ref[0:1, :] += jnp.sum(du1 * h0, axis=0, keepdims=True)
        acc_ref[1:2, :] += jnp.sum(du1, axis=0, keepdims=True)
        dh0 = du1 * (1.0 + mod_ref[1:2, :]) + ALPHA * dr1_ref[...]
        acc_ref[2:3, :] += jnp.sum(dh0 * xhat, axis=0, keepdims=True)
        acc_ref[3:4, :] += jnp.sum(dh0, axis=0, keepdims=True)
        dx_ref[...] = _ln_bwd(dh0, xhat, rstd, g_ref[...])

    v = _full((1, D_MODEL))
    return pl.pallas_call(
        body, name="inproj_bwd", grid=(nt,),
        in_specs=[_rows(tm, D_MODEL), _rows(tm, CONV_DIM), _rows(tm, D_MODEL), _rows(tm, 256), _rows(tm, 128),
                  _rows(tm, D_MODEL), _rows(tm, D_MODEL), _full((8, D_MODEL)), v, v, _resident((PROJ_WIDTH, D_MODEL)),
                  _ANY_SPEC],
        out_specs=[_rows(tm, D_MODEL), _full((8, D_MODEL)), _ANY_SPEC],
        out_shape=[jax.ShapeDtypeStruct((L, D_MODEL), f32), jax.ShapeDtypeStruct((8, D_MODEL), f32),
                   jax.ShapeDtypeStruct((N_CHIPS - 1,) + pb.shape[1:], bf16)],
        scratch_shapes=_sems(N_CHIPS - 1),
        compiler_params=_cparams("arbitrary"),
    )(dz, dxr, dq, dkv, ddt, dr1, x, mod, ln_g, ln_b, w_in, pb)


def _adamw_math(w, g, m, v):
    m = ADAM_B1 * m + (1.0 - ADAM_B1) * g
    v = ADAM_B2 * v + (1.0 - ADAM_B2) * (g * g)
    m_hat = m / (1.0 - ADAM_B1 ** ADAM_STEP)
    v_hat = v / (1.0 - ADAM_B2 ** ADAM_STEP)
    delta = -ADAM_LR * (m_hat / (jnp.sqrt(v_hat) + ADAM_EPS) + ADAM_WD * w)
    return delta, m, v


def _adamw(w, g, m, v, name, g_row0=0):
    R, C = w.shape

    def body(w_ref, g_ref, m_ref, v_ref, g2_ref, d_ref, m2_ref, v2_ref):
        gv = g_ref[...]
        g2_ref[...] = gv
        d_ref[...], m2_ref[...], v2_ref[...] = _adamw_math(w_ref[...], gv, m_ref[...], v_ref[...])

    cap = max(8, ADAMW_BLOCK_ELEMS // C)
    tr = R if R <= cap else next(t for t in range(cap - cap % 8, 7, -8) if R % t == 0)
    spec = pl.BlockSpec((tr, C), lambda i: (i, 0))
    g_spec = pl.BlockSpec((tr, C), lambda i: (g_row0 // tr + i, 0))
    return pl.pallas_call(
        body, name=name, grid=(R // tr,), in_specs=[spec, g_spec, spec, spec], out_specs=[spec] * 4,
        out_shape=[jax.ShapeDtypeStruct((R, C), f32)] * 4, compiler_params=_cparams("parallel"),
    )(w, g, m, v)


def _adamw_rows(w, g, m, v):
    R, _, C = w.shape
    tr = R // 4

    def body(w_ref, g_ref, m_ref, v_ref, d_ref, m2_ref, v2_ref):
        d_ref[...], m2_ref[...], v2_ref[...] = _adamw_math(w_ref[...], g_ref[...], m_ref[...], v_ref[...])

    spec = pl.BlockSpec((tr, 1, C), lambda i: (i, 0, 0))
    return pl.pallas_call(
        body, name="adamw_w_in", grid=(R // tr,), in_specs=[spec] * 4, out_specs=[spec] * 3,
        out_shape=[jax.ShapeDtypeStruct((R, 1, C), f32)] * 3, compiler_params=_cparams("parallel"),
    )(w, g, m, v)


ADA_COLS = 6 * D_MODEL // N_CHIPS
ADA_TN = 512


COND_LANES = 512


def _prologue(cond, ada_w, ada_b, blob):
    R = blob.shape[0]

    def body(cond_ref, w_ref, b_ref, blob_ref, call_ref, mod_ref, wall_ref, mod_s, stage, gs, gr, ms, mr, ws, wr, local_sem):
        x, y, c = _place()
        start_w, finish_w = _gather_job(blob_ref, wall_ref, ws, wr, R)

        def rows(ref, px, py, pc):
            return ref.at[pl.ds(pl.multiple_of((4 * px + 2 * py + pc) * 8, 8), 8), :]

        mine = pltpu.make_async_copy(cond_ref, rows(call_ref, x, y, c), local_sem)
        mine.start()
        sends = [_remote(cond_ref, rows(call_ref, x, y, c), gs, gr, m - 1, _flip(x, y, c, m)) for m in range(1, N_DEV)]
        for cp in sends:
            cp.start()
        start_w()
        for m in range(1, N_DEV):
            peer = _flip(x, y, c, m)
            _remote(cond_ref, rows(call_ref, *peer), gs, gr, m - 1, peer).wait_recv()
        for cp in sends:
            cp.wait_send()
        mine.wait()

        for k in range(R // STAGE_ROWS):
            part = pl.ds(STAGE_ROWS * k, STAGE_ROWS)
            cin = pltpu.make_async_copy(blob_ref.at[part, :], stage, local_sem)
            cin.start()
            cin.wait()
            cout = pltpu.make_async_copy(stage, wall_ref.at[2 * x + y, part, :], local_sem)
            cout.start()
            cout.wait()

        lo = jnp.concatenate([call_ref[8 * d:8 * d + 1, :] for d in range(N_DEV)], axis=0)
        hi = jnp.concatenate([call_ref[8 * d + 1:8 * d + 2, :] for d in range(N_DEV)], axis=0)
        mod_all = (_dot_exact(lo * _sigmoid(lo), w_ref[0:COND_LANES, :]) + _dot_exact(hi * _sigmoid(hi), w_ref[COND_LANES:, :])
                   + b_ref[...])
        for d in range(N_DEV):
            mod_s[8 * d:8 * d + 8, :] = jnp.broadcast_to(mod_all[d:d + 1, :], (8, ADA_COLS))

        mine = pltpu.make_async_copy(rows(mod_s, x, y, c), mod_ref.at[2 * x + y], local_sem)
        mine.start()
        sends = []
        for m in range(1, N_CHIPS):
            peer = _flip(x, y, c, 2 * m)
            sends.append(_remote(rows(mod_s, *peer), mod_ref.at[2 * x + y], ms, mr, m - 1, peer))
        for cp in sends:
            cp.start()
        for m in range(1, N_CHIPS):
            px, py, pc = _flip(x, y, c, 2 * m)
            _remote(rows(mod_s, x, y, c), mod_ref.at[2 * px + py], ms, mr, m - 1, (px, py, pc)).wait_recv()
        for cp in sends:
            cp.wait_send()
        mine.wait()
        finish_w()

    return pl.pallas_call(
        body, name="prologue",
        out_shape=[jax.ShapeDtypeStruct((8 * N_DEV, COND_LANES), f32), jax.ShapeDtypeStruct((N_CHIPS, 8, ADA_COLS), f32),
                   jax.ShapeDtypeStruct((N_CHIPS, R, D_MODEL), bf16)],
        in_specs=[_VMEM_SPEC, _VMEM_SPEC, _VMEM_SPEC, _ANY_SPEC], out_specs=[_VMEM_SPEC, _VMEM_SPEC, _ANY_SPEC],
        scratch_shapes=[pltpu.VMEM((8 * N_DEV, ADA_COLS), f32), pltpu.VMEM((STAGE_ROWS, D_MODEL), bf16)]
        + _sems(N_DEV - 1) + _sems(N_CHIPS - 1) + _sems(6) + [pltpu.SemaphoreType.DMA],
        compiler_params=pltpu.CompilerParams(vmem_limit_bytes=VMEM_LIMIT),
    )(cond, ada_w, ada_b, blob)


def _ada_bwd(c_all, dmod, w, m, v):
    def body(c_ref, d_ref, w_ref, m_ref, v_ref, g_ref, dl_ref, m2_ref, v2_ref):
        cv = c_ref[...]
        g = lax.dot_general(cv * _sigmoid(cv), d_ref[...], (((0,), (0,)), ((), ())), preferred_element_type=f32,
                            precision=lax.Precision.HIGHEST)
        g_ref[...] = g
        dl_ref[...], m2_ref[...], v2_ref[...] = _adamw_math(w_ref[...], g, m_ref[...], v_ref[...])

    wspec = pl.BlockSpec((D_MODEL, ADA_TN), lambda j: (0, j))
    return pl.pallas_call(
        body, name="ada_bwd", grid=(ADA_COLS // ADA_TN,),
        in_specs=[_full((N_DEV, D_MODEL)), pl.BlockSpec((N_DEV, ADA_TN), lambda j: (0, j)), wspec, wspec, wspec],
        out_specs=[wspec] * 4, out_shape=[jax.ShapeDtypeStruct((D_MODEL, ADA_COLS), f32)] * 4,
        compiler_params=_cparams("parallel"),
    )(c_all, dmod, w, m, v)


SMALL_SLOTS = (("ada_b", 6144), ("ln_in_g", 1024), ("ln_in_b", 1024), ("conv_b", 1536), ("dt_bias", 128), ("a_log", 128),
               ("d_skip", 128), ("ssd_norm_w", 1024), ("attn_sinks", 128), ("ln1_g", 1024), ("ln1_b", 1024),
               ("b_ff1", 4096), ("b_ff2", 1024), ("ln2_g", 1024), ("ln2_b", 1024), ("conv_w", 6144), ("loss", 1024))
SMALL_N = sum(n for _, n in SMALL_SLOTS)
SMALL_OFF = {name: sum(n for _, n in SMALL_SLOTS[:i]) for i, (name, _) in enumerate(SMALL_SLOTS)}
SMALL_PARAMS = tuple(name for name, _ in SMALL_SLOTS[:15])
assert SMALL_N % 1024 == 0


def _small_pack(acc_in, acc_out, acc_mlp, db1, acc_conv, acc_ssd, dd_x, dsink, alog, e_mat):
    def body(in_ref, out_ref, mlp_ref, db1_ref, conv_ref, ssd_ref, dd_ref, sink_ref, al_ref, e_ref, o_ref):
        def put(name, val, at=0):
            off = SMALL_OFF[name] + at
            o_ref[:, off:off + val.shape[1]] = val

        for k, row in enumerate((in_ref[1:2, :], in_ref[0:1, :], out_ref[0:1, :], mlp_ref[A_SH2:A_SH2 + 1, :],
                                 mlp_ref[A_SC2:A_SC2 + 1, :], mlp_ref[A_G2:A_G2 + 1, :])):
            put("ada_b", row, D_MODEL * k)
        put("ln_in_g", in_ref[2:3, :])
        put("ln_in_b", in_ref[3:4, :])
        put("conv_b", conv_ref[4:5, :])
        put("dt_bias", ssd_ref[1:2, :])
        put("a_log", ssd_ref[0:1, :] * (-jnp.exp(al_ref[...])))
        put("d_skip", _dot_sel_nt(jnp.broadcast_to(dd_ref[...], (8, SSD_WIDTH)), e_ref[...])[0:1, :])
        put("ssd_norm_w", out_ref[1:2, :])
        put("attn_sinks", sink_ref[...])
        put("ln1_g", mlp_ref[A_LN1G:A_LN1G + 1, :])
        put("ln1_b", mlp_ref[A_LN1B:A_LN1B + 1, :])
        put("b_ff1", db1_ref[...])
        put("b_ff2", mlp_ref[A_B2:A_B2 + 1, :])
        put("ln2_g", mlp_ref[A_LN2G:A_LN2G + 1, :])
        put("ln2_b", mlp_ref[A_LN2B:A_LN2B + 1, :])
        for k in range(CONV_K):
            put("conv_w", conv_ref[k:k + 1, :], CONV_DIM * k)
        put("loss", mlp_ref[A_LOSS:A_LOSS + 1, :])

    return pl.pallas_call(body, name="small_pack", out_shape=jax.ShapeDtypeStruct((1, SMALL_N), f32),
                          compiler_params=_cparams())(acc_in, acc_out, acc_mlp, db1, acc_conv, acc_ssd, dd_x, dsink, alog, e_mat)


def _small_update(gathered, params, moms, vels):
    k = len(SMALL_PARAMS)

    def body(g_ref, *refs):
        w_refs, m_refs, v_refs, outs = refs[:k], refs[k:2 * k], refs[2 * k:3 * k], refs[3 * k:]

        def total(name, n):
            off = SMALL_OFF[name]
            g = g_ref[0:1, off:off + n]
            for i in range(1, N_DEV):
                g = g + g_ref[i:i + 1, off:off + n]
            return g

        for j, name in enumerate(SMALL_PARAMS):
            n = w_refs[j].shape[1]
            g = total(name, max(n, 128))[:, :n]
            outs[4 * j][...] = g
            outs[4 * j + 1][...], outs[4 * j + 2][...], outs[4 * j + 3][...] = _adamw_math(
                w_refs[j][...], g, m_refs[j][...], v_refs[j][...])
        outs[4 * k][...] = total("conv_w", CONV_K * CONV_DIM)
        outs[4 * k + 1][...] = total("loss", D_MODEL)

    shapes = [jax.ShapeDtypeStruct(p.shape, f32) for p in params for _ in range(4)]
    shapes += [jax.ShapeDtypeStruct((1, CONV_K * CONV_DIM), f32), jax.ShapeDtypeStruct((1, D_MODEL), f32)]
    return pl.pallas_call(body, name="small_update", out_shape=shapes,
                          compiler_params=_cparams())(gathered, *params, *moms, *vels)


def _place():
    return lax.axis_index("x"), lax.axis_index("y"), lax.axis_index("c")


def _flip(x, y, c, m):
    return (1 - x if m & 4 else x, 1 - y if m & 2 else y, 1 - c if m & 1 else c)


_VMEM_SPEC = pl.BlockSpec(memory_space=pltpu.VMEM)
_ANY_SPEC = pl.BlockSpec(memory_space=pl.ANY)


def _allgather8(v, name):
    n = v.shape[1]

    def body(v_ref, out_ref, send_sems, recv_sems, local_sem):
        x, y, c = _place()

        def rows(px, py, pc):
            return out_ref.at[pl.ds(pl.multiple_of((4 * px + 2 * py + pc) * 8, 8), 8), :]

        def copy(m, src, dst, to):
            return pltpu.make_async_remote_copy(src_ref=src, dst_ref=dst, send_sem=send_sems.at[m - 1],
                                                recv_sem=recv_sems.at[m - 1], device_id=to, device_id_type=MESH)

        mine = pltpu.make_async_copy(v_ref, rows(x, y, c), local_sem)
        mine.start()
        sends = [copy(m, v_ref, rows(x, y, c), _flip(x, y, c, m)) for m in range(1, N_DEV)]
        for cp in sends:
            cp.start()
        for m in range(1, N_DEV):
            peer = _flip(x, y, c, m)
            copy(m, v_ref, rows(*peer), peer).wait_recv()
        for cp in sends:
            cp.wait_send()
        mine.wait()

    return pl.pallas_call(
        body, name=name, out_shape=jax.ShapeDtypeStruct((8 * N_DEV, n), f32), in_specs=[_VMEM_SPEC],
        out_specs=_VMEM_SPEC,
        scratch_shapes=[pltpu.SemaphoreType.DMA((N_DEV - 1,)), pltpu.SemaphoreType.DMA((N_DEV - 1,)),
                        pltpu.SemaphoreType.DMA],
    )(v)


def _remote(src, dst, send_sems, recv_sems, k, to):
    return pltpu.make_async_remote_copy(src_ref=src, dst_ref=dst, send_sem=send_sems.at[k], recv_sem=recv_sems.at[k],
                                        device_id=to, device_id_type=MESH)


def _gather_job(blob_ref, out_ref, send_sems, recv_sems, R):
    x, y, c = _place()
    sib = (x, y, 1 - c)
    hr = R // 2

    def half(px, py, pc):
        return out_ref.at[2 * px + py, pl.ds(pl.multiple_of(pc * hr, 16), hr), :]

    my_half = blob_ref.at[pl.ds(pl.multiple_of(c * hr, 16), hr), :]

    def first():
        return [_remote(my_half, half(x, y, c), send_sems, recv_sems, m - 1, _flip(x, y, c, 2 * m))
                for m in range(1, N_CHIPS)]

    def start():
        for cp in first():
            cp.start()

    def finish():
        passed = []
        for m in range(1, N_CHIPS):
            px, py, pc = _flip(x, y, c, 2 * m)
            _remote(my_half, half(px, py, pc), send_sems, recv_sems, m - 1, (px, py, pc)).wait_recv()
            fwd = _remote(half(px, py, pc), half(px, py, pc), send_sems, recv_sems, 2 + m, sib)
            fwd.start()
            passed.append(fwd)
        for m in range(1, N_CHIPS):
            px, py, pc = _flip(x, y, c, 2 * m)
            _remote(my_half, half(px, py, 1 - pc), send_sems, recv_sems, 2 + m, sib).wait_recv()
        for cp in first() + passed:
            cp.wait_send()

    return start, finish


def _to_sibling_job(g_ref, out_ref, send_sems, recv_sems):
    x, y, c = _place()

    def cps():
        return [_remote(g_ref.at[j, 1 - c], out_ref.at[j], send_sems, recv_sems, j, (x, y, 1 - c)) for j in range(N_CHIPS)]

    def start():
        for cp in cps():
            cp.start()

    def wait():
        for cp in cps():
            cp.wait()

    return start, wait


def _to_chips_job(p_ref, out_ref, send_sems, recv_sems):
    x, y, c = _place()

    def cps():
        out = []
        for m in range(1, N_CHIPS):
            px, py, pc = _flip(x, y, c, 2 * m)
            out.append(_remote(p_ref.at[2 * px + py], out_ref.at[m - 1], send_sems, recv_sems, m - 1, (px, py, pc)))
        return out

    def start():
        for cp in cps():
            cp.start()

    def wait():
        for cp in cps():
            cp.wait()

    return start, wait


def _share_job(g_ref, send_sems, recv_sems, R):
    x, y, c = _place()

    def rows(pc):
        return g_ref.at[pl.ds(pl.multiple_of(pc * (R // 2), 8), R // 2), :]

    def start():
        _remote(rows(c), rows(c), send_sems, recv_sems, 0, (x, y, 1 - c)).start()

    def wait():
        _remote(rows(c), rows(1 - c), send_sems, recv_sems, 0, (x, y, 1 - c)).wait_recv()
        _remote(rows(c), rows(c), send_sems, recv_sems, 0, (x, y, 1 - c)).wait_send()

    return start, wait


def _sems(n):
    return [pltpu.SemaphoreType.DMA((n,)), pltpu.SemaphoreType.DMA((n,))]


def _rs_to_sibling(gb):
    def body(g_ref, out_ref, send_sems, recv_sems):
        start, wait = _to_sibling_job(g_ref, out_ref, send_sems, recv_sems)
        start()
        wait()

    return pl.pallas_call(
        body, name="rs_to_sibling", out_shape=jax.ShapeDtypeStruct((N_CHIPS,) + gb.shape[2:], bf16),
        in_specs=[_ANY_SPEC], out_specs=_ANY_SPEC, scratch_shapes=_sems(N_CHIPS),
    )(gb)


def _rs_share(g):
    R = g.shape[0]

    def body(g_ref, out_ref, send_sems, recv_sems):
        start, wait = _share_job(out_ref, send_sems, recv_sems, R)
        start()
        wait()

    return pl.pallas_call(
        body, name="rs_share", out_shape=jax.ShapeDtypeStruct(g.shape, f32), in_specs=[_ANY_SPEC],
        out_specs=_ANY_SPEC, input_output_aliases={0: 0}, scratch_shapes=_sems(1),
    )(g)


RS_TR_MAX = 512


def _rs_sum_pair(place, gb, recv, name):
    hr = gb.shape[2]
    tr = min(hr, RS_TR_MAX)

    def body(pl_ref, g_ref, r_ref, o_ref):
        o_ref[0] = (g_ref[0, 0].astype(f32) + r_ref[0].astype(f32)).astype(bf16)

    return pl.pallas_call(
        body, name=name,
        grid_spec=pltpu.PrefetchScalarGridSpec(
            num_scalar_prefetch=1, grid=(N_CHIPS, hr // tr),
            in_specs=[pl.BlockSpec((1, 1, tr, D_MODEL), lambda j, i, p: (j, p[0], i, 0)),
                      pl.BlockSpec((1, tr, D_MODEL), lambda j, i, p: (j, i, 0))],
            out_specs=pl.BlockSpec((1, tr, D_MODEL), lambda j, i, p: (j, i, 0))),
        out_shape=jax.ShapeDtypeStruct((N_CHIPS, hr, D_MODEL), bf16),
        compiler_params=_cparams("parallel", "parallel"),
    )(place, gb, recv)


def _rs_sum_chips(place, gb, recv_sib, recv_chips, name):
    hr = gb.shape[2]
    tr = min(hr, RS_TR_MAX)
    nt = hr // tr

    def body(pl_ref, g_ref, r1_ref, r2_ref, o_ref):
        acc = g_ref[0, 0].astype(f32) + r1_ref[0].astype(f32)
        for k in range(N_CHIPS - 1):
            acc = acc + r2_ref[k].astype(f32)
        o_ref[...] = acc

    return pl.pallas_call(
        body, name=name,
        grid_spec=pltpu.PrefetchScalarGridSpec(
            num_scalar_prefetch=1, grid=(nt,),
            in_specs=[pl.BlockSpec((1, 1, tr, D_MODEL), lambda i, p: (p[1], p[0], i, 0)),
                      pl.BlockSpec((1, tr, D_MODEL), lambda i, p: (p[1], i, 0)),
                      pl.BlockSpec((N_CHIPS - 1, tr, D_MODEL), lambda i, p: (0, i, 0))],
            out_specs=pl.BlockSpec((tr, D_MODEL), lambda i, p: (p[0] * nt + i, 0))),
        out_shape=jax.ShapeDtypeStruct((2 * hr, D_MODEL), f32),
        compiler_params=_cparams("parallel"),
    )(place, gb, recv_sib, recv_chips)


def _pad128(v):
    v = v.reshape(1, -1)
    return jnp.pad(v, ((0, 0), (0, 128 - v.shape[1])))


def _row(v):
    return v.reshape(1, -1)


W_COLS = PROJ_WIDTH // N_CHIPS


def _g_in_blocks(gz, gxbc, gdt, gq, gkv):
    g = jnp.concatenate([gz, gxbc, gdt[:W_DT_ROWS], gq, gkv], axis=0)
    return jnp.pad(g.reshape(N_CHIPS, W_COLS, D_MODEL), ((0, 0), (0, D_MODEL - W_COLS), (0, 0)))


def kernel(x, c, ln_in_g, ln_in_b, ada_w, ada_b, w_in, conv_w, conv_b, dt_bias, a_log, d_skip, ssd_norm_w, attn_sinks, w_out, ln1_g, ln1_b, w_ff1, b_ff1, w_ff2, b_ff2, ln2_g, ln2_b, loss_target, m_ln_in_g, m_ln_in_b, m_ada_w, m_ada_b, m_w_in, m_conv_w, m_conv_b, m_dt_bias, m_a_log, m_d_skip, m_ssd_norm_w, m_attn_sinks, m_w_out, m_ln1_g, m_ln1_b, m_w_ff1, m_b_ff1, m_w_ff2, m_b_ff2, m_ln2_g, m_ln2_b, v_ln_in_g, v_ln_in_b, v_ada_w, v_ada_b, v_w_in, v_conv_w, v_conv_b, v_dt_bias, v_a_log, v_d_skip, v_ssd_norm_w, v_attn_sinks, v_w_out, v_ln1_g, v_ln1_b, v_w_ff1, v_b_ff1, v_w_ff2, v_b_ff2, v_ln2_g, v_ln2_b):
    xi, yi, ci = _place()
    chip = 2 * xi + yi
    place = jnp.stack([ci, chip]).astype(jnp.int32)
    x2, tgt = x[0], loss_target[0]

    def as_rows(a):
        return jnp.transpose(a, (2, 0, 1))

    def from_rows(a):
        return jnp.transpose(a, (1, 2, 0))

    cond = jnp.concatenate([c.reshape(2, COND_LANES), conv_w.reshape(3, COND_LANES), jnp.zeros((3, COND_LANES), f32)], axis=0)
    ada_b_mine = lax.dynamic_slice(ada_b, (0, chip * ADA_COLS), (1, ADA_COLS))
    blob_in = jnp.pad(w_in[0].T, ((0, D_MODEL - W_COLS), (0, 0))).astype(bf16)
    cond_all, mod_rows, wall_in = _prologue(cond, ada_w[0], ada_b_mine, blob_in)
    cond_all = cond_all.reshape(N_DEV, 8, COND_LANES)
    c_all = cond_all[:, 0:2].reshape(N_DEV, D_MODEL)
    conv_w_full = jnp.concatenate([cond_all[2 * j, 2:5].reshape(CONV_K, 384) for j in range(N_CHIPS)], axis=1)
    mod = jnp.concatenate([mod_rows[:, 0].reshape(6, D_MODEL), jnp.zeros((2, D_MODEL), f32)], axis=0)
    w_in_f = wall_in[:, :W_COLS].reshape(PROJ_WIDTH, D_MODEL)
    b_ffw = jnp.concatenate([w_ff1[0], w_ff2[0]], axis=0).astype(bf16)
    b_outw = w_out[0].astype(bf16)

    def with_mine(wall, mine):
        return lax.dynamic_update_slice(wall, mine[None], (chip, 0, 0))

    e_mat = _head_expand()
    dsk_x = jnp.repeat(d_skip[0], HEAD_DIM).reshape(1, SSD_WIDTH)
    dtb, alog = _pad128(dt_bias), _pad128(a_log)
    sinks = attn_sinks[0]
    lng, lnb = _row(ln_in_g), _row(ln_in_b)
    u1, z, xr, xc, q, kv, dtr = _inproj_fwd(x2, mod, lng, lnb, w_in_f, conv_w_full, conv_b)
    y, prev_all, wall_out = _ssd_fwd(xc, dtr, dtb, alog, dsk_x, b_outw)
    o, lse, wall_ff = _attn_fwd(q, kv, sinks, b_ffw)
    wall_ff, wall_out = with_mine(wall_ff, b_ffw), with_mine(wall_out, b_outw)
    yn, mix, r1 = _outproj_fwd(y, z, o, x2, mod, lng, lnb, ssd_norm_w, wall_out)

    dr1, u2, s_act, da, df, acc_mlp, db1 = _mlp_fwd_bwd(r1, tgt, mod, ln1_g, ln1_b, ln2_g, ln2_b, wall_ff, b_ff1, wall_ff,
                                                        b_ff2)
    ga = lax.empty((N_CHIPS, GA_ROWS, D_MODEL), bf16)
    ga = _wgrad_blob(ga, u2, da, "wgrad_ff1", lambda t, n: (n, t))
    ga = _wgrad_blob(ga, s_act, df, "wgrad_ff2", lambda t, n: (t // 2, 2 + t % 2))
    ga = ga.reshape(N_CHIPS, 2, GA_ROWS // 2, D_MODEL)
    dy, dz, do, dmix, acc_out, a_sib = _outproj_bwd(dr1, mix, y, z, mod, ssd_norm_w, wall_out, ga)
    gc = lax.empty((N_CHIPS, GC_ROWS, D_MODEL), bf16)
    gc = _wgrad_blob(gc, yn, dmix, "wgrad_out_y", lambda t, n: (t, 0))
    gc = _wgrad_blob(gc, o, dmix, "wgrad_out_o", lambda t, n: (2 + t, 0))
    gc = gc.reshape(N_CHIPS, 2, GC_ROWS // 2, D_MODEL)
    a_pair = _rs_sum_pair(place, ga, a_sib, "rs_sum_pair_a")
    dq, dkv, dsink, a_chips, c_sib = _attn_bwd(q, kv, do, lse, sinks, a_pair, gc)
    g_a = _rs_sum_chips(place, ga, a_sib, a_chips, "rs_sum_chips_a")
    c_pair = _rs_sum_pair(place, gc, c_sib, "rs_sum_pair_c")
    dxc, ddt, acc_ssd, dd_x, g_a, c_chips = _ssd_bwd(xc, dtr, dy, prev_all, dtb, alog, dsk_x, e_mat, g_a, c_pair)
    g_c = _rs_sum_chips(place, gc, c_sib, c_chips, "rs_sum_chips_c")
    dxr, acc_conv, g_c = _conv_bwd(dxc, xr, conv_w_full, conv_b, g_c)
    gb = _g_in_blocks(_wgrad(dz, u1, "wgrad_in_z"), _wgrad(dxr, u1, "wgrad_in_xbc"),
                      _wgrad(ddt.astype(bf16), u1, "wgrad_in_dt"), _wgrad(dq, u1, "wgrad_in_q"),
                      _wgrad(dkv, u1, "wgrad_in_kv")).reshape(N_CHIPS, 2, GB_ROWS // 2, D_MODEL)
    b_sib = _rs_to_sibling(gb)
    b_pair = _rs_sum_pair(place, gb, b_sib, "rs_sum_pair_b")
    grad_x, acc_in, b_chips = _inproj_bwd(dz, dxr, dq, dkv, ddt, dr1, x2, mod, lng, lnb, w_in_f, b_pair)
    g_b = _rs_share(_rs_sum_chips(place, gb, b_sib, b_chips, "rs_sum_chips_b"))

    packed = _small_pack(acc_in, acc_out, acc_mlp, db1, acc_conv, acc_ssd, dd_x, dsink, alog, e_mat)
    small_all = _allgather8(packed.reshape(8, SMALL_N // 8), "gather_small").reshape(N_DEV, SMALL_N)
    given = dict(ada_b=(ada_b, m_ada_b, v_ada_b), ln_in_g=(ln_in_g, m_ln_in_g, v_ln_in_g), ln_in_b=(ln_in_b, m_ln_in_b, v_ln_in_b),
                 conv_b=(conv_b, m_conv_b, v_conv_b), dt_bias=(dt_bias, m_dt_bias, v_dt_bias), a_log=(a_log, m_a_log, v_a_log),
                 d_skip=(d_skip, m_d_skip, v_d_skip), ssd_norm_w=(ssd_norm_w, m_ssd_norm_w, v_ssd_norm_w),
                 attn_sinks=(attn_sinks, m_attn_sinks, v_attn_sinks), ln1_g=(ln1_g, m_ln1_g, v_ln1_g),
                 ln1_b=(ln1_b, m_ln1_b, v_ln1_b), b_ff1=(b_ff1, m_b_ff1, v_b_ff1), b_ff2=(b_ff2, m_b_ff2, v_b_ff2),
                 ln2_g=(ln2_g, m_ln2_g, v_ln2_g), ln2_b=(ln2_b, m_ln2_b, v_ln2_b))
    upd = _small_update(small_all, *([_row(given[n][i]) for n in SMALL_PARAMS] for i in range(3)))
    small_res = {n: [t.reshape(given[n][0].shape) for t in upd[4 * j:4 * j + 4]] for j, n in enumerate(SMALL_PARAMS)}
    g_conv_all, loss_lanes = upd[4 * len(SMALL_PARAMS)], upd[4 * len(SMALL_PARAMS) + 1]
    loss = jnp.sum(loss_lanes)

    dmod_mine = lax.dynamic_slice(small_all[:, :6 * D_MODEL], (0, chip * ADA_COLS), (N_DEV, ADA_COLS))
    big = {"ada_w": [t[None] for t in _ada_bwd(c_all, dmod_mine, ada_w[0], m_ada_w[0], v_ada_w[0])]}

    g_conv = lax.dynamic_slice(g_conv_all.reshape(CONV_K, CONV_DIM), (0, chip * 384), (CONV_K, 384))
    big["conv_w"] = [t[None] for t in _adamw(conv_w[0], g_conv, m_conv_w[0], v_conv_w[0], "adamw_conv_w")]

    g_rows = g_b[:W_COLS].reshape(W_COLS, 1, D_MODEL)
    big["w_in"] = [from_rows(t) for t in (g_rows, *_adamw_rows(as_rows(w_in), g_rows, as_rows(m_w_in), as_rows(v_w_in)))]
    for name, g, row0, (w, m, v) in (("w_out", g_c, 0, (w_out, m_w_out, v_w_out)), ("w_ff1", g_a, 0, (w_ff1, m_w_ff1, v_w_ff1)),
                                     ("w_ff2", g_a, D_MODEL, (w_ff2, m_w_ff2, v_w_ff2))):
        big[name] = [t[None] for t in _adamw(w[0], g, m[0], v[0], "adamw_" + name, row0)]

    order = ("ln_in_g", "ln_in_b", "ada_w", "ada_b", "w_in", "conv_w", "conv_b", "dt_bias", "a_log", "d_skip", "ssd_norm_w",
             "attn_sinks", "w_out", "ln1_g", "ln1_b", "w_ff1", "b_ff1", "w_ff2", "b_ff2", "ln2_g", "ln2_b")
    res = {**small_res, **big}
    return (loss, grad_x[None], *[res[n][k] for k in range(4) for n in order])
```

```python
import numpy as np
import jax
import jax.numpy as jnp
from jax import lax
from jax.experimental import pallas as pl
from jax.experimental.pallas import tpu as pltpu

f32 = jnp.float32
bf16 = jnp.bfloat16

D_MODEL = 1024
SSD_WIDTH = 1024
SSD_HEADS = 16
HEAD_DIM = 64
SSD_STATE = 128
SSD_GROUPS = 2
CHUNK = 128
CONV_K = 4
CONV_DIM = 1536
ATTN_HEADS = 16
D_FF = 4096
PROJ_WIDTH = 3856
ALPHA = 2.0 ** 0.25
LN_EPS = 1e-5
RMS_EPS = 1e-5
ATTN_SCALE = HEAD_DIM ** -0.5
NEG = -1e30

ADAM_LR = 0.001
ADAM_B1 = 0.9
ADAM_B2 = 0.999
ADAM_EPS = 1e-08
ADAM_WD = 0.01
ADAM_STEP = 10

W_Z, W_XBC, W_Q, W_KV = slice(0, 1024), slice(1024, 2560), slice(2576, 3600), slice(3600, 3856)
W_DT = slice(2560, 2688)
W_DT_ROWS = 16
GA_ROWS = 2048
GB_ROWS = 1024
GC_ROWS = 512
WG_TM = 512
STAGE_ROWS = 512
ADAMW_BLOCK_ELEMS = 1 << 18
MLP_TM = 256
DENSE_TM = 512
N_CHIPS = 4
N_DEV = 8
VMEM_LIMIT = 56 * 1024 * 1024
MESH = pl.DeviceIdType.MESH

ALIBI_SLOPES = tuple(2.0 ** (-8.0 / ATTN_HEADS * (i + 1)) for i in range(ATTN_HEADS))


def _cparams(*sem):
    return pltpu.CompilerParams(dimension_semantics=sem, vmem_limit_bytes=VMEM_LIMIT)


def _sigmoid(x):
    return 1.0 / (1.0 + jnp.exp(-x))


def _softplus(x):
    return jnp.maximum(x, 0.0) + jnp.log1p(jnp.exp(-jnp.abs(x)))


def _ln_stats(x):
    mu = jnp.mean(x, axis=-1, keepdims=True)
    xc = x - mu
    var = jnp.mean(xc * xc, axis=-1, keepdims=True)
    rstd = lax.rsqrt(var + LN_EPS)
    return xc * rstd, rstd


def _ln_bwd(dy, xhat, rstd, g):
    dxh = dy * g
    m1 = jnp.mean(dxh, axis=-1, keepdims=True)
    m2 = jnp.mean(dxh * xhat, axis=-1, keepdims=True)
    return rstd * (dxh - m1 - xhat * m2)


def _dot(a, b):
    return jnp.dot(a, b, preferred_element_type=f32)


def _dot_nt(a, b):
    return lax.dot_general(a, b, (((1,), (1,)), ((), ())), preferred_element_type=f32)


def _dot_tn(a, b):
    return lax.dot_general(a, b, (((0,), (0,)), ((), ())), preferred_element_type=f32)


def _dot_exact(a, b):
    return jnp.dot(a, b, preferred_element_type=f32, precision=lax.Precision.HIGHEST)


def _split3(v):
    hi = v.astype(bf16)
    r1 = v - hi.astype(f32)
    mid = r1.astype(bf16)
    lo = (r1 - mid.astype(f32)).astype(bf16)
    return hi, mid, lo


def _sel_dot(sel, v):
    hi, mid, lo = _split3(v)
    return _dot(sel, hi) + _dot(sel, mid) + _dot(sel, lo)


def _dot_sel_nt(v, sel):
    hi = v.astype(bf16)
    mid = (v - hi.astype(f32)).astype(bf16)
    return _dot_nt(hi, sel) + _dot_nt(mid, sel)


def _expand_heads(v):
    lane = lax.broadcasted_iota(jnp.int32, (v.shape[0], 128), 1)
    blocks = [jnp.where(lane < HEAD_DIM, v[:, 2 * b:2 * b + 1], v[:, 2 * b + 1:2 * b + 2]) for b in range(SSD_HEADS // 2)]
    return jnp.concatenate(blocks, axis=1)


def _full(shape):
    nd = len(shape)
    return pl.BlockSpec(shape, lambda *_: (0,) * nd)


def _resident(shape):
    nd = len(shape)
    return pl.BlockSpec(shape, lambda *_: (0,) * nd, pipeline_mode=pl.Buffered(1))


def _rows(tm, n):
    return pl.BlockSpec((tm, n), lambda i: (i, 0))


def _wall_rows(k):
    return pl.BlockSpec((N_CHIPS, D_MODEL, D_MODEL), lambda *_: (0, k, 0), pipeline_mode=pl.Buffered(1))


def _inproj_fwd(x, mod, ln_g, ln_b, w_in_t, conv_w, conv_b):
    L = x.shape[0]
    tm = DENSE_TM
    nt = L // tm

    def body(x_ref, mod_ref, g_ref, b_ref, w_ref, cw_ref, cb_ref,
             u1_ref, z_ref, xr_ref, xc_ref, q_ref, kv_ref, dt_ref, halo, buf):
        @pl.when(pl.program_id(0) == 0)
        def _():
            halo[...] = jnp.zeros_like(halo)

        xhat, _ = _ln_stats(x_ref[...])
        h0 = xhat * g_ref[...] + b_ref[...]
        u1 = (h0 * (1.0 + mod_ref[1:2, :]) + mod_ref[0:1, :]).astype(bf16)
        u1_ref[...] = u1
        z_ref[...] = _dot_nt(u1, w_ref[W_Z, :])
        xr = _dot_nt(u1, w_ref[W_XBC, :])
        xr_ref[...] = xr
        q_ref[...] = _dot_nt(u1, w_ref[W_Q, :]).astype(bf16)
        kv_ref[...] = _dot_nt(u1, w_ref[W_KV, :]).astype(bf16)
        dt_ref[...] = _dot_nt(u1, w_ref[W_DT, :])
        buf[0:8, :] = halo[...]
        buf[8:8 + tm, :] = xr
        pre = cb_ref[...] + cw_ref[0:1, :] * buf[5:5 + tm, :]
        for k in range(1, CONV_K):
            pre = pre + cw_ref[k:k + 1, :] * buf[5 + k:5 + k + tm, :]
        xc_ref[...] = pre * _sigmoid(pre)
        halo[...] = xr[tm - 8:tm, :]

    return pl.pallas_call(
        body, name="inproj_fwd", grid=(nt,),
        in_specs=[_rows(tm, D_MODEL), _full((8, D_MODEL)), _full((1, D_MODEL)), _full((1, D_MODEL)),
                  _resident((PROJ_WIDTH, D_MODEL)), _full((CONV_K, CONV_DIM)), _full((1, CONV_DIM))],
        out_specs=[_rows(tm, D_MODEL), _rows(tm, D_MODEL), _rows(tm, CONV_DIM), _rows(tm, CONV_DIM),
                   _rows(tm, D_MODEL), _rows(tm, 256), _rows(tm, 128)],
        out_shape=[jax.ShapeDtypeStruct((L, D_MODEL), bf16), jax.ShapeDtypeStruct((L, D_MODEL), f32),
                   jax.ShapeDtypeStruct((L, CONV_DIM), f32), jax.ShapeDtypeStruct((L, CONV_DIM), f32),
                   jax.ShapeDtypeStruct((L, D_MODEL), bf16), jax.ShapeDtypeStruct((L, 256), bf16),
                   jax.ShapeDtypeStruct((L, 128), f32)],
        scratch_shapes=[pltpu.VMEM((8, CONV_DIM), f32), pltpu.VMEM((tm + 8, CONV_DIM), f32)],
        compiler_params=_cparams("arbitrary"),
    )(x, mod, ln_g, ln_b, w_in_t, conv_w, conv_b)


def _head_expand():
    e = np.zeros((128, SSD_WIDTH), np.float32)
    for h in range(SSD_HEADS):
        e[h, h * HEAD_DIM:(h + 1) * HEAD_DIM] = 1.0
    return jnp.asarray(e, dtype=bf16)


def _ssd_chunk_common(dt_raw, dtb, a_row):
    T = CHUNK
    lane = lax.broadcasted_iota(jnp.int32, (T, 128), 1)
    dt = jnp.where(lane < SSD_HEADS, _softplus(dt_raw + dtb), 0.0)
    a = dt * a_row
    r = lax.broadcasted_iota(jnp.int32, (T, T), 0)
    c = lax.broadcasted_iota(jnp.int32, (T, T), 1)
    tril = (c <= r).astype(bf16)
    cum = _sel_dot(tril, a)
    dtx = _expand_heads(dt)
    cumx = _expand_heads(cum)
    return dt, a, cum, dtx, cumx, r, c


def _ssd_fwd(xc, dt_raw, dt_bias, a_log, d_skip_x, blob):
    L = xc.shape[0]
    nc = L // CHUNK
    T = CHUNK
    R = blob.shape[0]

    def body(xc_ref, dt_ref, dtb_ref, al_ref, dsk_ref, blob_ref, y_ref, prev_ref, wall_ref, st, send_sems, recv_sems):
        start, finish = _gather_job(blob_ref, wall_ref, send_sems, recv_sems, R)

        @pl.when(pl.program_id(0) == 0)
        def _():
            st[...] = jnp.zeros_like(st)
            start()

        @pl.when(pl.program_id(0) == nc - 1)
        def _():
            finish()

        a_row = -jnp.exp(al_ref[...])
        lane1 = lax.broadcasted_iota(jnp.int32, (1, 128), 1)
        a_row = jnp.where(lane1 < SSD_HEADS, a_row, 0.0)
        dt, a, cum, dtx, cumx, r, c = _ssd_chunk_common(dt_ref[...], dtb_ref[...], a_row)
        cum_t = cum.T
        ex = jnp.exp(cumx)
        last = cumx[T - 1:T, :]
        wx = jnp.exp(last - cumx)
        cdx = jnp.exp(last)
        xs = xc_ref[:, 0:SSD_WIDTH]
        X = xs * dtx
        Xb = X.astype(bf16)
        Xd = (X * wx).astype(bf16)
        prev = st[...]
        prev_ref[0] = prev
        prevb = prev.astype(bf16)
        tri = c <= r
        lane = lax.broadcasted_iota(jnp.int32, (T, 128), 1)
        y_blocks = []
        new_states = []
        for g in range(SSD_GROUPS):
            Bg = xc_ref[:, 1024 + 128 * g:1152 + 128 * g].astype(bf16)
            Cg = xc_ref[:, 1280 + 128 * g:1408 + 128 * g].astype(bf16)
            G = _dot_nt(Cg, Bg)
            yoff = _dot(Cg, prevb[:, 512 * g:512 * (g + 1)])
            new_states.append(_dot_tn(Bg, Xd[:, 512 * g:512 * (g + 1)]))
            for j in range(4):
                blk = 4 * g + j
                Xblk = Xb[:, 128 * blk:128 * (blk + 1)]
                ys = []
                for half in range(2):
                    h = 2 * blk + half
                    seg = jnp.minimum(cum[:, h:h + 1] - cum_t[h:h + 1, :], 0.0)
                    M = jnp.where(tri, G * jnp.exp(seg), 0.0).astype(bf16)
                    ys.append(_dot(M, Xblk))
                yd = jnp.where(lane < HEAD_DIM, ys[0], ys[1])
                sl = slice(128 * blk, 128 * (blk + 1))
                y_blocks.append(yd + ex[:, sl] * yoff[:, 128 * j:128 * (j + 1)] + dsk_ref[:, sl] * xs[:, sl])
        y_ref[...] = jnp.concatenate(y_blocks, axis=1)
        st[...] = prev * cdx + jnp.concatenate(new_states, axis=1)

    return pl.pallas_call(
        body, name="ssd_fwd", grid=(nc,),
        in_specs=[_rows(T, CONV_DIM), _rows(T, 128), _full((1, 128)), _full((1, 128)), _full((1, SSD_WIDTH)), _ANY_SPEC],
        out_specs=[_rows(T, SSD_WIDTH), pl.BlockSpec((1, SSD_STATE, SSD_WIDTH), lambda i: (i, 0, 0)), _ANY_SPEC],
        out_shape=[jax.ShapeDtypeStruct((L, SSD_WIDTH), f32), jax.ShapeDtypeStruct((nc, SSD_STATE, SSD_WIDTH), f32),
                   jax.ShapeDtypeStruct((N_CHIPS, R, D_MODEL), bf16)],
        scratch_shapes=[pltpu.VMEM((SSD_STATE, SSD_WIDTH), f32)] + _sems(6),
        compiler_params=_cparams("arbitrary"),
    )(xc, dt_raw, dt_bias, a_log, d_skip_x, blob)


def _kv_halves(kv_prev, kv_cur, first):
    kv = jnp.concatenate([jnp.where(first, 0.0, kv_prev.astype(f32)), kv_cur.astype(f32)], axis=0)
    lane = lax.broadcasted_iota(jnp.int32, (2 * CHUNK, 128), 1)
    lo = lane < HEAD_DIM
    out = []
    for g in range(2):
        per_half = []
        for half in range(2):
            both = []
            for t in (kv[:, 0:128], kv[:, 128:256]):
                src = t if g == half else pltpu.roll(t, HEAD_DIM, 1)
                both.append(jnp.where(lo if half == 0 else ~lo, src, 0.0).astype(bf16))
            per_half.append(tuple(both))
        out.append(per_half)
    return out


def _attn_masks(first):
    r = lax.broadcasted_iota(jnp.int32, (CHUNK, 2 * CHUNK), 0)
    c = lax.broadcasted_iota(jnp.int32, (CHUNK, 2 * CHUNK), 1)
    dist = r + CHUNK - c
    valid = (dist >= 0) & (dist < CHUNK) & ((c >= CHUNK) | jnp.logical_not(first))
    return dist.astype(f32), valid


def _head_stack(g, half, sink_ref):
    blks = [4 * g + i for i in range(4)]
    heads = [2 * b + half for b in blks]
    slope = jnp.concatenate([jnp.full((CHUNK, 1), ALIBI_SLOPES[h], f32) for h in heads], axis=0)
    sink = jnp.concatenate([jnp.full((CHUNK, 1), sink_ref[h], f32) for h in heads], axis=0)
    return blks, heads, slope, sink


def _attn_fwd(q, kv, sinks, blob):
    L = q.shape[0]
    nb = L // CHUNK
    T = CHUNK
    R = blob.shape[0]

    def body(sink_ref, q_ref, kvp_ref, kvc_ref, blob_ref, o_ref, lse_ref, wall_ref, send_sems, recv_sems):
        first = pl.program_id(0) == 0
        start, finish = _gather_job(blob_ref, wall_ref, send_sems, recv_sems, R)

        @pl.when(first)
        def _():
            start()

        @pl.when(pl.program_id(0) == nb - 1)
        def _():
            finish()

        ext = _kv_halves(kvp_ref[...], kvc_ref[...], first)
        dist, valid = _attn_masks(first)
        lane = lax.broadcasted_iota(jnp.int32, (T, 128), 1)
        lse = jnp.zeros((T, 128), f32)
        o_blocks = []
        for blk in range(8):
            qb = q_ref[:, 128 * blk:128 * (blk + 1)]
            acc = None
            for half in range(2):
                h = 2 * blk + half
                k_ext, v_ext = ext[h // 8][half]
                s = _dot_nt(qb, k_ext) * ATTN_SCALE - ALIBI_SLOPES[h] * dist
                s = jnp.where(valid, s, NEG)
                sink = sink_ref[h]
                m = jnp.maximum(jnp.max(s, axis=-1, keepdims=True), sink)
                p = jnp.exp(s - m)
                den = jnp.sum(p, axis=-1, keepdims=True) + jnp.exp(sink - m)
                pn = (p * (1.0 / den)).astype(bf16)
                oh = _dot(pn, v_ext)
                acc = oh if acc is None else acc + oh
                lse = jnp.where(lane == h, m + jnp.log(den), lse)
            o_blocks.append(acc.astype(bf16))
        o_ref[...] = jnp.concatenate(o_blocks, axis=1)
        lse_ref[...] = lse

    return pl.pallas_call(
        body, name="attn_fwd", grid=(nb,),
        in_specs=[pl.BlockSpec(memory_space=pltpu.SMEM), _rows(T, D_MODEL),
                  pl.BlockSpec((T, 256), lambda i: (jnp.maximum(i - 1, 0), 0)), _rows(T, 256), _ANY_SPEC],
        out_specs=[_rows(T, D_MODEL), _rows(T, 128), _ANY_SPEC],
        out_shape=[jax.ShapeDtypeStruct((L, D_MODEL), bf16), jax.ShapeDtypeStruct((L, 128), f32),
                   jax.ShapeDtypeStruct((N_CHIPS, R, D_MODEL), bf16)],
        scratch_shapes=_sems(6),
        compiler_params=_cparams("arbitrary"),
    )(sinks, q, kv, kv, blob)


def _gated_norm(y, z, w):
    sz = _sigmoid(z)
    hg = y * (z * sz)
    ns, rss = [], []
    for g in range(SSD_GROUPS):
        hs = hg[:, 512 * g:512 * (g + 1)]
        rs = lax.rsqrt(jnp.mean(hs * hs, axis=-1, keepdims=True) + RMS_EPS)
        ns.append(hs * rs)
        rss.append(rs)
    n = jnp.concatenate(ns, axis=1)
    return n * w, n, rss, sz


def _outproj_fwd(y, z, o, x, mod, ln_g, ln_b, norm_w, w_out):
    L = x.shape[0]
    tm = DENSE_TM

    def body(y_ref, z_ref, o_ref, x_ref, mod_ref, g_ref, b_ref, nw_ref, w_ref, yn_ref, mix_ref, r1_ref):
        yn, _, _, _ = _gated_norm(y_ref[...], z_ref[...], nw_ref[...])
        ynb = yn.astype(bf16)
        yn_ref[...] = ynb
        mix = (_dot(ynb[:, 0:512], w_ref[0]) + _dot(ynb[:, 512:1024], w_ref[1])
               + _dot(o_ref[:, 0:512], w_ref[2]) + _dot(o_ref[:, 512:1024], w_ref[3]))
        mix_ref[...] = mix
        xhat, _ = _ln_stats(x_ref[...])
        h0 = xhat * g_ref[...] + b_ref[...]
        r1_ref[...] = ALPHA * h0 + (1.0 + mod_ref[2:3, :]) * mix

    v = _full((1, D_MODEL))
    return pl.pallas_call(
        body, name="outproj_fwd", grid=(L // tm,),
        in_specs=[_rows(tm, D_MODEL), _rows(tm, D_MODEL), _rows(tm, D_MODEL), _rows(tm, D_MODEL),
                  _full((8, D_MODEL)), v, v, v, _resident((N_CHIPS, 512, D_MODEL))],
        out_specs=[_rows(tm, D_MODEL)] * 3,
        out_shape=[jax.ShapeDtypeStruct((L, D_MODEL), bf16), jax.ShapeDtypeStruct((L, D_MODEL), f32),
                   jax.ShapeDtypeStruct((L, D_MODEL), f32)],
        compiler_params=_cparams("parallel"),
    )(y, z, o, x, mod, ln_g, ln_b, norm_w, w_out)


A_LN2G, A_LN2B, A_G2, A_B2, A_SC2, A_SH2, A_LN1G, A_LN1B, A_LOSS = range(9)


def _mlp_fwd_bwd(r1, target, mod, ln1_g, ln1_b, ln2_g, ln2_b, w1, b1, w2, b2):
    L = r1.shape[0]
    tm = MLP_TM
    nj = D_FF // 1024

    def body(r1_ref, t_ref, mod_ref, g1_ref, bb1_ref, g2_ref, bb2_ref, w1_ref, b1_ref, w2_ref, b2_ref,
             dr1_ref, u2_ref, s_ref, da_ref, df_ref, acc_ref, db1_ref, hr):
        @pl.when(pl.program_id(0) == 0)
        def _():
            acc_ref[...] = jnp.zeros_like(acc_ref)
            db1_ref[...] = jnp.zeros_like(db1_ref)

        sc2, sh2, gate2 = mod_ref[4:5, :], mod_ref[3:4, :], mod_ref[5:6, :]
        xhat1, rstd1 = _ln_stats(r1_ref[...])
        h1 = xhat1 * g1_ref[...] + bb1_ref[...]
        u2f = h1 * (1.0 + sc2) + sh2
        u2 = u2f.astype(bf16)
        u2_ref[...] = u2
        f = jnp.zeros((tm, D_MODEL), f32) + b2_ref[...]
        for j in range(nj):
            cs = slice(1024 * j, 1024 * (j + 1))
            a = _dot(u2, w1_ref[j]) + b1_ref[:, cs]
            hrj = jnp.maximum(a, 0.0)
            hr[:, cs] = hrj.astype(bf16)
            sj = (hrj * hrj).astype(bf16)
            s_ref[:, cs] = sj
            f = f + _dot(sj, w2_ref[j])
        r2 = ALPHA * h1 + (1.0 + gate2) * f
        xhat2, rstd2 = _ln_stats(r2)
        h2 = xhat2 * g2_ref[...] + bb2_ref[...]
        diff = h2 - t_ref[...]
        dh2 = diff * (1.0 / D_MODEL)

        def add(row, val):
            acc_ref[row:row + 1, :] += jnp.sum(val, axis=0, keepdims=True)

        add(A_LOSS, diff * diff * (0.5 / D_MODEL))
        add(A_LN2G, dh2 * xhat2)
        add(A_LN2B, dh2)
        dr2 = _ln_bwd(dh2, xhat2, rstd2, g2_ref[...])
        add(A_G2, dr2 * f)
        df = dr2 * (1.0 + gate2)
        add(A_B2, df)
        dfb = df.astype(bf16)
        df_ref[...] = dfb
        du2 = jnp.zeros((tm, D_MODEL), f32)
        for j in range(nj):
            cs = slice(1024 * j, 1024 * (j + 1))
            ds = _dot_nt(dfb, w2_ref[j])
            daj = ds * (2.0 * hr[:, cs].astype(f32))
            db1_ref[:, cs] += jnp.sum(daj, axis=0, keepdims=True)
            dajb = daj.astype(bf16)
            da_ref[:, cs] = dajb
            du2 = du2 + _dot_nt(dajb, w1_ref[j])
        add(A_SC2, du2 * h1)
        add(A_SH2, du2)
        dh1 = ALPHA * dr2 + du2 * (1.0 + sc2)
        add(A_LN1G, dh1 * xhat1)
        add(A_LN1B, dh1)
        dr1_ref[...] = _ln_bwd(dh1, xhat1, rstd1, g1_ref[...])

    v = _full((1, D_MODEL))
    return pl.pallas_call(
        body, name="mlp_fwd_bwd", grid=(L // tm,),
        in_specs=[_rows(tm, D_MODEL), _rows(tm, D_MODEL), _full((8, D_MODEL)), v, v, v, v,
                  _wall_rows(0), _full((1, D_FF)), _wall_rows(1), v],
        out_specs=[_rows(tm, D_MODEL), _rows(tm, D_MODEL), _rows(tm, D_FF), _rows(tm, D_FF), _rows(tm, D_MODEL),
                   _full((16, D_MODEL)), _full((1, D_FF))],
        out_shape=[jax.ShapeDtypeStruct((L, D_MODEL), f32), jax.ShapeDtypeStruct((L, D_MODEL), bf16),
                   jax.ShapeDtypeStruct((L, D_FF), bf16), jax.ShapeDtypeStruct((L, D_FF), bf16),
                   jax.ShapeDtypeStruct((L, D_MODEL), bf16), jax.ShapeDtypeStruct((16, D_MODEL), f32),
                   jax.ShapeDtypeStruct((1, D_FF), f32)],
        scratch_shapes=[pltpu.VMEM((tm, D_FF), bf16)],
        compiler_params=_cparams("arbitrary"),
    )(r1, target, mod, ln1_g, ln1_b, ln2_g, ln2_b, w1, b1, w2, b2)


def _wgrad(a, b, name):
    L, M = a.shape
    N = b.shape[1]
    tm = min(M, 512)
    tn = next(t for t in (1024, 768, 512, 256, 128) if N % t == 0)

    def body(a_ref, b_ref, o_ref):
        o_ref[...] = _dot_tn(a_ref[...], b_ref[...]).astype(bf16)

    return pl.pallas_call(
        body, name=name, grid=(M // tm, N // tn),
        in_specs=[pl.BlockSpec((L, tm), lambda i, j: (0, i)), pl.BlockSpec((L, tn), lambda i, j: (0, j))],
        out_specs=pl.BlockSpec((tm, tn), lambda i, j: (i, j)),
        out_shape=jax.ShapeDtypeStruct((M, N), bf16),
        compiler_params=_cparams("parallel", "parallel"),
    )(a, b)


def _wgrad_blob(blob, a, b, name, place_of):
    L, M = a.shape
    N = b.shape[1]

    def body(blob_ref, a_ref, b_ref, o_ref):
        o_ref[0] = _dot_tn(a_ref[...], b_ref[...]).astype(bf16)

    return pl.pallas_call(
        body, name=name, grid=(M // WG_TM, N // D_MODEL),
        in_specs=[pl.BlockSpec(memory_space=pl.ANY), pl.BlockSpec((L, WG_TM), lambda t, n: (0, t)),
                  pl.BlockSpec((L, D_MODEL), lambda t, n: (0, n))],
        out_specs=pl.BlockSpec((1, WG_TM, D_MODEL), lambda t, n: (*place_of(t, n), 0)),
        out_shape=jax.ShapeDtypeStruct(blob.shape, bf16), input_output_aliases={0: 0},
        compiler_params=_cparams("parallel", "parallel"),
    )(blob, a, b)


def _outproj_bwd(dr1, mix, y, z, mod, norm_w, w_out, gb):
    L = dr1.shape[0]
    tm = DENSE_TM
    nt = L // tm

    def body(dr1_ref, mix_ref, y_ref, z_ref, mod_ref, nw_ref, w_ref, gb_ref,
             dy_ref, dz_ref, do_ref, dmix_ref, acc_ref, sib_ref, send_sems, recv_sems):
        start, wait = _to_sibling_job(gb_ref, sib_ref, send_sems, recv_sems)

        @pl.when(pl.program_id(0) == 0)
        def _():
            acc_ref[...] = jnp.zeros_like(acc_ref)
            start()

        @pl.when(pl.program_id(0) == nt - 1)
        def _():
            wait()

        dr1 = dr1_ref[...]
        acc_ref[0:1, :] += jnp.sum(dr1 * mix_ref[...], axis=0, keepdims=True)
        dmix = (dr1 * (1.0 + mod_ref[2:3, :])).astype(bf16)
        dmix_ref[...] = dmix
        dyn = jnp.concatenate([_dot_nt(dmix, w_ref[0]), _dot_nt(dmix, w_ref[1])], axis=1)
        do_ref[...] = jnp.concatenate([_dot_nt(dmix, w_ref[2]), _dot_nt(dmix, w_ref[3])], axis=1).astype(bf16)
        yv, zv = y_ref[...], z_ref[...]
        _, n, rss, sz = _gated_norm(yv, zv, nw_ref[...])
        acc_ref[1:2, :] += jnp.sum(dyn * n, axis=0, keepdims=True)
        dn = dyn * nw_ref[...]
        parts = []
        for g in range(SSD_GROUPS):
            sl = slice(512 * g, 512 * (g + 1))
            dng, ng = dn[:, sl], n[:, sl]
            parts.append(rss[g] * (dng - ng * jnp.mean(dng * ng, axis=-1, keepdims=True)))
        dhg = jnp.concatenate(parts, axis=1)
        dy_ref[...] = dhg * (zv * sz)
        dz_ref[...] = (dhg * yv * (sz * (1.0 + zv * (1.0 - sz)))).astype(bf16)

    return pl.pallas_call(
        body, name="outproj_bwd", grid=(nt,),
        in_specs=[_rows(tm, D_MODEL)] * 4 + [_full((8, D_MODEL)), _full((1, D_MODEL)), _resident((N_CHIPS, 512, D_MODEL)),
                  _ANY_SPEC],
        out_specs=[_rows(tm, D_MODEL)] * 4 + [_full((8, D_MODEL)), _ANY_SPEC],
        out_shape=[jax.ShapeDtypeStruct((L, D_MODEL), f32)] + [jax.ShapeDtypeStruct((L, D_MODEL), bf16)] * 3
        + [jax.ShapeDtypeStruct((8, D_MODEL), f32), jax.ShapeDtypeStruct((N_CHIPS,) + gb.shape[2:], bf16)],
        scratch_shapes=_sems(N_CHIPS),
        compiler_params=_cparams("arbitrary"),
    )(dr1, mix, y, z, mod, norm_w, w_out, gb)


def _attn_bwd(q, kv, do, lse, sinks, pb, gb2):
    L = q.shape[0]
    nb = L // CHUNK
    T = CHUNK

    def body(sink_ref, q_ref, kvp_ref, kvc_ref, do_ref, lse_ref, pb_ref, gb2_ref, dq_ref, dkv_ref, dsink_ref, chips_ref,
             sib2_ref, carry, send_sems, recv_sems, send_sems2, recv_sems2):
        n = pl.program_id(0)
        start, wait = _to_chips_job(pb_ref, chips_ref, send_sems, recv_sems)
        start2, wait2 = _to_sibling_job(gb2_ref, sib2_ref, send_sems2, recv_sems2)

        @pl.when(n == 0)
        def _():
            carry[...] = jnp.zeros_like(carry)
            dsink_ref[...] = jnp.zeros_like(dsink_ref)
            start()
            start2()

        @pl.when(n < nb)
        def _():
            first = n == 0
            ext = _kv_halves(kvp_ref[...], kvc_ref[...], first)
            dist, valid = _attn_masks(first)
            dist4, valid4 = jnp.concatenate([dist] * 4, axis=0), jnp.concatenate([valid] * 4, axis=0)
            lane1 = lax.broadcasted_iota(jnp.int32, (1, 128), 1)
            lse = lse_ref[...]
            qts = [q_ref[:, 128 * b:128 * (b + 1)].astype(f32).T.astype(bf16) for b in range(8)]
            dots = [do_ref[:, 128 * b:128 * (b + 1)].astype(f32).T.astype(bf16) for b in range(8)]
            acck = [None, None]
            accv = [None, None]
            dsink = jnp.zeros((1, 128), f32)
            dq_acc = [None] * 8
            for g in range(2):
                for half in range(2):
                    k_ext, v_ext = ext[g][half]
                    blks, heads, slope, sink = _head_stack(g, half, sink_ref)
                    qs = jnp.concatenate([q_ref[:, 128 * b:128 * (b + 1)] for b in blks], axis=0)
                    dos = jnp.concatenate([do_ref[:, 128 * b:128 * (b + 1)] for b in blks], axis=0)
                    rows = slice(HEAD_DIM * half, HEAD_DIM * (half + 1))
                    qt = jnp.concatenate([qts[b][rows, :] for b in blks], axis=1)
                    dot_ = jnp.concatenate([dots[b][rows, :] for b in blks], axis=1)
                    lse_col = jnp.concatenate([lse[:, h:h + 1] for h in heads], axis=0)
                    s = _dot_nt(qs, k_ext) * ATTN_SCALE - slope * dist4
                    p = jnp.where(valid4, jnp.exp(s - lse_col), 0.0)
                    dp = _dot_nt(dos, v_ext)
                    delta = jnp.sum(p * dp, axis=-1, keepdims=True)
                    ds = (p * (dp - delta) * ATTN_SCALE).astype(bf16)
                    sd = jnp.exp(sink - lse_col) * delta
                    dqs = _dot(ds, k_ext)
                    for i, b in enumerate(blks):
                        seg = slice(T * i, T * (i + 1))
                        dq_acc[b] = dqs[seg, :] if dq_acc[b] is None else dq_acc[b] + dqs[seg, :]
                        dsink = dsink - jnp.where(lane1 == heads[i], jnp.sum(sd[seg, :], axis=0, keepdims=True), 0.0)
                    dk = _dot(qt, ds)
                    dv = _dot(dot_, p.astype(bf16))
                    acck[g] = dk if acck[g] is None else acck[g] + dk
                    accv[g] = dv if accv[g] is None else accv[g] + dv
            dq_ref[...] = jnp.concatenate([a.astype(bf16) for a in dq_acc], axis=1)
            dsink_ref[...] += dsink
            dkv = jnp.concatenate([jnp.concatenate(acck, axis=0).T, jnp.concatenate(accv, axis=0).T], axis=1)
            dkv_ref[...] = (carry[...] + dkv[0:T, :]).astype(bf16)
            carry[...] = dkv[T:2 * T, :]

        @pl.when(n == nb)
        def _():
            dkv_ref[...] = carry[...].astype(bf16)
            wait()
            wait2()

    cur = lambda i: (jnp.minimum(i, nb - 1), 0)
    return pl.pallas_call(
        body, name="attn_bwd", grid=(nb + 1,),
        in_specs=[pl.BlockSpec(memory_space=pltpu.SMEM), pl.BlockSpec((T, D_MODEL), cur),
                  pl.BlockSpec((T, 256), lambda i: (jnp.maximum(jnp.minimum(i, nb - 1) - 1, 0), 0)),
                  pl.BlockSpec((T, 256), cur), pl.BlockSpec((T, D_MODEL), cur), pl.BlockSpec((T, 128), cur), _ANY_SPEC,
                  _ANY_SPEC],
        out_specs=[pl.BlockSpec((T, D_MODEL), cur), pl.BlockSpec((T, 256), lambda i: (jnp.maximum(i - 1, 0), 0)),
                   _full((1, 128)), _ANY_SPEC, _ANY_SPEC],
        out_shape=[jax.ShapeDtypeStruct((L, D_MODEL), bf16), jax.ShapeDtypeStruct((L, 256), bf16),
                   jax.ShapeDtypeStruct((1, 128), f32), jax.ShapeDtypeStruct((N_CHIPS - 1,) + pb.shape[1:], bf16),
                   jax.ShapeDtypeStruct((N_CHIPS,) + gb2.shape[2:], bf16)],
        scratch_shapes=[pltpu.VMEM((T, 256), f32)] + _sems(N_CHIPS - 1) + _sems(N_CHIPS),
        compiler_params=_cparams("arbitrary"),
    )(sinks, q, kv, kv, do, lse, pb, gb2)


def _ssd_bwd(xc, dt_raw, dy, prev_all, dt_bias, a_log, d_skip_x, e_mat, g, pb2):
    L = xc.shape[0]
    nc = L // CHUNK
    T = CHUNK
    RG = g.shape[0]

    def body(xc_ref, dt_ref, dy_ref, prev_ref, dtb_ref, al_ref, dsk_ref, e_ref, g_in_ref, pb2_ref,
             dxc_ref, ddt_ref, acc_ref, dd_ref, g_ref, chips2_ref, dst, dxs_s, send_sems, recv_sems, send_sems2, recv_sems2):
        start, wait = _share_job(g_ref, send_sems, recv_sems, RG)
        start2, wait2 = _to_chips_job(pb2_ref, chips2_ref, send_sems2, recv_sems2)

        @pl.when(pl.program_id(0) == 0)
        def _():
            dst[...] = jnp.zeros_like(dst)
            acc_ref[...] = jnp.zeros_like(acc_ref)
            dd_ref[...] = jnp.zeros_like(dd_ref)
            start()
            start2()

        @pl.when(pl.program_id(0) == nc - 1)
        def _():
            wait()
            wait2()

        lane1 = lax.broadcasted_iota(jnp.int32, (1, 128), 1)
        a_row = jnp.where(lane1 < SSD_HEADS, -jnp.exp(al_ref[...]), 0.0)
        e_mat_v = e_ref[...]
        dt, a, cum, dtx, cumx, r, c = _ssd_chunk_common(dt_ref[...], dtb_ref[...], a_row)
        cum_t = cum.T
        ex = jnp.exp(cumx)
        last = cumx[T - 1:T, :]
        wx = jnp.exp(last - cumx)
        cdx = jnp.exp(last)
        xs = xc_ref[:, 0:SSD_WIDTH]
        X = xs * dtx
        Xb = X.astype(bf16)
        Xdb = (X * wx).astype(bf16)
        dyv = dy_ref[...]
        prev = prev_ref[0]
        prevb = prev.astype(bf16)
        dnew = dst[...]
        dnewb = dnew.astype(bf16)
        tri = c <= r
        lane = lax.broadcasted_iota(jnp.int32, (T, 128), 1)
        sub = lax.broadcasted_iota(jnp.int32, (128, T), 0)
        lo = lane < HEAD_DIM

        def red(vals, g):
            return _dot_sel_nt(vals, e_mat_v[:, 512 * g:512 * (g + 1)])

        de = jnp.zeros((T, 128), f32)
        dw = jnp.zeros((T, 128), f32)
        ddt_x = jnp.zeros((T, 128), f32)
        dcum_col = jnp.zeros((T, 128), f32)
        dcum_row = jnp.zeros((128, T), f32)
        dprev_parts, dBs, dCs = [], [], []
        for g in range(SSD_GROUPS):
            s5 = slice(512 * g, 512 * (g + 1))
            Bg = xc_ref[:, 1024 + 128 * g:1152 + 128 * g].astype(bf16)
            Cg = xc_ref[:, 1280 + 128 * g:1408 + 128 * g].astype(bf16)
            G = _dot_nt(Cg, Bg)
            Z = _dot(Cg, prevb[:, s5])
            dyg = dyv[:, s5]
            dZb = (dyg * ex[:, s5]).astype(bf16)
            dXd = _dot(Bg, dnewb[:, s5])
            dC = _dot_nt(dZb, prevb[:, s5])
            dB = _dot_nt(Xdb[:, s5], dnewb[:, s5])
            dprev_parts.append(_dot_tn(Cg, dZb) + dnew[:, s5] * cdx[:, s5])
            de = de + red(dyg * Z, g)
            dw = dw + red(dXd * X[:, s5], g)
            dXg = dXd * wx[:, s5]
            dG = jnp.zeros((T, T), f32)
            for j in range(4):
                blk = 4 * g + j
                sl = slice(128 * blk, 128 * (blk + 1))
                Xblk = Xb[:, sl]
                dyblk = dyv[:, sl]
                dyblk_b = dyblk.astype(bf16)
                dxh = []
                for half in range(2):
                    h = 2 * blk + half
                    seg = jnp.minimum(cum[:, h:h + 1] - cum_t[h:h + 1, :], 0.0)
                    Lm = jnp.where(tri, jnp.exp(seg), 0.0)
                    M = G * Lm
                    dyh = jnp.where(lo if half == 0 else ~lo, dyblk, 0.0).astype(bf16)
                    dM = _dot_nt(dyh, Xblk)
                    dG = dG + dM * Lm
                    Q = dM * M
                    dcum_col = dcum_col + jnp.where(lane == h, jnp.sum(Q, axis=1, keepdims=True), 0.0)
                    dcum_row = dcum_row + jnp.where(sub == h, jnp.sum(Q, axis=0, keepdims=True), 0.0)
                    dxh.append(_dot_tn(M.astype(bf16), dyblk_b))
                dXblk = dXg[:, 128 * j:128 * (j + 1)] + jnp.where(lo, dxh[0], dxh[1])
                xsb = xs[:, sl]
                dxs_s[:, sl] = dXblk * dtx[:, sl] + dsk_ref[:, sl] * dyblk
                ddt_x = ddt_x + _dot_sel_nt(dXblk * xsb, e_mat_v[:, sl])
                dd_ref[:, sl] += jnp.sum(dyblk * xsb, axis=0, keepdims=True)
            dGb = dG.astype(bf16)
            dCs.append(dC + _dot(dGb, Bg))
            dBs.append(dB + _dot_tn(dGb, Cg))
        e16 = jnp.exp(cum)
        cum_last = cum[T - 1:T, :]
        w16 = jnp.exp(cum_last - cum)
        dcd = jnp.sum(dnew * prev, axis=0, keepdims=True)
        dcd16 = red(dcd[:, 0:512], 0) + red(dcd[:, 512:1024], 1)
        dww = dw * w16
        extra = jnp.sum(dww, axis=0, keepdims=True) + dcd16 * jnp.exp(cum_last)
        rowi = lax.broadcasted_iota(jnp.int32, (T, 128), 0)
        dcum = dcum_col - dcum_row.T + de * e16 - dww + jnp.where(rowi == T - 1, extra, 0.0)
        da = _sel_dot((c >= r).astype(bf16), dcum)
        ddt = ddt_x + da * a_row
        acc_ref[0:1, :] += jnp.sum(da * dt, axis=0, keepdims=True)
        ddt_raw = jnp.where(lane < SSD_HEADS, ddt * _sigmoid(dt_ref[...] + dtb_ref[...]), 0.0)
        ddt_ref[...] = ddt_raw
        acc_ref[1:2, :] += jnp.sum(ddt_raw, axis=0, keepdims=True)
        dxc_ref[:, 0:SSD_WIDTH] = dxs_s[...]
        dxc_ref[:, 1024:1280] = jnp.concatenate(dBs, axis=1)
        dxc_ref[:, 1280:1536] = jnp.concatenate(dCs, axis=1)
        dst[...] = jnp.concatenate(dprev_parts, axis=1)

    rev = lambda i: (nc - 1 - i, 0)
    return pl.pallas_call(
        body, name="ssd_bwd", grid=(nc,),
        in_specs=[pl.BlockSpec((T, CONV_DIM), rev), pl.BlockSpec((T, 128), rev), pl.BlockSpec((T, SSD_WIDTH), rev),
                  pl.BlockSpec((1, SSD_STATE, SSD_WIDTH), lambda i: (nc - 1 - i, 0, 0)),
                  _full((1, 128)), _full((1, 128)), _full((1, SSD_WIDTH)), _full((128, SSD_WIDTH)), _ANY_SPEC, _ANY_SPEC],
        out_specs=[pl.BlockSpec((T, CONV_DIM), rev), pl.BlockSpec((T, 128), rev), _full((8, 128)),
                   _full((1, SSD_WIDTH)), _ANY_SPEC, _ANY_SPEC],
        out_shape=[jax.ShapeDtypeStruct((L, CONV_DIM), f32), jax.ShapeDtypeStruct((L, 128), f32),
                   jax.ShapeDtypeStruct((8, 128), f32), jax.ShapeDtypeStruct((1, SSD_WIDTH), f32),
                   jax.ShapeDtypeStruct(g.shape, f32), jax.ShapeDtypeStruct((N_CHIPS - 1,) + pb2.shape[1:], bf16)],
        input_output_aliases={8: 4},
        scratch_shapes=[pltpu.VMEM((SSD_STATE, SSD_WIDTH), f32), pltpu.VMEM((T, SSD_WIDTH), f32)] + _sems(1)
        + _sems(N_CHIPS - 1),
        compiler_params=_cparams("arbitrary"),
    )(xc, dt_raw, dy, prev_all, dt_bias, a_log, d_skip_x, e_mat, g, pb2)


def _conv_bwd(dxc, xr, conv_w, conv_b, g):
    L = dxc.shape[0]
    tm = 256
    nt = L // tm
    RG = g.shape[0]

    def body(dxc_ref, xr_ref, xh_ref, cw_ref, cb_ref, g_in_ref, dxr_ref, acc_ref, g_ref, carry, buf, ext, send_sems, recv_sems):
        i = pl.program_id(0)
        start, wait = _share_job(g_ref, send_sems, recv_sems, RG)

        @pl.when(i == 0)
        def _():
            carry[...] = jnp.zeros_like(carry)
            acc_ref[...] = jnp.zeros_like(acc_ref)
            ext[tm + 16:tm + CHUNK, :] = jnp.zeros((CHUNK - 16, CONV_DIM), bf16)
            start()

        @pl.when(i == nt - 1)
        def _():
            wait()

        buf[0:8, :] = jnp.where(i == nt - 1, 0.0, xh_ref[...])
        u = xr_ref[...]
        buf[8:8 + tm, :] = u
        pre = cb_ref[...] + cw_ref[CONV_K - 1:CONV_K, :] * u
        for k in range(CONV_K - 1):
            pre = pre + cw_ref[k:k + 1, :] * buf[5 + k:5 + k + tm, :]
        sg = _sigmoid(pre)
        dpre = dxc_ref[...] * (sg * (1.0 + pre * (1.0 - sg)))
        acc_ref[4:5, :] += jnp.sum(dpre, axis=0, keepdims=True)
        dpb = dpre.astype(bf16)
        ext[0:tm, :] = dpb
        ext[tm:tm + 16, :] = carry[...]
        acc_ref[CONV_K - 1:CONV_K, :] += jnp.sum(u * dpre, axis=0, keepdims=True)
        du = cw_ref[CONV_K - 1:CONV_K, :] * dpre
        r = lax.broadcasted_iota(jnp.int32, (CHUNK, 2 * CHUNK), 0)
        c = lax.broadcasted_iota(jnp.int32, (CHUNK, 2 * CHUNK), 1)
        for j in range(1, CONV_K):
            move = (c == r + j).astype(bf16)
            up = jnp.concatenate([_dot(move, ext[CHUNK * b:CHUNK * (b + 2), :]) for b in range(tm // CHUNK)], axis=0)
            k = CONV_K - 1 - j
            du = du + cw_ref[k:k + 1, :] * up
            acc_ref[k:k + 1, :] += jnp.sum(u * up, axis=0, keepdims=True)
        dxr_ref[...] = du.astype(bf16)
        carry[...] = dpb[0:16, :]

    rev = lambda i: (nt - 1 - i, 0)
    return pl.pallas_call(
        body, name="conv_bwd", grid=(nt,),
        in_specs=[pl.BlockSpec((tm, CONV_DIM), rev), pl.BlockSpec((tm, CONV_DIM), rev),
                  pl.BlockSpec((8, CONV_DIM), lambda i: (jnp.maximum((nt - 1 - i) * (tm // 8) - 1, 0), 0)),
                  _full((CONV_K, CONV_DIM)), _full((1, CONV_DIM)), _ANY_SPEC],
        out_specs=[pl.BlockSpec((tm, CONV_DIM), rev), _full((8, CONV_DIM)), _ANY_SPEC],
        out_shape=[jax.ShapeDtypeStruct((L, CONV_DIM), bf16), jax.ShapeDtypeStruct((8, CONV_DIM), f32),
                   jax.ShapeDtypeStruct(g.shape, f32)],
        input_output_aliases={5: 2},
        scratch_shapes=[pltpu.VMEM((16, CONV_DIM), bf16), pltpu.VMEM((tm + 8, CONV_DIM), f32),
                        pltpu.VMEM((tm + CHUNK, CONV_DIM), bf16)] + _sems(1),
        compiler_params=_cparams("arbitrary"),
    )(dxc, xr, xr, conv_w, conv_b, g)


def _inproj_bwd(dz, dxr, dq, dkv, ddt, dr1, x, mod, ln_g, ln_b, w_in, pb):
    L = x.shape[0]
    tm = DENSE_TM
    nt = L // tm

    def body(dz_ref, dxr_ref, dq_ref, dkv_ref, ddt_ref, dr1_ref, x_ref, mod_ref, g_ref, b_ref, w_ref, pb_ref,
             dx_ref, acc_ref, chips_ref, send_sems, recv_sems):
        start, wait = _to_chips_job(pb_ref, chips_ref, send_sems, recv_sems)

        @pl.when(pl.program_id(0) == 0)
        def _():
            acc_ref[...] = jnp.zeros_like(acc_ref)
            start()

        du1 = (_dot(dz_ref[...], w_ref[W_Z, :]) + _dot(dxr_ref[...], w_ref[W_XBC, :])
               + _dot(dq_ref[...], w_ref[W_Q, :]) + _dot(dkv_ref[...], w_ref[W_KV, :])
               + _dot(ddt_ref[...].astype(bf16), w_ref[W_DT, :]))
        xhat, rstd = _ln_stats(x_ref[...])
        h0 = xhat * g_ref[...] + b_ref[...]
        acc_ref[0:1, :] += jnp.sum(du1 * h0, axis=0, keepdims=True)
        acc_ref[1:2, :] += jnp.sum(du1, axis=0, keepdims=True)
        dh0 = du1 * (1.0 + mod_ref[1:2, :]) + ALPHA * dr1_ref[...]
        acc_ref[2:3, :] += jnp.sum(dh0 * xhat, axis=0, keepdims=True)
        acc_ref[3:4, :] += jnp.sum(dh0, axis=0, keepdims=True)
        dx_ref[...] = _ln_bwd(dh0, xhat, rstd, g_ref[...])

        @pl.when(pl.program_id(0) == nt - 1)
        def _():
            wait()

    v = _full((1, D_MODEL))
    return pl.pallas_call(
        body, name="inproj_bwd", grid=(nt,),
        in_specs=[_rows(tm, D_MODEL), _rows(tm, CONV_DIM), _rows(tm, D_MODEL), _rows(tm, 256), _rows(tm, 128),
                  _rows(tm, D_MODEL), _rows(tm, D_MODEL), _full((8, D_MODEL)), v, v, _resident((PROJ_WIDTH, D_MODEL)),
                  _ANY_SPEC],
        out_specs=[_rows(tm, D_MODEL), _full((8, D_MODEL)), _ANY_SPEC],
        out_shape=[jax.ShapeDtypeStruct((L, D_MODEL), f32), jax.ShapeDtypeStruct((8, D_MODEL), f32),
                   jax.ShapeDtypeStruct((N_CHIPS - 1,) + pb.shape[1:], bf16)],
        scratch_shapes=_sems(N_CHIPS - 1),
        compiler_params=_cparams("arbitrary"),
    )(dz, dxr, dq, dkv, ddt, dr1, x, mod, ln_g, ln_b, w_in, pb)


def _adamw_math(w, g, m, v):
    m = ADAM_B1 * m + (1.0 - ADAM_B1) * g
    v = ADAM_B2 * v + (1.0 - ADAM_B2) * (g * g)
    m_hat = m / (1.0 - ADAM_B1 ** ADAM_STEP)
    v_hat = v / (1.0 - ADAM_B2 ** ADAM_STEP)
    delta = -ADAM_LR * (m_hat / (jnp.sqrt(v_hat) + ADAM_EPS) + ADAM_WD * w)
    return delta, m, v


def _adamw(w, g, m, v, name, g_row0=0):
    R, C = w.shape

    def body(w_ref, g_ref, m_ref, v_ref, g2_ref, d_ref, m2_ref, v2_ref):
        gv = g_ref[...]
        g2_ref[...] = gv
        d_ref[...], m2_ref[...], v2_ref[...] = _adamw_math(w_ref[...], gv, m_ref[...], v_ref[...])

    cap = max(8, ADAMW_BLOCK_ELEMS // C)
    tr = R if R <= cap else next(t for t in range(cap - cap % 8, 7, -8) if R % t == 0)
    spec = pl.BlockSpec((tr, C), lambda i: (i, 0))
    g_spec = pl.BlockSpec((tr, C), lambda i: (g_row0 // tr + i, 0))
    return pl.pallas_call(
        body, name=name, grid=(R // tr,), in_specs=[spec, g_spec, spec, spec], out_specs=[spec] * 4,
        out_shape=[jax.ShapeDtypeStruct((R, C), f32)] * 4, compiler_params=_cparams("parallel"),
    )(w, g, m, v)


def _adamw_rows(w, g, m, v):
    R, _, C = w.shape
    tr = R // 4

    def body(w_ref, g_ref, m_ref, v_ref, d_ref, m2_ref, v2_ref):
        d_ref[...], m2_ref[...], v2_ref[...] = _adamw_math(w_ref[...], g_ref[...], m_ref[...], v_ref[...])

    spec = pl.BlockSpec((tr, 1, C), lambda i: (i, 0, 0))
    return pl.pallas_call(
        body, name="adamw_w_in", grid=(R // tr,), in_specs=[spec] * 4, out_specs=[spec] * 3,
        out_shape=[jax.ShapeDtypeStruct((R, 1, C), f32)] * 3, compiler_params=_cparams("parallel"),
    )(w, g, m, v)


ADA_COLS = 6 * D_MODEL // N_CHIPS
ADA_TN = 512


COND_LANES = 512


def _prologue(cond, ada_w, ada_b, blob):
    R = blob.shape[0]

    def body(cond_ref, w_ref, b_ref, blob_ref, call_ref, mod_ref, wall_ref, mod_s, stage, gs, gr, ms, mr, ws, wr, local_sem):
        x, y, c = _place()
        start_w, finish_w = _gather_job(blob_ref, wall_ref, ws, wr, R)

        def rows(ref, px, py, pc):
            return ref.at[pl.ds(pl.multiple_of((4 * px + 2 * py + pc) * 8, 8), 8), :]

        mine = pltpu.make_async_copy(cond_ref, rows(call_ref, x, y, c), local_sem)
        mine.start()
        sends = [_remote(cond_ref, rows(call_ref, x, y, c), gs, gr, m - 1, _flip(x, y, c, m)) for m in range(1, N_DEV)]
        for cp in sends:
            cp.start()
        start_w()
        for m in range(1, N_DEV):
            peer = _flip(x, y, c, m)
            _remote(cond_ref, rows(call_ref, *peer), gs, gr, m - 1, peer).wait_recv()
        for cp in sends:
            cp.wait_send()
        mine.wait()

        for k in range(R // STAGE_ROWS):
            part = pl.ds(STAGE_ROWS * k, STAGE_ROWS)
            cin = pltpu.make_async_copy(blob_ref.at[part, :], stage, local_sem)
            cin.start()
            cin.wait()
            cout = pltpu.make_async_copy(stage, wall_ref.at[2 * x + y, part, :], local_sem)
            cout.start()
            cout.wait()

        lo = jnp.concatenate([call_ref[8 * d:8 * d + 1, :] for d in range(N_DEV)], axis=0)
        hi = jnp.concatenate([call_ref[8 * d + 1:8 * d + 2, :] for d in range(N_DEV)], axis=0)
        mod_all = (_dot_exact(lo * _sigmoid(lo), w_ref[0:COND_LANES, :]) + _dot_exact(hi * _sigmoid(hi), w_ref[COND_LANES:, :])
                   + b_ref[...])
        for d in range(N_DEV):
            mod_s[8 * d:8 * d + 8, :] = jnp.broadcast_to(mod_all[d:d + 1, :], (8, ADA_COLS))

        mine = pltpu.make_async_copy(rows(mod_s, x, y, c), mod_ref.at[2 * x + y], local_sem)
        mine.start()
        sends = []
        for m in range(1, N_CHIPS):
            peer = _flip(x, y, c, 2 * m)
            sends.append(_remote(rows(mod_s, *peer), mod_ref.at[2 * x + y], ms, mr, m - 1, peer))
        for cp in sends:
            cp.start()
        for m in range(1, N_CHIPS):
            px, py, pc = _flip(x, y, c, 2 * m)
            _remote(rows(mod_s, x, y, c), mod_ref.at[2 * px + py], ms, mr, m - 1, (px, py, pc)).wait_recv()
        for cp in sends:
            cp.wait_send()
        mine.wait()
        finish_w()

    return pl.pallas_call(
        body, name="prologue",
        out_shape=[jax.ShapeDtypeStruct((8 * N_DEV, COND_LANES), f32), jax.ShapeDtypeStruct((N_CHIPS, 8, ADA_COLS), f32),
                   jax.ShapeDtypeStruct((N_CHIPS, R, D_MODEL), bf16)],
        in_specs=[_VMEM_SPEC, _VMEM_SPEC, _VMEM_SPEC, _ANY_SPEC], out_specs=[_VMEM_SPEC, _VMEM_SPEC, _ANY_SPEC],
        scratch_shapes=[pltpu.VMEM((8 * N_DEV, ADA_COLS), f32), pltpu.VMEM((STAGE_ROWS, D_MODEL), bf16)]
        + _sems(N_DEV - 1) + _sems(N_CHIPS - 1) + _sems(6) + [pltpu.SemaphoreType.DMA],
        compiler_params=pltpu.CompilerParams(vmem_limit_bytes=VMEM_LIMIT),
    )(cond, ada_w, ada_b, blob)


def _ada_bwd(c_all, dmod, w, m, v):
    def body(c_ref, d_ref, w_ref, m_ref, v_ref, g_ref, dl_ref, m2_ref, v2_ref):
        cv = c_ref[...]
        g = lax.dot_general(cv * _sigmoid(cv), d_ref[...], (((0,), (0,)), ((), ())), preferred_element_type=f32,
                            precision=lax.Precision.HIGHEST)
        g_ref[...] = g
        dl_ref[...], m2_ref[...], v2_ref[...] = _adamw_math(w_ref[...], g, m_ref[...], v_ref[...])

    wspec = pl.BlockSpec((D_MODEL, ADA_TN), lambda j: (0, j))
    return pl.pallas_call(
        body, name="ada_bwd", grid=(ADA_COLS // ADA_TN,),
        in_specs=[_full((N_DEV, D_MODEL)), pl.BlockSpec((N_DEV, ADA_TN), lambda j: (0, j)), wspec, wspec, wspec],
        out_specs=[wspec] * 4, out_shape=[jax.ShapeDtypeStruct((D_MODEL, ADA_COLS), f32)] * 4,
        compiler_params=_cparams("parallel"),
    )(c_all, dmod, w, m, v)


SMALL_SLOTS = (("ada_b", 6144), ("ln_in_g", 1024), ("ln_in_b", 1024), ("conv_b", 1536), ("dt_bias", 128), ("a_log", 128),
               ("d_skip", 128), ("ssd_norm_w", 1024), ("attn_sinks", 128), ("ln1_g", 1024), ("ln1_b", 1024),
               ("b_ff1", 4096), ("b_ff2", 1024), ("ln2_g", 1024), ("ln2_b", 1024), ("conv_w", 6144), ("loss", 1024))
SMALL_N = sum(n for _, n in SMALL_SLOTS)
SMALL_OFF = {name: sum(n for _, n in SMALL_SLOTS[:i]) for i, (name, _) in enumerate(SMALL_SLOTS)}
SMALL_PARAMS = tuple(name for name, _ in SMALL_SLOTS[:15])
assert SMALL_N % 1024 == 0


def _small_pack(acc_in, acc_out, acc_mlp, db1, acc_conv, acc_ssd, dd_x, dsink, alog, e_mat):
    def body(in_ref, out_ref, mlp_ref, db1_ref, conv_ref, ssd_ref, dd_ref, sink_ref, al_ref, e_ref, o_ref):
        def put(name, val, at=0):
            off = SMALL_OFF[name] + at
            o_ref[:, off:off + val.shape[1]] = val

        for k, row in enumerate((in_ref[1:2, :], in_ref[0:1, :], out_ref[0:1, :], mlp_ref[A_SH2:A_SH2 + 1, :],
                                 mlp_ref[A_SC2:A_SC2 + 1, :], mlp_ref[A_G2:A_G2 + 1, :])):
            put("ada_b", row, D_MODEL * k)
        put("ln_in_g", in_ref[2:3, :])
        put("ln_in_b", in_ref[3:4, :])
        put("conv_b", conv_ref[4:5, :])
        put("dt_bias", ssd_ref[1:2, :])
        put("a_log", ssd_ref[0:1, :] * (-jnp.exp(al_ref[...])))
        put("d_skip", _dot_sel_nt(jnp.broadcast_to(dd_ref[...], (8, SSD_WIDTH)), e_ref[...])[0:1, :])
        put("ssd_norm_w", out_ref[1:2, :])
        put("attn_sinks", sink_ref[...])
        put("ln1_g", mlp_ref[A_LN1G:A_LN1G + 1, :])
        put("ln1_b", mlp_ref[A_LN1B:A_LN1B + 1, :])
        put("b_ff1", db1_ref[...])
        put("b_ff2", mlp_ref[A_B2:A_B2 + 1, :])
        put("ln2_g", mlp_ref[A_LN2G:A_LN2G + 1, :])
        put("ln2_b", mlp_ref[A_LN2B:A_LN2B + 1, :])
        for k in range(CONV_K):
            put("conv_w", conv_ref[k:k + 1, :], CONV_DIM * k)
        put("loss", mlp_ref[A_LOSS:A_LOSS + 1, :])

    return pl.pallas_call(body, name="small_pack", out_shape=jax.ShapeDtypeStruct((1, SMALL_N), f32),
                          compiler_params=_cparams())(acc_in, acc_out, acc_mlp, db1, acc_conv, acc_ssd, dd_x, dsink, alog, e_mat)


def _small_update(gathered, params, moms, vels):
    k = len(SMALL_PARAMS)

    def body(g_ref, *refs):
        w_refs, m_refs, v_refs, outs = refs[:k], refs[k:2 * k], refs[2 * k:3 * k], refs[3 * k:]

        def total(name, n):
            off = SMALL_OFF[name]
            g = g_ref[0:1, off:off + n]
            for i in range(1, N_DEV):
                g = g + g_ref[i:i + 1, off:off + n]
            return g

        for j, name in enumerate(SMALL_PARAMS):
            n = w_refs[j].shape[1]
            g = total(name, max(n, 128))[:, :n]
            outs[4 * j][...] = g
            outs[4 * j + 1][...], outs[4 * j + 2][...], outs[4 * j + 3][...] = _adamw_math(
                w_refs[j][...], g, m_refs[j][...], v_refs[j][...])
        outs[4 * k][...] = total("conv_w", CONV_K * CONV_DIM)
        outs[4 * k + 1][...] = total("loss", D_MODEL)

    shapes = [jax.ShapeDtypeStruct(p.shape, f32) for p in params for _ in range(4)]
    shapes += [jax.ShapeDtypeStruct((1, CONV_K * CONV_DIM), f32), jax.ShapeDtypeStruct((1, D_MODEL), f32)]
    return pl.pallas_call(body, name="small_update", out_shape=shapes,
                          compiler_params=_cparams())(gathered, *params, *moms, *vels)


def _place():
    return lax.axis_index("x"), lax.axis_index("y"), lax.axis_index("c")


def _flip(x, y, c, m):
    return (1 - x if m & 4 else x, 1 - y if m & 2 else y, 1 - c if m & 1 else c)


_VMEM_SPEC = pl.BlockSpec(memory_space=pltpu.VMEM)
_ANY_SPEC = pl.BlockSpec(memory_space=pl.ANY)


def _allgather8(v, name):
    n = v.shape[1]

    def body(v_ref, out_ref, send_sems, recv_sems, local_sem):
        x, y, c = _place()

        def rows(px, py, pc):
            return out_ref.at[pl.ds(pl.multiple_of((4 * px + 2 * py + pc) * 8, 8), 8), :]

        def copy(m, src, dst, to):
            return pltpu.make_async_remote_copy(src_ref=src, dst_ref=dst, send_sem=send_sems.at[m - 1],
                                                recv_sem=recv_sems.at[m - 1], device_id=to, device_id_type=MESH)

        mine = pltpu.make_async_copy(v_ref, rows(x, y, c), local_sem)
        mine.start()
        sends = [copy(m, v_ref, rows(x, y, c), _flip(x, y, c, m)) for m in range(1, N_DEV)]
        for cp in sends:
            cp.start()
        for m in range(1, N_DEV):
            peer = _flip(x, y, c, m)
            copy(m, v_ref, rows(*peer), peer).wait_recv()
        for cp in sends:
            cp.wait_send()
        mine.wait()

    return pl.pallas_call(
        body, name=name, out_shape=jax.ShapeDtypeStruct((8 * N_DEV, n), f32), in_specs=[_VMEM_SPEC],
        out_specs=_VMEM_SPEC,
        scratch_shapes=[pltpu.SemaphoreType.DMA((N_DEV - 1,)), pltpu.SemaphoreType.DMA((N_DEV - 1,)),
                        pltpu.SemaphoreType.DMA],
    )(v)


def _remote(src, dst, send_sems, recv_sems, k, to):
    return pltpu.make_async_remote_copy(src_ref=src, dst_ref=dst, send_sem=send_sems.at[k], recv_sem=recv_sems.at[k],
                                        device_id=to, device_id_type=MESH)


def _gather_job(blob_ref, out_ref, send_sems, recv_sems, R):
    x, y, c = _place()
    sib = (x, y, 1 - c)
    hr = R // 2

    def half(px, py, pc):
        return out_ref.at[2 * px + py, pl.ds(pl.multiple_of(pc * hr, 16), hr), :]

    my_half = blob_ref.at[pl.ds(pl.multiple_of(c * hr, 16), hr), :]

    def first():
        return [_remote(my_half, half(x, y, c), send_sems, recv_sems, m - 1, _flip(x, y, c, 2 * m))
                for m in range(1, N_CHIPS)]

    def start():
        for cp in first():
            cp.start()

    def finish():
        passed = []
        for m in range(1, N_CHIPS):
            px, py, pc = _flip(x, y, c, 2 * m)
            _remote(my_half, half(px, py, pc), send_sems, recv_sems, m - 1, (px, py, pc)).wait_recv()
            fwd = _remote(half(px, py, pc), half(px, py, pc), send_sems, recv_sems, 2 + m, sib)
            fwd.start()
            passed.append(fwd)
        for m in range(1, N_CHIPS):
            px, py, pc = _flip(x, y, c, 2 * m)
            _remote(my_half, half(px, py, 1 - pc), send_sems, recv_sems, 2 + m, sib).wait_recv()
        for cp in first() + passed:
            cp.wait_send()

    return start, finish


def _to_sibling_job(g_ref, out_ref, send_sems, recv_sems):
    x, y, c = _place()

    def cps():
        return [_remote(g_ref.at[j, 1 - c], out_ref.at[j], send_sems, recv_sems, j, (x, y, 1 - c)) for j in range(N_CHIPS)]

    def start():
        for cp in cps():
            cp.start()

    def wait():
        for cp in cps():
            cp.wait()

    return start, wait


def _to_chips_job(p_ref, out_ref, send_sems, recv_sems):
    x, y, c = _place()

    def cps():
        out = []
        for m in range(1, N_CHIPS):
            px, py, pc = _flip(x, y, c, 2 * m)
            out.append(_remote(p_ref.at[2 * px + py], out_ref.at[m - 1], send_sems, recv_sems, m - 1, (px, py, pc)))
        return out

    def start():
        for cp in cps():
            cp.start()

    def wait():
        for cp in cps():
            cp.wait()

    return start, wait


def _share_job(g_ref, send_sems, recv_sems, R):
    x, y, c = _place()

    def rows(pc):
        return g_ref.at[pl.ds(pl.multiple_of(pc * (R // 2), 8), R // 2), :]

    def start():
        _remote(rows(c), rows(c), send_sems, recv_sems, 0, (x, y, 1 - c)).start()

    def wait():
        _remote(rows(c), rows(1 - c), send_sems, recv_sems, 0, (x, y, 1 - c)).wait_recv()
        _remote(rows(c), rows(c), send_sems, recv_sems, 0, (x, y, 1 - c)).wait_send()

    return start, wait


def _sems(n):
    return [pltpu.SemaphoreType.DMA((n,)), pltpu.SemaphoreType.DMA((n,))]


def _rs_to_sibling(gb):
    def body(g_ref, out_ref, send_sems, recv_sems):
        start, wait = _to_sibling_job(g_ref, out_ref, send_sems, recv_sems)
        start()
        wait()

    return pl.pallas_call(
        body, name="rs_to_sibling", out_shape=jax.ShapeDtypeStruct((N_CHIPS,) + gb.shape[2:], bf16),
        in_specs=[_ANY_SPEC], out_specs=_ANY_SPEC, scratch_shapes=_sems(N_CHIPS),
    )(gb)


def _rs_share(g):
    R = g.shape[0]

    def body(g_ref, out_ref, send_sems, recv_sems):
        start, wait = _share_job(out_ref, send_sems, recv_sems, R)
        start()
        wait()

    return pl.pallas_call(
        body, name="rs_share", out_shape=jax.ShapeDtypeStruct(g.shape, f32), in_specs=[_ANY_SPEC],
        out_specs=_ANY_SPEC, input_output_aliases={0: 0}, scratch_shapes=_sems(1),
    )(g)


RS_TR_MAX = 512


def _rs_sum_pair(place, gb, recv, name):
    hr = gb.shape[2]
    tr = min(hr, RS_TR_MAX)

    def body(pl_ref, g_ref, r_ref, o_ref):
        o_ref[0] = (g_ref[0, 0].astype(f32) + r_ref[0].astype(f32)).astype(bf16)

    return pl.pallas_call(
        body, name=name,
        grid_spec=pltpu.PrefetchScalarGridSpec(
            num_scalar_prefetch=1, grid=(N_CHIPS, hr // tr),
            in_specs=[pl.BlockSpec((1, 1, tr, D_MODEL), lambda j, i, p: (j, p[0], i, 0)),
                      pl.BlockSpec((1, tr, D_MODEL), lambda j, i, p: (j, i, 0))],
            out_specs=pl.BlockSpec((1, tr, D_MODEL), lambda j, i, p: (j, i, 0))),
        out_shape=jax.ShapeDtypeStruct((N_CHIPS, hr, D_MODEL), bf16),
        compiler_params=_cparams("parallel", "parallel"),
    )(place, gb, recv)


def _rs_sum_chips(place, gb, recv_sib, recv_chips, name):
    hr = gb.shape[2]
    tr = min(hr, RS_TR_MAX)
    nt = hr // tr

    def body(pl_ref, g_ref, r1_ref, r2_ref, o_ref):
        acc = g_ref[0, 0].astype(f32) + r1_ref[0].astype(f32)
        for k in range(N_CHIPS - 1):
            acc = acc + r2_ref[k].astype(f32)
        o_ref[...] = acc

    return pl.pallas_call(
        body, name=name,
        grid_spec=pltpu.PrefetchScalarGridSpec(
            num_scalar_prefetch=1, grid=(nt,),
            in_specs=[pl.BlockSpec((1, 1, tr, D_MODEL), lambda i, p: (p[1], p[0], i, 0)),
                      pl.BlockSpec((1, tr, D_MODEL), lambda i, p: (p[1], i, 0)),
                      pl.BlockSpec((N_CHIPS - 1, tr, D_MODEL), lambda i, p: (0, i, 0))],
            out_specs=pl.BlockSpec((tr, D_MODEL), lambda i, p: (p[0] * nt + i, 0))),
        out_shape=jax.ShapeDtypeStruct((2 * hr, D_MODEL), f32),
        compiler_params=_cparams("parallel"),
    )(place, gb, recv_sib, recv_chips)


def _pad128(v):
    v = v.reshape(1, -1)
    return jnp.pad(v, ((0, 0), (0, 128 - v.shape[1])))


def _row(v):
    return v.reshape(1, -1)


W_COLS = PROJ_WIDTH // N_CHIPS


def _g_in_blocks(gz, gxbc, gdt, gq, gkv):
    g = jnp.concatenate([gz, gxbc, gdt[:W_DT_ROWS], gq, gkv], axis=0)
    return jnp.pad(g.reshape(N_CHIPS, W_COLS, D_MODEL), ((0, 0), (0, D_MODEL - W_COLS), (0, 0)))


def kernel(x, c, ln_in_g, ln_in_b, ada_w, ada_b, w_in, conv_w, conv_b, dt_bias, a_log, d_skip, ssd_norm_w, attn_sinks, w_out, ln1_g, ln1_b, w_ff1, b_ff1, w_ff2, b_ff2, ln2_g, ln2_b, loss_target, m_ln_in_g, m_ln_in_b, m_ada_w, m_ada_b, m_w_in, m_conv_w, m_conv_b, m_dt_bias, m_a_log, m_d_skip, m_ssd_norm_w, m_attn_sinks, m_w_out, m_ln1_g, m_ln1_b, m_w_ff1, m_b_ff1, m_w_ff2, m_b_ff2, m_ln2_g, m_ln2_b, v_ln_in_g, v_ln_in_b, v_ada_w, v_ada_b, v_w_in, v_conv_w, v_conv_b, v_dt_bias, v_a_log, v_d_skip, v_ssd_norm_w, v_attn_sinks, v_w_out, v_ln1_g, v_ln1_b, v_w_ff1, v_b_ff1, v_w_ff2, v_b_ff2, v_ln2_g, v_ln2_b):
    xi, yi, ci = _place()
    chip = 2 * xi + yi
    place = jnp.stack([ci, chip]).astype(jnp.int32)
    x2, tgt = x[0], loss_target[0]

    def as_rows(a):
        return jnp.transpose(a, (2, 0, 1))

    def from_rows(a):
        return jnp.transpose(a, (1, 2, 0))

    cond = jnp.concatenate([c.reshape(2, COND_LANES), conv_w.reshape(3, COND_LANES), jnp.zeros((3, COND_LANES), f32)], axis=0)
    ada_b_mine = lax.dynamic_slice(ada_b, (0, chip * ADA_COLS), (1, ADA_COLS))
    blob_in = jnp.pad(w_in[0].T, ((0, D_MODEL - W_COLS), (0, 0))).astype(bf16)
    cond_all, mod_rows, wall_in = _prologue(cond, ada_w[0], ada_b_mine, blob_in)
    cond_all = cond_all.reshape(N_DEV, 8, COND_LANES)
    c_all = cond_all[:, 0:2].reshape(N_DEV, D_MODEL)
    conv_w_full = jnp.concatenate([cond_all[2 * j, 2:5].reshape(CONV_K, 384) for j in range(N_CHIPS)], axis=1)
    mod = jnp.concatenate([mod_rows[:, 0].reshape(6, D_MODEL), jnp.zeros((2, D_MODEL), f32)], axis=0)
    w_in_f = wall_in[:, :W_COLS].reshape(PROJ_WIDTH, D_MODEL)
    b_ffw = jnp.concatenate([w_ff1[0], w_ff2[0]], axis=0).astype(bf16)
    b_outw = w_out[0].astype(bf16)

    def with_mine(wall, mine):
        return lax.dynamic_update_slice(wall, mine[None], (chip, 0, 0))

    e_mat = _head_expand()
    dsk_x = jnp.repeat(d_skip[0], HEAD_DIM).reshape(1, SSD_WIDTH)
    dtb, alog = _pad128(dt_bias), _pad128(a_log)
    sinks = attn_sinks[0]
    lng, lnb = _row(ln_in_g), _row(ln_in_b)
    u1, z, xr, xc, q, kv, dtr = _inproj_fwd(x2, mod, lng, lnb, w_in_f, conv_w_full, conv_b)
    y, prev_all, wall_out = _ssd_fwd(xc, dtr, dtb, alog, dsk_x, b_outw)
    o, lse, wall_ff = _attn_fwd(q, kv, sinks, b_ffw)
    wall_ff, wall_out = with_mine(wall_ff, b_ffw), with_mine(wall_out, b_outw)
    yn, mix, r1 = _outproj_fwd(y, z, o, x2, mod, lng, lnb, ssd_norm_w, wall_out)

    dr1, u2, s_act, da, df, acc_mlp, db1 = _mlp_fwd_bwd(r1, tgt, mod, ln1_g, ln1_b, ln2_g, ln2_b, wall_ff, b_ff1, wall_ff,
                                                        b_ff2)
    ga = lax.empty((N_CHIPS, GA_ROWS, D_MODEL), bf16)
    ga = _wgrad_blob(ga, u2, da, "wgrad_ff1", lambda t, n: (n, t))
    ga = _wgrad_blob(ga, s_act, df, "wgrad_ff2", lambda t, n: (t // 2, 2 + t % 2))
    ga = ga.reshape(N_CHIPS, 2, GA_ROWS // 2, D_MODEL)
    dy, dz, do, dmix, acc_out, a_sib = _outproj_bwd(dr1, mix, y, z, mod, ssd_norm_w, wall_out, ga)
    gc = lax.empty((N_CHIPS, GC_ROWS, D_MODEL), bf16)
    gc = _wgrad_blob(gc, yn, dmix, "wgrad_out_y", lambda t, n: (t, 0))
    gc = _wgrad_blob(gc, o, dmix, "wgrad_out_o", lambda t, n: (2 + t, 0))
    gc = gc.reshape(N_CHIPS, 2, GC_ROWS // 2, D_MODEL)
    a_pair = _rs_sum_pair(place, ga, a_sib, "rs_sum_pair_a")
    dq, dkv, dsink, a_chips, c_sib = _attn_bwd(q, kv, do, lse, sinks, a_pair, gc)
    g_a = _rs_sum_chips(place, ga, a_sib, a_chips, "rs_sum_chips_a")
    c_pair = _rs_sum_pair(place, gc, c_sib, "rs_sum_pair_c")
    dxc, ddt, acc_ssd, dd_x, g_a, c_chips = _ssd_bwd(xc, dtr, dy, prev_all, dtb, alog, dsk_x, e_mat, g_a, c_pair)
    g_c = _rs_sum_chips(place, gc, c_sib, c_chips, "rs_sum_chips_c")
    dxr, acc_conv, g_c = _conv_bwd(dxc, xr, conv_w_full, conv_b, g_c)
    gb = _g_in_blocks(_wgrad(dz, u1, "wgrad_in_z"), _wgrad(dxr, u1, "wgrad_in_xbc"),
                      _wgrad(ddt.astype(bf16), u1, "wgrad_in_dt"), _wgrad(dq, u1, "wgrad_in_q"),
                      _wgrad(dkv, u1, "wgrad_in_kv")).reshape(N_CHIPS, 2, GB_ROWS // 2, D_MODEL)
    b_sib = _rs_to_sibling(gb)
    b_pair = _rs_sum_pair(place, gb, b_sib, "rs_sum_pair_b")
    grad_x, acc_in, b_chips = _inproj_bwd(dz, dxr, dq, dkv, ddt, dr1, x2, mod, lng, lnb, w_in_f, b_pair)
    g_b = _rs_share(_rs_sum_chips(place, gb, b_sib, b_chips, "rs_sum_chips_b"))

    packed = _small_pack(acc_in, acc_out, acc_mlp, db1, acc_conv, acc_ssd, dd_x, dsink, alog, e_mat)
    small_all = _allgather8(packed.reshape(8, SMALL_N // 8), "gather_small").reshape(N_DEV, SMALL_N)
    given = dict(ada_b=(ada_b, m_ada_b, v_ada_b), ln_in_g=(ln_in_g, m_ln_in_g, v_ln_in_g), ln_in_b=(ln_in_b, m_ln_in_b, v_ln_in_b),
                 conv_b=(conv_b, m_conv_b, v_conv_b), dt_bias=(dt_bias, m_dt_bias, v_dt_bias), a_log=(a_log, m_a_log, v_a_log),
                 d_skip=(d_skip, m_d_skip, v_d_skip), ssd_norm_w=(ssd_norm_w, m_ssd_norm_w, v_ssd_norm_w),
                 attn_sinks=(attn_sinks, m_attn_sinks, v_attn_sinks), ln1_g=(ln1_g, m_ln1_g, v_ln1_g),
                 ln1_b=(ln1_b, m_ln1_b, v_ln1_b), b_ff1=(b_ff1, m_b_ff1, v_b_ff1), b_ff2=(b_ff2, m_b_ff2, v_b_ff2),
                 ln2_g=(ln2_g, m_ln2_g, v_ln2_g), ln2_b=(ln2_b, m_ln2_b, v_ln2_b))
    upd = _small_update(small_all, *([_row(given[n][i]) for n in SMALL_PARAMS] for i in range(3)))
    small_res = {n: [t.reshape(given[n][0].shape) for t in upd[4 * j:4 * j + 4]] for j, n in enumerate(SMALL_PARAMS)}
    g_conv_all, loss_lanes = upd[4 * len(SMALL_PARAMS)], upd[4 * len(SMALL_PARAMS) + 1]
    loss = jnp.sum(loss_lanes)

    dmod_mine = lax.dynamic_slice(small_all[:, :6 * D_MODEL], (0, chip * ADA_COLS), (N_DEV, ADA_COLS))
    big = {"ada_w": [t[None] for t in _ada_bwd(c_all, dmod_mine, ada_w[0], m_ada_w[0], v_ada_w[0])]}

    g_conv = lax.dynamic_slice(g_conv_all.reshape(CONV_K, CONV_DIM), (0, chip * 384), (CONV_K, 384))
    big["conv_w"] = [t[None] for t in _adamw(conv_w[0], g_conv, m_conv_w[0], v_conv_w[0], "adamw_conv_w")]

    g_rows = g_b[:W_COLS].reshape(W_COLS, 1, D_MODEL)
    big["w_in"] = [from_rows(t) for t in (g_rows, *_adamw_rows(as_rows(w_in), g_rows, as_rows(m_w_in), as_rows(v_w_in)))]
    for name, g, row0, (w, m, v) in (("w_out", g_c, 0, (w_out, m_w_out, v_w_out)), ("w_ff1", g_a, 0, (w_ff1, m_w_ff1, v_w_ff1)),
                                     ("w_ff2", g_a, D_MODEL, (w_ff2, m_w_ff2, v_w_ff2))):
        big[name] = [t[None] for t in _adamw(w[0], g, m[0], v[0], "adamw_" + name, row0)]

    order = ("ln_in_g", "ln_in_b", "ada_w", "ada_b", "w_in", "conv_w", "conv_b", "dt_bias", "a_log", "d_skip", "ssd_norm_w",
             "attn_sinks", "w_out", "ln1_g", "ln1_b", "w_ff1", "b_ff1", "w_ff2", "b_ff2", "ln2_g", "ln2_b")
    res = {**small_res, **big}
    return (loss, grad_x[None], *[res[n][k] for k in range(4) for n in order])
```

```python
import numpy as np
import jax
import jax.numpy as jnp
from jax import lax
from jax.experimental import pallas as pl
from jax.experimental.pallas import tpu as pltpu

f32 = jnp.float32
bf16 = jnp.bfloat16

D_MODEL = 1024
SSD_WIDTH = 1024
SSD_HEADS = 16
HEAD_DIM = 64
SSD_STATE = 128
SSD_GROUPS = 2
CHUNK = 128
CONV_K = 4
CONV_DIM = 1536
ATTN_HEADS = 16
D_FF = 4096
PROJ_WIDTH = 3856
ALPHA = 2.0 ** 0.25
LN_EPS = 1e-5
RMS_EPS = 1e-5
ATTN_SCALE = HEAD_DIM ** -0.5
NEG = -1e30

ADAM_LR = 0.001
ADAM_B1 = 0.9
ADAM_B2 = 0.999
ADAM_EPS = 1e-08
ADAM_WD = 0.01
ADAM_STEP = 10

W_Z, W_XBC, W_Q, W_KV = slice(0, 1024), slice(1024, 2560), slice(2576, 3600), slice(3600, 3856)
W_DT = slice(2560, 2688)
W_DT_ROWS = 16
GA_ROWS = 2048
GB_ROWS = 1024
GC_ROWS = 512
WG_TM = 512
STAGE_ROWS = 512
ADAMW_BLOCK_ELEMS = 1 << 18
MLP_TM = 256
DENSE_TM = 512
N_CHIPS = 4
N_DEV = 8
VMEM_LIMIT = 56 * 1024 * 1024
MESH = pl.DeviceIdType.MESH

ALIBI_SLOPES = tuple(2.0 ** (-8.0 / ATTN_HEADS * (i + 1)) for i in range(ATTN_HEADS))


def _cparams(*sem):
    return pltpu.CompilerParams(dimension_semantics=sem, vmem_limit_bytes=VMEM_LIMIT)


def _sigmoid(x):
    return 1.0 / (1.0 + jnp.exp(-x))


def _softplus(x):
    return jnp.maximum(x, 0.0) + jnp.log1p(jnp.exp(-jnp.abs(x)))


def _ln_stats(x):
    mu = jnp.mean(x, axis=-1, keepdims=True)
    xc = x - mu
    var = jnp.mean(xc * xc, axis=-1, keepdims=True)
    rstd = lax.rsqrt(var + LN_EPS)
    return xc * rstd, rstd


def _ln_bwd(dy, xhat, rstd, g):
    dxh = dy * g
    m1 = jnp.mean(dxh, axis=-1, keepdims=True)
    m2 = jnp.mean(dxh * xhat, axis=-1, keepdims=True)
    return rstd * (dxh - m1 - xhat * m2)


def _dot(a, b):
    return jnp.dot(a, b, preferred_element_type=f32)


def _dot_nt(a, b):
    return lax.dot_general(a, b, (((1,), (1,)), ((), ())), preferred_element_type=f32)


def _dot_tn(a, b):
    return lax.dot_general(a, b, (((0,), (0,)), ((), ())), preferred_element_type=f32)


def _dot_exact(a, b):
    return jnp.dot(a, b, preferred_element_type=f32, precision=lax.Precision.HIGHEST)


def _split3(v):
    hi = v.astype(bf16)
    r1 = v - hi.astype(f32)
    mid = r1.astype(bf16)
    lo = (r1 - mid.astype(f32)).astype(bf16)
    return hi, mid, lo


def _sel_dot(sel, v):
    hi, mid, lo = _split3(v)
    return _dot(sel, hi) + _dot(sel, mid) + _dot(sel, lo)


def _dot_sel_nt(v, sel):
    hi = v.astype(bf16)
    mid = (v - hi.astype(f32)).astype(bf16)
    return _dot_nt(hi, sel) + _dot_nt(mid, sel)


def _expand_heads(v):
    lane = lax.broadcasted_iota(jnp.int32, (v.shape[0], 128), 1)
    blocks = [jnp.where(lane < HEAD_DIM, v[:, 2 * b:2 * b + 1], v[:, 2 * b + 1:2 * b + 2]) for b in range(SSD_HEADS // 2)]
    return jnp.concatenate(blocks, axis=1)


def _full(shape):
    nd = len(shape)
    return pl.BlockSpec(shape, lambda *_: (0,) * nd)


def _resident(shape):
    nd = len(shape)
    return pl.BlockSpec(shape, lambda *_: (0,) * nd, pipeline_mode=pl.Buffered(1))


def _rows(tm, n):
    return pl.BlockSpec((tm, n), lambda i: (i, 0))


def _wall_rows(k):
    return pl.BlockSpec((N_CHIPS, D_MODEL, D_MODEL), lambda *_: (0, k, 0), pipeline_mode=pl.Buffered(1))


def _inproj_fwd(x, mod, ln_g, ln_b, w_in_t, conv_w, conv_b):
    L = x.shape[0]
    tm = DENSE_TM
    nt = L // tm

    def body(x_ref, mod_ref, g_ref, b_ref, w_ref, cw_ref, cb_ref,
             u1_ref, z_ref, xr_ref, xc_ref, q_ref, kv_ref, dt_ref, halo, buf):
        @pl.when(pl.program_id(0) == 0)
        def _():
            halo[...] = jnp.zeros_like(halo)

        xhat, _ = _ln_stats(x_ref[...])
        h0 = xhat * g_ref[...] + b_ref[...]
        u1 = (h0 * (1.0 + mod_ref[1:2, :]) + mod_ref[0:1, :]).astype(bf16)
        u1_ref[...] = u1
        z_ref[...] = _dot_nt(u1, w_ref[W_Z, :])
        xr = _dot_nt(u1, w_ref[W_XBC, :])
        xr_ref[...] = xr
        q_ref[...] = _dot_nt(u1, w_ref[W_Q, :]).astype(bf16)
        kv_ref[...] = _dot_nt(u1, w_ref[W_KV, :]).astype(bf16)
        dt_ref[...] = _dot_nt(u1, w_ref[W_DT, :])
        buf[0:8, :] = halo[...]
        buf[8:8 + tm, :] = xr
        pre = cb_ref[...] + cw_ref[0:1, :] * buf[5:5 + tm, :]
        for k in range(1, CONV_K):
            pre = pre + cw_ref[k:k + 1, :] * buf[5 + k:5 + k + tm, :]
        xc_ref[...] = pre * _sigmoid(pre)
        halo[...] = xr[tm - 8:tm, :]

    return pl.pallas_call(
        body, name="inproj_fwd", grid=(nt,),
        in_specs=[_rows(tm, D_MODEL), _full((8, D_MODEL)), _full((1, D_MODEL)), _full((1, D_MODEL)),
                  _resident((PROJ_WIDTH, D_MODEL)), _full((CONV_K, CONV_DIM)), _full((1, CONV_DIM))],
        out_specs=[_rows(tm, D_MODEL), _rows(tm, D_MODEL), _rows(tm, CONV_DIM), _rows(tm, CONV_DIM),
                   _rows(tm, D_MODEL), _rows(tm, 256), _rows(tm, 128)],
        out_shape=[jax.ShapeDtypeStruct((L, D_MODEL), bf16), jax.ShapeDtypeStruct((L, D_MODEL), f32),
                   jax.ShapeDtypeStruct((L, CONV_DIM), f32), jax.ShapeDtypeStruct((L, CONV_DIM), f32),
                   jax.ShapeDtypeStruct((L, D_MODEL), bf16), jax.ShapeDtypeStruct((L, 256), bf16),
                   jax.ShapeDtypeStruct((L, 128), f32)],
        scratch_shapes=[pltpu.VMEM((8, CONV_DIM), f32), pltpu.VMEM((tm + 8, CONV_DIM), f32)],
        compiler_params=_cparams("arbitrary"),
    )(x, mod, ln_g, ln_b, w_in_t, conv_w, conv_b)


def _head_expand():
    e = np.zeros((128, SSD_WIDTH), np.float32)
    for h in range(SSD_HEADS):
        e[h, h * HEAD_DIM:(h + 1) * HEAD_DIM] = 1.0
    return jnp.asarray(e, dtype=bf16)


def _ssd_chunk_common(dt_raw, dtb, a_row):
    T = CHUNK
    lane = lax.broadcasted_iota(jnp.int32, (T, 128), 1)
    dt = jnp.where(lane < SSD_HEADS, _softplus(dt_raw + dtb), 0.0)
    a = dt * a_row
    r = lax.broadcasted_iota(jnp.int32, (T, T), 0)
    c = lax.broadcasted_iota(jnp.int32, (T, T), 1)
    tril = (c <= r).astype(bf16)
    cum = _sel_dot(tril, a)
    dtx = _expand_heads(dt)
    cumx = _expand_heads(cum)
    return dt, a, cum, dtx, cumx, r, c


def _ssd_fwd(xc, dt_raw, dt_bias, a_log, d_skip_x, blob):
    L = xc.shape[0]
    nc = L // CHUNK
    T = CHUNK
    R = blob.shape[0]

    def body(xc_ref, dt_ref, dtb_ref, al_ref, dsk_ref, blob_ref, y_ref, prev_ref, wall_ref, st, send_sems, recv_sems):
        start, finish = _gather_job(blob_ref, wall_ref, send_sems, recv_sems, R)

        @pl.when(pl.program_id(0) == 0)
        def _():
            st[...] = jnp.zeros_like(st)
            start()

        @pl.when(pl.program_id(0) == nc - 1)
        def _():
            finish()

        a_row = -jnp.exp(al_ref[...])
        lane1 = lax.broadcasted_iota(jnp.int32, (1, 128), 1)
        a_row = jnp.where(lane1 < SSD_HEADS, a_row, 0.0)
        dt, a, cum, dtx, cumx, r, c = _ssd_chunk_common(dt_ref[...], dtb_ref[...], a_row)
        cum_t = cum.T
        ex = jnp.exp(cumx)
        last = cumx[T - 1:T, :]
        wx = jnp.exp(last - cumx)
        cdx = jnp.exp(last)
        xs = xc_ref[:, 0:SSD_WIDTH]
        X = xs * dtx
        Xb = X.astype(bf16)
        Xd = (X * wx).astype(bf16)
        prev = st[...]
        prev_ref[0] = prev
        prevb = prev.astype(bf16)
        tri = c <= r
        lane = lax.broadcasted_iota(jnp.int32, (T, 128), 1)
        y_blocks = []
        new_states = []
        for g in range(SSD_GROUPS):
            Bg = xc_ref[:, 1024 + 128 * g:1152 + 128 * g].astype(bf16)
            Cg = xc_ref[:, 1280 + 128 * g:1408 + 128 * g].astype(bf16)
            G = _dot_nt(Cg, Bg)
            yoff = _dot(Cg, prevb[:, 512 * g:512 * (g + 1)])
            new_states.append(_dot_tn(Bg, Xd[:, 512 * g:512 * (g + 1)]))
            for j in range(4):
                blk = 4 * g + j
                Xblk = Xb[:, 128 * blk:128 * (blk + 1)]
                ys = []
                for half in range(2):
                    h = 2 * blk + half
                    seg = jnp.minimum(cum[:, h:h + 1] - cum_t[h:h + 1, :], 0.0)
                    M = jnp.where(tri, G * jnp.exp(seg), 0.0).astype(bf16)
                    ys.append(_dot(M, Xblk))
                yd = jnp.where(lane < HEAD_DIM, ys[0], ys[1])
                sl = slice(128 * blk, 128 * (blk + 1))
                y_blocks.append(yd + ex[:, sl] * yoff[:, 128 * j:128 * (j + 1)] + dsk_ref[:, sl] * xs[:, sl])
        y_ref[...] = jnp.concatenate(y_blocks, axis=1)
        st[...] = prev * cdx + jnp.concatenate(new_states, axis=1)

    return pl.pallas_call(
        body, name="ssd_fwd", grid=(nc,),
        in_specs=[_rows(T, CONV_DIM), _rows(T, 128), _full((1, 128)), _full((1, 128)), _full((1, SSD_WIDTH)), _ANY_SPEC],
        out_specs=[_rows(T, SSD_WIDTH), pl.BlockSpec((1, SSD_STATE, SSD_WIDTH), lambda i: (i, 0, 0)), _ANY_SPEC],
        out_shape=[jax.ShapeDtypeStruct((L, SSD_WIDTH), f32), jax.ShapeDtypeStruct((nc, SSD_STATE, SSD_WIDTH), f32),
                   jax.ShapeDtypeStruct((N_CHIPS, R, D_MODEL), bf16)],
        scratch_shapes=[pltpu.VMEM((SSD_STATE, SSD_WIDTH), f32)] + _sems(6),
        compiler_params=_cparams("arbitrary"),
    )(xc, dt_raw, dt_bias, a_log, d_skip_x, blob)


def _kv_halves(kv_prev, kv_cur, first):
    kv = jnp.concatenate([jnp.where(first, 0.0, kv_prev.astype(f32)), kv_cur.astype(f32)], axis=0)
    lane = lax.broadcasted_iota(jnp.int32, (2 * CHUNK, 128), 1)
    lo = lane < HEAD_DIM
    out = []
    for g in range(2):
        per_half = []
        for half in range(2):
            both = []
            for t in (kv[:, 0:128], kv[:, 128:256]):
                src = t if g == half else pltpu.roll(t, HEAD_DIM, 1)
                both.append(jnp.where(lo if half == 0 else ~lo, src, 0.0).astype(bf16))
            per_half.append(tuple(both))
        out.append(per_half)
    return out


def _attn_masks(first):
    r = lax.broadcasted_iota(jnp.int32, (CHUNK, 2 * CHUNK), 0)
    c = lax.broadcasted_iota(jnp.int32, (CHUNK, 2 * CHUNK), 1)
    dist = r + CHUNK - c
    valid = (dist >= 0) & (dist < CHUNK) & ((c >= CHUNK) | jnp.logical_not(first))
    return dist.astype(f32), valid


def _head_stack(g, half, sink_ref):
    blks = [4 * g + i for i in range(4)]
    heads = [2 * b + half for b in blks]
    slope = jnp.concatenate([jnp.full((CHUNK, 1), ALIBI_SLOPES[h], f32) for h in heads], axis=0)
    sink = jnp.concatenate([jnp.full((CHUNK, 1), sink_ref[h], f32) for h in heads], axis=0)
    return blks, heads, slope, sink


def _attn_fwd(q, kv, sinks, blob):
    L = q.shape[0]
    nb = L // CHUNK
    T = CHUNK
    R = blob.shape[0]

    def body(sink_ref, q_ref, kvp_ref, kvc_ref, blob_ref, o_ref, lse_ref, wall_ref, send_sems, recv_sems):
        first = pl.program_id(0) == 0
        start, finish = _gather_job(blob_ref, wall_ref, send_sems, recv_sems, R)

        @pl.when(first)
        def _():
            start()

        @pl.when(pl.program_id(0) == nb - 1)
        def _():
            finish()

        ext = _kv_halves(kvp_ref[...], kvc_ref[...], first)
        dist, valid = _attn_masks(first)
        lane = lax.broadcasted_iota(jnp.int32, (T, 128), 1)
        lse = jnp.zeros((T, 128), f32)
        o_blocks = []
        for blk in range(8):
            qb = q_ref[:, 128 * blk:128 * (blk + 1)]
            acc = None
            for half in range(2):
                h = 2 * blk + half
                k_ext, v_ext = ext[h // 8][half]
                s = _dot_nt(qb, k_ext) * ATTN_SCALE - ALIBI_SLOPES[h] * dist
                s = jnp.where(valid, s, NEG)
                sink = sink_ref[h]
                m = jnp.maximum(jnp.max(s, axis=-1, keepdims=True), sink)
                p = jnp.exp(s - m)
                den = jnp.sum(p, axis=-1, keepdims=True) + jnp.exp(sink - m)
                oh = _dot(p.astype(bf16), v_ext) * (1.0 / den)
                acc = oh if acc is None else acc + oh
                lse = jnp.where(lane == h, m + jnp.log(den), lse)
            o_blocks.append(acc.astype(bf16))
        o_ref[...] = jnp.concatenate(o_blocks, axis=1)
        lse_ref[...] = lse

    return pl.pallas_call(
        body, name="attn_fwd", grid=(nb,),
        in_specs=[pl.BlockSpec(memory_space=pltpu.SMEM), _rows(T, D_MODEL),
                  pl.BlockSpec((T, 256), lambda i: (jnp.maximum(i - 1, 0), 0)), _rows(T, 256), _ANY_SPEC],
        out_specs=[_rows(T, D_MODEL), _rows(T, 128), _ANY_SPEC],
        out_shape=[jax.ShapeDtypeStruct((L, D_MODEL), bf16), jax.ShapeDtypeStruct((L, 128), f32),
                   jax.ShapeDtypeStruct((N_CHIPS, R, D_MODEL), bf16)],
        scratch_shapes=_sems(6),
        compiler_params=_cparams("arbitrary"),
    )(sinks, q, kv, kv, blob)


def _gated_norm(y, z, w):
    sz = _sigmoid(z)
    hg = y * (z * sz)
    ns, rss = [], []
    for g in range(SSD_GROUPS):
        hs = hg[:, 512 * g:512 * (g + 1)]
        rs = lax.rsqrt(jnp.mean(hs * hs, axis=-1, keepdims=True) + RMS_EPS)
        ns.append(hs * rs)
        rss.append(rs)
    n = jnp.concatenate(ns, axis=1)
    return n * w, n, rss, sz


def _outproj_fwd(y, z, o, x, mod, ln_g, ln_b, norm_w, w_out):
    L = x.shape[0]
    tm = DENSE_TM

    def body(y_ref, z_ref, o_ref, x_ref, mod_ref, g_ref, b_ref, nw_ref, w_ref, yn_ref, mix_ref, r1_ref):
        yn, _, _, _ = _gated_norm(y_ref[...], z_ref[...], nw_ref[...])
        ynb = yn.astype(bf16)
        yn_ref[...] = ynb
        mix = (_dot(ynb[:, 0:512], w_ref[0]) + _dot(ynb[:, 512:1024], w_ref[1])
               + _dot(o_ref[:, 0:512], w_ref[2]) + _dot(o_ref[:, 512:1024], w_ref[3]))
        mix_ref[...] = mix
        xhat, _ = _ln_stats(x_ref[...])
        h0 = xhat * g_ref[...] + b_ref[...]
        r1_ref[...] = ALPHA * h0 + (1.0 + mod_ref[2:3, :]) * mix

    v = _full((1, D_MODEL))
    return pl.pallas_call(
        body, name="outproj_fwd", grid=(L // tm,),
        in_specs=[_rows(tm, D_MODEL), _rows(tm, D_MODEL), _rows(tm, D_MODEL), _rows(tm, D_MODEL),
                  _full((8, D_MODEL)), v, v, v, _resident((N_CHIPS, 512, D_MODEL))],
        out_specs=[_rows(tm, D_MODEL)] * 3,
        out_shape=[jax.ShapeDtypeStruct((L, D_MODEL), bf16), jax.ShapeDtypeStruct((L, D_MODEL), f32),
                   jax.ShapeDtypeStruct((L, D_MODEL), f32)],
        compiler_params=_cparams("parallel"),
    )(y, z, o, x, mod, ln_g, ln_b, norm_w, w_out)


A_LN2G, A_LN2B, A_G2, A_B2, A_SC2, A_SH2, A_LN1G, A_LN1B, A_LOSS = range(9)


def _mlp_fwd_bwd(r1, target, mod, ln1_g, ln1_b, ln2_g, ln2_b, w1, b1, w2, b2):
    L = r1.shape[0]
    tm = MLP_TM
    nj = D_FF // 1024

    def body(r1_ref, t_ref, mod_ref, g1_ref, bb1_ref, g2_ref, bb2_ref, w1_ref, b1_ref, w2_ref, b2_ref,
             dr1_ref, u2_ref, s_ref, da_ref, df_ref, acc_ref, db1_ref, hr):
        @pl.when(pl.program_id(0) == 0)
        def _():
            acc_ref[...] = jnp.zeros_like(acc_ref)
            db1_ref[...] = jnp.zeros_like(db1_ref)

        sc2, sh2, gate2 = mod_ref[4:5, :], mod_ref[3:4, :], mod_ref[5:6, :]
        xhat1, rstd1 = _ln_stats(r1_ref[...])
        h1 = xhat1 * g1_ref[...] + bb1_ref[...]
        u2f = h1 * (1.0 + sc2) + sh2
        u2 = u2f.astype(bf16)
        u2_ref[...] = u2
        f = jnp.zeros((tm, D_MODEL), f32) + b2_ref[...]
        for j in range(nj):
            cs = slice(1024 * j, 1024 * (j + 1))
            a = _dot(u2, w1_ref[j]) + b1_ref[:, cs]
            hrj = jnp.maximum(a, 0.0)
            hr[:, cs] = hrj.astype(bf16)
            sj = (hrj * hrj).astype(bf16)
            s_ref[:, cs] = sj
            f = f + _dot(sj, w2_ref[j])
        r2 = ALPHA * h1 + (1.0 + gate2) * f
        xhat2, rstd2 = _ln_stats(r2)
        h2 = xhat2 * g2_ref[...] + bb2_ref[...]
        diff = h2 - t_ref[...]
        dh2 = diff * (1.0 / D_MODEL)

        def add(row, val):
            acc_ref[row:row + 1, :] += jnp.sum(val, axis=0, keepdims=True)

        add(A_LOSS, diff * diff * (0.5 / D_MODEL))
        add(A_LN2G, dh2 * xhat2)
        add(A_LN2B, dh2)
        dr2 = _ln_bwd(dh2, xhat2, rstd2, g2_ref[...])
        add(A_G2, dr2 * f)
        df = dr2 * (1.0 + gate2)
        add(A_B2, df)
        dfb = df.astype(bf16)
        df_ref[...] = dfb
        du2 = jnp.zeros((tm, D_MODEL), f32)
        for j in range(nj):
            cs = slice(1024 * j, 1024 * (j + 1))
            ds = _dot_nt(dfb, w2_ref[j])
            daj = ds * (2.0 * hr[:, cs].astype(f32))
            db1_ref[:, cs] += jnp.sum(daj, axis=0, keepdims=True)
            dajb = daj.astype(bf16)
            da_ref[:, cs] = dajb
            du2 = du2 + _dot_nt(dajb, w1_ref[j])
        add(A_SC2, du2 * h1)
        add(A_SH2, du2)
        dh1 = ALPHA * dr2 + du2 * (1.0 + sc2)
        add(A_LN1G, dh1 * xhat1)
        add(A_LN1B, dh1)
        dr1_ref[...] = _ln_bwd(dh1, xhat1, rstd1, g1_ref[...])

    v = _full((1, D_MODEL))
    return pl.pallas_call(
        body, name="mlp_fwd_bwd", grid=(L // tm,),
        in_specs=[_rows(tm, D_MODEL), _rows(tm, D_MODEL), _full((8, D_MODEL)), v, v, v, v,
                  _wall_rows(0), _full((1, D_FF)), _wall_rows(1), v],
        out_specs=[_rows(tm, D_MODEL), _rows(tm, D_MODEL), _rows(tm, D_FF), _rows(tm, D_FF), _rows(tm, D_MODEL),
                   _full((16, D_MODEL)), _full((1, D_FF))],
        out_shape=[jax.ShapeDtypeStruct((L, D_MODEL), f32), jax.ShapeDtypeStruct((L, D_MODEL), bf16),
                   jax.ShapeDtypeStruct((L, D_FF), bf16), jax.ShapeDtypeStruct((L, D_FF), bf16),
                   jax.ShapeDtypeStruct((L, D_MODEL), bf16), jax.ShapeDtypeStruct((16, D_MODEL), f32),
                   jax.ShapeDtypeStruct((1, D_FF), f32)],
        scratch_shapes=[pltpu.VMEM((tm, D_FF), bf16)],
        compiler_params=_cparams("arbitrary"),
    )(r1, target, mod, ln1_g, ln1_b, ln2_g, ln2_b, w1, b1, w2, b2)


def _wgrad(a, b, name):
    L, M = a.shape
    N = b.shape[1]
    tm = min(M, 512)
    tn = next(t for t in (1024, 768, 512, 256, 128) if N % t == 0)

    def body(a_ref, b_ref, o_ref):
        o_ref[...] = _dot_tn(a_ref[...], b_ref[...]).astype(bf16)

    return pl.pallas_call(
        body, name=name, grid=(M // tm, N // tn),
        in_specs=[pl.BlockSpec((L, tm), lambda i, j: (0, i)), pl.BlockSpec((L, tn), lambda i, j: (0, j))],
        out_specs=pl.BlockSpec((tm, tn), lambda i, j: (i, j)),
        out_shape=jax.ShapeDtypeStruct((M, N), bf16),
        compiler_params=_cparams("parallel", "parallel"),
    )(a, b)


def _wgrad_blob(blob, a, b, name, place_of):
    L, M = a.shape
    N = b.shape[1]

    def body(blob_ref, a_ref, b_ref, o_ref):
        o_ref[0] = _dot_tn(a_ref[...], b_ref[...]).astype(bf16)

    return pl.pallas_call(
        body, name=name, grid=(M // WG_TM, N // D_MODEL),
        in_specs=[pl.BlockSpec(memory_space=pl.ANY), pl.BlockSpec((L, WG_TM), lambda t, n: (0, t)),
                  pl.BlockSpec((L, D_MODEL), lambda t, n: (0, n))],
        out_specs=pl.BlockSpec((1, WG_TM, D_MODEL), lambda t, n: (*place_of(t, n), 0)),
        out_shape=jax.ShapeDtypeStruct(blob.shape, bf16), input_output_aliases={0: 0},
        compiler_params=_cparams("parallel", "parallel"),
    )(blob, a, b)


def _outproj_bwd(dr1, mix, y, z, mod, norm_w, w_out, gb):
    L = dr1.shape[0]
    tm = DENSE_TM
    nt = L // tm

    def body(dr1_ref, mix_ref, y_ref, z_ref, mod_ref, nw_ref, w_ref, gb_ref,
             dy_ref, dz_ref, do_ref, dmix_ref, acc_ref, sib_ref, send_sems, recv_sems):
        start, wait = _to_sibling_job(gb_ref, sib_ref, send_sems, recv_sems)

        @pl.when(pl.program_id(0) == 0)
        def _():
            acc_ref[...] = jnp.zeros_like(acc_ref)
            start()

        @pl.when(pl.program_id(0) == nt - 1)
        def _():
            wait()

        dr1 = dr1_ref[...]
        acc_ref[0:1, :] += jnp.sum(dr1 * mix_ref[...], axis=0, keepdims=True)
        dmix = (dr1 * (1.0 + mod_ref[2:3, :])).astype(bf16)
        dmix_ref[...] = dmix
        dyn = jnp.concatenate([_dot_nt(dmix, w_ref[0]), _dot_nt(dmix, w_ref[1])], axis=1)
        do_ref[...] = jnp.concatenate([_dot_nt(dmix, w_ref[2]), _dot_nt(dmix, w_ref[3])], axis=1).astype(bf16)
        yv, zv = y_ref[...], z_ref[...]
        _, n, rss, sz = _gated_norm(yv, zv, nw_ref[...])
        acc_ref[1:2, :] += jnp.sum(dyn * n, axis=0, keepdims=True)
        dn = dyn * nw_ref[...]
        parts = []
        for g in range(SSD_GROUPS):
            sl = slice(512 * g, 512 * (g + 1))
            dng, ng = dn[:, sl], n[:, sl]
            parts.append(rss[g] * (dng - ng * jnp.mean(dng * ng, axis=-1, keepdims=True)))
        dhg = jnp.concatenate(parts, axis=1)
        dy_ref[...] = dhg * (zv * sz)
        dz_ref[...] = (dhg * yv * (sz * (1.0 + zv * (1.0 - sz)))).astype(bf16)

    return pl.pallas_call(
        body, name="outproj_bwd", grid=(nt,),
        in_specs=[_rows(tm, D_MODEL)] * 4 + [_full((8, D_MODEL)), _full((1, D_MODEL)), _resident((N_CHIPS, 512, D_MODEL)),
                  _ANY_SPEC],
        out_specs=[_rows(tm, D_MODEL)] * 4 + [_full((8, D_MODEL)), _ANY_SPEC],
        out_shape=[jax.ShapeDtypeStruct((L, D_MODEL), f32)] + [jax.ShapeDtypeStruct((L, D_MODEL), bf16)] * 3
        + [jax.ShapeDtypeStruct((8, D_MODEL), f32), jax.ShapeDtypeStruct((N_CHIPS,) + gb.shape[2:], bf16)],
        scratch_shapes=_sems(N_CHIPS),
        compiler_params=_cparams("arbitrary"),
    )(dr1, mix, y, z, mod, norm_w, w_out, gb)


def _attn_bwd(q, kv, do, lse, sinks, pb, gb2):
    L = q.shape[0]
    nb = L // CHUNK
    T = CHUNK

    def body(sink_ref, q_ref, kvp_ref, kvc_ref, do_ref, lse_ref, pb_ref, gb2_ref, dq_ref, dkv_ref, dsink_ref, chips_ref,
             sib2_ref, carry, send_sems, recv_sems, send_sems2, recv_sems2):
        n = pl.program_id(0)
        start, wait = _to_chips_job(pb_ref, chips_ref, send_sems, recv_sems)
        start2, wait2 = _to_sibling_job(gb2_ref, sib2_ref, send_sems2, recv_sems2)

        @pl.when(n == 0)
        def _():
            carry[...] = jnp.zeros_like(carry)
            dsink_ref[...] = jnp.zeros_like(dsink_ref)
            start()
            start2()

        @pl.when(n < nb)
        def _():
            first = n == 0
            ext = _kv_halves(kvp_ref[...], kvc_ref[...], first)
            dist, valid = _attn_masks(first)
            dist4, valid4 = jnp.concatenate([dist] * 4, axis=0), jnp.concatenate([valid] * 4, axis=0)
            lane1 = lax.broadcasted_iota(jnp.int32, (1, 128), 1)
            lse = lse_ref[...]
            qts = [q_ref[:, 128 * b:128 * (b + 1)].astype(f32).T.astype(bf16) for b in range(8)]
            dots = [do_ref[:, 128 * b:128 * (b + 1)].astype(f32).T.astype(bf16) for b in range(8)]
            acck = [None, None]
            accv = [None, None]
            dsink = jnp.zeros((1, 128), f32)
            dq_acc = [None] * 8
            for g in range(2):
                for half in range(2):
                    k_ext, v_ext = ext[g][half]
                    blks, heads, slope, sink = _head_stack(g, half, sink_ref)
                    qs = jnp.concatenate([q_ref[:, 128 * b:128 * (b + 1)] for b in blks], axis=0)
                    dos = jnp.concatenate([do_ref[:, 128 * b:128 * (b + 1)] for b in blks], axis=0)
                    rows = slice(HEAD_DIM * half, HEAD_DIM * (half + 1))
                    qt = jnp.concatenate([qts[b][rows, :] for b in blks], axis=1)
                    dot_ = jnp.concatenate([dots[b][rows, :] for b in blks], axis=1)
                    lse_col = jnp.concatenate([lse[:, h:h + 1] for h in heads], axis=0)
                    s = _dot_nt(qs, k_ext) * ATTN_SCALE - slope * dist4
                    p = jnp.where(valid4, jnp.exp(s - lse_col), 0.0)
                    dp = _dot_nt(dos, v_ext)
                    delta = jnp.sum(p * dp, axis=-1, keepdims=True)
                    ds = (p * (dp - delta) * ATTN_SCALE).astype(bf16)
                    sd = jnp.exp(sink - lse_col) * delta
                    dqs = _dot(ds, k_ext)
                    for i, b in enumerate(blks):
                        seg = slice(T * i, T * (i + 1))
                        dq_acc[b] = dqs[seg, :] if dq_acc[b] is None else dq_acc[b] + dqs[seg, :]
                        dsink = dsink - jnp.where(lane1 == heads[i], jnp.sum(sd[seg, :], axis=0, keepdims=True), 0.0)
                    dk = _dot(qt, ds)
                    dv = _dot(dot_, p.astype(bf16))
                    acck[g] = dk if acck[g] is None else acck[g] + dk
                    accv[g] = dv if accv[g] is None else accv[g] + dv
            dq_ref[...] = jnp.concatenate([a.astype(bf16) for a in dq_acc], axis=1)
            dsink_ref[...] += dsink
            dkv = jnp.concatenate([jnp.concatenate(acck, axis=0).T, jnp.concatenate(accv, axis=0).T], axis=1)
            dkv_ref[...] = (carry[...] + dkv[0:T, :]).astype(bf16)
            carry[...] = dkv[T:2 * T, :]

        @pl.when(n == nb)
        def _():
            dkv_ref[...] = carry[...].astype(bf16)
            wait()
            wait2()

    cur = lambda i: (jnp.minimum(i, nb - 1), 0)
    return pl.pallas_call(
        body, name="attn_bwd", grid=(nb + 1,),
        in_specs=[pl.BlockSpec(memory_space=pltpu.SMEM), pl.BlockSpec((T, D_MODEL), cur),
                  pl.BlockSpec((T, 256), lambda i: (jnp.maximum(jnp.minimum(i, nb - 1) - 1, 0), 0)),
                  pl.BlockSpec((T, 256), cur), pl.BlockSpec((T, D_MODEL), cur), pl.BlockSpec((T, 128), cur), _ANY_SPEC,
                  _ANY_SPEC],
        out_specs=[pl.BlockSpec((T, D_MODEL), cur), pl.BlockSpec((T, 256), lambda i: (jnp.maximum(i - 1, 0), 0)),
                   _full((1, 128)), _ANY_SPEC, _ANY_SPEC],
        out_shape=[jax.ShapeDtypeStruct((L, D_MODEL), bf16), jax.ShapeDtypeStruct((L, 256), bf16),
                   jax.ShapeDtypeStruct((1, 128), f32), jax.ShapeDtypeStruct((N_CHIPS - 1,) + pb.shape[1:], bf16),
                   jax.ShapeDtypeStruct((N_CHIPS,) + gb2.shape[2:], bf16)],
        scratch_shapes=[pltpu.VMEM((T, 256), f32)] + _sems(N_CHIPS - 1) + _sems(N_CHIPS),
        compiler_params=_cparams("arbitrary"),
    )(sinks, q, kv, kv, do, lse, pb, gb2)


def _ssd_bwd(xc, dt_raw, dy, prev_all, dt_bias, a_log, d_skip_x, e_mat, g, pb2):
    L = xc.shape[0]
    nc = L // CHUNK
    T = CHUNK
    RG = g.shape[0]

    def body(xc_ref, dt_ref, dy_ref, prev_ref, dtb_ref, al_ref, dsk_ref, e_ref, g_in_ref, pb2_ref,
             dxc_ref, ddt_ref, acc_ref, dd_ref, g_ref, chips2_ref, dst, dxs_s, send_sems, recv_sems, send_sems2, recv_sems2):
        start, wait = _share_job(g_ref, send_sems, recv_sems, RG)
        start2, wait2 = _to_chips_job(pb2_ref, chips2_ref, send_sems2, recv_sems2)

        @pl.when(pl.program_id(0) == 0)
        def _():
            dst[...] = jnp.zeros_like(dst)
            acc_ref[...] = jnp.zeros_like(acc_ref)
            dd_ref[...] = jnp.zeros_like(dd_ref)
            start()
            start2()

        @pl.when(pl.program_id(0) == nc - 1)
        def _():
            wait()
            wait2()

        lane1 = lax.broadcasted_iota(jnp.int32, (1, 128), 1)
        a_row = jnp.where(lane1 < SSD_HEADS, -jnp.exp(al_ref[...]), 0.0)
        e_mat_v = e_ref[...]
        dt, a, cum, dtx, cumx, r, c = _ssd_chunk_common(dt_ref[...], dtb_ref[...], a_row)
        cum_t = cum.T
        ex = jnp.exp(cumx)
        last = cumx[T - 1:T, :]
        wx = jnp.exp(last - cumx)
        cdx = jnp.exp(last)
        xs = xc_ref[:, 0:SSD_WIDTH]
        X = xs * dtx
        Xb = X.astype(bf16)
        Xdb = (X * wx).astype(bf16)
        dyv = dy_ref[...]
        prev = prev_ref[0]
        prevb = prev.astype(bf16)
        dnew = dst[...]
        dnewb = dnew.astype(bf16)
        tri = c <= r
        lane = lax.broadcasted_iota(jnp.int32, (T, 128), 1)
        sub = lax.broadcasted_iota(jnp.int32, (128, T), 0)
        lo = lane < HEAD_DIM

        def red(vals, g):
            return _dot_sel_nt(vals, e_mat_v[:, 512 * g:512 * (g + 1)])

        de = jnp.zeros((T, 128), f32)
        dw = jnp.zeros((T, 128), f32)
        ddt_x = jnp.zeros((T, 128), f32)
        dcum_col = jnp.zeros((T, 128), f32)
        dcum_row = jnp.zeros((128, T), f32)
        dprev_parts, dBs, dCs = [], [], []
        for g in range(SSD_GROUPS):
            s5 = slice(512 * g, 512 * (g + 1))
            Bg = xc_ref[:, 1024 + 128 * g:1152 + 128 * g].astype(bf16)
            Cg = xc_ref[:, 1280 + 128 * g:1408 + 128 * g].astype(bf16)
            G = _dot_nt(Cg, Bg)
            Z = _dot(Cg, prevb[:, s5])
            dyg = dyv[:, s5]
            dZb = (dyg * ex[:, s5]).astype(bf16)
            dXd = _dot(Bg, dnewb[:, s5])
            dC = _dot_nt(dZb, prevb[:, s5])
            dB = _dot_nt(Xdb[:, s5], dnewb[:, s5])
            dprev_parts.append(_dot_tn(Cg, dZb) + dnew[:, s5] * cdx[:, s5])
            de = de + red(dyg * Z, g)
            dw = dw + red(dXd * X[:, s5], g)
            dXg = dXd * wx[:, s5]
            dG = jnp.zeros((T, T), f32)
            for j in range(4):
                blk = 4 * g + j
                sl = slice(128 * blk, 128 * (blk + 1))
                Xblk = Xb[:, sl]
                dyblk = dyv[:, sl]
                dyblk_b = dyblk.astype(bf16)
                dxh = []
                for half in range(2):
                    h = 2 * blk + half
                    seg = jnp.minimum(cum[:, h:h + 1] - cum_t[h:h + 1, :], 0.0)
                    Lm = jnp.where(tri, jnp.exp(seg), 0.0)
                    M = G * Lm
                    dyh = jnp.where(lo if half == 0 else ~lo, dyblk, 0.0).astype(bf16)
                    dM = _dot_nt(dyh, Xblk)
                    dG = dG + dM * Lm
                    Q = dM * M
                    dcum_col = dcum_col + jnp.where(lane == h, jnp.sum(Q, axis=1, keepdims=True), 0.0)
                    dcum_row = dcum_row + jnp.where(sub == h, jnp.sum(Q, axis=0, keepdims=True), 0.0)
                    dxh.append(_dot_tn(M.astype(bf16), dyblk_b))
                dXblk = dXg[:, 128 * j:128 * (j + 1)] + jnp.where(lo, dxh[0], dxh[1])
                xsb = xs[:, sl]
                dxs_s[:, sl] = dXblk * dtx[:, sl] + dsk_ref[:, sl] * dyblk
                ddt_x = ddt_x + _dot_sel_nt(dXblk * xsb, e_mat_v[:, sl])
                dd_ref[:, sl] += jnp.sum(dyblk * xsb, axis=0, keepdims=True)
            dGb = dG.astype(bf16)
            dCs.append(dC + _dot(dGb, Bg))
            dBs.append(dB + _dot_tn(dGb, Cg))
        e16 = jnp.exp(cum)
        cum_last = cum[T - 1:T, :]
        w16 = jnp.exp(cum_last - cum)
        dcd = jnp.sum(dnew * prev, axis=0, keepdims=True)
        dcd16 = red(dcd[:, 0:512], 0) + red(dcd[:, 512:1024], 1)
        dww = dw * w16
        extra = jnp.sum(dww, axis=0, keepdims=True) + dcd16 * jnp.exp(cum_last)
        rowi = lax.broadcasted_iota(jnp.int32, (T, 128), 0)
        dcum = dcum_col - dcum_row.T + de * e16 - dww + jnp.where(rowi == T - 1, extra, 0.0)
        da = _sel_dot((c >= r).astype(bf16), dcum)
        ddt = ddt_x + da * a_row
        acc_ref[0:1, :] += jnp.sum(da * dt, axis=0, keepdims=True)
        ddt_raw = jnp.where(lane < SSD_HEADS, ddt * _sigmoid(dt_ref[...] + dtb_ref[...]), 0.0)
        ddt_ref[...] = ddt_raw
        acc_ref[1:2, :] += jnp.sum(ddt_raw, axis=0, keepdims=True)
        dxc_ref[:, 0:SSD_WIDTH] = dxs_s[...]
        dxc_ref[:, 1024:1280] = jnp.concatenate(dBs, axis=1)
        dxc_ref[:, 1280:1536] = jnp.concatenate(dCs, axis=1)
        dst[...] = jnp.concatenate(dprev_parts, axis=1)

    rev = lambda i: (nc - 1 - i, 0)
    return pl.pallas_call(
        body, name="ssd_bwd", grid=(nc,),
        in_specs=[pl.BlockSpec((T, CONV_DIM), rev), pl.BlockSpec((T, 128), rev), pl.BlockSpec((T, SSD_WIDTH), rev),
                  pl.BlockSpec((1, SSD_STATE, SSD_WIDTH), lambda i: (nc - 1 - i, 0, 0)),
                  _full((1, 128)), _full((1, 128)), _full((1, SSD_WIDTH)), _full((128, SSD_WIDTH)), _ANY_SPEC, _ANY_SPEC],
        out_specs=[pl.BlockSpec((T, CONV_DIM), rev), pl.BlockSpec((T, 128), rev), _full((8, 128)),
                   _full((1, SSD_WIDTH)), _ANY_SPEC, _ANY_SPEC],
        out_shape=[jax.ShapeDtypeStruct((L, CONV_DIM), f32), jax.ShapeDtypeStruct((L, 128), f32),
                   jax.ShapeDtypeStruct((8, 128), f32), jax.ShapeDtypeStruct((1, SSD_WIDTH), f32),
                   jax.ShapeDtypeStruct(g.shape, f32), jax.ShapeDtypeStruct((N_CHIPS - 1,) + pb2.shape[1:], bf16)],
        input_output_aliases={8: 4},
        scratch_shapes=[pltpu.VMEM((SSD_STATE, SSD_WIDTH), f32), pltpu.VMEM((T, SSD_WIDTH), f32)] + _sems(1)
        + _sems(N_CHIPS - 1),
        compiler_params=_cparams("arbitrary"),
    )(xc, dt_raw, dy, prev_all, dt_bias, a_log, d_skip_x, e_mat, g, pb2)


def _conv_bwd(dxc, xr, conv_w, conv_b, g):
    L = dxc.shape[0]
    tm = 256
    nt = L // tm
    RG = g.shape[0]

    def body(dxc_ref, xr_ref, xh_ref, cw_ref, cb_ref, g_in_ref, dxr_ref, acc_ref, g_ref, carry, buf, ext, send_sems, recv_sems):
        i = pl.program_id(0)
        start, wait = _share_job(g_ref, send_sems, recv_sems, RG)

        @pl.when(i == 0)
        def _():
            carry[...] = jnp.zeros_like(carry)
            acc_ref[...] = jnp.zeros_like(acc_ref)
            ext[tm + 16:tm + CHUNK, :] = jnp.zeros((CHUNK - 16, CONV_DIM), bf16)
            start()

        @pl.when(i == nt - 1)
        def _():
            wait()

        buf[0:8, :] = jnp.where(i == nt - 1, 0.0, xh_ref[...])
        u = xr_ref[...]
        buf[8:8 + tm, :] = u
        pre = cb_ref[...] + cw_ref[CONV_K - 1:CONV_K, :] * u
        for k in range(CONV_K - 1):
            pre = pre + cw_ref[k:k + 1, :] * buf[5 + k:5 + k + tm, :]
        sg = _sigmoid(pre)
        dpre = dxc_ref[...] * (sg * (1.0 + pre * (1.0 - sg)))
        acc_ref[4:5, :] += jnp.sum(dpre, axis=0, keepdims=True)
        dpb = dpre.astype(bf16)
        ext[0:tm, :] = dpb
        ext[tm:tm + 16, :] = carry[...]
        acc_ref[CONV_K - 1:CONV_K, :] += jnp.sum(u * dpre, axis=0, keepdims=True)
        du = cw_ref[CONV_K - 1:CONV_K, :] * dpre
        r = lax.broadcasted_iota(jnp.int32, (CHUNK, 2 * CHUNK), 0)
        c = lax.broadcasted_iota(jnp.int32, (CHUNK, 2 * CHUNK), 1)
        for j in range(1, CONV_K):
            move = (c == r + j).astype(bf16)
            up = jnp.concatenate([_dot(move, ext[CHUNK * b:CHUNK * (b + 2), :]) for b in range(tm // CHUNK)], axis=0)
            k = CONV_K - 1 - j
            du = du + cw_ref[k:k + 1, :] * up
            acc_ref[k:k + 1, :] += jnp.sum(u * up, axis=0, keepdims=True)
        dxr_ref[...] = du.astype(bf16)
        carry[...] = dpb[0:16, :]

    rev = lambda i: (nt - 1 - i, 0)
    return pl.pallas_call(
        body, name="conv_bwd", grid=(nt,),
        in_specs=[pl.BlockSpec((tm, CONV_DIM), rev), pl.BlockSpec((tm, CONV_DIM), rev),
                  pl.BlockSpec((8, CONV_DIM), lambda i: (jnp.maximum((nt - 1 - i) * (tm // 8) - 1, 0), 0)),
                  _full((CONV_K, CONV_DIM)), _full((1, CONV_DIM)), _ANY_SPEC],
        out_specs=[pl.BlockSpec((tm, CONV_DIM), rev), _full((8, CONV_DIM)), _ANY_SPEC],
        out_shape=[jax.ShapeDtypeStruct((L, CONV_DIM), bf16), jax.ShapeDtypeStruct((8, CONV_DIM), f32),
                   jax.ShapeDtypeStruct(g.shape, f32)],
        input_output_aliases={5: 2},
        scratch_shapes=[pltpu.VMEM((16, CONV_DIM), bf16), pltpu.VMEM((tm + 8, CONV_DIM), f32),
                        pltpu.VMEM((tm + CHUNK, CONV_DIM), bf16)] + _sems(1),
        compiler_params=_cparams("arbitrary"),
    )(dxc, xr, xr, conv_w, conv_b, g)


def _inproj_bwd(dz, dxr, dq, dkv, ddt, dr1, x, mod, ln_g, ln_b, w_in, pb):
    L = x.shape[0]
    tm = DENSE_TM
    nt = L // tm

    def body(dz_ref, dxr_ref, dq_ref, dkv_ref, ddt_ref, dr1_ref, x_ref, mod_ref, g_ref, b_ref, w_ref, pb_ref,
             dx_ref, acc_ref, chips_ref, send_sems, recv_sems):
        start, wait = _to_chips_job(pb_ref, chips_ref, send_sems, recv_sems)

        @pl.when(pl.program_id(0) == 0)
        def _():
            acc_ref[...] = jnp.zeros_like(acc_ref)
            start()

        du1 = (_dot(dz_ref[...], w_ref[W_Z, :]) + _dot(dxr_ref[...], w_ref[W_XBC, :])
               + _dot(dq_ref[...], w_ref[W_Q, :]) + _dot(dkv_ref[...], w_ref[W_KV, :])
               + _dot(ddt_ref[...].astype(bf16), w_ref[W_DT, :]))
        xhat, rstd = _ln_stats(x_ref[...])
        h0 = xhat * g_ref[...] + b_ref[...]
        acc_ref[0:1, :] += jnp.sum(du1 * h0, axis=0, keepdims=True)
        acc_ref[1:2, :] += jnp.sum(du1, axis=0, keepdims=True)
        dh0 = du1 * (1.0 + mod_ref[1:2, :]) + ALPHA * dr1_ref[...]
        acc_ref[2:3, :] += jnp.sum(dh0 * xhat, axis=0, keepdims=True)
        acc_ref[3:4, :] += jnp.sum(dh0, axis=0, keepdims=True)
        dx_ref[...] = _ln_bwd(dh0, xhat, rstd, g_ref[...])

        @pl.when(pl.program_id(0) == nt - 1)
        def _():
            wait()

    v = _full((1, D_MODEL))
    return pl.pallas_call(
        body, name="inproj_bwd", grid=(nt,),
        in_specs=[_rows(tm, D_MODEL), _rows(tm, CONV_DIM), _rows(tm, D_MODEL), _rows(tm, 256), _rows(tm, 128),
                  _rows(tm, D_MODEL), _rows(tm, D_MODEL), _full((8, D_MODEL)), v, v, _resident((PROJ_WIDTH, D_MODEL)),
                  _ANY_SPEC],
        out_specs=[_rows(tm, D_MODEL), _full((8, D_MODEL)), _ANY_SPEC],
        out_shape=[jax.ShapeDtypeStruct((L, D_MODEL), f32), jax.ShapeDtypeStruct((8, D_MODEL), f32),
                   jax.ShapeDtypeStruct((N_CHIPS - 1,) + pb.shape[1:], bf16)],
        scratch_shapes=_sems(N_CHIPS - 1),
        compiler_params=_cparams("arbitrary"),
    )(dz, dxr, dq, dkv, ddt, dr1, x, mod, ln_g, ln_b, w_in, pb)


def _adamw_math(w, g, m, v):
    m = ADAM_B1 * m + (1.0 - ADAM_B1) * g
    v = ADAM_B2 * v + (1.0 - ADAM_B2) * (g * g)
    m_hat = m / (1.0 - ADAM_B1 ** ADAM_STEP)
    v_hat = v / (1.0 - ADAM_B2 ** ADAM_STEP)
    delta = -ADAM_LR * (m_hat / (jnp.sqrt(v_hat) + ADAM_EPS) + ADAM_WD * w)
    return delta, m, v


def _adamw(w, g, m, v, name, g_row0=0):
    R, C = w.shape

    def body(w_ref, g_ref, m_ref, v_ref, g2_ref, d_ref, m2_ref, v2_ref):
        gv = g_ref[...]
        g2_ref[...] = gv
        d_ref[...], m2_ref[...], v2_ref[...] = _adamw_math(w_ref[...], gv, m_ref[...], v_ref[...])

    cap = max(8, ADAMW_BLOCK_ELEMS // C)
    tr = R if R <= cap else next(t for t in range(cap - cap % 8, 7, -8) if R % t == 0)
    spec = pl.BlockSpec((tr, C), lambda i: (i, 0))
    g_spec = pl.BlockSpec((tr, C), lambda i: (g_row0 // tr + i, 0))
    return pl.pallas_call(
        body, name=name, grid=(R // tr,), in_specs=[spec, g_spec, spec, spec], out_specs=[spec] * 4,
        out_shape=[jax.ShapeDtypeStruct((R, C), f32)] * 4, compiler_params=_cparams("parallel"),
    )(w, g, m, v)


def _adamw_rows(w, g, m, v):
    R, _, C = w.shape
    tr = R // 4

    def body(w_ref, g_ref, m_ref, v_ref, d_ref, m2_ref, v2_ref):
        d_ref[...], m2_ref[...], v2_ref[...] = _adamw_math(w_ref[...], g_ref[...], m_ref[...], v_ref[...])

    spec = pl.BlockSpec((tr, 1, C), lambda i: (i, 0, 0))
    return pl.pallas_call(
        body, name="adamw_w_in", grid=(R // tr,), in_specs=[spec] * 4, out_specs=[spec] * 3,
        out_shape=[jax.ShapeDtypeStruct((R, 1, C), f32)] * 3, compiler_params=_cparams("parallel"),
    )(w, g, m, v)


ADA_COLS = 6 * D_MODEL // N_CHIPS
ADA_TN = 512


COND_LANES = 512


def _prologue(cond, ada_w, ada_b, blob):
    R = blob.shape[0]

    def body(cond_ref, w_ref, b_ref, blob_ref, call_ref, mod_ref, wall_ref, mod_s, stage, gs, gr, ms, mr, ws, wr, local_sem):
        x, y, c = _place()
        start_w, finish_w = _gather_job(blob_ref, wall_ref, ws, wr, R)

        def rows(ref, px, py, pc):
            return ref.at[pl.ds(pl.multiple_of((4 * px + 2 * py + pc) * 8, 8), 8), :]

        mine = pltpu.make_async_copy(cond_ref, rows(call_ref, x, y, c), local_sem)
        mine.start()
        sends = [_remote(cond_ref, rows(call_ref, x, y, c), gs, gr, m - 1, _flip(x, y, c, m)) for m in range(1, N_DEV)]
        for cp in sends:
            cp.start()
        start_w()
        for m in range(1, N_DEV):
            peer = _flip(x, y, c, m)
            _remote(cond_ref, rows(call_ref, *peer), gs, gr, m - 1, peer).wait_recv()
        for cp in sends:
            cp.wait_send()
        mine.wait()

        for k in range(R // STAGE_ROWS):
            part = pl.ds(STAGE_ROWS * k, STAGE_ROWS)
            cin = pltpu.make_async_copy(blob_ref.at[part, :], stage, local_sem)
            cin.start()
            cin.wait()
            cout = pltpu.make_async_copy(stage, wall_ref.at[2 * x + y, part, :], local_sem)
            cout.start()
            cout.wait()

        lo = jnp.concatenate([call_ref[8 * d:8 * d + 1, :] for d in range(N_DEV)], axis=0)
        hi = jnp.concatenate([call_ref[8 * d + 1:8 * d + 2, :] for d in range(N_DEV)], axis=0)
        mod_all = (_dot_exact(lo * _sigmoid(lo), w_ref[0:COND_LANES, :]) + _dot_exact(hi * _sigmoid(hi), w_ref[COND_LANES:, :])
                   + b_ref[...])
        for d in range(N_DEV):
            mod_s[8 * d:8 * d + 8, :] = jnp.broadcast_to(mod_all[d:d + 1, :], (8, ADA_COLS))

        mine = pltpu.make_async_copy(rows(mod_s, x, y, c), mod_ref.at[2 * x + y], local_sem)
        mine.start()
        sends = []
        for m in range(1, N_CHIPS):
            peer = _flip(x, y, c, 2 * m)
            sends.append(_remote(rows(mod_s, *peer), mod_ref.at[2 * x + y], ms, mr, m - 1, peer))
        for cp in sends:
            cp.start()
        for m in range(1, N_CHIPS):
            px, py, pc = _flip(x, y, c, 2 * m)
            _remote(rows(mod_s, x, y, c), mod_ref.at[2 * px + py], ms, mr, m - 1, (px, py, pc)).wait_recv()
        for cp in sends:
            cp.wait_send()
        mine.wait()
        finish_w()

    return pl.pallas_call(
        body, name="prologue",
        out_shape=[jax.ShapeDtypeStruct((8 * N_DEV, COND_LANES), f32), jax.ShapeDtypeStruct((N_CHIPS, 8, ADA_COLS), f32),
                   jax.ShapeDtypeStruct((N_CHIPS, R, D_MODEL), bf16)],
        in_specs=[_VMEM_SPEC, _VMEM_SPEC, _VMEM_SPEC, _ANY_SPEC], out_specs=[_VMEM_SPEC, _VMEM_SPEC, _ANY_SPEC],
        scratch_shapes=[pltpu.VMEM((8 * N_DEV, ADA_COLS), f32), pltpu.VMEM((STAGE_ROWS, D_MODEL), bf16)]
        + _sems(N_DEV - 1) + _sems(N_CHIPS - 1) + _sems(6) + [pltpu.SemaphoreType.DMA],
        compiler_params=pltpu.CompilerParams(vmem_limit_bytes=VMEM_LIMIT),
    )(cond, ada_w, ada_b, blob)


def _ada_bwd(c_all, dmod, w, m, v):
    def body(c_ref, d_ref, w_ref, m_ref, v_ref, g_ref, dl_ref, m2_ref, v2_ref):
        cv = c_ref[...]
        g = lax.dot_general(cv * _sigmoid(cv), d_ref[...], (((0,), (0,)), ((), ())), preferred_element_type=f32,
                            precision=lax.Precision.HIGHEST)
        g_ref[...] = g
        dl_ref[...], m2_ref[...], v2_ref[...] = _adamw_math(w_ref[...], g, m_ref[...], v_ref[...])

    wspec = pl.BlockSpec((D_MODEL, ADA_TN), lambda j: (0, j))
    return pl.pallas_call(
        body, name="ada_bwd", grid=(ADA_COLS // ADA_TN,),
        in_specs=[_full((N_DEV, D_MODEL)), pl.BlockSpec((N_DEV, ADA_TN), lambda j: (0, j)), wspec, wspec, wspec],
        out_specs=[wspec] * 4, out_shape=[jax.ShapeDtypeStruct((D_MODEL, ADA_COLS), f32)] * 4,
        compiler_params=_cparams("parallel"),
    )(c_all, dmod, w, m, v)


SMALL_SLOTS = (("ada_b", 6144), ("ln_in_g", 1024), ("ln_in_b", 1024), ("conv_b", 1536), ("dt_bias", 128), ("a_log", 128),
               ("d_skip", 128), ("ssd_norm_w", 1024), ("attn_sinks", 128), ("ln1_g", 1024), ("ln1_b", 1024),
               ("b_ff1", 4096), ("b_ff2", 1024), ("ln2_g", 1024), ("ln2_b", 1024), ("conv_w", 6144), ("loss", 1024))
SMALL_N = sum(n for _, n in SMALL_SLOTS)
SMALL_OFF = {name: sum(n for _, n in SMALL_SLOTS[:i]) for i, (name, _) in enumerate(SMALL_SLOTS)}
SMALL_PARAMS = tuple(name for name, _ in SMALL_SLOTS[:15])
assert SMALL_N % 1024 == 0


def _small_pack(acc_in, acc_out, acc_mlp, db1, acc_conv, acc_ssd, dd_x, dsink, alog, e_mat):
    def body(in_ref, out_ref, mlp_ref, db1_ref, conv_ref, ssd_ref, dd_ref, sink_ref, al_ref, e_ref, o_ref):
        def put(name, val, at=0):
            off = SMALL_OFF[name] + at
            o_ref[:, off:off + val.shape[1]] = val

        for k, row in enumerate((in_ref[1:2, :], in_ref[0:1, :], out_ref[0:1, :], mlp_ref[A_SH2:A_SH2 + 1, :],
                                 mlp_ref[A_SC2:A_SC2 + 1, :], mlp_ref[A_G2:A_G2 + 1, :])):
            put("ada_b", row, D_MODEL * k)
        put("ln_in_g", in_ref[2:3, :])
        put("ln_in_b", in_ref[3:4, :])
        put("conv_b", conv_ref[4:5, :])
        put("dt_bias", ssd_ref[1:2, :])
        put("a_log", ssd_ref[0:1, :] * (-jnp.exp(al_ref[...])))
        put("d_skip", _dot_sel_nt(jnp.broadcast_to(dd_ref[...], (8, SSD_WIDTH)), e_ref[...])[0:1, :])
        put("ssd_norm_w", out_ref[1:2, :])
        put("attn_sinks", sink_ref[...])
        put("ln1_g", mlp_ref[A_LN1G:A_LN1G + 1, :])
        put("ln1_b", mlp_ref[A_LN1B:A_LN1B + 1, :])
        put("b_ff1", db1_ref[...])
        put("b_ff2", mlp_ref[A_B2:A_B2 + 1, :])
        put("ln2_g", mlp_ref[A_LN2G:A_LN2G + 1, :])
        put("ln2_b", mlp_ref[A_LN2B:A_LN2B + 1, :])
        for k in range(CONV_K):
            put("conv_w", conv_ref[k:k + 1, :], CONV_DIM * k)
        put("loss", mlp_ref[A_LOSS:A_LOSS + 1, :])

    return pl.pallas_call(body, name="small_pack", out_shape=jax.ShapeDtypeStruct((1, SMALL_N), f32),
                          compiler_params=_cparams())(acc_in, acc_out, acc_mlp, db1, acc_conv, acc_ssd, dd_x, dsink, alog, e_mat)


def _small_update(gathered, params, moms, vels):
    k = len(SMALL_PARAMS)

    def body(g_ref, *refs):
        w_refs, m_refs, v_refs, outs = refs[:k], refs[k:2 * k], refs[2 * k:3 * k], refs[3 * k:]

        def total(name, n):
            off = SMALL_OFF[name]
            g = g_ref[0:1, off:off + n]
            for i in range(1, N_DEV):
                g = g + g_ref[i:i + 1, off:off + n]
            return g

        for j, name in enumerate(SMALL_PARAMS):
            n = w_refs[j].shape[1]
            g = total(name, max(n, 128))[:, :n]
            outs[4 * j][...] = g
            outs[4 * j + 1][...], outs[4 * j + 2][...], outs[4 * j + 3][...] = _adamw_math(
                w_refs[j][...], g, m_refs[j][...], v_refs[j][...])
        outs[4 * k][...] = total("conv_w", CONV_K * CONV_DIM)
        outs[4 * k + 1][...] = total("loss", D_MODEL)

    shapes = [jax.ShapeDtypeStruct(p.shape, f32) for p in params for _ in range(4)]
    shapes += [jax.ShapeDtypeStruct((1, CONV_K * CONV_DIM), f32), jax.ShapeDtypeStruct((1, D_MODEL), f32)]
    return pl.pallas_call(body, name="small_update", out_shape=shapes,
                          compiler_params=_cparams())(gathered, *params, *moms, *vels)


def _place():
    return lax.axis_index("x"), lax.axis_index("y"), lax.axis_index("c")


def _flip(x, y, c, m):
    return (1 - x if m & 4 else x, 1 - y if m & 2 else y, 1 - c if m & 1 else c)


_VMEM_SPEC = pl.BlockSpec(memory_space=pltpu.VMEM)
_ANY_SPEC = pl.BlockSpec(memory_space=pl.ANY)


def _allgather8(v, name):
    n = v.shape[1]

    def body(v_ref, out_ref, send_sems, recv_sems, local_sem):
        x, y, c = _place()

        def rows(px, py, pc):
            return out_ref.at[pl.ds(pl.multiple_of((4 * px + 2 * py + pc) * 8, 8), 8), :]

        def copy(m, src, dst, to):
            return pltpu.make_async_remote_copy(src_ref=src, dst_ref=dst, send_sem=send_sems.at[m - 1],
                                                recv_sem=recv_sems.at[m - 1], device_id=to, device_id_type=MESH)

        mine = pltpu.make_async_copy(v_ref, rows(x, y, c), local_sem)
        mine.start()
        sends = [copy(m, v_ref, rows(x, y, c), _flip(x, y, c, m)) for m in range(1, N_DEV)]
        for cp in sends:
            cp.start()
        for m in range(1, N_DEV):
            peer = _flip(x, y, c, m)
            copy(m, v_ref, rows(*peer), peer).wait_recv()
        for cp in sends:
            cp.wait_send()
        mine.wait()

    return pl.pallas_call(
        body, name=name, out_shape=jax.ShapeDtypeStruct((8 * N_DEV, n), f32), in_specs=[_VMEM_SPEC],
        out_specs=_VMEM_SPEC,
        scratch_shapes=[pltpu.SemaphoreType.DMA((N_DEV - 1,)), pltpu.SemaphoreType.DMA((N_DEV - 1,)),
                        pltpu.SemaphoreType.DMA],
    )(v)


def _remote(src, dst, send_sems, recv_sems, k, to):
    return pltpu.make_async_remote_copy(src_ref=src, dst_ref=dst, send_sem=send_sems.at[k], recv_sem=recv_sems.at[k],
                                        device_id=to, device_id_type=MESH)


def _gather_job(blob_ref, out_ref, send_sems, recv_sems, R):
    x, y, c = _place()
    sib = (x, y, 1 - c)
    hr = R // 2

    def half(px, py, pc):
        return out_ref.at[2 * px + py, pl.ds(pl.multiple_of(pc * hr, 16), hr), :]

    my_half = blob_ref.at[pl.ds(pl.multiple_of(c * hr, 16), hr), :]

    def first():
        return [_remote(my_half, half(x, y, c), send_sems, recv_sems, m - 1, _flip(x, y, c, 2 * m))
                for m in range(1, N_CHIPS)]

    def start():
        for cp in first():
            cp.start()

    def finish():
        passed = []
        for m in range(1, N_CHIPS):
            px, py, pc = _flip(x, y, c, 2 * m)
            _remote(my_half, half(px, py, pc), send_sems, recv_sems, m - 1, (px, py, pc)).wait_recv()
            fwd = _remote(half(px, py, pc), half(px, py, pc), send_sems, recv_sems, 2 + m, sib)
            fwd.start()
            passed.append(fwd)
        for m in range(1, N_CHIPS):
            px, py, pc = _flip(x, y, c, 2 * m)
            _remote(my_half, half(px, py, 1 - pc), send_sems, recv_sems, 2 + m, sib).wait_recv()
        for cp in first() + passed:
            cp.wait_send()

    return start, finish


def _to_sibling_job(g_ref, out_ref, send_sems, recv_sems):
    x, y, c = _place()

    def cps():
        return [_remote(g_ref.at[j, 1 - c], out_ref.at[j], send_sems, recv_sems, j, (x, y, 1 - c)) for j in range(N_CHIPS)]

    def start():
        for cp in cps():
            cp.start()

    def wait():
        for cp in cps():
            cp.wait()

    return start, wait


def _to_chips_job(p_ref, out_ref, send_sems, recv_sems):
    x, y, c = _place()

    def cps():
        out = []
        for m in range(1, N_CHIPS):
            px, py, pc = _flip(x, y, c, 2 * m)
            out.append(_remote(p_ref.at[2 * px + py], out_ref.at[m - 1], send_sems, recv_sems, m - 1, (px, py, pc)))
        return out

    def start():
        for cp in cps():
            cp.start()

    def wait():
        for cp in cps():
            cp.wait()

    return start, wait


def _share_job(g_ref, send_sems, recv_sems, R):
    x, y, c = _place()

    def rows(pc):
        return g_ref.at[pl.ds(pl.multiple_of(pc * (R // 2), 8), R // 2), :]

    def start():
        _remote(rows(c), rows(c), send_sems, recv_sems, 0, (x, y, 1 - c)).start()

    def wait():
        _remote(rows(c), rows(1 - c), send_sems, recv_sems, 0, (x, y, 1 - c)).wait_recv()
        _remote(rows(c), rows(c), send_sems, recv_sems, 0, (x, y, 1 - c)).wait_send()

    return start, wait


def _sems(n):
    return [pltpu.SemaphoreType.DMA((n,)), pltpu.SemaphoreType.DMA((n,))]


def _rs_to_sibling(gb):
    def body(g_ref, out_ref, send_sems, recv_sems):
        start, wait = _to_sibling_job(g_ref, out_ref, send_sems, recv_sems)
        start()
        wait()

    return pl.pallas_call(
        body, name="rs_to_sibling", out_shape=jax.ShapeDtypeStruct((N_CHIPS,) + gb.shape[2:], bf16),
        in_specs=[_ANY_SPEC], out_specs=_ANY_SPEC, scratch_shapes=_sems(N_CHIPS),
    )(gb)


def _rs_share(g):
    R = g.shape[0]

    def body(g_ref, out_ref, send_sems, recv_sems):
        start, wait = _share_job(out_ref, send_sems, recv_sems, R)
        start()
        wait()

    return pl.pallas_call(
        body, name="rs_share", out_shape=jax.ShapeDtypeStruct(g.shape, f32), in_specs=[_ANY_SPEC],
        out_specs=_ANY_SPEC, input_output_aliases={0: 0}, scratch_shapes=_sems(1),
    )(g)


RS_TR_MAX = 512


def _rs_sum_pair(place, gb, recv, name):
    hr = gb.shape[2]
    tr = min(hr, RS_TR_MAX)

    def body(pl_ref, g_ref, r_ref, o_ref):
        o_ref[0] = (g_ref[0, 0].astype(f32) + r_ref[0].astype(f32)).astype(bf16)

    return pl.pallas_call(
        body, name=name,
        grid_spec=pltpu.PrefetchScalarGridSpec(
            num_scalar_prefetch=1, grid=(N_CHIPS, hr // tr),
            in_specs=[pl.BlockSpec((1, 1, tr, D_MODEL), lambda j, i, p: (j, p[0], i, 0)),
                      pl.BlockSpec((1, tr, D_MODEL), lambda j, i, p: (j, i, 0))],
            out_specs=pl.BlockSpec((1, tr, D_MODEL), lambda j, i, p: (j, i, 0))),
        out_shape=jax.ShapeDtypeStruct((N_CHIPS, hr, D_MODEL), bf16),
        compiler_params=_cparams("parallel", "parallel"),
    )(place, gb, recv)


def _rs_sum_chips(place, gb, recv_sib, recv_chips, name):
    hr = gb.shape[2]
    tr = min(hr, RS_TR_MAX)
    nt = hr // tr

    def body(pl_ref, g_ref, r1_ref, r2_ref, o_ref):
        acc = g_ref[0, 0].astype(f32) + r1_ref[0].astype(f32)
        for k in range(N_CHIPS - 1):
            acc = acc + r2_ref[k].astype(f32)
        o_ref[...] = acc

    return pl.pallas_call(
        body, name=name,
        grid_spec=pltpu.PrefetchScalarGridSpec(
            num_scalar_prefetch=1, grid=(nt,),
            in_specs=[pl.BlockSpec((1, 1, tr, D_MODEL), lambda i, p: (p[1], p[0], i, 0)),
                      pl.BlockSpec((1, tr, D_MODEL), lambda i, p: (p[1], i, 0)),
                      pl.BlockSpec((N_CHIPS - 1, tr, D_MODEL), lambda i, p: (0, i, 0))],
            out_specs=pl.BlockSpec((tr, D_MODEL), lambda i, p: (p[0] * nt + i, 0))),
        out_shape=jax.ShapeDtypeStruct((2 * hr, D_MODEL), f32),
        compiler_params=_cparams("parallel"),
    )(place, gb, recv_sib, recv_chips)


def _pad128(v):
    v = v.reshape(1, -1)
    return jnp.pad(v, ((0, 0), (0, 128 - v.shape[1])))


def _row(v):
    return v.reshape(1, -1)


W_COLS = PROJ_WIDTH // N_CHIPS


def _g_in_blocks(gz, gxbc, gdt, gq, gkv):
    g = jnp.concatenate([gz, gxbc, gdt[:W_DT_ROWS], gq, gkv], axis=0)
    return jnp.pad(g.reshape(N_CHIPS, W_COLS, D_MODEL), ((0, 0), (0, D_MODEL - W_COLS), (0, 0)))


def kernel(x, c, ln_in_g, ln_in_b, ada_w, ada_b, w_in, conv_w, conv_b, dt_bias, a_log, d_skip, ssd_norm_w, attn_sinks, w_out, ln1_g, ln1_b, w_ff1, b_ff1, w_ff2, b_ff2, ln2_g, ln2_b, loss_target, m_ln_in_g, m_ln_in_b, m_ada_w, m_ada_b, m_w_in, m_conv_w, m_conv_b, m_dt_bias, m_a_log, m_d_skip, m_ssd_norm_w, m_attn_sinks, m_w_out, m_ln1_g, m_ln1_b, m_w_ff1, m_b_ff1, m_w_ff2, m_b_ff2, m_ln2_g, m_ln2_b, v_ln_in_g, v_ln_in_b, v_ada_w, v_ada_b, v_w_in, v_conv_w, v_conv_b, v_dt_bias, v_a_log, v_d_skip, v_ssd_norm_w, v_attn_sinks, v_w_out, v_ln1_g, v_ln1_b, v_w_ff1, v_b_ff1, v_w_ff2, v_b_ff2, v_ln2_g, v_ln2_b):
    xi, yi, ci = _place()
    chip = 2 * xi + yi
    place = jnp.stack([ci, chip]).astype(jnp.int32)
    x2, tgt = x[0], loss_target[0]

    def as_rows(a):
        return jnp.transpose(a, (2, 0, 1))

    def from_rows(a):
        return jnp.transpose(a, (1, 2, 0))

    cond = jnp.concatenate([c.reshape(2, COND_LANES), conv_w.reshape(3, COND_LANES), jnp.zeros((3, COND_LANES), f32)], axis=0)
    ada_b_mine = lax.dynamic_slice(ada_b, (0, chip * ADA_COLS), (1, ADA_COLS))
    blob_in = jnp.pad(w_in[0].T, ((0, D_MODEL - W_COLS), (0, 0))).astype(bf16)
    cond_all, mod_rows, wall_in = _prologue(cond, ada_w[0], ada_b_mine, blob_in)
    cond_all = cond_all.reshape(N_DEV, 8, COND_LANES)
    c_all = cond_all[:, 0:2].reshape(N_DEV, D_MODEL)
    conv_w_full = jnp.concatenate([cond_all[2 * j, 2:5].reshape(CONV_K, 384) for j in range(N_CHIPS)], axis=1)
    mod = jnp.concatenate([mod_rows[:, 0].reshape(6, D_MODEL), jnp.zeros((2, D_MODEL), f32)], axis=0)
    w_in_f = wall_in[:, :W_COLS].reshape(PROJ_WIDTH, D_MODEL)
    b_ffw = jnp.concatenate([w_ff1[0], w_ff2[0]], axis=0).astype(bf16)
    b_outw = w_out[0].astype(bf16)

    def with_mine(wall, mine):
        return lax.dynamic_update_slice(wall, mine[None], (chip, 0, 0))

    e_mat = _head_expand()
    dsk_x = jnp.repeat(d_skip[0], HEAD_DIM).reshape(1, SSD_WIDTH)
    dtb, alog = _pad128(dt_bias), _pad128(a_log)
    sinks = attn_sinks[0]
    lng, lnb = _row(ln_in_g), _row(ln_in_b)
    u1, z, xr, xc, q, kv, dtr = _inproj_fwd(x2, mod, lng, lnb, w_in_f, conv_w_full, conv_b)
    y, prev_all, wall_out = _ssd_fwd(xc, dtr, dtb, alog, dsk_x, b_outw)
    o, lse, wall_ff = _attn_fwd(q, kv, sinks, b_ffw)
    wall_ff, wall_out = with_mine(wall_ff, b_ffw), with_mine(wall_out, b_outw)
    yn, mix, r1 = _outproj_fwd(y, z, o, x2, mod, lng, lnb, ssd_norm_w, wall_out)

    dr1, u2, s_act, da, df, acc_mlp, db1 = _mlp_fwd_bwd(r1, tgt, mod, ln1_g, ln1_b, ln2_g, ln2_b, wall_ff, b_ff1, wall_ff,
                                                        b_ff2)
    ga = lax.empty((N_CHIPS, GA_ROWS, D_MODEL), bf16)
    ga = _wgrad_blob(ga, u2, da, "wgrad_ff1", lambda t, n: (n, t))
    ga = _wgrad_blob(ga, s_act, df, "wgrad_ff2", lambda t, n: (t // 2, 2 + t % 2))
    ga = ga.reshape(N_CHIPS, 2, GA_ROWS // 2, D_MODEL)
    dy, dz, do, dmix, acc_out, a_sib = _outproj_bwd(dr1, mix, y, z, mod, ssd_norm_w, wall_out, ga)
    gc = lax.empty((N_CHIPS, GC_ROWS, D_MODEL), bf16)
    gc = _wgrad_blob(gc, yn, dmix, "wgrad_out_y", lambda t, n: (t, 0))
    gc = _wgrad_blob(gc, o, dmix, "wgrad_out_o", lambda t, n: (2 + t, 0))
    gc = gc.reshape(N_CHIPS, 2, GC_ROWS // 2, D_MODEL)
    a_pair = _rs_sum_pair(place, ga, a_sib, "rs_sum_pair_a")
    dq, dkv, dsink, a_chips, c_sib = _attn_bwd(q, kv, do, lse, sinks, a_pair, gc)
    g_a = _rs_sum_chips(place, ga, a_sib, a_chips, "rs_sum_chips_a")
    c_pair = _rs_sum_pair(place, gc, c_sib, "rs_sum_pair_c")
    dxc, ddt, acc_ssd, dd_x, g_a, c_chips = _ssd_bwd(xc, dtr, dy, prev_all, dtb, alog, dsk_x, e_mat, g_a, c_pair)
    g_c = _rs_sum_chips(place, gc, c_sib, c_chips, "rs_sum_chips_c")
    dxr, acc_conv, g_c = _conv_bwd(dxc, xr, conv_w_full, conv_b, g_c)
    gb = _g_in_blocks(_wgrad(dz, u1, "wgrad_in_z"), _wgrad(dxr, u1, "wgrad_in_xbc"),
                      _wgrad(ddt.astype(bf16), u1, "wgrad_in_dt"), _wgrad(dq, u1, "wgrad_in_q"),
                      _wgrad(dkv, u1, "wgrad_in_kv")).reshape(N_CHIPS, 2, GB_ROWS // 2, D_MODEL)
    b_sib = _rs_to_sibling(gb)
    b_pair = _rs_sum_pair(place, gb, b_sib, "rs_sum_pair_b")
    grad_x, acc_in, b_chips = _inproj_bwd(dz, dxr, dq, dkv, ddt, dr1, x2, mod, lng, lnb, w_in_f, b_pair)
    g_b = _rs_share(_rs_sum_chips(place, gb, b_sib, b_chips, "rs_sum_chips_b"))

    packed = _small_pack(acc_in, acc_out, acc_mlp, db1, acc_conv, acc_ssd, dd_x, dsink, alog, e_mat)
    small_all = _allgather8(packed.reshape(8, SMALL_N // 8), "gather_small").reshape(N_DEV, SMALL_N)
    given = dict(ada_b=(ada_b, m_ada_b, v_ada_b), ln_in_g=(ln_in_g, m_ln_in_g, v_ln_in_g), ln_in_b=(ln_in_b, m_ln_in_b, v_ln_in_b),
                 conv_b=(conv_b, m_conv_b, v_conv_b), dt_bias=(dt_bias, m_dt_bias, v_dt_bias), a_log=(a_log, m_a_log, v_a_log),
                 d_skip=(d_skip, m_d_skip, v_d_skip), ssd_norm_w=(ssd_norm_w, m_ssd_norm_w, v_ssd_norm_w),
                 attn_sinks=(attn_sinks, m_attn_sinks, v_attn_sinks), ln1_g=(ln1_g, m_ln1_g, v_ln1_g),
                 ln1_b=(ln1_b, m_ln1_b, v_ln1_b), b_ff1=(b_ff1, m_b_ff1, v_b_ff1), b_ff2=(b_ff2, m_b_ff2, v_b_ff2),
                 ln2_g=(ln2_g, m_ln2_g, v_ln2_g), ln2_b=(ln2_b, m_ln2_b, v_ln2_b))
    upd = _small_update(small_all, *([_row(given[n][i]) for n in SMALL_PARAMS] for i in range(3)))
    small_res = {n: [t.reshape(given[n][0].shape) for t in upd[4 * j:4 * j + 4]] for j, n in enumerate(SMALL_PARAMS)}
    g_conv_all, loss_lanes = upd[4 * len(SMALL_PARAMS)], upd[4 * len(SMALL_PARAMS) + 1]
    loss = jnp.sum(loss_lanes)

    dmod_mine = lax.dynamic_slice(small_all[:, :6 * D_MODEL], (0, chip * ADA_COLS), (N_DEV, ADA_COLS))
    big = {"ada_w": [t[None] for t in _ada_bwd(c_all, dmod_mine, ada_w[0], m_ada_w[0], v_ada_w[0])]}

    g_conv = lax.dynamic_slice(g_conv_all.reshape(CONV_K, CONV_DIM), (0, chip * 384), (CONV_K, 384))
    big["conv_w"] = [t[None] for t in _adamw(conv_w[0], g_conv, m_conv_w[0], v_conv_w[0], "adamw_conv_w")]

    g_rows = g_b[:W_COLS].reshape(W_COLS, 1, D_MODEL)
    big["w_in"] = [from_rows(t) for t in (g_rows, *_adamw_rows(as_rows(w_in), g_rows, as_rows(m_w_in), as_rows(v_w_in)))]
    for name, g, row0, (w, m, v) in (("w_out", g_c, 0, (w_out, m_w_out, v_w_out)), ("w_ff1", g_a, 0, (w_ff1, m_w_ff1, v_w_ff1)),
                                     ("w_ff2", g_a, D_MODEL, (w_ff2, m_w_ff2, v_w_ff2))):
        big[name] = [t[None] for t in _adamw(w[0], g, m[0], v[0], "adamw_" + name, row0)]

    order = ("ln_in_g", "ln_in_b", "ada_w", "ada_b", "w_in", "conv_w", "conv_b", "dt_bias", "a_log", "d_skip", "ssd_norm_w",
             "attn_sinks", "w_out", "ln1_g", "ln1_b", "w_ff1", "b_ff1", "w_ff2", "b_ff2", "ln2_g", "ln2_b")
    res = {**small_res, **big}
    return (loss, grad_x[None], *[res[n][k] for k in range(4) for n in order])
```
